```python
import math
import jax
import jax.numpy as jnp
from jax import lax
import numpy as np

D_MODEL = 1024
BATCH = 16
SEQ = 2048
DEPTH = 1

N_HEADS_A = 8
HEAD_DIM = 64
Q_RANK = 256
KV_RANK = 128
N_IDX_HEADS = 8
IDX_DIM = 64
TOPK_MAX = 256
Q_BLOCK = 128
REL_BUCKETS = 32
REL_MAX_DIST = 128
CONV_CH = 256
CONV_WIDTH = 3
N_MEM = 256
N_MEM_HEADS = 4
MIX_A = N_HEADS_A * HEAD_DIM
MIX_C = N_MEM_HEADS * HEAD_DIM
MIX_WIDTH = MIX_A + CONV_CH + MIX_C
IN_SIZES = (Q_RANK, KV_RANK, IDX_DIM, N_IDX_HEADS, CONV_CH, CONV_CH, CONV_CH, MIX_C)
IN_COLS = Q_RANK + KV_RANK + IDX_DIM + N_IDX_HEADS + 3 * CONV_CH + MIX_C
N_EXPERTS = 64
N_GROUPS = 8
TOPK_GROUPS = 4
TOP_K = 8
D_EXPERT = 256
ROUTED_SCALE = 2.5
MOE_BLOCK = 256
ALPHA = (2.0 * DEPTH) ** 0.25
BETA = (8.0 * DEPTH) ** -0.25
LN_EPS = 1e-5
RMS_EPS = 1e-6

kernel_name = 'hybrid_dsa_shortconv_memxattn_moe_deepnorm'


def layer_norm(x, g, b):
    xf = x.astype(jnp.float32)
    mu = jnp.mean(xf, axis=-1, keepdims=True)
    var = jnp.mean(jnp.square(xf - mu), axis=-1, keepdims=True)
    return ((xf - mu) * lax.rsqrt(var + LN_EPS) * g.astype(jnp.float32) + b.astype(jnp.float32)).astype(x.dtype)


def rms_norm(x, g):
    xf = x.astype(jnp.float32)
    return (xf * lax.rsqrt(jnp.mean(jnp.square(xf), axis=-1, keepdims=True) + RMS_EPS) * g.astype(jnp.float32)).astype(x.dtype)


def t5_bucket(rel):
    n = jnp.maximum(rel, 0)
    max_exact = REL_BUCKETS // 2
    nf = jnp.maximum(n.astype(jnp.float32), 1.0)
    large = max_exact + (jnp.log(nf / max_exact) / math.log(REL_MAX_DIST / max_exact) * (REL_BUCKETS - max_exact)).astype(jnp.int32)
    large = jnp.minimum(large, REL_BUCKETS - 1)
    return jnp.where(n < max_exact, n, large)


def swiglu(h, wg, wu, wd):
    return (jax.nn.silu(h @ wg) * (h @ wu)) @ wd


def dsa_attention(c_q, c_kv, k_idx, idx_w, w_uq, w_uk, w_uv, w_qidx, rel_bias):
    B, S, _ = c_q.shape
    k_sel = min(TOPK_MAX, S // 4)
    nb = S // Q_BLOCK
    q = jnp.einsum('bsr,rhd->bshd', c_q, w_uq)
    q_lat = jnp.einsum('bshd,chd->bshc', q, w_uk) * (HEAD_DIM ** -0.5)
    q_idx = jnp.einsum('bsr,rhd->bshd', c_q, w_qidx)
    idx_w = idx_w * (N_IDX_HEADS ** -0.5 * IDX_DIM ** -0.5)
    kpos = jnp.arange(S)

    def blocks(a):
        return a.reshape((B, nb, Q_BLOCK) + a.shape[2:]).swapaxes(0, 1)

    def attend_block(args):
        ql, qi, wi, start = args
        qpos = start + jnp.arange(Q_BLOCK)
        dots = jnp.einsum('bqhd,bsd->bqhs', qi, k_idx).astype(jnp.float32)
        score = jnp.einsum('bqhs,bqh->bqs', jax.nn.relu(dots), wi.astype(jnp.float32))
        score = jnp.where(kpos[None, None, :] <= qpos[None, :, None], score, -jnp.inf)
        _, sel = lax.top_k(score, k_sel)
        kv_sel = jax.vmap(lambda c, i: c[i])(c_kv, sel)
        rel = qpos[None, :, None] - sel
        bias = jnp.moveaxis(rel_bias[t5_bucket(rel)], -1, -2)
        logits = jnp.einsum('bqhc,bqkc->bqhk', ql, kv_sel).astype(jnp.float32) + bias.astype(jnp.float32)
        logits = jnp.where((rel >= 0)[:, :, None, :], logits, -jnp.inf)
        p = jax.nn.softmax(logits, axis=-1).astype(kv_sel.dtype)
        return jnp.einsum('bqhk,bqkc->bqhc', p, kv_sel)

    starts = jnp.arange(nb, dtype=jnp.int32) * Q_BLOCK
    o_lat = lax.map(attend_block, (blocks(q_lat), blocks(q_idx), blocks(idx_w), starts))
    o_lat = o_lat.swapaxes(0, 1).reshape(B, S, N_HEADS_A, KV_RANK)
    o = jnp.einsum('bshc,chd->bshd', o_lat, w_uv)
    return o.reshape(B, S, MIX_A)


def short_conv(b_gate, c_gate, h, conv_w):
    S = h.shape[1]
    u = c_gate * h
    u_pad = jnp.pad(u, ((0, 0), (CONV_WIDTH - 1, 0), (0, 0)))
    y = conv_w[0] * u_pad[:, 0:S]
    for j in range(1, CONV_WIDTH):
        y = y + conv_w[j] * u_pad[:, j:j + S]
    return b_gate * y


def memory_attention(q_in, mem, w_mem_k, w_mem_v):
    B, S, _ = q_in.shape
    q = q_in.reshape(B, S, N_MEM_HEADS, HEAD_DIM)
    k = (mem @ w_mem_k).reshape(B, -1, N_MEM_HEADS, HEAD_DIM)
    v = (mem @ w_mem_v).reshape(B, -1, N_MEM_HEADS, HEAD_DIM)
    logits = jnp.einsum('bshd,bmhd->bhsm', q, k).astype(jnp.float32) * (HEAD_DIM ** -0.5)
    p = jax.nn.softmax(logits, axis=-1).astype(v.dtype)
    return jnp.einsum('bhsm,bmhd->bshd', p, v).reshape(B, S, MIX_C)


def route(xt, w_router, router_bias):
    T = xt.shape[0]
    s = jax.nn.sigmoid(xt.astype(jnp.float32) @ w_router.astype(jnp.float32))
    s_choice = s + router_bias.astype(jnp.float32)
    grp = s_choice.reshape(T, N_GROUPS, N_EXPERTS // N_GROUPS)
    grp_score = jnp.sum(lax.top_k(grp, 2)[0], axis=-1)
    _, gidx = lax.top_k(grp_score, TOPK_GROUPS)
    gmask = jnp.any(gidx[..., None] == jnp.arange(N_GROUPS), axis=-2)
    emask = jnp.repeat(gmask, N_EXPERTS // N_GROUPS, axis=-1)
    _, top_idx = lax.top_k(jnp.where(emask, s_choice, -jnp.inf), TOP_K)
    top_s = jnp.take_along_axis(s, top_idx, axis=-1)
    top_w = top_s / jnp.sum(top_s, axis=-1, keepdims=True) * ROUTED_SCALE
    return top_idx, top_w


def routed_experts(xt, top_idx, top_w, w_gate, w_up, w_down):
    T, D = xt.shape
    n_slots = T * TOP_K
    flat_e = top_idx.reshape(-1)
    order = jnp.argsort(flat_e)
    sorted_e = flat_e[order]
    counts = jnp.bincount(flat_e, length=N_EXPERTS)
    padded = (counts + MOE_BLOCK - 1) // MOE_BLOCK * MOE_BLOCK
    pad_end = jnp.cumsum(padded)
    pad_start = pad_end - padded
    grp_start = jnp.cumsum(counts) - counts
    dest = pad_start[sorted_e] + (jnp.arange(n_slots) - grp_start[sorted_e])
    n_blocks = -(-n_slots // MOE_BLOCK) + N_EXPERTS
    n_rows = n_blocks * MOE_BLOCK
    row_tok = jnp.full((n_rows,), T, jnp.int32).at[dest].set((order // TOP_K).astype(jnp.int32))
    row_w = jnp.zeros((n_rows,), xt.dtype).at[dest].set(top_w.reshape(-1)[order])
    block_e = jnp.minimum(jnp.searchsorted(pad_end, jnp.arange(n_blocks) * MOE_BLOCK, side='right'), N_EXPERTS - 1)
    xt_pad = jnp.concatenate([xt, jnp.zeros((1, D), xt.dtype)], axis=0)

    def expert_block(acc, blk):
        tok, wt, e = blk
        y = swiglu(xt_pad[tok], w_gate[e], w_up[e], w_down[e]) * wt[:, None]
        return acc.at[tok].add(y), None

    acc, _ = lax.scan(expert_block, jnp.zeros((T + 1, D), xt.dtype),
                      (row_tok.reshape(n_blocks, MOE_BLOCK), row_w.reshape(n_blocks, MOE_BLOCK), block_e))
    return acc[:T]


def moe_ffn(x, w_router, router_bias, w_e_gate, w_e_up, w_e_down, w_s_gate, w_s_up, w_s_down):
    B, S, D = x.shape
    xt = x.reshape(B * S, D)
    top_idx, top_w = route(xt, w_router, router_bias)
    routed = routed_experts(xt, top_idx, top_w.astype(xt.dtype), w_e_gate, w_e_up, w_e_down)
    shared = swiglu(xt, w_s_gate, w_s_up, w_s_down)
    return (routed + shared).reshape(B, S, D)


def hybrid_layer(x, mem, w_in, q_norm_g, kv_norm_g, w_uq, w_uk, w_uv, w_qidx, rel_bias, conv_w,
                 w_mem_k, w_mem_v, w_out, ln1_g, ln1_b, w_router, router_bias, w_e_gate, w_e_up,
                 w_e_down, w_s_gate, w_s_up, w_s_down, ln2_g, ln2_b):
    proj = x @ w_in
    c_q, c_kv, k_idx, idx_w, g_b, g_c, h_conv, q_mem = jnp.split(proj, np.cumsum(IN_SIZES)[:-1], axis=-1)
    c_q = rms_norm(c_q, q_norm_g)
    c_kv = rms_norm(c_kv, kv_norm_g)
    y_a = dsa_attention(c_q, c_kv, k_idx, idx_w, w_uq, w_uk, w_uv, w_qidx, rel_bias)
    y_b = short_conv(g_b, g_c, h_conv, conv_w)
    y_c = memory_attention(q_mem, mem, w_mem_k, w_mem_v)
    mix = jnp.concatenate([y_a, y_b, y_c], axis=-1) @ w_out
    x = layer_norm(ALPHA * x + mix, ln1_g, ln1_b)
    ffn = moe_ffn(x, w_router, router_bias, w_e_gate, w_e_up, w_e_down, w_s_gate, w_s_up, w_s_down)
    return layer_norm(ALPHA * x + ffn, ln2_g, ln2_b)


def setup_inputs(seed: int = 0) -> dict:
    key = jax.random.key(seed)
    ks = iter(jax.random.split(key, 32))
    L = DEPTH

    def nrm(shape, fan_in, scale=1.0):
        return jax.random.normal(next(ks), shape, jnp.float32) * (scale * fan_in ** -0.5)

    def gain(shape):
        return 1.0 + 0.05 * jax.random.normal(next(ks), shape, jnp.float32)

    def small(shape, s):
        return s * jax.random.normal(next(ks), shape, jnp.float32)

    return {
        'x': jax.random.normal(next(ks), (BATCH, SEQ, D_MODEL), jnp.float32),
        'mem': jax.random.normal(next(ks), (BATCH, N_MEM, D_MODEL), jnp.float32),
        'w_in': nrm((L, D_MODEL, IN_COLS), D_MODEL),
        'q_norm_g': gain((L, Q_RANK)),
        'kv_norm_g': gain((L, KV_RANK)),
        'w_uq': nrm((L, Q_RANK, N_HEADS_A, HEAD_DIM), Q_RANK),
        'w_uk': nrm((L, KV_RANK, N_HEADS_A, HEAD_DIM), KV_RANK),
        'w_uv': nrm((L, KV_RANK, N_HEADS_A, HEAD_DIM), KV_RANK, BETA),
        'w_qidx': nrm((L, Q_RANK, N_IDX_HEADS, IDX_DIM), Q_RANK),
        'rel_bias': small((REL_BUCKETS, N_HEADS_A), 0.5),
        'conv_w': nrm((L, CONV_WIDTH, CONV_CH), CONV_WIDTH),
        'w_mem_k': nrm((L, D_MODEL, MIX_C), D_MODEL),
        'w_mem_v': nrm((L, D_MODEL, MIX_C), D_MODEL, BETA),
        'w_out': nrm((L, MIX_WIDTH, D_MODEL), MIX_WIDTH, BETA),
        'ln1_g': gain((L, D_MODEL)),
        'ln1_b': small((L, D_MODEL), 0.02),
        'w_router': nrm((L, D_MODEL, N_EXPERTS), D_MODEL),
        'router_bias': small((L, N_EXPERTS), 0.01),
        'w_e_gate': nrm((L, N_EXPERTS, D_MODEL, D_EXPERT), D_MODEL),
        'w_e_up': nrm((L, N_EXPERTS, D_MODEL, D_EXPERT), D_MODEL),
        'w_e_down': nrm((L, N_EXPERTS, D_EXPERT, D_MODEL), D_EXPERT, BETA),
        'w_s_gate': nrm((L, D_MODEL, D_EXPERT), D_MODEL),
        'w_s_up': nrm((L, D_MODEL, D_EXPERT), D_MODEL),
        'w_s_down': nrm((L, D_EXPERT, D_MODEL), D_EXPERT, BETA),
        'ln2_g': gain((L, D_MODEL)),
        'ln2_b': small((L, D_MODEL), 0.02),
    }


def reference(x, mem, w_in, q_norm_g, kv_norm_g, w_uq, w_uk, w_uv, w_qidx, rel_bias, conv_w,
              w_mem_k, w_mem_v, w_out, ln1_g, ln1_b, w_router, router_bias, w_e_gate, w_e_up,
              w_e_down, w_s_gate, w_s_up, w_s_down, ln2_g, ln2_b):
    for l in range(DEPTH):
        x = hybrid_layer(x, mem, w_in[l], q_norm_g[l], kv_norm_g[l], w_uq[l], w_uk[l], w_uv[l], w_qidx[l],
                         rel_bias, conv_w[l], w_mem_k[l], w_mem_v[l], w_out[l], ln1_g[l], ln1_b[l],
                         w_router[l], router_bias[l], w_e_gate[l], w_e_up[l], w_e_down[l],
                         w_s_gate[l], w_s_up[l], w_s_down[l], ln2_g[l], ln2_b[l])
    return x
```

```python
import functools
import math

import jax
import jax.numpy as jnp
from jax import lax
from jax.experimental import pallas as pl
from jax.experimental.pallas import tpu as pltpu

N_HEADS_A = 8
HEAD_DIM = 64
Q_RANK = 256
KV_RANK = 128
N_IDX_HEADS = 8
IDX_DIM = 64
TOPK_MAX = 256
REL_BUCKETS = 32
REL_MAX_DIST = 128
CONV_CH = 256
CONV_WIDTH = 3
N_MEM_HEADS = 4
MIX_A = N_HEADS_A * HEAD_DIM
MIX_C = N_MEM_HEADS * HEAD_DIM
N_EXPERTS = 64
N_GROUPS = 8
GROUP_SIZE = N_EXPERTS // N_GROUPS
TOPK_GROUPS = 4
TOP_K = 8
D_EXPERT = 256
ROUTED_SCALE = 2.5
MOE_BLOCK = 256
DEPTH = 1
ALPHA = (2.0 * DEPTH) ** 0.25
LN_EPS = 1e-5
RMS_EPS = 1e-6

LANES = 128
SUBLANES = 8
QB = 128
F32_LOWEST = -3.4028234663852886e38
VMEM_LIMIT = 56 * 1024 * 1024
MXU_DTYPE = jnp.bfloat16

_NT = (((1,), (1,)), ((), ()))


def _dot(a, b):
    return jnp.dot(a, b, preferred_element_type=jnp.float32)


def _dot_nt(a, b):
    return lax.dot_general(a, b, _NT, preferred_element_type=jnp.float32)


def _cparams(sem):
    return pltpu.CompilerParams(dimension_semantics=sem, vmem_limit_bytes=VMEM_LIMIT)


def _bias_kernel(rb_ref, o_ref):
    s = lax.broadcasted_iota(jnp.int32, (QB, QB), 0)
    t = lax.broadcasted_iota(jnp.int32, (QB, QB), 1)
    max_exact = REL_BUCKETS // 2
    for tile in range(2):
        n = jnp.maximum(t - s + tile * QB, 0)
        nf = jnp.maximum(n.astype(jnp.float32), 1.0)
        large = max_exact + (jnp.log(nf / max_exact) / math.log(REL_MAX_DIST / max_exact)
                             * (REL_BUCKETS - max_exact)).astype(jnp.int32)
        large = jnp.minimum(large, REL_BUCKETS - 1)
        bucket = jnp.where(n < max_exact, n, large)
        for h in range(N_HEADS_A):
            acc = jnp.zeros((QB, QB), jnp.float32)
            for b in range(REL_BUCKETS):
                acc = jnp.where(bucket == b, rb_ref[b, h], acc)
            o_ref[tile, h] = acc


def _bias_tiles(rel_bias):
    return pl.pallas_call(
        _bias_kernel,
        in_specs=[pl.BlockSpec(memory_space=pltpu.SMEM)],
        out_specs=pl.BlockSpec(memory_space=pltpu.VMEM),
        out_shape=jax.ShapeDtypeStruct((2, N_HEADS_A, QB, QB), jnp.float32),
        name="bias_tiles",
    )(rel_bias)


_MAIN_COLS = Q_RANK + KV_RANK + 3 * CONV_CH + MIX_C


def _proj_kernel(x_ref, mem_ref, wm_ref, ws_ref, qg_ref, kvg_ref, cw_ref, wmk_ref, wmv_ref,
                 cq_ref, ckv_ref, ckvt_ref, kidx_ref, iwt_ref, yb_ref, yc_ref,
                 carry_ref, mk_ref, mv_ref, *, tm):
    si = pl.program_id(1)

    @pl.when(si == 0)
    def _():
        carry_ref[...] = jnp.zeros_like(carry_ref)
        mb = mem_ref[0].astype(MXU_DTYPE)
        mk_ref[...] = _dot(mb, wmk_ref[...]).astype(MXU_DTYPE)
        mv_ref[...] = _dot(mb, wmv_ref[...]).astype(MXU_DTYPE)

    xb = x_ref[...].astype(MXU_DTYPE)
    p = _dot(xb, wm_ref[...])
    small = _dot(xb, ws_ref[...])

    o = 0
    cq = p[:, o:o + Q_RANK]; o += Q_RANK
    ckv = p[:, o:o + KV_RANK]; o += KV_RANK
    g_b = p[:, o:o + CONV_CH]; o += CONV_CH
    g_c = p[:, o:o + CONV_CH]; o += CONV_CH
    h_c = p[:, o:o + CONV_CH]; o += CONV_CH
    q_mem = p[:, o:o + MIX_C]

    cq = cq * lax.rsqrt(jnp.mean(cq * cq, axis=-1, keepdims=True) + RMS_EPS) * qg_ref[...]
    ckv = ckv * lax.rsqrt(jnp.mean(ckv * ckv, axis=-1, keepdims=True) + RMS_EPS) * kvg_ref[...]
    cq_ref[...] = cq.astype(MXU_DTYPE)
    ckv_b = ckv.astype(MXU_DTYPE)
    ckv_ref[...] = ckv_b
    ckvt_ref[0] = ckv.T.astype(MXU_DTYPE)

    kidx_ref[...] = small[:, :IDX_DIM].astype(MXU_DTYPE)
    small_t = small.T
    iwt_ref[0] = small_t[IDX_DIM:IDX_DIM + N_IDX_HEADS, :] * (N_IDX_HEADS ** -0.5 * IDX_DIM ** -0.5)

    u = g_c * h_c
    rows = lax.broadcasted_iota(jnp.int32, (tm, 1), 0)
    c6 = carry_ref[SUBLANES - 2:SUBLANES - 1, :]
    c7 = carry_ref[SUBLANES - 1:SUBLANES, :]
    u1 = jnp.where(rows == 0, c7, pltpu.roll(u, 1, 0))
    u2 = jnp.where(rows == 0, c6, jnp.where(rows == 1, c7, pltpu.roll(u, 2, 0)))
    y = cw_ref[0:1, :] * u2
    y = y + cw_ref[1:2, :] * u1
    y = y + cw_ref[2:3, :] * u
    yb_ref[...] = (g_b * y).astype(MXU_DTYPE)
    carry_ref[...] = u[tm - SUBLANES:, :]

    qm = q_mem.astype(MXU_DTYPE)
    outs = []
    for h in range(N_MEM_HEADS):
        sl = slice(h * HEAD_DIM, (h + 1) * HEAD_DIM)
        lg = _dot_nt(qm[:, sl], mk_ref[:, sl]) * (HEAD_DIM ** -0.5)
        lg = lg - jnp.max(lg, axis=-1, keepdims=True)
        e = jnp.exp(lg)
        pr = e / jnp.sum(e, axis=-1, keepdims=True)
        outs.append(_dot(pr.astype(MXU_DTYPE), mv_ref[:, sl]))
    yc_ref[...] = jnp.concatenate(outs, axis=-1).astype(MXU_DTYPE)


def _proj(x2, mem, w_main, w_small, q_g, kv_g, conv_w, w_mk, w_mv, B, S, tm):
    T, D = x2.shape
    n_mem = mem.shape[1]
    ns = S // tm
    row = lambda b, s: (b * ns + s, 0)
    const2 = lambda b, s: (0, 0)
    bf = MXU_DTYPE
    return pl.pallas_call(
        functools.partial(_proj_kernel, tm=tm),
        grid=(B, ns),
        in_specs=[
            pl.BlockSpec((tm, D), row),
            pl.BlockSpec((1, n_mem, D), lambda b, s: (b, 0, 0)),
            pl.BlockSpec(w_main.shape, const2),
            pl.BlockSpec(w_small.shape, const2),
            pl.BlockSpec(q_g.shape, const2),
            pl.BlockSpec(kv_g.shape, const2),
            pl.BlockSpec(conv_w.shape, const2),
            pl.BlockSpec(w_mk.shape, const2),
            pl.BlockSpec(w_mv.shape, const2),
        ],
        out_specs=[
            pl.BlockSpec((tm, Q_RANK), row),
            pl.BlockSpec((tm, KV_RANK), row),
            pl.BlockSpec((1, KV_RANK, tm), lambda b, s: (b, 0, s)),
            pl.BlockSpec((tm, IDX_DIM), row),
            pl.BlockSpec((1, N_IDX_HEADS, tm), lambda b, s: (b, 0, s)),
            pl.BlockSpec((tm, CONV_CH), row),
            pl.BlockSpec((tm, MIX_C), row),
        ],
        out_shape=[
            jax.ShapeDtypeStruct((T, Q_RANK), bf),
            jax.ShapeDtypeStruct((T, KV_RANK), bf),
            jax.ShapeDtypeStruct((B, KV_RANK, S), bf),
            jax.ShapeDtypeStruct((T, IDX_DIM), bf),
            jax.ShapeDtypeStruct((B, N_IDX_HEADS, S), jnp.float32),
            jax.ShapeDtypeStruct((T, CONV_CH), bf),
            jax.ShapeDtypeStruct((T, MIX_C), bf),
        ],
        scratch_shapes=[
            pltpu.VMEM((SUBLANES, CONV_CH), jnp.float32),
            pltpu.VMEM((n_mem, MIX_C), bf),
            pltpu.VMEM((n_mem, MIX_C), bf),
        ],
        compiler_params=_cparams(("arbitrary", "arbitrary")),
        name="proj",
    )(x2, mem, w_main, w_small, q_g, kv_g, conv_w, w_mk, w_mv)


def _key_to_f32(key):
    bits = jnp.where(key < 0, key ^ jnp.int32(0x7FFFFFFF), key)
    return pltpu.bitcast(bits, jnp.float32)


def _colsum8(v):
    return jnp.sum(v.reshape(QB // SUBLANES, SUBLANES, QB), axis=0)


def _colmax8(v):
    return jnp.max(v.reshape(QB // SUBLANES, SUBLANES, QB), axis=0)


def _dsa_kernel(cq_ref, iwt_ref, kidx_ref, ckv_ref, ckvt_ref, wqi_ref, wuq_ref, wuk_ref, wuvt_ref,
                bias_ref, bfar_ref, o_ref, score_ref, mask_ref, logit_ref, *, k_sel, idx_bits):
    i = pl.program_id(1)
    f32 = jnp.float32
    bf = MXU_DTYPE
    s_loc = lax.broadcasted_iota(jnp.int32, (QB, QB), 0)
    t_loc = lax.broadcasted_iota(jnp.int32, (QB, QB), 1)
    causal_diag = s_loc <= t_loc

    cq = cq_ref[...]
    q_idx = _dot(cq, wqi_ref[...]).astype(bf)
    iw = iwt_ref[0]

    def chunk_score(j):
        kc = kidx_ref[pl.ds(pl.multiple_of(j * QB, QB), QB), :]
        acc = jnp.zeros((QB, QB), f32)
        for h in range(N_IDX_HEADS):
            dts = _dot_nt(kc, q_idx[:, h * IDX_DIM:(h + 1) * IDX_DIM])
            acc = acc + jnp.maximum(dts, 0.0) * iw[h:h + 1, :]
        return acc + 0.0

    def score_body(j, c):
        score_ref[pl.ds(pl.multiple_of(j * QB, QB), QB), :] = chunk_score(j)
        return c

    lax.fori_loop(0, i, score_body, 0)
    diag = pl.multiple_of(i * QB, QB)
    score_ref[pl.ds(diag, QB), :] = jnp.where(causal_diag, chunk_score(i), F32_LOWEST)
    n_chunks = i + 1

    def count_where(pred):
        def body(j, acc):
            sc = score_ref[pl.ds(pl.multiple_of(j * QB, QB), QB), :]
            return acc + _colsum8(jnp.where(pred(sc, j), 1.0, 0.0))
        acc = lax.fori_loop(0, n_chunks, body, jnp.zeros((SUBLANES, QB), f32))
        return jnp.sum(acc, axis=0, keepdims=True)

    kf = float(k_sel)

    def search():
        c0 = count_where(lambda sc, j: sc >= 0.0)
        cand0 = jnp.where(c0 >= kf, jnp.int32(0), jnp.int32(-2 ** 31))

        def bit_body(it, cand):
            trial = cand + lax.shift_left(jnp.int32(1), 30 - it)
            tf = _key_to_f32(trial)
            cnt = count_where(lambda sc, j: sc >= tf)
            return jnp.where(cnt >= kf, trial, cand)

        cand = lax.fori_loop(0, 31, bit_body, cand0)
        thr = _key_to_f32(cand)
        n_gt = count_where(lambda sc, j: sc > thr)
        n_eq = count_where(lambda sc, j: sc == thr)
        need = kf - n_gt

        def tie_search():
            def tbody(it, xcut):
                trial = xcut + lax.shift_left(jnp.int32(1), idx_bits - 1 - it)
                cnt = count_where(lambda sc, j: (sc == thr) & (s_loc + j * QB < trial))
                return jnp.where(cnt < need, trial, xcut)
            return lax.fori_loop(0, idx_bits, tbody, jnp.zeros((1, QB), jnp.int32))

        any_extra = jnp.max(n_eq - need) > 0.0
        xcut = lax.cond(any_extra, tie_search, lambda: jnp.full((1, QB), 2 ** idx_bits - 1, jnp.int32))
        return thr, xcut

    def no_search():
        return jnp.full((1, QB), F32_LOWEST, f32), jnp.full((1, QB), 2 ** idx_bits - 1, jnp.int32)

    thr, xcut = lax.cond(n_chunks * QB > k_sel, search, no_search)

    def mask_chunk(j, extra):
        sc = score_ref[pl.ds(pl.multiple_of(j * QB, QB), QB), :]
        keep = (sc > thr) | ((sc == thr) & (s_loc + j * QB <= xcut))
        if extra is not None:
            keep = keep & extra
        mask_ref[pl.ds(pl.multiple_of(j * QB, QB), QB), :] = jnp.where(keep, 0.0, -jnp.inf)

    def mask_body(j, c):
        mask_chunk(j, None)
        return c

    lax.fori_loop(0, i, mask_body, 0)
    mask_chunk(i, causal_diag)

    q_all = _dot(cq, wuq_ref[...]).astype(bf)
    outs = []
    for h in range(N_HEADS_A):
        q_lat = (_dot_nt(q_all[:, h * HEAD_DIM:(h + 1) * HEAD_DIM], wuk_ref[h]) * (HEAD_DIM ** -0.5)).astype(bf)
        b_far = bfar_ref[h]

        def logits_chunk(j, bias):
            kv = ckv_ref[pl.ds(pl.multiple_of(j * QB, QB), QB), :]
            lg = _dot_nt(kv, q_lat) + bias + mask_ref[pl.ds(pl.multiple_of(j * QB, QB), QB), :]
            logit_ref[pl.ds(pl.multiple_of(j * QB, QB), QB), :] = lg
            return _colmax8(lg)

        def far_body(j, m8):
            return jnp.maximum(m8, logits_chunk(j, b_far))

        m8 = lax.fori_loop(0, jnp.maximum(i - 1, 0), far_body, jnp.full((SUBLANES, QB), -jnp.inf, f32))
        m8 = lax.cond(i >= 1, lambda m: jnp.maximum(m, logits_chunk(i - 1, bias_ref[1, h])), lambda m: m, m8)
        m8 = jnp.maximum(m8, logits_chunk(i, bias_ref[0, h]))
        m = jnp.max(m8, axis=0, keepdims=True)

        def pv_body(j, carry):
            l8, acc = carry
            off = pl.multiple_of(j * QB, QB)
            p = jnp.exp(logit_ref[pl.ds(off, QB), :] - m)
            kvt = ckvt_ref[0, :, pl.ds(off, QB)]
            return l8 + _colsum8(p), acc + _dot(kvt, p.astype(bf))

        l8, acc = lax.fori_loop(0, n_chunks, pv_body,
                                (jnp.zeros((SUBLANES, QB), f32), jnp.zeros((KV_RANK, QB), f32)))
        o_lat_t = (acc / jnp.sum(l8, axis=0, keepdims=True)).astype(bf)
        outs.append(_dot(wuvt_ref[h], o_lat_t))
    o_ref[...] = jnp.concatenate(outs, axis=0).T.astype(o_ref.dtype)


def _dsa(cq, iwt, kidx, ckv, ckvt, w_qidx, w_uq, w_uk_h, w_uvt_h, bias_tiles, bias_far, B, S):
    T = cq.shape[0]
    nq = S // QB
    k_sel = min(TOPK_MAX, S // 4)
    idx_bits = max(1, (S - 1).bit_length())
    c2 = lambda b, i: (0, 0)
    c3 = lambda b, i: (0, 0, 0)
    return pl.pallas_call(
        functools.partial(_dsa_kernel, k_sel=k_sel, idx_bits=idx_bits),
        grid=(B, nq),
        in_specs=[
            pl.BlockSpec((QB, Q_RANK), lambda b, i: (b * nq + i, 0)),
            pl.BlockSpec((1, N_IDX_HEADS, QB), lambda b, i: (b, 0, i)),
            pl.BlockSpec((S, IDX_DIM), lambda b, i: (b, 0)),
            pl.BlockSpec((S, KV_RANK), lambda b, i: (b, 0)),
            pl.BlockSpec((1, KV_RANK, S), lambda b, i: (b, 0, 0)),
            pl.BlockSpec(w_qidx.shape, c2),
            pl.BlockSpec(w_uq.shape, c2),
            pl.BlockSpec(w_uk_h.shape, c3),
            pl.BlockSpec(w_uvt_h.shape, c3),
            pl.BlockSpec(bias_tiles.shape, lambda b, i: (0, 0, 0, 0)),
            pl.BlockSpec(memory_space=pltpu.SMEM),
        ],
        out_specs=pl.BlockSpec((QB, MIX_A), lambda b, i: (b * nq + i, 0)),
        out_shape=jax.ShapeDtypeStruct((T, MIX_A), MXU_DTYPE),
        scratch_shapes=[
            pltpu.VMEM((S, QB), jnp.float32),
            pltpu.VMEM((S, QB), jnp.float32),
            pltpu.VMEM((S, QB), jnp.float32),
        ],
        compiler_params=_cparams(("arbitrary", "arbitrary")),
        name="dsa",
    )(cq, iwt, kidx, ckv, ckvt, w_qidx, w_uq, w_uk_h, w_uvt_h, bias_tiles, bias_far)


def _layer_norm(xf, g, b):
    mu = jnp.mean(xf, axis=-1, keepdims=True)
    xc = xf - mu
    var = jnp.mean(xc * xc, axis=-1, keepdims=True)
    return xc * lax.rsqrt(var + LN_EPS) * g + b


def _rank_rows(v, n):
    ri = lax.broadcasted_iota(jnp.int32, v.shape, 0)
    rank = jnp.zeros(v.shape, jnp.float32)
    for r2 in range(n):
        row = v[r2:r2 + 1, :]
        beats = (row > v) | ((row == v) & (ri > r2))
        rank = rank + jnp.where(beats, 1.0, 0.0)
    return rank


def _mix_router_kernel(x_ref, ya_ref, yb_ref, yc_ref, wo_ref, g_ref, b_ref, wrt_ref, rb_ref, exp_ref,
                       x1_ref, sel_ref, w_ref, pos_ref, cnt_ref, base_ref, *, tm):
    step = pl.program_id(0)
    f32 = jnp.float32

    @pl.when(step == 0)
    def _():
        base_ref[...] = jnp.zeros_like(base_ref)

    mix = _dot(ya_ref[...], wo_ref[0:MIX_A, :])
    mix = mix + _dot(yb_ref[...], wo_ref[MIX_A:MIX_A + CONV_CH, :])
    mix = mix + _dot(yc_ref[...], wo_ref[MIX_A + CONV_CH:, :])
    x1 = _layer_norm(ALPHA * x_ref[...] + mix, g_ref[...], b_ref[...])
    x1_ref[...] = x1

    lg = lax.dot_general(wrt_ref[...], x1, _NT, precision=lax.Precision.HIGHEST, preferred_element_type=f32)
    s = 1.0 / (1.0 + jnp.exp(-lg))
    sc = s + rb_ref[...]

    g3 = sc.reshape(N_GROUPS, GROUP_SIZE, tm)
    m1 = jnp.max(g3, axis=1, keepdims=True)
    is_m1 = g3 == m1
    n_m1 = jnp.sum(jnp.where(is_m1, 1.0, 0.0), axis=1, keepdims=True)
    m2 = jnp.max(jnp.where(is_m1, -jnp.inf, g3), axis=1, keepdims=True)
    gscore = (m1 + jnp.where(n_m1 > 1.0, m1, m2)).reshape(N_GROUPS, tm)
    gsel = jnp.where(_rank_rows(gscore, N_GROUPS) < float(TOPK_GROUPS), 1.0, 0.0)
    emask = _dot(exp_ref[...], gsel.astype(MXU_DTYPE)) > 0.5
    masked = jnp.where(emask, sc, -jnp.inf)
    sel = (_rank_rows(masked, N_EXPERTS) < float(TOP_K)) & emask
    self_ = jnp.where(sel, 1.0, 0.0)
    top_s = jnp.where(sel, s, 0.0)
    w = top_s / jnp.sum(top_s, axis=0, keepdims=True) * ROUTED_SCALE

    t_r = lax.broadcasted_iota(jnp.int32, (tm, tm), 0)
    t_c = lax.broadcasted_iota(jnp.int32, (tm, tm), 1)
    upper = jnp.where(t_r < t_c, 1.0, 0.0).astype(MXU_DTYPE)
    pref = _dot(self_.astype(MXU_DTYPE), upper)
    base = base_ref[...]
    sel_ref[...] = self_
    w_ref[...] = w
    pos_ref[...] = base + pref
    base = base + jnp.sum(self_, axis=1, keepdims=True)
    base_ref[...] = base
    cnt_ref[...] = jnp.broadcast_to(base, cnt_ref.shape)


def _mix_router(x2, ya, yb, yc, w_out, ln_g, ln_b, w_router_t, router_bias, tm):
    T, D = x2.shape
    E = N_EXPERTS
    expand = (jnp.arange(E)[:, None] // GROUP_SIZE == jnp.arange(N_GROUPS)[None, :]).astype(MXU_DTYPE)
    row = lambda i: (i, 0)
    col = lambda i: (0, i)
    c2 = lambda i: (0, 0)
    f32 = jnp.float32
    return pl.pallas_call(
        functools.partial(_mix_router_kernel, tm=tm),
        grid=(T // tm,),
        in_specs=[
            pl.BlockSpec((tm, D), row),
            pl.BlockSpec((tm, MIX_A), row),
            pl.BlockSpec((tm, CONV_CH), row),
            pl.BlockSpec((tm, MIX_C), row),
            pl.BlockSpec(w_out.shape, c2),
            pl.BlockSpec((1, D), c2),
            pl.BlockSpec((1, D), c2),
            pl.BlockSpec((E, D), c2),
            pl.BlockSpec((E, 1), c2),
            pl.BlockSpec((E, N_GROUPS), c2),
        ],
        out_specs=[
            pl.BlockSpec((tm, D), row),
            pl.BlockSpec((E, tm), col),
            pl.BlockSpec((E, tm), col),
            pl.BlockSpec((E, tm), col),
            pl.BlockSpec((E, LANES), c2),
        ],
        out_shape=[
            jax.ShapeDtypeStruct((T, D), f32),
            jax.ShapeDtypeStruct((E, T), f32),
            jax.ShapeDtypeStruct((E, T), f32),
            jax.ShapeDtypeStruct((E, T), f32),
            jax.ShapeDtypeStruct((E, LANES), f32),
        ],
        scratch_shapes=[pltpu.VMEM((E, 1), f32)],
        compiler_params=_cparams(("arbitrary",)),
        name="mix_router",
    )(x2, ya, yb, yc, w_out, ln_g, ln_b, w_router_t, router_bias, expand)


def _compact_kernel(sel_ref, w_ref, pos_ref, pstart_ref, low_ref, dest_ref, wk_ref):
    sel = sel_ref[...]
    on = sel > 0.5
    rank = _dot(low_ref[...], sel.astype(MXU_DTYPE))
    row = pstart_ref[...] + pos_ref[...]
    w = w_ref[...]
    dests, ws = [], []
    for k in range(TOP_K):
        m = on & (rank == float(k))
        dests.append(jnp.sum(jnp.where(m, row, 0.0), axis=0, keepdims=True))
        ws.append(jnp.sum(jnp.where(m, w, 0.0), axis=0, keepdims=True))
    dest_ref[...] = jnp.concatenate(dests, axis=0).astype(jnp.int32)
    wk_ref[...] = jnp.concatenate(ws, axis=0)


def _compact(sel_t, w_t, pos_t, pad_start, tm):
    E, T = sel_t.shape
    lower = (jnp.arange(E)[None, :] < jnp.arange(E)[:, None]).astype(MXU_DTYPE)
    col = lambda i: (0, i)
    c2 = lambda i: (0, 0)
    return pl.pallas_call(
        _compact_kernel,
        grid=(T // tm,),
        in_specs=[pl.BlockSpec((E, tm), col), pl.BlockSpec((E, tm), col), pl.BlockSpec((E, tm), col),
                  pl.BlockSpec((E, 1), c2), pl.BlockSpec((E, E), c2)],
        out_specs=[pl.BlockSpec((TOP_K, tm), col), pl.BlockSpec((TOP_K, tm), col)],
        out_shape=[jax.ShapeDtypeStruct((TOP_K, T), jnp.int32), jax.ShapeDtypeStruct((TOP_K, T), jnp.float32)],
        compiler_params=_cparams(("arbitrary",)),
        name="route_compact",
    )(sel_t, w_t, pos_t, pad_start, lower)


def _row_copy(src, s, dst, d, sem):
    return pltpu.make_async_copy(src.at[pl.ds(s, 1)], dst.at[pl.ds(d, 1)], sem)


def _dispatch_kernel(flo_ref, fhi_ref, dest_ref, x_hbm, xs_hbm, zero_ref, sem, zsem, *, td):
    step = pl.program_id(0)

    @pl.when(step == 0)
    def _():
        zero_ref[...] = jnp.zeros_like(zero_ref)

        def per_expert(fn):
            def ebody(e, c):
                lax.fori_loop(flo_ref[e], fhi_ref[e], lambda r, c2: (fn(r), c2)[1], 0)
                return c
            lax.fori_loop(0, N_EXPERTS, ebody, 0)

        per_expert(lambda r: _row_copy(zero_ref, 0, xs_hbm, r, zsem).start())
        per_expert(lambda r: _row_copy(zero_ref, 0, xs_hbm, r, zsem).wait())

    base = step * td

    def issue(r, c):
        for k in range(TOP_K):
            _row_copy(x_hbm, base + r, xs_hbm, dest_ref[k, r], sem).start()
        return c

    def drain(r, c):
        for k in range(TOP_K):
            _row_copy(x_hbm, base + r, xs_hbm, dest_ref[k, r], sem).wait()
        return c

    lax.fori_loop(0, td, issue, 0)
    lax.fori_loop(0, td, drain, 0)


def _dispatch(dest_t, x1, fill_lo, fill_hi, n_rows, td):
    T, D = x1.shape
    return pl.pallas_call(
        functools.partial(_dispatch_kernel, td=td),
        grid_spec=pltpu.PrefetchScalarGridSpec(
            num_scalar_prefetch=2,
            grid=(T // td,),
            in_specs=[
                pl.BlockSpec((TOP_K, td), lambda i, lo, hi: (0, i), memory_space=pltpu.SMEM),
                pl.BlockSpec(memory_space=pl.ANY),
            ],
            out_specs=pl.BlockSpec(memory_space=pl.ANY),
            scratch_shapes=[pltpu.VMEM((SUBLANES, D), jnp.float32),
                            pltpu.SemaphoreType.DMA, pltpu.SemaphoreType.DMA],
        ),
        out_shape=jax.ShapeDtypeStruct((n_rows, D), jnp.float32),
        compiler_params=_cparams(("arbitrary",)),
        name="dispatch",
    )(fill_lo, fill_hi, dest_t, x1)


def _silu(g):
    return g / (1.0 + jnp.exp(-g))


def _expert_kernel(be_ref, nu_ref, xs_ref, wg_ref, wu_ref, wd_ref, ys_ref):
    @pl.when(pl.program_id(0) < nu_ref[0])
    def _():
        xb = xs_ref[...].astype(MXU_DTYPE)
        a = (_silu(_dot(xb, wg_ref[0])) * _dot(xb, wu_ref[0])).astype(MXU_DTYPE)
        ys_ref[...] = _dot(a, wd_ref[0])


def _experts(xs, block_e, n_used, w_gate, w_up, w_down):
    n_rows, D = xs.shape
    n_blocks = n_rows // MOE_BLOCK
    blk = lambda i, be, nu: (jnp.minimum(i, nu[0] - 1), 0)
    wsel = lambda i, be, nu: (be[i], 0, 0)
    return pl.pallas_call(
        _expert_kernel,
        grid_spec=pltpu.PrefetchScalarGridSpec(
            num_scalar_prefetch=2,
            grid=(n_blocks,),
            in_specs=[
                pl.BlockSpec((MOE_BLOCK, D), blk),
                pl.BlockSpec((1, D, D_EXPERT), wsel),
                pl.BlockSpec((1, D, D_EXPERT), wsel),
                pl.BlockSpec((1, D_EXPERT, D), wsel),
            ],
            out_specs=pl.BlockSpec((MOE_BLOCK, D), blk),
        ),
        out_shape=jax.ShapeDtypeStruct((n_rows, D), jnp.float32),
        compiler_params=_cparams(("arbitrary",)),
        name="experts",
    )(block_e, n_used, xs, w_gate, w_up, w_down)


def _combine_kernel(dest_ref, wk_ref, x1_ref, ys_hbm, wsg_ref, wsu_ref, wsd_ref, g_ref, b_ref,
                    o_ref, buf_ref, sem, *, tc):
    def issue(r, c):
        for k in range(TOP_K):
            pltpu.make_async_copy(ys_hbm.at[pl.ds(dest_ref[k, r], 1)], buf_ref.at[k, pl.ds(r, 1)], sem).start()
        return c

    def drain(r, c):
        for k in range(TOP_K):
            pltpu.make_async_copy(ys_hbm.at[pl.ds(dest_ref[k, r], 1)], buf_ref.at[k, pl.ds(r, 1)], sem).wait()
        return c

    lax.fori_loop(0, tc, issue, 0)
    x1 = x1_ref[...]
    xb = x1.astype(MXU_DTYPE)
    a = (_silu(_dot(xb, wsg_ref[...])) * _dot(xb, wsu_ref[...])).astype(MXU_DTYPE)
    shared = _dot(a, wsd_ref[...])
    lax.fori_loop(0, tc, drain, 0)
    wk = wk_ref[...]
    routed = wk[:, 0:1] * buf_ref[0]
    for k in range(1, TOP_K):
        routed = routed + wk[:, k:k + 1] * buf_ref[k]
    o_ref[...] = _layer_norm(ALPHA * x1 + (routed + shared), g_ref[...], b_ref[...])


def _combine(dest_t, wk, x1, ys, w_sg, w_su, w_sd, ln_g, ln_b, tc):
    T, D = x1.shape
    row = lambda i: (i, 0)
    c2 = lambda i: (0, 0)
    return pl.pallas_call(
        functools.partial(_combine_kernel, tc=tc),
        grid=(T // tc,),
        in_specs=[
            pl.BlockSpec((TOP_K, tc), lambda i: (0, i), memory_space=pltpu.SMEM),
            pl.BlockSpec((tc, TOP_K), row),
            pl.BlockSpec((tc, D), row),
            pl.BlockSpec(memory_space=pl.ANY),
            pl.BlockSpec(w_sg.shape, c2),
            pl.BlockSpec(w_su.shape, c2),
            pl.BlockSpec(w_sd.shape, c2),
            pl.BlockSpec((1, D), c2),
            pl.BlockSpec((1, D), c2),
        ],
        out_specs=pl.BlockSpec((tc, D), row),
        out_shape=jax.ShapeDtypeStruct((T, D), jnp.float32),
        scratch_shapes=[pltpu.VMEM((TOP_K, tc, D), jnp.float32), pltpu.SemaphoreType.DMA],
        compiler_params=_cparams(("arbitrary",)),
        name="combine",
    )(dest_t, wk, x1, ys, w_sg, w_su, w_sd, ln_g, ln_b)


def _split_w_in(w_in):
    bf = MXU_DTYPE
    o_kv = Q_RANK
    o_ki = o_kv + KV_RANK
    o_iw = o_ki + IDX_DIM
    o_rest = o_iw + N_IDX_HEADS
    w_main = jnp.concatenate([w_in[:, :o_ki], w_in[:, o_rest:]], axis=1).astype(bf)
    w_small = jnp.pad(w_in[:, o_ki:o_rest], ((0, 0), (0, LANES - IDX_DIM - N_IDX_HEADS))).astype(bf)
    return w_main, w_small


def _stages(x, mem, w_in, q_norm_g, kv_norm_g, w_uq, w_uk, w_uv, w_qidx, rel_bias, conv_w, w_mem_k, w_mem_v, w_out, ln1_g, ln1_b, w_router, router_bias, w_e_gate, w_e_up, w_e_down, w_s_gate, w_s_up, w_s_down, ln2_g, ln2_b, upto=None):
    B, S, D = x.shape
    T = B * S
    bf = MXU_DTYPE
    l = 0
    res = {}
    x2 = x.reshape(T, D)
    w_main, w_small = _split_w_in(w_in[l])
    cq, ckv, ckvt, kidx, iwt, yb, yc = _proj(
        x2, mem, w_main, w_small, q_norm_g[l].reshape(1, -1), kv_norm_g[l].reshape(1, -1), conv_w[l],
        w_mem_k[l].astype(bf), w_mem_v[l].astype(bf), B, S, tm=min(512, S))
    res.update(c_q=cq, c_kv=ckv, k_idx=kidx, y_b=yb, y_c=yc,
               idx_w=jnp.swapaxes(iwt, 1, 2) / (N_IDX_HEADS ** -0.5 * IDX_DIM ** -0.5))
    if upto == "proj":
        return res
    bias_t = _bias_tiles(rel_bias)
    ya = _dsa(cq, iwt, kidx, ckv, ckvt,
              w_qidx[l].reshape(Q_RANK, -1).astype(bf), w_uq[l].reshape(Q_RANK, -1).astype(bf),
              jnp.transpose(w_uk[l], (1, 0, 2)).astype(bf), jnp.transpose(w_uv[l], (1, 2, 0)).astype(bf),
              bias_t, rel_bias[REL_BUCKETS - 1], B, S)
    res.update(y_a=ya)
    if upto == "dsa":
        return res

    x1, sel_t, w_t, pos_t, cnt = _mix_router(
        x2, ya, yb, yc, w_out[l].astype(bf), ln1_g[l].reshape(1, -1), ln1_b[l].reshape(1, -1),
        w_router[l].T, router_bias[l].reshape(-1, 1), tm=min(512, T))
    res.update(x1=x1)

    counts = cnt[:, 0].astype(jnp.int32)
    padded = (counts + MOE_BLOCK - 1) // MOE_BLOCK * MOE_BLOCK
    pad_end = jnp.cumsum(padded)
    pad_start = pad_end - padded
    n_blocks = -(-(T * TOP_K) // MOE_BLOCK) + N_EXPERTS
    n_rows = n_blocks * MOE_BLOCK
    block_e = jnp.minimum(jnp.searchsorted(pad_end, jnp.arange(n_blocks) * MOE_BLOCK, side='right'),
                          N_EXPERTS - 1).astype(jnp.int32)
    n_used = (pad_end[-1:] // MOE_BLOCK).astype(jnp.int32)

    dest_t, wk_t = _compact(sel_t, w_t, pos_t, pad_start.astype(jnp.float32).reshape(-1, 1), tm=min(512, T))
    xs = _dispatch(dest_t, x1, (pad_start + counts).astype(jnp.int32), pad_end.astype(jnp.int32), n_rows,
                   td=min(256, T))
    ys = _experts(xs, block_e, n_used, w_e_gate[l].astype(bf), w_e_up[l].astype(bf), w_e_down[l].astype(bf))
    out = _combine(dest_t, wk_t.T, x1, ys, w_s_gate[l].astype(bf), w_s_up[l].astype(bf), w_s_down[l].astype(bf),
                   ln2_g[l].reshape(1, -1), ln2_b[l].reshape(1, -1), tc=min(128, T))
    res.update(out=out.reshape(B, S, D))
    return res


def kernel(x, mem, w_in, q_norm_g, kv_norm_g, w_uq, w_uk, w_uv, w_qidx, rel_bias, conv_w, w_mem_k, w_mem_v, w_out, ln1_g, ln1_b, w_router, router_bias, w_e_gate, w_e_up, w_e_down, w_s_gate, w_s_up, w_s_down, ln2_g, ln2_b):
    return _stages(x, mem, w_in, q_norm_g, kv_norm_g, w_uq, w_uk, w_uv, w_qidx, rel_bias, conv_w, w_mem_k, w_mem_v, w_out, ln1_g, ln1_b, w_router, router_bias, w_e_gate, w_e_up, w_e_down, w_s_gate, w_s_up, w_s_down, ln2_g, ln2_b)["out"]
```

```python
import functools
import math

import jax
import jax.numpy as jnp
from jax import lax
from jax.experimental import pallas as pl
from jax.experimental.pallas import tpu as pltpu

N_HEADS_A = 8
HEAD_DIM = 64
Q_RANK = 256
KV_RANK = 128
N_IDX_HEADS = 8
IDX_DIM = 64
TOPK_MAX = 256
REL_BUCKETS = 32
REL_MAX_DIST = 128
CONV_CH = 256
CONV_WIDTH = 3
N_MEM_HEADS = 4
MIX_A = N_HEADS_A * HEAD_DIM
MIX_C = N_MEM_HEADS * HEAD_DIM
N_EXPERTS = 64
N_GROUPS = 8
GROUP_SIZE = N_EXPERTS // N_GROUPS
TOPK_GROUPS = 4
TOP_K = 8
D_EXPERT = 256
ROUTED_SCALE = 2.5
MOE_BLOCK = 256
DEPTH = 1
ALPHA = (2.0 * DEPTH) ** 0.25
LN_EPS = 1e-5
RMS_EPS = 1e-6

LANES = 128
SUBLANES = 8
QB = 128
F32_LOWEST = -3.4028234663852886e38
VMEM_LIMIT = 56 * 1024 * 1024
MXU_DTYPE = jnp.bfloat16

_NT = (((1,), (1,)), ((), ()))


def _dot(a, b):
    return jnp.dot(a, b, preferred_element_type=jnp.float32)


def _dot_nt(a, b):
    return lax.dot_general(a, b, _NT, preferred_element_type=jnp.float32)


def _cparams(sem):
    return pltpu.CompilerParams(dimension_semantics=sem, vmem_limit_bytes=VMEM_LIMIT)


def _bias_kernel(rb_ref, o_ref):
    s = lax.broadcasted_iota(jnp.int32, (QB, QB), 0)
    t = lax.broadcasted_iota(jnp.int32, (QB, QB), 1)
    max_exact = REL_BUCKETS // 2
    for tile in range(3):
        n = jnp.maximum(t - s + (2 - tile) * QB, 0)
        nf = jnp.maximum(n.astype(jnp.float32), 1.0)
        large = max_exact + (jnp.log(nf / max_exact) / math.log(REL_MAX_DIST / max_exact)
                             * (REL_BUCKETS - max_exact)).astype(jnp.int32)
        large = jnp.minimum(large, REL_BUCKETS - 1)
        bucket = jnp.where(n < max_exact, n, large)
        for h in range(N_HEADS_A):
            acc = jnp.zeros((QB, QB), jnp.float32)
            for b in range(REL_BUCKETS):
                acc = jnp.where(bucket == b, rb_ref[b, h], acc)
            o_ref[tile, h] = acc


def _bias_tiles(rel_bias):
    return pl.pallas_call(
        _bias_kernel,
        in_specs=[pl.BlockSpec(memory_space=pltpu.SMEM)],
        out_specs=pl.BlockSpec(memory_space=pltpu.VMEM),
        out_shape=jax.ShapeDtypeStruct((3, N_HEADS_A, QB, QB), jnp.float32),
        name="bias_tiles",
    )(rel_bias)


_MAIN_COLS = Q_RANK + KV_RANK + 3 * CONV_CH + MIX_C


def _proj_kernel(x_ref, mem_ref, wm_ref, ws_ref, qg_ref, kvg_ref, cw_ref, wmk_ref, wmv_ref,
                 cq_ref, ckv_ref, ckvt_ref, kidx_ref, iwt_ref, yb_ref, yc_ref,
                 carry_ref, mk_ref, mv_ref, *, tm):
    si = pl.program_id(1)

    @pl.when(si == 0)
    def _():
        carry_ref[...] = jnp.zeros_like(carry_ref)
        mb = mem_ref[0].astype(MXU_DTYPE)
        mk_ref[...] = _dot(mb, wmk_ref[...]).astype(MXU_DTYPE)
        mv_ref[...] = _dot(mb, wmv_ref[...]).astype(MXU_DTYPE)

    xb = x_ref[...].astype(MXU_DTYPE)
    p = _dot(xb, wm_ref[...])
    small = _dot(xb, ws_ref[...])

    o = 0
    cq = p[:, o:o + Q_RANK]; o += Q_RANK
    ckv = p[:, o:o + KV_RANK]; o += KV_RANK
    g_b = p[:, o:o + CONV_CH]; o += CONV_CH
    g_c = p[:, o:o + CONV_CH]; o += CONV_CH
    h_c = p[:, o:o + CONV_CH]; o += CONV_CH
    q_mem = p[:, o:o + MIX_C]

    cq = cq * lax.rsqrt(jnp.mean(cq * cq, axis=-1, keepdims=True) + RMS_EPS) * qg_ref[...]
    ckv = ckv * lax.rsqrt(jnp.mean(ckv * ckv, axis=-1, keepdims=True) + RMS_EPS) * kvg_ref[...]
    cq_ref[...] = cq.astype(MXU_DTYPE)
    ckv_b = ckv.astype(MXU_DTYPE)
    ckv_ref[...] = ckv_b
    ckvt_ref[0] = ckv.T.astype(MXU_DTYPE)

    kidx_ref[...] = small[:, :IDX_DIM].astype(MXU_DTYPE)
    small_t = small.T
    iwt_ref[0] = small_t[IDX_DIM:IDX_DIM + N_IDX_HEADS, :] * (N_IDX_HEADS ** -0.5 * IDX_DIM ** -0.5)

    u = g_c * h_c
    rows = lax.broadcasted_iota(jnp.int32, (tm, 1), 0)
    c6 = carry_ref[SUBLANES - 2:SUBLANES - 1, :]
    c7 = carry_ref[SUBLANES - 1:SUBLANES, :]
    u1 = jnp.where(rows == 0, c7, pltpu.roll(u, 1, 0))
    u2 = jnp.where(rows == 0, c6, jnp.where(rows == 1, c7, pltpu.roll(u, 2, 0)))
    y = cw_ref[0:1, :] * u2
    y = y + cw_ref[1:2, :] * u1
    y = y + cw_ref[2:3, :] * u
    yb_ref[...] = (g_b * y).astype(MXU_DTYPE)
    carry_ref[...] = u[tm - SUBLANES:, :]

    qm = q_mem.astype(MXU_DTYPE)
    outs = []
    for h in range(N_MEM_HEADS):
        sl = slice(h * HEAD_DIM, (h + 1) * HEAD_DIM)
        lg = _dot_nt(qm[:, sl], mk_ref[:, sl]) * (HEAD_DIM ** -0.5)
        lg = lg - jnp.max(lg, axis=-1, keepdims=True)
        e = jnp.exp(lg)
        pr = e / jnp.sum(e, axis=-1, keepdims=True)
        outs.append(_dot(pr.astype(MXU_DTYPE), mv_ref[:, sl]))
    yc_ref[...] = jnp.concatenate(outs, axis=-1).astype(MXU_DTYPE)


def _proj(x2, mem, w_main, w_small, q_g, kv_g, conv_w, w_mk, w_mv, B, S, tm):
    T, D = x2.shape
    n_mem = mem.shape[1]
    ns = S // tm
    row = lambda b, s: (b * ns + s, 0)
    const2 = lambda b, s: (0, 0)
    bf = MXU_DTYPE
    return pl.pallas_call(
        functools.partial(_proj_kernel, tm=tm),
        grid=(B, ns),
        in_specs=[
            pl.BlockSpec((tm, D), row),
            pl.BlockSpec((1, n_mem, D), lambda b, s: (b, 0, 0)),
            pl.BlockSpec(w_main.shape, const2),
            pl.BlockSpec(w_small.shape, const2),
            pl.BlockSpec(q_g.shape, const2),
            pl.BlockSpec(kv_g.shape, const2),
            pl.BlockSpec(conv_w.shape, const2),
            pl.BlockSpec(w_mk.shape, const2),
            pl.BlockSpec(w_mv.shape, const2),
        ],
        out_specs=[
            pl.BlockSpec((tm, Q_RANK), row),
            pl.BlockSpec((tm, KV_RANK), row),
            pl.BlockSpec((1, KV_RANK, tm), lambda b, s: (b, 0, s)),
            pl.BlockSpec((tm, IDX_DIM), row),
            pl.BlockSpec((1, N_IDX_HEADS, tm), lambda b, s: (b, 0, s)),
            pl.BlockSpec((tm, CONV_CH), row),
            pl.BlockSpec((tm, MIX_C), row),
        ],
        out_shape=[
            jax.ShapeDtypeStruct((T, Q_RANK), bf),
            jax.ShapeDtypeStruct((T, KV_RANK), bf),
            jax.ShapeDtypeStruct((B, KV_RANK, S), bf),
            jax.ShapeDtypeStruct((T, IDX_DIM), bf),
            jax.ShapeDtypeStruct((B, N_IDX_HEADS, S), jnp.float32),
            jax.ShapeDtypeStruct((T, CONV_CH), bf),
            jax.ShapeDtypeStruct((T, MIX_C), bf),
        ],
        scratch_shapes=[
            pltpu.VMEM((SUBLANES, CONV_CH), jnp.float32),
            pltpu.VMEM((n_mem, MIX_C), bf),
            pltpu.VMEM((n_mem, MIX_C), bf),
        ],
        compiler_params=_cparams(("arbitrary", "arbitrary")),
        name="proj",
    )(x2, mem, w_main, w_small, q_g, kv_g, conv_w, w_mk, w_mv)


def _key_to_f32(key):
    bits = jnp.where(key < 0, key ^ jnp.int32(0x7FFFFFFF), key)
    return pltpu.bitcast(bits, jnp.float32)


def _colsum8(v):
    return jnp.sum(v.reshape(QB // SUBLANES, SUBLANES, QB), axis=0)


def _colmax8(v):
    return jnp.max(v.reshape(QB // SUBLANES, SUBLANES, QB), axis=0)


NSB = 2
CK = NSB * QB


def _dsa_kernel(cq_ref, iwt_ref, kidx_ref, ckv_ref, ckvt_ref, wqi_ref, wuq_ref, wuk_ref, wuvt_ref,
                bias_ref, o_ref, qidx_ref, qlat_ref, score_ref, mask_ref, logit_ref, acc_ref,
                *, k_sel, idx_bits):
    i = pl.program_id(1)
    f32 = jnp.float32
    bf = MXU_DTYPE
    n_it = (i + NSB) // NSB
    s_loc = lax.broadcasted_iota(jnp.int32, (QB, QB), 0)
    t_glob = i * QB + lax.broadcasted_iota(jnp.int32, (QB, QB), 1)

    def blk(it, sb):
        return pl.multiple_of((it * NSB + sb) * QB, QB)

    cq = cq_ref[...]
    q_all = _dot(cq, wuq_ref[...]).astype(bf)
    for h in range(N_HEADS_A):
        qidx_ref[h * QB:(h + 1) * QB, :] = _dot(cq, wqi_ref[:, h * IDX_DIM:(h + 1) * IDX_DIM]).astype(bf)
        qlat_ref[h * QB:(h + 1) * QB, :] = (
            _dot_nt(q_all[:, h * HEAD_DIM:(h + 1) * HEAD_DIM], wuk_ref[h]) * (HEAD_DIM ** -0.5)).astype(bf)
    iw = iwt_ref[0]

    def score_body(it, c):
        for sb in range(NSB):
            off = blk(it, sb)
            d_all = _dot_nt(kidx_ref[pl.ds(off, QB), :], qidx_ref[...])
            acc = jnp.maximum(d_all[:, 0:QB], 0.0) * iw[0:1, :]
            for h in range(1, N_IDX_HEADS):
                acc = acc + jnp.maximum(d_all[:, h * QB:(h + 1) * QB], 0.0) * iw[h:h + 1, :]
            score_ref[pl.ds(off, QB), :] = jnp.where(s_loc + off <= t_glob, acc + 0.0, F32_LOWEST)
        return c

    lax.fori_loop(0, n_it, score_body, 0)

    def count_where(pred):
        def body(it, acc):
            for sb in range(NSB):
                off = blk(it, sb)
                acc = acc + _colsum8(jnp.where(pred(score_ref[pl.ds(off, QB), :], off), 1.0, 0.0))
            return acc
        acc = lax.fori_loop(0, n_it, body, jnp.zeros((SUBLANES, QB), f32))
        return jnp.sum(acc, axis=0, keepdims=True)

    kf = float(k_sel)

    def search():
        c0 = count_where(lambda sc, off: sc >= 0.0)
        cand0 = jnp.where(c0 >= kf, jnp.int32(0), jnp.int32(-2 ** 31))

        def bit_body(it, cand):
            trial = cand + lax.shift_left(jnp.int32(1), 30 - it)
            tf = _key_to_f32(trial)
            cnt = count_where(lambda sc, off: sc >= tf)
            return jnp.where(cnt >= kf, trial, cand)

        cand = lax.fori_loop(0, 31, bit_body, cand0)
        thr = _key_to_f32(cand)
        n_gt = count_where(lambda sc, off: sc > thr)
        n_eq = count_where(lambda sc, off: sc == thr)
        need = kf - n_gt

        def tie_search():
            def tbody(it, xcut):
                trial = xcut + lax.shift_left(jnp.int32(1), idx_bits - 1 - it)
                cnt = count_where(lambda sc, off: (sc == thr) & (s_loc + off < trial))
                return jnp.where(cnt < need, trial, xcut)
            return lax.fori_loop(0, idx_bits, tbody, jnp.zeros((1, QB), jnp.int32))

        any_extra = jnp.max(n_eq - need) > 0.0
        xcut = lax.cond(any_extra, tie_search, lambda: jnp.full((1, QB), 2 ** idx_bits - 1, jnp.int32))
        return thr, xcut

    def no_search():
        return jnp.full((1, QB), F32_LOWEST, f32), jnp.full((1, QB), 2 ** idx_bits - 1, jnp.int32)

    thr, xcut = lax.cond((i + 1) * QB > k_sel, search, no_search)

    def mask_body(it, c):
        for sb in range(NSB):
            off = blk(it, sb)
            sc = score_ref[pl.ds(off, QB), :]
            s_glob = s_loc + off
            keep = ((sc > thr) | ((sc == thr) & (s_glob <= xcut))) & (s_glob <= t_glob)
            mask_ref[pl.ds(off, QB), :] = jnp.where(keep, 0.0, -jnp.inf)
        return c

    lax.fori_loop(0, n_it, mask_body, 0)

    def p1_body(it, m8):
        m8 = list(m8)
        for sb in range(NSB):
            off = blk(it, sb)
            lg = _dot_nt(ckv_ref[pl.ds(off, QB), :], qlat_ref[...])
            msk = mask_ref[pl.ds(off, QB), :]
            bsel = jnp.clip(it * NSB + sb - i + 2, 0, 2)
            for h in range(N_HEADS_A):
                lgh = lg[:, h * QB:(h + 1) * QB] + bias_ref[bsel, h] + msk
                logit_ref[pl.ds(off, QB), h * QB:(h + 1) * QB] = lgh
                m8[h] = jnp.maximum(m8[h], _colmax8(lgh))
        return tuple(m8)

    m8 = lax.fori_loop(0, n_it, p1_body,
                       tuple(jnp.full((SUBLANES, QB), -jnp.inf, f32) for _ in range(N_HEADS_A)))
    m_row = [jnp.max(m, axis=0, keepdims=True) for m in m8]

    acc_ref[...] = jnp.zeros_like(acc_ref)

    def p2_body(it, l8):
        l8 = list(l8)
        off = pl.multiple_of(it * CK, CK)
        ps = []
        for h in range(N_HEADS_A):
            p = jnp.exp(logit_ref[pl.ds(off, CK), h * QB:(h + 1) * QB] - m_row[h])
            l8[h] = l8[h] + jnp.sum(p.reshape(CK // SUBLANES, SUBLANES, QB), axis=0)
            ps.append(p.astype(bf))
        acc_ref[...] += _dot(ckvt_ref[0, :, pl.ds(off, CK)], jnp.concatenate(ps, axis=1))
        return tuple(l8)

    l8 = lax.fori_loop(0, n_it, p2_body,
                       tuple(jnp.zeros((SUBLANES, QB), f32) for _ in range(N_HEADS_A)))

    outs = []
    for h in range(N_HEADS_A):
        l_row = jnp.sum(l8[h], axis=0, keepdims=True)
        o_lat_t = (acc_ref[:, h * QB:(h + 1) * QB] / l_row).astype(bf)
        outs.append(_dot(wuvt_ref[h], o_lat_t))
    o_ref[...] = jnp.concatenate(outs, axis=0).T.astype(o_ref.dtype)


def _dsa(cq, iwt, kidx, ckv, ckvt, w_qidx, w_uq, w_uk_h, w_uvt_h, bias_tiles, B, S):
    T = cq.shape[0]
    assert S % CK == 0 and QB >= REL_MAX_DIST
    nq = S // QB
    k_sel = min(TOPK_MAX, S // 4)
    idx_bits = max(1, (S - 1).bit_length())
    c2 = lambda b, i: (0, 0)
    c3 = lambda b, i: (0, 0, 0)
    return pl.pallas_call(
        functools.partial(_dsa_kernel, k_sel=k_sel, idx_bits=idx_bits),
        grid=(B, nq),
        in_specs=[
            pl.BlockSpec((QB, Q_RANK), lambda b, i: (b * nq + i, 0)),
            pl.BlockSpec((1, N_IDX_HEADS, QB), lambda b, i: (b, 0, i)),
            pl.BlockSpec((S, IDX_DIM), lambda b, i: (b, 0)),
            pl.BlockSpec((S, KV_RANK), lambda b, i: (b, 0)),
            pl.BlockSpec((1, KV_RANK, S), lambda b, i: (b, 0, 0)),
            pl.BlockSpec(w_qidx.shape, c2),
            pl.BlockSpec(w_uq.shape, c2),
            pl.BlockSpec(w_uk_h.shape, c3),
            pl.BlockSpec(w_uvt_h.shape, c3),
            pl.BlockSpec(bias_tiles.shape, lambda b, i: (0, 0, 0, 0)),
        ],
        out_specs=pl.BlockSpec((QB, MIX_A), lambda b, i: (b * nq + i, 0)),
        out_shape=jax.ShapeDtypeStruct((T, MIX_A), MXU_DTYPE),
        scratch_shapes=[
            pltpu.VMEM((N_IDX_HEADS * QB, IDX_DIM), MXU_DTYPE),
            pltpu.VMEM((N_HEADS_A * QB, KV_RANK), MXU_DTYPE),
            pltpu.VMEM((S, QB), jnp.float32),
            pltpu.VMEM((S, QB), jnp.float32),
            pltpu.VMEM((S, N_HEADS_A * QB), jnp.float32),
            pltpu.VMEM((KV_RANK, N_HEADS_A * QB), jnp.float32),
        ],
        compiler_params=_cparams(("arbitrary", "arbitrary")),
        name="dsa",
    )(cq, iwt, kidx, ckv, ckvt, w_qidx, w_uq, w_uk_h, w_uvt_h, bias_tiles)


def _layer_norm(xf, g, b):
    mu = jnp.mean(xf, axis=-1, keepdims=True)
    xc = xf - mu
    var = jnp.mean(xc * xc, axis=-1, keepdims=True)
    return xc * lax.rsqrt(var + LN_EPS) * g + b


def _rank_rows(v, n):
    ri = lax.broadcasted_iota(jnp.int32, v.shape, 0)
    rank = jnp.zeros(v.shape, jnp.float32)
    for r2 in range(n):
        row = v[r2:r2 + 1, :]
        beats = (row > v) | ((row == v) & (ri > r2))
        rank = rank + jnp.where(beats, 1.0, 0.0)
    return rank


def _pack_factor():
    return 4 // jnp.dtype(MXU_DTYPE).itemsize


def _pack_rows(x):
    if _pack_factor() == 1:
        return pltpu.bitcast(x, jnp.uint32)
    half = x.shape[1] // 2
    b = pltpu.bitcast(x.astype(MXU_DTYPE).astype(jnp.float32), jnp.uint32)
    return b[:, half:] | lax.shift_right_logical(b[:, :half], jnp.uint32(16))


def _unpack_rows(p):
    if _pack_factor() == 1:
        return [pltpu.bitcast(p, jnp.float32)]
    lo = pltpu.bitcast(lax.shift_left(p, jnp.uint32(16)), jnp.float32).astype(MXU_DTYPE)
    hi = pltpu.bitcast(p & jnp.uint32(0xFFFF0000), jnp.float32).astype(MXU_DTYPE)
    return [lo, hi]


def _mix_router_kernel(x_ref, ya_ref, yb_ref, yc_ref, wo_ref, g_ref, b_ref, wrt_ref, rb_ref, exp_ref,
                       x1_ref, x1p_ref, sel_ref, w_ref, pos_ref, cnt_ref, base_ref, *, tm):
    step = pl.program_id(0)
    f32 = jnp.float32

    @pl.when(step == 0)
    def _():
        base_ref[...] = jnp.zeros_like(base_ref)

    mix = _dot(ya_ref[...], wo_ref[0:MIX_A, :])
    mix = mix + _dot(yb_ref[...], wo_ref[MIX_A:MIX_A + CONV_CH, :])
    mix = mix + _dot(yc_ref[...], wo_ref[MIX_A + CONV_CH:, :])
    x1 = _layer_norm(ALPHA * x_ref[...] + mix, g_ref[...], b_ref[...])
    x1_ref[...] = x1
    x1p_ref[...] = _pack_rows(x1)

    lg = lax.dot_general(wrt_ref[...], x1, _NT, precision=lax.Precision.HIGHEST, preferred_element_type=f32)
    s = 1.0 / (1.0 + jnp.exp(-lg))
    sc = s + rb_ref[...]

    g3 = sc.reshape(N_GROUPS, GROUP_SIZE, tm)
    m1 = jnp.max(g3, axis=1, keepdims=True)
    is_m1 = g3 == m1
    n_m1 = jnp.sum(jnp.where(is_m1, 1.0, 0.0), axis=1, keepdims=True)
    m2 = jnp.max(jnp.where(is_m1, -jnp.inf, g3), axis=1, keepdims=True)
    gscore = (m1 + jnp.where(n_m1 > 1.0, m1, m2)).reshape(N_GROUPS, tm)
    gsel = jnp.where(_rank_rows(gscore, N_GROUPS) < float(TOPK_GROUPS), 1.0, 0.0)
    emask = _dot(exp_ref[...], gsel.astype(MXU_DTYPE)) > 0.5
    masked = jnp.where(emask, sc, -jnp.inf)
    sel = (_rank_rows(masked, N_EXPERTS) < float(TOP_K)) & emask
    self_ = jnp.where(sel, 1.0, 0.0)
    top_s = jnp.where(sel, s, 0.0)
    w = top_s / jnp.sum(top_s, axis=0, keepdims=True) * ROUTED_SCALE

    t_r = lax.broadcasted_iota(jnp.int32, (tm, tm), 0)
    t_c = lax.broadcasted_iota(jnp.int32, (tm, tm), 1)
    upper = jnp.where(t_r < t_c, 1.0, 0.0).astype(MXU_DTYPE)
    pref = _dot(self_.astype(MXU_DTYPE), upper)
    base = base_ref[...]
    sel_ref[...] = self_
    w_ref[...] = w
    pos_ref[...] = base + pref
    base = base + jnp.sum(self_, axis=1, keepdims=True)
    base_ref[...] = base
    cnt_ref[...] = jnp.broadcast_to(base, cnt_ref.shape)


def _mix_router(x2, ya, yb, yc, w_out, ln_g, ln_b, w_router_t, router_bias, tm):
    T, D = x2.shape
    E = N_EXPERTS
    expand = (jnp.arange(E)[:, None] // GROUP_SIZE == jnp.arange(N_GROUPS)[None, :]).astype(MXU_DTYPE)
    row = lambda i: (i, 0)
    col = lambda i: (0, i)
    c2 = lambda i: (0, 0)
    f32 = jnp.float32
    return pl.pallas_call(
        functools.partial(_mix_router_kernel, tm=tm),
        grid=(T // tm,),
        in_specs=[
            pl.BlockSpec((tm, D), row),
            pl.BlockSpec((tm, MIX_A), row),
            pl.BlockSpec((tm, CONV_CH), row),
            pl.BlockSpec((tm, MIX_C), row),
            pl.BlockSpec(w_out.shape, c2),
            pl.BlockSpec((1, D), c2),
            pl.BlockSpec((1, D), c2),
            pl.BlockSpec((E, D), c2),
            pl.BlockSpec((E, 1), c2),
            pl.BlockSpec((E, N_GROUPS), c2),
        ],
        out_specs=[
            pl.BlockSpec((tm, D), row),
            pl.BlockSpec((tm, D // _pack_factor()), row),
            pl.BlockSpec((E, tm), col),
            pl.BlockSpec((E, tm), col),
            pl.BlockSpec((E, tm), col),
            pl.BlockSpec((E, LANES), c2),
        ],
        out_shape=[
            jax.ShapeDtypeStruct((T, D), f32),
            jax.ShapeDtypeStruct((T, D // _pack_factor()), jnp.uint32),
            jax.ShapeDtypeStruct((E, T), f32),
            jax.ShapeDtypeStruct((E, T), f32),
            jax.ShapeDtypeStruct((E, T), f32),
            jax.ShapeDtypeStruct((E, LANES), f32),
        ],
        scratch_shapes=[pltpu.VMEM((E, 1), f32)],
        compiler_params=_cparams(("arbitrary",)),
        name="mix_router",
    )(x2, ya, yb, yc, w_out, ln_g, ln_b, w_router_t, router_bias, expand)


def _compact_kernel(sel_ref, w_ref, pos_ref, pstart_ref, low_ref, dest_ref, wk_ref):
    sel = sel_ref[...]
    on = sel > 0.5
    rank = _dot(low_ref[...], sel.astype(MXU_DTYPE))
    row = pstart_ref[...] + pos_ref[...]
    w = w_ref[...]
    dests, ws = [], []
    for k in range(TOP_K):
        m = on & (rank == float(k))
        dests.append(jnp.sum(jnp.where(m, row, 0.0), axis=0, keepdims=True))
        ws.append(jnp.sum(jnp.where(m, w, 0.0), axis=0, keepdims=True))
    dest_ref[...] = jnp.concatenate(dests, axis=0).astype(jnp.int32)
    wk_ref[...] = jnp.concatenate(ws, axis=0)


def _compact(sel_t, w_t, pos_t, pad_start, tm):
    E, T = sel_t.shape
    lower = (jnp.arange(E)[None, :] < jnp.arange(E)[:, None]).astype(MXU_DTYPE)
    col = lambda i: (0, i)
    c2 = lambda i: (0, 0)
    return pl.pallas_call(
        _compact_kernel,
        grid=(T // tm,),
        in_specs=[pl.BlockSpec((E, tm), col), pl.BlockSpec((E, tm), col), pl.BlockSpec((E, tm), col),
                  pl.BlockSpec((E, 1), c2), pl.BlockSpec((E, E), c2)],
        out_specs=[pl.BlockSpec((TOP_K, tm), col), pl.BlockSpec((TOP_K, tm), col)],
        out_shape=[jax.ShapeDtypeStruct((TOP_K, T), jnp.int32), jax.ShapeDtypeStruct((TOP_K, T), jnp.float32)],
        compiler_params=_cparams(("arbitrary",)),
        name="route_compact",
    )(sel_t, w_t, pos_t, pad_start, lower)


def _row_copy(src, s, dst, d, sem):
    return pltpu.make_async_copy(src.at[pl.ds(s, 1)], dst.at[pl.ds(d, 1)], sem)


def _dispatch_kernel(flo_ref, fhi_ref, dest_ref, x_ref, xs_hbm, zero_ref, sem, zsem, *, td):
    step = pl.program_id(0)

    @pl.when(step == 0)
    def _():
        zero_ref[...] = jnp.zeros_like(zero_ref)

        def per_expert(fn):
            def ebody(e, c):
                lax.fori_loop(flo_ref[e], fhi_ref[e], lambda r, c2: (fn(r), c2)[1], 0)
                return c
            lax.fori_loop(0, N_EXPERTS, ebody, 0)

        per_expert(lambda r: _row_copy(zero_ref, 0, xs_hbm, r, zsem).start())
        per_expert(lambda r: _row_copy(zero_ref, 0, xs_hbm, r, zsem).wait())

    def issue(r, c):
        for k in range(TOP_K):
            _row_copy(x_ref, r, xs_hbm, dest_ref[k, r], sem).start()
        return c

    def drain(r, c):
        for k in range(TOP_K):
            _row_copy(x_ref, r, xs_hbm, dest_ref[k, r], sem).wait()
        return c

    lax.fori_loop(0, td, issue, 0)
    lax.fori_loop(0, td, drain, 0)


def _dispatch(dest_t, x1p, fill_lo, fill_hi, n_rows, td):
    T, W = x1p.shape
    return pl.pallas_call(
        functools.partial(_dispatch_kernel, td=td),
        grid_spec=pltpu.PrefetchScalarGridSpec(
            num_scalar_prefetch=2,
            grid=(T // td,),
            in_specs=[
                pl.BlockSpec((TOP_K, td), lambda i, lo, hi: (0, i), memory_space=pltpu.SMEM),
                pl.BlockSpec((td, W), lambda i, lo, hi: (i, 0)),
            ],
            out_specs=pl.BlockSpec(memory_space=pl.ANY),
            scratch_shapes=[pltpu.VMEM((SUBLANES, W), x1p.dtype),
                            pltpu.SemaphoreType.DMA, pltpu.SemaphoreType.DMA],
        ),
        out_shape=jax.ShapeDtypeStruct((n_rows, W), x1p.dtype),
        compiler_params=_cparams(("arbitrary",)),
        name="dispatch",
    )(fill_lo, fill_hi, dest_t, x1p)


def _silu(g):
    return g / (1.0 + jnp.exp(-g))


def _expert_kernel(be_ref, nu_ref, xs_ref, wg_ref, wu_ref, wd_ref, ys_ref):
    @pl.when(pl.program_id(0) < nu_ref[0])
    def _():
        parts = _unpack_rows(xs_ref[...])
        dk = wg_ref.shape[1] // len(parts)

        def proj(w_ref):
            acc = _dot(parts[0], w_ref[0, 0:dk, :])
            for n in range(1, len(parts)):
                acc = acc + _dot(parts[n], w_ref[0, n * dk:(n + 1) * dk, :])
            return acc

        a = (_silu(proj(wg_ref)) * proj(wu_ref)).astype(MXU_DTYPE)
        ys_ref[...] = _dot(a, wd_ref[0])


def _experts(xs, block_e, n_used, w_gate, w_up, w_down):
    n_rows, W = xs.shape
    D = w_gate.shape[1]
    n_blocks = n_rows // MOE_BLOCK
    blk = lambda i, be, nu: (jnp.minimum(i, nu[0] - 1), 0)
    wsel = lambda i, be, nu: (be[i], 0, 0)
    return pl.pallas_call(
        _expert_kernel,
        grid_spec=pltpu.PrefetchScalarGridSpec(
            num_scalar_prefetch=2,
            grid=(n_blocks,),
            in_specs=[
                pl.BlockSpec((MOE_BLOCK, W), blk),
                pl.BlockSpec((1, D, D_EXPERT), wsel),
                pl.BlockSpec((1, D, D_EXPERT), wsel),
                pl.BlockSpec((1, D_EXPERT, D), wsel),
            ],
            out_specs=pl.BlockSpec((MOE_BLOCK, D), blk),
        ),
        out_shape=jax.ShapeDtypeStruct((n_rows, D), jnp.float32),
        compiler_params=_cparams(("arbitrary",)),
        name="experts",
    )(block_e, n_used, xs, w_gate, w_up, w_down)


def _combine_kernel(dest_ref, wk_ref, x1_ref, ys_hbm, wsg_ref, wsu_ref, wsd_ref, g_ref, b_ref,
                    o_ref, buf_ref, sem, *, tc):
    def issue(r, c):
        for k in range(TOP_K):
            _row_copy(ys_hbm, dest_ref[k, r], buf_ref.at[k], r, sem).start()
        return c

    def drain(r, c):
        for k in range(TOP_K):
            _row_copy(ys_hbm, dest_ref[k, r], buf_ref.at[k], r, sem).wait()
        return c

    lax.fori_loop(0, tc, issue, 0)
    x1 = x1_ref[...]
    xb = x1.astype(MXU_DTYPE)
    a = (_silu(_dot(xb, wsg_ref[...])) * _dot(xb, wsu_ref[...])).astype(MXU_DTYPE)
    shared = _dot(a, wsd_ref[...])
    lax.fori_loop(0, tc, drain, 0)
    wk = wk_ref[...]
    routed = wk[:, 0:1] * buf_ref[0]
    for k in range(1, TOP_K):
        routed = routed + wk[:, k:k + 1] * buf_ref[k]
    o_ref[...] = _layer_norm(ALPHA * x1 + (routed + shared), g_ref[...], b_ref[...])


def _combine(dest_t, wk, x1, ys, w_sg, w_su, w_sd, ln_g, ln_b, tc):
    T, D = x1.shape
    row = lambda i: (i, 0)
    c2 = lambda i: (0, 0)
    return pl.pallas_call(
        functools.partial(_combine_kernel, tc=tc),
        grid=(T // tc,),
        in_specs=[
            pl.BlockSpec((TOP_K, tc), lambda i: (0, i), memory_space=pltpu.SMEM),
            pl.BlockSpec((tc, TOP_K), row),
            pl.BlockSpec((tc, D), row),
            pl.BlockSpec(memory_space=pl.ANY),
            pl.BlockSpec(w_sg.shape, c2),
            pl.BlockSpec(w_su.shape, c2),
            pl.BlockSpec(w_sd.shape, c2),
            pl.BlockSpec((1, D), c2),
            pl.BlockSpec((1, D), c2),
        ],
        out_specs=pl.BlockSpec((tc, D), row),
        out_shape=jax.ShapeDtypeStruct((T, D), jnp.float32),
        scratch_shapes=[pltpu.VMEM((TOP_K, tc, D), jnp.float32), pltpu.SemaphoreType.DMA],
        compiler_params=_cparams(("arbitrary",)),
        name="combine",
    )(dest_t, wk, x1, ys, w_sg, w_su, w_sd, ln_g, ln_b)


def _split_w_in(w_in):
    bf = MXU_DTYPE
    o_kv = Q_RANK
    o_ki = o_kv + KV_RANK
    o_iw = o_ki + IDX_DIM
    o_rest = o_iw + N_IDX_HEADS
    w_main = jnp.concatenate([w_in[:, :o_ki], w_in[:, o_rest:]], axis=1).astype(bf)
    w_small = jnp.pad(w_in[:, o_ki:o_rest], ((0, 0), (0, LANES - IDX_DIM - N_IDX_HEADS))).astype(bf)
    return w_main, w_small


def _stages(x, mem, w_in, q_norm_g, kv_norm_g, w_uq, w_uk, w_uv, w_qidx, rel_bias, conv_w, w_mem_k, w_mem_v, w_out, ln1_g, ln1_b, w_router, router_bias, w_e_gate, w_e_up, w_e_down, w_s_gate, w_s_up, w_s_down, ln2_g, ln2_b, upto=None):
    B, S, D = x.shape
    T = B * S
    bf = MXU_DTYPE
    l = 0
    res = {}
    x2 = x.reshape(T, D)
    w_main, w_small = _split_w_in(w_in[l])
    cq, ckv, ckvt, kidx, iwt, yb, yc = _proj(
        x2, mem, w_main, w_small, q_norm_g[l].reshape(1, -1), kv_norm_g[l].reshape(1, -1), conv_w[l],
        w_mem_k[l].astype(bf), w_mem_v[l].astype(bf), B, S, tm=min(512, S))
    res.update(c_q=cq, c_kv=ckv, k_idx=kidx, y_b=yb, y_c=yc,
               idx_w=jnp.swapaxes(iwt, 1, 2) / (N_IDX_HEADS ** -0.5 * IDX_DIM ** -0.5))
    if upto == "proj":
        return res
    bias_t = _bias_tiles(rel_bias)
    ya = _dsa(cq, iwt, kidx, ckv, ckvt,
              w_qidx[l].reshape(Q_RANK, -1).astype(bf), w_uq[l].reshape(Q_RANK, -1).astype(bf),
              jnp.transpose(w_uk[l], (1, 0, 2)).astype(bf), jnp.transpose(w_uv[l], (1, 2, 0)).astype(bf),
              bias_t, B, S)
    res.update(y_a=ya)
    if upto == "dsa":
        return res

    x1, x1p, sel_t, w_t, pos_t, cnt = _mix_router(
        x2, ya, yb, yc, w_out[l].astype(bf), ln1_g[l].reshape(1, -1), ln1_b[l].reshape(1, -1),
        w_router[l].T, router_bias[l].reshape(-1, 1), tm=min(512, T))
    res.update(x1=x1)

    counts = cnt[:, 0].astype(jnp.int32)
    padded = (counts + MOE_BLOCK - 1) // MOE_BLOCK * MOE_BLOCK
    pad_end = jnp.cumsum(padded)
    pad_start = pad_end - padded
    n_blocks = -(-(T * TOP_K) // MOE_BLOCK) + N_EXPERTS
    n_rows = n_blocks * MOE_BLOCK
    block_start = jnp.arange(n_blocks, dtype=jnp.int32) * MOE_BLOCK
    block_e = jnp.minimum(jnp.sum((pad_end[None, :] <= block_start[:, None]).astype(jnp.int32), axis=1),
                          N_EXPERTS - 1)
    n_used = (pad_end[-1:] // MOE_BLOCK).astype(jnp.int32)

    dest_t, wk_t = _compact(sel_t, w_t, pos_t, pad_start.astype(jnp.float32).reshape(-1, 1), tm=min(512, T))
    xs = _dispatch(dest_t, x1p, (pad_start + counts).astype(jnp.int32), pad_end.astype(jnp.int32), n_rows,
                   td=min(256, T))
    ys = _experts(xs, block_e, n_used, w_e_gate[l].astype(bf), w_e_up[l].astype(bf), w_e_down[l].astype(bf))
    out = _combine(dest_t, wk_t.T, x1, ys, w_s_gate[l].astype(bf), w_s_up[l].astype(bf), w_s_down[l].astype(bf),
                   ln2_g[l].reshape(1, -1), ln2_b[l].reshape(1, -1), tc=min(128, T))
    res.update(out=out.reshape(B, S, D))
    return res


def kernel(x, mem, w_in, q_norm_g, kv_norm_g, w_uq, w_uk, w_uv, w_qidx, rel_bias, conv_w, w_mem_k, w_mem_v, w_out, ln1_g, ln1_b, w_router, router_bias, w_e_gate, w_e_up, w_e_down, w_s_gate, w_s_up, w_s_down, ln2_g, ln2_b):
    return _stages(x, mem, w_in, q_norm_g, kv_norm_g, w_uq, w_uk, w_uv, w_qidx, rel_bias, conv_w, w_mem_k, w_mem_v, w_out, ln1_g, ln1_b, w_router, router_bias, w_e_gate, w_e_up, w_e_down, w_s_gate, w_s_up, w_s_down, ln2_g, ln2_b)["out"]
```

```python
import functools
import math

import jax
import jax.numpy as jnp
from jax import lax
from jax.experimental import pallas as pl
from jax.experimental.pallas import tpu as pltpu

N_HEADS_A = 8
HEAD_DIM = 64
Q_RANK = 256
KV_RANK = 128
N_IDX_HEADS = 8
IDX_DIM = 64
TOPK_MAX = 256
REL_BUCKETS = 32
REL_MAX_DIST = 128
CONV_CH = 256
CONV_WIDTH = 3
N_MEM_HEADS = 4
MIX_A = N_HEADS_A * HEAD_DIM
MIX_C = N_MEM_HEADS * HEAD_DIM
N_EXPERTS = 64
N_GROUPS = 8
GROUP_SIZE = N_EXPERTS // N_GROUPS
TOPK_GROUPS = 4
TOP_K = 8
D_EXPERT = 256
ROUTED_SCALE = 2.5
MOE_BLOCK = 256
DEPTH = 1
ALPHA = (2.0 * DEPTH) ** 0.25
LN_EPS = 1e-5
RMS_EPS = 1e-6

LANES = 128
SUBLANES = 8
QB = 128
F32_LOWEST = -3.4028234663852886e38
VMEM_LIMIT = 56 * 1024 * 1024
MXU_DTYPE = jnp.bfloat16
ROW_BLOCK = 512

_NT = (((1,), (1,)), ((), ()))


def _dot(a, b):
    return jnp.dot(a, b, preferred_element_type=jnp.float32)


def _dot_nt(a, b):
    return lax.dot_general(a, b, _NT, preferred_element_type=jnp.float32)


def _cparams(sem):
    return pltpu.CompilerParams(dimension_semantics=sem, vmem_limit_bytes=VMEM_LIMIT)


def _bias_kernel(rb_ref, o_ref):
    s = lax.broadcasted_iota(jnp.int32, (QB, QB), 0)
    t = lax.broadcasted_iota(jnp.int32, (QB, QB), 1)
    max_exact = REL_BUCKETS // 2
    for tile in range(3):
        n = jnp.maximum(t - s + (2 - tile) * QB, 0)
        nf = jnp.maximum(n.astype(jnp.float32), 1.0)
        large = max_exact + (jnp.log(nf / max_exact) / math.log(REL_MAX_DIST / max_exact)
                             * (REL_BUCKETS - max_exact)).astype(jnp.int32)
        large = jnp.minimum(large, REL_BUCKETS - 1)
        bucket = jnp.where(n < max_exact, n, large)
        for h in range(N_HEADS_A):
            acc = jnp.zeros((QB, QB), jnp.float32)
            for b in range(REL_BUCKETS):
                acc = jnp.where(bucket == b, rb_ref[b, h], acc)
            o_ref[tile, h] = acc


def _bias_tiles(rel_bias):
    return pl.pallas_call(
        _bias_kernel,
        in_specs=[pl.BlockSpec(memory_space=pltpu.SMEM)],
        out_specs=pl.BlockSpec(memory_space=pltpu.VMEM),
        out_shape=jax.ShapeDtypeStruct((3, N_HEADS_A, QB, QB), jnp.float32),
        name="bias_tiles",
    )(rel_bias)


_MAIN_COLS = Q_RANK + KV_RANK + 3 * CONV_CH + MIX_C


def _proj_kernel(x_ref, mem_ref, wm_ref, ws_ref, qg_ref, kvg_ref, cw_ref, wmk_ref, wmv_ref,
                 cq_ref, ckv_ref, ckvt_ref, kidx_ref, iwt_ref, yb_ref, yc_ref,
                 carry_ref, mk_ref, mv_ref, *, tm):
    si = pl.program_id(1)

    @pl.when(si == 0)
    def _():
        carry_ref[...] = jnp.zeros_like(carry_ref)
        mb = mem_ref[0].astype(MXU_DTYPE)
        mk_ref[...] = _dot(mb, wmk_ref[...]).astype(MXU_DTYPE)
        mv_ref[...] = _dot(mb, wmv_ref[...]).astype(MXU_DTYPE)

    xb = x_ref[...].astype(MXU_DTYPE)
    p = _dot(xb, wm_ref[...])
    small = _dot(xb, ws_ref[...])

    o = 0
    cq = p[:, o:o + Q_RANK]; o += Q_RANK
    ckv = p[:, o:o + KV_RANK]; o += KV_RANK
    g_b = p[:, o:o + CONV_CH]; o += CONV_CH
    g_c = p[:, o:o + CONV_CH]; o += CONV_CH
    h_c = p[:, o:o + CONV_CH]; o += CONV_CH
    q_mem = p[:, o:o + MIX_C]

    cq = cq * lax.rsqrt(jnp.mean(cq * cq, axis=-1, keepdims=True) + RMS_EPS) * qg_ref[...]
    ckv = ckv * lax.rsqrt(jnp.mean(ckv * ckv, axis=-1, keepdims=True) + RMS_EPS) * kvg_ref[...]
    cq_ref[...] = cq.astype(MXU_DTYPE)
    ckv_b = ckv.astype(MXU_DTYPE)
    ckv_ref[...] = ckv_b
    ckvt_ref[0] = ckv.T.astype(MXU_DTYPE)

    kidx_ref[...] = small[:, :IDX_DIM].astype(MXU_DTYPE)
    small_t = small.T
    iwt_ref[0] = small_t[IDX_DIM:IDX_DIM + N_IDX_HEADS, :] * (N_IDX_HEADS ** -0.5 * IDX_DIM ** -0.5)

    u = g_c * h_c
    rows = lax.broadcasted_iota(jnp.int32, (tm, 1), 0)
    c6 = carry_ref[SUBLANES - 2:SUBLANES - 1, :]
    c7 = carry_ref[SUBLANES - 1:SUBLANES, :]
    u1 = jnp.where(rows == 0, c7, pltpu.roll(u, 1, 0))
    u2 = jnp.where(rows == 0, c6, jnp.where(rows == 1, c7, pltpu.roll(u, 2, 0)))
    y = cw_ref[0:1, :] * u2
    y = y + cw_ref[1:2, :] * u1
    y = y + cw_ref[2:3, :] * u
    yb_ref[...] = (g_b * y).astype(MXU_DTYPE)
    carry_ref[...] = u[tm - SUBLANES:, :]

    qm = q_mem.astype(MXU_DTYPE)
    outs = []
    for h in range(N_MEM_HEADS):
        sl = slice(h * HEAD_DIM, (h + 1) * HEAD_DIM)
        lg = _dot_nt(qm[:, sl], mk_ref[:, sl]) * (HEAD_DIM ** -0.5)
        lg = lg - jnp.max(lg, axis=-1, keepdims=True)
        e = jnp.exp(lg)
        pr = e / jnp.sum(e, axis=-1, keepdims=True)
        outs.append(_dot(pr.astype(MXU_DTYPE), mv_ref[:, sl]))
    yc_ref[...] = jnp.concatenate(outs, axis=-1).astype(MXU_DTYPE)


def _proj(x2, mem, w_main, w_small, q_g, kv_g, conv_w, w_mk, w_mv, B, S, tm):
    T, D = x2.shape
    n_mem = mem.shape[1]
    ns = S // tm
    row = lambda b, s: (b * ns + s, 0)
    const2 = lambda b, s: (0, 0)
    bf = MXU_DTYPE
    return pl.pallas_call(
        functools.partial(_proj_kernel, tm=tm),
        grid=(B, ns),
        in_specs=[
            pl.BlockSpec((tm, D), row),
            pl.BlockSpec((1, n_mem, D), lambda b, s: (b, 0, 0)),
            pl.BlockSpec(w_main.shape, const2),
            pl.BlockSpec(w_small.shape, const2),
            pl.BlockSpec(q_g.shape, const2),
            pl.BlockSpec(kv_g.shape, const2),
            pl.BlockSpec(conv_w.shape, const2),
            pl.BlockSpec(w_mk.shape, const2),
            pl.BlockSpec(w_mv.shape, const2),
        ],
        out_specs=[
            pl.BlockSpec((tm, Q_RANK), row),
            pl.BlockSpec((tm, KV_RANK), row),
            pl.BlockSpec((1, KV_RANK, tm), lambda b, s: (b, 0, s)),
            pl.BlockSpec((tm, IDX_DIM), row),
            pl.BlockSpec((1, N_IDX_HEADS, tm), lambda b, s: (b, 0, s)),
            pl.BlockSpec((tm, CONV_CH), row),
            pl.BlockSpec((tm, MIX_C), row),
        ],
        out_shape=[
            jax.ShapeDtypeStruct((T, Q_RANK), bf),
            jax.ShapeDtypeStruct((T, KV_RANK), bf),
            jax.ShapeDtypeStruct((B, KV_RANK, S), bf),
            jax.ShapeDtypeStruct((T, IDX_DIM), bf),
            jax.ShapeDtypeStruct((B, N_IDX_HEADS, S), jnp.float32),
            jax.ShapeDtypeStruct((T, CONV_CH), bf),
            jax.ShapeDtypeStruct((T, MIX_C), bf),
        ],
        scratch_shapes=[
            pltpu.VMEM((SUBLANES, CONV_CH), jnp.float32),
            pltpu.VMEM((n_mem, MIX_C), bf),
            pltpu.VMEM((n_mem, MIX_C), bf),
        ],
        compiler_params=_cparams(("arbitrary", "arbitrary")),
        name="proj",
    )(x2, mem, w_main, w_small, q_g, kv_g, conv_w, w_mk, w_mv)


def _key_to_f32(key):
    bits = jnp.where(key < 0, key ^ jnp.int32(0x7FFFFFFF), key)
    return pltpu.bitcast(bits, jnp.float32)


def _colsum8(v):
    return jnp.sum(v.reshape(QB // SUBLANES, SUBLANES, QB), axis=0)


def _colmax8(v):
    return jnp.max(v.reshape(QB // SUBLANES, SUBLANES, QB), axis=0)


UNROLL = 4


def _dsa_kernel(cq_ref, iwt_ref, kidx_ref, ckv_ref, ckvt_ref, wqi_ref, wuq_ref, wuk_ref, wuvt_ref,
                bias_ref, o_ref, qidx_ref, qlat_ref, score_ref, mask_ref, logit_ref, acc_ref,
                *, k_sel, idx_bits):
    i = pl.program_id(1)
    f32 = jnp.float32
    bf = MXU_DTYPE
    n_blocks = i + 1
    s_loc = lax.broadcasted_iota(jnp.int32, (QB, QB), 0)
    t_glob = i * QB + lax.broadcasted_iota(jnp.int32, (QB, QB), 1)

    def blk(jb):
        return pl.multiple_of(jb * QB, QB)

    def block_loop(fn, init):
        n_main = n_blocks // UNROLL
        c = lax.fori_loop(0, n_main, lambda it, c: fn(it * UNROLL, UNROLL, c), init)
        return lax.fori_loop(n_main * UNROLL, n_blocks, lambda jb, c: fn(jb, 1, c), c)

    cq = cq_ref[...]
    q_all = _dot(cq, wuq_ref[...]).astype(bf)
    for h in range(N_HEADS_A):
        qidx_ref[h * QB:(h + 1) * QB, :] = _dot(cq, wqi_ref[:, h * IDX_DIM:(h + 1) * IDX_DIM]).astype(bf)
        qlat_ref[h * QB:(h + 1) * QB, :] = (
            _dot_nt(q_all[:, h * HEAD_DIM:(h + 1) * HEAD_DIM], wuk_ref[h]) * (HEAD_DIM ** -0.5)).astype(bf)
    iw = iwt_ref[0]

    def score_body(jb0, nb, c):
        for sb in range(nb):
            off = blk(jb0 + sb)
            d_all = _dot_nt(kidx_ref[pl.ds(off, QB), :], qidx_ref[...])
            acc = jnp.maximum(d_all[:, 0:QB], 0.0) * iw[0:1, :]
            for h in range(1, N_IDX_HEADS):
                acc = acc + jnp.maximum(d_all[:, h * QB:(h + 1) * QB], 0.0) * iw[h:h + 1, :]
            score_ref[pl.ds(off, QB), :] = jnp.where(s_loc + off <= t_glob, acc + 0.0, F32_LOWEST)
        return c

    block_loop(score_body, 0)

    def count_where(pred):
        def body(jb0, nb, acc):
            for sb in range(nb):
                off = blk(jb0 + sb)
                acc = acc + _colsum8(jnp.where(pred(score_ref[pl.ds(off, QB), :], off), 1.0, 0.0))
            return acc
        acc = block_loop(body, jnp.zeros((SUBLANES, QB), f32))
        return jnp.sum(acc, axis=0, keepdims=True)

    kf = float(k_sel)

    def search():
        c0 = count_where(lambda sc, off: sc >= 0.0)
        cand0 = jnp.where(c0 >= kf, jnp.int32(0), jnp.int32(-2 ** 31))

        def bit_body(it, cand):
            trial = cand + lax.shift_left(jnp.int32(1), 30 - it)
            tf = _key_to_f32(trial)
            cnt = count_where(lambda sc, off: sc >= tf)
            return jnp.where(cnt >= kf, trial, cand)

        cand = lax.fori_loop(0, 31, bit_body, cand0)
        thr = _key_to_f32(cand)
        n_gt = count_where(lambda sc, off: sc > thr)
        n_eq = count_where(lambda sc, off: sc == thr)
        need = kf - n_gt

        def tie_search():
            def tbody(it, xcut):
                trial = xcut + lax.shift_left(jnp.int32(1), idx_bits - 1 - it)
                cnt = count_where(lambda sc, off: (sc == thr) & (s_loc + off < trial))
                return jnp.where(cnt < need, trial, xcut)
            return lax.fori_loop(0, idx_bits, tbody, jnp.zeros((1, QB), jnp.int32))

        any_extra = jnp.max(n_eq - need) > 0.0
        xcut = lax.cond(any_extra, tie_search, lambda: jnp.full((1, QB), 2 ** idx_bits - 1, jnp.int32))
        return thr, xcut

    def no_search():
        return jnp.full((1, QB), F32_LOWEST, f32), jnp.full((1, QB), 2 ** idx_bits - 1, jnp.int32)

    thr, xcut = lax.cond((i + 1) * QB > k_sel, search, no_search)

    def mask_body(jb0, nb, c):
        for sb in range(nb):
            off = blk(jb0 + sb)
            sc = score_ref[pl.ds(off, QB), :]
            s_glob = s_loc + off
            keep = ((sc > thr) | ((sc == thr) & (s_glob <= xcut))) & (s_glob <= t_glob)
            mask_ref[pl.ds(off, QB), :] = jnp.where(keep, 0.0, -jnp.inf)
        return c

    block_loop(mask_body, 0)

    def p1_body(jb0, nb, m8):
        m8 = list(m8)
        for sb in range(nb):
            off = blk(jb0 + sb)
            lg = _dot_nt(ckv_ref[pl.ds(off, QB), :], qlat_ref[...])
            msk = mask_ref[pl.ds(off, QB), :]
            bsel = jnp.clip(jb0 + sb - i + 2, 0, 2)
            for h in range(N_HEADS_A):
                lgh = lg[:, h * QB:(h + 1) * QB] + bias_ref[bsel, h] + msk
                logit_ref[pl.ds(off, QB), h * QB:(h + 1) * QB] = lgh
                m8[h] = jnp.maximum(m8[h], _colmax8(lgh))
        return tuple(m8)

    m8 = block_loop(p1_body, tuple(jnp.full((SUBLANES, QB), -jnp.inf, f32) for _ in range(N_HEADS_A)))
    m_row = [jnp.max(m, axis=0, keepdims=True) for m in m8]

    acc_ref[...] = jnp.zeros_like(acc_ref)

    def p2_body(jb0, nb, l8):
        l8 = list(l8)
        off = blk(jb0)
        rows = nb * QB
        ps = []
        for h in range(N_HEADS_A):
            p = jnp.exp(logit_ref[pl.ds(off, rows), h * QB:(h + 1) * QB] - m_row[h])
            l8[h] = l8[h] + jnp.sum(p.reshape(rows // SUBLANES, SUBLANES, QB), axis=0)
            ps.append(p.astype(bf))
        acc_ref[...] += _dot(ckvt_ref[0, :, pl.ds(off, rows)], jnp.concatenate(ps, axis=1))
        return tuple(l8)

    l8 = block_loop(p2_body, tuple(jnp.zeros((SUBLANES, QB), f32) for _ in range(N_HEADS_A)))

    outs = []
    for h in range(N_HEADS_A):
        l_row = jnp.sum(l8[h], axis=0, keepdims=True)
        o_lat_t = (acc_ref[:, h * QB:(h + 1) * QB] / l_row).astype(bf)
        outs.append(_dot(wuvt_ref[h], o_lat_t))
    o_ref[...] = jnp.concatenate(outs, axis=0).T.astype(o_ref.dtype)


def _dsa(cq, iwt, kidx, ckv, ckvt, w_qidx, w_uq, w_uk_h, w_uvt_h, bias_tiles, B, S):
    T = cq.shape[0]
    assert S % QB == 0 and QB >= REL_MAX_DIST
    nq = S // QB
    k_sel = min(TOPK_MAX, S // 4)
    idx_bits = max(1, (S - 1).bit_length())
    c2 = lambda b, i: (0, 0)
    c3 = lambda b, i: (0, 0, 0)
    return pl.pallas_call(
        functools.partial(_dsa_kernel, k_sel=k_sel, idx_bits=idx_bits),
        grid=(B, nq),
        in_specs=[
            pl.BlockSpec((QB, Q_RANK), lambda b, i: (b * nq + i, 0)),
            pl.BlockSpec((1, N_IDX_HEADS, QB), lambda b, i: (b, 0, i)),
            pl.BlockSpec((S, IDX_DIM), lambda b, i: (b, 0)),
            pl.BlockSpec((S, KV_RANK), lambda b, i: (b, 0)),
            pl.BlockSpec((1, KV_RANK, S), lambda b, i: (b, 0, 0)),
            pl.BlockSpec(w_qidx.shape, c2),
            pl.BlockSpec(w_uq.shape, c2),
            pl.BlockSpec(w_uk_h.shape, c3),
            pl.BlockSpec(w_uvt_h.shape, c3),
            pl.BlockSpec(bias_tiles.shape, lambda b, i: (0, 0, 0, 0)),
        ],
        out_specs=pl.BlockSpec((QB, MIX_A), lambda b, i: (b * nq + i, 0)),
        out_shape=jax.ShapeDtypeStruct((T, MIX_A), MXU_DTYPE),
        scratch_shapes=[
            pltpu.VMEM((N_IDX_HEADS * QB, IDX_DIM), MXU_DTYPE),
            pltpu.VMEM((N_HEADS_A * QB, KV_RANK), MXU_DTYPE),
            pltpu.VMEM((S, QB), jnp.float32),
            pltpu.VMEM((S, QB), jnp.float32),
            pltpu.VMEM((S, N_HEADS_A * QB), jnp.float32),
            pltpu.VMEM((KV_RANK, N_HEADS_A * QB), jnp.float32),
        ],
        compiler_params=_cparams(("arbitrary", "arbitrary")),
        name="dsa",
    )(cq, iwt, kidx, ckv, ckvt, w_qidx, w_uq, w_uk_h, w_uvt_h, bias_tiles)


def _layer_norm(xf, g, b):
    mu = jnp.mean(xf, axis=-1, keepdims=True)
    xc = xf - mu
    var = jnp.mean(xc * xc, axis=-1, keepdims=True)
    return xc * lax.rsqrt(var + LN_EPS) * g + b


def _rank_rows(v, n):
    ri = lax.broadcasted_iota(jnp.int32, v.shape, 0)
    rank = jnp.zeros(v.shape, jnp.float32)
    for r2 in range(n):
        row = v[r2:r2 + 1, :]
        beats = (row > v) | ((row == v) & (ri > r2))
        rank = rank + jnp.where(beats, 1.0, 0.0)
    return rank


def _pack_factor():
    return 4 // jnp.dtype(MXU_DTYPE).itemsize


def _pack_rows(x):
    if _pack_factor() == 1:
        return pltpu.bitcast(x, jnp.uint32)
    half = x.shape[1] // 2
    b = pltpu.bitcast(x.astype(MXU_DTYPE).astype(jnp.float32), jnp.uint32)
    return b[:, half:] | lax.shift_right_logical(b[:, :half], jnp.uint32(16))


def _unpack_rows_f32(p):
    if _pack_factor() == 1:
        return [pltpu.bitcast(p, jnp.float32)]
    lo = pltpu.bitcast(lax.shift_left(p, jnp.uint32(16)), jnp.float32)
    hi = pltpu.bitcast(p & jnp.uint32(0xFFFF0000), jnp.float32)
    return [lo, hi]


def _unpack_rows(p):
    return [v.astype(MXU_DTYPE) for v in _unpack_rows_f32(p)]


def _mix_router_kernel(x_ref, ya_ref, yb_ref, yc_ref, wo_ref, g_ref, b_ref, wrt_ref, rb_ref, exp_ref,
                       x1_ref, x1p_ref, sel_ref, w_ref, pos_ref, cnt_ref, base_ref, *, tm):
    step = pl.program_id(0)
    f32 = jnp.float32

    @pl.when(step == 0)
    def _():
        base_ref[...] = jnp.zeros_like(base_ref)

    mix = _dot(ya_ref[...], wo_ref[0:MIX_A, :])
    mix = mix + _dot(yb_ref[...], wo_ref[MIX_A:MIX_A + CONV_CH, :])
    mix = mix + _dot(yc_ref[...], wo_ref[MIX_A + CONV_CH:, :])
    x1 = _layer_norm(ALPHA * x_ref[...] + mix, g_ref[...], b_ref[...])
    x1_ref[...] = x1
    x1p_ref[...] = _pack_rows(x1)

    lg = lax.dot_general(wrt_ref[...], x1, _NT, precision=lax.Precision.HIGHEST, preferred_element_type=f32)
    s = 1.0 / (1.0 + jnp.exp(-lg))
    sc = s + rb_ref[...]

    g3 = sc.reshape(N_GROUPS, GROUP_SIZE, tm)
    m1 = jnp.max(g3, axis=1, keepdims=True)
    is_m1 = g3 == m1
    n_m1 = jnp.sum(jnp.where(is_m1, 1.0, 0.0), axis=1, keepdims=True)
    m2 = jnp.max(jnp.where(is_m1, -jnp.inf, g3), axis=1, keepdims=True)
    gscore = (m1 + jnp.where(n_m1 > 1.0, m1, m2)).reshape(N_GROUPS, tm)
    gsel = jnp.where(_rank_rows(gscore, N_GROUPS) < float(TOPK_GROUPS), 1.0, 0.0)
    emask = _dot(exp_ref[...], gsel.astype(MXU_DTYPE)) > 0.5
    masked = jnp.where(emask, sc, -jnp.inf)
    sel = (_rank_rows(masked, N_EXPERTS) < float(TOP_K)) & emask
    self_ = jnp.where(sel, 1.0, 0.0)
    top_s = jnp.where(sel, s, 0.0)
    w = top_s / jnp.sum(top_s, axis=0, keepdims=True) * ROUTED_SCALE

    t_r = lax.broadcasted_iota(jnp.int32, (tm, tm), 0)
    t_c = lax.broadcasted_iota(jnp.int32, (tm, tm), 1)
    upper = jnp.where(t_r < t_c, 1.0, 0.0).astype(MXU_DTYPE)
    pref = _dot(self_.astype(MXU_DTYPE), upper)
    base = base_ref[...]
    sel_ref[...] = self_
    w_ref[...] = w
    pos_ref[...] = base + pref
    base = base + jnp.sum(self_, axis=1, keepdims=True)
    base_ref[...] = base
    cnt_ref[...] = jnp.broadcast_to(base, cnt_ref.shape)


def _mix_router(x2, ya, yb, yc, w_out, ln_g, ln_b, w_router_t, router_bias, tm):
    T, D = x2.shape
    E = N_EXPERTS
    expand = (jnp.arange(E)[:, None] // GROUP_SIZE == jnp.arange(N_GROUPS)[None, :]).astype(MXU_DTYPE)
    row = lambda i: (i, 0)
    col = lambda i: (0, i)
    c2 = lambda i: (0, 0)
    f32 = jnp.float32
    return pl.pallas_call(
        functools.partial(_mix_router_kernel, tm=tm),
        grid=(T // tm,),
        in_specs=[
            pl.BlockSpec((tm, D), row),
            pl.BlockSpec((tm, MIX_A), row),
            pl.BlockSpec((tm, CONV_CH), row),
            pl.BlockSpec((tm, MIX_C), row),
            pl.BlockSpec(w_out.shape, c2),
            pl.BlockSpec((1, D), c2),
            pl.BlockSpec((1, D), c2),
            pl.BlockSpec((E, D), c2),
            pl.BlockSpec((E, 1), c2),
            pl.BlockSpec((E, N_GROUPS), c2),
        ],
        out_specs=[
            pl.BlockSpec((tm, D), row),
            pl.BlockSpec((tm, D // _pack_factor()), row),
            pl.BlockSpec((E, tm), col),
            pl.BlockSpec((E, tm), col),
            pl.BlockSpec((E, tm), col),
            pl.BlockSpec((E, LANES), c2),
        ],
        out_shape=[
            jax.ShapeDtypeStruct((T, D), f32),
            jax.ShapeDtypeStruct((T, D // _pack_factor()), jnp.uint32),
            jax.ShapeDtypeStruct((E, T), f32),
            jax.ShapeDtypeStruct((E, T), f32),
            jax.ShapeDtypeStruct((E, T), f32),
            jax.ShapeDtypeStruct((E, LANES), f32),
        ],
        scratch_shapes=[pltpu.VMEM((E, 1), f32)],
        compiler_params=_cparams(("arbitrary",)),
        name="mix_router",
    )(x2, ya, yb, yc, w_out, ln_g, ln_b, w_router_t, router_bias, expand)


def _compact_kernel(sel_ref, w_ref, pos_ref, pstart_ref, low_ref, dest_ref, wk_ref):
    sel = sel_ref[...]
    on = sel > 0.5
    rank = _dot(low_ref[...], sel.astype(MXU_DTYPE))
    row = pstart_ref[...] + pos_ref[...]
    w = w_ref[...]
    dests, ws = [], []
    for k in range(TOP_K):
        m = on & (rank == float(k))
        dests.append(jnp.sum(jnp.where(m, row, 0.0), axis=0, keepdims=True))
        ws.append(jnp.sum(jnp.where(m, w, 0.0), axis=0, keepdims=True))
    dest_ref[...] = jnp.concatenate(dests, axis=0).astype(jnp.int32)
    wk_ref[...] = jnp.concatenate(ws, axis=0)


def _compact(sel_t, w_t, pos_t, pad_start, tm):
    E, T = sel_t.shape
    lower = (jnp.arange(E)[None, :] < jnp.arange(E)[:, None]).astype(MXU_DTYPE)
    col = lambda i: (0, i)
    c2 = lambda i: (0, 0)
    return pl.pallas_call(
        _compact_kernel,
        grid=(T // tm,),
        in_specs=[pl.BlockSpec((E, tm), col), pl.BlockSpec((E, tm), col), pl.BlockSpec((E, tm), col),
                  pl.BlockSpec((E, 1), c2), pl.BlockSpec((E, E), c2)],
        out_specs=[pl.BlockSpec((TOP_K, tm), col), pl.BlockSpec((TOP_K, tm), col)],
        out_shape=[jax.ShapeDtypeStruct((TOP_K, T), jnp.int32), jax.ShapeDtypeStruct((TOP_K, T), jnp.float32)],
        compiler_params=_cparams(("arbitrary",)),
        name="route_compact",
    )(sel_t, w_t, pos_t, pad_start, lower)


def _row_copy(src, s, dst, d, sem):
    return pltpu.make_async_copy(src.at[pl.ds(s, 1)], dst.at[pl.ds(d, 1)], sem)


def _dispatch_kernel(flo_ref, fhi_ref, dest_ref, x_ref, xs_hbm, zero_ref, sem, zsem, *, td):
    step = pl.program_id(0)

    @pl.when(step == 0)
    def _():
        zero_ref[...] = jnp.zeros_like(zero_ref)

        def per_expert(fn):
            def ebody(e, c):
                lax.fori_loop(flo_ref[e], fhi_ref[e], lambda r, c2: (fn(r), c2)[1], 0)
                return c
            lax.fori_loop(0, N_EXPERTS, ebody, 0)

        per_expert(lambda r: _row_copy(zero_ref, 0, xs_hbm, r, zsem).start())
        per_expert(lambda r: _row_copy(zero_ref, 0, xs_hbm, r, zsem).wait())

    def issue(r, c):
        for k in range(TOP_K):
            _row_copy(x_ref, r, xs_hbm, dest_ref[k, r], sem).start()
        return c

    def drain(r, c):
        for k in range(TOP_K):
            _row_copy(x_ref, r, xs_hbm, dest_ref[k, r], sem).wait()
        return c

    lax.fori_loop(0, td, issue, 0)
    lax.fori_loop(0, td, drain, 0)


def _dispatch(dest_t, x1p, fill_lo, fill_hi, n_rows, td):
    T, W = x1p.shape
    return pl.pallas_call(
        functools.partial(_dispatch_kernel, td=td),
        grid_spec=pltpu.PrefetchScalarGridSpec(
            num_scalar_prefetch=2,
            grid=(T // td,),
            in_specs=[
                pl.BlockSpec((TOP_K, td), lambda i, lo, hi: (0, i), memory_space=pltpu.SMEM),
                pl.BlockSpec((td, W), lambda i, lo, hi: (i, 0)),
            ],
            out_specs=pl.BlockSpec(memory_space=pl.ANY),
            scratch_shapes=[pltpu.VMEM((SUBLANES, W), x1p.dtype),
                            pltpu.SemaphoreType.DMA, pltpu.SemaphoreType.DMA],
        ),
        out_shape=jax.ShapeDtypeStruct((n_rows, W), x1p.dtype),
        compiler_params=_cparams(("arbitrary",)),
        name="dispatch",
    )(fill_lo, fill_hi, dest_t, x1p)


def _silu(g):
    return g / (1.0 + jnp.exp(-g))


def _expert_kernel(be_ref, nu_ref, xs_ref, wg_ref, wu_ref, wd_ref, ys_ref):
    @pl.when(pl.program_id(0) < nu_ref[0])
    def _():
        parts = _unpack_rows(xs_ref[...])
        dk = wg_ref.shape[1] // len(parts)

        def proj(w_ref):
            acc = _dot(parts[0], w_ref[0, 0:dk, :])
            for n in range(1, len(parts)):
                acc = acc + _dot(parts[n], w_ref[0, n * dk:(n + 1) * dk, :])
            return acc

        a = (_silu(proj(wg_ref)) * proj(wu_ref)).astype(MXU_DTYPE)
        ys_ref[...] = _pack_rows(_dot(a, wd_ref[0]))


def _experts(xs, block_e, n_used, w_gate, w_up, w_down):
    n_rows, W = xs.shape
    D = w_gate.shape[1]
    n_blocks = n_rows // ROW_BLOCK
    blk = lambda i, be, nu: (jnp.minimum(i, nu[0] - 1), 0)
    wsel = lambda i, be, nu: (be[i], 0, 0)
    return pl.pallas_call(
        _expert_kernel,
        grid_spec=pltpu.PrefetchScalarGridSpec(
            num_scalar_prefetch=2,
            grid=(n_blocks,),
            in_specs=[
                pl.BlockSpec((ROW_BLOCK, W), blk),
                pl.BlockSpec((1, D, D_EXPERT), wsel),
                pl.BlockSpec((1, D, D_EXPERT), wsel),
                pl.BlockSpec((1, D_EXPERT, D), wsel),
            ],
            out_specs=pl.BlockSpec((ROW_BLOCK, W), blk),
        ),
        out_shape=jax.ShapeDtypeStruct((n_rows, W), xs.dtype),
        compiler_params=_cparams(("arbitrary",)),
        name="experts",
    )(block_e, n_used, xs, w_gate, w_up, w_down)


def _combine_kernel(dest_ref, wk_ref, x1_ref, ys_hbm, wsg_ref, wsu_ref, wsd_ref, g_ref, b_ref,
                    o_ref, buf_ref, sem, *, tc):
    def issue(r, c):
        for k in range(TOP_K):
            _row_copy(ys_hbm, dest_ref[k, r], buf_ref.at[k], r, sem).start()
        return c

    def drain(r, c):
        for k in range(TOP_K):
            _row_copy(ys_hbm, dest_ref[k, r], buf_ref.at[k], r, sem).wait()
        return c

    lax.fori_loop(0, tc, issue, 0)
    x1 = x1_ref[...]
    xb = x1.astype(MXU_DTYPE)
    a = (_silu(_dot(xb, wsg_ref[...])) * _dot(xb, wsu_ref[...])).astype(MXU_DTYPE)
    shared = _dot(a, wsd_ref[...])
    lax.fori_loop(0, tc, drain, 0)
    wk = wk_ref[...]
    groups = [wk[:, 0:1] * v for v in _unpack_rows_f32(buf_ref[0])]
    for k in range(1, TOP_K):
        groups = [g + wk[:, k:k + 1] * v for g, v in zip(groups, _unpack_rows_f32(buf_ref[k]))]
    routed = jnp.concatenate(groups, axis=1)
    o_ref[...] = _layer_norm(ALPHA * x1 + (routed + shared), g_ref[...], b_ref[...])


def _combine(dest_t, wk, x1, ys, w_sg, w_su, w_sd, ln_g, ln_b, tc):
    T, D = x1.shape
    row = lambda i: (i, 0)
    c2 = lambda i: (0, 0)
    return pl.pallas_call(
        functools.partial(_combine_kernel, tc=tc),
        grid=(T // tc,),
        in_specs=[
            pl.BlockSpec((TOP_K, tc), lambda i: (0, i), memory_space=pltpu.SMEM),
            pl.BlockSpec((tc, TOP_K), row),
            pl.BlockSpec((tc, D), row),
            pl.BlockSpec(memory_space=pl.ANY),
            pl.BlockSpec(w_sg.shape, c2),
            pl.BlockSpec(w_su.shape, c2),
            pl.BlockSpec(w_sd.shape, c2),
            pl.BlockSpec((1, D), c2),
            pl.BlockSpec((1, D), c2),
        ],
        out_specs=pl.BlockSpec((tc, D), row),
        out_shape=jax.ShapeDtypeStruct((T, D), jnp.float32),
        scratch_shapes=[pltpu.VMEM((TOP_K, tc, ys.shape[1]), ys.dtype), pltpu.SemaphoreType.DMA],
        compiler_params=_cparams(("arbitrary",)),
        name="combine",
    )(dest_t, wk, x1, ys, w_sg, w_su, w_sd, ln_g, ln_b)


def _split_w_in(w_in):
    bf = MXU_DTYPE
    o_kv = Q_RANK
    o_ki = o_kv + KV_RANK
    o_iw = o_ki + IDX_DIM
    o_rest = o_iw + N_IDX_HEADS
    w_main = jnp.concatenate([w_in[:, :o_ki], w_in[:, o_rest:]], axis=1).astype(bf)
    w_small = jnp.pad(w_in[:, o_ki:o_rest], ((0, 0), (0, LANES - IDX_DIM - N_IDX_HEADS))).astype(bf)
    return w_main, w_small


def _stages(x, mem, w_in, q_norm_g, kv_norm_g, w_uq, w_uk, w_uv, w_qidx, rel_bias, conv_w, w_mem_k, w_mem_v, w_out, ln1_g, ln1_b, w_router, router_bias, w_e_gate, w_e_up, w_e_down, w_s_gate, w_s_up, w_s_down, ln2_g, ln2_b, upto=None):
    B, S, D = x.shape
    T = B * S
    bf = MXU_DTYPE
    l = 0
    res = {}
    x2 = x.reshape(T, D)
    w_main, w_small = _split_w_in(w_in[l])
    cq, ckv, ckvt, kidx, iwt, yb, yc = _proj(
        x2, mem, w_main, w_small, q_norm_g[l].reshape(1, -1), kv_norm_g[l].reshape(1, -1), conv_w[l],
        w_mem_k[l].astype(bf), w_mem_v[l].astype(bf), B, S, tm=min(512, S))
    res.update(c_q=cq, c_kv=ckv, k_idx=kidx, y_b=yb, y_c=yc,
               idx_w=jnp.swapaxes(iwt, 1, 2) / (N_IDX_HEADS ** -0.5 * IDX_DIM ** -0.5))
    if upto == "proj":
        return res
    bias_t = _bias_tiles(rel_bias)
    ya = _dsa(cq, iwt, kidx, ckv, ckvt,
              w_qidx[l].reshape(Q_RANK, -1).astype(bf), w_uq[l].reshape(Q_RANK, -1).astype(bf),
              jnp.transpose(w_uk[l], (1, 0, 2)).astype(bf), jnp.transpose(w_uv[l], (1, 2, 0)).astype(bf),
              bias_t, B, S)
    res.update(y_a=ya)
    if upto == "dsa":
        return res

    x1, x1p, sel_t, w_t, pos_t, cnt = _mix_router(
        x2, ya, yb, yc, w_out[l].astype(bf), ln1_g[l].reshape(1, -1), ln1_b[l].reshape(1, -1),
        w_router[l].T, router_bias[l].reshape(-1, 1), tm=min(512, T))
    res.update(x1=x1)

    counts = cnt[:, 0].astype(jnp.int32)
    padded = (counts + ROW_BLOCK - 1) // ROW_BLOCK * ROW_BLOCK
    pad_end = jnp.cumsum(padded)
    pad_start = pad_end - padded
    n_blocks = -(-(T * TOP_K) // ROW_BLOCK) + N_EXPERTS
    n_rows = n_blocks * ROW_BLOCK
    block_start = jnp.arange(n_blocks, dtype=jnp.int32) * ROW_BLOCK
    block_e = jnp.minimum(jnp.sum((pad_end[None, :] <= block_start[:, None]).astype(jnp.int32), axis=1),
                          N_EXPERTS - 1)
    n_used = (pad_end[-1:] // ROW_BLOCK).astype(jnp.int32)

    dest_t, wk_t = _compact(sel_t, w_t, pos_t, pad_start.astype(jnp.float32).reshape(-1, 1), tm=min(512, T))
    xs = _dispatch(dest_t, x1p, (pad_start + counts).astype(jnp.int32), pad_end.astype(jnp.int32), n_rows,
                   td=min(256, T))
    ys = _experts(xs, block_e, n_used, w_e_gate[l].astype(bf), w_e_up[l].astype(bf), w_e_down[l].astype(bf))
    out = _combine(dest_t, wk_t.T, x1, ys, w_s_gate[l].astype(bf), w_s_up[l].astype(bf), w_s_down[l].astype(bf),
                   ln2_g[l].reshape(1, -1), ln2_b[l].reshape(1, -1), tc=min(128, T))
    res.update(out=out.reshape(B, S, D))
    return res


def kernel(x, mem, w_in, q_norm_g, kv_norm_g, w_uq, w_uk, w_uv, w_qidx, rel_bias, conv_w, w_mem_k, w_mem_v, w_out, ln1_g, ln1_b, w_router, router_bias, w_e_gate, w_e_up, w_e_down, w_s_gate, w_s_up, w_s_down, ln2_g, ln2_b):
    return _stages(x, mem, w_in, q_norm_g, kv_norm_g, w_uq, w_uk, w_uv, w_qidx, rel_bias, conv_w, w_mem_k, w_mem_v, w_out, ln1_g, ln1_b, w_router, router_bias, w_e_gate, w_e_up, w_e_down, w_s_gate, w_s_up, w_s_down, ln2_g, ln2_b)["out"]
```

```python
import functools
import math

import jax
import jax.numpy as jnp
from jax import lax
from jax.experimental import pallas as pl
from jax.experimental.pallas import tpu as pltpu
from jax.experimental.pallas import tpu_sc as plsc

N_HEADS_A = 8
HEAD_DIM = 64
Q_RANK = 256
KV_RANK = 128
N_IDX_HEADS = 8
IDX_DIM = 64
TOPK_MAX = 256
REL_BUCKETS = 32
REL_MAX_DIST = 128
CONV_CH = 256
CONV_WIDTH = 3
N_MEM_HEADS = 4
MIX_A = N_HEADS_A * HEAD_DIM
MIX_C = N_MEM_HEADS * HEAD_DIM
N_EXPERTS = 64
N_GROUPS = 8
GROUP_SIZE = N_EXPERTS // N_GROUPS
TOPK_GROUPS = 4
TOP_K = 8
D_EXPERT = 256
ROUTED_SCALE = 2.5
MOE_BLOCK = 256
DEPTH = 1
ALPHA = (2.0 * DEPTH) ** 0.25
LN_EPS = 1e-5
RMS_EPS = 1e-6

LANES = 128
SUBLANES = 8
QB = 128
F32_LOWEST = -3.4028234663852886e38
VMEM_LIMIT = 56 * 1024 * 1024
MXU_DTYPE = jnp.bfloat16
ROW_BLOCK = 512

_NT = (((1,), (1,)), ((), ()))


def _dot(a, b):
    return jnp.dot(a, b, preferred_element_type=jnp.float32)


def _dot_nt(a, b):
    return lax.dot_general(a, b, _NT, preferred_element_type=jnp.float32)


def _cparams(sem):
    return pltpu.CompilerParams(dimension_semantics=sem, vmem_limit_bytes=VMEM_LIMIT)


def _bias_kernel(rb_ref, o_ref):
    s = lax.broadcasted_iota(jnp.int32, (QB, QB), 0)
    t = lax.broadcasted_iota(jnp.int32, (QB, QB), 1)
    max_exact = REL_BUCKETS // 2
    for tile in range(3):
        n = jnp.maximum(t - s + (2 - tile) * QB, 0)
        nf = jnp.maximum(n.astype(jnp.float32), 1.0)
        large = max_exact + (jnp.log(nf / max_exact) / math.log(REL_MAX_DIST / max_exact)
                             * (REL_BUCKETS - max_exact)).astype(jnp.int32)
        large = jnp.minimum(large, REL_BUCKETS - 1)
        bucket = jnp.where(n < max_exact, n, large)
        for h in range(N_HEADS_A):
            acc = jnp.zeros((QB, QB), jnp.float32)
            for b in range(REL_BUCKETS):
                acc = jnp.where(bucket == b, rb_ref[b, h], acc)
            o_ref[tile, h] = acc


def _bias_tiles(rel_bias):
    return pl.pallas_call(
        _bias_kernel,
        in_specs=[pl.BlockSpec(memory_space=pltpu.SMEM)],
        out_specs=pl.BlockSpec(memory_space=pltpu.VMEM),
        out_shape=jax.ShapeDtypeStruct((3, N_HEADS_A, QB, QB), jnp.float32),
        name="bias_tiles",
    )(rel_bias)


_MAIN_COLS = Q_RANK + KV_RANK + 3 * CONV_CH + MIX_C


def _proj_kernel(x_ref, mem_ref, wm_ref, ws_ref, qg_ref, kvg_ref, cw_ref, wmk_ref, wmv_ref,
                 cq_ref, ckv_ref, ckvt_ref, kidx_ref, iwt_ref, yb_ref, yc_ref,
                 carry_ref, mk_ref, mv_ref, *, tm):
    si = pl.program_id(1)

    @pl.when(si == 0)
    def _():
        carry_ref[...] = jnp.zeros_like(carry_ref)
        mb = mem_ref[0].astype(MXU_DTYPE)
        mk_ref[...] = _dot(mb, wmk_ref[...]).astype(MXU_DTYPE)
        mv_ref[...] = _dot(mb, wmv_ref[...]).astype(MXU_DTYPE)

    xb = x_ref[...].astype(MXU_DTYPE)
    p = _dot(xb, wm_ref[...])
    small = _dot(xb, ws_ref[...])

    o = 0
    cq = p[:, o:o + Q_RANK]; o += Q_RANK
    ckv = p[:, o:o + KV_RANK]; o += KV_RANK
    g_b = p[:, o:o + CONV_CH]; o += CONV_CH
    g_c = p[:, o:o + CONV_CH]; o += CONV_CH
    h_c = p[:, o:o + CONV_CH]; o += CONV_CH
    q_mem = p[:, o:o + MIX_C]

    cq = cq * lax.rsqrt(jnp.mean(cq * cq, axis=-1, keepdims=True) + RMS_EPS) * qg_ref[...]
    ckv = ckv * lax.rsqrt(jnp.mean(ckv * ckv, axis=-1, keepdims=True) + RMS_EPS) * kvg_ref[...]
    cq_ref[...] = cq.astype(MXU_DTYPE)
    ckv_b = ckv.astype(MXU_DTYPE)
    ckv_ref[...] = ckv_b
    ckvt_ref[0] = ckv.T.astype(MXU_DTYPE)

    kidx_ref[...] = small[:, :IDX_DIM].astype(MXU_DTYPE)
    small_t = small.T
    iwt_ref[0] = small_t[IDX_DIM:IDX_DIM + N_IDX_HEADS, :] * (N_IDX_HEADS ** -0.5 * IDX_DIM ** -0.5)

    u = g_c * h_c
    rows = lax.broadcasted_iota(jnp.int32, (tm, 1), 0)
    c6 = carry_ref[SUBLANES - 2:SUBLANES - 1, :]
    c7 = carry_ref[SUBLANES - 1:SUBLANES, :]
    u1 = jnp.where(rows == 0, c7, pltpu.roll(u, 1, 0))
    u2 = jnp.where(rows == 0, c6, jnp.where(rows == 1, c7, pltpu.roll(u, 2, 0)))
    y = cw_ref[0:1, :] * u2
    y = y + cw_ref[1:2, :] * u1
    y = y + cw_ref[2:3, :] * u
    yb_ref[...] = (g_b * y).astype(MXU_DTYPE)
    carry_ref[...] = u[tm - SUBLANES:, :]

    qm = q_mem.astype(MXU_DTYPE)
    outs = []
    for h in range(N_MEM_HEADS):
        sl = slice(h * HEAD_DIM, (h + 1) * HEAD_DIM)
        lg = _dot_nt(qm[:, sl], mk_ref[:, sl]) * (HEAD_DIM ** -0.5)
        lg = lg - jnp.max(lg, axis=-1, keepdims=True)
        e = jnp.exp(lg)
        pr = e / jnp.sum(e, axis=-1, keepdims=True)
        outs.append(_dot(pr.astype(MXU_DTYPE), mv_ref[:, sl]))
    yc_ref[...] = jnp.concatenate(outs, axis=-1).astype(MXU_DTYPE)


def _proj(x2, mem, w_main, w_small, q_g, kv_g, conv_w, w_mk, w_mv, B, S, tm):
    T, D = x2.shape
    n_mem = mem.shape[1]
    ns = S // tm
    row = lambda b, s: (b * ns + s, 0)
    const2 = lambda b, s: (0, 0)
    bf = MXU_DTYPE
    return pl.pallas_call(
        functools.partial(_proj_kernel, tm=tm),
        grid=(B, ns),
        in_specs=[
            pl.BlockSpec((tm, D), row),
            pl.BlockSpec((1, n_mem, D), lambda b, s: (b, 0, 0)),
            pl.BlockSpec(w_main.shape, const2),
            pl.BlockSpec(w_small.shape, const2),
            pl.BlockSpec(q_g.shape, const2),
            pl.BlockSpec(kv_g.shape, const2),
            pl.BlockSpec(conv_w.shape, const2),
            pl.BlockSpec(w_mk.shape, const2),
            pl.BlockSpec(w_mv.shape, const2),
        ],
        out_specs=[
            pl.BlockSpec((tm, Q_RANK), row),
            pl.BlockSpec((tm, KV_RANK), row),
            pl.BlockSpec((1, KV_RANK, tm), lambda b, s: (b, 0, s)),
            pl.BlockSpec((tm, IDX_DIM), row),
            pl.BlockSpec((1, N_IDX_HEADS, tm), lambda b, s: (b, 0, s)),
            pl.BlockSpec((tm, CONV_CH), row),
            pl.BlockSpec((tm, MIX_C), row),
        ],
        out_shape=[
            jax.ShapeDtypeStruct((T, Q_RANK), bf),
            jax.ShapeDtypeStruct((T, KV_RANK), bf),
            jax.ShapeDtypeStruct((B, KV_RANK, S), bf),
            jax.ShapeDtypeStruct((T, IDX_DIM), bf),
            jax.ShapeDtypeStruct((B, N_IDX_HEADS, S), jnp.float32),
            jax.ShapeDtypeStruct((T, CONV_CH), bf),
            jax.ShapeDtypeStruct((T, MIX_C), bf),
        ],
        scratch_shapes=[
            pltpu.VMEM((SUBLANES, CONV_CH), jnp.float32),
            pltpu.VMEM((n_mem, MIX_C), bf),
            pltpu.VMEM((n_mem, MIX_C), bf),
        ],
        compiler_params=_cparams(("arbitrary", "arbitrary")),
        name="proj",
    )(x2, mem, w_main, w_small, q_g, kv_g, conv_w, w_mk, w_mv)


def _key_to_f32(key):
    bits = jnp.where(key < 0, key ^ jnp.int32(0x7FFFFFFF), key)
    return pltpu.bitcast(bits, jnp.float32)


def _colsum8(v):
    return jnp.sum(v.reshape(QB // SUBLANES, SUBLANES, QB), axis=0)


def _colmax8(v):
    return jnp.max(v.reshape(QB // SUBLANES, SUBLANES, QB), axis=0)


UNROLL = 4


def _dsa_kernel(cq_ref, iwt_ref, kidx_ref, ckv_ref, ckvt_ref, wqi_ref, wuq_ref, wuk_ref, wuvt_ref,
                bias_ref, o_ref, qidx_ref, qlat_ref, score_ref, mask_ref, logit_ref, acc_ref,
                *, k_sel, idx_bits):
    i = pl.program_id(1)
    f32 = jnp.float32
    bf = MXU_DTYPE
    n_blocks = i + 1
    s_loc = lax.broadcasted_iota(jnp.int32, (QB, QB), 0)
    t_glob = i * QB + lax.broadcasted_iota(jnp.int32, (QB, QB), 1)

    def blk(jb):
        return pl.multiple_of(jb * QB, QB)

    def block_loop(fn, init):
        n_main = n_blocks // UNROLL
        c = lax.fori_loop(0, n_main, lambda it, c: fn(it * UNROLL, UNROLL, c), init)
        return lax.fori_loop(n_main * UNROLL, n_blocks, lambda jb, c: fn(jb, 1, c), c)

    cq = cq_ref[...]
    q_all = _dot(cq, wuq_ref[...]).astype(bf)
    for h in range(N_HEADS_A):
        qidx_ref[h * QB:(h + 1) * QB, :] = _dot(cq, wqi_ref[:, h * IDX_DIM:(h + 1) * IDX_DIM]).astype(bf)
        qlat_ref[h * QB:(h + 1) * QB, :] = (
            _dot_nt(q_all[:, h * HEAD_DIM:(h + 1) * HEAD_DIM], wuk_ref[h]) * (HEAD_DIM ** -0.5)).astype(bf)
    iw = iwt_ref[0]

    def score_body(jb0, nb, c):
        for sb in range(nb):
            off = blk(jb0 + sb)
            d_all = _dot_nt(kidx_ref[pl.ds(off, QB), :], qidx_ref[...])
            acc = jnp.maximum(d_all[:, 0:QB], 0.0) * iw[0:1, :]
            for h in range(1, N_IDX_HEADS):
                acc = acc + jnp.maximum(d_all[:, h * QB:(h + 1) * QB], 0.0) * iw[h:h + 1, :]
            score_ref[pl.ds(off, QB), :] = jnp.where(s_loc + off <= t_glob, acc + 0.0, F32_LOWEST)
        return c

    block_loop(score_body, 0)

    def count_where(pred):
        def body(jb0, nb, acc):
            for sb in range(nb):
                off = blk(jb0 + sb)
                acc = acc + _colsum8(jnp.where(pred(score_ref[pl.ds(off, QB), :], off), 1.0, 0.0))
            return acc
        acc = block_loop(body, jnp.zeros((SUBLANES, QB), f32))
        return jnp.sum(acc, axis=0, keepdims=True)

    kf = float(k_sel)

    def search():
        c0 = count_where(lambda sc, off: sc >= 0.0)
        cand0 = jnp.where(c0 >= kf, jnp.int32(0), jnp.int32(-2 ** 31))

        def bit_body(it, cand):
            trial = cand + lax.shift_left(jnp.int32(1), 30 - it)
            tf = _key_to_f32(trial)
            cnt = count_where(lambda sc, off: sc >= tf)
            return jnp.where(cnt >= kf, trial, cand)

        cand = lax.fori_loop(0, 31, bit_body, cand0)
        thr = _key_to_f32(cand)
        n_gt = count_where(lambda sc, off: sc > thr)
        n_eq = count_where(lambda sc, off: sc == thr)
        need = kf - n_gt

        def tie_search():
            def tbody(it, xcut):
                trial = xcut + lax.shift_left(jnp.int32(1), idx_bits - 1 - it)
                cnt = count_where(lambda sc, off: (sc == thr) & (s_loc + off < trial))
                return jnp.where(cnt < need, trial, xcut)
            return lax.fori_loop(0, idx_bits, tbody, jnp.zeros((1, QB), jnp.int32))

        any_extra = jnp.max(n_eq - need) > 0.0
        xcut = lax.cond(any_extra, tie_search, lambda: jnp.full((1, QB), 2 ** idx_bits - 1, jnp.int32))
        return thr, xcut

    def no_search():
        return jnp.full((1, QB), F32_LOWEST, f32), jnp.full((1, QB), 2 ** idx_bits - 1, jnp.int32)

    thr, xcut = lax.cond((i + 1) * QB > k_sel, search, no_search)

    def mask_body(jb0, nb, c):
        for sb in range(nb):
            off = blk(jb0 + sb)
            sc = score_ref[pl.ds(off, QB), :]
            s_glob = s_loc + off
            keep = ((sc > thr) | ((sc == thr) & (s_glob <= xcut))) & (s_glob <= t_glob)
            mask_ref[pl.ds(off, QB), :] = jnp.where(keep, 0.0, -jnp.inf)
        return c

    block_loop(mask_body, 0)

    def p1_body(jb0, nb, m8):
        m8 = list(m8)
        for sb in range(nb):
            off = blk(jb0 + sb)
            lg = _dot_nt(ckv_ref[pl.ds(off, QB), :], qlat_ref[...])
            msk = mask_ref[pl.ds(off, QB), :]
            bsel = jnp.clip(jb0 + sb - i + 2, 0, 2)
            for h in range(N_HEADS_A):
                lgh = lg[:, h * QB:(h + 1) * QB] + bias_ref[bsel, h] + msk
                logit_ref[pl.ds(off, QB), h * QB:(h + 1) * QB] = lgh
                m8[h] = jnp.maximum(m8[h], _colmax8(lgh))
        return tuple(m8)

    m8 = block_loop(p1_body, tuple(jnp.full((SUBLANES, QB), -jnp.inf, f32) for _ in range(N_HEADS_A)))
    m_row = [jnp.max(m, axis=0, keepdims=True) for m in m8]

    acc_ref[...] = jnp.zeros_like(acc_ref)

    def p2_body(jb0, nb, l8):
        l8 = list(l8)
        off = blk(jb0)
        rows = nb * QB
        ps = []
        for h in range(N_HEADS_A):
            p = jnp.exp(logit_ref[pl.ds(off, rows), h * QB:(h + 1) * QB] - m_row[h])
            l8[h] = l8[h] + jnp.sum(p.reshape(rows // SUBLANES, SUBLANES, QB), axis=0)
            ps.append(p.astype(bf))
        acc_ref[...] += _dot(ckvt_ref[0, :, pl.ds(off, rows)], jnp.concatenate(ps, axis=1))
        return tuple(l8)

    l8 = block_loop(p2_body, tuple(jnp.zeros((SUBLANES, QB), f32) for _ in range(N_HEADS_A)))

    outs = []
    for h in range(N_HEADS_A):
        l_row = jnp.sum(l8[h], axis=0, keepdims=True)
        o_lat_t = (acc_ref[:, h * QB:(h + 1) * QB] / l_row).astype(bf)
        outs.append(_dot(wuvt_ref[h], o_lat_t))
    o_ref[...] = jnp.concatenate(outs, axis=0).T.astype(o_ref.dtype)


def _dsa(cq, iwt, kidx, ckv, ckvt, w_qidx, w_uq, w_uk_h, w_uvt_h, bias_tiles, B, S):
    T = cq.shape[0]
    assert S % QB == 0 and QB >= REL_MAX_DIST
    nq = S // QB
    k_sel = min(TOPK_MAX, S // 4)
    idx_bits = max(1, (S - 1).bit_length())
    c2 = lambda b, i: (0, 0)
    c3 = lambda b, i: (0, 0, 0)
    return pl.pallas_call(
        functools.partial(_dsa_kernel, k_sel=k_sel, idx_bits=idx_bits),
        grid=(B, nq),
        in_specs=[
            pl.BlockSpec((QB, Q_RANK), lambda b, i: (b * nq + i, 0)),
            pl.BlockSpec((1, N_IDX_HEADS, QB), lambda b, i: (b, 0, i)),
            pl.BlockSpec((S, IDX_DIM), lambda b, i: (b, 0)),
            pl.BlockSpec((S, KV_RANK), lambda b, i: (b, 0)),
            pl.BlockSpec((1, KV_RANK, S), lambda b, i: (b, 0, 0)),
            pl.BlockSpec(w_qidx.shape, c2),
            pl.BlockSpec(w_uq.shape, c2),
            pl.BlockSpec(w_uk_h.shape, c3),
            pl.BlockSpec(w_uvt_h.shape, c3),
            pl.BlockSpec(bias_tiles.shape, lambda b, i: (0, 0, 0, 0)),
        ],
        out_specs=pl.BlockSpec((QB, MIX_A), lambda b, i: (b * nq + i, 0)),
        out_shape=jax.ShapeDtypeStruct((T, MIX_A), MXU_DTYPE),
        scratch_shapes=[
            pltpu.VMEM((N_IDX_HEADS * QB, IDX_DIM), MXU_DTYPE),
            pltpu.VMEM((N_HEADS_A * QB, KV_RANK), MXU_DTYPE),
            pltpu.VMEM((S, QB), jnp.float32),
            pltpu.VMEM((S, QB), jnp.float32),
            pltpu.VMEM((S, N_HEADS_A * QB), jnp.float32),
            pltpu.VMEM((KV_RANK, N_HEADS_A * QB), jnp.float32),
        ],
        compiler_params=_cparams(("arbitrary", "arbitrary")),
        name="dsa",
    )(cq, iwt, kidx, ckv, ckvt, w_qidx, w_uq, w_uk_h, w_uvt_h, bias_tiles)


def _layer_norm(xf, g, b):
    mu = jnp.mean(xf, axis=-1, keepdims=True)
    xc = xf - mu
    var = jnp.mean(xc * xc, axis=-1, keepdims=True)
    return xc * lax.rsqrt(var + LN_EPS) * g + b


def _rank_rows(v, n):
    ri = lax.broadcasted_iota(jnp.int32, v.shape, 0)
    rank = jnp.zeros(v.shape, jnp.float32)
    for r2 in range(n):
        row = v[r2:r2 + 1, :]
        beats = (row > v) | ((row == v) & (ri > r2))
        rank = rank + jnp.where(beats, 1.0, 0.0)
    return rank


def _pack_factor():
    return 4 // jnp.dtype(MXU_DTYPE).itemsize


def _pack_rows(x):
    if _pack_factor() == 1:
        return pltpu.bitcast(x, jnp.uint32)
    half = x.shape[1] // 2
    b = pltpu.bitcast(x.astype(MXU_DTYPE).astype(jnp.float32), jnp.uint32)
    return b[:, half:] | lax.shift_right_logical(b[:, :half], jnp.uint32(16))


def _unpack_rows_f32(p):
    if _pack_factor() == 1:
        return [pltpu.bitcast(p, jnp.float32)]
    lo = pltpu.bitcast(lax.shift_left(p, jnp.uint32(16)), jnp.float32)
    hi = pltpu.bitcast(p & jnp.uint32(0xFFFF0000), jnp.float32)
    return [lo, hi]


def _unpack_rows(p):
    return [v.astype(MXU_DTYPE) for v in _unpack_rows_f32(p)]


def _mix_router_kernel(x_ref, ya_ref, yb_ref, yc_ref, wo_ref, g_ref, b_ref, wrt_ref, rb_ref, exp_ref,
                       x1_ref, x1p_ref, sel_ref, w_ref, pos_ref, cnt_ref, base_ref, *, tm):
    step = pl.program_id(0)
    f32 = jnp.float32

    @pl.when(step == 0)
    def _():
        base_ref[...] = jnp.zeros_like(base_ref)

    mix = _dot(ya_ref[...], wo_ref[0:MIX_A, :])
    mix = mix + _dot(yb_ref[...], wo_ref[MIX_A:MIX_A + CONV_CH, :])
    mix = mix + _dot(yc_ref[...], wo_ref[MIX_A + CONV_CH:, :])
    x1 = _layer_norm(ALPHA * x_ref[...] + mix, g_ref[...], b_ref[...])
    x1_ref[...] = x1
    x1p_ref[...] = _pack_rows(x1)

    lg = lax.dot_general(wrt_ref[...], x1, _NT, precision=lax.Precision.HIGHEST, preferred_element_type=f32)
    s = 1.0 / (1.0 + jnp.exp(-lg))
    sc = s + rb_ref[...]

    g3 = sc.reshape(N_GROUPS, GROUP_SIZE, tm)
    m1 = jnp.max(g3, axis=1, keepdims=True)
    is_m1 = g3 == m1
    n_m1 = jnp.sum(jnp.where(is_m1, 1.0, 0.0), axis=1, keepdims=True)
    m2 = jnp.max(jnp.where(is_m1, -jnp.inf, g3), axis=1, keepdims=True)
    gscore = (m1 + jnp.where(n_m1 > 1.0, m1, m2)).reshape(N_GROUPS, tm)
    gsel = jnp.where(_rank_rows(gscore, N_GROUPS) < float(TOPK_GROUPS), 1.0, 0.0)
    emask = _dot(exp_ref[...], gsel.astype(MXU_DTYPE)) > 0.5
    masked = jnp.where(emask, sc, -jnp.inf)
    sel = (_rank_rows(masked, N_EXPERTS) < float(TOP_K)) & emask
    self_ = jnp.where(sel, 1.0, 0.0)
    top_s = jnp.where(sel, s, 0.0)
    w = top_s / jnp.sum(top_s, axis=0, keepdims=True) * ROUTED_SCALE

    t_r = lax.broadcasted_iota(jnp.int32, (tm, tm), 0)
    t_c = lax.broadcasted_iota(jnp.int32, (tm, tm), 1)
    upper = jnp.where(t_r < t_c, 1.0, 0.0).astype(MXU_DTYPE)
    pref = _dot(self_.astype(MXU_DTYPE), upper)
    base = base_ref[...]
    sel_ref[...] = self_
    w_ref[...] = w
    pos_ref[...] = base + pref
    base = base + jnp.sum(self_, axis=1, keepdims=True)
    base_ref[...] = base
    cnt_ref[...] = jnp.broadcast_to(base, cnt_ref.shape)


def _mix_router(x2, ya, yb, yc, w_out, ln_g, ln_b, w_router_t, router_bias, tm):
    T, D = x2.shape
    E = N_EXPERTS
    expand = (jnp.arange(E)[:, None] // GROUP_SIZE == jnp.arange(N_GROUPS)[None, :]).astype(MXU_DTYPE)
    row = lambda i: (i, 0)
    col = lambda i: (0, i)
    c2 = lambda i: (0, 0)
    f32 = jnp.float32
    return pl.pallas_call(
        functools.partial(_mix_router_kernel, tm=tm),
        grid=(T // tm,),
        in_specs=[
            pl.BlockSpec((tm, D), row),
            pl.BlockSpec((tm, MIX_A), row),
            pl.BlockSpec((tm, CONV_CH), row),
            pl.BlockSpec((tm, MIX_C), row),
            pl.BlockSpec(w_out.shape, c2),
            pl.BlockSpec((1, D), c2),
            pl.BlockSpec((1, D), c2),
            pl.BlockSpec((E, D), c2),
            pl.BlockSpec((E, 1), c2),
            pl.BlockSpec((E, N_GROUPS), c2),
        ],
        out_specs=[
            pl.BlockSpec((tm, D), row),
            pl.BlockSpec((tm, D // _pack_factor()), row),
            pl.BlockSpec((E, tm), col),
            pl.BlockSpec((E, tm), col),
            pl.BlockSpec((E, tm), col),
            pl.BlockSpec((E, LANES), c2),
        ],
        out_shape=[
            jax.ShapeDtypeStruct((T, D), f32),
            jax.ShapeDtypeStruct((T, D // _pack_factor()), jnp.uint32),
            jax.ShapeDtypeStruct((E, T), f32),
            jax.ShapeDtypeStruct((E, T), f32),
            jax.ShapeDtypeStruct((E, T), f32),
            jax.ShapeDtypeStruct((E, LANES), f32),
        ],
        scratch_shapes=[pltpu.VMEM((E, 1), f32)],
        compiler_params=_cparams(("arbitrary",)),
        name="mix_router",
    )(x2, ya, yb, yc, w_out, ln_g, ln_b, w_router_t, router_bias, expand)


def _compact_kernel(sel_ref, w_ref, pos_ref, pstart_ref, low_ref, dest_ref, wk_ref):
    sel = sel_ref[...]
    on = sel > 0.5
    rank = _dot(low_ref[...], sel.astype(MXU_DTYPE))
    row = pstart_ref[...] + pos_ref[...]
    w = w_ref[...]
    dests, ws = [], []
    for k in range(TOP_K):
        m = on & (rank == float(k))
        dests.append(jnp.sum(jnp.where(m, row, 0.0), axis=0, keepdims=True))
        ws.append(jnp.sum(jnp.where(m, w, 0.0), axis=0, keepdims=True))
    dest_ref[...] = jnp.concatenate(dests, axis=0).astype(jnp.int32)
    wk_ref[...] = jnp.concatenate(ws, axis=0)


def _compact(sel_t, w_t, pos_t, pad_start, tm):
    E, T = sel_t.shape
    lower = (jnp.arange(E)[None, :] < jnp.arange(E)[:, None]).astype(MXU_DTYPE)
    col = lambda i: (0, i)
    c2 = lambda i: (0, 0)
    return pl.pallas_call(
        _compact_kernel,
        grid=(T // tm,),
        in_specs=[pl.BlockSpec((E, tm), col), pl.BlockSpec((E, tm), col), pl.BlockSpec((E, tm), col),
                  pl.BlockSpec((E, 1), c2), pl.BlockSpec((E, E), c2)],
        out_specs=[pl.BlockSpec((TOP_K, tm), col), pl.BlockSpec((TOP_K, tm), col)],
        out_shape=[jax.ShapeDtypeStruct((TOP_K, T), jnp.int32), jax.ShapeDtypeStruct((TOP_K, T), jnp.float32)],
        compiler_params=_cparams(("arbitrary",)),
        name="route_compact",
    )(sel_t, w_t, pos_t, pad_start, lower)


def _row_copy(src, s, dst, d, sem):
    return pltpu.make_async_copy(src.at[pl.ds(s, 1)], dst.at[pl.ds(d, 1)], sem)


def _dispatch_kernel(flo_ref, fhi_ref, dest_ref, x_ref, xs_hbm, zero_ref, sem, zsem, *, td):
    step = pl.program_id(0)

    @pl.when(step == 0)
    def _():
        zero_ref[...] = jnp.zeros_like(zero_ref)

        def per_expert(fn):
            def ebody(e, c):
                lax.fori_loop(flo_ref[e], fhi_ref[e], lambda r, c2: (fn(r), c2)[1], 0)
                return c
            lax.fori_loop(0, N_EXPERTS, ebody, 0)

        per_expert(lambda r: _row_copy(zero_ref, 0, xs_hbm, r, zsem).start())
        per_expert(lambda r: _row_copy(zero_ref, 0, xs_hbm, r, zsem).wait())

    def issue(r, c):
        for k in range(TOP_K):
            _row_copy(x_ref, r, xs_hbm, dest_ref[k, r], sem).start()
        return c

    def drain(r, c):
        for k in range(TOP_K):
            _row_copy(x_ref, r, xs_hbm, dest_ref[k, r], sem).wait()
        return c

    lax.fori_loop(0, td, issue, 0)
    lax.fori_loop(0, td, drain, 0)


def _dispatch(dest_t, x1p, fill_lo, fill_hi, n_rows, td):
    T, W = x1p.shape
    return pl.pallas_call(
        functools.partial(_dispatch_kernel, td=td),
        grid_spec=pltpu.PrefetchScalarGridSpec(
            num_scalar_prefetch=2,
            grid=(T // td,),
            in_specs=[
                pl.BlockSpec((TOP_K, td), lambda i, lo, hi: (0, i), memory_space=pltpu.SMEM),
                pl.BlockSpec((td, W), lambda i, lo, hi: (i, 0)),
            ],
            out_specs=pl.BlockSpec(memory_space=pl.ANY),
            scratch_shapes=[pltpu.VMEM((SUBLANES, W), x1p.dtype),
                            pltpu.SemaphoreType.DMA, pltpu.SemaphoreType.DMA],
        ),
        out_shape=jax.ShapeDtypeStruct((n_rows, W), x1p.dtype),
        compiler_params=_cparams(("arbitrary",)),
        name="dispatch",
    )(fill_lo, fill_hi, dest_t, x1p)


def _silu(g):
    return g / (1.0 + jnp.exp(-g))


def _expert_kernel(be_ref, nu_ref, xs_ref, wg_ref, wu_ref, wd_ref, ys_ref):
    @pl.when(pl.program_id(0) < nu_ref[0])
    def _():
        parts = _unpack_rows(xs_ref[...])
        dk = wg_ref.shape[1] // len(parts)

        def proj(w_ref):
            acc = _dot(parts[0], w_ref[0, 0:dk, :])
            for n in range(1, len(parts)):
                acc = acc + _dot(parts[n], w_ref[0, n * dk:(n + 1) * dk, :])
            return acc

        a = (_silu(proj(wg_ref)) * proj(wu_ref)).astype(MXU_DTYPE)
        ys_ref[...] = _pack_rows(_dot(a, wd_ref[0]))


def _experts(xs, block_e, n_used, w_gate, w_up, w_down):
    n_rows, W = xs.shape
    D = w_gate.shape[1]
    n_blocks = n_rows // ROW_BLOCK
    blk = lambda i, be, nu: (jnp.minimum(i, nu[0] - 1), 0)
    wsel = lambda i, be, nu: (be[i], 0, 0)
    return pl.pallas_call(
        _expert_kernel,
        grid_spec=pltpu.PrefetchScalarGridSpec(
            num_scalar_prefetch=2,
            grid=(n_blocks,),
            in_specs=[
                pl.BlockSpec((ROW_BLOCK, W), blk),
                pl.BlockSpec((1, D, D_EXPERT), wsel),
                pl.BlockSpec((1, D, D_EXPERT), wsel),
                pl.BlockSpec((1, D_EXPERT, D), wsel),
            ],
            out_specs=pl.BlockSpec((ROW_BLOCK, W), blk),
        ),
        out_shape=jax.ShapeDtypeStruct((n_rows, W), xs.dtype),
        compiler_params=_cparams(("arbitrary",)),
        name="experts",
    )(block_e, n_used, xs, w_gate, w_up, w_down)


SC_CORES = 2
SC_SUBCORES = 16
SC_GATHER_ROWS = 64


def _sc_gather_rows(table, idx):
    n = idx.shape[0]
    w = table.shape[1]
    n_workers = SC_CORES * SC_SUBCORES
    per_worker = n // n_workers
    assert n % n_workers == 0 and per_worker % SC_GATHER_ROWS == 0
    mesh = plsc.VectorSubcoreMesh(core_axis_name="c", subcore_axis_name="s")

    @functools.partial(
        pl.kernel, mesh=mesh,
        out_type=jax.ShapeDtypeStruct((n, w), table.dtype),
        scratch_types=[
            pltpu.VMEM((SC_GATHER_ROWS,), jnp.int32),
            pltpu.VMEM((SC_GATHER_ROWS, w), table.dtype),
            pltpu.SemaphoreType.DMA,
        ],
        name="sc_gather_rows",
    )
    def gather(table_hbm, idx_hbm, out_hbm, idx_v, rows_v, sem):
        wid = lax.axis_index("s") * SC_CORES + lax.axis_index("c")
        base = wid * per_worker

        @pl.loop(0, per_worker // SC_GATHER_ROWS)
        def _(g):
            off = base + g * SC_GATHER_ROWS
            pltpu.sync_copy(idx_hbm.at[pl.ds(off, SC_GATHER_ROWS)], idx_v)
            pltpu.async_copy(table_hbm.at[idx_v], rows_v, sem).wait()
            pltpu.sync_copy(rows_v, out_hbm.at[pl.ds(off, SC_GATHER_ROWS)])

    return gather(table, idx)


def _combine2_kernel(wk_ref, x1_ref, g_ref_rows, wsg_ref, wsu_ref, wsd_ref, g_ref, b_ref, o_ref):
    x1 = x1_ref[...]
    xb = x1.astype(MXU_DTYPE)
    a = (_silu(_dot(xb, wsg_ref[...])) * _dot(xb, wsu_ref[...])).astype(MXU_DTYPE)
    shared = _dot(a, wsd_ref[...])
    wk = wk_ref[...]
    groups = [wk[:, 0:1] * v for v in _unpack_rows_f32(g_ref_rows[0])]
    for k in range(1, TOP_K):
        groups = [g + wk[:, k:k + 1] * v for g, v in zip(groups, _unpack_rows_f32(g_ref_rows[k]))]
    routed = jnp.concatenate(groups, axis=1)
    o_ref[...] = _layer_norm(ALPHA * x1 + (routed + shared), g_ref[...], b_ref[...])


def _combine2(wk, x1, gathered, w_sg, w_su, w_sd, ln_g, ln_b, tc):
    T, D = x1.shape
    W = gathered.shape[2]
    row = lambda i: (i, 0)
    c2 = lambda i: (0, 0)
    return pl.pallas_call(
        _combine2_kernel,
        grid=(T // tc,),
        in_specs=[
            pl.BlockSpec((tc, TOP_K), row),
            pl.BlockSpec((tc, D), row),
            pl.BlockSpec((TOP_K, tc, W), lambda i: (0, i, 0)),
            pl.BlockSpec(w_sg.shape, c2),
            pl.BlockSpec(w_su.shape, c2),
            pl.BlockSpec(w_sd.shape, c2),
            pl.BlockSpec((1, D), c2),
            pl.BlockSpec((1, D), c2),
        ],
        out_specs=pl.BlockSpec((tc, D), row),
        out_shape=jax.ShapeDtypeStruct((T, D), jnp.float32),
        compiler_params=_cparams(("arbitrary",)),
        name="combine",
    )(wk, x1, gathered, w_sg, w_su, w_sd, ln_g, ln_b)


def _combine_kernel(dest_ref, wk_ref, x1_ref, ys_hbm, wsg_ref, wsu_ref, wsd_ref, g_ref, b_ref,
                    o_ref, buf_ref, sem, *, tc):
    def issue(r, c):
        for k in range(TOP_K):
            _row_copy(ys_hbm, dest_ref[k, r], buf_ref.at[k], r, sem).start()
        return c

    def drain(r, c):
        for k in range(TOP_K):
            _row_copy(ys_hbm, dest_ref[k, r], buf_ref.at[k], r, sem).wait()
        return c

    lax.fori_loop(0, tc, issue, 0)
    x1 = x1_ref[...]
    xb = x1.astype(MXU_DTYPE)
    a = (_silu(_dot(xb, wsg_ref[...])) * _dot(xb, wsu_ref[...])).astype(MXU_DTYPE)
    shared = _dot(a, wsd_ref[...])
    lax.fori_loop(0, tc, drain, 0)
    wk = wk_ref[...]
    groups = [wk[:, 0:1] * v for v in _unpack_rows_f32(buf_ref[0])]
    for k in range(1, TOP_K):
        groups = [g + wk[:, k:k + 1] * v for g, v in zip(groups, _unpack_rows_f32(buf_ref[k]))]
    routed = jnp.concatenate(groups, axis=1)
    o_ref[...] = _layer_norm(ALPHA * x1 + (routed + shared), g_ref[...], b_ref[...])


def _combine(dest_t, wk, x1, ys, w_sg, w_su, w_sd, ln_g, ln_b, tc):
    T, D = x1.shape
    row = lambda i: (i, 0)
    c2 = lambda i: (0, 0)
    return pl.pallas_call(
        functools.partial(_combine_kernel, tc=tc),
        grid=(T // tc,),
        in_specs=[
            pl.BlockSpec((TOP_K, tc), lambda i: (0, i), memory_space=pltpu.SMEM),
            pl.BlockSpec((tc, TOP_K), row),
            pl.BlockSpec((tc, D), row),
            pl.BlockSpec(memory_space=pl.ANY),
            pl.BlockSpec(w_sg.shape, c2),
            pl.BlockSpec(w_su.shape, c2),
            pl.BlockSpec(w_sd.shape, c2),
            pl.BlockSpec((1, D), c2),
            pl.BlockSpec((1, D), c2),
        ],
        out_specs=pl.BlockSpec((tc, D), row),
        out_shape=jax.ShapeDtypeStruct((T, D), jnp.float32),
        scratch_shapes=[pltpu.VMEM((TOP_K, tc, ys.shape[1]), ys.dtype), pltpu.SemaphoreType.DMA],
        compiler_params=_cparams(("arbitrary",)),
        name="combine",
    )(dest_t, wk, x1, ys, w_sg, w_su, w_sd, ln_g, ln_b)


def _split_w_in(w_in):
    bf = MXU_DTYPE
    o_kv = Q_RANK
    o_ki = o_kv + KV_RANK
    o_iw = o_ki + IDX_DIM
    o_rest = o_iw + N_IDX_HEADS
    w_main = jnp.concatenate([w_in[:, :o_ki], w_in[:, o_rest:]], axis=1).astype(bf)
    w_small = jnp.pad(w_in[:, o_ki:o_rest], ((0, 0), (0, LANES - IDX_DIM - N_IDX_HEADS))).astype(bf)
    return w_main, w_small


def _stages(x, mem, w_in, q_norm_g, kv_norm_g, w_uq, w_uk, w_uv, w_qidx, rel_bias, conv_w, w_mem_k, w_mem_v, w_out, ln1_g, ln1_b, w_router, router_bias, w_e_gate, w_e_up, w_e_down, w_s_gate, w_s_up, w_s_down, ln2_g, ln2_b, upto=None):
    B, S, D = x.shape
    T = B * S
    bf = MXU_DTYPE
    l = 0
    res = {}
    x2 = x.reshape(T, D)
    w_main, w_small = _split_w_in(w_in[l])
    cq, ckv, ckvt, kidx, iwt, yb, yc = _proj(
        x2, mem, w_main, w_small, q_norm_g[l].reshape(1, -1), kv_norm_g[l].reshape(1, -1), conv_w[l],
        w_mem_k[l].astype(bf), w_mem_v[l].astype(bf), B, S, tm=min(512, S))
    res.update(c_q=cq, c_kv=ckv, k_idx=kidx, y_b=yb, y_c=yc,
               idx_w=jnp.swapaxes(iwt, 1, 2) / (N_IDX_HEADS ** -0.5 * IDX_DIM ** -0.5))
    if upto == "proj":
        return res
    bias_t = _bias_tiles(rel_bias)
    ya = _dsa(cq, iwt, kidx, ckv, ckvt,
              w_qidx[l].reshape(Q_RANK, -1).astype(bf), w_uq[l].reshape(Q_RANK, -1).astype(bf),
              jnp.transpose(w_uk[l], (1, 0, 2)).astype(bf), jnp.transpose(w_uv[l], (1, 2, 0)).astype(bf),
              bias_t, B, S)
    res.update(y_a=ya)
    if upto == "dsa":
        return res

    x1, x1p, sel_t, w_t, pos_t, cnt = _mix_router(
        x2, ya, yb, yc, w_out[l].astype(bf), ln1_g[l].reshape(1, -1), ln1_b[l].reshape(1, -1),
        w_router[l].T, router_bias[l].reshape(-1, 1), tm=min(512, T))
    res.update(x1=x1)

    counts = cnt[:, 0].astype(jnp.int32)
    padded = (counts + ROW_BLOCK - 1) // ROW_BLOCK * ROW_BLOCK
    pad_end = jnp.cumsum(padded)
    pad_start = pad_end - padded
    n_blocks = -(-(T * TOP_K) // ROW_BLOCK) + N_EXPERTS
    n_rows = n_blocks * ROW_BLOCK
    block_start = jnp.arange(n_blocks, dtype=jnp.int32) * ROW_BLOCK
    block_e = jnp.minimum(jnp.sum((pad_end[None, :] <= block_start[:, None]).astype(jnp.int32), axis=1),
                          N_EXPERTS - 1)
    n_used = (pad_end[-1:] // ROW_BLOCK).astype(jnp.int32)

    dest_t, wk_t = _compact(sel_t, w_t, pos_t, pad_start.astype(jnp.float32).reshape(-1, 1), tm=min(512, T))
    xs = _dispatch(dest_t, x1p, (pad_start + counts).astype(jnp.int32), pad_end.astype(jnp.int32), n_rows,
                   td=min(256, T))
    ys = _experts(xs, block_e, n_used, w_e_gate[l].astype(bf), w_e_up[l].astype(bf), w_e_down[l].astype(bf))
    gathered = _sc_gather_rows(lax.bitcast_convert_type(ys, jnp.int32), dest_t.reshape(-1))
    gathered = lax.bitcast_convert_type(gathered, ys.dtype).reshape(TOP_K, T, -1)
    out = _combine2(wk_t.T, x1, gathered, w_s_gate[l].astype(bf), w_s_up[l].astype(bf), w_s_down[l].astype(bf),
                    ln2_g[l].reshape(1, -1), ln2_b[l].reshape(1, -1), tc=min(256, T))
    res.update(out=out.reshape(B, S, D))
    return res


def kernel(x, mem, w_in, q_norm_g, kv_norm_g, w_uq, w_uk, w_uv, w_qidx, rel_bias, conv_w, w_mem_k, w_mem_v, w_out, ln1_g, ln1_b, w_router, router_bias, w_e_gate, w_e_up, w_e_down, w_s_gate, w_s_up, w_s_down, ln2_g, ln2_b):
    return _stages(x, mem, w_in, q_norm_g, kv_norm_g, w_uq, w_uk, w_uv, w_qidx, rel_bias, conv_w, w_mem_k, w_mem_v, w_out, ln1_g, ln1_b, w_router, router_bias, w_e_gate, w_e_up, w_e_down, w_s_gate, w_s_up, w_s_down, ln2_g, ln2_b)["out"]
```

```python
import functools
import math

import jax
import jax.numpy as jnp
from jax import lax
from jax.experimental import pallas as pl
from jax.experimental.pallas import tpu as pltpu
from jax.experimental.pallas import tpu_sc as plsc

N_HEADS_A = 8
HEAD_DIM = 64
Q_RANK = 256
KV_RANK = 128
N_IDX_HEADS = 8
IDX_DIM = 64
TOPK_MAX = 256
REL_BUCKETS = 32
REL_MAX_DIST = 128
CONV_CH = 256
CONV_WIDTH = 3
N_MEM_HEADS = 4
MIX_A = N_HEADS_A * HEAD_DIM
MIX_C = N_MEM_HEADS * HEAD_DIM
N_EXPERTS = 64
N_GROUPS = 8
GROUP_SIZE = N_EXPERTS // N_GROUPS
TOPK_GROUPS = 4
TOP_K = 8
D_EXPERT = 256
ROUTED_SCALE = 2.5
MOE_BLOCK = 256
DEPTH = 1
ALPHA = (2.0 * DEPTH) ** 0.25
LN_EPS = 1e-5
RMS_EPS = 1e-6

LANES = 128
SUBLANES = 8
QB = 128
F32_LOWEST = -3.4028234663852886e38
VMEM_LIMIT = 56 * 1024 * 1024
MXU_DTYPE = jnp.bfloat16
ROW_BLOCK = 512

_NT = (((1,), (1,)), ((), ()))


def _dot(a, b):
    return jnp.dot(a, b, preferred_element_type=jnp.float32)


def _dot_nt(a, b):
    return lax.dot_general(a, b, _NT, preferred_element_type=jnp.float32)


def _cparams(sem):
    return pltpu.CompilerParams(dimension_semantics=sem, vmem_limit_bytes=VMEM_LIMIT)


def _bias_kernel(rb_ref, o_ref):
    s = lax.broadcasted_iota(jnp.int32, (QB, QB), 0)
    t = lax.broadcasted_iota(jnp.int32, (QB, QB), 1)
    max_exact = REL_BUCKETS // 2
    for tile in range(3):
        n = jnp.maximum(t - s + (2 - tile) * QB, 0)
        nf = jnp.maximum(n.astype(jnp.float32), 1.0)
        large = max_exact + (jnp.log(nf / max_exact) / math.log(REL_MAX_DIST / max_exact)
                             * (REL_BUCKETS - max_exact)).astype(jnp.int32)
        large = jnp.minimum(large, REL_BUCKETS - 1)
        bucket = jnp.where(n < max_exact, n, large)
        for h in range(N_HEADS_A):
            acc = jnp.zeros((QB, QB), jnp.float32)
            for b in range(REL_BUCKETS):
                acc = jnp.where(bucket == b, rb_ref[b, h], acc)
            o_ref[tile, h] = acc


def _bias_tiles(rel_bias):
    return pl.pallas_call(
        _bias_kernel,
        in_specs=[pl.BlockSpec(memory_space=pltpu.SMEM)],
        out_specs=pl.BlockSpec(memory_space=pltpu.VMEM),
        out_shape=jax.ShapeDtypeStruct((3, N_HEADS_A, QB, QB), jnp.float32),
        name="bias_tiles",
    )(rel_bias)


_MAIN_COLS = Q_RANK + KV_RANK + 3 * CONV_CH + MIX_C


def _proj_kernel(x_ref, mem_ref, wm_ref, ws_ref, qg_ref, kvg_ref, cw_ref, wmk_ref, wmv_ref,
                 cq_ref, ckv_ref, ckvt_ref, kidx_ref, iwt_ref, yb_ref, yc_ref,
                 carry_ref, mk_ref, mv_ref, *, tm):
    si = pl.program_id(1)

    @pl.when(si == 0)
    def _():
        carry_ref[...] = jnp.zeros_like(carry_ref)
        mb = mem_ref[0].astype(MXU_DTYPE)
        mk_ref[...] = _dot(mb, wmk_ref[...]).astype(MXU_DTYPE)
        mv_ref[...] = _dot(mb, wmv_ref[...]).astype(MXU_DTYPE)

    xb = x_ref[...].astype(MXU_DTYPE)
    p = _dot(xb, wm_ref[...])
    small = _dot(xb, ws_ref[...])

    o = 0
    cq = p[:, o:o + Q_RANK]; o += Q_RANK
    ckv = p[:, o:o + KV_RANK]; o += KV_RANK
    g_b = p[:, o:o + CONV_CH]; o += CONV_CH
    g_c = p[:, o:o + CONV_CH]; o += CONV_CH
    h_c = p[:, o:o + CONV_CH]; o += CONV_CH
    q_mem = p[:, o:o + MIX_C]

    cq = cq * lax.rsqrt(jnp.mean(cq * cq, axis=-1, keepdims=True) + RMS_EPS) * qg_ref[...]
    ckv = ckv * lax.rsqrt(jnp.mean(ckv * ckv, axis=-1, keepdims=True) + RMS_EPS) * kvg_ref[...]
    cq_ref[...] = cq.astype(MXU_DTYPE)
    ckv_b = ckv.astype(MXU_DTYPE)
    ckv_ref[...] = ckv_b
    ckvt_ref[0] = ckv.T.astype(MXU_DTYPE)

    kidx_ref[...] = small[:, :IDX_DIM].astype(MXU_DTYPE)
    small_t = small.T
    iwt_ref[0] = small_t[IDX_DIM:IDX_DIM + N_IDX_HEADS, :] * (N_IDX_HEADS ** -0.5 * IDX_DIM ** -0.5)

    u = g_c * h_c
    rows = lax.broadcasted_iota(jnp.int32, (tm, 1), 0)
    c6 = carry_ref[SUBLANES - 2:SUBLANES - 1, :]
    c7 = carry_ref[SUBLANES - 1:SUBLANES, :]
    u1 = jnp.where(rows == 0, c7, pltpu.roll(u, 1, 0))
    u2 = jnp.where(rows == 0, c6, jnp.where(rows == 1, c7, pltpu.roll(u, 2, 0)))
    y = cw_ref[0:1, :] * u2
    y = y + cw_ref[1:2, :] * u1
    y = y + cw_ref[2:3, :] * u
    yb_ref[...] = (g_b * y).astype(MXU_DTYPE)
    carry_ref[...] = u[tm - SUBLANES:, :]

    qm = q_mem.astype(MXU_DTYPE)
    outs = []
    for h in range(N_MEM_HEADS):
        sl = slice(h * HEAD_DIM, (h + 1) * HEAD_DIM)
        lg = _dot_nt(qm[:, sl], mk_ref[:, sl]) * (HEAD_DIM ** -0.5)
        lg = lg - jnp.max(lg, axis=-1, keepdims=True)
        e = jnp.exp(lg)
        pr = e / jnp.sum(e, axis=-1, keepdims=True)
        outs.append(_dot(pr.astype(MXU_DTYPE), mv_ref[:, sl]))
    yc_ref[...] = jnp.concatenate(outs, axis=-1).astype(MXU_DTYPE)


def _proj(x2, mem, w_main, w_small, q_g, kv_g, conv_w, w_mk, w_mv, B, S, tm):
    T, D = x2.shape
    n_mem = mem.shape[1]
    ns = S // tm
    row = lambda b, s: (b * ns + s, 0)
    const2 = lambda b, s: (0, 0)
    bf = MXU_DTYPE
    return pl.pallas_call(
        functools.partial(_proj_kernel, tm=tm),
        grid=(B, ns),
        in_specs=[
            pl.BlockSpec((tm, D), row),
            pl.BlockSpec((1, n_mem, D), lambda b, s: (b, 0, 0)),
            pl.BlockSpec(w_main.shape, const2),
            pl.BlockSpec(w_small.shape, const2),
            pl.BlockSpec(q_g.shape, const2),
            pl.BlockSpec(kv_g.shape, const2),
            pl.BlockSpec(conv_w.shape, const2),
            pl.BlockSpec(w_mk.shape, const2),
            pl.BlockSpec(w_mv.shape, const2),
        ],
        out_specs=[
            pl.BlockSpec((tm, Q_RANK), row),
            pl.BlockSpec((tm, KV_RANK), row),
            pl.BlockSpec((1, KV_RANK, tm), lambda b, s: (b, 0, s)),
            pl.BlockSpec((tm, IDX_DIM), row),
            pl.BlockSpec((1, N_IDX_HEADS, tm), lambda b, s: (b, 0, s)),
            pl.BlockSpec((tm, CONV_CH), row),
            pl.BlockSpec((tm, MIX_C), row),
        ],
        out_shape=[
            jax.ShapeDtypeStruct((T, Q_RANK), bf),
            jax.ShapeDtypeStruct((T, KV_RANK), bf),
            jax.ShapeDtypeStruct((B, KV_RANK, S), bf),
            jax.ShapeDtypeStruct((T, IDX_DIM), bf),
            jax.ShapeDtypeStruct((B, N_IDX_HEADS, S), jnp.float32),
            jax.ShapeDtypeStruct((T, CONV_CH), bf),
            jax.ShapeDtypeStruct((T, MIX_C), bf),
        ],
        scratch_shapes=[
            pltpu.VMEM((SUBLANES, CONV_CH), jnp.float32),
            pltpu.VMEM((n_mem, MIX_C), bf),
            pltpu.VMEM((n_mem, MIX_C), bf),
        ],
        compiler_params=_cparams(("arbitrary", "arbitrary")),
        name="proj",
    )(x2, mem, w_main, w_small, q_g, kv_g, conv_w, w_mk, w_mv)


def _key_to_f32(key):
    bits = jnp.where(key < 0, key ^ jnp.int32(0x7FFFFFFF), key)
    return pltpu.bitcast(bits, jnp.float32)


def _colsum8(v):
    return jnp.sum(v.reshape(QB // SUBLANES, SUBLANES, QB), axis=0)


def _colmax8(v):
    return jnp.max(v.reshape(QB // SUBLANES, SUBLANES, QB), axis=0)


UNROLL = 4


def _dsa_kernel(cq_ref, iwt_ref, kidx_ref, ckv_ref, ckvt_ref, wqi_ref, wuq_ref, wuk_ref, wuvt_ref,
                bias_ref, o_ref, qidx_ref, qlat_ref, score_ref, mask_ref, logit_ref, acc_ref,
                *, k_sel, idx_bits):
    i = pl.program_id(1)
    f32 = jnp.float32
    bf = MXU_DTYPE
    n_blocks = i + 1
    s_loc = lax.broadcasted_iota(jnp.int32, (QB, QB), 0)
    t_glob = i * QB + lax.broadcasted_iota(jnp.int32, (QB, QB), 1)

    def blk(jb):
        return pl.multiple_of(jb * QB, QB)

    def block_loop(fn, init):
        n_main = n_blocks // UNROLL
        c = lax.fori_loop(0, n_main, lambda it, c: fn(it * UNROLL, UNROLL, c), init)
        return lax.fori_loop(n_main * UNROLL, n_blocks, lambda jb, c: fn(jb, 1, c), c)

    cq = cq_ref[...]
    q_all = _dot(cq, wuq_ref[...]).astype(bf)
    for h in range(N_HEADS_A):
        qidx_ref[h * QB:(h + 1) * QB, :] = _dot(cq, wqi_ref[:, h * IDX_DIM:(h + 1) * IDX_DIM]).astype(bf)
        qlat_ref[h * QB:(h + 1) * QB, :] = (
            _dot_nt(q_all[:, h * HEAD_DIM:(h + 1) * HEAD_DIM], wuk_ref[h]) * (HEAD_DIM ** -0.5)).astype(bf)
    iw = iwt_ref[0]

    def score_body(jb0, nb, c):
        for sb in range(nb):
            off = blk(jb0 + sb)
            d_all = _dot_nt(kidx_ref[pl.ds(off, QB), :], qidx_ref[...])
            acc = jnp.maximum(d_all[:, 0:QB], 0.0) * iw[0:1, :]
            for h in range(1, N_IDX_HEADS):
                acc = acc + jnp.maximum(d_all[:, h * QB:(h + 1) * QB], 0.0) * iw[h:h + 1, :]
            score_ref[pl.ds(off, QB), :] = jnp.where(s_loc + off <= t_glob, acc + 0.0, F32_LOWEST)
        return c

    block_loop(score_body, 0)

    def count_where(pred):
        def body(jb0, nb, acc):
            for sb in range(nb):
                off = blk(jb0 + sb)
                acc = acc + _colsum8(jnp.where(pred(score_ref[pl.ds(off, QB), :], off), 1.0, 0.0))
            return acc
        acc = block_loop(body, jnp.zeros((SUBLANES, QB), f32))
        return jnp.sum(acc, axis=0, keepdims=True)

    kf = float(k_sel)

    def search():
        c0 = count_where(lambda sc, off: sc >= 0.0)
        cand0 = jnp.where(c0 >= kf, jnp.int32(0), jnp.int32(-2 ** 31))

        def bit_body(it, cand):
            trial = cand + lax.shift_left(jnp.int32(1), 30 - it)
            tf = _key_to_f32(trial)
            cnt = count_where(lambda sc, off: sc >= tf)
            return jnp.where(cnt >= kf, trial, cand)

        cand = lax.fori_loop(0, 31, bit_body, cand0)
        thr = _key_to_f32(cand)
        n_gt = count_where(lambda sc, off: sc > thr)
        n_eq = count_where(lambda sc, off: sc == thr)
        need = kf - n_gt

        def tie_search():
            def tbody(it, xcut):
                trial = xcut + lax.shift_left(jnp.int32(1), idx_bits - 1 - it)
                cnt = count_where(lambda sc, off: (sc == thr) & (s_loc + off < trial))
                return jnp.where(cnt < need, trial, xcut)
            return lax.fori_loop(0, idx_bits, tbody, jnp.zeros((1, QB), jnp.int32))

        any_extra = jnp.max(n_eq - need) > 0.0
        xcut = lax.cond(any_extra, tie_search, lambda: jnp.full((1, QB), 2 ** idx_bits - 1, jnp.int32))
        return thr, xcut

    def no_search():
        return jnp.full((1, QB), F32_LOWEST, f32), jnp.full((1, QB), 2 ** idx_bits - 1, jnp.int32)

    thr, xcut = lax.cond((i + 1) * QB > k_sel, search, no_search)

    def mask_body(jb0, nb, c):
        for sb in range(nb):
            off = blk(jb0 + sb)
            sc = score_ref[pl.ds(off, QB), :]
            s_glob = s_loc + off
            keep = ((sc > thr) | ((sc == thr) & (s_glob <= xcut))) & (s_glob <= t_glob)
            mask_ref[pl.ds(off, QB), :] = jnp.where(keep, 0.0, -jnp.inf)
        return c

    block_loop(mask_body, 0)

    def p1_body(jb0, nb, m8):
        m8 = list(m8)
        for sb in range(nb):
            off = blk(jb0 + sb)
            lg = _dot_nt(ckv_ref[pl.ds(off, QB), :], qlat_ref[...])
            msk = mask_ref[pl.ds(off, QB), :]
            bsel = jnp.clip(jb0 + sb - i + 2, 0, 2)
            for h in range(N_HEADS_A):
                lgh = lg[:, h * QB:(h + 1) * QB] + bias_ref[bsel, h] + msk
                logit_ref[pl.ds(off, QB), h * QB:(h + 1) * QB] = lgh
                m8[h] = jnp.maximum(m8[h], _colmax8(lgh))
        return tuple(m8)

    m8 = block_loop(p1_body, tuple(jnp.full((SUBLANES, QB), -jnp.inf, f32) for _ in range(N_HEADS_A)))
    m_row = [jnp.max(m, axis=0, keepdims=True) for m in m8]

    acc_ref[...] = jnp.zeros_like(acc_ref)

    def p2_body(jb0, nb, l8):
        l8 = list(l8)
        off = blk(jb0)
        rows = nb * QB
        ps = []
        for h in range(N_HEADS_A):
            p = jnp.exp(logit_ref[pl.ds(off, rows), h * QB:(h + 1) * QB] - m_row[h])
            l8[h] = l8[h] + jnp.sum(p.reshape(rows // SUBLANES, SUBLANES, QB), axis=0)
            ps.append(p.astype(bf))
        acc_ref[...] += _dot(ckvt_ref[0, :, pl.ds(off, rows)], jnp.concatenate(ps, axis=1))
        return tuple(l8)

    l8 = block_loop(p2_body, tuple(jnp.zeros((SUBLANES, QB), f32) for _ in range(N_HEADS_A)))

    outs = []
    for h in range(N_HEADS_A):
        l_row = jnp.sum(l8[h], axis=0, keepdims=True)
        o_lat_t = (acc_ref[:, h * QB:(h + 1) * QB] / l_row).astype(bf)
        outs.append(_dot(wuvt_ref[h], o_lat_t))
    o_ref[...] = jnp.concatenate(outs, axis=0).T.astype(o_ref.dtype)


def _dsa(cq, iwt, kidx, ckv, ckvt, w_qidx, w_uq, w_uk_h, w_uvt_h, bias_tiles, B, S):
    T = cq.shape[0]
    assert S % QB == 0 and QB >= REL_MAX_DIST
    nq = S // QB
    k_sel = min(TOPK_MAX, S // 4)
    idx_bits = max(1, (S - 1).bit_length())
    c2 = lambda b, i: (0, 0)
    c3 = lambda b, i: (0, 0, 0)
    return pl.pallas_call(
        functools.partial(_dsa_kernel, k_sel=k_sel, idx_bits=idx_bits),
        grid=(B, nq),
        in_specs=[
            pl.BlockSpec((QB, Q_RANK), lambda b, i: (b * nq + i, 0)),
            pl.BlockSpec((1, N_IDX_HEADS, QB), lambda b, i: (b, 0, i)),
            pl.BlockSpec((S, IDX_DIM), lambda b, i: (b, 0)),
            pl.BlockSpec((S, KV_RANK), lambda b, i: (b, 0)),
            pl.BlockSpec((1, KV_RANK, S), lambda b, i: (b, 0, 0)),
            pl.BlockSpec(w_qidx.shape, c2),
            pl.BlockSpec(w_uq.shape, c2),
            pl.BlockSpec(w_uk_h.shape, c3),
            pl.BlockSpec(w_uvt_h.shape, c3),
            pl.BlockSpec(bias_tiles.shape, lambda b, i: (0, 0, 0, 0)),
        ],
        out_specs=pl.BlockSpec((QB, MIX_A), lambda b, i: (b * nq + i, 0)),
        out_shape=jax.ShapeDtypeStruct((T, MIX_A), MXU_DTYPE),
        scratch_shapes=[
            pltpu.VMEM((N_IDX_HEADS * QB, IDX_DIM), MXU_DTYPE),
            pltpu.VMEM((N_HEADS_A * QB, KV_RANK), MXU_DTYPE),
            pltpu.VMEM((S, QB), jnp.float32),
            pltpu.VMEM((S, QB), jnp.float32),
            pltpu.VMEM((S, N_HEADS_A * QB), jnp.float32),
            pltpu.VMEM((KV_RANK, N_HEADS_A * QB), jnp.float32),
        ],
        compiler_params=_cparams(("arbitrary", "arbitrary")),
        name="dsa",
    )(cq, iwt, kidx, ckv, ckvt, w_qidx, w_uq, w_uk_h, w_uvt_h, bias_tiles)


def _layer_norm(xf, g, b):
    mu = jnp.mean(xf, axis=-1, keepdims=True)
    xc = xf - mu
    var = jnp.mean(xc * xc, axis=-1, keepdims=True)
    return xc * lax.rsqrt(var + LN_EPS) * g + b


def _rank_rows(v, n):
    ri = lax.broadcasted_iota(jnp.int32, v.shape, 0)
    rank = jnp.zeros(v.shape, jnp.float32)
    for r2 in range(n):
        row = v[r2:r2 + 1, :]
        beats = (row > v) | ((row == v) & (ri > r2))
        rank = rank + jnp.where(beats, 1.0, 0.0)
    return rank


def _pack_factor():
    return 4 // jnp.dtype(MXU_DTYPE).itemsize


def _pack_rows(x):
    if _pack_factor() == 1:
        return pltpu.bitcast(x, jnp.int32)
    half = x.shape[1] // 2
    b = pltpu.bitcast(x.astype(MXU_DTYPE).astype(jnp.float32), jnp.int32)
    return b[:, half:] | lax.shift_right_logical(b[:, :half], jnp.int32(16))


_HIGH_HALF = -(1 << 16)


def _unpack_rows_f32(p):
    if _pack_factor() == 1:
        return [pltpu.bitcast(p, jnp.float32)]
    lo = pltpu.bitcast(lax.shift_left(p, jnp.int32(16)), jnp.float32)
    hi = pltpu.bitcast(p & jnp.int32(_HIGH_HALF), jnp.float32)
    return [lo, hi]


def _unpack_rows(p):
    return [v.astype(MXU_DTYPE) for v in _unpack_rows_f32(p)]


def _mix_router_kernel(x_ref, ya_ref, yb_ref, yc_ref, wo_ref, g_ref, b_ref, wrt_ref, rb_ref, exp_ref,
                       x1_ref, x1p_ref, sel_ref, w_ref, pos_ref, cnt_ref, base_ref, *, tm):
    step = pl.program_id(0)
    f32 = jnp.float32

    @pl.when(step == 0)
    def _():
        base_ref[...] = jnp.zeros_like(base_ref)

    mix = _dot(ya_ref[...], wo_ref[0:MIX_A, :])
    mix = mix + _dot(yb_ref[...], wo_ref[MIX_A:MIX_A + CONV_CH, :])
    mix = mix + _dot(yc_ref[...], wo_ref[MIX_A + CONV_CH:, :])
    x1 = _layer_norm(ALPHA * x_ref[...] + mix, g_ref[...], b_ref[...])
    x1_ref[...] = x1
    x1p_ref[...] = _pack_rows(x1)

    lg = lax.dot_general(wrt_ref[...], x1, _NT, precision=lax.Precision.HIGHEST, preferred_element_type=f32)
    s = 1.0 / (1.0 + jnp.exp(-lg))
    sc = s + rb_ref[...]

    g3 = sc.reshape(N_GROUPS, GROUP_SIZE, tm)
    m1 = jnp.max(g3, axis=1, keepdims=True)
    is_m1 = g3 == m1
    n_m1 = jnp.sum(jnp.where(is_m1, 1.0, 0.0), axis=1, keepdims=True)
    m2 = jnp.max(jnp.where(is_m1, -jnp.inf, g3), axis=1, keepdims=True)
    gscore = (m1 + jnp.where(n_m1 > 1.0, m1, m2)).reshape(N_GROUPS, tm)
    gsel = jnp.where(_rank_rows(gscore, N_GROUPS) < float(TOPK_GROUPS), 1.0, 0.0)
    emask = _dot(exp_ref[...], gsel.astype(MXU_DTYPE)) > 0.5
    masked = jnp.where(emask, sc, -jnp.inf)
    sel = (_rank_rows(masked, N_EXPERTS) < float(TOP_K)) & emask
    self_ = jnp.where(sel, 1.0, 0.0)
    top_s = jnp.where(sel, s, 0.0)
    w = top_s / jnp.sum(top_s, axis=0, keepdims=True) * ROUTED_SCALE

    t_r = lax.broadcasted_iota(jnp.int32, (tm, tm), 0)
    t_c = lax.broadcasted_iota(jnp.int32, (tm, tm), 1)
    upper = jnp.where(t_r < t_c, 1.0, 0.0).astype(MXU_DTYPE)
    pref = _dot(self_.astype(MXU_DTYPE), upper)
    base = base_ref[...]
    sel_ref[...] = self_
    w_ref[...] = w
    pos_ref[...] = base + pref
    base = base + jnp.sum(self_, axis=1, keepdims=True)
    base_ref[...] = base
    cnt_ref[...] = jnp.broadcast_to(base, cnt_ref.shape)


def _mix_router(x2, ya, yb, yc, w_out, ln_g, ln_b, w_router_t, router_bias, tm):
    T, D = x2.shape
    E = N_EXPERTS
    expand = (jnp.arange(E)[:, None] // GROUP_SIZE == jnp.arange(N_GROUPS)[None, :]).astype(MXU_DTYPE)
    row = lambda i: (i, 0)
    col = lambda i: (0, i)
    c2 = lambda i: (0, 0)
    f32 = jnp.float32
    return pl.pallas_call(
        functools.partial(_mix_router_kernel, tm=tm),
        grid=(T // tm,),
        in_specs=[
            pl.BlockSpec((tm, D), row),
            pl.BlockSpec((tm, MIX_A), row),
            pl.BlockSpec((tm, CONV_CH), row),
            pl.BlockSpec((tm, MIX_C), row),
            pl.BlockSpec(w_out.shape, c2),
            pl.BlockSpec((1, D), c2),
            pl.BlockSpec((1, D), c2),
            pl.BlockSpec((E, D), c2),
            pl.BlockSpec((E, 1), c2),
            pl.BlockSpec((E, N_GROUPS), c2),
        ],
        out_specs=[
            pl.BlockSpec((tm, D), row),
            pl.BlockSpec((tm, D // _pack_factor()), row),
            pl.BlockSpec((E, tm), col),
            pl.BlockSpec((E, tm), col),
            pl.BlockSpec((E, tm), col),
            pl.BlockSpec((E, LANES), c2),
        ],
        out_shape=[
            jax.ShapeDtypeStruct((T, D), f32),
            jax.ShapeDtypeStruct((T, D // _pack_factor()), jnp.int32),
            jax.ShapeDtypeStruct((E, T), f32),
            jax.ShapeDtypeStruct((E, T), f32),
            jax.ShapeDtypeStruct((E, T), f32),
            jax.ShapeDtypeStruct((E, LANES), f32),
        ],
        scratch_shapes=[pltpu.VMEM((E, 1), f32)],
        compiler_params=_cparams(("arbitrary",)),
        name="mix_router",
    )(x2, ya, yb, yc, w_out, ln_g, ln_b, w_router_t, router_bias, expand)


def _compact_kernel(sel_ref, w_ref, pos_ref, pstart_ref, low_ref, dest_ref, wk_ref):
    sel = sel_ref[...]
    on = sel > 0.5
    rank = _dot(low_ref[...], sel.astype(MXU_DTYPE))
    row = pstart_ref[...] + pos_ref[...]
    w = w_ref[...]
    dests, ws = [], []
    for k in range(TOP_K):
        m = on & (rank == float(k))
        dests.append(jnp.sum(jnp.where(m, row, 0.0), axis=0, keepdims=True))
        ws.append(jnp.sum(jnp.where(m, w, 0.0), axis=0, keepdims=True))
    dest_ref[...] = jnp.concatenate(dests, axis=0).astype(jnp.int32)
    wk_ref[...] = jnp.concatenate(ws, axis=0)


def _compact(sel_t, w_t, pos_t, pad_start, tm):
    E, T = sel_t.shape
    lower = (jnp.arange(E)[None, :] < jnp.arange(E)[:, None]).astype(MXU_DTYPE)
    col = lambda i: (0, i)
    c2 = lambda i: (0, 0)
    return pl.pallas_call(
        _compact_kernel,
        grid=(T // tm,),
        in_specs=[pl.BlockSpec((E, tm), col), pl.BlockSpec((E, tm), col), pl.BlockSpec((E, tm), col),
                  pl.BlockSpec((E, 1), c2), pl.BlockSpec((E, E), c2)],
        out_specs=[pl.BlockSpec((TOP_K, tm), col), pl.BlockSpec((TOP_K, tm), col)],
        out_shape=[jax.ShapeDtypeStruct((TOP_K, T), jnp.int32), jax.ShapeDtypeStruct((TOP_K, T), jnp.float32)],
        compiler_params=_cparams(("arbitrary",)),
        name="route_compact",
    )(sel_t, w_t, pos_t, pad_start, lower)


def _row_copy(src, s, dst, d, sem):
    return pltpu.make_async_copy(src.at[pl.ds(s, 1)], dst.at[pl.ds(d, 1)], sem)


def _dispatch_kernel(flo_ref, fhi_ref, dest_ref, x_ref, xs_hbm, zero_ref, sem, zsem, *, td):
    step = pl.program_id(0)

    @pl.when(step == 0)
    def _():
        zero_ref[...] = jnp.zeros_like(zero_ref)

        def per_expert(fn):
            def ebody(e, c):
                lax.fori_loop(flo_ref[e], fhi_ref[e], lambda r, c2: (fn(r), c2)[1], 0)
                return c
            lax.fori_loop(0, N_EXPERTS, ebody, 0)

        per_expert(lambda r: _row_copy(zero_ref, 0, xs_hbm, r, zsem).start())
        per_expert(lambda r: _row_copy(zero_ref, 0, xs_hbm, r, zsem).wait())

    def issue(r, c):
        for k in range(TOP_K):
            _row_copy(x_ref, r, xs_hbm, dest_ref[k, r], sem).start()
        return c

    def drain(r, c):
        for k in range(TOP_K):
            _row_copy(x_ref, r, xs_hbm, dest_ref[k, r], sem).wait()
        return c

    lax.fori_loop(0, td, issue, 0)
    lax.fori_loop(0, td, drain, 0)


def _dispatch(dest_t, x1p, fill_lo, fill_hi, n_rows, td):
    T, W = x1p.shape
    return pl.pallas_call(
        functools.partial(_dispatch_kernel, td=td),
        grid_spec=pltpu.PrefetchScalarGridSpec(
            num_scalar_prefetch=2,
            grid=(T // td,),
            in_specs=[
                pl.BlockSpec((TOP_K, td), lambda i, lo, hi: (0, i), memory_space=pltpu.SMEM),
                pl.BlockSpec((td, W), lambda i, lo, hi: (i, 0)),
            ],
            out_specs=pl.BlockSpec(memory_space=pl.ANY),
            scratch_shapes=[pltpu.VMEM((SUBLANES, W), x1p.dtype),
                            pltpu.SemaphoreType.DMA, pltpu.SemaphoreType.DMA],
        ),
        out_shape=jax.ShapeDtypeStruct((n_rows, W), x1p.dtype),
        compiler_params=_cparams(("arbitrary",)),
        name="dispatch",
    )(fill_lo, fill_hi, dest_t, x1p)


def _silu(g):
    return g / (1.0 + jnp.exp(-g))


def _expert_kernel(be_ref, nu_ref, xs_ref, wg_ref, wu_ref, wd_ref, ys_ref):
    @pl.when(pl.program_id(0) < nu_ref[0])
    def _():
        parts = _unpack_rows(xs_ref[...])
        dk = wg_ref.shape[1] // len(parts)

        def proj(w_ref):
            acc = _dot(parts[0], w_ref[0, 0:dk, :])
            for n in range(1, len(parts)):
                acc = acc + _dot(parts[n], w_ref[0, n * dk:(n + 1) * dk, :])
            return acc

        a = (_silu(proj(wg_ref)) * proj(wu_ref)).astype(MXU_DTYPE)
        ys_ref[...] = _pack_rows(_dot(a, wd_ref[0]))


def _experts(xs, block_e, n_used, w_gate, w_up, w_down):
    n_rows, W = xs.shape
    D = w_gate.shape[1]
    n_blocks = n_rows // ROW_BLOCK
    blk = lambda i, be, nu: (jnp.minimum(i, nu[0] - 1), 0)
    wsel = lambda i, be, nu: (be[i], 0, 0)
    return pl.pallas_call(
        _expert_kernel,
        grid_spec=pltpu.PrefetchScalarGridSpec(
            num_scalar_prefetch=2,
            grid=(n_blocks,),
            in_specs=[
                pl.BlockSpec((ROW_BLOCK, W), blk),
                pl.BlockSpec((1, D, D_EXPERT), wsel),
                pl.BlockSpec((1, D, D_EXPERT), wsel),
                pl.BlockSpec((1, D_EXPERT, D), wsel),
            ],
            out_specs=pl.BlockSpec((ROW_BLOCK, W), blk),
        ),
        out_shape=jax.ShapeDtypeStruct((n_rows, W), xs.dtype),
        compiler_params=_cparams(("arbitrary",)),
        name="experts",
    )(block_e, n_used, xs, w_gate, w_up, w_down)


SC_CORES = 2
SC_SUBCORES = 16
SC_GATHER_ROWS = 64


def _sc_gather_rows(table, idx):
    n = idx.shape[0]
    w = table.shape[1]
    n_workers = SC_CORES * SC_SUBCORES
    per_worker = n // n_workers
    assert n % n_workers == 0 and per_worker % SC_GATHER_ROWS == 0
    mesh = plsc.VectorSubcoreMesh(core_axis_name="c", subcore_axis_name="s")

    @functools.partial(
        pl.kernel, mesh=mesh,
        out_type=jax.ShapeDtypeStruct((n, w), table.dtype),
        scratch_types=[
            pltpu.VMEM((SC_GATHER_ROWS,), jnp.int32),
            pltpu.VMEM((SC_GATHER_ROWS, w), table.dtype),
            pltpu.SemaphoreType.DMA,
        ],
        name="sc_gather_rows",
    )
    def gather(table_hbm, idx_hbm, out_hbm, idx_v, rows_v, sem):
        wid = lax.axis_index("s") * SC_CORES + lax.axis_index("c")
        base = wid * per_worker

        @pl.loop(0, per_worker // SC_GATHER_ROWS)
        def _(g):
            off = base + g * SC_GATHER_ROWS
            pltpu.sync_copy(idx_hbm.at[pl.ds(off, SC_GATHER_ROWS)], idx_v)
            pltpu.async_copy(table_hbm.at[idx_v], rows_v, sem).wait()
            pltpu.sync_copy(rows_v, out_hbm.at[pl.ds(off, SC_GATHER_ROWS)])

    return gather(table, idx)


SC_SCATTER_ROWS = 64


def _sc_scatter_rows(rows, idx3, n_out):
    n_src, w = rows.shape
    n_chunks, n_dst, batch = idx3.shape
    n_workers = SC_CORES * SC_SUBCORES
    assert batch == SC_SCATTER_ROWS and n_chunks * batch == n_src and n_chunks % n_workers == 0
    assert n_src * n_dst == n_out
    per_worker = n_chunks // n_workers
    mesh = plsc.VectorSubcoreMesh(core_axis_name="c", subcore_axis_name="s")

    @functools.partial(
        pl.kernel, mesh=mesh,
        out_type=jax.ShapeDtypeStruct((n_out, w), rows.dtype),
        scratch_types=[
            pltpu.VMEM((n_dst, batch), jnp.int32),
            pltpu.VMEM((batch, w), rows.dtype),
        ],
        name="sc_scatter_rows",
    )
    def scatter(rows_hbm, idx_hbm, out_hbm, idx_v, rows_v):
        wid = lax.axis_index("s") * SC_CORES + lax.axis_index("c")

        @pl.loop(0, per_worker)
        def _(g):
            c = wid * per_worker + g
            pltpu.sync_copy(idx_hbm.at[c], idx_v)
            pltpu.sync_copy(rows_hbm.at[pl.ds(c * batch, batch)], rows_v)
            for k in range(n_dst):
                pltpu.sync_copy(rows_v, out_hbm.at[idx_v.at[k]])

    return scatter(rows, idx3)


def _combine2_kernel(wk_ref, x1_ref, g_ref_rows, wsg_ref, wsu_ref, wsd_ref, g_ref, b_ref, o_ref):
    x1 = x1_ref[...]
    xb = x1.astype(MXU_DTYPE)
    a = (_silu(_dot(xb, wsg_ref[...])) * _dot(xb, wsu_ref[...])).astype(MXU_DTYPE)
    shared = _dot(a, wsd_ref[...])
    wk = wk_ref[...]
    groups = [wk[:, 0:1] * v for v in _unpack_rows_f32(g_ref_rows[0])]
    for k in range(1, TOP_K):
        groups = [g + wk[:, k:k + 1] * v for g, v in zip(groups, _unpack_rows_f32(g_ref_rows[k]))]
    routed = jnp.concatenate(groups, axis=1)
    o_ref[...] = _layer_norm(ALPHA * x1 + (routed + shared), g_ref[...], b_ref[...])


def _combine2(wk, x1, gathered, w_sg, w_su, w_sd, ln_g, ln_b, tc):
    T, D = x1.shape
    W = gathered.shape[2]
    row = lambda i: (i, 0)
    c2 = lambda i: (0, 0)
    return pl.pallas_call(
        _combine2_kernel,
        grid=(T // tc,),
        in_specs=[
            pl.BlockSpec((tc, TOP_K), row),
            pl.BlockSpec((tc, D), row),
            pl.BlockSpec((TOP_K, tc, W), lambda i: (0, i, 0)),
            pl.BlockSpec(w_sg.shape, c2),
            pl.BlockSpec(w_su.shape, c2),
            pl.BlockSpec(w_sd.shape, c2),
            pl.BlockSpec((1, D), c2),
            pl.BlockSpec((1, D), c2),
        ],
        out_specs=pl.BlockSpec((tc, D), row),
        out_shape=jax.ShapeDtypeStruct((T, D), jnp.float32),
        compiler_params=_cparams(("arbitrary",)),
        name="combine",
    )(wk, x1, gathered, w_sg, w_su, w_sd, ln_g, ln_b)


def _combine_kernel(dest_ref, wk_ref, x1_ref, ys_hbm, wsg_ref, wsu_ref, wsd_ref, g_ref, b_ref,
                    o_ref, buf_ref, sem, *, tc):
    def issue(r, c):
        for k in range(TOP_K):
            _row_copy(ys_hbm, dest_ref[k, r], buf_ref.at[k], r, sem).start()
        return c

    def drain(r, c):
        for k in range(TOP_K):
            _row_copy(ys_hbm, dest_ref[k, r], buf_ref.at[k], r, sem).wait()
        return c

    lax.fori_loop(0, tc, issue, 0)
    x1 = x1_ref[...]
    xb = x1.astype(MXU_DTYPE)
    a = (_silu(_dot(xb, wsg_ref[...])) * _dot(xb, wsu_ref[...])).astype(MXU_DTYPE)
    shared = _dot(a, wsd_ref[...])
    lax.fori_loop(0, tc, drain, 0)
    wk = wk_ref[...]
    groups = [wk[:, 0:1] * v for v in _unpack_rows_f32(buf_ref[0])]
    for k in range(1, TOP_K):
        groups = [g + wk[:, k:k + 1] * v for g, v in zip(groups, _unpack_rows_f32(buf_ref[k]))]
    routed = jnp.concatenate(groups, axis=1)
    o_ref[...] = _layer_norm(ALPHA * x1 + (routed + shared), g_ref[...], b_ref[...])


def _combine(dest_t, wk, x1, ys, w_sg, w_su, w_sd, ln_g, ln_b, tc):
    T, D = x1.shape
    row = lambda i: (i, 0)
    c2 = lambda i: (0, 0)
    return pl.pallas_call(
        functools.partial(_combine_kernel, tc=tc),
        grid=(T // tc,),
        in_specs=[
            pl.BlockSpec((TOP_K, tc), lambda i: (0, i), memory_space=pltpu.SMEM),
            pl.BlockSpec((tc, TOP_K), row),
            pl.BlockSpec((tc, D), row),
            pl.BlockSpec(memory_space=pl.ANY),
            pl.BlockSpec(w_sg.shape, c2),
            pl.BlockSpec(w_su.shape, c2),
            pl.BlockSpec(w_sd.shape, c2),
            pl.BlockSpec((1, D), c2),
            pl.BlockSpec((1, D), c2),
        ],
        out_specs=pl.BlockSpec((tc, D), row),
        out_shape=jax.ShapeDtypeStruct((T, D), jnp.float32),
        scratch_shapes=[pltpu.VMEM((TOP_K, tc, ys.shape[1]), ys.dtype), pltpu.SemaphoreType.DMA],
        compiler_params=_cparams(("arbitrary",)),
        name="combine",
    )(dest_t, wk, x1, ys, w_sg, w_su, w_sd, ln_g, ln_b)


def _split_w_in(w_in):
    bf = MXU_DTYPE
    o_kv = Q_RANK
    o_ki = o_kv + KV_RANK
    o_iw = o_ki + IDX_DIM
    o_rest = o_iw + N_IDX_HEADS
    w_main = jnp.concatenate([w_in[:, :o_ki], w_in[:, o_rest:]], axis=1).astype(bf)
    w_small = jnp.pad(w_in[:, o_ki:o_rest], ((0, 0), (0, LANES - IDX_DIM - N_IDX_HEADS))).astype(bf)
    return w_main, w_small


def _stages(x, mem, w_in, q_norm_g, kv_norm_g, w_uq, w_uk, w_uv, w_qidx, rel_bias, conv_w, w_mem_k, w_mem_v, w_out, ln1_g, ln1_b, w_router, router_bias, w_e_gate, w_e_up, w_e_down, w_s_gate, w_s_up, w_s_down, ln2_g, ln2_b, upto=None):
    B, S, D = x.shape
    T = B * S
    bf = MXU_DTYPE
    l = 0
    res = {}
    x2 = x.reshape(T, D)
    w_main, w_small = _split_w_in(w_in[l])
    cq, ckv, ckvt, kidx, iwt, yb, yc = _proj(
        x2, mem, w_main, w_small, q_norm_g[l].reshape(1, -1), kv_norm_g[l].reshape(1, -1), conv_w[l],
        w_mem_k[l].astype(bf), w_mem_v[l].astype(bf), B, S, tm=min(512, S))
    res.update(c_q=cq, c_kv=ckv, k_idx=kidx, y_b=yb, y_c=yc,
               idx_w=jnp.swapaxes(iwt, 1, 2) / (N_IDX_HEADS ** -0.5 * IDX_DIM ** -0.5))
    if upto == "proj":
        return res
    bias_t = _bias_tiles(rel_bias)
    ya = _dsa(cq, iwt, kidx, ckv, ckvt,
              w_qidx[l].reshape(Q_RANK, -1).astype(bf), w_uq[l].reshape(Q_RANK, -1).astype(bf),
              jnp.transpose(w_uk[l], (1, 0, 2)).astype(bf), jnp.transpose(w_uv[l], (1, 2, 0)).astype(bf),
              bias_t, B, S)
    res.update(y_a=ya)
    if upto == "dsa":
        return res

    x1, x1p, sel_t, w_t, pos_t, cnt = _mix_router(
        x2, ya, yb, yc, w_out[l].astype(bf), ln1_g[l].reshape(1, -1), ln1_b[l].reshape(1, -1),
        w_router[l].T, router_bias[l].reshape(-1, 1), tm=min(512, T))
    res.update(x1=x1)

    counts = cnt[:, 0].astype(jnp.int32)
    padded = (counts + ROW_BLOCK - 1) // ROW_BLOCK * ROW_BLOCK
    pad_end = jnp.cumsum(padded)
    pad_start = pad_end - padded
    n_blocks = -(-(T * TOP_K) // ROW_BLOCK) + N_EXPERTS
    n_rows = n_blocks * ROW_BLOCK
    block_start = jnp.arange(n_blocks, dtype=jnp.int32) * ROW_BLOCK
    block_e = jnp.minimum(jnp.sum((pad_end[None, :] <= block_start[:, None]).astype(jnp.int32), axis=1),
                          N_EXPERTS - 1)
    n_used = (pad_end[-1:] // ROW_BLOCK).astype(jnp.int32)

    dest_t, wk_t = _compact(sel_t, w_t, pos_t, pad_start.astype(jnp.float32).reshape(-1, 1), tm=min(512, T))
    n_pad = n_rows - T * TOP_K
    gap = padded - counts
    cum_gap = jnp.cumsum(gap)
    j = jnp.arange(n_pad, dtype=jnp.int32)
    e_j = jnp.sum((cum_gap[None, :] <= j[:, None]).astype(jnp.int32), axis=1)
    e_c = jnp.minimum(e_j, N_EXPERTS - 1)
    in_expert = (pad_start + counts)[e_c] + (j - (cum_gap - gap)[e_c])
    pad_rows = jnp.where(e_j < N_EXPERTS, in_expert, pad_end[-1] + (j - cum_gap[-1])).astype(jnp.int32)
    bt = SC_SCATTER_ROWS
    idx3 = jnp.concatenate([
        jnp.transpose(dest_t.reshape(TOP_K, T // bt, bt), (1, 0, 2)),
        pad_rows.reshape(n_pad // (TOP_K * bt), TOP_K, bt)], axis=0)
    rows_ext = jnp.concatenate([x1p, jnp.zeros((n_pad // TOP_K, x1p.shape[1]), x1p.dtype)], axis=0)
    xs = _sc_scatter_rows(rows_ext, idx3, n_rows)
    ys = _experts(xs, block_e, n_used, w_e_gate[l].astype(bf), w_e_up[l].astype(bf), w_e_down[l].astype(bf))
    gathered = _sc_gather_rows(ys, dest_t.reshape(-1)).reshape(TOP_K, T, -1)
    out = _combine2(wk_t.T, x1, gathered, w_s_gate[l].astype(bf), w_s_up[l].astype(bf), w_s_down[l].astype(bf),
                    ln2_g[l].reshape(1, -1), ln2_b[l].reshape(1, -1), tc=min(256, T))
    res.update(out=out.reshape(B, S, D))
    return res


def kernel(x, mem, w_in, q_norm_g, kv_norm_g, w_uq, w_uk, w_uv, w_qidx, rel_bias, conv_w, w_mem_k, w_mem_v, w_out, ln1_g, ln1_b, w_router, router_bias, w_e_gate, w_e_up, w_e_down, w_s_gate, w_s_up, w_s_down, ln2_g, ln2_b):
    return _stages(x, mem, w_in, q_norm_g, kv_norm_g, w_uq, w_uk, w_uv, w_qidx, rel_bias, conv_w, w_mem_k, w_mem_v, w_out, ln1_g, ln1_b, w_router, router_bias, w_e_gate, w_e_up, w_e_down, w_s_gate, w_s_up, w_s_down, ln2_g, ln2_b)["out"]
```

```python
import functools
import math

import jax
import jax.numpy as jnp
from jax import lax
from jax.experimental import pallas as pl
from jax.experimental.pallas import tpu as pltpu
from jax.experimental.pallas import tpu_sc as plsc

N_HEADS_A = 8
HEAD_DIM = 64
Q_RANK = 256
KV_RANK = 128
N_IDX_HEADS = 8
IDX_DIM = 64
TOPK_MAX = 256
REL_BUCKETS = 32
REL_MAX_DIST = 128
CONV_CH = 256
CONV_WIDTH = 3
N_MEM_HEADS = 4
MIX_A = N_HEADS_A * HEAD_DIM
MIX_C = N_MEM_HEADS * HEAD_DIM
N_EXPERTS = 64
N_GROUPS = 8
GROUP_SIZE = N_EXPERTS // N_GROUPS
TOPK_GROUPS = 4
TOP_K = 8
D_EXPERT = 256
ROUTED_SCALE = 2.5
MOE_BLOCK = 256
DEPTH = 1
ALPHA = (2.0 * DEPTH) ** 0.25
LN_EPS = 1e-5
RMS_EPS = 1e-6

LANES = 128
SUBLANES = 8
QB = 128
F32_LOWEST = -3.4028234663852886e38
VMEM_LIMIT = 56 * 1024 * 1024
MXU_DTYPE = jnp.bfloat16
ROW_BLOCK = 512

_NT = (((1,), (1,)), ((), ()))


def _dot(a, b):
    return jnp.dot(a, b, preferred_element_type=jnp.float32)


def _dot_nt(a, b):
    return lax.dot_general(a, b, _NT, preferred_element_type=jnp.float32)


def _cparams(sem):
    return pltpu.CompilerParams(dimension_semantics=sem, vmem_limit_bytes=VMEM_LIMIT)


def _bias_kernel(rb_ref, o_ref):
    s = lax.broadcasted_iota(jnp.int32, (QB, QB), 0)
    t = lax.broadcasted_iota(jnp.int32, (QB, QB), 1)
    max_exact = REL_BUCKETS // 2
    for tile in range(3):
        n = jnp.maximum(t - s + (2 - tile) * QB, 0)
        nf = jnp.maximum(n.astype(jnp.float32), 1.0)
        large = max_exact + (jnp.log(nf / max_exact) / math.log(REL_MAX_DIST / max_exact)
                             * (REL_BUCKETS - max_exact)).astype(jnp.int32)
        large = jnp.minimum(large, REL_BUCKETS - 1)
        bucket = jnp.where(n < max_exact, n, large)
        for h in range(N_HEADS_A):
            acc = jnp.zeros((QB, QB), jnp.float32)
            for b in range(REL_BUCKETS):
                acc = jnp.where(bucket == b, rb_ref[b, h], acc)
            o_ref[tile, h] = acc


def _bias_tiles(rel_bias):
    return pl.pallas_call(
        _bias_kernel,
        in_specs=[pl.BlockSpec(memory_space=pltpu.SMEM)],
        out_specs=pl.BlockSpec(memory_space=pltpu.VMEM),
        out_shape=jax.ShapeDtypeStruct((3, N_HEADS_A, QB, QB), jnp.float32),
        name="bias_tiles",
    )(rel_bias)


_MAIN_COLS = Q_RANK + KV_RANK + 3 * CONV_CH + MIX_C


def _proj_kernel(x_ref, mem_ref, wm_ref, ws_ref, qg_ref, kvg_ref, cw_ref, wmk_ref, wmv_ref,
                 cq_ref, ckv_ref, ckvt_ref, kidx_ref, iwt_ref, yb_ref, yc_ref,
                 carry_ref, mk_ref, mv_ref, *, tm):
    si = pl.program_id(1)

    @pl.when(si == 0)
    def _():
        carry_ref[...] = jnp.zeros_like(carry_ref)
        mb = mem_ref[0].astype(MXU_DTYPE)
        mk_ref[...] = _dot(mb, wmk_ref[...]).astype(MXU_DTYPE)
        mv_ref[...] = _dot(mb, wmv_ref[...]).astype(MXU_DTYPE)

    xb = x_ref[...].astype(MXU_DTYPE)
    p = _dot(xb, wm_ref[...])
    small = _dot(xb, ws_ref[...])

    o = 0
    cq = p[:, o:o + Q_RANK]; o += Q_RANK
    ckv = p[:, o:o + KV_RANK]; o += KV_RANK
    g_b = p[:, o:o + CONV_CH]; o += CONV_CH
    g_c = p[:, o:o + CONV_CH]; o += CONV_CH
    h_c = p[:, o:o + CONV_CH]; o += CONV_CH
    q_mem = p[:, o:o + MIX_C]

    cq = cq * lax.rsqrt(jnp.mean(cq * cq, axis=-1, keepdims=True) + RMS_EPS) * qg_ref[...]
    ckv = ckv * lax.rsqrt(jnp.mean(ckv * ckv, axis=-1, keepdims=True) + RMS_EPS) * kvg_ref[...]
    cq_ref[...] = cq.astype(MXU_DTYPE)
    ckv_b = ckv.astype(MXU_DTYPE)
    ckv_ref[...] = ckv_b
    ckvt_ref[0] = ckv.T.astype(MXU_DTYPE)

    kidx_ref[...] = small[:, :IDX_DIM].astype(MXU_DTYPE)
    small_t = small.T
    iwt_ref[0] = small_t[IDX_DIM:IDX_DIM + N_IDX_HEADS, :] * (N_IDX_HEADS ** -0.5 * IDX_DIM ** -0.5)

    u = g_c * h_c
    rows = lax.broadcasted_iota(jnp.int32, (tm, 1), 0)
    c6 = carry_ref[SUBLANES - 2:SUBLANES - 1, :]
    c7 = carry_ref[SUBLANES - 1:SUBLANES, :]
    u1 = jnp.where(rows == 0, c7, pltpu.roll(u, 1, 0))
    u2 = jnp.where(rows == 0, c6, jnp.where(rows == 1, c7, pltpu.roll(u, 2, 0)))
    y = cw_ref[0:1, :] * u2
    y = y + cw_ref[1:2, :] * u1
    y = y + cw_ref[2:3, :] * u
    yb_ref[...] = (g_b * y).astype(MXU_DTYPE)
    carry_ref[...] = u[tm - SUBLANES:, :]

    qm = q_mem.astype(MXU_DTYPE)
    outs = []
    for h in range(N_MEM_HEADS):
        sl = slice(h * HEAD_DIM, (h + 1) * HEAD_DIM)
        lg = _dot_nt(qm[:, sl], mk_ref[:, sl]) * (HEAD_DIM ** -0.5)
        lg = lg - jnp.max(lg, axis=-1, keepdims=True)
        e = jnp.exp(lg)
        pr = e / jnp.sum(e, axis=-1, keepdims=True)
        outs.append(_dot(pr.astype(MXU_DTYPE), mv_ref[:, sl]))
    yc_ref[...] = jnp.concatenate(outs, axis=-1).astype(MXU_DTYPE)


def _proj(x2, mem, w_main, w_small, q_g, kv_g, conv_w, w_mk, w_mv, B, S, tm):
    T, D = x2.shape
    n_mem = mem.shape[1]
    ns = S // tm
    row = lambda b, s: (b * ns + s, 0)
    const2 = lambda b, s: (0, 0)
    bf = MXU_DTYPE
    return pl.pallas_call(
        functools.partial(_proj_kernel, tm=tm),
        grid=(B, ns),
        in_specs=[
            pl.BlockSpec((tm, D), row),
            pl.BlockSpec((1, n_mem, D), lambda b, s: (b, 0, 0)),
            pl.BlockSpec(w_main.shape, const2),
            pl.BlockSpec(w_small.shape, const2),
            pl.BlockSpec(q_g.shape, const2),
            pl.BlockSpec(kv_g.shape, const2),
            pl.BlockSpec(conv_w.shape, const2),
            pl.BlockSpec(w_mk.shape, const2),
            pl.BlockSpec(w_mv.shape, const2),
        ],
        out_specs=[
            pl.BlockSpec((tm, Q_RANK), row),
            pl.BlockSpec((tm, KV_RANK), row),
            pl.BlockSpec((1, KV_RANK, tm), lambda b, s: (b, 0, s)),
            pl.BlockSpec((tm, IDX_DIM), row),
            pl.BlockSpec((1, N_IDX_HEADS, tm), lambda b, s: (b, 0, s)),
            pl.BlockSpec((tm, CONV_CH), row),
            pl.BlockSpec((tm, MIX_C), row),
        ],
        out_shape=[
            jax.ShapeDtypeStruct((T, Q_RANK), bf),
            jax.ShapeDtypeStruct((T, KV_RANK), bf),
            jax.ShapeDtypeStruct((B, KV_RANK, S), bf),
            jax.ShapeDtypeStruct((T, IDX_DIM), bf),
            jax.ShapeDtypeStruct((B, N_IDX_HEADS, S), jnp.float32),
            jax.ShapeDtypeStruct((T, CONV_CH), bf),
            jax.ShapeDtypeStruct((T, MIX_C), bf),
        ],
        scratch_shapes=[
            pltpu.VMEM((SUBLANES, CONV_CH), jnp.float32),
            pltpu.VMEM((n_mem, MIX_C), bf),
            pltpu.VMEM((n_mem, MIX_C), bf),
        ],
        compiler_params=_cparams(("arbitrary", "arbitrary")),
        name="proj",
    )(x2, mem, w_main, w_small, q_g, kv_g, conv_w, w_mk, w_mv)


def _key_to_f32(key):
    bits = jnp.where(key < 0, key ^ jnp.int32(0x7FFFFFFF), key)
    return pltpu.bitcast(bits, jnp.float32)


def _colsum8(v):
    return jnp.sum(v.reshape(QB // SUBLANES, SUBLANES, QB), axis=0)


def _colmax8(v):
    return jnp.max(v.reshape(QB // SUBLANES, SUBLANES, QB), axis=0)


UNROLL = 4


def _dsa_kernel(cq_ref, iwt_ref, kidx_ref, ckv_ref, ckvt_ref, wqi_ref, wuq_ref, wuk_ref, wuvt_ref,
                bias_ref, o_ref, qidx_ref, qlat_ref, score_ref, mask_ref, logit_ref, acc_ref,
                *, k_sel, idx_bits):
    i = pl.program_id(1)
    f32 = jnp.float32
    bf = MXU_DTYPE
    n_blocks = i + 1
    s_loc = lax.broadcasted_iota(jnp.int32, (QB, QB), 0)
    t_glob = i * QB + lax.broadcasted_iota(jnp.int32, (QB, QB), 1)

    def blk(jb):
        return pl.multiple_of(jb * QB, QB)

    def block_loop(fn, init):
        n_main = n_blocks // UNROLL
        c = lax.fori_loop(0, n_main, lambda it, c: fn(it * UNROLL, UNROLL, c), init)
        return lax.fori_loop(n_main * UNROLL, n_blocks, lambda jb, c: fn(jb, 1, c), c)

    cq = cq_ref[...]
    q_all = _dot(cq, wuq_ref[...]).astype(bf)
    for h in range(N_HEADS_A):
        qidx_ref[h * QB:(h + 1) * QB, :] = _dot(cq, wqi_ref[:, h * IDX_DIM:(h + 1) * IDX_DIM]).astype(bf)
        qlat_ref[h * QB:(h + 1) * QB, :] = (
            _dot_nt(q_all[:, h * HEAD_DIM:(h + 1) * HEAD_DIM], wuk_ref[h]) * (HEAD_DIM ** -0.5)).astype(bf)
    iw = iwt_ref[0]

    def score_body(jb0, nb, c):
        d_blk = _dot_nt(kidx_ref[pl.ds(blk(jb0), nb * QB), :], qidx_ref[...])
        for sb in range(nb):
            off = blk(jb0 + sb)
            d_all = d_blk[sb * QB:(sb + 1) * QB, :]
            acc = jnp.maximum(d_all[:, 0:QB], 0.0) * iw[0:1, :]
            for h in range(1, N_IDX_HEADS):
                acc = acc + jnp.maximum(d_all[:, h * QB:(h + 1) * QB], 0.0) * iw[h:h + 1, :]
            score_ref[pl.ds(off, QB), :] = jnp.where(s_loc + off <= t_glob, acc + 0.0, F32_LOWEST)
        return c

    block_loop(score_body, 0)

    def count_where(pred):
        def body(jb0, nb, acc):
            for sb in range(nb):
                off = blk(jb0 + sb)
                acc = acc + _colsum8(jnp.where(pred(score_ref[pl.ds(off, QB), :], off), 1.0, 0.0))
            return acc
        acc = block_loop(body, jnp.zeros((SUBLANES, QB), f32))
        return jnp.sum(acc, axis=0, keepdims=True)

    kf = float(k_sel)

    def search():
        c0 = count_where(lambda sc, off: sc >= 0.0)
        cand0 = jnp.where(c0 >= kf, jnp.int32(0), jnp.int32(-2 ** 31))

        def bit_body(it, cand):
            trial = cand + lax.shift_left(jnp.int32(1), 30 - it)
            tf = _key_to_f32(trial)
            cnt = count_where(lambda sc, off: sc >= tf)
            return jnp.where(cnt >= kf, trial, cand)

        cand = lax.fori_loop(0, 31, bit_body, cand0)
        thr = _key_to_f32(cand)
        n_gt = count_where(lambda sc, off: sc > thr)
        n_eq = count_where(lambda sc, off: sc == thr)
        need = kf - n_gt

        def tie_search():
            def tbody(it, xcut):
                trial = xcut + lax.shift_left(jnp.int32(1), idx_bits - 1 - it)
                cnt = count_where(lambda sc, off: (sc == thr) & (s_loc + off < trial))
                return jnp.where(cnt < need, trial, xcut)
            return lax.fori_loop(0, idx_bits, tbody, jnp.zeros((1, QB), jnp.int32))

        any_extra = jnp.max(n_eq - need) > 0.0
        xcut = lax.cond(any_extra, tie_search, lambda: jnp.full((1, QB), 2 ** idx_bits - 1, jnp.int32))
        return thr, xcut

    def no_search():
        return jnp.full((1, QB), F32_LOWEST, f32), jnp.full((1, QB), 2 ** idx_bits - 1, jnp.int32)

    thr, xcut = lax.cond((i + 1) * QB > k_sel, search, no_search)

    def mask_body(jb0, nb, c):
        for sb in range(nb):
            off = blk(jb0 + sb)
            sc = score_ref[pl.ds(off, QB), :]
            s_glob = s_loc + off
            keep = ((sc > thr) | ((sc == thr) & (s_glob <= xcut))) & (s_glob <= t_glob)
            mask_ref[pl.ds(off, QB), :] = jnp.where(keep, 0.0, -jnp.inf)
        return c

    block_loop(mask_body, 0)

    def p1_body(jb0, nb, m8):
        m8 = list(m8)
        lg_blk = _dot_nt(ckv_ref[pl.ds(blk(jb0), nb * QB), :], qlat_ref[...])
        for sb in range(nb):
            off = blk(jb0 + sb)
            lg = lg_blk[sb * QB:(sb + 1) * QB, :]
            msk = mask_ref[pl.ds(off, QB), :]
            bsel = jnp.clip(jb0 + sb - i + 2, 0, 2)
            for h in range(N_HEADS_A):
                lgh = lg[:, h * QB:(h + 1) * QB] + bias_ref[bsel, h] + msk
                logit_ref[pl.ds(off, QB), h * QB:(h + 1) * QB] = lgh
                m8[h] = jnp.maximum(m8[h], _colmax8(lgh))
        return tuple(m8)

    m8 = block_loop(p1_body, tuple(jnp.full((SUBLANES, QB), -jnp.inf, f32) for _ in range(N_HEADS_A)))
    m_row = [jnp.max(m, axis=0, keepdims=True) for m in m8]

    acc_ref[...] = jnp.zeros_like(acc_ref)

    def p2_body(jb0, nb, l8):
        l8 = list(l8)
        off = blk(jb0)
        rows = nb * QB
        ps = []
        for h in range(N_HEADS_A):
            p = jnp.exp(logit_ref[pl.ds(off, rows), h * QB:(h + 1) * QB] - m_row[h])
            l8[h] = l8[h] + jnp.sum(p.reshape(rows // SUBLANES, SUBLANES, QB), axis=0)
            ps.append(p.astype(bf))
        acc_ref[...] += _dot(ckvt_ref[0, :, pl.ds(off, rows)], jnp.concatenate(ps, axis=1))
        return tuple(l8)

    l8 = block_loop(p2_body, tuple(jnp.zeros((SUBLANES, QB), f32) for _ in range(N_HEADS_A)))

    outs = []
    for h in range(N_HEADS_A):
        l_row = jnp.sum(l8[h], axis=0, keepdims=True)
        o_lat_t = (acc_ref[:, h * QB:(h + 1) * QB] / l_row).astype(bf)
        outs.append(_dot(wuvt_ref[h], o_lat_t))
    o_ref[...] = jnp.concatenate(outs, axis=0).T.astype(o_ref.dtype)


def _dsa(cq, iwt, kidx, ckv, ckvt, w_qidx, w_uq, w_uk_h, w_uvt_h, bias_tiles, B, S):
    T = cq.shape[0]
    assert S % QB == 0 and QB >= REL_MAX_DIST
    nq = S // QB
    k_sel = min(TOPK_MAX, S // 4)
    idx_bits = max(1, (S - 1).bit_length())
    c2 = lambda b, i: (0, 0)
    c3 = lambda b, i: (0, 0, 0)
    return pl.pallas_call(
        functools.partial(_dsa_kernel, k_sel=k_sel, idx_bits=idx_bits),
        grid=(B, nq),
        in_specs=[
            pl.BlockSpec((QB, Q_RANK), lambda b, i: (b * nq + i, 0)),
            pl.BlockSpec((1, N_IDX_HEADS, QB), lambda b, i: (b, 0, i)),
            pl.BlockSpec((S, IDX_DIM), lambda b, i: (b, 0)),
            pl.BlockSpec((S, KV_RANK), lambda b, i: (b, 0)),
            pl.BlockSpec((1, KV_RANK, S), lambda b, i: (b, 0, 0)),
            pl.BlockSpec(w_qidx.shape, c2),
            pl.BlockSpec(w_uq.shape, c2),
            pl.BlockSpec(w_uk_h.shape, c3),
            pl.BlockSpec(w_uvt_h.shape, c3),
            pl.BlockSpec(bias_tiles.shape, lambda b, i: (0, 0, 0, 0)),
        ],
        out_specs=pl.BlockSpec((QB, MIX_A), lambda b, i: (b * nq + i, 0)),
        out_shape=jax.ShapeDtypeStruct((T, MIX_A), MXU_DTYPE),
        scratch_shapes=[
            pltpu.VMEM((N_IDX_HEADS * QB, IDX_DIM), MXU_DTYPE),
            pltpu.VMEM((N_HEADS_A * QB, KV_RANK), MXU_DTYPE),
            pltpu.VMEM((S, QB), jnp.float32),
            pltpu.VMEM((S, QB), jnp.float32),
            pltpu.VMEM((S, N_HEADS_A * QB), jnp.float32),
            pltpu.VMEM((KV_RANK, N_HEADS_A * QB), jnp.float32),
        ],
        compiler_params=_cparams(("arbitrary", "arbitrary")),
        name="dsa",
    )(cq, iwt, kidx, ckv, ckvt, w_qidx, w_uq, w_uk_h, w_uvt_h, bias_tiles)


def _layer_norm(xf, g, b):
    mu = jnp.mean(xf, axis=-1, keepdims=True)
    xc = xf - mu
    var = jnp.mean(xc * xc, axis=-1, keepdims=True)
    return xc * lax.rsqrt(var + LN_EPS) * g + b


def _rank_rows(v, n):
    ri = lax.broadcasted_iota(jnp.int32, v.shape, 0)
    rank = jnp.zeros(v.shape, jnp.float32)
    for r2 in range(n):
        row = v[r2:r2 + 1, :]
        beats = (row > v) | ((row == v) & (ri > r2))
        rank = rank + jnp.where(beats, 1.0, 0.0)
    return rank


def _pack_factor():
    return 4 // jnp.dtype(MXU_DTYPE).itemsize


def _pack_rows(x):
    if _pack_factor() == 1:
        return pltpu.bitcast(x, jnp.int32)
    half = x.shape[1] // 2
    b = pltpu.bitcast(x.astype(MXU_DTYPE).astype(jnp.float32), jnp.int32)
    return b[:, half:] | lax.shift_right_logical(b[:, :half], jnp.int32(16))


_HIGH_HALF = -(1 << 16)


def _unpack_rows_f32(p):
    if _pack_factor() == 1:
        return [pltpu.bitcast(p, jnp.float32)]
    lo = pltpu.bitcast(lax.shift_left(p, jnp.int32(16)), jnp.float32)
    hi = pltpu.bitcast(p & jnp.int32(_HIGH_HALF), jnp.float32)
    return [lo, hi]


def _unpack_rows(p):
    return [v.astype(MXU_DTYPE) for v in _unpack_rows_f32(p)]


def _mix_router_kernel(x_ref, ya_ref, yb_ref, yc_ref, wo_ref, g_ref, b_ref, wrt_ref, rb_ref, exp_ref,
                       x1_ref, x1p_ref, sel_ref, w_ref, pos_ref, cnt_ref, base_ref, *, tm):
    step = pl.program_id(0)
    f32 = jnp.float32

    @pl.when(step == 0)
    def _():
        base_ref[...] = jnp.zeros_like(base_ref)

    mix = _dot(ya_ref[...], wo_ref[0:MIX_A, :])
    mix = mix + _dot(yb_ref[...], wo_ref[MIX_A:MIX_A + CONV_CH, :])
    mix = mix + _dot(yc_ref[...], wo_ref[MIX_A + CONV_CH:, :])
    x1 = _layer_norm(ALPHA * x_ref[...] + mix, g_ref[...], b_ref[...])
    x1_ref[...] = x1
    x1p_ref[...] = _pack_rows(x1)

    lg = lax.dot_general(wrt_ref[...], x1, _NT, precision=lax.Precision.HIGHEST, preferred_element_type=f32)
    s = 1.0 / (1.0 + jnp.exp(-lg))
    sc = s + rb_ref[...]

    g3 = sc.reshape(N_GROUPS, GROUP_SIZE, tm)
    m1 = jnp.max(g3, axis=1, keepdims=True)
    is_m1 = g3 == m1
    n_m1 = jnp.sum(jnp.where(is_m1, 1.0, 0.0), axis=1, keepdims=True)
    m2 = jnp.max(jnp.where(is_m1, -jnp.inf, g3), axis=1, keepdims=True)
    gscore = (m1 + jnp.where(n_m1 > 1.0, m1, m2)).reshape(N_GROUPS, tm)
    gsel = jnp.where(_rank_rows(gscore, N_GROUPS) < float(TOPK_GROUPS), 1.0, 0.0)
    emask = _dot(exp_ref[...], gsel.astype(MXU_DTYPE)) > 0.5
    masked = jnp.where(emask, sc, -jnp.inf)
    sel = (_rank_rows(masked, N_EXPERTS) < float(TOP_K)) & emask
    self_ = jnp.where(sel, 1.0, 0.0)
    top_s = jnp.where(sel, s, 0.0)
    w = top_s / jnp.sum(top_s, axis=0, keepdims=True) * ROUTED_SCALE

    t_r = lax.broadcasted_iota(jnp.int32, (tm, tm), 0)
    t_c = lax.broadcasted_iota(jnp.int32, (tm, tm), 1)
    upper = jnp.where(t_r < t_c, 1.0, 0.0).astype(MXU_DTYPE)
    pref = _dot(self_.astype(MXU_DTYPE), upper)
    base = base_ref[...]
    sel_ref[...] = self_
    w_ref[...] = w
    pos_ref[...] = base + pref
    base = base + jnp.sum(self_, axis=1, keepdims=True)
    base_ref[...] = base
    cnt_ref[...] = jnp.broadcast_to(base, cnt_ref.shape)


def _mix_router(x2, ya, yb, yc, w_out, ln_g, ln_b, w_router_t, router_bias, tm):
    T, D = x2.shape
    E = N_EXPERTS
    expand = (jnp.arange(E)[:, None] // GROUP_SIZE == jnp.arange(N_GROUPS)[None, :]).astype(MXU_DTYPE)
    row = lambda i: (i, 0)
    col = lambda i: (0, i)
    c2 = lambda i: (0, 0)
    f32 = jnp.float32
    return pl.pallas_call(
        functools.partial(_mix_router_kernel, tm=tm),
        grid=(T // tm,),
        in_specs=[
            pl.BlockSpec((tm, D), row),
            pl.BlockSpec((tm, MIX_A), row),
            pl.BlockSpec((tm, CONV_CH), row),
            pl.BlockSpec((tm, MIX_C), row),
            pl.BlockSpec(w_out.shape, c2),
            pl.BlockSpec((1, D), c2),
            pl.BlockSpec((1, D), c2),
            pl.BlockSpec((E, D), c2),
            pl.BlockSpec((E, 1), c2),
            pl.BlockSpec((E, N_GROUPS), c2),
        ],
        out_specs=[
            pl.BlockSpec((tm, D), row),
            pl.BlockSpec((tm, D // _pack_factor()), row),
            pl.BlockSpec((E, tm), col),
            pl.BlockSpec((E, tm), col),
            pl.BlockSpec((E, tm), col),
            pl.BlockSpec((E, LANES), c2),
        ],
        out_shape=[
            jax.ShapeDtypeStruct((T, D), f32),
            jax.ShapeDtypeStruct((T, D // _pack_factor()), jnp.int32),
            jax.ShapeDtypeStruct((E, T), f32),
            jax.ShapeDtypeStruct((E, T), f32),
            jax.ShapeDtypeStruct((E, T), f32),
            jax.ShapeDtypeStruct((E, LANES), f32),
        ],
        scratch_shapes=[pltpu.VMEM((E, 1), f32)],
        compiler_params=_cparams(("arbitrary",)),
        name="mix_router",
    )(x2, ya, yb, yc, w_out, ln_g, ln_b, w_router_t, router_bias, expand)


def _compact_kernel(sel_ref, w_ref, pos_ref, pstart_ref, low_ref, dest_ref, wk_ref):
    sel = sel_ref[...]
    on = sel > 0.5
    rank = _dot(low_ref[...], sel.astype(MXU_DTYPE))
    row = pstart_ref[...] + pos_ref[...]
    w = w_ref[...]
    dests, ws = [], []
    for k in range(TOP_K):
        m = on & (rank == float(k))
        dests.append(jnp.sum(jnp.where(m, row, 0.0), axis=0, keepdims=True))
        ws.append(jnp.sum(jnp.where(m, w, 0.0), axis=0, keepdims=True))
    dest_ref[...] = jnp.concatenate(dests, axis=0).astype(jnp.int32)
    wk_ref[...] = jnp.concatenate(ws, axis=0)


def _compact(sel_t, w_t, pos_t, pad_start, tm):
    E, T = sel_t.shape
    lower = (jnp.arange(E)[None, :] < jnp.arange(E)[:, None]).astype(MXU_DTYPE)
    col = lambda i: (0, i)
    c2 = lambda i: (0, 0)
    return pl.pallas_call(
        _compact_kernel,
        grid=(T // tm,),
        in_specs=[pl.BlockSpec((E, tm), col), pl.BlockSpec((E, tm), col), pl.BlockSpec((E, tm), col),
                  pl.BlockSpec((E, 1), c2), pl.BlockSpec((E, E), c2)],
        out_specs=[pl.BlockSpec((TOP_K, tm), col), pl.BlockSpec((TOP_K, tm), col)],
        out_shape=[jax.ShapeDtypeStruct((TOP_K, T), jnp.int32), jax.ShapeDtypeStruct((TOP_K, T), jnp.float32)],
        compiler_params=_cparams(("arbitrary",)),
        name="route_compact",
    )(sel_t, w_t, pos_t, pad_start, lower)


def _row_copy(src, s, dst, d, sem):
    return pltpu.make_async_copy(src.at[pl.ds(s, 1)], dst.at[pl.ds(d, 1)], sem)


def _dispatch_kernel(flo_ref, fhi_ref, dest_ref, x_ref, xs_hbm, zero_ref, sem, zsem, *, td):
    step = pl.program_id(0)

    @pl.when(step == 0)
    def _():
        zero_ref[...] = jnp.zeros_like(zero_ref)

        def per_expert(fn):
            def ebody(e, c):
                lax.fori_loop(flo_ref[e], fhi_ref[e], lambda r, c2: (fn(r), c2)[1], 0)
                return c
            lax.fori_loop(0, N_EXPERTS, ebody, 0)

        per_expert(lambda r: _row_copy(zero_ref, 0, xs_hbm, r, zsem).start())
        per_expert(lambda r: _row_copy(zero_ref, 0, xs_hbm, r, zsem).wait())

    def issue(r, c):
        for k in range(TOP_K):
            _row_copy(x_ref, r, xs_hbm, dest_ref[k, r], sem).start()
        return c

    def drain(r, c):
        for k in range(TOP_K):
            _row_copy(x_ref, r, xs_hbm, dest_ref[k, r], sem).wait()
        return c

    lax.fori_loop(0, td, issue, 0)
    lax.fori_loop(0, td, drain, 0)


def _dispatch(dest_t, x1p, fill_lo, fill_hi, n_rows, td):
    T, W = x1p.shape
    return pl.pallas_call(
        functools.partial(_dispatch_kernel, td=td),
        grid_spec=pltpu.PrefetchScalarGridSpec(
            num_scalar_prefetch=2,
            grid=(T // td,),
            in_specs=[
                pl.BlockSpec((TOP_K, td), lambda i, lo, hi: (0, i), memory_space=pltpu.SMEM),
                pl.BlockSpec((td, W), lambda i, lo, hi: (i, 0)),
            ],
            out_specs=pl.BlockSpec(memory_space=pl.ANY),
            scratch_shapes=[pltpu.VMEM((SUBLANES, W), x1p.dtype),
                            pltpu.SemaphoreType.DMA, pltpu.SemaphoreType.DMA],
        ),
        out_shape=jax.ShapeDtypeStruct((n_rows, W), x1p.dtype),
        compiler_params=_cparams(("arbitrary",)),
        name="dispatch",
    )(fill_lo, fill_hi, dest_t, x1p)


def _silu(g):
    return g / (1.0 + jnp.exp(-g))


def _expert_kernel(be_ref, nu_ref, xs_ref, wg_ref, wu_ref, wd_ref, ys_ref, wgb_ref, wub_ref, wdb_ref):
    i = pl.program_id(0)

    @pl.when((i == 0) | (be_ref[i] != be_ref[jnp.maximum(i - 1, 0)]))
    def _():
        wgb_ref[...] = wg_ref[0].astype(MXU_DTYPE)
        wub_ref[...] = wu_ref[0].astype(MXU_DTYPE)
        wdb_ref[...] = wd_ref[0].astype(MXU_DTYPE)

    @pl.when(i < nu_ref[0])
    def _():
        parts = _unpack_rows(xs_ref[...])
        dk = wgb_ref.shape[0] // len(parts)

        def proj(w_ref):
            acc = _dot(parts[0], w_ref[0:dk, :])
            for n in range(1, len(parts)):
                acc = acc + _dot(parts[n], w_ref[n * dk:(n + 1) * dk, :])
            return acc

        a = (_silu(proj(wgb_ref)) * proj(wub_ref)).astype(MXU_DTYPE)
        ys_ref[...] = _pack_rows(_dot(a, wdb_ref[...]))


def _experts(xs, block_e, n_used, w_gate, w_up, w_down):
    n_rows, W = xs.shape
    D = w_gate.shape[1]
    n_blocks = n_rows // ROW_BLOCK
    blk = lambda i, be, nu: (jnp.minimum(i, nu[0] - 1), 0)
    wsel = lambda i, be, nu: (be[i], 0, 0)
    return pl.pallas_call(
        _expert_kernel,
        grid_spec=pltpu.PrefetchScalarGridSpec(
            num_scalar_prefetch=2,
            grid=(n_blocks,),
            in_specs=[
                pl.BlockSpec((ROW_BLOCK, W), blk),
                pl.BlockSpec((1, D, D_EXPERT), wsel),
                pl.BlockSpec((1, D, D_EXPERT), wsel),
                pl.BlockSpec((1, D_EXPERT, D), wsel),
            ],
            out_specs=pl.BlockSpec((ROW_BLOCK, W), blk),
            scratch_shapes=[pltpu.VMEM((D, D_EXPERT), MXU_DTYPE), pltpu.VMEM((D, D_EXPERT), MXU_DTYPE),
                            pltpu.VMEM((D_EXPERT, D), MXU_DTYPE)],
        ),
        out_shape=jax.ShapeDtypeStruct((n_rows, W), xs.dtype),
        compiler_params=_cparams(("arbitrary",)),
        name="experts",
    )(block_e, n_used, xs, w_gate, w_up, w_down)


SC_CORES = 2
SC_SUBCORES = 16
SC_GATHER_ROWS = 64


def _sc_gather_rows(table, idx):
    n = idx.shape[0]
    w = table.shape[1]
    n_workers = SC_CORES * SC_SUBCORES
    per_worker = n // n_workers
    assert n % n_workers == 0 and per_worker % SC_GATHER_ROWS == 0
    mesh = plsc.VectorSubcoreMesh(core_axis_name="c", subcore_axis_name="s")

    @functools.partial(
        pl.kernel, mesh=mesh,
        out_type=jax.ShapeDtypeStruct((n, w), table.dtype),
        scratch_types=[
            pltpu.VMEM((2, SC_GATHER_ROWS), jnp.int32),
            pltpu.VMEM((2, SC_GATHER_ROWS, w), table.dtype),
            pltpu.SemaphoreType.DMA((2,)),
        ],
        name="sc_gather_rows",
    )
    def gather(table_hbm, idx_hbm, out_hbm, idx_v, rows_v, sem):
        wid = lax.axis_index("s") * SC_CORES + lax.axis_index("c")
        base = wid * per_worker
        n_steps = per_worker // SC_GATHER_ROWS

        def gather_copy(slot):
            return pltpu.make_async_copy(table_hbm.at[idx_v.at[slot]], rows_v.at[slot], sem.at[slot])

        def start(step, slot):
            pltpu.sync_copy(idx_hbm.at[pl.ds(base + step * SC_GATHER_ROWS, SC_GATHER_ROWS)], idx_v.at[slot])
            gather_copy(slot).start()

        start(0, 0)

        @pl.loop(0, n_steps, step=2)
        def _(g):
            for slot in range(2):
                step = g + slot

                @pl.when(step + 1 < n_steps)
                def _():
                    start(step + 1, 1 - slot)

                gather_copy(slot).wait()
                pltpu.sync_copy(rows_v.at[slot], out_hbm.at[pl.ds(base + step * SC_GATHER_ROWS, SC_GATHER_ROWS)])

    return gather(table, idx)


SC_SCATTER_ROWS = 64


def _sc_scatter_rows(rows, idx3, n_out):
    n_src, w = rows.shape
    n_chunks, n_dst, batch = idx3.shape
    n_workers = SC_CORES * SC_SUBCORES
    assert batch == SC_SCATTER_ROWS and n_chunks * batch == n_src and n_chunks % n_workers == 0
    assert n_src * n_dst == n_out
    per_worker = n_chunks // n_workers
    mesh = plsc.VectorSubcoreMesh(core_axis_name="c", subcore_axis_name="s")

    @functools.partial(
        pl.kernel, mesh=mesh,
        out_type=jax.ShapeDtypeStruct((n_out, w), rows.dtype),
        scratch_types=[
            pltpu.VMEM((n_dst, batch), jnp.int32),
            pltpu.VMEM((batch, w), rows.dtype),
        ],
        name="sc_scatter_rows",
    )
    def scatter(rows_hbm, idx_hbm, out_hbm, idx_v, rows_v):
        wid = lax.axis_index("s") * SC_CORES + lax.axis_index("c")

        @pl.loop(0, per_worker)
        def _(g):
            c = wid * per_worker + g
            pltpu.sync_copy(idx_hbm.at[c], idx_v)
            pltpu.sync_copy(rows_hbm.at[pl.ds(c * batch, batch)], rows_v)
            for k in range(n_dst):
                pltpu.sync_copy(rows_v, out_hbm.at[idx_v.at[k]])

    return scatter(rows, idx3)


def _shared_kernel(x1_ref, wsg_ref, wsu_ref, wsd_ref, o_ref):
    xb = x1_ref[...].astype(MXU_DTYPE)
    a = (_silu(_dot(xb, wsg_ref[...])) * _dot(xb, wsu_ref[...])).astype(MXU_DTYPE)
    o_ref[...] = _dot(a, wsd_ref[...])


def _shared_expert(x1, w_sg, w_su, w_sd, tm):
    T, D = x1.shape
    row = lambda i: (i, 0)
    c2 = lambda i: (0, 0)
    return pl.pallas_call(
        _shared_kernel,
        grid=(T // tm,),
        in_specs=[pl.BlockSpec((tm, D), row), pl.BlockSpec(w_sg.shape, c2), pl.BlockSpec(w_su.shape, c2),
                  pl.BlockSpec(w_sd.shape, c2)],
        out_specs=pl.BlockSpec((tm, D), row),
        out_shape=jax.ShapeDtypeStruct((T, D), jnp.float32),
        compiler_params=_cparams(("arbitrary",)),
        name="shared_expert",
    )(x1, w_sg, w_su, w_sd)


def _combine2_kernel(wk_ref, x1_ref, sh_ref, g_ref_rows, g_ref, b_ref, o_ref):
    x1 = x1_ref[...]
    shared = sh_ref[...]
    wk = wk_ref[...]
    groups = [wk[:, 0:1] * v for v in _unpack_rows_f32(g_ref_rows[0])]
    for k in range(1, TOP_K):
        groups = [g + wk[:, k:k + 1] * v for g, v in zip(groups, _unpack_rows_f32(g_ref_rows[k]))]
    routed = jnp.concatenate(groups, axis=1)
    o_ref[...] = _layer_norm(ALPHA * x1 + (routed + shared), g_ref[...], b_ref[...])


def _combine2(wk, x1, shared, gathered, ln_g, ln_b, tc):
    T, D = x1.shape
    W = gathered.shape[2]
    row = lambda i: (i, 0)
    c2 = lambda i: (0, 0)
    return pl.pallas_call(
        _combine2_kernel,
        grid=(T // tc,),
        in_specs=[
            pl.BlockSpec((tc, TOP_K), row),
            pl.BlockSpec((tc, D), row),
            pl.BlockSpec((tc, D), row),
            pl.BlockSpec((TOP_K, tc, W), lambda i: (0, i, 0)),
            pl.BlockSpec((1, D), c2),
            pl.BlockSpec((1, D), c2),
        ],
        out_specs=pl.BlockSpec((tc, D), row),
        out_shape=jax.ShapeDtypeStruct((T, D), jnp.float32),
        compiler_params=_cparams(("arbitrary",)),
        name="combine",
    )(wk, x1, shared, gathered, ln_g, ln_b)


def _combine_kernel(dest_ref, wk_ref, x1_ref, ys_hbm, wsg_ref, wsu_ref, wsd_ref, g_ref, b_ref,
                    o_ref, buf_ref, sem, *, tc):
    def issue(r, c):
        for k in range(TOP_K):
            _row_copy(ys_hbm, dest_ref[k, r], buf_ref.at[k], r, sem).start()
        return c

    def drain(r, c):
        for k in range(TOP_K):
            _row_copy(ys_hbm, dest_ref[k, r], buf_ref.at[k], r, sem).wait()
        return c

    lax.fori_loop(0, tc, issue, 0)
    x1 = x1_ref[...]
    xb = x1.astype(MXU_DTYPE)
    a = (_silu(_dot(xb, wsg_ref[...])) * _dot(xb, wsu_ref[...])).astype(MXU_DTYPE)
    shared = _dot(a, wsd_ref[...])
    lax.fori_loop(0, tc, drain, 0)
    wk = wk_ref[...]
    groups = [wk[:, 0:1] * v for v in _unpack_rows_f32(buf_ref[0])]
    for k in range(1, TOP_K):
        groups = [g + wk[:, k:k + 1] * v for g, v in zip(groups, _unpack_rows_f32(buf_ref[k]))]
    routed = jnp.concatenate(groups, axis=1)
    o_ref[...] = _layer_norm(ALPHA * x1 + (routed + shared), g_ref[...], b_ref[...])


def _combine(dest_t, wk, x1, ys, w_sg, w_su, w_sd, ln_g, ln_b, tc):
    T, D = x1.shape
    row = lambda i: (i, 0)
    c2 = lambda i: (0, 0)
    return pl.pallas_call(
        functools.partial(_combine_kernel, tc=tc),
        grid=(T // tc,),
        in_specs=[
            pl.BlockSpec((TOP_K, tc), lambda i: (0, i), memory_space=pltpu.SMEM),
            pl.BlockSpec((tc, TOP_K), row),
            pl.BlockSpec((tc, D), row),
            pl.BlockSpec(memory_space=pl.ANY),
            pl.BlockSpec(w_sg.shape, c2),
            pl.BlockSpec(w_su.shape, c2),
            pl.BlockSpec(w_sd.shape, c2),
            pl.BlockSpec((1, D), c2),
            pl.BlockSpec((1, D), c2),
        ],
        out_specs=pl.BlockSpec((tc, D), row),
        out_shape=jax.ShapeDtypeStruct((T, D), jnp.float32),
        scratch_shapes=[pltpu.VMEM((TOP_K, tc, ys.shape[1]), ys.dtype), pltpu.SemaphoreType.DMA],
        compiler_params=_cparams(("arbitrary",)),
        name="combine",
    )(dest_t, wk, x1, ys, w_sg, w_su, w_sd, ln_g, ln_b)


def _split_w_in(w_in):
    bf = MXU_DTYPE
    o_kv = Q_RANK
    o_ki = o_kv + KV_RANK
    o_iw = o_ki + IDX_DIM
    o_rest = o_iw + N_IDX_HEADS
    w_main = jnp.concatenate([w_in[:, :o_ki], w_in[:, o_rest:]], axis=1).astype(bf)
    w_small = jnp.pad(w_in[:, o_ki:o_rest], ((0, 0), (0, LANES - IDX_DIM - N_IDX_HEADS))).astype(bf)
    return w_main, w_small


def _stages(x, mem, w_in, q_norm_g, kv_norm_g, w_uq, w_uk, w_uv, w_qidx, rel_bias, conv_w, w_mem_k, w_mem_v, w_out, ln1_g, ln1_b, w_router, router_bias, w_e_gate, w_e_up, w_e_down, w_s_gate, w_s_up, w_s_down, ln2_g, ln2_b, upto=None):
    B, S, D = x.shape
    T = B * S
    bf = MXU_DTYPE
    l = 0
    res = {}
    x2 = x.reshape(T, D)
    w_main, w_small = _split_w_in(w_in[l])
    cq, ckv, ckvt, kidx, iwt, yb, yc = _proj(
        x2, mem, w_main, w_small, q_norm_g[l].reshape(1, -1), kv_norm_g[l].reshape(1, -1), conv_w[l],
        w_mem_k[l].astype(bf), w_mem_v[l].astype(bf), B, S, tm=min(512, S))
    res.update(c_q=cq, c_kv=ckv, k_idx=kidx, y_b=yb, y_c=yc,
               idx_w=jnp.swapaxes(iwt, 1, 2) / (N_IDX_HEADS ** -0.5 * IDX_DIM ** -0.5))
    if upto == "proj":
        return res
    bias_t = _bias_tiles(rel_bias)
    ya = _dsa(cq, iwt, kidx, ckv, ckvt,
              w_qidx[l].reshape(Q_RANK, -1).astype(bf), w_uq[l].reshape(Q_RANK, -1).astype(bf),
              jnp.transpose(w_uk[l], (1, 0, 2)).astype(bf), jnp.transpose(w_uv[l], (1, 2, 0)).astype(bf),
              bias_t, B, S)
    res.update(y_a=ya)
    if upto == "dsa":
        return res

    x1, x1p, sel_t, w_t, pos_t, cnt = _mix_router(
        x2, ya, yb, yc, w_out[l].astype(bf), ln1_g[l].reshape(1, -1), ln1_b[l].reshape(1, -1),
        w_router[l].T, router_bias[l].reshape(-1, 1), tm=min(512, T))
    res.update(x1=x1)

    counts = cnt[:, 0].astype(jnp.int32)
    padded = (counts + ROW_BLOCK - 1) // ROW_BLOCK * ROW_BLOCK
    pad_end = jnp.cumsum(padded)
    pad_start = pad_end - padded
    n_blocks = -(-(T * TOP_K) // ROW_BLOCK) + N_EXPERTS
    n_rows = n_blocks * ROW_BLOCK
    block_start = jnp.arange(n_blocks, dtype=jnp.int32) * ROW_BLOCK
    block_e = jnp.minimum(jnp.sum((pad_end[None, :] <= block_start[:, None]).astype(jnp.int32), axis=1),
                          N_EXPERTS - 1)
    n_used = (pad_end[-1:] // ROW_BLOCK).astype(jnp.int32)

    dest_t, wk_t = _compact(sel_t, w_t, pos_t, pad_start.astype(jnp.float32).reshape(-1, 1), tm=min(512, T))
    n_pad = n_rows - T * TOP_K
    gap = padded - counts
    cum_gap = jnp.cumsum(gap)
    j = jnp.arange(n_pad, dtype=jnp.int32)
    e_j = jnp.sum((cum_gap[None, :] <= j[:, None]).astype(jnp.int32), axis=1)
    e_c = jnp.minimum(e_j, N_EXPERTS - 1)
    in_expert = (pad_start + counts)[e_c] + (j - (cum_gap - gap)[e_c])
    pad_rows = jnp.where(e_j < N_EXPERTS, in_expert, pad_end[-1] + (j - cum_gap[-1])).astype(jnp.int32)
    bt = SC_SCATTER_ROWS
    idx3 = jnp.concatenate([
        jnp.transpose(dest_t.reshape(TOP_K, T // bt, bt), (1, 0, 2)),
        pad_rows.reshape(n_pad // (TOP_K * bt), TOP_K, bt)], axis=0)
    rows_ext = jnp.concatenate([x1p, jnp.zeros((n_pad // TOP_K, x1p.shape[1]), x1p.dtype)], axis=0)
    xs = _sc_scatter_rows(rows_ext, idx3, n_rows)
    ys = _experts(xs, block_e, n_used, w_e_gate[l], w_e_up[l], w_e_down[l])
    gathered = _sc_gather_rows(ys, dest_t.reshape(-1)).reshape(TOP_K, T, -1)
    shared = _shared_expert(x1, w_s_gate[l].astype(bf), w_s_up[l].astype(bf), w_s_down[l].astype(bf),
                            tm=min(512, T))
    out = _combine2(wk_t.T, x1, shared, gathered, ln2_g[l].reshape(1, -1), ln2_b[l].reshape(1, -1),
                    tc=min(256, T))
    res.update(out=out.reshape(B, S, D))
    return res


def kernel(x, mem, w_in, q_norm_g, kv_norm_g, w_uq, w_uk, w_uv, w_qidx, rel_bias, conv_w, w_mem_k, w_mem_v, w_out, ln1_g, ln1_b, w_router, router_bias, w_e_gate, w_e_up, w_e_down, w_s_gate, w_s_up, w_s_down, ln2_g, ln2_b):
    return _stages(x, mem, w_in, q_norm_g, kv_norm_g, w_uq, w_uk, w_uv, w_qidx, rel_bias, conv_w, w_mem_k, w_mem_v, w_out, ln1_g, ln1_b, w_router, router_bias, w_e_gate, w_e_up, w_e_down, w_s_gate, w_s_up, w_s_down, ln2_g, ln2_b)["out"]
```

```python
import functools
import math

import jax
import jax.numpy as jnp
from jax import lax
from jax.experimental import pallas as pl
from jax.experimental.pallas import tpu as pltpu
from jax.experimental.pallas import tpu_sc as plsc

N_HEADS_A = 8
HEAD_DIM = 64
Q_RANK = 256
KV_RANK = 128
N_IDX_HEADS = 8
IDX_DIM = 64
TOPK_MAX = 256
REL_BUCKETS = 32
REL_MAX_DIST = 128
CONV_CH = 256
CONV_WIDTH = 3
N_MEM_HEADS = 4
MIX_A = N_HEADS_A * HEAD_DIM
MIX_C = N_MEM_HEADS * HEAD_DIM
N_EXPERTS = 64
N_GROUPS = 8
GROUP_SIZE = N_EXPERTS // N_GROUPS
TOPK_GROUPS = 4
TOP_K = 8
D_EXPERT = 256
ROUTED_SCALE = 2.5
MOE_BLOCK = 256
DEPTH = 1
ALPHA = (2.0 * DEPTH) ** 0.25
LN_EPS = 1e-5
RMS_EPS = 1e-6

LANES = 128
SUBLANES = 8
QB = 128
F32_LOWEST = -3.4028234663852886e38
VMEM_LIMIT = 56 * 1024 * 1024
MXU_DTYPE = jnp.bfloat16
ROW_BLOCK = 512

_NT = (((1,), (1,)), ((), ()))


def _dot(a, b):
    return jnp.dot(a, b, preferred_element_type=jnp.float32)


def _dot_nt(a, b):
    return lax.dot_general(a, b, _NT, preferred_element_type=jnp.float32)


def _cparams(sem):
    return pltpu.CompilerParams(dimension_semantics=sem, vmem_limit_bytes=VMEM_LIMIT)


def _bias_kernel(rb_ref, o_ref):
    s = lax.broadcasted_iota(jnp.int32, (QB, QB), 0)
    t = lax.broadcasted_iota(jnp.int32, (QB, QB), 1)
    max_exact = REL_BUCKETS // 2
    for tile in range(3):
        n = jnp.maximum(t - s + (2 - tile) * QB, 0)
        nf = jnp.maximum(n.astype(jnp.float32), 1.0)
        large = max_exact + (jnp.log(nf / max_exact) / math.log(REL_MAX_DIST / max_exact)
                             * (REL_BUCKETS - max_exact)).astype(jnp.int32)
        large = jnp.minimum(large, REL_BUCKETS - 1)
        bucket = jnp.where(n < max_exact, n, large)
        for h in range(N_HEADS_A):
            acc = jnp.zeros((QB, QB), jnp.float32)
            for b in range(REL_BUCKETS):
                acc = jnp.where(bucket == b, rb_ref[b, h], acc)
            o_ref[tile, h] = acc


def _bias_tiles(rel_bias):
    return pl.pallas_call(
        _bias_kernel,
        in_specs=[pl.BlockSpec(memory_space=pltpu.SMEM)],
        out_specs=pl.BlockSpec(memory_space=pltpu.VMEM),
        out_shape=jax.ShapeDtypeStruct((3, N_HEADS_A, QB, QB), jnp.float32),
        name="bias_tiles",
    )(rel_bias)


_MAIN_COLS = Q_RANK + KV_RANK + 3 * CONV_CH + MIX_C


def _proj_kernel(x_ref, mem_ref, wm_ref, ws_ref, qg_ref, kvg_ref, cw_ref, wmk_ref, wmv_ref,
                 cq_ref, ckv_ref, ckvt_ref, kidx_ref, iwt_ref, yb_ref, yc_ref,
                 carry_ref, mk_ref, mv_ref, *, tm):
    si = pl.program_id(1)

    @pl.when(si == 0)
    def _():
        carry_ref[...] = jnp.zeros_like(carry_ref)
        mb = mem_ref[0].astype(MXU_DTYPE)
        mk_ref[...] = _dot(mb, wmk_ref[...]).astype(MXU_DTYPE)
        mv_ref[...] = _dot(mb, wmv_ref[...]).astype(MXU_DTYPE)

    xb = x_ref[...].astype(MXU_DTYPE)
    p = _dot(xb, wm_ref[...])
    small = _dot(xb, ws_ref[...])

    o = 0
    cq = p[:, o:o + Q_RANK]; o += Q_RANK
    ckv = p[:, o:o + KV_RANK]; o += KV_RANK
    g_b = p[:, o:o + CONV_CH]; o += CONV_CH
    g_c = p[:, o:o + CONV_CH]; o += CONV_CH
    h_c = p[:, o:o + CONV_CH]; o += CONV_CH
    q_mem = p[:, o:o + MIX_C]

    cq = cq * lax.rsqrt(jnp.mean(cq * cq, axis=-1, keepdims=True) + RMS_EPS) * qg_ref[...]
    ckv = ckv * lax.rsqrt(jnp.mean(ckv * ckv, axis=-1, keepdims=True) + RMS_EPS) * kvg_ref[...]
    cq_ref[...] = cq.astype(MXU_DTYPE)
    ckv_b = ckv.astype(MXU_DTYPE)
    ckv_ref[...] = ckv_b
    ckvt_ref[0] = ckv.T.astype(MXU_DTYPE)

    kidx_ref[...] = small[:, :IDX_DIM].astype(MXU_DTYPE)
    small_t = small.T
    iwt_ref[0] = small_t[IDX_DIM:IDX_DIM + N_IDX_HEADS, :] * (N_IDX_HEADS ** -0.5 * IDX_DIM ** -0.5)

    u = g_c * h_c
    rows = lax.broadcasted_iota(jnp.int32, (tm, 1), 0)
    c6 = carry_ref[SUBLANES - 2:SUBLANES - 1, :]
    c7 = carry_ref[SUBLANES - 1:SUBLANES, :]
    u1 = jnp.where(rows == 0, c7, pltpu.roll(u, 1, 0))
    u2 = jnp.where(rows == 0, c6, jnp.where(rows == 1, c7, pltpu.roll(u, 2, 0)))
    y = cw_ref[0:1, :] * u2
    y = y + cw_ref[1:2, :] * u1
    y = y + cw_ref[2:3, :] * u
    yb_ref[...] = (g_b * y).astype(MXU_DTYPE)
    carry_ref[...] = u[tm - SUBLANES:, :]

    qm = q_mem.astype(MXU_DTYPE)
    outs = []
    for h in range(N_MEM_HEADS):
        sl = slice(h * HEAD_DIM, (h + 1) * HEAD_DIM)
        lg = _dot_nt(qm[:, sl], mk_ref[:, sl]) * (HEAD_DIM ** -0.5)
        lg = lg - jnp.max(lg, axis=-1, keepdims=True)
        e = jnp.exp(lg)
        pr = e / jnp.sum(e, axis=-1, keepdims=True)
        outs.append(_dot(pr.astype(MXU_DTYPE), mv_ref[:, sl]))
    yc_ref[...] = jnp.concatenate(outs, axis=-1).astype(MXU_DTYPE)


def _proj(x2, mem, w_main, w_small, q_g, kv_g, conv_w, w_mk, w_mv, B, S, tm):
    T, D = x2.shape
    n_mem = mem.shape[1]
    ns = S // tm
    row = lambda b, s: (b * ns + s, 0)
    const2 = lambda b, s: (0, 0)
    bf = MXU_DTYPE
    return pl.pallas_call(
        functools.partial(_proj_kernel, tm=tm),
        grid=(B, ns),
        in_specs=[
            pl.BlockSpec((tm, D), row),
            pl.BlockSpec((1, n_mem, D), lambda b, s: (b, 0, 0)),
            pl.BlockSpec(w_main.shape, const2),
            pl.BlockSpec(w_small.shape, const2),
            pl.BlockSpec(q_g.shape, const2),
            pl.BlockSpec(kv_g.shape, const2),
            pl.BlockSpec(conv_w.shape, const2),
            pl.BlockSpec(w_mk.shape, const2),
            pl.BlockSpec(w_mv.shape, const2),
        ],
        out_specs=[
            pl.BlockSpec((tm, Q_RANK), row),
            pl.BlockSpec((tm, KV_RANK), row),
            pl.BlockSpec((1, KV_RANK, tm), lambda b, s: (b, 0, s)),
            pl.BlockSpec((tm, IDX_DIM), row),
            pl.BlockSpec((1, N_IDX_HEADS, tm), lambda b, s: (b, 0, s)),
            pl.BlockSpec((tm, CONV_CH), row),
            pl.BlockSpec((tm, MIX_C), row),
        ],
        out_shape=[
            jax.ShapeDtypeStruct((T, Q_RANK), bf),
            jax.ShapeDtypeStruct((T, KV_RANK), bf),
            jax.ShapeDtypeStruct((B, KV_RANK, S), bf),
            jax.ShapeDtypeStruct((T, IDX_DIM), bf),
            jax.ShapeDtypeStruct((B, N_IDX_HEADS, S), jnp.float32),
            jax.ShapeDtypeStruct((T, CONV_CH), bf),
            jax.ShapeDtypeStruct((T, MIX_C), bf),
        ],
        scratch_shapes=[
            pltpu.VMEM((SUBLANES, CONV_CH), jnp.float32),
            pltpu.VMEM((n_mem, MIX_C), bf),
            pltpu.VMEM((n_mem, MIX_C), bf),
        ],
        compiler_params=_cparams(("arbitrary", "arbitrary")),
        name="proj",
    )(x2, mem, w_main, w_small, q_g, kv_g, conv_w, w_mk, w_mv)


def _key_to_f32(key):
    bits = jnp.where(key < 0, key ^ jnp.int32(0x7FFFFFFF), key)
    return pltpu.bitcast(bits, jnp.float32)


def _colsum8(v):
    return jnp.sum(v.reshape(QB // SUBLANES, SUBLANES, QB), axis=0)


def _colmax8(v):
    return jnp.max(v.reshape(QB // SUBLANES, SUBLANES, QB), axis=0)


UNROLL_WIDTHS = (4, 2, 1)


def _dsa_kernel(cq_ref, iwt_ref, kidx_ref, ckv_ref, ckvt_ref, wqi_ref, wuq_ref, wuk_ref, wuvt_ref,
                bias_ref, o_ref, qidx_ref, qlat_ref, score_ref, mask_ref, logit_ref, acc_ref,
                *, k_sel, idx_bits):
    i = pl.program_id(1)
    f32 = jnp.float32
    bf = MXU_DTYPE
    n_blocks = i + 1
    s_loc = lax.broadcasted_iota(jnp.int32, (QB, QB), 0)
    t_glob = i * QB + lax.broadcasted_iota(jnp.int32, (QB, QB), 1)

    def blk(jb):
        return pl.multiple_of(jb * QB, QB)

    def block_loop(fn, init):
        c, start = init, 0
        for width in UNROLL_WIDTHS:
            n = (n_blocks - start) // width
            c = lax.fori_loop(0, n, lambda it, c, w=width, s=start: fn(s + it * w, w, c), c)
            start = start + n * width
        return c

    cq = cq_ref[...]
    q_all = _dot(cq, wuq_ref[...]).astype(bf)
    for h in range(N_HEADS_A):
        qidx_ref[h * QB:(h + 1) * QB, :] = _dot(cq, wqi_ref[:, h * IDX_DIM:(h + 1) * IDX_DIM]).astype(bf)
        qlat_ref[h * QB:(h + 1) * QB, :] = (
            _dot_nt(q_all[:, h * HEAD_DIM:(h + 1) * HEAD_DIM], wuk_ref[h]) * (HEAD_DIM ** -0.5)).astype(bf)
    iw = iwt_ref[0]

    def score_body(jb0, nb, c):
        d_blk = _dot_nt(kidx_ref[pl.ds(blk(jb0), nb * QB), :], qidx_ref[...])
        for sb in range(nb):
            off = blk(jb0 + sb)
            d_all = d_blk[sb * QB:(sb + 1) * QB, :]
            acc = jnp.maximum(d_all[:, 0:QB], 0.0) * iw[0:1, :]
            for h in range(1, N_IDX_HEADS):
                acc = acc + jnp.maximum(d_all[:, h * QB:(h + 1) * QB], 0.0) * iw[h:h + 1, :]
            score_ref[pl.ds(off, QB), :] = jnp.where(s_loc + off <= t_glob, acc + 0.0, F32_LOWEST)
        return c

    block_loop(score_body, 0)

    def count_where(pred):
        def body(jb0, nb, acc):
            for sb in range(nb):
                off = blk(jb0 + sb)
                acc = acc + _colsum8(jnp.where(pred(score_ref[pl.ds(off, QB), :], off), 1.0, 0.0))
            return acc
        acc = block_loop(body, jnp.zeros((SUBLANES, QB), f32))
        return jnp.sum(acc, axis=0, keepdims=True)

    kf = float(k_sel)

    def search():
        c0 = count_where(lambda sc, off: sc >= 0.0)
        cand0 = jnp.where(c0 >= kf, jnp.int32(0), jnp.int32(-2 ** 31))

        def bit_body(it, cand):
            trial = cand + lax.shift_left(jnp.int32(1), 30 - it)
            tf = _key_to_f32(trial)
            cnt = count_where(lambda sc, off: sc >= tf)
            return jnp.where(cnt >= kf, trial, cand)

        cand = lax.fori_loop(0, 31, bit_body, cand0)
        thr = _key_to_f32(cand)
        n_gt = count_where(lambda sc, off: sc > thr)
        n_eq = count_where(lambda sc, off: sc == thr)
        need = kf - n_gt

        def tie_search():
            def tbody(it, xcut):
                trial = xcut + lax.shift_left(jnp.int32(1), idx_bits - 1 - it)
                cnt = count_where(lambda sc, off: (sc == thr) & (s_loc + off < trial))
                return jnp.where(cnt < need, trial, xcut)
            return lax.fori_loop(0, idx_bits, tbody, jnp.zeros((1, QB), jnp.int32))

        any_extra = jnp.max(n_eq - need) > 0.0
        xcut = lax.cond(any_extra, tie_search, lambda: jnp.full((1, QB), 2 ** idx_bits - 1, jnp.int32))
        return thr, xcut

    def no_search():
        return jnp.full((1, QB), F32_LOWEST, f32), jnp.full((1, QB), 2 ** idx_bits - 1, jnp.int32)

    thr, xcut = lax.cond((i + 1) * QB > k_sel, search, no_search)

    def mask_body(jb0, nb, c):
        for sb in range(nb):
            off = blk(jb0 + sb)
            sc = score_ref[pl.ds(off, QB), :]
            s_glob = s_loc + off
            keep = ((sc > thr) | ((sc == thr) & (s_glob <= xcut))) & (s_glob <= t_glob)
            mask_ref[pl.ds(off, QB), :] = jnp.where(keep, 0.0, -jnp.inf)
        return c

    block_loop(mask_body, 0)

    def p1_body(jb0, nb, m8):
        m8 = list(m8)
        lg_blk = _dot_nt(ckv_ref[pl.ds(blk(jb0), nb * QB), :], qlat_ref[...])
        for sb in range(nb):
            off = blk(jb0 + sb)
            lg = lg_blk[sb * QB:(sb + 1) * QB, :]
            msk = mask_ref[pl.ds(off, QB), :]
            bsel = jnp.clip(jb0 + sb - i + 2, 0, 2)
            for h in range(N_HEADS_A):
                lgh = lg[:, h * QB:(h + 1) * QB] + bias_ref[bsel, h] + msk
                logit_ref[pl.ds(off, QB), h * QB:(h + 1) * QB] = lgh
                m8[h] = jnp.maximum(m8[h], _colmax8(lgh))
        return tuple(m8)

    m8 = block_loop(p1_body, tuple(jnp.full((SUBLANES, QB), -jnp.inf, f32) for _ in range(N_HEADS_A)))
    m_row = [jnp.max(m, axis=0, keepdims=True) for m in m8]

    acc_ref[...] = jnp.zeros_like(acc_ref)

    def p2_body(jb0, nb, l8):
        l8 = list(l8)
        off = blk(jb0)
        rows = nb * QB
        ps = []
        for h in range(N_HEADS_A):
            p = jnp.exp(logit_ref[pl.ds(off, rows), h * QB:(h + 1) * QB] - m_row[h])
            l8[h] = l8[h] + jnp.sum(p.reshape(rows // SUBLANES, SUBLANES, QB), axis=0)
            ps.append(p.astype(bf))
        acc_ref[...] += _dot(ckvt_ref[0, :, pl.ds(off, rows)], jnp.concatenate(ps, axis=1))
        return tuple(l8)

    l8 = block_loop(p2_body, tuple(jnp.zeros((SUBLANES, QB), f32) for _ in range(N_HEADS_A)))

    outs = []
    for h in range(N_HEADS_A):
        l_row = jnp.sum(l8[h], axis=0, keepdims=True)
        o_lat_t = (acc_ref[:, h * QB:(h + 1) * QB] / l_row).astype(bf)
        outs.append(_dot(wuvt_ref[h], o_lat_t))
    o_ref[...] = jnp.concatenate(outs, axis=0).T.astype(o_ref.dtype)


def _dsa(cq, iwt, kidx, ckv, ckvt, w_qidx, w_uq, w_uk_h, w_uvt_h, bias_tiles, B, S):
    T = cq.shape[0]
    assert S % QB == 0 and QB >= REL_MAX_DIST
    nq = S // QB
    k_sel = min(TOPK_MAX, S // 4)
    idx_bits = max(1, (S - 1).bit_length())
    c2 = lambda b, i: (0, 0)
    c3 = lambda b, i: (0, 0, 0)
    return pl.pallas_call(
        functools.partial(_dsa_kernel, k_sel=k_sel, idx_bits=idx_bits),
        grid=(B, nq),
        in_specs=[
            pl.BlockSpec((QB, Q_RANK), lambda b, i: (b * nq + i, 0)),
            pl.BlockSpec((1, N_IDX_HEADS, QB), lambda b, i: (b, 0, i)),
            pl.BlockSpec((S, IDX_DIM), lambda b, i: (b, 0)),
            pl.BlockSpec((S, KV_RANK), lambda b, i: (b, 0)),
            pl.BlockSpec((1, KV_RANK, S), lambda b, i: (b, 0, 0)),
            pl.BlockSpec(w_qidx.shape, c2),
            pl.BlockSpec(w_uq.shape, c2),
            pl.BlockSpec(w_uk_h.shape, c3),
            pl.BlockSpec(w_uvt_h.shape, c3),
            pl.BlockSpec(bias_tiles.shape, lambda b, i: (0, 0, 0, 0)),
        ],
        out_specs=pl.BlockSpec((QB, MIX_A), lambda b, i: (b * nq + i, 0)),
        out_shape=jax.ShapeDtypeStruct((T, MIX_A), MXU_DTYPE),
        scratch_shapes=[
            pltpu.VMEM((N_IDX_HEADS * QB, IDX_DIM), MXU_DTYPE),
            pltpu.VMEM((N_HEADS_A * QB, KV_RANK), MXU_DTYPE),
            pltpu.VMEM((S, QB), jnp.float32),
            pltpu.VMEM((S, QB), jnp.float32),
            pltpu.VMEM((S, N_HEADS_A * QB), jnp.float32),
            pltpu.VMEM((KV_RANK, N_HEADS_A * QB), jnp.float32),
        ],
        compiler_params=_cparams(("arbitrary", "arbitrary")),
        name="dsa",
    )(cq, iwt, kidx, ckv, ckvt, w_qidx, w_uq, w_uk_h, w_uvt_h, bias_tiles)


def _layer_norm(xf, g, b):
    mu = jnp.mean(xf, axis=-1, keepdims=True)
    xc = xf - mu
    var = jnp.mean(xc * xc, axis=-1, keepdims=True)
    return xc * lax.rsqrt(var + LN_EPS) * g + b


def _rank_rows(v, n):
    ri = lax.broadcasted_iota(jnp.int32, v.shape, 0)
    rank = jnp.zeros(v.shape, jnp.float32)
    for r2 in range(n):
        row = v[r2:r2 + 1, :]
        beats = (row > v) | ((row == v) & (ri > r2))
        rank = rank + jnp.where(beats, 1.0, 0.0)
    return rank


def _pack_factor():
    return 4 // jnp.dtype(MXU_DTYPE).itemsize


def _pack_rows(x):
    if _pack_factor() == 1:
        return pltpu.bitcast(x, jnp.int32)
    half = x.shape[1] // 2
    b = pltpu.bitcast(x.astype(MXU_DTYPE).astype(jnp.float32), jnp.int32)
    return b[:, half:] | lax.shift_right_logical(b[:, :half], jnp.int32(16))


_HIGH_HALF = -(1 << 16)


def _unpack_rows_f32(p):
    if _pack_factor() == 1:
        return [pltpu.bitcast(p, jnp.float32)]
    lo = pltpu.bitcast(lax.shift_left(p, jnp.int32(16)), jnp.float32)
    hi = pltpu.bitcast(p & jnp.int32(_HIGH_HALF), jnp.float32)
    return [lo, hi]


def _unpack_rows(p):
    return [v.astype(MXU_DTYPE) for v in _unpack_rows_f32(p)]


def _mix_router_kernel(x_ref, ya_ref, yb_ref, yc_ref, wo_ref, g_ref, b_ref, wrt_ref, rb_ref, exp_ref,
                       x1_ref, x1p_ref, sel_ref, w_ref, pos_ref, cnt_ref, base_ref, *, tm):
    step = pl.program_id(0)
    f32 = jnp.float32

    @pl.when(step == 0)
    def _():
        base_ref[...] = jnp.zeros_like(base_ref)

    mix = _dot(ya_ref[...], wo_ref[0:MIX_A, :])
    mix = mix + _dot(yb_ref[...], wo_ref[MIX_A:MIX_A + CONV_CH, :])
    mix = mix + _dot(yc_ref[...], wo_ref[MIX_A + CONV_CH:, :])
    x1 = _layer_norm(ALPHA * x_ref[...] + mix, g_ref[...], b_ref[...])
    x1_ref[...] = x1
    x1p_ref[...] = _pack_rows(x1)

    lg = lax.dot_general(wrt_ref[...], x1, _NT, precision=lax.Precision.HIGHEST, preferred_element_type=f32)
    s = 1.0 / (1.0 + jnp.exp(-lg))
    sc = s + rb_ref[...]

    g3 = sc.reshape(N_GROUPS, GROUP_SIZE, tm)
    m1 = jnp.max(g3, axis=1, keepdims=True)
    is_m1 = g3 == m1
    n_m1 = jnp.sum(jnp.where(is_m1, 1.0, 0.0), axis=1, keepdims=True)
    m2 = jnp.max(jnp.where(is_m1, -jnp.inf, g3), axis=1, keepdims=True)
    gscore = (m1 + jnp.where(n_m1 > 1.0, m1, m2)).reshape(N_GROUPS, tm)
    gsel = jnp.where(_rank_rows(gscore, N_GROUPS) < float(TOPK_GROUPS), 1.0, 0.0)
    emask = _dot(exp_ref[...], gsel.astype(MXU_DTYPE)) > 0.5
    masked = jnp.where(emask, sc, -jnp.inf)
    sel = (_rank_rows(masked, N_EXPERTS) < float(TOP_K)) & emask
    self_ = jnp.where(sel, 1.0, 0.0)
    top_s = jnp.where(sel, s, 0.0)
    w = top_s / jnp.sum(top_s, axis=0, keepdims=True) * ROUTED_SCALE

    t_r = lax.broadcasted_iota(jnp.int32, (tm, tm), 0)
    t_c = lax.broadcasted_iota(jnp.int32, (tm, tm), 1)
    upper = jnp.where(t_r < t_c, 1.0, 0.0).astype(MXU_DTYPE)
    pref = _dot(self_.astype(MXU_DTYPE), upper)
    base = base_ref[...]
    sel_ref[...] = self_
    w_ref[...] = w
    pos_ref[...] = base + pref
    base = base + jnp.sum(self_, axis=1, keepdims=True)
    base_ref[...] = base
    cnt_ref[...] = jnp.broadcast_to(base, cnt_ref.shape)


def _mix_router(x2, ya, yb, yc, w_out, ln_g, ln_b, w_router_t, router_bias, tm):
    T, D = x2.shape
    E = N_EXPERTS
    expand = (jnp.arange(E)[:, None] // GROUP_SIZE == jnp.arange(N_GROUPS)[None, :]).astype(MXU_DTYPE)
    row = lambda i: (i, 0)
    col = lambda i: (0, i)
    c2 = lambda i: (0, 0)
    f32 = jnp.float32
    return pl.pallas_call(
        functools.partial(_mix_router_kernel, tm=tm),
        grid=(T // tm,),
        in_specs=[
            pl.BlockSpec((tm, D), row),
            pl.BlockSpec((tm, MIX_A), row),
            pl.BlockSpec((tm, CONV_CH), row),
            pl.BlockSpec((tm, MIX_C), row),
            pl.BlockSpec(w_out.shape, c2),
            pl.BlockSpec((1, D), c2),
            pl.BlockSpec((1, D), c2),
            pl.BlockSpec((E, D), c2),
            pl.BlockSpec((E, 1), c2),
            pl.BlockSpec((E, N_GROUPS), c2),
        ],
        out_specs=[
            pl.BlockSpec((tm, D), row),
            pl.BlockSpec((tm, D // _pack_factor()), row),
            pl.BlockSpec((E, tm), col),
            pl.BlockSpec((E, tm), col),
            pl.BlockSpec((E, tm), col),
            pl.BlockSpec((E, LANES), c2),
        ],
        out_shape=[
            jax.ShapeDtypeStruct((T, D), f32),
            jax.ShapeDtypeStruct((T, D // _pack_factor()), jnp.int32),
            jax.ShapeDtypeStruct((E, T), f32),
            jax.ShapeDtypeStruct((E, T), f32),
            jax.ShapeDtypeStruct((E, T), f32),
            jax.ShapeDtypeStruct((E, LANES), f32),
        ],
        scratch_shapes=[pltpu.VMEM((E, 1), f32)],
        compiler_params=_cparams(("arbitrary",)),
        name="mix_router",
    )(x2, ya, yb, yc, w_out, ln_g, ln_b, w_router_t, router_bias, expand)


def _compact_kernel(sel_ref, w_ref, pos_ref, pstart_ref, low_ref, dest_ref, wk_ref):
    sel = sel_ref[...]
    on = sel > 0.5
    rank = _dot(low_ref[...], sel.astype(MXU_DTYPE))
    row = pstart_ref[...] + pos_ref[...]
    w = w_ref[...]
    dests, ws = [], []
    for k in range(TOP_K):
        m = on & (rank == float(k))
        dests.append(jnp.sum(jnp.where(m, row, 0.0), axis=0, keepdims=True))
        ws.append(jnp.sum(jnp.where(m, w, 0.0), axis=0, keepdims=True))
    dest_ref[...] = jnp.concatenate(dests, axis=0).astype(jnp.int32)
    wk_ref[...] = jnp.concatenate(ws, axis=0)


def _compact(sel_t, w_t, pos_t, pad_start, tm):
    E, T = sel_t.shape
    lower = (jnp.arange(E)[None, :] < jnp.arange(E)[:, None]).astype(MXU_DTYPE)
    col = lambda i: (0, i)
    c2 = lambda i: (0, 0)
    return pl.pallas_call(
        _compact_kernel,
        grid=(T // tm,),
        in_specs=[pl.BlockSpec((E, tm), col), pl.BlockSpec((E, tm), col), pl.BlockSpec((E, tm), col),
                  pl.BlockSpec((E, 1), c2), pl.BlockSpec((E, E), c2)],
        out_specs=[pl.BlockSpec((TOP_K, tm), col), pl.BlockSpec((TOP_K, tm), col)],
        out_shape=[jax.ShapeDtypeStruct((TOP_K, T), jnp.int32), jax.ShapeDtypeStruct((TOP_K, T), jnp.float32)],
        compiler_params=_cparams(("arbitrary",)),
        name="route_compact",
    )(sel_t, w_t, pos_t, pad_start, lower)


def _row_copy(src, s, dst, d, sem):
    return pltpu.make_async_copy(src.at[pl.ds(s, 1)], dst.at[pl.ds(d, 1)], sem)


def _dispatch_kernel(flo_ref, fhi_ref, dest_ref, x_ref, xs_hbm, zero_ref, sem, zsem, *, td):
    step = pl.program_id(0)

    @pl.when(step == 0)
    def _():
        zero_ref[...] = jnp.zeros_like(zero_ref)

        def per_expert(fn):
            def ebody(e, c):
                lax.fori_loop(flo_ref[e], fhi_ref[e], lambda r, c2: (fn(r), c2)[1], 0)
                return c
            lax.fori_loop(0, N_EXPERTS, ebody, 0)

        per_expert(lambda r: _row_copy(zero_ref, 0, xs_hbm, r, zsem).start())
        per_expert(lambda r: _row_copy(zero_ref, 0, xs_hbm, r, zsem).wait())

    def issue(r, c):
        for k in range(TOP_K):
            _row_copy(x_ref, r, xs_hbm, dest_ref[k, r], sem).start()
        return c

    def drain(r, c):
        for k in range(TOP_K):
            _row_copy(x_ref, r, xs_hbm, dest_ref[k, r], sem).wait()
        return c

    lax.fori_loop(0, td, issue, 0)
    lax.fori_loop(0, td, drain, 0)


def _dispatch(dest_t, x1p, fill_lo, fill_hi, n_rows, td):
    T, W = x1p.shape
    return pl.pallas_call(
        functools.partial(_dispatch_kernel, td=td),
        grid_spec=pltpu.PrefetchScalarGridSpec(
            num_scalar_prefetch=2,
            grid=(T // td,),
            in_specs=[
                pl.BlockSpec((TOP_K, td), lambda i, lo, hi: (0, i), memory_space=pltpu.SMEM),
                pl.BlockSpec((td, W), lambda i, lo, hi: (i, 0)),
            ],
            out_specs=pl.BlockSpec(memory_space=pl.ANY),
            scratch_shapes=[pltpu.VMEM((SUBLANES, W), x1p.dtype),
                            pltpu.SemaphoreType.DMA, pltpu.SemaphoreType.DMA],
        ),
        out_shape=jax.ShapeDtypeStruct((n_rows, W), x1p.dtype),
        compiler_params=_cparams(("arbitrary",)),
        name="dispatch",
    )(fill_lo, fill_hi, dest_t, x1p)


def _silu(g):
    return g / (1.0 + jnp.exp(-g))


def _expert_kernel(be_ref, nv_ref, nu_ref, xs_ref, wg_ref, wu_ref, wd_ref, ys_ref, wgb_ref, wub_ref, wdb_ref):
    i = pl.program_id(0)

    @pl.when((i == 0) | (be_ref[i] != be_ref[jnp.maximum(i - 1, 0)]))
    def _():
        wgb_ref[...] = wg_ref[0].astype(MXU_DTYPE)
        wub_ref[...] = wu_ref[0].astype(MXU_DTYPE)
        wdb_ref[...] = wd_ref[0].astype(MXU_DTYPE)

    @pl.when(i < nu_ref[0])
    def _():
        live = lax.broadcasted_iota(jnp.int32, (ROW_BLOCK, 1), 0) < nv_ref[i]
        parts = [jnp.where(live, v, jnp.zeros_like(v)) for v in _unpack_rows(xs_ref[...])]
        dk = wgb_ref.shape[0] // len(parts)

        def proj(w_ref):
            acc = _dot(parts[0], w_ref[0:dk, :])
            for n in range(1, len(parts)):
                acc = acc + _dot(parts[n], w_ref[n * dk:(n + 1) * dk, :])
            return acc

        a = (_silu(proj(wgb_ref)) * proj(wub_ref)).astype(MXU_DTYPE)
        ys_ref[...] = _pack_rows(_dot(a, wdb_ref[...]))


def _experts(xs, block_e, block_valid, n_used, w_gate, w_up, w_down):
    n_rows, W = xs.shape
    D = w_gate.shape[1]
    n_blocks = n_rows // ROW_BLOCK
    blk = lambda i, be, nv, nu: (jnp.minimum(i, nu[0] - 1), 0)
    wsel = lambda i, be, nv, nu: (be[i], 0, 0)
    return pl.pallas_call(
        _expert_kernel,
        grid_spec=pltpu.PrefetchScalarGridSpec(
            num_scalar_prefetch=3,
            grid=(n_blocks,),
            in_specs=[
                pl.BlockSpec((ROW_BLOCK, W), blk),
                pl.BlockSpec((1, D, D_EXPERT), wsel),
                pl.BlockSpec((1, D, D_EXPERT), wsel),
                pl.BlockSpec((1, D_EXPERT, D), wsel),
            ],
            out_specs=pl.BlockSpec((ROW_BLOCK, W), blk),
            scratch_shapes=[pltpu.VMEM((D, D_EXPERT), MXU_DTYPE), pltpu.VMEM((D, D_EXPERT), MXU_DTYPE),
                            pltpu.VMEM((D_EXPERT, D), MXU_DTYPE)],
        ),
        out_shape=jax.ShapeDtypeStruct((n_rows, W), xs.dtype),
        compiler_params=_cparams(("arbitrary",)),
        name="experts",
    )(block_e, block_valid, n_used, xs, w_gate, w_up, w_down)


SC_CORES = 2
SC_SUBCORES = 16
SC_GATHER_ROWS = 64


def _sc_gather_rows(table, idx):
    n = idx.shape[0]
    w = table.shape[1]
    n_workers = SC_CORES * SC_SUBCORES
    per_worker = n // n_workers
    assert n % n_workers == 0 and per_worker % SC_GATHER_ROWS == 0
    mesh = plsc.VectorSubcoreMesh(core_axis_name="c", subcore_axis_name="s")

    @functools.partial(
        pl.kernel, mesh=mesh,
        out_type=jax.ShapeDtypeStruct((n, w), table.dtype),
        scratch_types=[
            pltpu.VMEM((2, SC_GATHER_ROWS), jnp.int32),
            pltpu.VMEM((2, SC_GATHER_ROWS, w), table.dtype),
            pltpu.SemaphoreType.DMA((2,)),
        ],
        name="sc_gather_rows",
    )
    def gather(table_hbm, idx_hbm, out_hbm, idx_v, rows_v, sem):
        wid = lax.axis_index("s") * SC_CORES + lax.axis_index("c")
        base = wid * per_worker
        n_steps = per_worker // SC_GATHER_ROWS

        def gather_copy(slot):
            return pltpu.make_async_copy(table_hbm.at[idx_v.at[slot]], rows_v.at[slot], sem.at[slot])

        def start(step, slot):
            pltpu.sync_copy(idx_hbm.at[pl.ds(base + step * SC_GATHER_ROWS, SC_GATHER_ROWS)], idx_v.at[slot])
            gather_copy(slot).start()

        start(0, 0)

        @pl.loop(0, n_steps, step=2)
        def _(g):
            for slot in range(2):
                step = g + slot

                @pl.when(step + 1 < n_steps)
                def _():
                    start(step + 1, 1 - slot)

                gather_copy(slot).wait()
                pltpu.sync_copy(rows_v.at[slot], out_hbm.at[pl.ds(base + step * SC_GATHER_ROWS, SC_GATHER_ROWS)])

    return gather(table, idx)


SC_SCATTER_ROWS = 64


def _sc_scatter_rows(rows, idx3, n_out):
    n_src, w = rows.shape
    n_chunks, n_dst, batch = idx3.shape
    n_workers = SC_CORES * SC_SUBCORES
    assert batch == SC_SCATTER_ROWS and n_chunks * batch == n_src and n_chunks % (2 * n_workers) == 0
    per_worker = n_chunks // n_workers
    mesh = plsc.VectorSubcoreMesh(core_axis_name="c", subcore_axis_name="s")

    @functools.partial(
        pl.kernel, mesh=mesh,
        out_type=jax.ShapeDtypeStruct((n_out, w), rows.dtype),
        scratch_types=[
            pltpu.VMEM((2, n_dst, batch), jnp.int32),
            pltpu.VMEM((2, batch, w), rows.dtype),
            pltpu.SemaphoreType.DMA((2,)),
            pltpu.SemaphoreType.DMA,
        ],
        name="sc_scatter_rows",
    )
    def scatter(rows_hbm, idx_hbm, out_hbm, idx_v, rows_v, load_sem, store_sem):
        wid = lax.axis_index("s") * SC_CORES + lax.axis_index("c")

        def load_copy(step, slot):
            c = wid * per_worker + step
            return pltpu.make_async_copy(rows_hbm.at[pl.ds(c * batch, batch)], rows_v.at[slot], load_sem.at[slot])

        def load(step, slot):
            pltpu.sync_copy(idx_hbm.at[wid * per_worker + step], idx_v.at[slot])
            load_copy(step, slot).start()

        def store_copy(slot, k):
            return pltpu.make_async_copy(rows_v.at[slot], out_hbm.at[idx_v.at[slot].at[k]], store_sem)

        load(0, 0)

        @pl.loop(0, per_worker, step=2)
        def _(g):
            for slot in range(2):
                step = g + slot

                @pl.when(step + 1 < per_worker)
                def _():
                    load(step + 1, 1 - slot)

                load_copy(step, slot).wait()
                for k in range(n_dst):
                    store_copy(slot, k).start()
                for k in range(n_dst):
                    store_copy(slot, k).wait()

    return scatter(rows, idx3)


def _shared_kernel(x1_ref, wsg_ref, wsu_ref, wsd_ref, o_ref):
    xb = x1_ref[...].astype(MXU_DTYPE)
    a = (_silu(_dot(xb, wsg_ref[...])) * _dot(xb, wsu_ref[...])).astype(MXU_DTYPE)
    o_ref[...] = _dot(a, wsd_ref[...])


def _shared_expert(x1, w_sg, w_su, w_sd, tm):
    T, D = x1.shape
    row = lambda i: (i, 0)
    c2 = lambda i: (0, 0)
    return pl.pallas_call(
        _shared_kernel,
        grid=(T // tm,),
        in_specs=[pl.BlockSpec((tm, D), row), pl.BlockSpec(w_sg.shape, c2), pl.BlockSpec(w_su.shape, c2),
                  pl.BlockSpec(w_sd.shape, c2)],
        out_specs=pl.BlockSpec((tm, D), row),
        out_shape=jax.ShapeDtypeStruct((T, D), jnp.float32),
        compiler_params=_cparams(("arbitrary",)),
        name="shared_expert",
    )(x1, w_sg, w_su, w_sd)


def _combine2_kernel(wk_ref, x1_ref, sh_ref, g_ref_rows, g_ref, b_ref, o_ref):
    x1 = x1_ref[...]
    shared = sh_ref[...]
    wk = wk_ref[...]
    groups = [wk[:, 0:1] * v for v in _unpack_rows_f32(g_ref_rows[0])]
    for k in range(1, TOP_K):
        groups = [g + wk[:, k:k + 1] * v for g, v in zip(groups, _unpack_rows_f32(g_ref_rows[k]))]
    routed = jnp.concatenate(groups, axis=1)
    o_ref[...] = _layer_norm(ALPHA * x1 + (routed + shared), g_ref[...], b_ref[...])


def _combine2(wk, x1, shared, gathered, ln_g, ln_b, tc):
    T, D = x1.shape
    W = gathered.shape[2]
    row = lambda i: (i, 0)
    c2 = lambda i: (0, 0)
    return pl.pallas_call(
        _combine2_kernel,
        grid=(T // tc,),
        in_specs=[
            pl.BlockSpec((tc, TOP_K), row),
            pl.BlockSpec((tc, D), row),
            pl.BlockSpec((tc, D), row),
            pl.BlockSpec((TOP_K, tc, W), lambda i: (0, i, 0)),
            pl.BlockSpec((1, D), c2),
            pl.BlockSpec((1, D), c2),
        ],
        out_specs=pl.BlockSpec((tc, D), row),
        out_shape=jax.ShapeDtypeStruct((T, D), jnp.float32),
        compiler_params=_cparams(("arbitrary",)),
        name="combine",
    )(wk, x1, shared, gathered, ln_g, ln_b)


def _combine_kernel(dest_ref, wk_ref, x1_ref, ys_hbm, wsg_ref, wsu_ref, wsd_ref, g_ref, b_ref,
                    o_ref, buf_ref, sem, *, tc):
    def issue(r, c):
        for k in range(TOP_K):
            _row_copy(ys_hbm, dest_ref[k, r], buf_ref.at[k], r, sem).start()
        return c

    def drain(r, c):
        for k in range(TOP_K):
            _row_copy(ys_hbm, dest_ref[k, r], buf_ref.at[k], r, sem).wait()
        return c

    lax.fori_loop(0, tc, issue, 0)
    x1 = x1_ref[...]
    xb = x1.astype(MXU_DTYPE)
    a = (_silu(_dot(xb, wsg_ref[...])) * _dot(xb, wsu_ref[...])).astype(MXU_DTYPE)
    shared = _dot(a, wsd_ref[...])
    lax.fori_loop(0, tc, drain, 0)
    wk = wk_ref[...]
    groups = [wk[:, 0:1] * v for v in _unpack_rows_f32(buf_ref[0])]
    for k in range(1, TOP_K):
        groups = [g + wk[:, k:k + 1] * v for g, v in zip(groups, _unpack_rows_f32(buf_ref[k]))]
    routed = jnp.concatenate(groups, axis=1)
    o_ref[...] = _layer_norm(ALPHA * x1 + (routed + shared), g_ref[...], b_ref[...])


def _combine(dest_t, wk, x1, ys, w_sg, w_su, w_sd, ln_g, ln_b, tc):
    T, D = x1.shape
    row = lambda i: (i, 0)
    c2 = lambda i: (0, 0)
    return pl.pallas_call(
        functools.partial(_combine_kernel, tc=tc),
        grid=(T // tc,),
        in_specs=[
            pl.BlockSpec((TOP_K, tc), lambda i: (0, i), memory_space=pltpu.SMEM),
            pl.BlockSpec((tc, TOP_K), row),
            pl.BlockSpec((tc, D), row),
            pl.BlockSpec(memory_space=pl.ANY),
            pl.BlockSpec(w_sg.shape, c2),
            pl.BlockSpec(w_su.shape, c2),
            pl.BlockSpec(w_sd.shape, c2),
            pl.BlockSpec((1, D), c2),
            pl.BlockSpec((1, D), c2),
        ],
        out_specs=pl.BlockSpec((tc, D), row),
        out_shape=jax.ShapeDtypeStruct((T, D), jnp.float32),
        scratch_shapes=[pltpu.VMEM((TOP_K, tc, ys.shape[1]), ys.dtype), pltpu.SemaphoreType.DMA],
        compiler_params=_cparams(("arbitrary",)),
        name="combine",
    )(dest_t, wk, x1, ys, w_sg, w_su, w_sd, ln_g, ln_b)


def _split_w_in(w_in):
    bf = MXU_DTYPE
    o_kv = Q_RANK
    o_ki = o_kv + KV_RANK
    o_iw = o_ki + IDX_DIM
    o_rest = o_iw + N_IDX_HEADS
    w_main = jnp.concatenate([w_in[:, :o_ki], w_in[:, o_rest:]], axis=1).astype(bf)
    w_small = jnp.pad(w_in[:, o_ki:o_rest], ((0, 0), (0, LANES - IDX_DIM - N_IDX_HEADS))).astype(bf)
    return w_main, w_small


def _stages(x, mem, w_in, q_norm_g, kv_norm_g, w_uq, w_uk, w_uv, w_qidx, rel_bias, conv_w, w_mem_k, w_mem_v, w_out, ln1_g, ln1_b, w_router, router_bias, w_e_gate, w_e_up, w_e_down, w_s_gate, w_s_up, w_s_down, ln2_g, ln2_b, upto=None):
    B, S, D = x.shape
    T = B * S
    bf = MXU_DTYPE
    l = 0
    res = {}
    x2 = x.reshape(T, D)
    w_main, w_small = _split_w_in(w_in[l])
    cq, ckv, ckvt, kidx, iwt, yb, yc = _proj(
        x2, mem, w_main, w_small, q_norm_g[l].reshape(1, -1), kv_norm_g[l].reshape(1, -1), conv_w[l],
        w_mem_k[l].astype(bf), w_mem_v[l].astype(bf), B, S, tm=min(512, S))
    res.update(c_q=cq, c_kv=ckv, k_idx=kidx, y_b=yb, y_c=yc,
               idx_w=jnp.swapaxes(iwt, 1, 2) / (N_IDX_HEADS ** -0.5 * IDX_DIM ** -0.5))
    if upto == "proj":
        return res
    bias_t = _bias_tiles(rel_bias)
    ya = _dsa(cq, iwt, kidx, ckv, ckvt,
              w_qidx[l].reshape(Q_RANK, -1).astype(bf), w_uq[l].reshape(Q_RANK, -1).astype(bf),
              jnp.transpose(w_uk[l], (1, 0, 2)).astype(bf), jnp.transpose(w_uv[l], (1, 2, 0)).astype(bf),
              bias_t, B, S)
    res.update(y_a=ya)
    if upto == "dsa":
        return res

    x1, x1p, sel_t, w_t, pos_t, cnt = _mix_router(
        x2, ya, yb, yc, w_out[l].astype(bf), ln1_g[l].reshape(1, -1), ln1_b[l].reshape(1, -1),
        w_router[l].T, router_bias[l].reshape(-1, 1), tm=min(512, T))
    res.update(x1=x1)

    counts = cnt[:, 0].astype(jnp.int32)
    padded = (counts + ROW_BLOCK - 1) // ROW_BLOCK * ROW_BLOCK
    pad_end = jnp.cumsum(padded)
    pad_start = pad_end - padded
    n_blocks = -(-(T * TOP_K) // ROW_BLOCK) + N_EXPERTS
    n_rows = n_blocks * ROW_BLOCK
    block_start = jnp.arange(n_blocks, dtype=jnp.int32) * ROW_BLOCK
    block_e = jnp.minimum(jnp.sum((pad_end[None, :] <= block_start[:, None]).astype(jnp.int32), axis=1),
                          N_EXPERTS - 1)
    n_used = (pad_end[-1:] // ROW_BLOCK).astype(jnp.int32)

    dest_t, wk_t = _compact(sel_t, w_t, pos_t, pad_start.astype(jnp.float32).reshape(-1, 1), tm=min(512, T))
    block_valid = jnp.clip((pad_start + counts)[block_e] - block_start, 0, ROW_BLOCK).astype(jnp.int32)
    bt = SC_SCATTER_ROWS
    idx3 = jnp.transpose(dest_t.reshape(TOP_K, T // bt, bt), (1, 0, 2))
    xs = _sc_scatter_rows(x1p, idx3, n_rows)
    ys = _experts(xs, block_e, block_valid, n_used, w_e_gate[l], w_e_up[l], w_e_down[l])
    gathered = _sc_gather_rows(ys, dest_t.reshape(-1)).reshape(TOP_K, T, -1)
    shared = _shared_expert(x1, w_s_gate[l].astype(bf), w_s_up[l].astype(bf), w_s_down[l].astype(bf),
                            tm=min(512, T))
    out = _combine2(wk_t.T, x1, shared, gathered, ln2_g[l].reshape(1, -1), ln2_b[l].reshape(1, -1),
                    tc=min(256, T))
    res.update(out=out.reshape(B, S, D))
    return res


def kernel(x, mem, w_in, q_norm_g, kv_norm_g, w_uq, w_uk, w_uv, w_qidx, rel_bias, conv_w, w_mem_k, w_mem_v, w_out, ln1_g, ln1_b, w_router, router_bias, w_e_gate, w_e_up, w_e_down, w_s_gate, w_s_up, w_s_down, ln2_g, ln2_b):
    return _stages(x, mem, w_in, q_norm_g, kv_norm_g, w_uq, w_uk, w_uv, w_qidx, rel_bias, conv_w, w_mem_k, w_mem_v, w_out, ln1_g, ln1_b, w_router, router_bias, w_e_gate, w_e_up, w_e_down, w_s_gate, w_s_up, w_s_down, ln2_g, ln2_b)["out"]
```

```python
import functools
import math

import jax
import jax.numpy as jnp
from jax import lax
from jax.experimental import pallas as pl
from jax.experimental.pallas import tpu as pltpu
from jax.experimental.pallas import tpu_sc as plsc

N_HEADS_A = 8
HEAD_DIM = 64
Q_RANK = 256
KV_RANK = 128
N_IDX_HEADS = 8
IDX_DIM = 64
TOPK_MAX = 256
REL_BUCKETS = 32
REL_MAX_DIST = 128
CONV_CH = 256
CONV_WIDTH = 3
N_MEM_HEADS = 4
MIX_A = N_HEADS_A * HEAD_DIM
MIX_C = N_MEM_HEADS * HEAD_DIM
N_EXPERTS = 64
N_GROUPS = 8
GROUP_SIZE = N_EXPERTS // N_GROUPS
TOPK_GROUPS = 4
TOP_K = 8
D_EXPERT = 256
ROUTED_SCALE = 2.5
MOE_BLOCK = 256
DEPTH = 1
ALPHA = (2.0 * DEPTH) ** 0.25
LN_EPS = 1e-5
RMS_EPS = 1e-6

LANES = 128
SUBLANES = 8
QB = 128
F32_LOWEST = -3.4028234663852886e38
VMEM_LIMIT = 56 * 1024 * 1024
MXU_DTYPE = jnp.bfloat16
ROW_BLOCK = 512

_NT = (((1,), (1,)), ((), ()))


def _dot(a, b):
    return jnp.dot(a, b, preferred_element_type=jnp.float32)


def _dot_nt(a, b):
    return lax.dot_general(a, b, _NT, preferred_element_type=jnp.float32)


def _cparams(sem):
    return pltpu.CompilerParams(dimension_semantics=sem, vmem_limit_bytes=VMEM_LIMIT)


def _bias_kernel(rb_ref, o_ref):
    s = lax.broadcasted_iota(jnp.int32, (QB, QB), 0)
    t = lax.broadcasted_iota(jnp.int32, (QB, QB), 1)
    max_exact = REL_BUCKETS // 2
    for tile in range(3):
        n = jnp.maximum(t - s + (2 - tile) * QB, 0)
        nf = jnp.maximum(n.astype(jnp.float32), 1.0)
        large = max_exact + (jnp.log(nf / max_exact) / math.log(REL_MAX_DIST / max_exact)
                             * (REL_BUCKETS - max_exact)).astype(jnp.int32)
        large = jnp.minimum(large, REL_BUCKETS - 1)
        bucket = jnp.where(n < max_exact, n, large)
        for h in range(N_HEADS_A):
            acc = jnp.zeros((QB, QB), jnp.float32)
            for b in range(REL_BUCKETS):
                acc = jnp.where(bucket == b, rb_ref[b, h], acc)
            o_ref[tile, h] = acc


def _bias_tiles(rel_bias):
    return pl.pallas_call(
        _bias_kernel,
        in_specs=[pl.BlockSpec(memory_space=pltpu.SMEM)],
        out_specs=pl.BlockSpec(memory_space=pltpu.VMEM),
        out_shape=jax.ShapeDtypeStruct((3, N_HEADS_A, QB, QB), jnp.float32),
        name="bias_tiles",
    )(rel_bias)


_MAIN_COLS = Q_RANK + KV_RANK + 3 * CONV_CH + MIX_C


def _proj_kernel(x_ref, mem_ref, wm_ref, ws_ref, qg_ref, kvg_ref, cw_ref, wmk_ref, wmv_ref,
                 cq_ref, ckv_ref, ckvt_ref, kidx_ref, iwt_ref, yb_ref, yc_ref,
                 carry_ref, mk_ref, mv_ref, *, tm):
    si = pl.program_id(1)

    @pl.when(si == 0)
    def _():
        carry_ref[...] = jnp.zeros_like(carry_ref)
        mb = mem_ref[0].astype(MXU_DTYPE)
        mk_ref[...] = _dot(mb, wmk_ref[...]).astype(MXU_DTYPE)
        mv_ref[...] = _dot(mb, wmv_ref[...]).astype(MXU_DTYPE)

    xb = x_ref[...].astype(MXU_DTYPE)
    p = _dot(xb, wm_ref[...])
    small = _dot(xb, ws_ref[...])

    o = 0
    cq = p[:, o:o + Q_RANK]; o += Q_RANK
    ckv = p[:, o:o + KV_RANK]; o += KV_RANK
    g_b = p[:, o:o + CONV_CH]; o += CONV_CH
    g_c = p[:, o:o + CONV_CH]; o += CONV_CH
    h_c = p[:, o:o + CONV_CH]; o += CONV_CH
    q_mem = p[:, o:o + MIX_C]

    cq = cq * lax.rsqrt(jnp.mean(cq * cq, axis=-1, keepdims=True) + RMS_EPS) * qg_ref[...]
    ckv = ckv * lax.rsqrt(jnp.mean(ckv * ckv, axis=-1, keepdims=True) + RMS_EPS) * kvg_ref[...]
    cq_ref[...] = cq.astype(MXU_DTYPE)
    ckv_b = ckv.astype(MXU_DTYPE)
    ckv_ref[...] = ckv_b
    ckvt_ref[0] = ckv.T.astype(MXU_DTYPE)

    kidx_ref[...] = small[:, :IDX_DIM].astype(MXU_DTYPE)
    small_t = small.T
    iwt_ref[0] = small_t[IDX_DIM:IDX_DIM + N_IDX_HEADS, :] * (N_IDX_HEADS ** -0.5 * IDX_DIM ** -0.5)

    u = g_c * h_c
    rows = lax.broadcasted_iota(jnp.int32, (tm, 1), 0)
    c6 = carry_ref[SUBLANES - 2:SUBLANES - 1, :]
    c7 = carry_ref[SUBLANES - 1:SUBLANES, :]
    u1 = jnp.where(rows == 0, c7, pltpu.roll(u, 1, 0))
    u2 = jnp.where(rows == 0, c6, jnp.where(rows == 1, c7, pltpu.roll(u, 2, 0)))
    y = cw_ref[0:1, :] * u2
    y = y + cw_ref[1:2, :] * u1
    y = y + cw_ref[2:3, :] * u
    yb_ref[...] = (g_b * y).astype(MXU_DTYPE)
    carry_ref[...] = u[tm - SUBLANES:, :]

    qm = q_mem.astype(MXU_DTYPE)
    outs = []
    for h in range(N_MEM_HEADS):
        sl = slice(h * HEAD_DIM, (h + 1) * HEAD_DIM)
        lg = _dot_nt(qm[:, sl], mk_ref[:, sl]) * (HEAD_DIM ** -0.5)
        lg = lg - jnp.max(lg, axis=-1, keepdims=True)
        e = jnp.exp(lg)
        pr = e / jnp.sum(e, axis=-1, keepdims=True)
        outs.append(_dot(pr.astype(MXU_DTYPE), mv_ref[:, sl]))
    yc_ref[...] = jnp.concatenate(outs, axis=-1).astype(MXU_DTYPE)


def _proj(x2, mem, w_main, w_small, q_g, kv_g, conv_w, w_mk, w_mv, B, S, tm):
    T, D = x2.shape
    n_mem = mem.shape[1]
    ns = S // tm
    row = lambda b, s: (b * ns + s, 0)
    const2 = lambda b, s: (0, 0)
    bf = MXU_DTYPE
    return pl.pallas_call(
        functools.partial(_proj_kernel, tm=tm),
        grid=(B, ns),
        in_specs=[
            pl.BlockSpec((tm, D), row),
            pl.BlockSpec((1, n_mem, D), lambda b, s: (b, 0, 0)),
            pl.BlockSpec(w_main.shape, const2),
            pl.BlockSpec(w_small.shape, const2),
            pl.BlockSpec(q_g.shape, const2),
            pl.BlockSpec(kv_g.shape, const2),
            pl.BlockSpec(conv_w.shape, const2),
            pl.BlockSpec(w_mk.shape, const2),
            pl.BlockSpec(w_mv.shape, const2),
        ],
        out_specs=[
            pl.BlockSpec((tm, Q_RANK), row),
            pl.BlockSpec((tm, KV_RANK), row),
            pl.BlockSpec((1, KV_RANK, tm), lambda b, s: (b, 0, s)),
            pl.BlockSpec((tm, IDX_DIM), row),
            pl.BlockSpec((1, N_IDX_HEADS, tm), lambda b, s: (b, 0, s)),
            pl.BlockSpec((tm, CONV_CH), row),
            pl.BlockSpec((tm, MIX_C), row),
        ],
        out_shape=[
            jax.ShapeDtypeStruct((T, Q_RANK), bf),
            jax.ShapeDtypeStruct((T, KV_RANK), bf),
            jax.ShapeDtypeStruct((B, KV_RANK, S), bf),
            jax.ShapeDtypeStruct((T, IDX_DIM), bf),
            jax.ShapeDtypeStruct((B, N_IDX_HEADS, S), jnp.float32),
            jax.ShapeDtypeStruct((T, CONV_CH), bf),
            jax.ShapeDtypeStruct((T, MIX_C), bf),
        ],
        scratch_shapes=[
            pltpu.VMEM((SUBLANES, CONV_CH), jnp.float32),
            pltpu.VMEM((n_mem, MIX_C), bf),
            pltpu.VMEM((n_mem, MIX_C), bf),
        ],
        compiler_params=_cparams(("arbitrary", "arbitrary")),
        name="proj",
    )(x2, mem, w_main, w_small, q_g, kv_g, conv_w, w_mk, w_mv)


def _key_to_f32(key):
    bits = jnp.where(key < 0, key ^ jnp.int32(0x7FFFFFFF), key)
    return pltpu.bitcast(bits, jnp.float32)


def _colsum8(v):
    return jnp.sum(v.reshape(QB // SUBLANES, SUBLANES, QB), axis=0)


def _colmax8(v):
    return jnp.max(v.reshape(QB // SUBLANES, SUBLANES, QB), axis=0)


UNROLL_WIDTHS = (4, 2, 1)


def _dsa_kernel(cq_ref, iwt_ref, kidx_ref, ckv_ref, ckvt_ref, wqi_ref, wuq_ref, wuk_ref, wuvt_ref,
                bias_ref, o_ref, qidx_ref, qlat_ref, score_ref, mask_ref, logit_ref, acc_ref,
                *, k_sel, idx_bits):
    i = pl.program_id(1)
    f32 = jnp.float32
    bf = MXU_DTYPE
    n_blocks = i + 1
    s_loc = lax.broadcasted_iota(jnp.int32, (QB, QB), 0)
    t_glob = i * QB + lax.broadcasted_iota(jnp.int32, (QB, QB), 1)

    def blk(jb):
        return pl.multiple_of(jb * QB, QB)

    def block_loop(fn, init):
        c, start = init, 0
        for width in UNROLL_WIDTHS:
            n = (n_blocks - start) // width
            c = lax.fori_loop(0, n, lambda it, c, w=width, s=start: fn(s + it * w, w, c), c)
            start = start + n * width
        return c

    cq = cq_ref[...]
    q_all = _dot(cq, wuq_ref[...]).astype(bf)
    for h in range(N_HEADS_A):
        qidx_ref[h * QB:(h + 1) * QB, :] = _dot(cq, wqi_ref[:, h * IDX_DIM:(h + 1) * IDX_DIM]).astype(bf)
        qlat_ref[h * QB:(h + 1) * QB, :] = (
            _dot_nt(q_all[:, h * HEAD_DIM:(h + 1) * HEAD_DIM], wuk_ref[h]) * (HEAD_DIM ** -0.5)).astype(bf)
    iw = iwt_ref[0]

    def score_body(jb0, nb, c):
        d_blk = _dot_nt(kidx_ref[pl.ds(blk(jb0), nb * QB), :], qidx_ref[...])
        for sb in range(nb):
            off = blk(jb0 + sb)
            d_all = d_blk[sb * QB:(sb + 1) * QB, :]
            acc = jnp.maximum(d_all[:, 0:QB], 0.0) * iw[0:1, :]
            for h in range(1, N_IDX_HEADS):
                acc = acc + jnp.maximum(d_all[:, h * QB:(h + 1) * QB], 0.0) * iw[h:h + 1, :]
            score_ref[pl.ds(off, QB), :] = jnp.where(s_loc + off <= t_glob, acc + 0.0, F32_LOWEST)
        return c

    block_loop(score_body, 0)

    def count_where(pred):
        def body(jb0, nb, acc):
            for sb in range(nb):
                off = blk(jb0 + sb)
                acc = acc + _colsum8(jnp.where(pred(score_ref[pl.ds(off, QB), :], off), 1.0, 0.0))
            return acc
        acc = block_loop(body, jnp.zeros((SUBLANES, QB), f32))
        return jnp.sum(acc, axis=0, keepdims=True)

    kf = float(k_sel)

    def search():
        c0 = count_where(lambda sc, off: sc >= 0.0)
        cand0 = jnp.where(c0 >= kf, jnp.int32(0), jnp.int32(-2 ** 31))

        def bit_body(it, cand):
            trial = cand + lax.shift_left(jnp.int32(1), 30 - it)
            tf = _key_to_f32(trial)
            cnt = count_where(lambda sc, off: sc >= tf)
            return jnp.where(cnt >= kf, trial, cand)

        cand = lax.fori_loop(0, 31, bit_body, cand0)
        thr = _key_to_f32(cand)
        n_gt = count_where(lambda sc, off: sc > thr)
        n_eq = count_where(lambda sc, off: sc == thr)
        need = kf - n_gt

        def tie_search():
            def tbody(it, xcut):
                trial = xcut + lax.shift_left(jnp.int32(1), idx_bits - 1 - it)
                cnt = count_where(lambda sc, off: (sc == thr) & (s_loc + off < trial))
                return jnp.where(cnt < need, trial, xcut)
            return lax.fori_loop(0, idx_bits, tbody, jnp.zeros((1, QB), jnp.int32))

        any_extra = jnp.max(n_eq - need) > 0.0
        xcut = lax.cond(any_extra, tie_search, lambda: jnp.full((1, QB), 2 ** idx_bits - 1, jnp.int32))
        return thr, xcut

    def no_search():
        return jnp.full((1, QB), F32_LOWEST, f32), jnp.full((1, QB), 2 ** idx_bits - 1, jnp.int32)

    thr, xcut = lax.cond((i + 1) * QB > k_sel, search, no_search)

    def mask_body(jb0, nb, c):
        for sb in range(nb):
            off = blk(jb0 + sb)
            sc = score_ref[pl.ds(off, QB), :]
            s_glob = s_loc + off
            keep = ((sc > thr) | ((sc == thr) & (s_glob <= xcut))) & (s_glob <= t_glob)
            mask_ref[pl.ds(off, QB), :] = jnp.where(keep, 0.0, -jnp.inf)
        return c

    block_loop(mask_body, 0)

    def p1_body(jb0, nb, m8):
        m8 = list(m8)
        lg_blk = _dot_nt(ckv_ref[pl.ds(blk(jb0), nb * QB), :], qlat_ref[...])
        for sb in range(nb):
            off = blk(jb0 + sb)
            lg = lg_blk[sb * QB:(sb + 1) * QB, :]
            msk = mask_ref[pl.ds(off, QB), :]
            bsel = jnp.clip(jb0 + sb - i + 2, 0, 2)
            for h in range(N_HEADS_A):
                lgh = lg[:, h * QB:(h + 1) * QB] + bias_ref[bsel, h] + msk
                logit_ref[pl.ds(off, QB), h * QB:(h + 1) * QB] = lgh
                m8[h] = jnp.maximum(m8[h], _colmax8(lgh))
        return tuple(m8)

    m8 = block_loop(p1_body, tuple(jnp.full((SUBLANES, QB), -jnp.inf, f32) for _ in range(N_HEADS_A)))
    m_row = [jnp.max(m, axis=0, keepdims=True) for m in m8]

    acc_ref[...] = jnp.zeros_like(acc_ref)

    def p2_body(jb0, nb, l8):
        l8 = list(l8)
        off = blk(jb0)
        rows = nb * QB
        ps = []
        for h in range(N_HEADS_A):
            p = jnp.exp(logit_ref[pl.ds(off, rows), h * QB:(h + 1) * QB] - m_row[h])
            l8[h] = l8[h] + jnp.sum(p.reshape(rows // SUBLANES, SUBLANES, QB), axis=0)
            ps.append(p.astype(bf))
        acc_ref[...] += _dot(ckvt_ref[0, :, pl.ds(off, rows)], jnp.concatenate(ps, axis=1))
        return tuple(l8)

    l8 = block_loop(p2_body, tuple(jnp.zeros((SUBLANES, QB), f32) for _ in range(N_HEADS_A)))

    outs = []
    for h in range(N_HEADS_A):
        l_row = jnp.sum(l8[h], axis=0, keepdims=True)
        o_lat_t = (acc_ref[:, h * QB:(h + 1) * QB] / l_row).astype(bf)
        outs.append(_dot(wuvt_ref[h], o_lat_t))
    o_ref[...] = jnp.concatenate(outs, axis=0).T.astype(o_ref.dtype)


def _dsa(cq, iwt, kidx, ckv, ckvt, w_qidx, w_uq, w_uk_h, w_uvt_h, bias_tiles, B, S):
    T = cq.shape[0]
    assert S % QB == 0 and QB >= REL_MAX_DIST
    nq = S // QB
    k_sel = min(TOPK_MAX, S // 4)
    idx_bits = max(1, (S - 1).bit_length())
    c2 = lambda b, i: (0, 0)
    c3 = lambda b, i: (0, 0, 0)
    return pl.pallas_call(
        functools.partial(_dsa_kernel, k_sel=k_sel, idx_bits=idx_bits),
        grid=(B, nq),
        in_specs=[
            pl.BlockSpec((QB, Q_RANK), lambda b, i: (b * nq + i, 0)),
            pl.BlockSpec((1, N_IDX_HEADS, QB), lambda b, i: (b, 0, i)),
            pl.BlockSpec((S, IDX_DIM), lambda b, i: (b, 0)),
            pl.BlockSpec((S, KV_RANK), lambda b, i: (b, 0)),
            pl.BlockSpec((1, KV_RANK, S), lambda b, i: (b, 0, 0)),
            pl.BlockSpec(w_qidx.shape, c2),
            pl.BlockSpec(w_uq.shape, c2),
            pl.BlockSpec(w_uk_h.shape, c3),
            pl.BlockSpec(w_uvt_h.shape, c3),
            pl.BlockSpec(bias_tiles.shape, lambda b, i: (0, 0, 0, 0)),
        ],
        out_specs=pl.BlockSpec((QB, MIX_A), lambda b, i: (b * nq + i, 0)),
        out_shape=jax.ShapeDtypeStruct((T, MIX_A), MXU_DTYPE),
        scratch_shapes=[
            pltpu.VMEM((N_IDX_HEADS * QB, IDX_DIM), MXU_DTYPE),
            pltpu.VMEM((N_HEADS_A * QB, KV_RANK), MXU_DTYPE),
            pltpu.VMEM((S, QB), jnp.float32),
            pltpu.VMEM((S, QB), jnp.float32),
            pltpu.VMEM((S, N_HEADS_A * QB), jnp.float32),
            pltpu.VMEM((KV_RANK, N_HEADS_A * QB), jnp.float32),
        ],
        compiler_params=_cparams(("arbitrary", "arbitrary")),
        name="dsa",
    )(cq, iwt, kidx, ckv, ckvt, w_qidx, w_uq, w_uk_h, w_uvt_h, bias_tiles)


def _layer_norm(xf, g, b):
    mu = jnp.mean(xf, axis=-1, keepdims=True)
    xc = xf - mu
    var = jnp.mean(xc * xc, axis=-1, keepdims=True)
    return xc * lax.rsqrt(var + LN_EPS) * g + b


def _rank_rows(v, n):
    ri = lax.broadcasted_iota(jnp.int32, v.shape, 0)
    rank = jnp.zeros(v.shape, jnp.float32)
    for r2 in range(n):
        row = v[r2:r2 + 1, :]
        beats = (row > v) | ((row == v) & (ri > r2))
        rank = rank + jnp.where(beats, 1.0, 0.0)
    return rank


def _pack_factor():
    return 4 // jnp.dtype(MXU_DTYPE).itemsize


def _pack_rows(x):
    if _pack_factor() == 1:
        return pltpu.bitcast(x, jnp.int32)
    half = x.shape[1] // 2
    b = pltpu.bitcast(x.astype(MXU_DTYPE).astype(jnp.float32), jnp.int32)
    return b[:, half:] | lax.shift_right_logical(b[:, :half], jnp.int32(16))


_HIGH_HALF = -(1 << 16)


def _unpack_rows_f32(p):
    if _pack_factor() == 1:
        return [pltpu.bitcast(p, jnp.float32)]
    lo = pltpu.bitcast(lax.shift_left(p, jnp.int32(16)), jnp.float32)
    hi = pltpu.bitcast(p & jnp.int32(_HIGH_HALF), jnp.float32)
    return [lo, hi]


def _unpack_rows(p):
    return [v.astype(MXU_DTYPE) for v in _unpack_rows_f32(p)]


def _mix_router_kernel(x_ref, ya_ref, yb_ref, yc_ref, wo_ref, g_ref, b_ref, wrt_ref, rb_ref, exp_ref,
                       x1_ref, x1p_ref, sel_ref, w_ref, pos_ref, cnt_ref, base_ref, *, tm):
    step = pl.program_id(0)
    f32 = jnp.float32

    @pl.when(step == 0)
    def _():
        base_ref[...] = jnp.zeros_like(base_ref)

    mix = _dot(ya_ref[...], wo_ref[0:MIX_A, :])
    mix = mix + _dot(yb_ref[...], wo_ref[MIX_A:MIX_A + CONV_CH, :])
    mix = mix + _dot(yc_ref[...], wo_ref[MIX_A + CONV_CH:, :])
    x1 = _layer_norm(ALPHA * x_ref[...] + mix, g_ref[...], b_ref[...])
    x1_ref[...] = x1
    x1p_ref[...] = _pack_rows(x1)

    lg = lax.dot_general(wrt_ref[...], x1, _NT, precision=lax.Precision.HIGHEST, preferred_element_type=f32)
    s = 1.0 / (1.0 + jnp.exp(-lg))
    sc = s + rb_ref[...]

    g3 = sc.reshape(N_GROUPS, GROUP_SIZE, tm)
    m1 = jnp.max(g3, axis=1, keepdims=True)
    is_m1 = g3 == m1
    n_m1 = jnp.sum(jnp.where(is_m1, 1.0, 0.0), axis=1, keepdims=True)
    m2 = jnp.max(jnp.where(is_m1, -jnp.inf, g3), axis=1, keepdims=True)
    gscore = (m1 + jnp.where(n_m1 > 1.0, m1, m2)).reshape(N_GROUPS, tm)
    gsel = jnp.where(_rank_rows(gscore, N_GROUPS) < float(TOPK_GROUPS), 1.0, 0.0)
    emask = _dot(exp_ref[...], gsel.astype(MXU_DTYPE)) > 0.5
    masked = jnp.where(emask, sc, -jnp.inf)
    sel = (_rank_rows(masked, N_EXPERTS) < float(TOP_K)) & emask
    self_ = jnp.where(sel, 1.0, 0.0)
    top_s = jnp.where(sel, s, 0.0)
    w = top_s / jnp.sum(top_s, axis=0, keepdims=True) * ROUTED_SCALE

    t_r = lax.broadcasted_iota(jnp.int32, (tm, tm), 0)
    t_c = lax.broadcasted_iota(jnp.int32, (tm, tm), 1)
    upper = jnp.where(t_r < t_c, 1.0, 0.0).astype(MXU_DTYPE)
    pref = _dot(self_.astype(MXU_DTYPE), upper)
    base = base_ref[...]
    sel_ref[...] = self_
    w_ref[...] = w
    pos_ref[...] = base + pref
    base = base + jnp.sum(self_, axis=1, keepdims=True)
    base_ref[...] = base
    cnt_ref[...] = jnp.broadcast_to(base, cnt_ref.shape)


def _mix_router(x2, ya, yb, yc, w_out, ln_g, ln_b, w_router_t, router_bias, tm):
    T, D = x2.shape
    E = N_EXPERTS
    expand = (jnp.arange(E)[:, None] // GROUP_SIZE == jnp.arange(N_GROUPS)[None, :]).astype(MXU_DTYPE)
    row = lambda i: (i, 0)
    col = lambda i: (0, i)
    c2 = lambda i: (0, 0)
    f32 = jnp.float32
    return pl.pallas_call(
        functools.partial(_mix_router_kernel, tm=tm),
        grid=(T // tm,),
        in_specs=[
            pl.BlockSpec((tm, D), row),
            pl.BlockSpec((tm, MIX_A), row),
            pl.BlockSpec((tm, CONV_CH), row),
            pl.BlockSpec((tm, MIX_C), row),
            pl.BlockSpec(w_out.shape, c2),
            pl.BlockSpec((1, D), c2),
            pl.BlockSpec((1, D), c2),
            pl.BlockSpec((E, D), c2),
            pl.BlockSpec((E, 1), c2),
            pl.BlockSpec((E, N_GROUPS), c2),
        ],
        out_specs=[
            pl.BlockSpec((tm, D), row),
            pl.BlockSpec((tm, D // _pack_factor()), row),
            pl.BlockSpec((E, tm), col),
            pl.BlockSpec((E, tm), col),
            pl.BlockSpec((E, tm), col),
            pl.BlockSpec((E, LANES), c2),
        ],
        out_shape=[
            jax.ShapeDtypeStruct((T, D), f32),
            jax.ShapeDtypeStruct((T, D // _pack_factor()), jnp.int32),
            jax.ShapeDtypeStruct((E, T), f32),
            jax.ShapeDtypeStruct((E, T), f32),
            jax.ShapeDtypeStruct((E, T), f32),
            jax.ShapeDtypeStruct((E, LANES), f32),
        ],
        scratch_shapes=[pltpu.VMEM((E, 1), f32)],
        compiler_params=_cparams(("arbitrary",)),
        name="mix_router",
    )(x2, ya, yb, yc, w_out, ln_g, ln_b, w_router_t, router_bias, expand)


def _compact_kernel(sel_ref, w_ref, pos_ref, pstart_ref, low_ref, dest_ref, wk_ref):
    sel = sel_ref[...]
    on = sel > 0.5
    rank = _dot(low_ref[...], sel.astype(MXU_DTYPE))
    row = pstart_ref[...] + pos_ref[...]
    w = w_ref[...]
    dests, ws = [], []
    for k in range(TOP_K):
        m = on & (rank == float(k))
        dests.append(jnp.sum(jnp.where(m, row, 0.0), axis=0, keepdims=True))
        ws.append(jnp.sum(jnp.where(m, w, 0.0), axis=0, keepdims=True))
    dest_ref[...] = jnp.concatenate(dests, axis=0).astype(jnp.int32)
    wk_ref[...] = jnp.concatenate(ws, axis=0)


def _compact(sel_t, w_t, pos_t, pad_start, tm):
    E, T = sel_t.shape
    lower = (jnp.arange(E)[None, :] < jnp.arange(E)[:, None]).astype(MXU_DTYPE)
    col = lambda i: (0, i)
    c2 = lambda i: (0, 0)
    return pl.pallas_call(
        _compact_kernel,
        grid=(T // tm,),
        in_specs=[pl.BlockSpec((E, tm), col), pl.BlockSpec((E, tm), col), pl.BlockSpec((E, tm), col),
                  pl.BlockSpec((E, 1), c2), pl.BlockSpec((E, E), c2)],
        out_specs=[pl.BlockSpec((TOP_K, tm), col), pl.BlockSpec((TOP_K, tm), col)],
        out_shape=[jax.ShapeDtypeStruct((TOP_K, T), jnp.int32), jax.ShapeDtypeStruct((TOP_K, T), jnp.float32)],
        compiler_params=_cparams(("arbitrary",)),
        name="route_compact",
    )(sel_t, w_t, pos_t, pad_start, lower)


def _row_copy(src, s, dst, d, sem):
    return pltpu.make_async_copy(src.at[pl.ds(s, 1)], dst.at[pl.ds(d, 1)], sem)


def _dispatch_kernel(flo_ref, fhi_ref, dest_ref, x_ref, xs_hbm, zero_ref, sem, zsem, *, td):
    step = pl.program_id(0)

    @pl.when(step == 0)
    def _():
        zero_ref[...] = jnp.zeros_like(zero_ref)

        def per_expert(fn):
            def ebody(e, c):
                lax.fori_loop(flo_ref[e], fhi_ref[e], lambda r, c2: (fn(r), c2)[1], 0)
                return c
            lax.fori_loop(0, N_EXPERTS, ebody, 0)

        per_expert(lambda r: _row_copy(zero_ref, 0, xs_hbm, r, zsem).start())
        per_expert(lambda r: _row_copy(zero_ref, 0, xs_hbm, r, zsem).wait())

    def issue(r, c):
        for k in range(TOP_K):
            _row_copy(x_ref, r, xs_hbm, dest_ref[k, r], sem).start()
        return c

    def drain(r, c):
        for k in range(TOP_K):
            _row_copy(x_ref, r, xs_hbm, dest_ref[k, r], sem).wait()
        return c

    lax.fori_loop(0, td, issue, 0)
    lax.fori_loop(0, td, drain, 0)


def _dispatch(dest_t, x1p, fill_lo, fill_hi, n_rows, td):
    T, W = x1p.shape
    return pl.pallas_call(
        functools.partial(_dispatch_kernel, td=td),
        grid_spec=pltpu.PrefetchScalarGridSpec(
            num_scalar_prefetch=2,
            grid=(T // td,),
            in_specs=[
                pl.BlockSpec((TOP_K, td), lambda i, lo, hi: (0, i), memory_space=pltpu.SMEM),
                pl.BlockSpec((td, W), lambda i, lo, hi: (i, 0)),
            ],
            out_specs=pl.BlockSpec(memory_space=pl.ANY),
            scratch_shapes=[pltpu.VMEM((SUBLANES, W), x1p.dtype),
                            pltpu.SemaphoreType.DMA, pltpu.SemaphoreType.DMA],
        ),
        out_shape=jax.ShapeDtypeStruct((n_rows, W), x1p.dtype),
        compiler_params=_cparams(("arbitrary",)),
        name="dispatch",
    )(fill_lo, fill_hi, dest_t, x1p)


def _silu(g):
    return g / (1.0 + jnp.exp(-g))


def _expert_kernel(be_ref, nv_ref, nu_ref, xs_ref, wg_ref, wu_ref, wd_ref, ys_ref, wgb_ref, wub_ref, wdb_ref):
    i = pl.program_id(0)

    @pl.when((i == 0) | (be_ref[i] != be_ref[jnp.maximum(i - 1, 0)]))
    def _():
        wgb_ref[...] = wg_ref[0].astype(MXU_DTYPE)
        wub_ref[...] = wu_ref[0].astype(MXU_DTYPE)
        wdb_ref[...] = wd_ref[0].astype(MXU_DTYPE)

    @pl.when(i < nu_ref[0])
    def _():
        live = lax.broadcasted_iota(jnp.int32, (ROW_BLOCK, 1), 0) < nv_ref[i]
        parts = [jnp.where(live, v, jnp.zeros_like(v)) for v in _unpack_rows(xs_ref[...])]
        dk = wgb_ref.shape[0] // len(parts)

        def proj(w_ref):
            acc = _dot(parts[0], w_ref[0:dk, :])
            for n in range(1, len(parts)):
                acc = acc + _dot(parts[n], w_ref[n * dk:(n + 1) * dk, :])
            return acc

        a = (_silu(proj(wgb_ref)) * proj(wub_ref)).astype(MXU_DTYPE)
        ys_ref[...] = _pack_rows(_dot(a, wdb_ref[...]))


def _experts(xs, block_e, block_valid, n_used, w_gate, w_up, w_down):
    n_rows, W = xs.shape
    D = w_gate.shape[1]
    n_blocks = n_rows // ROW_BLOCK
    blk = lambda i, be, nv, nu: (jnp.minimum(i, nu[0] - 1), 0)
    wsel = lambda i, be, nv, nu: (be[i], 0, 0)
    return pl.pallas_call(
        _expert_kernel,
        grid_spec=pltpu.PrefetchScalarGridSpec(
            num_scalar_prefetch=3,
            grid=(n_blocks,),
            in_specs=[
                pl.BlockSpec((ROW_BLOCK, W), blk),
                pl.BlockSpec((1, D, D_EXPERT), wsel),
                pl.BlockSpec((1, D, D_EXPERT), wsel),
                pl.BlockSpec((1, D_EXPERT, D), wsel),
            ],
            out_specs=pl.BlockSpec((ROW_BLOCK, W), blk),
            scratch_shapes=[pltpu.VMEM((D, D_EXPERT), MXU_DTYPE), pltpu.VMEM((D, D_EXPERT), MXU_DTYPE),
                            pltpu.VMEM((D_EXPERT, D), MXU_DTYPE)],
        ),
        out_shape=jax.ShapeDtypeStruct((n_rows, W), xs.dtype),
        compiler_params=_cparams(("arbitrary",)),
        name="experts",
    )(block_e, block_valid, n_used, xs, w_gate, w_up, w_down)


SC_CORES = 2
SC_SUBCORES = 16
SC_GATHER_ROWS = 64


def _sc_gather_rows(table, idx):
    n = idx.shape[0]
    w = table.shape[1]
    n_workers = SC_CORES * SC_SUBCORES
    per_worker = n // n_workers
    assert n % n_workers == 0 and per_worker % SC_GATHER_ROWS == 0
    mesh = plsc.VectorSubcoreMesh(core_axis_name="c", subcore_axis_name="s")

    @functools.partial(
        pl.kernel, mesh=mesh,
        out_type=jax.ShapeDtypeStruct((n, w), table.dtype),
        scratch_types=[
            pltpu.VMEM((2, SC_GATHER_ROWS), jnp.int32),
            pltpu.VMEM((2, SC_GATHER_ROWS, w), table.dtype),
            pltpu.SemaphoreType.DMA((2,)),
        ],
        name="sc_gather_rows",
    )
    def gather(table_hbm, idx_hbm, out_hbm, idx_v, rows_v, sem):
        wid = lax.axis_index("s") * SC_CORES + lax.axis_index("c")
        base = wid * per_worker
        n_steps = per_worker // SC_GATHER_ROWS

        def gather_copy(slot):
            return pltpu.make_async_copy(table_hbm.at[idx_v.at[slot]], rows_v.at[slot], sem.at[slot])

        def start(step, slot):
            pltpu.sync_copy(idx_hbm.at[pl.ds(base + step * SC_GATHER_ROWS, SC_GATHER_ROWS)], idx_v.at[slot])
            gather_copy(slot).start()

        start(0, 0)

        @pl.loop(0, n_steps, step=2)
        def _(g):
            for slot in range(2):
                step = g + slot

                @pl.when(step + 1 < n_steps)
                def _():
                    start(step + 1, 1 - slot)

                gather_copy(slot).wait()
                pltpu.sync_copy(rows_v.at[slot], out_hbm.at[pl.ds(base + step * SC_GATHER_ROWS, SC_GATHER_ROWS)])

    return gather(table, idx)


SC_SCATTER_ROWS = 64


def _sc_scatter_rows(rows, idx3, n_out):
    n_src, w = rows.shape
    n_chunks, n_dst, batch = idx3.shape
    n_workers = SC_CORES * SC_SUBCORES
    assert batch == SC_SCATTER_ROWS and n_chunks * batch == n_src and n_chunks % (2 * n_workers) == 0
    per_worker = n_chunks // n_workers
    mesh = plsc.VectorSubcoreMesh(core_axis_name="c", subcore_axis_name="s")

    @functools.partial(
        pl.kernel, mesh=mesh,
        out_type=jax.ShapeDtypeStruct((n_out, w), rows.dtype),
        scratch_types=[
            pltpu.VMEM((2, n_dst, batch), jnp.int32),
            pltpu.VMEM((2, batch, w), rows.dtype),
            pltpu.SemaphoreType.DMA((2,)),
            pltpu.SemaphoreType.DMA,
        ],
        name="sc_scatter_rows",
    )
    def scatter(rows_hbm, idx_hbm, out_hbm, idx_v, rows_v, load_sem, store_sem):
        wid = lax.axis_index("s") * SC_CORES + lax.axis_index("c")

        def load_copy(step, slot):
            c = wid * per_worker + step
            return pltpu.make_async_copy(rows_hbm.at[pl.ds(c * batch, batch)], rows_v.at[slot], load_sem.at[slot])

        def load(step, slot):
            pltpu.sync_copy(idx_hbm.at[wid * per_worker + step], idx_v.at[slot])
            load_copy(step, slot).start()

        def store_copy(slot, k):
            return pltpu.make_async_copy(rows_v.at[slot], out_hbm.at[idx_v.at[slot].at[k]], store_sem)

        load(0, 0)

        @pl.loop(0, per_worker, step=2)
        def _(g):
            for slot in range(2):
                step = g + slot

                @pl.when(step + 1 < per_worker)
                def _():
                    load(step + 1, 1 - slot)

                load_copy(step, slot).wait()
                for k in range(n_dst):
                    store_copy(slot, k).start()
                for k in range(n_dst):
                    store_copy(slot, k).wait()

    return scatter(rows, idx3)


def _shared_kernel(x1_ref, wsg_ref, wsu_ref, wsd_ref, o_ref):
    xb = x1_ref[...].astype(MXU_DTYPE)
    a = (_silu(_dot(xb, wsg_ref[...])) * _dot(xb, wsu_ref[...])).astype(MXU_DTYPE)
    o_ref[...] = _dot(a, wsd_ref[...])


def _shared_expert(x1, w_sg, w_su, w_sd, tm):
    T, D = x1.shape
    row = lambda i: (i, 0)
    c2 = lambda i: (0, 0)
    return pl.pallas_call(
        _shared_kernel,
        grid=(T // tm,),
        in_specs=[pl.BlockSpec((tm, D), row), pl.BlockSpec(w_sg.shape, c2), pl.BlockSpec(w_su.shape, c2),
                  pl.BlockSpec(w_sd.shape, c2)],
        out_specs=pl.BlockSpec((tm, D), row),
        out_shape=jax.ShapeDtypeStruct((T, D), jnp.float32),
        compiler_params=_cparams(("arbitrary",)),
        name="shared_expert",
    )(x1, w_sg, w_su, w_sd)


def _combine2_kernel(wk_ref, x1_ref, g_ref_rows, wsg_ref, wsu_ref, wsd_ref, g_ref, b_ref, o_ref):
    x1 = x1_ref[...]
    xb = x1.astype(MXU_DTYPE)
    a = (_silu(_dot(xb, wsg_ref[...])) * _dot(xb, wsu_ref[...])).astype(MXU_DTYPE)
    shared = _dot(a, wsd_ref[...])
    wk = wk_ref[...].T
    groups = [wk[:, 0:1] * v for v in _unpack_rows_f32(g_ref_rows[0])]
    for k in range(1, TOP_K):
        groups = [g + wk[:, k:k + 1] * v for g, v in zip(groups, _unpack_rows_f32(g_ref_rows[k]))]
    routed = jnp.concatenate(groups, axis=1)
    o_ref[...] = _layer_norm(ALPHA * x1 + (routed + shared), g_ref[...], b_ref[...])


def _combine2(wk_t, x1, gathered, w_sg, w_su, w_sd, ln_g, ln_b, tc):
    T, D = x1.shape
    W = gathered.shape[2]
    row = lambda i: (i, 0)
    c2 = lambda i: (0, 0)
    return pl.pallas_call(
        _combine2_kernel,
        grid=(T // tc,),
        in_specs=[
            pl.BlockSpec((TOP_K, tc), lambda i: (0, i)),
            pl.BlockSpec((tc, D), row),
            pl.BlockSpec((TOP_K, tc, W), lambda i: (0, i, 0)),
            pl.BlockSpec(w_sg.shape, c2),
            pl.BlockSpec(w_su.shape, c2),
            pl.BlockSpec(w_sd.shape, c2),
            pl.BlockSpec((1, D), c2),
            pl.BlockSpec((1, D), c2),
        ],
        out_specs=pl.BlockSpec((tc, D), row),
        out_shape=jax.ShapeDtypeStruct((T, D), jnp.float32),
        compiler_params=_cparams(("arbitrary",)),
        name="combine",
    )(wk_t, x1, gathered, w_sg, w_su, w_sd, ln_g, ln_b)


def _combine_kernel(dest_ref, wk_ref, x1_ref, ys_hbm, wsg_ref, wsu_ref, wsd_ref, g_ref, b_ref,
                    o_ref, buf_ref, sem, *, tc):
    def issue(r, c):
        for k in range(TOP_K):
            _row_copy(ys_hbm, dest_ref[k, r], buf_ref.at[k], r, sem).start()
        return c

    def drain(r, c):
        for k in range(TOP_K):
            _row_copy(ys_hbm, dest_ref[k, r], buf_ref.at[k], r, sem).wait()
        return c

    lax.fori_loop(0, tc, issue, 0)
    x1 = x1_ref[...]
    xb = x1.astype(MXU_DTYPE)
    a = (_silu(_dot(xb, wsg_ref[...])) * _dot(xb, wsu_ref[...])).astype(MXU_DTYPE)
    shared = _dot(a, wsd_ref[...])
    lax.fori_loop(0, tc, drain, 0)
    wk = wk_ref[...]
    groups = [wk[:, 0:1] * v for v in _unpack_rows_f32(buf_ref[0])]
    for k in range(1, TOP_K):
        groups = [g + wk[:, k:k + 1] * v for g, v in zip(groups, _unpack_rows_f32(buf_ref[k]))]
    routed = jnp.concatenate(groups, axis=1)
    o_ref[...] = _layer_norm(ALPHA * x1 + (routed + shared), g_ref[...], b_ref[...])


def _combine(dest_t, wk, x1, ys, w_sg, w_su, w_sd, ln_g, ln_b, tc):
    T, D = x1.shape
    row = lambda i: (i, 0)
    c2 = lambda i: (0, 0)
    return pl.pallas_call(
        functools.partial(_combine_kernel, tc=tc),
        grid=(T // tc,),
        in_specs=[
            pl.BlockSpec((TOP_K, tc), lambda i: (0, i), memory_space=pltpu.SMEM),
            pl.BlockSpec((tc, TOP_K), row),
            pl.BlockSpec((tc, D), row),
            pl.BlockSpec(memory_space=pl.ANY),
            pl.BlockSpec(w_sg.shape, c2),
            pl.BlockSpec(w_su.shape, c2),
            pl.BlockSpec(w_sd.shape, c2),
            pl.BlockSpec((1, D), c2),
            pl.BlockSpec((1, D), c2),
        ],
        out_specs=pl.BlockSpec((tc, D), row),
        out_shape=jax.ShapeDtypeStruct((T, D), jnp.float32),
        scratch_shapes=[pltpu.VMEM((TOP_K, tc, ys.shape[1]), ys.dtype), pltpu.SemaphoreType.DMA],
        compiler_params=_cparams(("arbitrary",)),
        name="combine",
    )(dest_t, wk, x1, ys, w_sg, w_su, w_sd, ln_g, ln_b)


def _split_w_in(w_in):
    bf = MXU_DTYPE
    o_kv = Q_RANK
    o_ki = o_kv + KV_RANK
    o_iw = o_ki + IDX_DIM
    o_rest = o_iw + N_IDX_HEADS
    w_main = jnp.concatenate([w_in[:, :o_ki], w_in[:, o_rest:]], axis=1).astype(bf)
    w_small = jnp.pad(w_in[:, o_ki:o_rest], ((0, 0), (0, LANES - IDX_DIM - N_IDX_HEADS))).astype(bf)
    return w_main, w_small


def _stages(x, mem, w_in, q_norm_g, kv_norm_g, w_uq, w_uk, w_uv, w_qidx, rel_bias, conv_w, w_mem_k, w_mem_v, w_out, ln1_g, ln1_b, w_router, router_bias, w_e_gate, w_e_up, w_e_down, w_s_gate, w_s_up, w_s_down, ln2_g, ln2_b, upto=None):
    B, S, D = x.shape
    T = B * S
    bf = MXU_DTYPE
    l = 0
    res = {}
    x2 = x.reshape(T, D)
    w_main, w_small = _split_w_in(w_in[l])
    cq, ckv, ckvt, kidx, iwt, yb, yc = _proj(
        x2, mem, w_main, w_small, q_norm_g[l].reshape(1, -1), kv_norm_g[l].reshape(1, -1), conv_w[l],
        w_mem_k[l].astype(bf), w_mem_v[l].astype(bf), B, S, tm=min(512, S))
    res.update(c_q=cq, c_kv=ckv, k_idx=kidx, y_b=yb, y_c=yc,
               idx_w=jnp.swapaxes(iwt, 1, 2) / (N_IDX_HEADS ** -0.5 * IDX_DIM ** -0.5))
    if upto == "proj":
        return res
    bias_t = _bias_tiles(rel_bias)
    ya = _dsa(cq, iwt, kidx, ckv, ckvt,
              w_qidx[l].reshape(Q_RANK, -1).astype(bf), w_uq[l].reshape(Q_RANK, -1).astype(bf),
              jnp.transpose(w_uk[l], (1, 0, 2)).astype(bf), jnp.transpose(w_uv[l], (1, 2, 0)).astype(bf),
              bias_t, B, S)
    res.update(y_a=ya)
    if upto == "dsa":
        return res

    x1, x1p, sel_t, w_t, pos_t, cnt = _mix_router(
        x2, ya, yb, yc, w_out[l].astype(bf), ln1_g[l].reshape(1, -1), ln1_b[l].reshape(1, -1),
        w_router[l].T, router_bias[l].reshape(-1, 1), tm=min(512, T))
    res.update(x1=x1)

    counts = cnt[:, 0].astype(jnp.int32)
    padded = (counts + ROW_BLOCK - 1) // ROW_BLOCK * ROW_BLOCK
    pad_end = jnp.cumsum(padded)
    pad_start = pad_end - padded
    n_blocks = -(-(T * TOP_K) // ROW_BLOCK) + N_EXPERTS
    n_rows = n_blocks * ROW_BLOCK
    block_start = jnp.arange(n_blocks, dtype=jnp.int32) * ROW_BLOCK
    block_e = jnp.minimum(jnp.sum((pad_end[None, :] <= block_start[:, None]).astype(jnp.int32), axis=1),
                          N_EXPERTS - 1)
    n_used = (pad_end[-1:] // ROW_BLOCK).astype(jnp.int32)

    dest_t, wk_t = _compact(sel_t, w_t, pos_t, pad_start.astype(jnp.float32).reshape(-1, 1), tm=min(512, T))
    block_valid = jnp.clip((pad_start + counts)[block_e] - block_start, 0, ROW_BLOCK).astype(jnp.int32)
    bt = SC_SCATTER_ROWS
    idx3 = jnp.transpose(dest_t.reshape(TOP_K, T // bt, bt), (1, 0, 2))
    xs = _sc_scatter_rows(x1p, idx3, n_rows)
    ys = _experts(xs, block_e, block_valid, n_used, w_e_gate[l], w_e_up[l], w_e_down[l])
    gathered = _sc_gather_rows(ys, dest_t.reshape(-1)).reshape(TOP_K, T, -1)
    out = _combine2(wk_t, x1, gathered, w_s_gate[l].astype(bf), w_s_up[l].astype(bf), w_s_down[l].astype(bf),
                    ln2_g[l].reshape(1, -1), ln2_b[l].reshape(1, -1), tc=min(256, T))
    res.update(out=out.reshape(B, S, D))
    return res


def kernel(x, mem, w_in, q_norm_g, kv_norm_g, w_uq, w_uk, w_uv, w_qidx, rel_bias, conv_w, w_mem_k, w_mem_v, w_out, ln1_g, ln1_b, w_router, router_bias, w_e_gate, w_e_up, w_e_down, w_s_gate, w_s_up, w_s_down, ln2_g, ln2_b):
    return _stages(x, mem, w_in, q_norm_g, kv_norm_g, w_uq, w_uk, w_uv, w_qidx, rel_bias, conv_w, w_mem_k, w_mem_v, w_out, ln1_g, ln1_b, w_router, router_bias, w_e_gate, w_e_up, w_e_down, w_s_gate, w_s_up, w_s_down, ln2_g, ln2_b)["out"]
```

```python
import functools
import math

import jax
import jax.numpy as jnp
from jax import lax
from jax.experimental import pallas as pl
from jax.experimental.pallas import tpu as pltpu
from jax.experimental.pallas import tpu_sc as plsc

N_HEADS_A = 8
HEAD_DIM = 64
Q_RANK = 256
KV_RANK = 128
N_IDX_HEADS = 8
IDX_DIM = 64
TOPK_MAX = 256
REL_BUCKETS = 32
REL_MAX_DIST = 128
CONV_CH = 256
CONV_WIDTH = 3
N_MEM_HEADS = 4
MIX_A = N_HEADS_A * HEAD_DIM
MIX_C = N_MEM_HEADS * HEAD_DIM
N_EXPERTS = 64
N_GROUPS = 8
GROUP_SIZE = N_EXPERTS // N_GROUPS
TOPK_GROUPS = 4
TOP_K = 8
D_EXPERT = 256
ROUTED_SCALE = 2.5
MOE_BLOCK = 256
DEPTH = 1
ALPHA = (2.0 * DEPTH) ** 0.25
LN_EPS = 1e-5
RMS_EPS = 1e-6

LANES = 128
SUBLANES = 8
QB = 128
F32_LOWEST = -3.4028234663852886e38
VMEM_LIMIT = 56 * 1024 * 1024
MXU_DTYPE = jnp.bfloat16
ROW_BLOCK = 512

_NT = (((1,), (1,)), ((), ()))


def _dot(a, b):
    return jnp.dot(a, b, preferred_element_type=jnp.float32)


def _dot_nt(a, b):
    return lax.dot_general(a, b, _NT, preferred_element_type=jnp.float32)


def _cparams(sem):
    return pltpu.CompilerParams(dimension_semantics=sem, vmem_limit_bytes=VMEM_LIMIT)


def _bias_kernel(rb_ref, o_ref):
    s = lax.broadcasted_iota(jnp.int32, (QB, QB), 0)
    t = lax.broadcasted_iota(jnp.int32, (QB, QB), 1)
    max_exact = REL_BUCKETS // 2
    for tile in range(3):
        n = jnp.maximum(t - s + (2 - tile) * QB, 0)
        nf = jnp.maximum(n.astype(jnp.float32), 1.0)
        large = max_exact + (jnp.log(nf / max_exact) / math.log(REL_MAX_DIST / max_exact)
                             * (REL_BUCKETS - max_exact)).astype(jnp.int32)
        large = jnp.minimum(large, REL_BUCKETS - 1)
        bucket = jnp.where(n < max_exact, n, large)
        for h in range(N_HEADS_A):
            acc = jnp.zeros((QB, QB), jnp.float32)
            for b in range(REL_BUCKETS):
                acc = jnp.where(bucket == b, rb_ref[b, h], acc)
            o_ref[tile, h] = acc


def _bias_tiles(rel_bias):
    return pl.pallas_call(
        _bias_kernel,
        in_specs=[pl.BlockSpec(memory_space=pltpu.SMEM)],
        out_specs=pl.BlockSpec(memory_space=pltpu.VMEM),
        out_shape=jax.ShapeDtypeStruct((3, N_HEADS_A, QB, QB), jnp.float32),
        name="bias_tiles",
    )(rel_bias)


_MAIN_COLS = Q_RANK + KV_RANK + 3 * CONV_CH + MIX_C


def _proj_kernel(x_ref, mem_ref, wm_ref, ws_ref, qg_ref, kvg_ref, cw_ref, wmk_ref, wmv_ref,
                 cq_ref, ckv_ref, ckvt_ref, kidx_ref, iwt_ref, yb_ref, yc_ref,
                 carry_ref, mk_ref, mv_ref, *, tm):
    si = pl.program_id(1)

    @pl.when(si == 0)
    def _():
        carry_ref[...] = jnp.zeros_like(carry_ref)
        mb = mem_ref[0].astype(MXU_DTYPE)
        mk_ref[...] = _dot(mb, wmk_ref[...]).astype(MXU_DTYPE)
        mv_ref[...] = _dot(mb, wmv_ref[...]).astype(MXU_DTYPE)

    xb = x_ref[...].astype(MXU_DTYPE)
    p = _dot(xb, wm_ref[...])
    small = _dot(xb, ws_ref[...])

    o = 0
    cq = p[:, o:o + Q_RANK]; o += Q_RANK
    ckv = p[:, o:o + KV_RANK]; o += KV_RANK
    g_b = p[:, o:o + CONV_CH]; o += CONV_CH
    g_c = p[:, o:o + CONV_CH]; o += CONV_CH
    h_c = p[:, o:o + CONV_CH]; o += CONV_CH
    q_mem = p[:, o:o + MIX_C]

    cq = cq * lax.rsqrt(jnp.mean(cq * cq, axis=-1, keepdims=True) + RMS_EPS) * qg_ref[...]
    ckv = ckv * lax.rsqrt(jnp.mean(ckv * ckv, axis=-1, keepdims=True) + RMS_EPS) * kvg_ref[...]
    cq_ref[...] = cq.astype(MXU_DTYPE)
    ckv_b = ckv.astype(MXU_DTYPE)
    ckv_ref[...] = ckv_b
    ckvt_ref[0] = ckv.T.astype(MXU_DTYPE)

    kidx_ref[...] = small[:, :IDX_DIM].astype(MXU_DTYPE)
    small_t = small.T
    iwt_ref[0] = small_t[IDX_DIM:IDX_DIM + N_IDX_HEADS, :] * (N_IDX_HEADS ** -0.5 * IDX_DIM ** -0.5)

    u = g_c * h_c
    rows = lax.broadcasted_iota(jnp.int32, (tm, 1), 0)
    c6 = carry_ref[SUBLANES - 2:SUBLANES - 1, :]
    c7 = carry_ref[SUBLANES - 1:SUBLANES, :]
    u1 = jnp.where(rows == 0, c7, pltpu.roll(u, 1, 0))
    u2 = jnp.where(rows == 0, c6, jnp.where(rows == 1, c7, pltpu.roll(u, 2, 0)))
    y = cw_ref[0:1, :] * u2
    y = y + cw_ref[1:2, :] * u1
    y = y + cw_ref[2:3, :] * u
    yb_ref[...] = (g_b * y).astype(MXU_DTYPE)
    carry_ref[...] = u[tm - SUBLANES:, :]

    qm = q_mem.astype(MXU_DTYPE)
    outs = []
    for h in range(N_MEM_HEADS):
        sl = slice(h * HEAD_DIM, (h + 1) * HEAD_DIM)
        lg = _dot_nt(qm[:, sl], mk_ref[:, sl]) * (HEAD_DIM ** -0.5)
        lg = lg - jnp.max(lg, axis=-1, keepdims=True)
        e = jnp.exp(lg)
        pr = e / jnp.sum(e, axis=-1, keepdims=True)
        outs.append(_dot(pr.astype(MXU_DTYPE), mv_ref[:, sl]))
    yc_ref[...] = jnp.concatenate(outs, axis=-1).astype(MXU_DTYPE)


def _proj(x2, mem, w_main, w_small, q_g, kv_g, conv_w, w_mk, w_mv, B, S, tm):
    T, D = x2.shape
    n_mem = mem.shape[1]
    ns = S // tm
    row = lambda b, s: (b * ns + s, 0)
    const2 = lambda b, s: (0, 0)
    bf = MXU_DTYPE
    return pl.pallas_call(
        functools.partial(_proj_kernel, tm=tm),
        grid=(B, ns),
        in_specs=[
            pl.BlockSpec((tm, D), row),
            pl.BlockSpec((1, n_mem, D), lambda b, s: (b, 0, 0)),
            pl.BlockSpec(w_main.shape, const2),
            pl.BlockSpec(w_small.shape, const2),
            pl.BlockSpec(q_g.shape, const2),
            pl.BlockSpec(kv_g.shape, const2),
            pl.BlockSpec(conv_w.shape, const2),
            pl.BlockSpec(w_mk.shape, const2),
            pl.BlockSpec(w_mv.shape, const2),
        ],
        out_specs=[
            pl.BlockSpec((tm, Q_RANK), row),
            pl.BlockSpec((tm, KV_RANK), row),
            pl.BlockSpec((1, KV_RANK, tm), lambda b, s: (b, 0, s)),
            pl.BlockSpec((tm, IDX_DIM), row),
            pl.BlockSpec((1, N_IDX_HEADS, tm), lambda b, s: (b, 0, s)),
            pl.BlockSpec((tm, CONV_CH), row),
            pl.BlockSpec((tm, MIX_C), row),
        ],
        out_shape=[
            jax.ShapeDtypeStruct((T, Q_RANK), bf),
            jax.ShapeDtypeStruct((T, KV_RANK), bf),
            jax.ShapeDtypeStruct((B, KV_RANK, S), bf),
            jax.ShapeDtypeStruct((T, IDX_DIM), bf),
            jax.ShapeDtypeStruct((B, N_IDX_HEADS, S), jnp.float32),
            jax.ShapeDtypeStruct((T, CONV_CH), bf),
            jax.ShapeDtypeStruct((T, MIX_C), bf),
        ],
        scratch_shapes=[
            pltpu.VMEM((SUBLANES, CONV_CH), jnp.float32),
            pltpu.VMEM((n_mem, MIX_C), bf),
            pltpu.VMEM((n_mem, MIX_C), bf),
        ],
        compiler_params=_cparams(("arbitrary", "arbitrary")),
        name="proj",
    )(x2, mem, w_main, w_small, q_g, kv_g, conv_w, w_mk, w_mv)


def _key_to_f32(key):
    bits = jnp.where(key < 0, key ^ jnp.int32(0x7FFFFFFF), key)
    return pltpu.bitcast(bits, jnp.float32)


def _colsum8(v):
    return jnp.sum(v.reshape(QB // SUBLANES, SUBLANES, QB), axis=0)


def _colmax8(v):
    return jnp.max(v.reshape(QB // SUBLANES, SUBLANES, QB), axis=0)


UNROLL_WIDTHS = (4, 2, 1)


def _dsa_kernel(cq_ref, iwt_ref, kidx_ref, ckv_ref, ckvt_ref, wqi_ref, wuq_ref, wuk_ref, wuvt_ref,
                bias_ref, o_ref, qidx_ref, qlat_ref, score_ref, mask_ref, logit_ref, acc_ref,
                *, k_sel, idx_bits):
    i = pl.program_id(1)
    f32 = jnp.float32
    bf = MXU_DTYPE
    n_blocks = i + 1
    s_loc = lax.broadcasted_iota(jnp.int32, (QB, QB), 0)
    t_glob = i * QB + lax.broadcasted_iota(jnp.int32, (QB, QB), 1)

    def blk(jb):
        return pl.multiple_of(jb * QB, QB)

    def block_loop(fn, init):
        c, start = init, 0
        for width in UNROLL_WIDTHS:
            n = (n_blocks - start) // width
            c = lax.fori_loop(0, n, lambda it, c, w=width, s=start: fn(s + it * w, w, c), c)
            start = start + n * width
        return c

    cq = cq_ref[...]
    q_all = _dot(cq, wuq_ref[...]).astype(bf)
    for h in range(N_HEADS_A):
        qidx_ref[h * QB:(h + 1) * QB, :] = _dot(cq, wqi_ref[:, h * IDX_DIM:(h + 1) * IDX_DIM]).astype(bf)
        qlat_ref[h * QB:(h + 1) * QB, :] = (
            _dot_nt(q_all[:, h * HEAD_DIM:(h + 1) * HEAD_DIM], wuk_ref[h]) * (HEAD_DIM ** -0.5)).astype(bf)
    iw = iwt_ref[0]

    def score_body(jb0, nb, c):
        d_blk = _dot_nt(kidx_ref[pl.ds(blk(jb0), nb * QB), :], qidx_ref[...])
        for sb in range(nb):
            off = blk(jb0 + sb)
            d_all = d_blk[sb * QB:(sb + 1) * QB, :]
            acc = jnp.maximum(d_all[:, 0:QB], 0.0) * iw[0:1, :]
            for h in range(1, N_IDX_HEADS):
                acc = acc + jnp.maximum(d_all[:, h * QB:(h + 1) * QB], 0.0) * iw[h:h + 1, :]
            score_ref[pl.ds(off, QB), :] = jnp.where(s_loc + off <= t_glob, acc + 0.0, F32_LOWEST)
        return c

    block_loop(score_body, 0)

    def count_where(pred):
        def body(jb0, nb, acc):
            for sb in range(nb):
                off = blk(jb0 + sb)
                acc = acc + _colsum8(jnp.where(pred(score_ref[pl.ds(off, QB), :], off), 1.0, 0.0))
            return acc
        acc = block_loop(body, jnp.zeros((SUBLANES, QB), f32))
        return jnp.sum(acc, axis=0, keepdims=True)

    kf = float(k_sel)

    def search():
        c0 = count_where(lambda sc, off: sc >= 0.0)
        cand0 = jnp.where(c0 >= kf, jnp.int32(0), jnp.int32(-2 ** 31))

        def bit_body(it, cand):
            trial = cand + lax.shift_left(jnp.int32(1), 30 - it)
            tf = _key_to_f32(trial)
            cnt = count_where(lambda sc, off: sc >= tf)
            return jnp.where(cnt >= kf, trial, cand)

        cand = lax.fori_loop(0, 31, bit_body, cand0)
        thr = _key_to_f32(cand)
        n_gt = count_where(lambda sc, off: sc > thr)
        n_eq = count_where(lambda sc, off: sc == thr)
        need = kf - n_gt

        def tie_search():
            def tbody(it, xcut):
                trial = xcut + lax.shift_left(jnp.int32(1), idx_bits - 1 - it)
                cnt = count_where(lambda sc, off: (sc == thr) & (s_loc + off < trial))
                return jnp.where(cnt < need, trial, xcut)
            return lax.fori_loop(0, idx_bits, tbody, jnp.zeros((1, QB), jnp.int32))

        any_extra = jnp.max(n_eq - need) > 0.0
        xcut = lax.cond(any_extra, tie_search, lambda: jnp.full((1, QB), 2 ** idx_bits - 1, jnp.int32))
        return thr, xcut

    def no_search():
        return jnp.full((1, QB), F32_LOWEST, f32), jnp.full((1, QB), 2 ** idx_bits - 1, jnp.int32)

    thr, xcut = lax.cond((i + 1) * QB > k_sel, search, no_search)

    def mask_body(jb0, nb, c):
        for sb in range(nb):
            off = blk(jb0 + sb)
            sc = score_ref[pl.ds(off, QB), :]
            s_glob = s_loc + off
            keep = ((sc > thr) | ((sc == thr) & (s_glob <= xcut))) & (s_glob <= t_glob)
            mask_ref[pl.ds(off, QB), :] = jnp.where(keep, 0.0, -jnp.inf)
        return c

    block_loop(mask_body, 0)

    def p1_body(jb0, nb, m8):
        m8 = list(m8)
        lg_blk = _dot_nt(ckv_ref[pl.ds(blk(jb0), nb * QB), :], qlat_ref[...])
        for sb in range(nb):
            off = blk(jb0 + sb)
            lg = lg_blk[sb * QB:(sb + 1) * QB, :]
            msk = mask_ref[pl.ds(off, QB), :]
            bsel = jnp.clip(jb0 + sb - i + 2, 0, 2)
            for h in range(N_HEADS_A):
                lgh = lg[:, h * QB:(h + 1) * QB] + bias_ref[bsel, h] + msk
                logit_ref[pl.ds(off, QB), h * QB:(h + 1) * QB] = lgh
                m8[h] = jnp.maximum(m8[h], _colmax8(lgh))
        return tuple(m8)

    m8 = block_loop(p1_body, tuple(jnp.full((SUBLANES, QB), -jnp.inf, f32) for _ in range(N_HEADS_A)))
    m_row = [jnp.max(m, axis=0, keepdims=True) for m in m8]

    acc_ref[...] = jnp.zeros_like(acc_ref)

    def p2_body(jb0, nb, l8):
        l8 = list(l8)
        off = blk(jb0)
        rows = nb * QB
        ps = []
        for h in range(N_HEADS_A):
            p = jnp.exp(logit_ref[pl.ds(off, rows), h * QB:(h + 1) * QB] - m_row[h])
            l8[h] = l8[h] + jnp.sum(p.reshape(rows // SUBLANES, SUBLANES, QB), axis=0)
            ps.append(p.astype(bf))
        acc_ref[...] += _dot(ckvt_ref[0, :, pl.ds(off, rows)], jnp.concatenate(ps, axis=1))
        return tuple(l8)

    l8 = block_loop(p2_body, tuple(jnp.zeros((SUBLANES, QB), f32) for _ in range(N_HEADS_A)))

    outs = []
    for h in range(N_HEADS_A):
        l_row = jnp.sum(l8[h], axis=0, keepdims=True)
        o_lat_t = (acc_ref[:, h * QB:(h + 1) * QB] / l_row).astype(bf)
        outs.append(_dot(wuvt_ref[h], o_lat_t))
    o_ref[...] = jnp.concatenate(outs, axis=0).T.astype(o_ref.dtype)


def _dsa(cq, iwt, kidx, ckv, ckvt, w_qidx, w_uq, w_uk_h, w_uvt_h, bias_tiles, B, S):
    T = cq.shape[0]
    assert S % QB == 0 and QB >= REL_MAX_DIST
    nq = S // QB
    k_sel = min(TOPK_MAX, S // 4)
    idx_bits = max(1, (S - 1).bit_length())
    c2 = lambda b, i: (0, 0)
    c3 = lambda b, i: (0, 0, 0)
    return pl.pallas_call(
        functools.partial(_dsa_kernel, k_sel=k_sel, idx_bits=idx_bits),
        grid=(B, nq),
        in_specs=[
            pl.BlockSpec((QB, Q_RANK), lambda b, i: (b * nq + i, 0)),
            pl.BlockSpec((1, N_IDX_HEADS, QB), lambda b, i: (b, 0, i)),
            pl.BlockSpec((S, IDX_DIM), lambda b, i: (b, 0)),
            pl.BlockSpec((S, KV_RANK), lambda b, i: (b, 0)),
            pl.BlockSpec((1, KV_RANK, S), lambda b, i: (b, 0, 0)),
            pl.BlockSpec(w_qidx.shape, c2),
            pl.BlockSpec(w_uq.shape, c2),
            pl.BlockSpec(w_uk_h.shape, c3),
            pl.BlockSpec(w_uvt_h.shape, c3),
            pl.BlockSpec(bias_tiles.shape, lambda b, i: (0, 0, 0, 0)),
        ],
        out_specs=pl.BlockSpec((QB, MIX_A), lambda b, i: (b * nq + i, 0)),
        out_shape=jax.ShapeDtypeStruct((T, MIX_A), MXU_DTYPE),
        scratch_shapes=[
            pltpu.VMEM((N_IDX_HEADS * QB, IDX_DIM), MXU_DTYPE),
            pltpu.VMEM((N_HEADS_A * QB, KV_RANK), MXU_DTYPE),
            pltpu.VMEM((S, QB), jnp.float32),
            pltpu.VMEM((S, QB), jnp.float32),
            pltpu.VMEM((S, N_HEADS_A * QB), jnp.float32),
            pltpu.VMEM((KV_RANK, N_HEADS_A * QB), jnp.float32),
        ],
        compiler_params=_cparams(("arbitrary", "arbitrary")),
        name="dsa",
    )(cq, iwt, kidx, ckv, ckvt, w_qidx, w_uq, w_uk_h, w_uvt_h, bias_tiles)


def _layer_norm(xf, g, b):
    mu = jnp.mean(xf, axis=-1, keepdims=True)
    xc = xf - mu
    var = jnp.mean(xc * xc, axis=-1, keepdims=True)
    return xc * lax.rsqrt(var + LN_EPS) * g + b


def _rank_rows(v, n):
    ri = lax.broadcasted_iota(jnp.int32, v.shape, 0)
    rank = jnp.zeros(v.shape, jnp.float32)
    for r2 in range(n):
        row = v[r2:r2 + 1, :]
        beats = (row > v) | ((row == v) & (ri > r2))
        rank = rank + jnp.where(beats, 1.0, 0.0)
    return rank


def _pack_factor():
    return 4 // jnp.dtype(MXU_DTYPE).itemsize


def _pack_rows(x):
    if _pack_factor() == 1:
        return pltpu.bitcast(x, jnp.int32)
    half = x.shape[1] // 2
    b = pltpu.bitcast(x.astype(MXU_DTYPE).astype(jnp.float32), jnp.int32)
    return b[:, half:] | lax.shift_right_logical(b[:, :half], jnp.int32(16))


_HIGH_HALF = -(1 << 16)


def _unpack_rows_f32(p):
    if _pack_factor() == 1:
        return [pltpu.bitcast(p, jnp.float32)]
    lo = pltpu.bitcast(lax.shift_left(p, jnp.int32(16)), jnp.float32)
    hi = pltpu.bitcast(p & jnp.int32(_HIGH_HALF), jnp.float32)
    return [lo, hi]


def _unpack_rows(p):
    return [v.astype(MXU_DTYPE) for v in _unpack_rows_f32(p)]


def _mix_router_kernel(x_ref, ya_ref, yb_ref, yc_ref, wo_ref, g_ref, b_ref, wrt_ref, rb_ref, exp_ref,
                       x1_ref, x1p_ref, sel_ref, w_ref, pos_ref, cnt_ref, base_ref, *, tm):
    step = pl.program_id(0)
    f32 = jnp.float32

    @pl.when(step == 0)
    def _():
        base_ref[...] = jnp.zeros_like(base_ref)

    mix = _dot(ya_ref[...], wo_ref[0:MIX_A, :])
    mix = mix + _dot(yb_ref[...], wo_ref[MIX_A:MIX_A + CONV_CH, :])
    mix = mix + _dot(yc_ref[...], wo_ref[MIX_A + CONV_CH:, :])
    x1 = _layer_norm(ALPHA * x_ref[...] + mix, g_ref[...], b_ref[...])
    x1_ref[...] = x1
    x1p_ref[...] = _pack_rows(x1)

    lg = lax.dot_general(wrt_ref[...], x1, _NT, precision=lax.Precision.HIGHEST, preferred_element_type=f32)
    s = 1.0 / (1.0 + jnp.exp(-lg))
    sc = s + rb_ref[...]

    g3 = sc.reshape(N_GROUPS, GROUP_SIZE, tm)
    m1 = jnp.max(g3, axis=1, keepdims=True)
    is_m1 = g3 == m1
    n_m1 = jnp.sum(jnp.where(is_m1, 1.0, 0.0), axis=1, keepdims=True)
    m2 = jnp.max(jnp.where(is_m1, -jnp.inf, g3), axis=1, keepdims=True)
    gscore = (m1 + jnp.where(n_m1 > 1.0, m1, m2)).reshape(N_GROUPS, tm)
    gsel = jnp.where(_rank_rows(gscore, N_GROUPS) < float(TOPK_GROUPS), 1.0, 0.0)
    emask = _dot(exp_ref[...], gsel.astype(MXU_DTYPE)) > 0.5
    masked = jnp.where(emask, sc, -jnp.inf)
    sel = (_rank_rows(masked, N_EXPERTS) < float(TOP_K)) & emask
    self_ = jnp.where(sel, 1.0, 0.0)
    top_s = jnp.where(sel, s, 0.0)
    w = top_s / jnp.sum(top_s, axis=0, keepdims=True) * ROUTED_SCALE

    t_r = lax.broadcasted_iota(jnp.int32, (tm, tm), 0)
    t_c = lax.broadcasted_iota(jnp.int32, (tm, tm), 1)
    upper = jnp.where(t_r < t_c, 1.0, 0.0).astype(MXU_DTYPE)
    pref = _dot(self_.astype(MXU_DTYPE), upper)
    base = base_ref[...]
    sel_ref[...] = self_
    w_ref[...] = w
    pos_ref[...] = base + pref
    base = base + jnp.sum(self_, axis=1, keepdims=True)
    base_ref[...] = base
    cnt_ref[...] = jnp.broadcast_to(base, cnt_ref.shape)


def _mix_router(x2, ya, yb, yc, w_out, ln_g, ln_b, w_router_t, router_bias, tm):
    T, D = x2.shape
    E = N_EXPERTS
    expand = (jnp.arange(E)[:, None] // GROUP_SIZE == jnp.arange(N_GROUPS)[None, :]).astype(MXU_DTYPE)
    row = lambda i: (i, 0)
    col = lambda i: (0, i)
    c2 = lambda i: (0, 0)
    f32 = jnp.float32
    return pl.pallas_call(
        functools.partial(_mix_router_kernel, tm=tm),
        grid=(T // tm,),
        in_specs=[
            pl.BlockSpec((tm, D), row),
            pl.BlockSpec((tm, MIX_A), row),
            pl.BlockSpec((tm, CONV_CH), row),
            pl.BlockSpec((tm, MIX_C), row),
            pl.BlockSpec(w_out.shape, c2),
            pl.BlockSpec((1, D), c2),
            pl.BlockSpec((1, D), c2),
            pl.BlockSpec((E, D), c2),
            pl.BlockSpec((E, 1), c2),
            pl.BlockSpec((E, N_GROUPS), c2),
        ],
        out_specs=[
            pl.BlockSpec((tm, D), row),
            pl.BlockSpec((tm, D // _pack_factor()), row),
            pl.BlockSpec((E, tm), col),
            pl.BlockSpec((E, tm), col),
            pl.BlockSpec((E, tm), col),
            pl.BlockSpec((E, LANES), c2),
        ],
        out_shape=[
            jax.ShapeDtypeStruct((T, D), f32),
            jax.ShapeDtypeStruct((T, D // _pack_factor()), jnp.int32),
            jax.ShapeDtypeStruct((E, T), f32),
            jax.ShapeDtypeStruct((E, T), f32),
            jax.ShapeDtypeStruct((E, T), f32),
            jax.ShapeDtypeStruct((E, LANES), f32),
        ],
        scratch_shapes=[pltpu.VMEM((E, 1), f32)],
        compiler_params=_cparams(("arbitrary",)),
        name="mix_router",
    )(x2, ya, yb, yc, w_out, ln_g, ln_b, w_router_t, router_bias, expand)


def _compact_kernel(sel_ref, w_ref, pos_ref, pstart_ref, low_ref, dest_ref, wk_ref):
    sel = sel_ref[...]
    on = sel > 0.5
    rank = _dot(low_ref[...], sel.astype(MXU_DTYPE))
    row = pstart_ref[...] + pos_ref[...]
    w = w_ref[...]
    dests, ws = [], []
    for k in range(TOP_K):
        m = on & (rank == float(k))
        dests.append(jnp.sum(jnp.where(m, row, 0.0), axis=0, keepdims=True))
        ws.append(jnp.sum(jnp.where(m, w, 0.0), axis=0, keepdims=True))
    dest_ref[...] = jnp.concatenate(dests, axis=0).astype(jnp.int32)
    wk_ref[...] = jnp.concatenate(ws, axis=0)


def _compact(sel_t, w_t, pos_t, pad_start, tm):
    E, T = sel_t.shape
    lower = (jnp.arange(E)[None, :] < jnp.arange(E)[:, None]).astype(MXU_DTYPE)
    col = lambda i: (0, i)
    c2 = lambda i: (0, 0)
    return pl.pallas_call(
        _compact_kernel,
        grid=(T // tm,),
        in_specs=[pl.BlockSpec((E, tm), col), pl.BlockSpec((E, tm), col), pl.BlockSpec((E, tm), col),
                  pl.BlockSpec((E, 1), c2), pl.BlockSpec((E, E), c2)],
        out_specs=[pl.BlockSpec((TOP_K, tm), col), pl.BlockSpec((TOP_K, tm), col)],
        out_shape=[jax.ShapeDtypeStruct((TOP_K, T), jnp.int32), jax.ShapeDtypeStruct((TOP_K, T), jnp.float32)],
        compiler_params=_cparams(("arbitrary",)),
        name="route_compact",
    )(sel_t, w_t, pos_t, pad_start, lower)


def _row_copy(src, s, dst, d, sem):
    return pltpu.make_async_copy(src.at[pl.ds(s, 1)], dst.at[pl.ds(d, 1)], sem)


def _dispatch_kernel(flo_ref, fhi_ref, dest_ref, x_ref, xs_hbm, zero_ref, sem, zsem, *, td):
    step = pl.program_id(0)

    @pl.when(step == 0)
    def _():
        zero_ref[...] = jnp.zeros_like(zero_ref)

        def per_expert(fn):
            def ebody(e, c):
                lax.fori_loop(flo_ref[e], fhi_ref[e], lambda r, c2: (fn(r), c2)[1], 0)
                return c
            lax.fori_loop(0, N_EXPERTS, ebody, 0)

        per_expert(lambda r: _row_copy(zero_ref, 0, xs_hbm, r, zsem).start())
        per_expert(lambda r: _row_copy(zero_ref, 0, xs_hbm, r, zsem).wait())

    def issue(r, c):
        for k in range(TOP_K):
            _row_copy(x_ref, r, xs_hbm, dest_ref[k, r], sem).start()
        return c

    def drain(r, c):
        for k in range(TOP_K):
            _row_copy(x_ref, r, xs_hbm, dest_ref[k, r], sem).wait()
        return c

    lax.fori_loop(0, td, issue, 0)
    lax.fori_loop(0, td, drain, 0)


def _dispatch(dest_t, x1p, fill_lo, fill_hi, n_rows, td):
    T, W = x1p.shape
    return pl.pallas_call(
        functools.partial(_dispatch_kernel, td=td),
        grid_spec=pltpu.PrefetchScalarGridSpec(
            num_scalar_prefetch=2,
            grid=(T // td,),
            in_specs=[
                pl.BlockSpec((TOP_K, td), lambda i, lo, hi: (0, i), memory_space=pltpu.SMEM),
                pl.BlockSpec((td, W), lambda i, lo, hi: (i, 0)),
            ],
            out_specs=pl.BlockSpec(memory_space=pl.ANY),
            scratch_shapes=[pltpu.VMEM((SUBLANES, W), x1p.dtype),
                            pltpu.SemaphoreType.DMA, pltpu.SemaphoreType.DMA],
        ),
        out_shape=jax.ShapeDtypeStruct((n_rows, W), x1p.dtype),
        compiler_params=_cparams(("arbitrary",)),
        name="dispatch",
    )(fill_lo, fill_hi, dest_t, x1p)


def _silu(g):
    return g / (1.0 + jnp.exp(-g))


def _expert_kernel(be_ref, nv_ref, nu_ref, xs_ref, wg_ref, wu_ref, wd_ref, ys_ref, wgb_ref, wub_ref, wdb_ref):
    i = pl.program_id(0)

    @pl.when((i == 0) | (be_ref[i] != be_ref[jnp.maximum(i - 1, 0)]))
    def _():
        wgb_ref[...] = wg_ref[0].astype(MXU_DTYPE)
        wub_ref[...] = wu_ref[0].astype(MXU_DTYPE)
        wdb_ref[...] = wd_ref[0].astype(MXU_DTYPE)

    @pl.when(i < nu_ref[0])
    def _():
        live = lax.broadcasted_iota(jnp.int32, (ROW_BLOCK, 1), 0) < nv_ref[i]
        parts = [jnp.where(live, v, jnp.zeros_like(v)) for v in _unpack_rows(xs_ref[...])]
        dk = wgb_ref.shape[0] // len(parts)

        def proj(w_ref):
            acc = _dot(parts[0], w_ref[0:dk, :])
            for n in range(1, len(parts)):
                acc = acc + _dot(parts[n], w_ref[n * dk:(n + 1) * dk, :])
            return acc

        a = (_silu(proj(wgb_ref)) * proj(wub_ref)).astype(MXU_DTYPE)
        ys_ref[...] = _pack_rows(_dot(a, wdb_ref[...]))


def _experts(xs, block_e, block_valid, n_used, w_gate, w_up, w_down):
    n_rows, W = xs.shape
    D = w_gate.shape[1]
    n_blocks = n_rows // ROW_BLOCK
    blk = lambda i, be, nv, nu: (jnp.minimum(i, nu[0] - 1), 0)
    wsel = lambda i, be, nv, nu: (be[i], 0, 0)
    return pl.pallas_call(
        _expert_kernel,
        grid_spec=pltpu.PrefetchScalarGridSpec(
            num_scalar_prefetch=3,
            grid=(n_blocks,),
            in_specs=[
                pl.BlockSpec((ROW_BLOCK, W), blk),
                pl.BlockSpec((1, D, D_EXPERT), wsel),
                pl.BlockSpec((1, D, D_EXPERT), wsel),
                pl.BlockSpec((1, D_EXPERT, D), wsel),
            ],
            out_specs=pl.BlockSpec((ROW_BLOCK, W), blk),
            scratch_shapes=[pltpu.VMEM((D, D_EXPERT), MXU_DTYPE), pltpu.VMEM((D, D_EXPERT), MXU_DTYPE),
                            pltpu.VMEM((D_EXPERT, D), MXU_DTYPE)],
        ),
        out_shape=jax.ShapeDtypeStruct((n_rows, W), xs.dtype),
        compiler_params=_cparams(("arbitrary",)),
        name="experts",
    )(block_e, block_valid, n_used, xs, w_gate, w_up, w_down)


SC_CORES = 2
SC_SUBCORES = 16
SC_GATHER_ROWS = 64
COMBINE_CHUNKS = 4


def _sc_gather_rows(table, idx):
    n = idx.shape[0]
    w = table.shape[1]
    n_workers = SC_CORES * SC_SUBCORES
    per_worker = n // n_workers
    assert n % n_workers == 0 and per_worker % SC_GATHER_ROWS == 0
    mesh = plsc.VectorSubcoreMesh(core_axis_name="c", subcore_axis_name="s")

    @functools.partial(
        pl.kernel, mesh=mesh,
        out_type=jax.ShapeDtypeStruct((n, w), table.dtype),
        scratch_types=[
            pltpu.VMEM((2, SC_GATHER_ROWS), jnp.int32),
            pltpu.VMEM((2, SC_GATHER_ROWS, w), table.dtype),
            pltpu.SemaphoreType.DMA((2,)),
        ],
        name="sc_gather_rows",
    )
    def gather(table_hbm, idx_hbm, out_hbm, idx_v, rows_v, sem):
        wid = lax.axis_index("s") * SC_CORES + lax.axis_index("c")
        base = wid * per_worker
        n_steps = per_worker // SC_GATHER_ROWS

        def gather_copy(slot):
            return pltpu.make_async_copy(table_hbm.at[idx_v.at[slot]], rows_v.at[slot], sem.at[slot])

        def start(step, slot):
            pltpu.sync_copy(idx_hbm.at[pl.ds(base + step * SC_GATHER_ROWS, SC_GATHER_ROWS)], idx_v.at[slot])
            gather_copy(slot).start()

        start(0, 0)

        @pl.loop(0, n_steps, step=2)
        def _(g):
            for slot in range(2):
                step = g + slot

                @pl.when(step + 1 < n_steps)
                def _():
                    start(step + 1, 1 - slot)

                gather_copy(slot).wait()
                pltpu.sync_copy(rows_v.at[slot], out_hbm.at[pl.ds(base + step * SC_GATHER_ROWS, SC_GATHER_ROWS)])

    return gather(table, idx)


SC_SCATTER_ROWS = 64


def _sc_scatter_rows(rows, idx3, n_out):
    n_src, w = rows.shape
    n_chunks, n_dst, batch = idx3.shape
    n_workers = SC_CORES * SC_SUBCORES
    assert batch == SC_SCATTER_ROWS and n_chunks * batch == n_src and n_chunks % (2 * n_workers) == 0
    per_worker = n_chunks // n_workers
    mesh = plsc.VectorSubcoreMesh(core_axis_name="c", subcore_axis_name="s")

    @functools.partial(
        pl.kernel, mesh=mesh,
        out_type=jax.ShapeDtypeStruct((n_out, w), rows.dtype),
        scratch_types=[
            pltpu.VMEM((2, n_dst, batch), jnp.int32),
            pltpu.VMEM((2, batch, w), rows.dtype),
            pltpu.SemaphoreType.DMA((2,)),
            pltpu.SemaphoreType.DMA,
        ],
        name="sc_scatter_rows",
    )
    def scatter(rows_hbm, idx_hbm, out_hbm, idx_v, rows_v, load_sem, store_sem):
        wid = lax.axis_index("s") * SC_CORES + lax.axis_index("c")

        def load_copy(step, slot):
            c = wid * per_worker + step
            return pltpu.make_async_copy(rows_hbm.at[pl.ds(c * batch, batch)], rows_v.at[slot], load_sem.at[slot])

        def load(step, slot):
            pltpu.sync_copy(idx_hbm.at[wid * per_worker + step], idx_v.at[slot])
            load_copy(step, slot).start()

        def store_copy(slot, k):
            return pltpu.make_async_copy(rows_v.at[slot], out_hbm.at[idx_v.at[slot].at[k]], store_sem)

        load(0, 0)

        @pl.loop(0, per_worker, step=2)
        def _(g):
            for slot in range(2):
                step = g + slot

                @pl.when(step + 1 < per_worker)
                def _():
                    load(step + 1, 1 - slot)

                load_copy(step, slot).wait()
                for k in range(n_dst):
                    store_copy(slot, k).start()
                for k in range(n_dst):
                    store_copy(slot, k).wait()

    return scatter(rows, idx3)


def _shared_kernel(x1_ref, wsg_ref, wsu_ref, wsd_ref, o_ref):
    xb = x1_ref[...].astype(MXU_DTYPE)
    a = (_silu(_dot(xb, wsg_ref[...])) * _dot(xb, wsu_ref[...])).astype(MXU_DTYPE)
    o_ref[...] = _dot(a, wsd_ref[...])


def _shared_expert(x1, w_sg, w_su, w_sd, tm):
    T, D = x1.shape
    row = lambda i: (i, 0)
    c2 = lambda i: (0, 0)
    return pl.pallas_call(
        _shared_kernel,
        grid=(T // tm,),
        in_specs=[pl.BlockSpec((tm, D), row), pl.BlockSpec(w_sg.shape, c2), pl.BlockSpec(w_su.shape, c2),
                  pl.BlockSpec(w_sd.shape, c2)],
        out_specs=pl.BlockSpec((tm, D), row),
        out_shape=jax.ShapeDtypeStruct((T, D), jnp.float32),
        compiler_params=_cparams(("arbitrary",)),
        name="shared_expert",
    )(x1, w_sg, w_su, w_sd)


def _combine2_kernel(wk_ref, x1_ref, g_ref_rows, wsg_ref, wsu_ref, wsd_ref, g_ref, b_ref, o_ref):
    x1 = x1_ref[...]
    xb = x1.astype(MXU_DTYPE)
    a = (_silu(_dot(xb, wsg_ref[...])) * _dot(xb, wsu_ref[...])).astype(MXU_DTYPE)
    shared = _dot(a, wsd_ref[...])
    wk = wk_ref[...].T
    groups = [wk[:, 0:1] * v for v in _unpack_rows_f32(g_ref_rows[0])]
    for k in range(1, TOP_K):
        groups = [g + wk[:, k:k + 1] * v for g, v in zip(groups, _unpack_rows_f32(g_ref_rows[k]))]
    routed = jnp.concatenate(groups, axis=1)
    o_ref[...] = _layer_norm(ALPHA * x1 + (routed + shared), g_ref[...], b_ref[...])


def _combine2_kernel_into(wk_ref, x1_ref, g_ref_rows, wsg_ref, wsu_ref, wsd_ref, g_ref, b_ref, prev_ref, o_ref):
    del prev_ref
    _combine2_kernel(wk_ref, x1_ref, g_ref_rows, wsg_ref, wsu_ref, wsd_ref, g_ref, b_ref, o_ref)


def _combine2(wk_t, x1, gathered, w_sg, w_su, w_sd, ln_g, ln_b, tc, chunk, prev):
    T, D = x1.shape
    _, t_chunk, W = gathered.shape
    base = chunk * (t_chunk // tc)
    row = lambda i: (base + i, 0)
    c2 = lambda i: (0, 0)
    in_specs = [
        pl.BlockSpec((TOP_K, tc), lambda i: (0, base + i)),
        pl.BlockSpec((tc, D), row),
        pl.BlockSpec((TOP_K, tc, W), lambda i: (0, i, 0)),
        pl.BlockSpec(w_sg.shape, c2),
        pl.BlockSpec(w_su.shape, c2),
        pl.BlockSpec(w_sd.shape, c2),
        pl.BlockSpec((1, D), c2),
        pl.BlockSpec((1, D), c2),
    ]
    args = [wk_t, x1, gathered, w_sg, w_su, w_sd, ln_g, ln_b]
    if prev is None:
        body, aliases = _combine2_kernel, {}
    else:
        body, aliases = _combine2_kernel_into, {len(args): 0}
        in_specs.append(pl.BlockSpec(memory_space=pl.ANY))
        args.append(prev)
    return pl.pallas_call(
        body,
        grid=(t_chunk // tc,),
        in_specs=in_specs,
        out_specs=pl.BlockSpec((tc, D), row),
        out_shape=jax.ShapeDtypeStruct((T, D), jnp.float32),
        input_output_aliases=aliases,
        compiler_params=_cparams(("arbitrary",)),
        name="combine",
    )(*args)


def _combine_kernel(dest_ref, wk_ref, x1_ref, ys_hbm, wsg_ref, wsu_ref, wsd_ref, g_ref, b_ref,
                    o_ref, buf_ref, sem, *, tc):
    def issue(r, c):
        for k in range(TOP_K):
            _row_copy(ys_hbm, dest_ref[k, r], buf_ref.at[k], r, sem).start()
        return c

    def drain(r, c):
        for k in range(TOP_K):
            _row_copy(ys_hbm, dest_ref[k, r], buf_ref.at[k], r, sem).wait()
        return c

    lax.fori_loop(0, tc, issue, 0)
    x1 = x1_ref[...]
    xb = x1.astype(MXU_DTYPE)
    a = (_silu(_dot(xb, wsg_ref[...])) * _dot(xb, wsu_ref[...])).astype(MXU_DTYPE)
    shared = _dot(a, wsd_ref[...])
    lax.fori_loop(0, tc, drain, 0)
    wk = wk_ref[...]
    groups = [wk[:, 0:1] * v for v in _unpack_rows_f32(buf_ref[0])]
    for k in range(1, TOP_K):
        groups = [g + wk[:, k:k + 1] * v for g, v in zip(groups, _unpack_rows_f32(buf_ref[k]))]
    routed = jnp.concatenate(groups, axis=1)
    o_ref[...] = _layer_norm(ALPHA * x1 + (routed + shared), g_ref[...], b_ref[...])


def _combine(dest_t, wk, x1, ys, w_sg, w_su, w_sd, ln_g, ln_b, tc):
    T, D = x1.shape
    row = lambda i: (i, 0)
    c2 = lambda i: (0, 0)
    return pl.pallas_call(
        functools.partial(_combine_kernel, tc=tc),
        grid=(T // tc,),
        in_specs=[
            pl.BlockSpec((TOP_K, tc), lambda i: (0, i), memory_space=pltpu.SMEM),
            pl.BlockSpec((tc, TOP_K), row),
            pl.BlockSpec((tc, D), row),
            pl.BlockSpec(memory_space=pl.ANY),
            pl.BlockSpec(w_sg.shape, c2),
            pl.BlockSpec(w_su.shape, c2),
            pl.BlockSpec(w_sd.shape, c2),
            pl.BlockSpec((1, D), c2),
            pl.BlockSpec((1, D), c2),
        ],
        out_specs=pl.BlockSpec((tc, D), row),
        out_shape=jax.ShapeDtypeStruct((T, D), jnp.float32),
        scratch_shapes=[pltpu.VMEM((TOP_K, tc, ys.shape[1]), ys.dtype), pltpu.SemaphoreType.DMA],
        compiler_params=_cparams(("arbitrary",)),
        name="combine",
    )(dest_t, wk, x1, ys, w_sg, w_su, w_sd, ln_g, ln_b)


def _split_w_in(w_in):
    bf = MXU_DTYPE
    o_kv = Q_RANK
    o_ki = o_kv + KV_RANK
    o_iw = o_ki + IDX_DIM
    o_rest = o_iw + N_IDX_HEADS
    w_main = jnp.concatenate([w_in[:, :o_ki], w_in[:, o_rest:]], axis=1).astype(bf)
    w_small = jnp.pad(w_in[:, o_ki:o_rest], ((0, 0), (0, LANES - IDX_DIM - N_IDX_HEADS))).astype(bf)
    return w_main, w_small


def _stages(x, mem, w_in, q_norm_g, kv_norm_g, w_uq, w_uk, w_uv, w_qidx, rel_bias, conv_w, w_mem_k, w_mem_v, w_out, ln1_g, ln1_b, w_router, router_bias, w_e_gate, w_e_up, w_e_down, w_s_gate, w_s_up, w_s_down, ln2_g, ln2_b, upto=None):
    B, S, D = x.shape
    T = B * S
    bf = MXU_DTYPE
    l = 0
    res = {}
    x2 = x.reshape(T, D)
    w_main, w_small = _split_w_in(w_in[l])
    cq, ckv, ckvt, kidx, iwt, yb, yc = _proj(
        x2, mem, w_main, w_small, q_norm_g[l].reshape(1, -1), kv_norm_g[l].reshape(1, -1), conv_w[l],
        w_mem_k[l].astype(bf), w_mem_v[l].astype(bf), B, S, tm=min(512, S))
    res.update(c_q=cq, c_kv=ckv, k_idx=kidx, y_b=yb, y_c=yc,
               idx_w=jnp.swapaxes(iwt, 1, 2) / (N_IDX_HEADS ** -0.5 * IDX_DIM ** -0.5))
    if upto == "proj":
        return res
    bias_t = _bias_tiles(rel_bias)
    ya = _dsa(cq, iwt, kidx, ckv, ckvt,
              w_qidx[l].reshape(Q_RANK, -1).astype(bf), w_uq[l].reshape(Q_RANK, -1).astype(bf),
              jnp.transpose(w_uk[l], (1, 0, 2)).astype(bf), jnp.transpose(w_uv[l], (1, 2, 0)).astype(bf),
              bias_t, B, S)
    res.update(y_a=ya)
    if upto == "dsa":
        return res

    x1, x1p, sel_t, w_t, pos_t, cnt = _mix_router(
        x2, ya, yb, yc, w_out[l].astype(bf), ln1_g[l].reshape(1, -1), ln1_b[l].reshape(1, -1),
        w_router[l].T, router_bias[l].reshape(-1, 1), tm=min(512, T))
    res.update(x1=x1)

    counts = cnt[:, 0].astype(jnp.int32)
    padded = (counts + ROW_BLOCK - 1) // ROW_BLOCK * ROW_BLOCK
    pad_end = jnp.cumsum(padded)
    pad_start = pad_end - padded
    n_blocks = -(-(T * TOP_K) // ROW_BLOCK) + N_EXPERTS
    n_rows = n_blocks * ROW_BLOCK
    block_start = jnp.arange(n_blocks, dtype=jnp.int32) * ROW_BLOCK
    block_e = jnp.minimum(jnp.sum((pad_end[None, :] <= block_start[:, None]).astype(jnp.int32), axis=1),
                          N_EXPERTS - 1)
    n_used = (pad_end[-1:] // ROW_BLOCK).astype(jnp.int32)

    dest_t, wk_t = _compact(sel_t, w_t, pos_t, pad_start.astype(jnp.float32).reshape(-1, 1), tm=min(512, T))
    block_valid = jnp.clip((pad_start + counts)[block_e] - block_start, 0, ROW_BLOCK).astype(jnp.int32)
    bt = SC_SCATTER_ROWS
    idx3 = jnp.transpose(dest_t.reshape(TOP_K, T // bt, bt), (1, 0, 2))
    xs = _sc_scatter_rows(x1p, idx3, n_rows)
    ys = _experts(xs, block_e, block_valid, n_used, w_e_gate[l], w_e_up[l], w_e_down[l])
    n_chunks = COMBINE_CHUNKS if T % (COMBINE_CHUNKS * 256) == 0 else 1
    t_chunk = T // n_chunks
    out = None
    for c in range(n_chunks):
        idx_c = dest_t[:, c * t_chunk:(c + 1) * t_chunk].reshape(-1)
        gathered = _sc_gather_rows(ys, idx_c).reshape(TOP_K, t_chunk, -1)
        out = _combine2(wk_t, x1, gathered, w_s_gate[l].astype(bf), w_s_up[l].astype(bf), w_s_down[l].astype(bf),
                        ln2_g[l].reshape(1, -1), ln2_b[l].reshape(1, -1), tc=min(256, t_chunk), chunk=c, prev=out)
    res.update(out=out.reshape(B, S, D))
    return res


def kernel(x, mem, w_in, q_norm_g, kv_norm_g, w_uq, w_uk, w_uv, w_qidx, rel_bias, conv_w, w_mem_k, w_mem_v, w_out, ln1_g, ln1_b, w_router, router_bias, w_e_gate, w_e_up, w_e_down, w_s_gate, w_s_up, w_s_down, ln2_g, ln2_b):
    return _stages(x, mem, w_in, q_norm_g, kv_norm_g, w_uq, w_uk, w_uv, w_qidx, rel_bias, conv_w, w_mem_k, w_mem_v, w_out, ln1_g, ln1_b, w_router, router_bias, w_e_gate, w_e_up, w_e_down, w_s_gate, w_s_up, w_s_down, ln2_g, ln2_b)["out"]
```

```python
import functools
import math

import jax
import jax.numpy as jnp
from jax import lax
from jax.experimental import pallas as pl
from jax.experimental.pallas import tpu as pltpu
from jax.experimental.pallas import tpu_sc as plsc

N_HEADS_A = 8
HEAD_DIM = 64
Q_RANK = 256
KV_RANK = 128
N_IDX_HEADS = 8
IDX_DIM = 64
TOPK_MAX = 256
REL_BUCKETS = 32
REL_MAX_DIST = 128
CONV_CH = 256
CONV_WIDTH = 3
N_MEM_HEADS = 4
MIX_A = N_HEADS_A * HEAD_DIM
MIX_C = N_MEM_HEADS * HEAD_DIM
N_EXPERTS = 64
N_GROUPS = 8
GROUP_SIZE = N_EXPERTS // N_GROUPS
TOPK_GROUPS = 4
TOP_K = 8
D_EXPERT = 256
ROUTED_SCALE = 2.5
MOE_BLOCK = 256
DEPTH = 1
ALPHA = (2.0 * DEPTH) ** 0.25
LN_EPS = 1e-5
RMS_EPS = 1e-6

LANES = 128
SUBLANES = 8
BF16_ROWS = 16
QB = 128
F32_LOWEST = -3.4028234663852886e38
VMEM_LIMIT = 56 * 1024 * 1024
MXU_DTYPE = jnp.bfloat16
ROW_BLOCK = 512

_NT = (((1,), (1,)), ((), ()))


def _dot(a, b):
    return jnp.dot(a, b, preferred_element_type=jnp.float32)


def _dot_nt(a, b):
    return lax.dot_general(a, b, _NT, preferred_element_type=jnp.float32)


def _cparams(sem):
    return pltpu.CompilerParams(dimension_semantics=sem, vmem_limit_bytes=VMEM_LIMIT)


def _bias_kernel(rb_ref, o_ref):
    s = lax.broadcasted_iota(jnp.int32, (QB, QB), 0)
    t = lax.broadcasted_iota(jnp.int32, (QB, QB), 1)
    max_exact = REL_BUCKETS // 2
    for tile in range(3):
        n = jnp.maximum(t - s + (2 - tile) * QB, 0)
        nf = jnp.maximum(n.astype(jnp.float32), 1.0)
        large = max_exact + (jnp.log(nf / max_exact) / math.log(REL_MAX_DIST / max_exact)
                             * (REL_BUCKETS - max_exact)).astype(jnp.int32)
        large = jnp.minimum(large, REL_BUCKETS - 1)
        bucket = jnp.where(n < max_exact, n, large)
        for h in range(N_HEADS_A):
            acc = jnp.zeros((QB, QB), jnp.float32)
            for b in range(REL_BUCKETS):
                acc = jnp.where(bucket == b, rb_ref[b, h], acc)
            o_ref[tile, h] = acc


def _bias_tiles(rel_bias):
    return pl.pallas_call(
        _bias_kernel,
        in_specs=[pl.BlockSpec(memory_space=pltpu.SMEM)],
        out_specs=pl.BlockSpec(memory_space=pltpu.VMEM),
        out_shape=jax.ShapeDtypeStruct((3, N_HEADS_A, QB, QB), jnp.float32),
        name="bias_tiles",
    )(rel_bias)


_MAIN_COLS = Q_RANK + KV_RANK + 3 * CONV_CH + MIX_C


def _proj_kernel(x_ref, mem_ref, wm_ref, ws_ref, qg_ref, kvg_ref, cw_ref, wmk_ref, wmv_ref,
                 cq_ref, ckv_ref, ckvt_ref, kidx_ref, iwt_ref, yb_ref, yc_ref,
                 carry_ref, mk_ref, mv_ref, *, tm):
    si = pl.program_id(1)

    @pl.when(si == 0)
    def _():
        carry_ref[...] = jnp.zeros_like(carry_ref)
        mb = mem_ref[0].astype(MXU_DTYPE)
        mk_ref[...] = _dot(mb, wmk_ref[...]).astype(MXU_DTYPE)
        mv_ref[...] = _dot(mb, wmv_ref[...]).astype(MXU_DTYPE)

    xb = x_ref[...].astype(MXU_DTYPE)
    p = _dot(xb, wm_ref[...])
    small = _dot(xb, ws_ref[...])

    o = 0
    cq = p[:, o:o + Q_RANK]; o += Q_RANK
    ckv = p[:, o:o + KV_RANK]; o += KV_RANK
    g_b = p[:, o:o + CONV_CH]; o += CONV_CH
    g_c = p[:, o:o + CONV_CH]; o += CONV_CH
    h_c = p[:, o:o + CONV_CH]; o += CONV_CH
    q_mem = p[:, o:o + MIX_C]

    cq = cq * lax.rsqrt(jnp.mean(cq * cq, axis=-1, keepdims=True) + RMS_EPS) * qg_ref[...]
    ckv = ckv * lax.rsqrt(jnp.mean(ckv * ckv, axis=-1, keepdims=True) + RMS_EPS) * kvg_ref[...]
    cq_ref[...] = cq.astype(MXU_DTYPE)
    ckv_b = ckv.astype(MXU_DTYPE)
    ckv_ref[...] = ckv_b
    ckvt_ref[0] = ckv.T.astype(MXU_DTYPE)

    kidx_ref[...] = small[:, :IDX_DIM].astype(MXU_DTYPE)
    small_t = small.T
    iwt_ref[0] = small_t[IDX_DIM:IDX_DIM + N_IDX_HEADS, :] * (N_IDX_HEADS ** -0.5 * IDX_DIM ** -0.5)

    u = g_c * h_c
    rows = lax.broadcasted_iota(jnp.int32, (tm, 1), 0)
    c6 = carry_ref[SUBLANES - 2:SUBLANES - 1, :]
    c7 = carry_ref[SUBLANES - 1:SUBLANES, :]
    u1 = jnp.where(rows == 0, c7, pltpu.roll(u, 1, 0))
    u2 = jnp.where(rows == 0, c6, jnp.where(rows == 1, c7, pltpu.roll(u, 2, 0)))
    y = cw_ref[0:1, :] * u2
    y = y + cw_ref[1:2, :] * u1
    y = y + cw_ref[2:3, :] * u
    yb_ref[...] = (g_b * y).astype(MXU_DTYPE)
    carry_ref[...] = u[tm - SUBLANES:, :]

    qm = q_mem.astype(MXU_DTYPE)
    outs = []
    for h in range(N_MEM_HEADS):
        sl = slice(h * HEAD_DIM, (h + 1) * HEAD_DIM)
        lg = _dot_nt(qm[:, sl], mk_ref[:, sl]) * (HEAD_DIM ** -0.5)
        lg = lg - jnp.max(lg, axis=-1, keepdims=True)
        e = jnp.exp(lg)
        pr = e / jnp.sum(e, axis=-1, keepdims=True)
        outs.append(_dot(pr.astype(MXU_DTYPE), mv_ref[:, sl]))
    yc_ref[...] = jnp.concatenate(outs, axis=-1).astype(MXU_DTYPE)


def _proj(x2, mem, w_main, w_small, q_g, kv_g, conv_w, w_mk, w_mv, B, S, tm):
    T, D = x2.shape
    n_mem = mem.shape[1]
    ns = S // tm
    row = lambda b, s: (b * ns + s, 0)
    const2 = lambda b, s: (0, 0)
    bf = MXU_DTYPE
    return pl.pallas_call(
        functools.partial(_proj_kernel, tm=tm),
        grid=(B, ns),
        in_specs=[
            pl.BlockSpec((tm, D), row),
            pl.BlockSpec((1, n_mem, D), lambda b, s: (b, 0, 0)),
            pl.BlockSpec(w_main.shape, const2),
            pl.BlockSpec(w_small.shape, const2),
            pl.BlockSpec(q_g.shape, const2),
            pl.BlockSpec(kv_g.shape, const2),
            pl.BlockSpec(conv_w.shape, const2),
            pl.BlockSpec(w_mk.shape, const2),
            pl.BlockSpec(w_mv.shape, const2),
        ],
        out_specs=[
            pl.BlockSpec((tm, Q_RANK), row),
            pl.BlockSpec((tm, KV_RANK), row),
            pl.BlockSpec((1, KV_RANK, tm), lambda b, s: (b, 0, s)),
            pl.BlockSpec((tm, IDX_DIM), row),
            pl.BlockSpec((1, N_IDX_HEADS, tm), lambda b, s: (b, 0, s)),
            pl.BlockSpec((tm, CONV_CH), row),
            pl.BlockSpec((tm, MIX_C), row),
        ],
        out_shape=[
            jax.ShapeDtypeStruct((T, Q_RANK), bf),
            jax.ShapeDtypeStruct((T, KV_RANK), bf),
            jax.ShapeDtypeStruct((B, KV_RANK, S), bf),
            jax.ShapeDtypeStruct((T, IDX_DIM), bf),
            jax.ShapeDtypeStruct((B, N_IDX_HEADS, S), jnp.float32),
            jax.ShapeDtypeStruct((T, CONV_CH), bf),
            jax.ShapeDtypeStruct((T, MIX_C), bf),
        ],
        scratch_shapes=[
            pltpu.VMEM((SUBLANES, CONV_CH), jnp.float32),
            pltpu.VMEM((n_mem, MIX_C), bf),
            pltpu.VMEM((n_mem, MIX_C), bf),
        ],
        compiler_params=_cparams(("arbitrary", "arbitrary")),
        name="proj",
    )(x2, mem, w_main, w_small, q_g, kv_g, conv_w, w_mk, w_mv)


def _key_to_f32(key):
    bits = jnp.where(key < 0, key ^ jnp.int32(0x7FFFFFFF), key)
    return pltpu.bitcast(bits, jnp.float32)


def _colsum8(v):
    return jnp.sum(v.reshape(QB // SUBLANES, SUBLANES, QB), axis=0)


def _colmax8(v):
    return jnp.max(v.reshape(QB // SUBLANES, SUBLANES, QB), axis=0)


UNROLL_WIDTHS = (4, 2, 1)


def _dsa_kernel(cq_ref, iwt_ref, kidx_ref, ckv_ref, ckvt_ref, wqi_ref, wuq_ref, wuk_ref, wuvt_ref,
                bias_ref, o_ref, qidx_ref, qlat_ref, score_ref, top_ref, mask_ref, logit_ref, acc_ref,
                *, k_sel, idx_bits):
    i = pl.program_id(1)
    f32 = jnp.float32
    bf = MXU_DTYPE
    n_blocks = i + 1
    s_loc = lax.broadcasted_iota(jnp.int32, (QB, QB), 0)
    t_glob = i * QB + lax.broadcasted_iota(jnp.int32, (QB, QB), 1)

    def blk(jb):
        return pl.multiple_of(jb * QB, QB)

    def block_loop(fn, init):
        c, start = init, 0
        for width in UNROLL_WIDTHS:
            n = (n_blocks - start) // width
            c = lax.fori_loop(0, n, lambda it, c, w=width, s=start: fn(s + it * w, w, c), c)
            start = start + n * width
        return c

    cq = cq_ref[...]
    q_all = _dot(cq, wuq_ref[...]).astype(bf)
    for h in range(N_HEADS_A):
        qidx_ref[h * QB:(h + 1) * QB, :] = _dot(cq, wqi_ref[:, h * IDX_DIM:(h + 1) * IDX_DIM]).astype(bf)
        qlat_ref[h * QB:(h + 1) * QB, :] = (
            _dot_nt(q_all[:, h * HEAD_DIM:(h + 1) * HEAD_DIM], wuk_ref[h]) * (HEAD_DIM ** -0.5)).astype(bf)
    iw = iwt_ref[0]

    def score_body(jb0, nb, c):
        d_blk = _dot_nt(kidx_ref[pl.ds(blk(jb0), nb * QB), :], qidx_ref[...])
        for sb in range(nb):
            off = blk(jb0 + sb)
            d_all = d_blk[sb * QB:(sb + 1) * QB, :]
            acc = jnp.maximum(d_all[:, 0:QB], 0.0) * iw[0:1, :]
            for h in range(1, N_IDX_HEADS):
                acc = acc + jnp.maximum(d_all[:, h * QB:(h + 1) * QB], 0.0) * iw[h:h + 1, :]
            sc = jnp.where(s_loc + off <= t_glob, acc + 0.0, F32_LOWEST)
            score_ref[pl.ds(off, QB), :] = sc
            top = pltpu.bitcast(pltpu.bitcast(sc, jnp.int32) & jnp.int32(_HIGH_HALF), f32)
            top_ref[pl.ds(off, QB), :] = top.astype(jnp.bfloat16)
        return c

    block_loop(score_body, 0)

    def count_top(trial_bf):
        one = jnp.ones((), jnp.bfloat16)
        zero = jnp.zeros((), jnp.bfloat16)

        def body(jb0, nb, acc):
            ind = jnp.where(top_ref[pl.ds(blk(jb0), nb * QB), :] >= trial_bf, one, zero)
            parts = [ind[r * BF16_ROWS:(r + 1) * BF16_ROWS, :] for r in range(nb * QB // BF16_ROWS)]
            while len(parts) > 1:
                parts = [a + b for a, b in zip(parts[0::2], parts[1::2])]
            return acc + parts[0]
        acc = block_loop(body, jnp.zeros((BF16_ROWS, QB), jnp.bfloat16))
        return jnp.sum(acc.astype(f32), axis=0, keepdims=True)

    def count_where(pred):
        def body(jb0, nb, acc):
            for sb in range(nb):
                off = blk(jb0 + sb)
                acc = acc + _colsum8(jnp.where(pred(score_ref[pl.ds(off, QB), :], off), 1.0, 0.0))
            return acc
        acc = block_loop(body, jnp.zeros((SUBLANES, QB), f32))
        return jnp.sum(acc, axis=0, keepdims=True)

    kf = float(k_sel)

    def search():
        def top_body(it, cand):
            trial = jnp.where(it == 0, jnp.int32(0), cand + lax.shift_left(jnp.int32(1), 31 - it))
            bits = jnp.where(trial < 0, trial ^ jnp.int32(0x7FFFFFFF), trial) & jnp.int32(_HIGH_HALF)
            cnt = count_top(pltpu.bitcast(bits, f32).astype(jnp.bfloat16))
            return jnp.where(cnt >= kf, trial, cand)

        cand = lax.fori_loop(0, 16, top_body, jnp.full((1, QB), -2 ** 31, jnp.int32))

        def bit_body(it, cand):
            trial = cand + lax.shift_left(jnp.int32(1), 15 - it)
            tf = _key_to_f32(trial)
            cnt = count_where(lambda sc, off: sc >= tf)
            return jnp.where(cnt >= kf, trial, cand)

        cand = lax.fori_loop(0, 16, bit_body, cand)
        thr = _key_to_f32(cand)
        n_gt = count_where(lambda sc, off: sc > thr)
        n_eq = count_where(lambda sc, off: sc == thr)
        need = kf - n_gt

        def tie_search():
            def tbody(it, xcut):
                trial = xcut + lax.shift_left(jnp.int32(1), idx_bits - 1 - it)
                cnt = count_where(lambda sc, off: (sc == thr) & (s_loc + off < trial))
                return jnp.where(cnt < need, trial, xcut)
            return lax.fori_loop(0, idx_bits, tbody, jnp.zeros((1, QB), jnp.int32))

        any_extra = jnp.max(n_eq - need) > 0.0
        xcut = lax.cond(any_extra, tie_search, lambda: jnp.full((1, QB), 2 ** idx_bits - 1, jnp.int32))
        return thr, xcut

    def no_search():
        return jnp.full((1, QB), F32_LOWEST, f32), jnp.full((1, QB), 2 ** idx_bits - 1, jnp.int32)

    thr, xcut = lax.cond((i + 1) * QB > k_sel, search, no_search)

    def mask_body(jb0, nb, c):
        for sb in range(nb):
            off = blk(jb0 + sb)
            sc = score_ref[pl.ds(off, QB), :]
            s_glob = s_loc + off
            keep = ((sc > thr) | ((sc == thr) & (s_glob <= xcut))) & (s_glob <= t_glob)
            mask_ref[pl.ds(off, QB), :] = jnp.where(keep, 0.0, -jnp.inf)
        return c

    block_loop(mask_body, 0)

    def p1_body(jb0, nb, m8):
        m8 = list(m8)
        lg_blk = _dot_nt(ckv_ref[pl.ds(blk(jb0), nb * QB), :], qlat_ref[...])
        for sb in range(nb):
            off = blk(jb0 + sb)
            lg = lg_blk[sb * QB:(sb + 1) * QB, :]
            msk = mask_ref[pl.ds(off, QB), :]
            bsel = jnp.clip(jb0 + sb - i + 2, 0, 2)
            for h in range(N_HEADS_A):
                lgh = lg[:, h * QB:(h + 1) * QB] + bias_ref[bsel, h] + msk
                logit_ref[pl.ds(off, QB), h * QB:(h + 1) * QB] = lgh
                m8[h] = jnp.maximum(m8[h], _colmax8(lgh))
        return tuple(m8)

    m8 = block_loop(p1_body, tuple(jnp.full((SUBLANES, QB), -jnp.inf, f32) for _ in range(N_HEADS_A)))
    m_row = [jnp.max(m, axis=0, keepdims=True) for m in m8]

    acc_ref[...] = jnp.zeros_like(acc_ref)

    def p2_body(jb0, nb, l8):
        l8 = list(l8)
        off = blk(jb0)
        rows = nb * QB
        ps = []
        for h in range(N_HEADS_A):
            p = jnp.exp(logit_ref[pl.ds(off, rows), h * QB:(h + 1) * QB] - m_row[h])
            l8[h] = l8[h] + jnp.sum(p.reshape(rows // SUBLANES, SUBLANES, QB), axis=0)
            ps.append(p.astype(bf))
        acc_ref[...] += _dot(ckvt_ref[0, :, pl.ds(off, rows)], jnp.concatenate(ps, axis=1))
        return tuple(l8)

    l8 = block_loop(p2_body, tuple(jnp.zeros((SUBLANES, QB), f32) for _ in range(N_HEADS_A)))

    outs = []
    for h in range(N_HEADS_A):
        l_row = jnp.sum(l8[h], axis=0, keepdims=True)
        o_lat_t = (acc_ref[:, h * QB:(h + 1) * QB] / l_row).astype(bf)
        outs.append(_dot(wuvt_ref[h], o_lat_t))
    o_ref[...] = jnp.concatenate(outs, axis=0).T.astype(o_ref.dtype)


def _dsa(cq, iwt, kidx, ckv, ckvt, w_qidx, w_uq, w_uk_h, w_uvt_h, bias_tiles, B, S):
    T = cq.shape[0]
    assert S % QB == 0 and QB >= REL_MAX_DIST
    nq = S // QB
    k_sel = min(TOPK_MAX, S // 4)
    idx_bits = max(1, (S - 1).bit_length())
    c2 = lambda b, i: (0, 0)
    c3 = lambda b, i: (0, 0, 0)
    return pl.pallas_call(
        functools.partial(_dsa_kernel, k_sel=k_sel, idx_bits=idx_bits),
        grid=(B, nq),
        in_specs=[
            pl.BlockSpec((QB, Q_RANK), lambda b, i: (b * nq + i, 0)),
            pl.BlockSpec((1, N_IDX_HEADS, QB), lambda b, i: (b, 0, i)),
            pl.BlockSpec((S, IDX_DIM), lambda b, i: (b, 0)),
            pl.BlockSpec((S, KV_RANK), lambda b, i: (b, 0)),
            pl.BlockSpec((1, KV_RANK, S), lambda b, i: (b, 0, 0)),
            pl.BlockSpec(w_qidx.shape, c2),
            pl.BlockSpec(w_uq.shape, c2),
            pl.BlockSpec(w_uk_h.shape, c3),
            pl.BlockSpec(w_uvt_h.shape, c3),
            pl.BlockSpec(bias_tiles.shape, lambda b, i: (0, 0, 0, 0)),
        ],
        out_specs=pl.BlockSpec((QB, MIX_A), lambda b, i: (b * nq + i, 0)),
        out_shape=jax.ShapeDtypeStruct((T, MIX_A), MXU_DTYPE),
        scratch_shapes=[
            pltpu.VMEM((N_IDX_HEADS * QB, IDX_DIM), MXU_DTYPE),
            pltpu.VMEM((N_HEADS_A * QB, KV_RANK), MXU_DTYPE),
            pltpu.VMEM((S, QB), jnp.float32),
            pltpu.VMEM((S, QB), jnp.bfloat16),
            pltpu.VMEM((S, QB), jnp.float32),
            pltpu.VMEM((S, N_HEADS_A * QB), jnp.float32),
            pltpu.VMEM((KV_RANK, N_HEADS_A * QB), jnp.float32),
        ],
        compiler_params=_cparams(("arbitrary", "arbitrary")),
        name="dsa",
    )(cq, iwt, kidx, ckv, ckvt, w_qidx, w_uq, w_uk_h, w_uvt_h, bias_tiles)


def _layer_norm(xf, g, b):
    mu = jnp.mean(xf, axis=-1, keepdims=True)
    xc = xf - mu
    var = jnp.mean(xc * xc, axis=-1, keepdims=True)
    return xc * lax.rsqrt(var + LN_EPS) * g + b


def _rank_rows(v, n):
    ri = lax.broadcasted_iota(jnp.int32, v.shape, 0)
    rank = jnp.zeros(v.shape, jnp.float32)
    for r2 in range(n):
        row = v[r2:r2 + 1, :]
        beats = (row > v) | ((row == v) & (ri > r2))
        rank = rank + jnp.where(beats, 1.0, 0.0)
    return rank


def _pack_factor():
    return 4 // jnp.dtype(MXU_DTYPE).itemsize


def _pack_rows(x):
    if _pack_factor() == 1:
        return pltpu.bitcast(x, jnp.int32)
    half = x.shape[1] // 2
    b = pltpu.bitcast(x.astype(MXU_DTYPE).astype(jnp.float32), jnp.int32)
    return b[:, half:] | lax.shift_right_logical(b[:, :half], jnp.int32(16))


_HIGH_HALF = -(1 << 16)


def _unpack_rows_f32(p):
    if _pack_factor() == 1:
        return [pltpu.bitcast(p, jnp.float32)]
    lo = pltpu.bitcast(lax.shift_left(p, jnp.int32(16)), jnp.float32)
    hi = pltpu.bitcast(p & jnp.int32(_HIGH_HALF), jnp.float32)
    return [lo, hi]


def _unpack_rows(p):
    return [v.astype(MXU_DTYPE) for v in _unpack_rows_f32(p)]


def _mix_router_kernel(x_ref, ya_ref, yb_ref, yc_ref, wo_ref, g_ref, b_ref, wrt_ref, rb_ref, exp_ref,
                       x1_ref, x1p_ref, sel_ref, w_ref, pos_ref, cnt_ref, base_ref, *, tm):
    step = pl.program_id(0)
    f32 = jnp.float32

    @pl.when(step == 0)
    def _():
        base_ref[...] = jnp.zeros_like(base_ref)

    mix = _dot(ya_ref[...], wo_ref[0:MIX_A, :])
    mix = mix + _dot(yb_ref[...], wo_ref[MIX_A:MIX_A + CONV_CH, :])
    mix = mix + _dot(yc_ref[...], wo_ref[MIX_A + CONV_CH:, :])
    x1 = _layer_norm(ALPHA * x_ref[...] + mix, g_ref[...], b_ref[...])
    x1_ref[...] = x1
    x1p_ref[...] = _pack_rows(x1)

    lg = lax.dot_general(wrt_ref[...], x1, _NT, precision=lax.Precision.HIGHEST, preferred_element_type=f32)
    s = 1.0 / (1.0 + jnp.exp(-lg))
    sc = s + rb_ref[...]

    g3 = sc.reshape(N_GROUPS, GROUP_SIZE, tm)
    m1 = jnp.max(g3, axis=1, keepdims=True)
    is_m1 = g3 == m1
    n_m1 = jnp.sum(jnp.where(is_m1, 1.0, 0.0), axis=1, keepdims=True)
    m2 = jnp.max(jnp.where(is_m1, -jnp.inf, g3), axis=1, keepdims=True)
    gscore = (m1 + jnp.where(n_m1 > 1.0, m1, m2)).reshape(N_GROUPS, tm)
    gsel = jnp.where(_rank_rows(gscore, N_GROUPS) < float(TOPK_GROUPS), 1.0, 0.0)
    emask = _dot(exp_ref[...], gsel.astype(MXU_DTYPE)) > 0.5
    masked = jnp.where(emask, sc, -jnp.inf)
    sel = (_rank_rows(masked, N_EXPERTS) < float(TOP_K)) & emask
    self_ = jnp.where(sel, 1.0, 0.0)
    top_s = jnp.where(sel, s, 0.0)
    w = top_s / jnp.sum(top_s, axis=0, keepdims=True) * ROUTED_SCALE

    t_r = lax.broadcasted_iota(jnp.int32, (tm, tm), 0)
    t_c = lax.broadcasted_iota(jnp.int32, (tm, tm), 1)
    upper = jnp.where(t_r < t_c, 1.0, 0.0).astype(MXU_DTYPE)
    pref = _dot(self_.astype(MXU_DTYPE), upper)
    base = base_ref[...]
    sel_ref[...] = self_
    w_ref[...] = w
    pos_ref[...] = base + pref
    base = base + jnp.sum(self_, axis=1, keepdims=True)
    base_ref[...] = base
    cnt_ref[...] = jnp.broadcast_to(base, cnt_ref.shape)


def _mix_router(x2, ya, yb, yc, w_out, ln_g, ln_b, w_router_t, router_bias, tm):
    T, D = x2.shape
    E = N_EXPERTS
    expand = (jnp.arange(E)[:, None] // GROUP_SIZE == jnp.arange(N_GROUPS)[None, :]).astype(MXU_DTYPE)
    row = lambda i: (i, 0)
    col = lambda i: (0, i)
    c2 = lambda i: (0, 0)
    f32 = jnp.float32
    return pl.pallas_call(
        functools.partial(_mix_router_kernel, tm=tm),
        grid=(T // tm,),
        in_specs=[
            pl.BlockSpec((tm, D), row),
            pl.BlockSpec((tm, MIX_A), row),
            pl.BlockSpec((tm, CONV_CH), row),
            pl.BlockSpec((tm, MIX_C), row),
            pl.BlockSpec(w_out.shape, c2),
            pl.BlockSpec((1, D), c2),
            pl.BlockSpec((1, D), c2),
            pl.BlockSpec((E, D), c2),
            pl.BlockSpec((E, 1), c2),
            pl.BlockSpec((E, N_GROUPS), c2),
        ],
        out_specs=[
            pl.BlockSpec((tm, D), row),
            pl.BlockSpec((tm, D // _pack_factor()), row),
            pl.BlockSpec((E, tm), col),
            pl.BlockSpec((E, tm), col),
            pl.BlockSpec((E, tm), col),
            pl.BlockSpec((E, LANES), c2),
        ],
        out_shape=[
            jax.ShapeDtypeStruct((T, D), f32),
            jax.ShapeDtypeStruct((T, D // _pack_factor()), jnp.int32),
            jax.ShapeDtypeStruct((E, T), f32),
            jax.ShapeDtypeStruct((E, T), f32),
            jax.ShapeDtypeStruct((E, T), f32),
            jax.ShapeDtypeStruct((E, LANES), f32),
        ],
        scratch_shapes=[pltpu.VMEM((E, 1), f32)],
        compiler_params=_cparams(("arbitrary",)),
        name="mix_router",
    )(x2, ya, yb, yc, w_out, ln_g, ln_b, w_router_t, router_bias, expand)


def _compact_kernel(sel_ref, w_ref, pos_ref, pstart_ref, low_ref, dest_ref, wk_ref):
    sel = sel_ref[...]
    on = sel > 0.5
    rank = _dot(low_ref[...], sel.astype(MXU_DTYPE))
    row = pstart_ref[...] + pos_ref[...]
    w = w_ref[...]
    dests, ws = [], []
    for k in range(TOP_K):
        m = on & (rank == float(k))
        dests.append(jnp.sum(jnp.where(m, row, 0.0), axis=0, keepdims=True))
        ws.append(jnp.sum(jnp.where(m, w, 0.0), axis=0, keepdims=True))
    dest_ref[...] = jnp.concatenate(dests, axis=0).astype(jnp.int32)
    wk_ref[...] = jnp.concatenate(ws, axis=0)


def _compact(sel_t, w_t, pos_t, pad_start, tm):
    E, T = sel_t.shape
    lower = (jnp.arange(E)[None, :] < jnp.arange(E)[:, None]).astype(MXU_DTYPE)
    col = lambda i: (0, i)
    c2 = lambda i: (0, 0)
    return pl.pallas_call(
        _compact_kernel,
        grid=(T // tm,),
        in_specs=[pl.BlockSpec((E, tm), col), pl.BlockSpec((E, tm), col), pl.BlockSpec((E, tm), col),
                  pl.BlockSpec((E, 1), c2), pl.BlockSpec((E, E), c2)],
        out_specs=[pl.BlockSpec((TOP_K, tm), col), pl.BlockSpec((TOP_K, tm), col)],
        out_shape=[jax.ShapeDtypeStruct((TOP_K, T), jnp.int32), jax.ShapeDtypeStruct((TOP_K, T), jnp.float32)],
        compiler_params=_cparams(("arbitrary",)),
        name="route_compact",
    )(sel_t, w_t, pos_t, pad_start, lower)


def _row_copy(src, s, dst, d, sem):
    return pltpu.make_async_copy(src.at[pl.ds(s, 1)], dst.at[pl.ds(d, 1)], sem)


def _dispatch_kernel(flo_ref, fhi_ref, dest_ref, x_ref, xs_hbm, zero_ref, sem, zsem, *, td):
    step = pl.program_id(0)

    @pl.when(step == 0)
    def _():
        zero_ref[...] = jnp.zeros_like(zero_ref)

        def per_expert(fn):
            def ebody(e, c):
                lax.fori_loop(flo_ref[e], fhi_ref[e], lambda r, c2: (fn(r), c2)[1], 0)
                return c
            lax.fori_loop(0, N_EXPERTS, ebody, 0)

        per_expert(lambda r: _row_copy(zero_ref, 0, xs_hbm, r, zsem).start())
        per_expert(lambda r: _row_copy(zero_ref, 0, xs_hbm, r, zsem).wait())

    def issue(r, c):
        for k in range(TOP_K):
            _row_copy(x_ref, r, xs_hbm, dest_ref[k, r], sem).start()
        return c

    def drain(r, c):
        for k in range(TOP_K):
            _row_copy(x_ref, r, xs_hbm, dest_ref[k, r], sem).wait()
        return c

    lax.fori_loop(0, td, issue, 0)
    lax.fori_loop(0, td, drain, 0)


def _dispatch(dest_t, x1p, fill_lo, fill_hi, n_rows, td):
    T, W = x1p.shape
    return pl.pallas_call(
        functools.partial(_dispatch_kernel, td=td),
        grid_spec=pltpu.PrefetchScalarGridSpec(
            num_scalar_prefetch=2,
            grid=(T // td,),
            in_specs=[
                pl.BlockSpec((TOP_K, td), lambda i, lo, hi: (0, i), memory_space=pltpu.SMEM),
                pl.BlockSpec((td, W), lambda i, lo, hi: (i, 0)),
            ],
            out_specs=pl.BlockSpec(memory_space=pl.ANY),
            scratch_shapes=[pltpu.VMEM((SUBLANES, W), x1p.dtype),
                            pltpu.SemaphoreType.DMA, pltpu.SemaphoreType.DMA],
        ),
        out_shape=jax.ShapeDtypeStruct((n_rows, W), x1p.dtype),
        compiler_params=_cparams(("arbitrary",)),
        name="dispatch",
    )(fill_lo, fill_hi, dest_t, x1p)


def _silu(g):
    return g / (1.0 + jnp.exp(-g))


def _expert_kernel(be_ref, nv_ref, nu_ref, xs_ref, wg_ref, wu_ref, wd_ref, ys_ref, wgb_ref, wub_ref, wdb_ref):
    i = pl.program_id(0)

    @pl.when((i == 0) | (be_ref[i] != be_ref[jnp.maximum(i - 1, 0)]))
    def _():
        wgb_ref[...] = wg_ref[0].astype(MXU_DTYPE)
        wub_ref[...] = wu_ref[0].astype(MXU_DTYPE)
        wdb_ref[...] = wd_ref[0].astype(MXU_DTYPE)

    @pl.when(i < nu_ref[0])
    def _():
        live = lax.broadcasted_iota(jnp.int32, (ROW_BLOCK, 1), 0) < nv_ref[i]
        parts = [jnp.where(live, v, jnp.zeros_like(v)) for v in _unpack_rows(xs_ref[...])]
        dk = wgb_ref.shape[0] // len(parts)

        def proj(w_ref):
            acc = _dot(parts[0], w_ref[0:dk, :])
            for n in range(1, len(parts)):
                acc = acc + _dot(parts[n], w_ref[n * dk:(n + 1) * dk, :])
            return acc

        a = (_silu(proj(wgb_ref)) * proj(wub_ref)).astype(MXU_DTYPE)
        ys_ref[...] = _pack_rows(_dot(a, wdb_ref[...]))


def _experts(xs, block_e, block_valid, n_used, w_gate, w_up, w_down):
    n_rows, W = xs.shape
    D = w_gate.shape[1]
    n_blocks = n_rows // ROW_BLOCK
    blk = lambda i, be, nv, nu: (jnp.minimum(i, nu[0] - 1), 0)
    wsel = lambda i, be, nv, nu: (be[i], 0, 0)
    return pl.pallas_call(
        _expert_kernel,
        grid_spec=pltpu.PrefetchScalarGridSpec(
            num_scalar_prefetch=3,
            grid=(n_blocks,),
            in_specs=[
                pl.BlockSpec((ROW_BLOCK, W), blk),
                pl.BlockSpec((1, D, D_EXPERT), wsel),
                pl.BlockSpec((1, D, D_EXPERT), wsel),
                pl.BlockSpec((1, D_EXPERT, D), wsel),
            ],
            out_specs=pl.BlockSpec((ROW_BLOCK, W), blk),
            scratch_shapes=[pltpu.VMEM((D, D_EXPERT), MXU_DTYPE), pltpu.VMEM((D, D_EXPERT), MXU_DTYPE),
                            pltpu.VMEM((D_EXPERT, D), MXU_DTYPE)],
        ),
        out_shape=jax.ShapeDtypeStruct((n_rows, W), xs.dtype),
        compiler_params=_cparams(("arbitrary",)),
        name="experts",
    )(block_e, block_valid, n_used, xs, w_gate, w_up, w_down)


SC_CORES = 2
SC_SUBCORES = 16
SC_GATHER_ROWS = 64
COMBINE_CHUNKS = 4


def _sc_gather_rows(table, idx):
    n = idx.shape[0]
    w = table.shape[1]
    n_workers = SC_CORES * SC_SUBCORES
    per_worker = n // n_workers
    assert n % n_workers == 0 and per_worker % SC_GATHER_ROWS == 0
    mesh = plsc.VectorSubcoreMesh(core_axis_name="c", subcore_axis_name="s")

    @functools.partial(
        pl.kernel, mesh=mesh,
        out_type=jax.ShapeDtypeStruct((n, w), table.dtype),
        scratch_types=[
            pltpu.VMEM((2, SC_GATHER_ROWS), jnp.int32),
            pltpu.VMEM((2, SC_GATHER_ROWS, w), table.dtype),
            pltpu.SemaphoreType.DMA((2,)),
        ],
        name="sc_gather_rows",
    )
    def gather(table_hbm, idx_hbm, out_hbm, idx_v, rows_v, sem):
        wid = lax.axis_index("s") * SC_CORES + lax.axis_index("c")
        base = wid * per_worker
        n_steps = per_worker // SC_GATHER_ROWS

        def gather_copy(slot):
            return pltpu.make_async_copy(table_hbm.at[idx_v.at[slot]], rows_v.at[slot], sem.at[slot])

        def start(step, slot):
            pltpu.sync_copy(idx_hbm.at[pl.ds(base + step * SC_GATHER_ROWS, SC_GATHER_ROWS)], idx_v.at[slot])
            gather_copy(slot).start()

        start(0, 0)

        @pl.loop(0, n_steps, step=2)
        def _(g):
            for slot in range(2):
                step = g + slot

                @pl.when(step + 1 < n_steps)
                def _():
                    start(step + 1, 1 - slot)

                gather_copy(slot).wait()
                pltpu.sync_copy(rows_v.at[slot], out_hbm.at[pl.ds(base + step * SC_GATHER_ROWS, SC_GATHER_ROWS)])

    return gather(table, idx)


SC_SCATTER_ROWS = 64


def _sc_scatter_rows(rows, idx3, n_out):
    n_src, w = rows.shape
    n_chunks, n_dst, batch = idx3.shape
    n_workers = SC_CORES * SC_SUBCORES
    assert batch == SC_SCATTER_ROWS and n_chunks * batch == n_src and n_chunks % (2 * n_workers) == 0
    per_worker = n_chunks // n_workers
    mesh = plsc.VectorSubcoreMesh(core_axis_name="c", subcore_axis_name="s")

    @functools.partial(
        pl.kernel, mesh=mesh,
        out_type=jax.ShapeDtypeStruct((n_out, w), rows.dtype),
        scratch_types=[
            pltpu.VMEM((2, n_dst, batch), jnp.int32),
            pltpu.VMEM((2, batch, w), rows.dtype),
            pltpu.SemaphoreType.DMA((2,)),
            pltpu.SemaphoreType.DMA,
        ],
        name="sc_scatter_rows",
    )
    def scatter(rows_hbm, idx_hbm, out_hbm, idx_v, rows_v, load_sem, store_sem):
        wid = lax.axis_index("s") * SC_CORES + lax.axis_index("c")

        def load_copy(step, slot):
            c = wid * per_worker + step
            return pltpu.make_async_copy(rows_hbm.at[pl.ds(c * batch, batch)], rows_v.at[slot], load_sem.at[slot])

        def load(step, slot):
            pltpu.sync_copy(idx_hbm.at[wid * per_worker + step], idx_v.at[slot])
            load_copy(step, slot).start()

        def store_copy(slot, k):
            return pltpu.make_async_copy(rows_v.at[slot], out_hbm.at[idx_v.at[slot].at[k]], store_sem)

        load(0, 0)

        @pl.loop(0, per_worker, step=2)
        def _(g):
            for slot in range(2):
                step = g + slot

                @pl.when(step + 1 < per_worker)
                def _():
                    load(step + 1, 1 - slot)

                load_copy(step, slot).wait()
                for k in range(n_dst):
                    store_copy(slot, k).start()
                for k in range(n_dst):
                    store_copy(slot, k).wait()

    return scatter(rows, idx3)


def _shared_kernel(x1_ref, wsg_ref, wsu_ref, wsd_ref, o_ref):
    xb = x1_ref[...].astype(MXU_DTYPE)
    a = (_silu(_dot(xb, wsg_ref[...])) * _dot(xb, wsu_ref[...])).astype(MXU_DTYPE)
    o_ref[...] = _dot(a, wsd_ref[...])


def _shared_expert(x1, w_sg, w_su, w_sd, tm):
    T, D = x1.shape
    row = lambda i: (i, 0)
    c2 = lambda i: (0, 0)
    return pl.pallas_call(
        _shared_kernel,
        grid=(T // tm,),
        in_specs=[pl.BlockSpec((tm, D), row), pl.BlockSpec(w_sg.shape, c2), pl.BlockSpec(w_su.shape, c2),
                  pl.BlockSpec(w_sd.shape, c2)],
        out_specs=pl.BlockSpec((tm, D), row),
        out_shape=jax.ShapeDtypeStruct((T, D), jnp.float32),
        compiler_params=_cparams(("arbitrary",)),
        name="shared_expert",
    )(x1, w_sg, w_su, w_sd)


def _combine2_kernel(wk_ref, x1_ref, g_ref_rows, wsg_ref, wsu_ref, wsd_ref, g_ref, b_ref, o_ref):
    x1 = x1_ref[...]
    xb = x1.astype(MXU_DTYPE)
    a = (_silu(_dot(xb, wsg_ref[...])) * _dot(xb, wsu_ref[...])).astype(MXU_DTYPE)
    shared = _dot(a, wsd_ref[...])
    wk = wk_ref[...].T
    groups = [wk[:, 0:1] * v for v in _unpack_rows_f32(g_ref_rows[0])]
    for k in range(1, TOP_K):
        groups = [g + wk[:, k:k + 1] * v for g, v in zip(groups, _unpack_rows_f32(g_ref_rows[k]))]
    routed = jnp.concatenate(groups, axis=1)
    o_ref[...] = _layer_norm(ALPHA * x1 + (routed + shared), g_ref[...], b_ref[...])


def _combine2_kernel_into(wk_ref, x1_ref, g_ref_rows, wsg_ref, wsu_ref, wsd_ref, g_ref, b_ref, prev_ref, o_ref):
    del prev_ref
    _combine2_kernel(wk_ref, x1_ref, g_ref_rows, wsg_ref, wsu_ref, wsd_ref, g_ref, b_ref, o_ref)


def _combine2(wk_t, x1, gathered, w_sg, w_su, w_sd, ln_g, ln_b, tc, chunk, prev):
    T, D = x1.shape
    _, t_chunk, W = gathered.shape
    base = chunk * (t_chunk // tc)
    row = lambda i: (base + i, 0)
    c2 = lambda i: (0, 0)
    in_specs = [
        pl.BlockSpec((TOP_K, tc), lambda i: (0, base + i)),
        pl.BlockSpec((tc, D), row),
        pl.BlockSpec((TOP_K, tc, W), lambda i: (0, i, 0)),
        pl.BlockSpec(w_sg.shape, c2),
        pl.BlockSpec(w_su.shape, c2),
        pl.BlockSpec(w_sd.shape, c2),
        pl.BlockSpec((1, D), c2),
        pl.BlockSpec((1, D), c2),
    ]
    args = [wk_t, x1, gathered, w_sg, w_su, w_sd, ln_g, ln_b]
    if prev is None:
        body, aliases = _combine2_kernel, {}
    else:
        body, aliases = _combine2_kernel_into, {len(args): 0}
        in_specs.append(pl.BlockSpec(memory_space=pl.ANY))
        args.append(prev)
    return pl.pallas_call(
        body,
        grid=(t_chunk // tc,),
        in_specs=in_specs,
        out_specs=pl.BlockSpec((tc, D), row),
        out_shape=jax.ShapeDtypeStruct((T, D), jnp.float32),
        input_output_aliases=aliases,
        compiler_params=_cparams(("arbitrary",)),
        name="combine",
    )(*args)


def _combine_kernel(dest_ref, wk_ref, x1_ref, ys_hbm, wsg_ref, wsu_ref, wsd_ref, g_ref, b_ref,
                    o_ref, buf_ref, sem, *, tc):
    def issue(r, c):
        for k in range(TOP_K):
            _row_copy(ys_hbm, dest_ref[k, r], buf_ref.at[k], r, sem).start()
        return c

    def drain(r, c):
        for k in range(TOP_K):
            _row_copy(ys_hbm, dest_ref[k, r], buf_ref.at[k], r, sem).wait()
        return c

    lax.fori_loop(0, tc, issue, 0)
    x1 = x1_ref[...]
    xb = x1.astype(MXU_DTYPE)
    a = (_silu(_dot(xb, wsg_ref[...])) * _dot(xb, wsu_ref[...])).astype(MXU_DTYPE)
    shared = _dot(a, wsd_ref[...])
    lax.fori_loop(0, tc, drain, 0)
    wk = wk_ref[...]
    groups = [wk[:, 0:1] * v for v in _unpack_rows_f32(buf_ref[0])]
    for k in range(1, TOP_K):
        groups = [g + wk[:, k:k + 1] * v for g, v in zip(groups, _unpack_rows_f32(buf_ref[k]))]
    routed = jnp.concatenate(groups, axis=1)
    o_ref[...] = _layer_norm(ALPHA * x1 + (routed + shared), g_ref[...], b_ref[...])


def _combine(dest_t, wk, x1, ys, w_sg, w_su, w_sd, ln_g, ln_b, tc):
    T, D = x1.shape
    row = lambda i: (i, 0)
    c2 = lambda i: (0, 0)
    return pl.pallas_call(
        functools.partial(_combine_kernel, tc=tc),
        grid=(T // tc,),
        in_specs=[
            pl.BlockSpec((TOP_K, tc), lambda i: (0, i), memory_space=pltpu.SMEM),
            pl.BlockSpec((tc, TOP_K), row),
            pl.BlockSpec((tc, D), row),
            pl.BlockSpec(memory_space=pl.ANY),
            pl.BlockSpec(w_sg.shape, c2),
            pl.BlockSpec(w_su.shape, c2),
            pl.BlockSpec(w_sd.shape, c2),
            pl.BlockSpec((1, D), c2),
            pl.BlockSpec((1, D), c2),
        ],
        out_specs=pl.BlockSpec((tc, D), row),
        out_shape=jax.ShapeDtypeStruct((T, D), jnp.float32),
        scratch_shapes=[pltpu.VMEM((TOP_K, tc, ys.shape[1]), ys.dtype), pltpu.SemaphoreType.DMA],
        compiler_params=_cparams(("arbitrary",)),
        name="combine",
    )(dest_t, wk, x1, ys, w_sg, w_su, w_sd, ln_g, ln_b)


def _split_w_in(w_in):
    bf = MXU_DTYPE
    o_kv = Q_RANK
    o_ki = o_kv + KV_RANK
    o_iw = o_ki + IDX_DIM
    o_rest = o_iw + N_IDX_HEADS
    w_main = jnp.concatenate([w_in[:, :o_ki], w_in[:, o_rest:]], axis=1).astype(bf)
    w_small = jnp.pad(w_in[:, o_ki:o_rest], ((0, 0), (0, LANES - IDX_DIM - N_IDX_HEADS))).astype(bf)
    return w_main, w_small


def _stages(x, mem, w_in, q_norm_g, kv_norm_g, w_uq, w_uk, w_uv, w_qidx, rel_bias, conv_w, w_mem_k, w_mem_v, w_out, ln1_g, ln1_b, w_router, router_bias, w_e_gate, w_e_up, w_e_down, w_s_gate, w_s_up, w_s_down, ln2_g, ln2_b, upto=None):
    B, S, D = x.shape
    T = B * S
    bf = MXU_DTYPE
    l = 0
    res = {}
    x2 = x.reshape(T, D)
    w_main, w_small = _split_w_in(w_in[l])
    cq, ckv, ckvt, kidx, iwt, yb, yc = _proj(
        x2, mem, w_main, w_small, q_norm_g[l].reshape(1, -1), kv_norm_g[l].reshape(1, -1), conv_w[l],
        w_mem_k[l].astype(bf), w_mem_v[l].astype(bf), B, S, tm=min(512, S))
    res.update(c_q=cq, c_kv=ckv, k_idx=kidx, y_b=yb, y_c=yc,
               idx_w=jnp.swapaxes(iwt, 1, 2) / (N_IDX_HEADS ** -0.5 * IDX_DIM ** -0.5))
    if upto == "proj":
        return res
    bias_t = _bias_tiles(rel_bias)
    ya = _dsa(cq, iwt, kidx, ckv, ckvt,
              w_qidx[l].reshape(Q_RANK, -1).astype(bf), w_uq[l].reshape(Q_RANK, -1).astype(bf),
              jnp.transpose(w_uk[l], (1, 0, 2)).astype(bf), jnp.transpose(w_uv[l], (1, 2, 0)).astype(bf),
              bias_t, B, S)
    res.update(y_a=ya)
    if upto == "dsa":
        return res

    x1, x1p, sel_t, w_t, pos_t, cnt = _mix_router(
        x2, ya, yb, yc, w_out[l].astype(bf), ln1_g[l].reshape(1, -1), ln1_b[l].reshape(1, -1),
        w_router[l].T, router_bias[l].reshape(-1, 1), tm=min(512, T))
    res.update(x1=x1)

    counts = cnt[:, 0].astype(jnp.int32)
    padded = (counts + ROW_BLOCK - 1) // ROW_BLOCK * ROW_BLOCK
    pad_end = jnp.cumsum(padded)
    pad_start = pad_end - padded
    n_blocks = -(-(T * TOP_K) // ROW_BLOCK) + N_EXPERTS
    n_rows = n_blocks * ROW_BLOCK
    block_start = jnp.arange(n_blocks, dtype=jnp.int32) * ROW_BLOCK
    block_e = jnp.minimum(jnp.sum((pad_end[None, :] <= block_start[:, None]).astype(jnp.int32), axis=1),
                          N_EXPERTS - 1)
    n_used = (pad_end[-1:] // ROW_BLOCK).astype(jnp.int32)

    dest_t, wk_t = _compact(sel_t, w_t, pos_t, pad_start.astype(jnp.float32).reshape(-1, 1), tm=min(512, T))
    block_valid = jnp.clip((pad_start + counts)[block_e] - block_start, 0, ROW_BLOCK).astype(jnp.int32)
    bt = SC_SCATTER_ROWS
    idx3 = jnp.transpose(dest_t.reshape(TOP_K, T // bt, bt), (1, 0, 2))
    xs = _sc_scatter_rows(x1p, idx3, n_rows)
    ys = _experts(xs, block_e, block_valid, n_used, w_e_gate[l], w_e_up[l], w_e_down[l])
    n_chunks = COMBINE_CHUNKS if T % (COMBINE_CHUNKS * 256) == 0 else 1
    t_chunk = T // n_chunks
    out = None
    for c in range(n_chunks):
        idx_c = dest_t[:, c * t_chunk:(c + 1) * t_chunk].reshape(-1)
        gathered = _sc_gather_rows(ys, idx_c).reshape(TOP_K, t_chunk, -1)
        out = _combine2(wk_t, x1, gathered, w_s_gate[l].astype(bf), w_s_up[l].astype(bf), w_s_down[l].astype(bf),
                        ln2_g[l].reshape(1, -1), ln2_b[l].reshape(1, -1), tc=min(256, t_chunk), chunk=c, prev=out)
    res.update(out=out.reshape(B, S, D))
    return res


def kernel(x, mem, w_in, q_norm_g, kv_norm_g, w_uq, w_uk, w_uv, w_qidx, rel_bias, conv_w, w_mem_k, w_mem_v, w_out, ln1_g, ln1_b, w_router, router_bias, w_e_gate, w_e_up, w_e_down, w_s_gate, w_s_up, w_s_down, ln2_g, ln2_b):
    return _stages(x, mem, w_in, q_norm_g, kv_norm_g, w_uq, w_uk, w_uv, w_qidx, rel_bias, conv_w, w_mem_k, w_mem_v, w_out, ln1_g, ln1_b, w_router, router_bias, w_e_gate, w_e_up, w_e_down, w_s_gate, w_s_up, w_s_down, ln2_g, ln2_b)["out"]
```

```python
import functools
import math

import jax
import jax.numpy as jnp
from jax import lax
from jax.experimental import pallas as pl
from jax.experimental.pallas import tpu as pltpu
from jax.experimental.pallas import tpu_sc as plsc

N_HEADS_A = 8
HEAD_DIM = 64
Q_RANK = 256
KV_RANK = 128
N_IDX_HEADS = 8
IDX_DIM = 64
TOPK_MAX = 256
REL_BUCKETS = 32
REL_MAX_DIST = 128
CONV_CH = 256
CONV_WIDTH = 3
N_MEM_HEADS = 4
MIX_A = N_HEADS_A * HEAD_DIM
MIX_C = N_MEM_HEADS * HEAD_DIM
N_EXPERTS = 64
N_GROUPS = 8
GROUP_SIZE = N_EXPERTS // N_GROUPS
TOPK_GROUPS = 4
TOP_K = 8
D_EXPERT = 256
ROUTED_SCALE = 2.5
MOE_BLOCK = 256
DEPTH = 1
ALPHA = (2.0 * DEPTH) ** 0.25
LN_EPS = 1e-5
RMS_EPS = 1e-6

LANES = 128
SUBLANES = 8
QB = 128
F32_LOWEST = -3.4028234663852886e38
VMEM_LIMIT = 56 * 1024 * 1024
MXU_DTYPE = jnp.bfloat16
ROW_BLOCK = 512

_NT = (((1,), (1,)), ((), ()))


def _dot(a, b):
    return jnp.dot(a, b, preferred_element_type=jnp.float32)


def _dot_nt(a, b):
    return lax.dot_general(a, b, _NT, preferred_element_type=jnp.float32)


def _cparams(sem):
    return pltpu.CompilerParams(dimension_semantics=sem, vmem_limit_bytes=VMEM_LIMIT)


def _bias_kernel(rb_ref, o_ref):
    s = lax.broadcasted_iota(jnp.int32, (QB, QB), 0)
    t = lax.broadcasted_iota(jnp.int32, (QB, QB), 1)
    max_exact = REL_BUCKETS // 2
    for tile in range(3):
        n = jnp.maximum(t - s + (2 - tile) * QB, 0)
        nf = jnp.maximum(n.astype(jnp.float32), 1.0)
        large = max_exact + (jnp.log(nf / max_exact) / math.log(REL_MAX_DIST / max_exact)
                             * (REL_BUCKETS - max_exact)).astype(jnp.int32)
        large = jnp.minimum(large, REL_BUCKETS - 1)
        bucket = jnp.where(n < max_exact, n, large)
        for h in range(N_HEADS_A):
            acc = jnp.zeros((QB, QB), jnp.float32)
            for b in range(REL_BUCKETS):
                acc = jnp.where(bucket == b, rb_ref[b, h], acc)
            o_ref[tile, h] = acc


def _bias_tiles(rel_bias):
    return pl.pallas_call(
        _bias_kernel,
        in_specs=[pl.BlockSpec(memory_space=pltpu.SMEM)],
        out_specs=pl.BlockSpec(memory_space=pltpu.VMEM),
        out_shape=jax.ShapeDtypeStruct((3, N_HEADS_A, QB, QB), jnp.float32),
        name="bias_tiles",
    )(rel_bias)


_MAIN_COLS = Q_RANK + KV_RANK + 3 * CONV_CH + MIX_C


def _proj_kernel(x_ref, mem_ref, wm_ref, ws_ref, qg_ref, kvg_ref, cw_ref, wmk_ref, wmv_ref,
                 cq_ref, ckv_ref, ckvt_ref, kidx_ref, iwt_ref, yb_ref, yc_ref,
                 carry_ref, mk_ref, mv_ref, *, tm):
    si = pl.program_id(1)

    @pl.when(si == 0)
    def _():
        carry_ref[...] = jnp.zeros_like(carry_ref)
        mb = mem_ref[0].astype(MXU_DTYPE)
        mk_ref[...] = _dot(mb, wmk_ref[...]).astype(MXU_DTYPE)
        mv_ref[...] = _dot(mb, wmv_ref[...]).astype(MXU_DTYPE)

    xb = x_ref[...].astype(MXU_DTYPE)
    p = _dot(xb, wm_ref[...])
    small = _dot(xb, ws_ref[...])

    o = 0
    cq = p[:, o:o + Q_RANK]; o += Q_RANK
    ckv = p[:, o:o + KV_RANK]; o += KV_RANK
    g_b = p[:, o:o + CONV_CH]; o += CONV_CH
    g_c = p[:, o:o + CONV_CH]; o += CONV_CH
    h_c = p[:, o:o + CONV_CH]; o += CONV_CH
    q_mem = p[:, o:o + MIX_C]

    cq = cq * lax.rsqrt(jnp.mean(cq * cq, axis=-1, keepdims=True) + RMS_EPS) * qg_ref[...]
    ckv = ckv * lax.rsqrt(jnp.mean(ckv * ckv, axis=-1, keepdims=True) + RMS_EPS) * kvg_ref[...]
    cq_ref[...] = cq.astype(MXU_DTYPE)
    ckv_b = ckv.astype(MXU_DTYPE)
    ckv_ref[...] = ckv_b
    ckvt_ref[0] = ckv.T.astype(MXU_DTYPE)

    kidx_ref[...] = small[:, :IDX_DIM].astype(MXU_DTYPE)
    small_t = small.T
    iwt_ref[0] = small_t[IDX_DIM:IDX_DIM + N_IDX_HEADS, :] * (N_IDX_HEADS ** -0.5 * IDX_DIM ** -0.5)

    u = g_c * h_c
    rows = lax.broadcasted_iota(jnp.int32, (tm, 1), 0)
    c6 = carry_ref[SUBLANES - 2:SUBLANES - 1, :]
    c7 = carry_ref[SUBLANES - 1:SUBLANES, :]
    u1 = jnp.where(rows == 0, c7, pltpu.roll(u, 1, 0))
    u2 = jnp.where(rows == 0, c6, jnp.where(rows == 1, c7, pltpu.roll(u, 2, 0)))
    y = cw_ref[0:1, :] * u2
    y = y + cw_ref[1:2, :] * u1
    y = y + cw_ref[2:3, :] * u
    yb_ref[...] = (g_b * y).astype(MXU_DTYPE)
    carry_ref[...] = u[tm - SUBLANES:, :]

    qm = q_mem.astype(MXU_DTYPE)
    outs = []
    for h in range(N_MEM_HEADS):
        sl = slice(h * HEAD_DIM, (h + 1) * HEAD_DIM)
        lg = _dot_nt(qm[:, sl], mk_ref[:, sl]) * (HEAD_DIM ** -0.5)
        lg = lg - jnp.max(lg, axis=-1, keepdims=True)
        e = jnp.exp(lg)
        pr = e / jnp.sum(e, axis=-1, keepdims=True)
        outs.append(_dot(pr.astype(MXU_DTYPE), mv_ref[:, sl]))
    yc_ref[...] = jnp.concatenate(outs, axis=-1).astype(MXU_DTYPE)


def _proj(x2, mem, w_main, w_small, q_g, kv_g, conv_w, w_mk, w_mv, B, S, tm):
    T, D = x2.shape
    n_mem = mem.shape[1]
    ns = S // tm
    row = lambda b, s: (b * ns + s, 0)
    const2 = lambda b, s: (0, 0)
    bf = MXU_DTYPE
    return pl.pallas_call(
        functools.partial(_proj_kernel, tm=tm),
        grid=(B, ns),
        in_specs=[
            pl.BlockSpec((tm, D), row),
            pl.BlockSpec((1, n_mem, D), lambda b, s: (b, 0, 0)),
            pl.BlockSpec(w_main.shape, const2),
            pl.BlockSpec(w_small.shape, const2),
            pl.BlockSpec(q_g.shape, const2),
            pl.BlockSpec(kv_g.shape, const2),
            pl.BlockSpec(conv_w.shape, const2),
            pl.BlockSpec(w_mk.shape, const2),
            pl.BlockSpec(w_mv.shape, const2),
        ],
        out_specs=[
            pl.BlockSpec((tm, Q_RANK), row),
            pl.BlockSpec((tm, KV_RANK), row),
            pl.BlockSpec((1, KV_RANK, tm), lambda b, s: (b, 0, s)),
            pl.BlockSpec((tm, IDX_DIM), row),
            pl.BlockSpec((1, N_IDX_HEADS, tm), lambda b, s: (b, 0, s)),
            pl.BlockSpec((tm, CONV_CH), row),
            pl.BlockSpec((tm, MIX_C), row),
        ],
        out_shape=[
            jax.ShapeDtypeStruct((T, Q_RANK), bf),
            jax.ShapeDtypeStruct((T, KV_RANK), bf),
            jax.ShapeDtypeStruct((B, KV_RANK, S), bf),
            jax.ShapeDtypeStruct((T, IDX_DIM), bf),
            jax.ShapeDtypeStruct((B, N_IDX_HEADS, S), jnp.float32),
            jax.ShapeDtypeStruct((T, CONV_CH), bf),
            jax.ShapeDtypeStruct((T, MIX_C), bf),
        ],
        scratch_shapes=[
            pltpu.VMEM((SUBLANES, CONV_CH), jnp.float32),
            pltpu.VMEM((n_mem, MIX_C), bf),
            pltpu.VMEM((n_mem, MIX_C), bf),
        ],
        compiler_params=_cparams(("arbitrary", "arbitrary")),
        name="proj",
    )(x2, mem, w_main, w_small, q_g, kv_g, conv_w, w_mk, w_mv)


def _key_to_f32(key):
    bits = jnp.where(key < 0, key ^ jnp.int32(0x7FFFFFFF), key)
    return pltpu.bitcast(bits, jnp.float32)


def _colsum8(v):
    return jnp.sum(v.reshape(QB // SUBLANES, SUBLANES, QB), axis=0)


def _colmax8(v):
    return jnp.max(v.reshape(QB // SUBLANES, SUBLANES, QB), axis=0)


UNROLL_WIDTHS = (8, 4, 2, 1)


def _dsa_kernel(cq_ref, iwt_ref, kidx_ref, ckv_ref, ckvt_ref, wqi_ref, wuq_ref, wuk_ref, wuvt_ref,
                bias_ref, o_ref, qidx_ref, qlat_ref, score_ref, mask_ref, logit_ref, acc_ref,
                *, k_sel, idx_bits):
    i = pl.program_id(1)
    f32 = jnp.float32
    bf = MXU_DTYPE
    n_blocks = i + 1
    s_loc = lax.broadcasted_iota(jnp.int32, (QB, QB), 0)
    t_glob = i * QB + lax.broadcasted_iota(jnp.int32, (QB, QB), 1)

    def blk(jb):
        return pl.multiple_of(jb * QB, QB)

    def block_loop(fn, init):
        c, start = init, 0
        for width in UNROLL_WIDTHS:
            n = (n_blocks - start) // width
            c = lax.fori_loop(0, n, lambda it, c, w=width, s=start: fn(s + it * w, w, c), c)
            start = start + n * width
        return c

    cq = cq_ref[...]
    q_all = _dot(cq, wuq_ref[...]).astype(bf)
    for h in range(N_HEADS_A):
        qidx_ref[h * QB:(h + 1) * QB, :] = _dot(cq, wqi_ref[:, h * IDX_DIM:(h + 1) * IDX_DIM]).astype(bf)
        qlat_ref[h * QB:(h + 1) * QB, :] = (
            _dot_nt(q_all[:, h * HEAD_DIM:(h + 1) * HEAD_DIM], wuk_ref[h]) * (HEAD_DIM ** -0.5)).astype(bf)
    iw = iwt_ref[0]

    def score_body(jb0, nb, c):
        d_blk = _dot_nt(kidx_ref[pl.ds(blk(jb0), nb * QB), :], qidx_ref[...])
        for sb in range(nb):
            off = blk(jb0 + sb)
            d_all = d_blk[sb * QB:(sb + 1) * QB, :]
            acc = jnp.maximum(d_all[:, 0:QB], 0.0) * iw[0:1, :]
            for h in range(1, N_IDX_HEADS):
                acc = acc + jnp.maximum(d_all[:, h * QB:(h + 1) * QB], 0.0) * iw[h:h + 1, :]
            score_ref[pl.ds(off, QB), :] = jnp.where(s_loc + off <= t_glob, acc + 0.0, F32_LOWEST)
        return c

    block_loop(score_body, 0)

    def count_where(pred):
        def body(jb0, nb, acc):
            for sb in range(nb):
                off = blk(jb0 + sb)
                acc = acc + _colsum8(jnp.where(pred(score_ref[pl.ds(off, QB), :], off), 1.0, 0.0))
            return acc
        acc = block_loop(body, jnp.zeros((SUBLANES, QB), f32))
        return jnp.sum(acc, axis=0, keepdims=True)

    kf = float(k_sel)

    def search():
        c0 = count_where(lambda sc, off: sc >= 0.0)
        cand0 = jnp.where(c0 >= kf, jnp.int32(0), jnp.int32(-2 ** 31))

        def bit_body(it, cand):
            trial = cand + lax.shift_left(jnp.int32(1), 30 - it)
            tf = _key_to_f32(trial)
            cnt = count_where(lambda sc, off: sc >= tf)
            return jnp.where(cnt >= kf, trial, cand)

        cand = lax.fori_loop(0, 31, bit_body, cand0)
        thr = _key_to_f32(cand)
        n_gt = count_where(lambda sc, off: sc > thr)
        n_eq = count_where(lambda sc, off: sc == thr)
        need = kf - n_gt

        def tie_search():
            def tbody(it, xcut):
                trial = xcut + lax.shift_left(jnp.int32(1), idx_bits - 1 - it)
                cnt = count_where(lambda sc, off: (sc == thr) & (s_loc + off < trial))
                return jnp.where(cnt < need, trial, xcut)
            return lax.fori_loop(0, idx_bits, tbody, jnp.zeros((1, QB), jnp.int32))

        any_extra = jnp.max(n_eq - need) > 0.0
        xcut = lax.cond(any_extra, tie_search, lambda: jnp.full((1, QB), 2 ** idx_bits - 1, jnp.int32))
        return thr, xcut

    def no_search():
        return jnp.full((1, QB), F32_LOWEST, f32), jnp.full((1, QB), 2 ** idx_bits - 1, jnp.int32)

    thr, xcut = lax.cond((i + 1) * QB > k_sel, search, no_search)

    def mask_body(jb0, nb, c):
        for sb in range(nb):
            off = blk(jb0 + sb)
            sc = score_ref[pl.ds(off, QB), :]
            s_glob = s_loc + off
            keep = ((sc > thr) | ((sc == thr) & (s_glob <= xcut))) & (s_glob <= t_glob)
            mask_ref[pl.ds(off, QB), :] = jnp.where(keep, 0.0, -jnp.inf)
        return c

    block_loop(mask_body, 0)

    def p1_body(jb0, nb, m8):
        m8 = list(m8)
        lg_blk = _dot_nt(ckv_ref[pl.ds(blk(jb0), nb * QB), :], qlat_ref[...])
        for sb in range(nb):
            off = blk(jb0 + sb)
            lg = lg_blk[sb * QB:(sb + 1) * QB, :]
            msk = mask_ref[pl.ds(off, QB), :]
            bsel = jnp.clip(jb0 + sb - i + 2, 0, 2)
            for h in range(N_HEADS_A):
                lgh = lg[:, h * QB:(h + 1) * QB] + bias_ref[bsel, h] + msk
                logit_ref[pl.ds(off, QB), h * QB:(h + 1) * QB] = lgh
                m8[h] = jnp.maximum(m8[h], _colmax8(lgh))
        return tuple(m8)

    m8 = block_loop(p1_body, tuple(jnp.full((SUBLANES, QB), -jnp.inf, f32) for _ in range(N_HEADS_A)))
    m_row = [jnp.max(m, axis=0, keepdims=True) for m in m8]

    acc_ref[...] = jnp.zeros_like(acc_ref)

    def p2_body(jb0, nb, l8):
        l8 = list(l8)
        off = blk(jb0)
        rows = nb * QB
        ps = []
        for h in range(N_HEADS_A):
            p = jnp.exp(logit_ref[pl.ds(off, rows), h * QB:(h + 1) * QB] - m_row[h])
            l8[h] = l8[h] + jnp.sum(p.reshape(rows // SUBLANES, SUBLANES, QB), axis=0)
            ps.append(p.astype(bf))
        acc_ref[...] += _dot(ckvt_ref[0, :, pl.ds(off, rows)], jnp.concatenate(ps, axis=1))
        return tuple(l8)

    l8 = block_loop(p2_body, tuple(jnp.zeros((SUBLANES, QB), f32) for _ in range(N_HEADS_A)))

    outs = []
    for h in range(N_HEADS_A):
        l_row = jnp.sum(l8[h], axis=0, keepdims=True)
        o_lat_t = (acc_ref[:, h * QB:(h + 1) * QB] / l_row).astype(bf)
        outs.append(_dot(wuvt_ref[h], o_lat_t))
    o_ref[...] = jnp.concatenate(outs, axis=0).T.astype(o_ref.dtype)


def _dsa(cq, iwt, kidx, ckv, ckvt, w_qidx, w_uq, w_uk_h, w_uvt_h, bias_tiles, B, S):
    T = cq.shape[0]
    assert S % QB == 0 and QB >= REL_MAX_DIST
    nq = S // QB
    k_sel = min(TOPK_MAX, S // 4)
    idx_bits = max(1, (S - 1).bit_length())
    c2 = lambda b, i: (0, 0)
    c3 = lambda b, i: (0, 0, 0)
    return pl.pallas_call(
        functools.partial(_dsa_kernel, k_sel=k_sel, idx_bits=idx_bits),
        grid=(B, nq),
        in_specs=[
            pl.BlockSpec((QB, Q_RANK), lambda b, i: (b * nq + i, 0)),
            pl.BlockSpec((1, N_IDX_HEADS, QB), lambda b, i: (b, 0, i)),
            pl.BlockSpec((S, IDX_DIM), lambda b, i: (b, 0)),
            pl.BlockSpec((S, KV_RANK), lambda b, i: (b, 0)),
            pl.BlockSpec((1, KV_RANK, S), lambda b, i: (b, 0, 0)),
            pl.BlockSpec(w_qidx.shape, c2),
            pl.BlockSpec(w_uq.shape, c2),
            pl.BlockSpec(w_uk_h.shape, c3),
            pl.BlockSpec(w_uvt_h.shape, c3),
            pl.BlockSpec(bias_tiles.shape, lambda b, i: (0, 0, 0, 0)),
        ],
        out_specs=pl.BlockSpec((QB, MIX_A), lambda b, i: (b * nq + i, 0)),
        out_shape=jax.ShapeDtypeStruct((T, MIX_A), MXU_DTYPE),
        scratch_shapes=[
            pltpu.VMEM((N_IDX_HEADS * QB, IDX_DIM), MXU_DTYPE),
            pltpu.VMEM((N_HEADS_A * QB, KV_RANK), MXU_DTYPE),
            pltpu.VMEM((S, QB), jnp.float32),
            pltpu.VMEM((S, QB), jnp.float32),
            pltpu.VMEM((S, N_HEADS_A * QB), jnp.float32),
            pltpu.VMEM((KV_RANK, N_HEADS_A * QB), jnp.float32),
        ],
        compiler_params=_cparams(("arbitrary", "arbitrary")),
        name="dsa",
    )(cq, iwt, kidx, ckv, ckvt, w_qidx, w_uq, w_uk_h, w_uvt_h, bias_tiles)


def _layer_norm(xf, g, b):
    mu = jnp.mean(xf, axis=-1, keepdims=True)
    xc = xf - mu
    var = jnp.mean(xc * xc, axis=-1, keepdims=True)
    return xc * lax.rsqrt(var + LN_EPS) * g + b


def _rank_rows(v, n):
    ri = lax.broadcasted_iota(jnp.int32, v.shape, 0)
    rank = jnp.zeros(v.shape, jnp.float32)
    for r2 in range(n):
        row = v[r2:r2 + 1, :]
        beats = (row > v) | ((row == v) & (ri > r2))
        rank = rank + jnp.where(beats, 1.0, 0.0)
    return rank


def _pack_factor():
    return 4 // jnp.dtype(MXU_DTYPE).itemsize


def _pack_rows(x):
    if _pack_factor() == 1:
        return pltpu.bitcast(x, jnp.int32)
    half = x.shape[1] // 2
    b = pltpu.bitcast(x.astype(MXU_DTYPE).astype(jnp.float32), jnp.int32)
    return b[:, half:] | lax.shift_right_logical(b[:, :half], jnp.int32(16))


_HIGH_HALF = -(1 << 16)


def _unpack_rows_f32(p):
    if _pack_factor() == 1:
        return [pltpu.bitcast(p, jnp.float32)]
    lo = pltpu.bitcast(lax.shift_left(p, jnp.int32(16)), jnp.float32)
    hi = pltpu.bitcast(p & jnp.int32(_HIGH_HALF), jnp.float32)
    return [lo, hi]


def _unpack_rows(p):
    return [v.astype(MXU_DTYPE) for v in _unpack_rows_f32(p)]


def _mix_router_kernel(x_ref, ya_ref, yb_ref, yc_ref, wo_ref, g_ref, b_ref, wrt_ref, rb_ref, exp_ref,
                       x1_ref, x1p_ref, sel_ref, w_ref, pos_ref, cnt_ref, base_ref, *, tm):
    step = pl.program_id(0)
    f32 = jnp.float32

    @pl.when(step == 0)
    def _():
        base_ref[...] = jnp.zeros_like(base_ref)

    mix = _dot(ya_ref[...], wo_ref[0:MIX_A, :])
    mix = mix + _dot(yb_ref[...], wo_ref[MIX_A:MIX_A + CONV_CH, :])
    mix = mix + _dot(yc_ref[...], wo_ref[MIX_A + CONV_CH:, :])
    x1 = _layer_norm(ALPHA * x_ref[...] + mix, g_ref[...], b_ref[...])
    x1_ref[...] = x1
    x1p_ref[...] = _pack_rows(x1)

    lg = lax.dot_general(wrt_ref[...], x1, _NT, precision=lax.Precision.HIGHEST, preferred_element_type=f32)
    s = 1.0 / (1.0 + jnp.exp(-lg))
    sc = s + rb_ref[...]

    g3 = sc.reshape(N_GROUPS, GROUP_SIZE, tm)
    m1 = jnp.max(g3, axis=1, keepdims=True)
    is_m1 = g3 == m1
    n_m1 = jnp.sum(jnp.where(is_m1, 1.0, 0.0), axis=1, keepdims=True)
    m2 = jnp.max(jnp.where(is_m1, -jnp.inf, g3), axis=1, keepdims=True)
    gscore = (m1 + jnp.where(n_m1 > 1.0, m1, m2)).reshape(N_GROUPS, tm)
    gsel = jnp.where(_rank_rows(gscore, N_GROUPS) < float(TOPK_GROUPS), 1.0, 0.0)
    emask = _dot(exp_ref[...], gsel.astype(MXU_DTYPE)) > 0.5
    masked = jnp.where(emask, sc, -jnp.inf)
    sel = (_rank_rows(masked, N_EXPERTS) < float(TOP_K)) & emask
    self_ = jnp.where(sel, 1.0, 0.0)
    top_s = jnp.where(sel, s, 0.0)
    w = top_s / jnp.sum(top_s, axis=0, keepdims=True) * ROUTED_SCALE

    t_r = lax.broadcasted_iota(jnp.int32, (tm, tm), 0)
    t_c = lax.broadcasted_iota(jnp.int32, (tm, tm), 1)
    upper = jnp.where(t_r < t_c, 1.0, 0.0).astype(MXU_DTYPE)
    pref = _dot(self_.astype(MXU_DTYPE), upper)
    base = base_ref[...]
    sel_ref[...] = self_
    w_ref[...] = w
    pos_ref[...] = base + pref
    base = base + jnp.sum(self_, axis=1, keepdims=True)
    base_ref[...] = base
    cnt_ref[...] = jnp.broadcast_to(base, cnt_ref.shape)


def _mix_router(x2, ya, yb, yc, w_out, ln_g, ln_b, w_router_t, router_bias, tm):
    T, D = x2.shape
    E = N_EXPERTS
    expand = (jnp.arange(E)[:, None] // GROUP_SIZE == jnp.arange(N_GROUPS)[None, :]).astype(MXU_DTYPE)
    row = lambda i: (i, 0)
    col = lambda i: (0, i)
    c2 = lambda i: (0, 0)
    f32 = jnp.float32
    return pl.pallas_call(
        functools.partial(_mix_router_kernel, tm=tm),
        grid=(T // tm,),
        in_specs=[
            pl.BlockSpec((tm, D), row),
            pl.BlockSpec((tm, MIX_A), row),
            pl.BlockSpec((tm, CONV_CH), row),
            pl.BlockSpec((tm, MIX_C), row),
            pl.BlockSpec(w_out.shape, c2),
            pl.BlockSpec((1, D), c2),
            pl.BlockSpec((1, D), c2),
            pl.BlockSpec((E, D), c2),
            pl.BlockSpec((E, 1), c2),
            pl.BlockSpec((E, N_GROUPS), c2),
        ],
        out_specs=[
            pl.BlockSpec((tm, D), row),
            pl.BlockSpec((tm, D // _pack_factor()), row),
            pl.BlockSpec((E, tm), col),
            pl.BlockSpec((E, tm), col),
            pl.BlockSpec((E, tm), col),
            pl.BlockSpec((E, LANES), c2),
        ],
        out_shape=[
            jax.ShapeDtypeStruct((T, D), f32),
            jax.ShapeDtypeStruct((T, D // _pack_factor()), jnp.int32),
            jax.ShapeDtypeStruct((E, T), f32),
            jax.ShapeDtypeStruct((E, T), f32),
            jax.ShapeDtypeStruct((E, T), f32),
            jax.ShapeDtypeStruct((E, LANES), f32),
        ],
        scratch_shapes=[pltpu.VMEM((E, 1), f32)],
        compiler_params=_cparams(("arbitrary",)),
        name="mix_router",
    )(x2, ya, yb, yc, w_out, ln_g, ln_b, w_router_t, router_bias, expand)


def _compact_kernel(sel_ref, w_ref, pos_ref, pstart_ref, low_ref, dest_ref, wk_ref):
    sel = sel_ref[...]
    on = sel > 0.5
    rank = _dot(low_ref[...], sel.astype(MXU_DTYPE))
    row = pstart_ref[...] + pos_ref[...]
    w = w_ref[...]
    dests, ws = [], []
    for k in range(TOP_K):
        m = on & (rank == float(k))
        dests.append(jnp.sum(jnp.where(m, row, 0.0), axis=0, keepdims=True))
        ws.append(jnp.sum(jnp.where(m, w, 0.0), axis=0, keepdims=True))
    dest_ref[...] = jnp.concatenate(dests, axis=0).astype(jnp.int32)
    wk_ref[...] = jnp.concatenate(ws, axis=0)


def _compact(sel_t, w_t, pos_t, pad_start, tm):
    E, T = sel_t.shape
    lower = (jnp.arange(E)[None, :] < jnp.arange(E)[:, None]).astype(MXU_DTYPE)
    col = lambda i: (0, i)
    c2 = lambda i: (0, 0)
    return pl.pallas_call(
        _compact_kernel,
        grid=(T // tm,),
        in_specs=[pl.BlockSpec((E, tm), col), pl.BlockSpec((E, tm), col), pl.BlockSpec((E, tm), col),
                  pl.BlockSpec((E, 1), c2), pl.BlockSpec((E, E), c2)],
        out_specs=[pl.BlockSpec((TOP_K, tm), col), pl.BlockSpec((TOP_K, tm), col)],
        out_shape=[jax.ShapeDtypeStruct((TOP_K, T), jnp.int32), jax.ShapeDtypeStruct((TOP_K, T), jnp.float32)],
        compiler_params=_cparams(("arbitrary",)),
        name="route_compact",
    )(sel_t, w_t, pos_t, pad_start, lower)


def _row_copy(src, s, dst, d, sem):
    return pltpu.make_async_copy(src.at[pl.ds(s, 1)], dst.at[pl.ds(d, 1)], sem)


def _dispatch_kernel(flo_ref, fhi_ref, dest_ref, x_ref, xs_hbm, zero_ref, sem, zsem, *, td):
    step = pl.program_id(0)

    @pl.when(step == 0)
    def _():
        zero_ref[...] = jnp.zeros_like(zero_ref)

        def per_expert(fn):
            def ebody(e, c):
                lax.fori_loop(flo_ref[e], fhi_ref[e], lambda r, c2: (fn(r), c2)[1], 0)
                return c
            lax.fori_loop(0, N_EXPERTS, ebody, 0)

        per_expert(lambda r: _row_copy(zero_ref, 0, xs_hbm, r, zsem).start())
        per_expert(lambda r: _row_copy(zero_ref, 0, xs_hbm, r, zsem).wait())

    def issue(r, c):
        for k in range(TOP_K):
            _row_copy(x_ref, r, xs_hbm, dest_ref[k, r], sem).start()
        return c

    def drain(r, c):
        for k in range(TOP_K):
            _row_copy(x_ref, r, xs_hbm, dest_ref[k, r], sem).wait()
        return c

    lax.fori_loop(0, td, issue, 0)
    lax.fori_loop(0, td, drain, 0)


def _dispatch(dest_t, x1p, fill_lo, fill_hi, n_rows, td):
    T, W = x1p.shape
    return pl.pallas_call(
        functools.partial(_dispatch_kernel, td=td),
        grid_spec=pltpu.PrefetchScalarGridSpec(
            num_scalar_prefetch=2,
            grid=(T // td,),
            in_specs=[
                pl.BlockSpec((TOP_K, td), lambda i, lo, hi: (0, i), memory_space=pltpu.SMEM),
                pl.BlockSpec((td, W), lambda i, lo, hi: (i, 0)),
            ],
            out_specs=pl.BlockSpec(memory_space=pl.ANY),
            scratch_shapes=[pltpu.VMEM((SUBLANES, W), x1p.dtype),
                            pltpu.SemaphoreType.DMA, pltpu.SemaphoreType.DMA],
        ),
        out_shape=jax.ShapeDtypeStruct((n_rows, W), x1p.dtype),
        compiler_params=_cparams(("arbitrary",)),
        name="dispatch",
    )(fill_lo, fill_hi, dest_t, x1p)


def _silu(g):
    return g / (1.0 + jnp.exp(-g))


def _expert_kernel(be_ref, nv_ref, nu_ref, xs_ref, wg_ref, wu_ref, wd_ref, ys_ref, wgb_ref, wub_ref, wdb_ref):
    i = pl.program_id(0)

    @pl.when((i == 0) | (be_ref[i] != be_ref[jnp.maximum(i - 1, 0)]))
    def _():
        wgb_ref[...] = wg_ref[0].astype(MXU_DTYPE)
        wub_ref[...] = wu_ref[0].astype(MXU_DTYPE)
        wdb_ref[...] = wd_ref[0].astype(MXU_DTYPE)

    @pl.when(i < nu_ref[0])
    def _():
        live = lax.broadcasted_iota(jnp.int32, (ROW_BLOCK, 1), 0) < nv_ref[i]
        parts = [jnp.where(live, v, jnp.zeros_like(v)) for v in _unpack_rows(xs_ref[...])]
        dk = wgb_ref.shape[0] // len(parts)

        def proj(w_ref):
            acc = _dot(parts[0], w_ref[0:dk, :])
            for n in range(1, len(parts)):
                acc = acc + _dot(parts[n], w_ref[n * dk:(n + 1) * dk, :])
            return acc

        a = (_silu(proj(wgb_ref)) * proj(wub_ref)).astype(MXU_DTYPE)
        ys_ref[...] = _pack_rows(_dot(a, wdb_ref[...]))


def _experts(xs, block_e, block_valid, n_used, w_gate, w_up, w_down):
    n_rows, W = xs.shape
    D = w_gate.shape[1]
    n_blocks = n_rows // ROW_BLOCK
    blk = lambda i, be, nv, nu: (jnp.minimum(i, nu[0] - 1), 0)
    wsel = lambda i, be, nv, nu: (be[i], 0, 0)
    return pl.pallas_call(
        _expert_kernel,
        grid_spec=pltpu.PrefetchScalarGridSpec(
            num_scalar_prefetch=3,
            grid=(n_blocks,),
            in_specs=[
                pl.BlockSpec((ROW_BLOCK, W), blk),
                pl.BlockSpec((1, D, D_EXPERT), wsel),
                pl.BlockSpec((1, D, D_EXPERT), wsel),
                pl.BlockSpec((1, D_EXPERT, D), wsel),
            ],
            out_specs=pl.BlockSpec((ROW_BLOCK, W), blk),
            scratch_shapes=[pltpu.VMEM((D, D_EXPERT), MXU_DTYPE), pltpu.VMEM((D, D_EXPERT), MXU_DTYPE),
                            pltpu.VMEM((D_EXPERT, D), MXU_DTYPE)],
        ),
        out_shape=jax.ShapeDtypeStruct((n_rows, W), xs.dtype),
        compiler_params=_cparams(("arbitrary",)),
        name="experts",
    )(block_e, block_valid, n_used, xs, w_gate, w_up, w_down)


SC_CORES = 2
SC_SUBCORES = 16
SC_GATHER_ROWS = 64
COMBINE_CHUNKS = 4


def _sc_gather_rows(table, idx):
    n = idx.shape[0]
    w = table.shape[1]
    n_workers = SC_CORES * SC_SUBCORES
    per_worker = n // n_workers
    assert n % n_workers == 0 and per_worker % SC_GATHER_ROWS == 0
    mesh = plsc.VectorSubcoreMesh(core_axis_name="c", subcore_axis_name="s")

    @functools.partial(
        pl.kernel, mesh=mesh,
        out_type=jax.ShapeDtypeStruct((n, w), table.dtype),
        scratch_types=[
            pltpu.VMEM((2, SC_GATHER_ROWS), jnp.int32),
            pltpu.VMEM((2, SC_GATHER_ROWS, w), table.dtype),
            pltpu.SemaphoreType.DMA((2,)),
        ],
        name="sc_gather_rows",
    )
    def gather(table_hbm, idx_hbm, out_hbm, idx_v, rows_v, sem):
        wid = lax.axis_index("s") * SC_CORES + lax.axis_index("c")
        base = wid * per_worker
        n_steps = per_worker // SC_GATHER_ROWS

        def gather_copy(slot):
            return pltpu.make_async_copy(table_hbm.at[idx_v.at[slot]], rows_v.at[slot], sem.at[slot])

        def start(step, slot):
            pltpu.sync_copy(idx_hbm.at[pl.ds(base + step * SC_GATHER_ROWS, SC_GATHER_ROWS)], idx_v.at[slot])
            gather_copy(slot).start()

        start(0, 0)

        @pl.loop(0, n_steps, step=2)
        def _(g):
            for slot in range(2):
                step = g + slot

                @pl.when(step + 1 < n_steps)
                def _():
                    start(step + 1, 1 - slot)

                gather_copy(slot).wait()
                pltpu.sync_copy(rows_v.at[slot], out_hbm.at[pl.ds(base + step * SC_GATHER_ROWS, SC_GATHER_ROWS)])

    return gather(table, idx)


SC_SCATTER_ROWS = 64


def _sc_scatter_rows(rows, idx3, n_out):
    n_src, w = rows.shape
    n_chunks, n_dst, batch = idx3.shape
    n_workers = SC_CORES * SC_SUBCORES
    assert batch == SC_SCATTER_ROWS and n_chunks * batch == n_src and n_chunks % (2 * n_workers) == 0
    per_worker = n_chunks // n_workers
    mesh = plsc.VectorSubcoreMesh(core_axis_name="c", subcore_axis_name="s")

    @functools.partial(
        pl.kernel, mesh=mesh,
        out_type=jax.ShapeDtypeStruct((n_out, w), rows.dtype),
        scratch_types=[
            pltpu.VMEM((2, n_dst, batch), jnp.int32),
            pltpu.VMEM((2, batch, w), rows.dtype),
            pltpu.SemaphoreType.DMA((2,)),
            pltpu.SemaphoreType.DMA,
        ],
        name="sc_scatter_rows",
    )
    def scatter(rows_hbm, idx_hbm, out_hbm, idx_v, rows_v, load_sem, store_sem):
        wid = lax.axis_index("s") * SC_CORES + lax.axis_index("c")

        def load_copy(step, slot):
            c = wid * per_worker + step
            return pltpu.make_async_copy(rows_hbm.at[pl.ds(c * batch, batch)], rows_v.at[slot], load_sem.at[slot])

        def load(step, slot):
            pltpu.sync_copy(idx_hbm.at[wid * per_worker + step], idx_v.at[slot])
            load_copy(step, slot).start()

        def store_copy(slot, k):
            return pltpu.make_async_copy(rows_v.at[slot], out_hbm.at[idx_v.at[slot].at[k]], store_sem)

        load(0, 0)

        @pl.loop(0, per_worker, step=2)
        def _(g):
            for slot in range(2):
                step = g + slot

                @pl.when(step + 1 < per_worker)
                def _():
                    load(step + 1, 1 - slot)

                load_copy(step, slot).wait()
                for k in range(n_dst):
                    store_copy(slot, k).start()
                for k in range(n_dst):
                    store_copy(slot, k).wait()

    return scatter(rows, idx3)


def _shared_kernel(x1_ref, wsg_ref, wsu_ref, wsd_ref, o_ref):
    xb = x1_ref[...].astype(MXU_DTYPE)
    a = (_silu(_dot(xb, wsg_ref[...])) * _dot(xb, wsu_ref[...])).astype(MXU_DTYPE)
    o_ref[...] = _dot(a, wsd_ref[...])


def _shared_expert(x1, w_sg, w_su, w_sd, tm):
    T, D = x1.shape
    row = lambda i: (i, 0)
    c2 = lambda i: (0, 0)
    return pl.pallas_call(
        _shared_kernel,
        grid=(T // tm,),
        in_specs=[pl.BlockSpec((tm, D), row), pl.BlockSpec(w_sg.shape, c2), pl.BlockSpec(w_su.shape, c2),
                  pl.BlockSpec(w_sd.shape, c2)],
        out_specs=pl.BlockSpec((tm, D), row),
        out_shape=jax.ShapeDtypeStruct((T, D), jnp.float32),
        compiler_params=_cparams(("arbitrary",)),
        name="shared_expert",
    )(x1, w_sg, w_su, w_sd)


def _combine2_kernel(wk_ref, x1_ref, g_ref_rows, wsg_ref, wsu_ref, wsd_ref, g_ref, b_ref, o_ref):
    x1 = x1_ref[...]
    xb = x1.astype(MXU_DTYPE)
    a = (_silu(_dot(xb, wsg_ref[...])) * _dot(xb, wsu_ref[...])).astype(MXU_DTYPE)
    shared = _dot(a, wsd_ref[...])
    wk = wk_ref[...].T
    groups = [wk[:, 0:1] * v for v in _unpack_rows_f32(g_ref_rows[0])]
    for k in range(1, TOP_K):
        groups = [g + wk[:, k:k + 1] * v for g, v in zip(groups, _unpack_rows_f32(g_ref_rows[k]))]
    routed = jnp.concatenate(groups, axis=1)
    o_ref[...] = _layer_norm(ALPHA * x1 + (routed + shared), g_ref[...], b_ref[...])


def _combine2_kernel_into(wk_ref, x1_ref, g_ref_rows, wsg_ref, wsu_ref, wsd_ref, g_ref, b_ref, prev_ref, o_ref):
    del prev_ref
    _combine2_kernel(wk_ref, x1_ref, g_ref_rows, wsg_ref, wsu_ref, wsd_ref, g_ref, b_ref, o_ref)


def _combine2(wk_t, x1, gathered, w_sg, w_su, w_sd, ln_g, ln_b, tc, chunk, prev):
    T, D = x1.shape
    _, t_chunk, W = gathered.shape
    base = chunk * (t_chunk // tc)
    row = lambda i: (base + i, 0)
    c2 = lambda i: (0, 0)
    in_specs = [
        pl.BlockSpec((TOP_K, tc), lambda i: (0, base + i)),
        pl.BlockSpec((tc, D), row),
        pl.BlockSpec((TOP_K, tc, W), lambda i: (0, i, 0)),
        pl.BlockSpec(w_sg.shape, c2),
        pl.BlockSpec(w_su.shape, c2),
        pl.BlockSpec(w_sd.shape, c2),
        pl.BlockSpec((1, D), c2),
        pl.BlockSpec((1, D), c2),
    ]
    args = [wk_t, x1, gathered, w_sg, w_su, w_sd, ln_g, ln_b]
    if prev is None:
        body, aliases = _combine2_kernel, {}
    else:
        body, aliases = _combine2_kernel_into, {len(args): 0}
        in_specs.append(pl.BlockSpec(memory_space=pl.ANY))
        args.append(prev)
    return pl.pallas_call(
        body,
        grid=(t_chunk // tc,),
        in_specs=in_specs,
        out_specs=pl.BlockSpec((tc, D), row),
        out_shape=jax.ShapeDtypeStruct((T, D), jnp.float32),
        input_output_aliases=aliases,
        compiler_params=_cparams(("arbitrary",)),
        name="combine",
    )(*args)


def _combine_kernel(dest_ref, wk_ref, x1_ref, ys_hbm, wsg_ref, wsu_ref, wsd_ref, g_ref, b_ref,
                    o_ref, buf_ref, sem, *, tc):
    def issue(r, c):
        for k in range(TOP_K):
            _row_copy(ys_hbm, dest_ref[k, r], buf_ref.at[k], r, sem).start()
        return c

    def drain(r, c):
        for k in range(TOP_K):
            _row_copy(ys_hbm, dest_ref[k, r], buf_ref.at[k], r, sem).wait()
        return c

    lax.fori_loop(0, tc, issue, 0)
    x1 = x1_ref[...]
    xb = x1.astype(MXU_DTYPE)
    a = (_silu(_dot(xb, wsg_ref[...])) * _dot(xb, wsu_ref[...])).astype(MXU_DTYPE)
    shared = _dot(a, wsd_ref[...])
    lax.fori_loop(0, tc, drain, 0)
    wk = wk_ref[...]
    groups = [wk[:, 0:1] * v for v in _unpack_rows_f32(buf_ref[0])]
    for k in range(1, TOP_K):
        groups = [g + wk[:, k:k + 1] * v for g, v in zip(groups, _unpack_rows_f32(buf_ref[k]))]
    routed = jnp.concatenate(groups, axis=1)
    o_ref[...] = _layer_norm(ALPHA * x1 + (routed + shared), g_ref[...], b_ref[...])


def _combine(dest_t, wk, x1, ys, w_sg, w_su, w_sd, ln_g, ln_b, tc):
    T, D = x1.shape
    row = lambda i: (i, 0)
    c2 = lambda i: (0, 0)
    return pl.pallas_call(
        functools.partial(_combine_kernel, tc=tc),
        grid=(T // tc,),
        in_specs=[
            pl.BlockSpec((TOP_K, tc), lambda i: (0, i), memory_space=pltpu.SMEM),
            pl.BlockSpec((tc, TOP_K), row),
            pl.BlockSpec((tc, D), row),
            pl.BlockSpec(memory_space=pl.ANY),
            pl.BlockSpec(w_sg.shape, c2),
            pl.BlockSpec(w_su.shape, c2),
            pl.BlockSpec(w_sd.shape, c2),
            pl.BlockSpec((1, D), c2),
            pl.BlockSpec((1, D), c2),
        ],
        out_specs=pl.BlockSpec((tc, D), row),
        out_shape=jax.ShapeDtypeStruct((T, D), jnp.float32),
        scratch_shapes=[pltpu.VMEM((TOP_K, tc, ys.shape[1]), ys.dtype), pltpu.SemaphoreType.DMA],
        compiler_params=_cparams(("arbitrary",)),
        name="combine",
    )(dest_t, wk, x1, ys, w_sg, w_su, w_sd, ln_g, ln_b)


def _split_w_in(w_in):
    bf = MXU_DTYPE
    o_kv = Q_RANK
    o_ki = o_kv + KV_RANK
    o_iw = o_ki + IDX_DIM
    o_rest = o_iw + N_IDX_HEADS
    w_main = jnp.concatenate([w_in[:, :o_ki], w_in[:, o_rest:]], axis=1).astype(bf)
    w_small = jnp.pad(w_in[:, o_ki:o_rest], ((0, 0), (0, LANES - IDX_DIM - N_IDX_HEADS))).astype(bf)
    return w_main, w_small


def _stages(x, mem, w_in, q_norm_g, kv_norm_g, w_uq, w_uk, w_uv, w_qidx, rel_bias, conv_w, w_mem_k, w_mem_v, w_out, ln1_g, ln1_b, w_router, router_bias, w_e_gate, w_e_up, w_e_down, w_s_gate, w_s_up, w_s_down, ln2_g, ln2_b, upto=None):
    B, S, D = x.shape
    T = B * S
    bf = MXU_DTYPE
    l = 0
    res = {}
    x2 = x.reshape(T, D)
    w_main, w_small = _split_w_in(w_in[l])
    cq, ckv, ckvt, kidx, iwt, yb, yc = _proj(
        x2, mem, w_main, w_small, q_norm_g[l].reshape(1, -1), kv_norm_g[l].reshape(1, -1), conv_w[l],
        w_mem_k[l].astype(bf), w_mem_v[l].astype(bf), B, S, tm=min(512, S))
    res.update(c_q=cq, c_kv=ckv, k_idx=kidx, y_b=yb, y_c=yc,
               idx_w=jnp.swapaxes(iwt, 1, 2) / (N_IDX_HEADS ** -0.5 * IDX_DIM ** -0.5))
    if upto == "proj":
        return res
    bias_t = _bias_tiles(rel_bias)
    ya = _dsa(cq, iwt, kidx, ckv, ckvt,
              w_qidx[l].reshape(Q_RANK, -1).astype(bf), w_uq[l].reshape(Q_RANK, -1).astype(bf),
              jnp.transpose(w_uk[l], (1, 0, 2)).astype(bf), jnp.transpose(w_uv[l], (1, 2, 0)).astype(bf),
              bias_t, B, S)
    res.update(y_a=ya)
    if upto == "dsa":
        return res

    x1, x1p, sel_t, w_t, pos_t, cnt = _mix_router(
        x2, ya, yb, yc, w_out[l].astype(bf), ln1_g[l].reshape(1, -1), ln1_b[l].reshape(1, -1),
        w_router[l].T, router_bias[l].reshape(-1, 1), tm=min(512, T))
    res.update(x1=x1)

    counts = cnt[:, 0].astype(jnp.int32)
    padded = (counts + ROW_BLOCK - 1) // ROW_BLOCK * ROW_BLOCK
    pad_end = jnp.cumsum(padded)
    pad_start = pad_end - padded
    n_blocks = -(-(T * TOP_K) // ROW_BLOCK) + N_EXPERTS
    n_rows = n_blocks * ROW_BLOCK
    block_start = jnp.arange(n_blocks, dtype=jnp.int32) * ROW_BLOCK
    block_e = jnp.minimum(jnp.sum((pad_end[None, :] <= block_start[:, None]).astype(jnp.int32), axis=1),
                          N_EXPERTS - 1)
    n_used = (pad_end[-1:] // ROW_BLOCK).astype(jnp.int32)

    dest_t, wk_t = _compact(sel_t, w_t, pos_t, pad_start.astype(jnp.float32).reshape(-1, 1), tm=min(512, T))
    block_valid = jnp.clip((pad_start + counts)[block_e] - block_start, 0, ROW_BLOCK).astype(jnp.int32)
    bt = SC_SCATTER_ROWS
    idx3 = jnp.transpose(dest_t.reshape(TOP_K, T // bt, bt), (1, 0, 2))
    xs = _sc_scatter_rows(x1p, idx3, n_rows)
    ys = _experts(xs, block_e, block_valid, n_used, w_e_gate[l], w_e_up[l], w_e_down[l])
    n_chunks = COMBINE_CHUNKS if T % (COMBINE_CHUNKS * 256) == 0 else 1
    t_chunk = T // n_chunks
    out = None
    for c in range(n_chunks):
        idx_c = dest_t[:, c * t_chunk:(c + 1) * t_chunk].reshape(-1)
        gathered = _sc_gather_rows(ys, idx_c).reshape(TOP_K, t_chunk, -1)
        out = _combine2(wk_t, x1, gathered, w_s_gate[l].astype(bf), w_s_up[l].astype(bf), w_s_down[l].astype(bf),
                        ln2_g[l].reshape(1, -1), ln2_b[l].reshape(1, -1), tc=min(256, t_chunk), chunk=c, prev=out)
    res.update(out=out.reshape(B, S, D))
    return res


def kernel(x, mem, w_in, q_norm_g, kv_norm_g, w_uq, w_uk, w_uv, w_qidx, rel_bias, conv_w, w_mem_k, w_mem_v, w_out, ln1_g, ln1_b, w_router, router_bias, w_e_gate, w_e_up, w_e_down, w_s_gate, w_s_up, w_s_down, ln2_g, ln2_b):
    return _stages(x, mem, w_in, q_norm_g, kv_norm_g, w_uq, w_uk, w_uv, w_qidx, rel_bias, conv_w, w_mem_k, w_mem_v, w_out, ln1_g, ln1_b, w_router, router_bias, w_e_gate, w_e_up, w_e_down, w_s_gate, w_s_up, w_s_down, ln2_g, ln2_b)["out"]
```

```python
import functools
import math

import jax
import jax.numpy as jnp
from jax import lax
from jax.experimental import pallas as pl
from jax.experimental.pallas import tpu as pltpu
from jax.experimental.pallas import tpu_sc as plsc

N_HEADS_A = 8
HEAD_DIM = 64
Q_RANK = 256
KV_RANK = 128
N_IDX_HEADS = 8
IDX_DIM = 64
TOPK_MAX = 256
REL_BUCKETS = 32
REL_MAX_DIST = 128
CONV_CH = 256
CONV_WIDTH = 3
N_MEM_HEADS = 4
MIX_A = N_HEADS_A * HEAD_DIM
MIX_C = N_MEM_HEADS * HEAD_DIM
N_EXPERTS = 64
N_GROUPS = 8
GROUP_SIZE = N_EXPERTS // N_GROUPS
TOPK_GROUPS = 4
TOP_K = 8
D_EXPERT = 256
ROUTED_SCALE = 2.5
MOE_BLOCK = 256
DEPTH = 1
ALPHA = (2.0 * DEPTH) ** 0.25
LN_EPS = 1e-5
RMS_EPS = 1e-6

LANES = 128
SUBLANES = 8
QB = 128
F32_LOWEST = -3.4028234663852886e38
VMEM_LIMIT = 56 * 1024 * 1024
MXU_DTYPE = jnp.bfloat16
ROW_BLOCK = 512

_NT = (((1,), (1,)), ((), ()))


def _dot(a, b):
    return jnp.dot(a, b, preferred_element_type=jnp.float32)


def _dot_nt(a, b):
    return lax.dot_general(a, b, _NT, preferred_element_type=jnp.float32)


def _cparams(sem):
    return pltpu.CompilerParams(dimension_semantics=sem, vmem_limit_bytes=VMEM_LIMIT)


def _bias_kernel(rb_ref, o_ref):
    s = lax.broadcasted_iota(jnp.int32, (QB, QB), 0)
    t = lax.broadcasted_iota(jnp.int32, (QB, QB), 1)
    max_exact = REL_BUCKETS // 2
    for tile in range(3):
        n = jnp.maximum(t - s + (2 - tile) * QB, 0)
        nf = jnp.maximum(n.astype(jnp.float32), 1.0)
        large = max_exact + (jnp.log(nf / max_exact) / math.log(REL_MAX_DIST / max_exact)
                             * (REL_BUCKETS - max_exact)).astype(jnp.int32)
        large = jnp.minimum(large, REL_BUCKETS - 1)
        bucket = jnp.where(n < max_exact, n, large)
        for h in range(N_HEADS_A):
            acc = jnp.zeros((QB, QB), jnp.float32)
            for b in range(REL_BUCKETS):
                acc = jnp.where(bucket == b, rb_ref[b, h], acc)
            o_ref[tile, h] = acc


def _bias_tiles(rel_bias):
    return pl.pallas_call(
        _bias_kernel,
        in_specs=[pl.BlockSpec(memory_space=pltpu.SMEM)],
        out_specs=pl.BlockSpec(memory_space=pltpu.VMEM),
        out_shape=jax.ShapeDtypeStruct((3, N_HEADS_A, QB, QB), jnp.float32),
        name="bias_tiles",
    )(rel_bias)


_MAIN_COLS = Q_RANK + KV_RANK + 3 * CONV_CH + MIX_C


def _proj_kernel(x_ref, mem_ref, wm_ref, ws_ref, qg_ref, kvg_ref, cw_ref, wmk_ref, wmv_ref,
                 cq_ref, ckv_ref, ckvt_ref, kidx_ref, iwt_ref, yb_ref, yc_ref,
                 carry_ref, mk_ref, mv_ref, *, tm):
    si = pl.program_id(1)

    @pl.when(si == 0)
    def _():
        carry_ref[...] = jnp.zeros_like(carry_ref)
        mb = mem_ref[0].astype(MXU_DTYPE)
        mk_ref[...] = _dot(mb, wmk_ref[...]).astype(MXU_DTYPE)
        mv_ref[...] = _dot(mb, wmv_ref[...]).astype(MXU_DTYPE)

    xb = x_ref[...].astype(MXU_DTYPE)
    p = _dot(xb, wm_ref[...])
    small = _dot(xb, ws_ref[...])

    o = 0
    cq = p[:, o:o + Q_RANK]; o += Q_RANK
    ckv = p[:, o:o + KV_RANK]; o += KV_RANK
    g_b = p[:, o:o + CONV_CH]; o += CONV_CH
    g_c = p[:, o:o + CONV_CH]; o += CONV_CH
    h_c = p[:, o:o + CONV_CH]; o += CONV_CH
    q_mem = p[:, o:o + MIX_C]

    cq = cq * lax.rsqrt(jnp.mean(cq * cq, axis=-1, keepdims=True) + RMS_EPS) * qg_ref[...]
    ckv = ckv * lax.rsqrt(jnp.mean(ckv * ckv, axis=-1, keepdims=True) + RMS_EPS) * kvg_ref[...]
    cq_ref[...] = cq.astype(MXU_DTYPE)
    ckv_b = ckv.astype(MXU_DTYPE)
    ckv_ref[...] = ckv_b
    ckvt_ref[0] = ckv.T.astype(MXU_DTYPE)

    kidx_ref[...] = small[:, :IDX_DIM].astype(MXU_DTYPE)
    small_t = small.T
    iwt_ref[0] = small_t[IDX_DIM:IDX_DIM + N_IDX_HEADS, :] * (N_IDX_HEADS ** -0.5 * IDX_DIM ** -0.5)

    u = g_c * h_c
    rows = lax.broadcasted_iota(jnp.int32, (tm, 1), 0)
    c6 = carry_ref[SUBLANES - 2:SUBLANES - 1, :]
    c7 = carry_ref[SUBLANES - 1:SUBLANES, :]
    u1 = jnp.where(rows == 0, c7, pltpu.roll(u, 1, 0))
    u2 = jnp.where(rows == 0, c6, jnp.where(rows == 1, c7, pltpu.roll(u, 2, 0)))
    y = cw_ref[0:1, :] * u2
    y = y + cw_ref[1:2, :] * u1
    y = y + cw_ref[2:3, :] * u
    yb_ref[...] = (g_b * y).astype(MXU_DTYPE)
    carry_ref[...] = u[tm - SUBLANES:, :]

    qm = q_mem.astype(MXU_DTYPE)
    outs = []
    for h in range(N_MEM_HEADS):
        sl = slice(h * HEAD_DIM, (h + 1) * HEAD_DIM)
        lg = _dot_nt(qm[:, sl], mk_ref[:, sl]) * (HEAD_DIM ** -0.5)
        lg = lg - jnp.max(lg, axis=-1, keepdims=True)
        e = jnp.exp(lg)
        pr = e / jnp.sum(e, axis=-1, keepdims=True)
        outs.append(_dot(pr.astype(MXU_DTYPE), mv_ref[:, sl]))
    yc_ref[...] = jnp.concatenate(outs, axis=-1).astype(MXU_DTYPE)


def _proj(x2, mem, w_main, w_small, q_g, kv_g, conv_w, w_mk, w_mv, B, S, tm):
    T, D = x2.shape
    n_mem = mem.shape[1]
    ns = S // tm
    row = lambda b, s: (b * ns + s, 0)
    const2 = lambda b, s: (0, 0)
    bf = MXU_DTYPE
    return pl.pallas_call(
        functools.partial(_proj_kernel, tm=tm),
        grid=(B, ns),
        in_specs=[
            pl.BlockSpec((tm, D), row),
            pl.BlockSpec((1, n_mem, D), lambda b, s: (b, 0, 0)),
            pl.BlockSpec(w_main.shape, const2),
            pl.BlockSpec(w_small.shape, const2),
            pl.BlockSpec(q_g.shape, const2),
            pl.BlockSpec(kv_g.shape, const2),
            pl.BlockSpec(conv_w.shape, const2),
            pl.BlockSpec(w_mk.shape, const2),
            pl.BlockSpec(w_mv.shape, const2),
        ],
        out_specs=[
            pl.BlockSpec((tm, Q_RANK), row),
            pl.BlockSpec((tm, KV_RANK), row),
            pl.BlockSpec((1, KV_RANK, tm), lambda b, s: (b, 0, s)),
            pl.BlockSpec((tm, IDX_DIM), row),
            pl.BlockSpec((1, N_IDX_HEADS, tm), lambda b, s: (b, 0, s)),
            pl.BlockSpec((tm, CONV_CH), row),
            pl.BlockSpec((tm, MIX_C), row),
        ],
        out_shape=[
            jax.ShapeDtypeStruct((T, Q_RANK), bf),
            jax.ShapeDtypeStruct((T, KV_RANK), bf),
            jax.ShapeDtypeStruct((B, KV_RANK, S), bf),
            jax.ShapeDtypeStruct((T, IDX_DIM), bf),
            jax.ShapeDtypeStruct((B, N_IDX_HEADS, S), jnp.float32),
            jax.ShapeDtypeStruct((T, CONV_CH), bf),
            jax.ShapeDtypeStruct((T, MIX_C), bf),
        ],
        scratch_shapes=[
            pltpu.VMEM((SUBLANES, CONV_CH), jnp.float32),
            pltpu.VMEM((n_mem, MIX_C), bf),
            pltpu.VMEM((n_mem, MIX_C), bf),
        ],
        compiler_params=_cparams(("arbitrary", "arbitrary")),
        name="proj",
    )(x2, mem, w_main, w_small, q_g, kv_g, conv_w, w_mk, w_mv)


def _key_to_f32(key):
    bits = jnp.where(key < 0, key ^ jnp.int32(0x7FFFFFFF), key)
    return pltpu.bitcast(bits, jnp.float32)


def _colsum8(v):
    return jnp.sum(v.reshape(QB // SUBLANES, SUBLANES, QB), axis=0)


def _colmax8(v):
    return jnp.max(v.reshape(QB // SUBLANES, SUBLANES, QB), axis=0)


UNROLL_WIDTHS = (8, 4, 2, 1)


def _dsa_kernel(cq_ref, iwt_ref, kidx_ref, ckv_ref, ckvt_ref, wqi_ref, wuq_ref, wuk_ref, wuvt_ref,
                bias_ref, o_ref, wfold_ref, qidx_ref, qlat_ref, score_ref, mask_ref, logit_ref, acc_ref,
                *, k_sel, idx_bits):
    i = pl.program_id(1)
    f32 = jnp.float32
    bf = MXU_DTYPE
    n_blocks = i + 1
    n_blocks = n_blocks + jnp.where((n_blocks % 4 == 3) & (n_blocks < pl.num_programs(1)), 1, 0)
    s_loc = lax.broadcasted_iota(jnp.int32, (QB, QB), 0)
    t_glob = i * QB + lax.broadcasted_iota(jnp.int32, (QB, QB), 1)

    def blk(jb):
        return pl.multiple_of(jb * QB, QB)

    def block_loop(fn, init):
        c, start = init, 0
        for width in UNROLL_WIDTHS:
            n = (n_blocks - start) // width
            c = lax.fori_loop(0, n, lambda it, c, w=width, s=start: fn(s + it * w, w, c), c)
            start = start + n * width
        return c

    @pl.when(i == 0)
    def _():
        for h in range(N_HEADS_A):
            wfold_ref[:, h * KV_RANK:(h + 1) * KV_RANK] = (
                _dot_nt(wuq_ref[:, h * HEAD_DIM:(h + 1) * HEAD_DIM], wuk_ref[h]) * (HEAD_DIM ** -0.5)).astype(bf)

    cq = cq_ref[...]
    q_idx = _dot(cq, wqi_ref[...]).astype(bf)
    q_lat = _dot(cq, wfold_ref[...]).astype(bf)
    for h in range(N_HEADS_A):
        qidx_ref[h * QB:(h + 1) * QB, :] = q_idx[:, h * IDX_DIM:(h + 1) * IDX_DIM]
        qlat_ref[h * QB:(h + 1) * QB, :] = q_lat[:, h * KV_RANK:(h + 1) * KV_RANK]
    iw = iwt_ref[0]

    def score_body(jb0, nb, c):
        d_blk = _dot_nt(kidx_ref[pl.ds(blk(jb0), nb * QB), :], qidx_ref[...])
        for sb in range(nb):
            off = blk(jb0 + sb)
            d_all = d_blk[sb * QB:(sb + 1) * QB, :]
            acc = jnp.maximum(d_all[:, 0:QB], 0.0) * iw[0:1, :]
            for h in range(1, N_IDX_HEADS):
                acc = acc + jnp.maximum(d_all[:, h * QB:(h + 1) * QB], 0.0) * iw[h:h + 1, :]
            score_ref[pl.ds(off, QB), :] = jnp.where(s_loc + off <= t_glob, acc + 0.0, F32_LOWEST)
        return c

    block_loop(score_body, 0)

    def count_where(pred):
        def body(jb0, nb, acc):
            for sb in range(nb):
                off = blk(jb0 + sb)
                acc = acc + _colsum8(jnp.where(pred(score_ref[pl.ds(off, QB), :], off), 1.0, 0.0))
            return acc
        acc = block_loop(body, jnp.zeros((SUBLANES, QB), f32))
        return jnp.sum(acc, axis=0, keepdims=True)

    kf = float(k_sel)

    def search():
        c0 = count_where(lambda sc, off: sc >= 0.0)
        cand0 = jnp.where(c0 >= kf, jnp.int32(0), jnp.int32(-2 ** 31))

        def bit_body(it, cand):
            trial = cand + lax.shift_left(jnp.int32(1), 30 - it)
            tf = _key_to_f32(trial)
            cnt = count_where(lambda sc, off: sc >= tf)
            return jnp.where(cnt >= kf, trial, cand)

        cand = lax.fori_loop(0, 31, bit_body, cand0)
        thr = _key_to_f32(cand)
        n_gt = count_where(lambda sc, off: sc > thr)
        n_eq = count_where(lambda sc, off: sc == thr)
        need = kf - n_gt

        def tie_search():
            def tbody(it, xcut):
                trial = xcut + lax.shift_left(jnp.int32(1), idx_bits - 1 - it)
                cnt = count_where(lambda sc, off: (sc == thr) & (s_loc + off < trial))
                return jnp.where(cnt < need, trial, xcut)
            return lax.fori_loop(0, idx_bits, tbody, jnp.zeros((1, QB), jnp.int32))

        any_extra = jnp.max(n_eq - need) > 0.0
        xcut = lax.cond(any_extra, tie_search, lambda: jnp.full((1, QB), 2 ** idx_bits - 1, jnp.int32))
        return thr, xcut

    def no_search():
        return jnp.full((1, QB), F32_LOWEST, f32), jnp.full((1, QB), 2 ** idx_bits - 1, jnp.int32)

    thr, xcut = lax.cond((i + 1) * QB > k_sel, search, no_search)

    def mask_body(jb0, nb, c):
        for sb in range(nb):
            off = blk(jb0 + sb)
            sc = score_ref[pl.ds(off, QB), :]
            s_glob = s_loc + off
            keep = ((sc > thr) | ((sc == thr) & (s_glob <= xcut))) & (s_glob <= t_glob)
            mask_ref[pl.ds(off, QB), :] = jnp.where(keep, 0.0, -jnp.inf)
        return c

    block_loop(mask_body, 0)

    def p1_body(jb0, nb, m8):
        m8 = list(m8)
        lg_blk = _dot_nt(ckv_ref[pl.ds(blk(jb0), nb * QB), :], qlat_ref[...])
        for sb in range(nb):
            off = blk(jb0 + sb)
            lg = lg_blk[sb * QB:(sb + 1) * QB, :]
            msk = mask_ref[pl.ds(off, QB), :]
            bsel = jnp.clip(jb0 + sb - i + 2, 0, 2)
            for h in range(N_HEADS_A):
                lgh = lg[:, h * QB:(h + 1) * QB] + bias_ref[bsel, h] + msk
                logit_ref[pl.ds(off, QB), h * QB:(h + 1) * QB] = lgh
                m8[h] = jnp.maximum(m8[h], _colmax8(lgh))
        return tuple(m8)

    m8 = block_loop(p1_body, tuple(jnp.full((SUBLANES, QB), -jnp.inf, f32) for _ in range(N_HEADS_A)))
    m_row = [jnp.max(m, axis=0, keepdims=True) for m in m8]

    acc_ref[...] = jnp.zeros_like(acc_ref)

    def p2_body(jb0, nb, l8):
        l8 = list(l8)
        off = blk(jb0)
        rows = nb * QB
        ps = []
        for h in range(N_HEADS_A):
            p = jnp.exp(logit_ref[pl.ds(off, rows), h * QB:(h + 1) * QB] - m_row[h])
            l8[h] = l8[h] + jnp.sum(p.reshape(rows // SUBLANES, SUBLANES, QB), axis=0)
            ps.append(p.astype(bf))
        acc_ref[...] += _dot(ckvt_ref[0, :, pl.ds(off, rows)], jnp.concatenate(ps, axis=1))
        return tuple(l8)

    l8 = block_loop(p2_body, tuple(jnp.zeros((SUBLANES, QB), f32) for _ in range(N_HEADS_A)))

    outs = []
    for h in range(N_HEADS_A):
        l_row = jnp.sum(l8[h], axis=0, keepdims=True)
        o_lat_t = (acc_ref[:, h * QB:(h + 1) * QB] / l_row).astype(bf)
        outs.append(_dot(wuvt_ref[h], o_lat_t))
    o_ref[...] = jnp.concatenate(outs, axis=0).T.astype(o_ref.dtype)


def _dsa(cq, iwt, kidx, ckv, ckvt, w_qidx, w_uq, w_uk_h, w_uvt_h, bias_tiles, B, S):
    T = cq.shape[0]
    assert S % QB == 0 and QB >= REL_MAX_DIST
    nq = S // QB
    k_sel = min(TOPK_MAX, S // 4)
    idx_bits = max(1, (S - 1).bit_length())
    c2 = lambda b, i: (0, 0)
    c3 = lambda b, i: (0, 0, 0)
    return pl.pallas_call(
        functools.partial(_dsa_kernel, k_sel=k_sel, idx_bits=idx_bits),
        grid=(B, nq),
        in_specs=[
            pl.BlockSpec((QB, Q_RANK), lambda b, i: (b * nq + i, 0)),
            pl.BlockSpec((1, N_IDX_HEADS, QB), lambda b, i: (b, 0, i)),
            pl.BlockSpec((S, IDX_DIM), lambda b, i: (b, 0)),
            pl.BlockSpec((S, KV_RANK), lambda b, i: (b, 0)),
            pl.BlockSpec((1, KV_RANK, S), lambda b, i: (b, 0, 0)),
            pl.BlockSpec(w_qidx.shape, c2),
            pl.BlockSpec(w_uq.shape, c2),
            pl.BlockSpec(w_uk_h.shape, c3),
            pl.BlockSpec(w_uvt_h.shape, c3),
            pl.BlockSpec(bias_tiles.shape, lambda b, i: (0, 0, 0, 0)),
        ],
        out_specs=pl.BlockSpec((QB, MIX_A), lambda b, i: (b * nq + i, 0)),
        out_shape=jax.ShapeDtypeStruct((T, MIX_A), MXU_DTYPE),
        scratch_shapes=[
            pltpu.VMEM((Q_RANK, N_HEADS_A * KV_RANK), MXU_DTYPE),
            pltpu.VMEM((N_IDX_HEADS * QB, IDX_DIM), MXU_DTYPE),
            pltpu.VMEM((N_HEADS_A * QB, KV_RANK), MXU_DTYPE),
            pltpu.VMEM((S, QB), jnp.float32),
            pltpu.VMEM((S, QB), jnp.float32),
            pltpu.VMEM((S, N_HEADS_A * QB), jnp.float32),
            pltpu.VMEM((KV_RANK, N_HEADS_A * QB), jnp.float32),
        ],
        compiler_params=_cparams(("arbitrary", "arbitrary")),
        name="dsa",
    )(cq, iwt, kidx, ckv, ckvt, w_qidx, w_uq, w_uk_h, w_uvt_h, bias_tiles)


def _layer_norm(xf, g, b):
    mu = jnp.mean(xf, axis=-1, keepdims=True)
    xc = xf - mu
    var = jnp.mean(xc * xc, axis=-1, keepdims=True)
    return xc * lax.rsqrt(var + LN_EPS) * g + b


def _rank_rows(v, n):
    ri = lax.broadcasted_iota(jnp.int32, v.shape, 0)
    rank = jnp.zeros(v.shape, jnp.float32)
    for r2 in range(n):
        row = v[r2:r2 + 1, :]
        beats = (row > v) | ((row == v) & (ri > r2))
        rank = rank + jnp.where(beats, 1.0, 0.0)
    return rank


def _pack_factor():
    return 4 // jnp.dtype(MXU_DTYPE).itemsize


def _pack_rows(x):
    if _pack_factor() == 1:
        return pltpu.bitcast(x, jnp.int32)
    half = x.shape[1] // 2
    b = pltpu.bitcast(x.astype(MXU_DTYPE).astype(jnp.float32), jnp.int32)
    return b[:, half:] | lax.shift_right_logical(b[:, :half], jnp.int32(16))


_HIGH_HALF = -(1 << 16)


def _unpack_rows_f32(p):
    if _pack_factor() == 1:
        return [pltpu.bitcast(p, jnp.float32)]
    lo = pltpu.bitcast(lax.shift_left(p, jnp.int32(16)), jnp.float32)
    hi = pltpu.bitcast(p & jnp.int32(_HIGH_HALF), jnp.float32)
    return [lo, hi]


def _unpack_rows(p):
    return [v.astype(MXU_DTYPE) for v in _unpack_rows_f32(p)]


def _mix_router_kernel(x_ref, ya_ref, yb_ref, yc_ref, wo_ref, g_ref, b_ref, wrt_ref, rb_ref, exp_ref,
                       x1_ref, x1p_ref, sel_ref, w_ref, pos_ref, cnt_ref, base_ref, *, tm):
    step = pl.program_id(0)
    f32 = jnp.float32

    @pl.when(step == 0)
    def _():
        base_ref[...] = jnp.zeros_like(base_ref)

    mix = _dot(ya_ref[...], wo_ref[0:MIX_A, :])
    mix = mix + _dot(yb_ref[...], wo_ref[MIX_A:MIX_A + CONV_CH, :])
    mix = mix + _dot(yc_ref[...], wo_ref[MIX_A + CONV_CH:, :])
    x1 = _layer_norm(ALPHA * x_ref[...] + mix, g_ref[...], b_ref[...])
    x1_ref[...] = x1
    x1p_ref[...] = _pack_rows(x1)

    lg = lax.dot_general(wrt_ref[...], x1, _NT, precision=lax.Precision.HIGHEST, preferred_element_type=f32)
    s = 1.0 / (1.0 + jnp.exp(-lg))
    sc = s + rb_ref[...]

    g3 = sc.reshape(N_GROUPS, GROUP_SIZE, tm)
    m1 = jnp.max(g3, axis=1, keepdims=True)
    is_m1 = g3 == m1
    n_m1 = jnp.sum(jnp.where(is_m1, 1.0, 0.0), axis=1, keepdims=True)
    m2 = jnp.max(jnp.where(is_m1, -jnp.inf, g3), axis=1, keepdims=True)
    gscore = (m1 + jnp.where(n_m1 > 1.0, m1, m2)).reshape(N_GROUPS, tm)
    gsel = jnp.where(_rank_rows(gscore, N_GROUPS) < float(TOPK_GROUPS), 1.0, 0.0)
    emask = _dot(exp_ref[...], gsel.astype(MXU_DTYPE)) > 0.5
    masked = jnp.where(emask, sc, -jnp.inf)
    sel = (_rank_rows(masked, N_EXPERTS) < float(TOP_K)) & emask
    self_ = jnp.where(sel, 1.0, 0.0)
    top_s = jnp.where(sel, s, 0.0)
    w = top_s / jnp.sum(top_s, axis=0, keepdims=True) * ROUTED_SCALE

    t_r = lax.broadcasted_iota(jnp.int32, (tm, tm), 0)
    t_c = lax.broadcasted_iota(jnp.int32, (tm, tm), 1)
    upper = jnp.where(t_r < t_c, 1.0, 0.0).astype(MXU_DTYPE)
    pref = _dot(self_.astype(MXU_DTYPE), upper)
    base = base_ref[...]
    sel_ref[...] = self_
    w_ref[...] = w
    pos_ref[...] = base + pref
    base = base + jnp.sum(self_, axis=1, keepdims=True)
    base_ref[...] = base
    cnt_ref[...] = jnp.broadcast_to(base, cnt_ref.shape)


def _mix_router(x2, ya, yb, yc, w_out, ln_g, ln_b, w_router_t, router_bias, tm):
    T, D = x2.shape
    E = N_EXPERTS
    expand = (jnp.arange(E)[:, None] // GROUP_SIZE == jnp.arange(N_GROUPS)[None, :]).astype(MXU_DTYPE)
    row = lambda i: (i, 0)
    col = lambda i: (0, i)
    c2 = lambda i: (0, 0)
    f32 = jnp.float32
    return pl.pallas_call(
        functools.partial(_mix_router_kernel, tm=tm),
        grid=(T // tm,),
        in_specs=[
            pl.BlockSpec((tm, D), row),
            pl.BlockSpec((tm, MIX_A), row),
            pl.BlockSpec((tm, CONV_CH), row),
            pl.BlockSpec((tm, MIX_C), row),
            pl.BlockSpec(w_out.shape, c2),
            pl.BlockSpec((1, D), c2),
            pl.BlockSpec((1, D), c2),
            pl.BlockSpec((E, D), c2),
            pl.BlockSpec((E, 1), c2),
            pl.BlockSpec((E, N_GROUPS), c2),
        ],
        out_specs=[
            pl.BlockSpec((tm, D), row),
            pl.BlockSpec((tm, D // _pack_factor()), row),
            pl.BlockSpec((E, tm), col),
            pl.BlockSpec((E, tm), col),
            pl.BlockSpec((E, tm), col),
            pl.BlockSpec((E, LANES), c2),
        ],
        out_shape=[
            jax.ShapeDtypeStruct((T, D), f32),
            jax.ShapeDtypeStruct((T, D // _pack_factor()), jnp.int32),
            jax.ShapeDtypeStruct((E, T), f32),
            jax.ShapeDtypeStruct((E, T), f32),
            jax.ShapeDtypeStruct((E, T), f32),
            jax.ShapeDtypeStruct((E, LANES), f32),
        ],
        scratch_shapes=[pltpu.VMEM((E, 1), f32)],
        compiler_params=_cparams(("arbitrary",)),
        name="mix_router",
    )(x2, ya, yb, yc, w_out, ln_g, ln_b, w_router_t, router_bias, expand)


def _compact_kernel(sel_ref, w_ref, pos_ref, pstart_ref, low_ref, dest_ref, wk_ref):
    sel = sel_ref[...]
    on = sel > 0.5
    rank = _dot(low_ref[...], sel.astype(MXU_DTYPE))
    row = pstart_ref[...] + pos_ref[...]
    w = w_ref[...]
    dests, ws = [], []
    for k in range(TOP_K):
        m = on & (rank == float(k))
        dests.append(jnp.sum(jnp.where(m, row, 0.0), axis=0, keepdims=True))
        ws.append(jnp.sum(jnp.where(m, w, 0.0), axis=0, keepdims=True))
    dest_ref[...] = jnp.concatenate(dests, axis=0).astype(jnp.int32)
    wk_ref[...] = jnp.concatenate(ws, axis=0)


def _compact(sel_t, w_t, pos_t, pad_start, tm):
    E, T = sel_t.shape
    lower = (jnp.arange(E)[None, :] < jnp.arange(E)[:, None]).astype(MXU_DTYPE)
    col = lambda i: (0, i)
    c2 = lambda i: (0, 0)
    return pl.pallas_call(
        _compact_kernel,
        grid=(T // tm,),
        in_specs=[pl.BlockSpec((E, tm), col), pl.BlockSpec((E, tm), col), pl.BlockSpec((E, tm), col),
                  pl.BlockSpec((E, 1), c2), pl.BlockSpec((E, E), c2)],
        out_specs=[pl.BlockSpec((TOP_K, tm), col), pl.BlockSpec((TOP_K, tm), col)],
        out_shape=[jax.ShapeDtypeStruct((TOP_K, T), jnp.int32), jax.ShapeDtypeStruct((TOP_K, T), jnp.float32)],
        compiler_params=_cparams(("arbitrary",)),
        name="route_compact",
    )(sel_t, w_t, pos_t, pad_start, lower)


def _row_copy(src, s, dst, d, sem):
    return pltpu.make_async_copy(src.at[pl.ds(s, 1)], dst.at[pl.ds(d, 1)], sem)


def _dispatch_kernel(flo_ref, fhi_ref, dest_ref, x_ref, xs_hbm, zero_ref, sem, zsem, *, td):
    step = pl.program_id(0)

    @pl.when(step == 0)
    def _():
        zero_ref[...] = jnp.zeros_like(zero_ref)

        def per_expert(fn):
            def ebody(e, c):
                lax.fori_loop(flo_ref[e], fhi_ref[e], lambda r, c2: (fn(r), c2)[1], 0)
                return c
            lax.fori_loop(0, N_EXPERTS, ebody, 0)

        per_expert(lambda r: _row_copy(zero_ref, 0, xs_hbm, r, zsem).start())
        per_expert(lambda r: _row_copy(zero_ref, 0, xs_hbm, r, zsem).wait())

    def issue(r, c):
        for k in range(TOP_K):
            _row_copy(x_ref, r, xs_hbm, dest_ref[k, r], sem).start()
        return c

    def drain(r, c):
        for k in range(TOP_K):
            _row_copy(x_ref, r, xs_hbm, dest_ref[k, r], sem).wait()
        return c

    lax.fori_loop(0, td, issue, 0)
    lax.fori_loop(0, td, drain, 0)


def _dispatch(dest_t, x1p, fill_lo, fill_hi, n_rows, td):
    T, W = x1p.shape
    return pl.pallas_call(
        functools.partial(_dispatch_kernel, td=td),
        grid_spec=pltpu.PrefetchScalarGridSpec(
            num_scalar_prefetch=2,
            grid=(T // td,),
            in_specs=[
                pl.BlockSpec((TOP_K, td), lambda i, lo, hi: (0, i), memory_space=pltpu.SMEM),
                pl.BlockSpec((td, W), lambda i, lo, hi: (i, 0)),
            ],
            out_specs=pl.BlockSpec(memory_space=pl.ANY),
            scratch_shapes=[pltpu.VMEM((SUBLANES, W), x1p.dtype),
                            pltpu.SemaphoreType.DMA, pltpu.SemaphoreType.DMA],
        ),
        out_shape=jax.ShapeDtypeStruct((n_rows, W), x1p.dtype),
        compiler_params=_cparams(("arbitrary",)),
        name="dispatch",
    )(fill_lo, fill_hi, dest_t, x1p)


def _silu(g):
    return g / (1.0 + jnp.exp(-g))


def _expert_kernel(be_ref, nv_ref, nu_ref, xs_ref, wg_ref, wu_ref, wd_ref, ys_ref, wgb_ref, wub_ref, wdb_ref):
    i = pl.program_id(0)

    @pl.when((i == 0) | (be_ref[i] != be_ref[jnp.maximum(i - 1, 0)]))
    def _():
        wgb_ref[...] = wg_ref[0].astype(MXU_DTYPE)
        wub_ref[...] = wu_ref[0].astype(MXU_DTYPE)
        wdb_ref[...] = wd_ref[0].astype(MXU_DTYPE)

    @pl.when(i < nu_ref[0])
    def _():
        live = lax.broadcasted_iota(jnp.int32, (ROW_BLOCK, 1), 0) < nv_ref[i]
        parts = [jnp.where(live, v, jnp.zeros_like(v)) for v in _unpack_rows(xs_ref[...])]
        dk = wgb_ref.shape[0] // len(parts)

        def proj(w_ref):
            acc = _dot(parts[0], w_ref[0:dk, :])
            for n in range(1, len(parts)):
                acc = acc + _dot(parts[n], w_ref[n * dk:(n + 1) * dk, :])
            return acc

        a = (_silu(proj(wgb_ref)) * proj(wub_ref)).astype(MXU_DTYPE)
        ys_ref[...] = _pack_rows(_dot(a, wdb_ref[...]))


def _experts(xs, block_e, block_valid, n_used, w_gate, w_up, w_down):
    n_rows, W = xs.shape
    D = w_gate.shape[1]
    n_blocks = n_rows // ROW_BLOCK
    blk = lambda i, be, nv, nu: (jnp.minimum(i, nu[0] - 1), 0)
    wsel = lambda i, be, nv, nu: (be[i], 0, 0)
    return pl.pallas_call(
        _expert_kernel,
        grid_spec=pltpu.PrefetchScalarGridSpec(
            num_scalar_prefetch=3,
            grid=(n_blocks,),
            in_specs=[
                pl.BlockSpec((ROW_BLOCK, W), blk),
                pl.BlockSpec((1, D, D_EXPERT), wsel),
                pl.BlockSpec((1, D, D_EXPERT), wsel),
                pl.BlockSpec((1, D_EXPERT, D), wsel),
            ],
            out_specs=pl.BlockSpec((ROW_BLOCK, W), blk),
            scratch_shapes=[pltpu.VMEM((D, D_EXPERT), MXU_DTYPE), pltpu.VMEM((D, D_EXPERT), MXU_DTYPE),
                            pltpu.VMEM((D_EXPERT, D), MXU_DTYPE)],
        ),
        out_shape=jax.ShapeDtypeStruct((n_rows, W), xs.dtype),
        compiler_params=_cparams(("arbitrary",)),
        name="experts",
    )(block_e, block_valid, n_used, xs, w_gate, w_up, w_down)


SC_CORES = 2
SC_SUBCORES = 16
SC_GATHER_ROWS = 64
COMBINE_CHUNKS = 4


def _sc_gather_rows(table, idx):
    n = idx.shape[0]
    w = table.shape[1]
    n_workers = SC_CORES * SC_SUBCORES
    per_worker = n // n_workers
    assert n % n_workers == 0 and per_worker % SC_GATHER_ROWS == 0
    mesh = plsc.VectorSubcoreMesh(core_axis_name="c", subcore_axis_name="s")

    @functools.partial(
        pl.kernel, mesh=mesh,
        out_type=jax.ShapeDtypeStruct((n, w), table.dtype),
        scratch_types=[
            pltpu.VMEM((2, SC_GATHER_ROWS), jnp.int32),
            pltpu.VMEM((2, SC_GATHER_ROWS, w), table.dtype),
            pltpu.SemaphoreType.DMA((2,)),
        ],
        name="sc_gather_rows",
    )
    def gather(table_hbm, idx_hbm, out_hbm, idx_v, rows_v, sem):
        wid = lax.axis_index("s") * SC_CORES + lax.axis_index("c")
        base = wid * per_worker
        n_steps = per_worker // SC_GATHER_ROWS

        def gather_copy(slot):
            return pltpu.make_async_copy(table_hbm.at[idx_v.at[slot]], rows_v.at[slot], sem.at[slot])

        def start(step, slot):
            pltpu.sync_copy(idx_hbm.at[pl.ds(base + step * SC_GATHER_ROWS, SC_GATHER_ROWS)], idx_v.at[slot])
            gather_copy(slot).start()

        start(0, 0)

        @pl.loop(0, n_steps, step=2)
        def _(g):
            for slot in range(2):
                step = g + slot

                @pl.when(step + 1 < n_steps)
                def _():
                    start(step + 1, 1 - slot)

                gather_copy(slot).wait()
                pltpu.sync_copy(rows_v.at[slot], out_hbm.at[pl.ds(base + step * SC_GATHER_ROWS, SC_GATHER_ROWS)])

    return gather(table, idx)


SC_SCATTER_ROWS = 64


def _sc_scatter_rows(rows, idx3, n_out):
    n_src, w = rows.shape
    n_chunks, n_dst, batch = idx3.shape
    n_workers = SC_CORES * SC_SUBCORES
    assert batch == SC_SCATTER_ROWS and n_chunks * batch == n_src and n_chunks % (2 * n_workers) == 0
    per_worker = n_chunks // n_workers
    mesh = plsc.VectorSubcoreMesh(core_axis_name="c", subcore_axis_name="s")

    @functools.partial(
        pl.kernel, mesh=mesh,
        out_type=jax.ShapeDtypeStruct((n_out, w), rows.dtype),
        scratch_types=[
            pltpu.VMEM((2, n_dst, batch), jnp.int32),
            pltpu.VMEM((2, batch, w), rows.dtype),
            pltpu.SemaphoreType.DMA((2,)),
            pltpu.SemaphoreType.DMA,
        ],
        name="sc_scatter_rows",
    )
    def scatter(rows_hbm, idx_hbm, out_hbm, idx_v, rows_v, load_sem, store_sem):
        wid = lax.axis_index("s") * SC_CORES + lax.axis_index("c")

        def load_copy(step, slot):
            c = wid * per_worker + step
            return pltpu.make_async_copy(rows_hbm.at[pl.ds(c * batch, batch)], rows_v.at[slot], load_sem.at[slot])

        def load(step, slot):
            pltpu.sync_copy(idx_hbm.at[wid * per_worker + step], idx_v.at[slot])
            load_copy(step, slot).start()

        def store_copy(slot, k):
            return pltpu.make_async_copy(rows_v.at[slot], out_hbm.at[idx_v.at[slot].at[k]], store_sem)

        load(0, 0)

        @pl.loop(0, per_worker, step=2)
        def _(g):
            for slot in range(2):
                step = g + slot

                @pl.when(step + 1 < per_worker)
                def _():
                    load(step + 1, 1 - slot)

                load_copy(step, slot).wait()
                for k in range(n_dst):
                    store_copy(slot, k).start()
                for k in range(n_dst):
                    store_copy(slot, k).wait()

    return scatter(rows, idx3)


def _shared_kernel(x1_ref, wsg_ref, wsu_ref, wsd_ref, o_ref):
    xb = x1_ref[...].astype(MXU_DTYPE)
    a = (_silu(_dot(xb, wsg_ref[...])) * _dot(xb, wsu_ref[...])).astype(MXU_DTYPE)
    o_ref[...] = _dot(a, wsd_ref[...])


def _shared_expert(x1, w_sg, w_su, w_sd, tm):
    T, D = x1.shape
    row = lambda i: (i, 0)
    c2 = lambda i: (0, 0)
    return pl.pallas_call(
        _shared_kernel,
        grid=(T // tm,),
        in_specs=[pl.BlockSpec((tm, D), row), pl.BlockSpec(w_sg.shape, c2), pl.BlockSpec(w_su.shape, c2),
                  pl.BlockSpec(w_sd.shape, c2)],
        out_specs=pl.BlockSpec((tm, D), row),
        out_shape=jax.ShapeDtypeStruct((T, D), jnp.float32),
        compiler_params=_cparams(("arbitrary",)),
        name="shared_expert",
    )(x1, w_sg, w_su, w_sd)


def _combine2_kernel(wk_ref, x1_ref, g_ref_rows, wsg_ref, wsu_ref, wsd_ref, g_ref, b_ref, o_ref):
    x1 = x1_ref[...]
    xb = x1.astype(MXU_DTYPE)
    a = (_silu(_dot(xb, wsg_ref[...])) * _dot(xb, wsu_ref[...])).astype(MXU_DTYPE)
    shared = _dot(a, wsd_ref[...])
    wk = wk_ref[...].T
    groups = [wk[:, 0:1] * v for v in _unpack_rows_f32(g_ref_rows[0])]
    for k in range(1, TOP_K):
        groups = [g + wk[:, k:k + 1] * v for g, v in zip(groups, _unpack_rows_f32(g_ref_rows[k]))]
    routed = jnp.concatenate(groups, axis=1)
    o_ref[...] = _layer_norm(ALPHA * x1 + (routed + shared), g_ref[...], b_ref[...])


def _combine2_kernel_into(wk_ref, x1_ref, g_ref_rows, wsg_ref, wsu_ref, wsd_ref, g_ref, b_ref, prev_ref, o_ref):
    del prev_ref
    _combine2_kernel(wk_ref, x1_ref, g_ref_rows, wsg_ref, wsu_ref, wsd_ref, g_ref, b_ref, o_ref)


def _combine2(wk_t, x1, gathered, w_sg, w_su, w_sd, ln_g, ln_b, tc, chunk, prev):
    T, D = x1.shape
    _, t_chunk, W = gathered.shape
    base = chunk * (t_chunk // tc)
    row = lambda i: (base + i, 0)
    c2 = lambda i: (0, 0)
    in_specs = [
        pl.BlockSpec((TOP_K, tc), lambda i: (0, base + i)),
        pl.BlockSpec((tc, D), row),
        pl.BlockSpec((TOP_K, tc, W), lambda i: (0, i, 0)),
        pl.BlockSpec(w_sg.shape, c2),
        pl.BlockSpec(w_su.shape, c2),
        pl.BlockSpec(w_sd.shape, c2),
        pl.BlockSpec((1, D), c2),
        pl.BlockSpec((1, D), c2),
    ]
    args = [wk_t, x1, gathered, w_sg, w_su, w_sd, ln_g, ln_b]
    if prev is None:
        body, aliases = _combine2_kernel, {}
    else:
        body, aliases = _combine2_kernel_into, {len(args): 0}
        in_specs.append(pl.BlockSpec(memory_space=pl.ANY))
        args.append(prev)
    return pl.pallas_call(
        body,
        grid=(t_chunk // tc,),
        in_specs=in_specs,
        out_specs=pl.BlockSpec((tc, D), row),
        out_shape=jax.ShapeDtypeStruct((T, D), jnp.float32),
        input_output_aliases=aliases,
        compiler_params=_cparams(("arbitrary",)),
        name="combine",
    )(*args)


def _combine_kernel(dest_ref, wk_ref, x1_ref, ys_hbm, wsg_ref, wsu_ref, wsd_ref, g_ref, b_ref,
                    o_ref, buf_ref, sem, *, tc):
    def issue(r, c):
        for k in range(TOP_K):
            _row_copy(ys_hbm, dest_ref[k, r], buf_ref.at[k], r, sem).start()
        return c

    def drain(r, c):
        for k in range(TOP_K):
            _row_copy(ys_hbm, dest_ref[k, r], buf_ref.at[k], r, sem).wait()
        return c

    lax.fori_loop(0, tc, issue, 0)
    x1 = x1_ref[...]
    xb = x1.astype(MXU_DTYPE)
    a = (_silu(_dot(xb, wsg_ref[...])) * _dot(xb, wsu_ref[...])).astype(MXU_DTYPE)
    shared = _dot(a, wsd_ref[...])
    lax.fori_loop(0, tc, drain, 0)
    wk = wk_ref[...]
    groups = [wk[:, 0:1] * v for v in _unpack_rows_f32(buf_ref[0])]
    for k in range(1, TOP_K):
        groups = [g + wk[:, k:k + 1] * v for g, v in zip(groups, _unpack_rows_f32(buf_ref[k]))]
    routed = jnp.concatenate(groups, axis=1)
    o_ref[...] = _layer_norm(ALPHA * x1 + (routed + shared), g_ref[...], b_ref[...])


def _combine(dest_t, wk, x1, ys, w_sg, w_su, w_sd, ln_g, ln_b, tc):
    T, D = x1.shape
    row = lambda i: (i, 0)
    c2 = lambda i: (0, 0)
    return pl.pallas_call(
        functools.partial(_combine_kernel, tc=tc),
        grid=(T // tc,),
        in_specs=[
            pl.BlockSpec((TOP_K, tc), lambda i: (0, i), memory_space=pltpu.SMEM),
            pl.BlockSpec((tc, TOP_K), row),
            pl.BlockSpec((tc, D), row),
            pl.BlockSpec(memory_space=pl.ANY),
            pl.BlockSpec(w_sg.shape, c2),
            pl.BlockSpec(w_su.shape, c2),
            pl.BlockSpec(w_sd.shape, c2),
            pl.BlockSpec((1, D), c2),
            pl.BlockSpec((1, D), c2),
        ],
        out_specs=pl.BlockSpec((tc, D), row),
        out_shape=jax.ShapeDtypeStruct((T, D), jnp.float32),
        scratch_shapes=[pltpu.VMEM((TOP_K, tc, ys.shape[1]), ys.dtype), pltpu.SemaphoreType.DMA],
        compiler_params=_cparams(("arbitrary",)),
        name="combine",
    )(dest_t, wk, x1, ys, w_sg, w_su, w_sd, ln_g, ln_b)


def _split_w_in(w_in):
    bf = MXU_DTYPE
    o_kv = Q_RANK
    o_ki = o_kv + KV_RANK
    o_iw = o_ki + IDX_DIM
    o_rest = o_iw + N_IDX_HEADS
    w_main = jnp.concatenate([w_in[:, :o_ki], w_in[:, o_rest:]], axis=1).astype(bf)
    w_small = jnp.pad(w_in[:, o_ki:o_rest], ((0, 0), (0, LANES - IDX_DIM - N_IDX_HEADS))).astype(bf)
    return w_main, w_small


def _stages(x, mem, w_in, q_norm_g, kv_norm_g, w_uq, w_uk, w_uv, w_qidx, rel_bias, conv_w, w_mem_k, w_mem_v, w_out, ln1_g, ln1_b, w_router, router_bias, w_e_gate, w_e_up, w_e_down, w_s_gate, w_s_up, w_s_down, ln2_g, ln2_b, upto=None):
    B, S, D = x.shape
    T = B * S
    bf = MXU_DTYPE
    l = 0
    res = {}
    x2 = x.reshape(T, D)
    w_main, w_small = _split_w_in(w_in[l])
    cq, ckv, ckvt, kidx, iwt, yb, yc = _proj(
        x2, mem, w_main, w_small, q_norm_g[l].reshape(1, -1), kv_norm_g[l].reshape(1, -1), conv_w[l],
        w_mem_k[l].astype(bf), w_mem_v[l].astype(bf), B, S, tm=min(512, S))
    res.update(c_q=cq, c_kv=ckv, k_idx=kidx, y_b=yb, y_c=yc,
               idx_w=jnp.swapaxes(iwt, 1, 2) / (N_IDX_HEADS ** -0.5 * IDX_DIM ** -0.5))
    if upto == "proj":
        return res
    bias_t = _bias_tiles(rel_bias)
    ya = _dsa(cq, iwt, kidx, ckv, ckvt,
              w_qidx[l].reshape(Q_RANK, -1).astype(bf), w_uq[l].reshape(Q_RANK, -1).astype(bf),
              jnp.transpose(w_uk[l], (1, 0, 2)).astype(bf), jnp.transpose(w_uv[l], (1, 2, 0)).astype(bf),
              bias_t, B, S)
    res.update(y_a=ya)
    if upto == "dsa":
        return res

    x1, x1p, sel_t, w_t, pos_t, cnt = _mix_router(
        x2, ya, yb, yc, w_out[l].astype(bf), ln1_g[l].reshape(1, -1), ln1_b[l].reshape(1, -1),
        w_router[l].T, router_bias[l].reshape(-1, 1), tm=min(512, T))
    res.update(x1=x1)

    counts = cnt[:, 0].astype(jnp.int32)
    padded = (counts + ROW_BLOCK - 1) // ROW_BLOCK * ROW_BLOCK
    pad_end = jnp.cumsum(padded)
    pad_start = pad_end - padded
    n_blocks = -(-(T * TOP_K) // ROW_BLOCK) + N_EXPERTS
    n_rows = n_blocks * ROW_BLOCK
    block_start = jnp.arange(n_blocks, dtype=jnp.int32) * ROW_BLOCK
    block_e = jnp.minimum(jnp.sum((pad_end[None, :] <= block_start[:, None]).astype(jnp.int32), axis=1),
                          N_EXPERTS - 1)
    n_used = (pad_end[-1:] // ROW_BLOCK).astype(jnp.int32)

    dest_t, wk_t = _compact(sel_t, w_t, pos_t, pad_start.astype(jnp.float32).reshape(-1, 1), tm=min(512, T))
    block_valid = jnp.clip((pad_start + counts)[block_e] - block_start, 0, ROW_BLOCK).astype(jnp.int32)
    bt = SC_SCATTER_ROWS
    idx3 = jnp.transpose(dest_t.reshape(TOP_K, T // bt, bt), (1, 0, 2))
    xs = _sc_scatter_rows(x1p, idx3, n_rows)
    ys = _experts(xs, block_e, block_valid, n_used, w_e_gate[l], w_e_up[l], w_e_down[l])
    n_chunks = COMBINE_CHUNKS if T % (COMBINE_CHUNKS * 256) == 0 else 1
    t_chunk = T // n_chunks
    out = None
    for c in range(n_chunks):
        idx_c = dest_t[:, c * t_chunk:(c + 1) * t_chunk].reshape(-1)
        gathered = _sc_gather_rows(ys, idx_c).reshape(TOP_K, t_chunk, -1)
        out = _combine2(wk_t, x1, gathered, w_s_gate[l].astype(bf), w_s_up[l].astype(bf), w_s_down[l].astype(bf),
                        ln2_g[l].reshape(1, -1), ln2_b[l].reshape(1, -1), tc=min(256, t_chunk), chunk=c, prev=out)
    res.update(out=out.reshape(B, S, D))
    return res


def kernel(x, mem, w_in, q_norm_g, kv_norm_g, w_uq, w_uk, w_uv, w_qidx, rel_bias, conv_w, w_mem_k, w_mem_v, w_out, ln1_g, ln1_b, w_router, router_bias, w_e_gate, w_e_up, w_e_down, w_s_gate, w_s_up, w_s_down, ln2_g, ln2_b):
    return _stages(x, mem, w_in, q_norm_g, kv_norm_g, w_uq, w_uk, w_uv, w_qidx, rel_bias, conv_w, w_mem_k, w_mem_v, w_out, ln1_g, ln1_b, w_router, router_bias, w_e_gate, w_e_up, w_e_down, w_s_gate, w_s_up, w_s_down, ln2_g, ln2_b)["out"]
```

```python
import functools
import math

import jax
import jax.numpy as jnp
from jax import lax
from jax.experimental import pallas as pl
from jax.experimental.pallas import tpu as pltpu
from jax.experimental.pallas import tpu_sc as plsc

N_HEADS_A = 8
HEAD_DIM = 64
Q_RANK = 256
KV_RANK = 128
N_IDX_HEADS = 8
IDX_DIM = 64
TOPK_MAX = 256
REL_BUCKETS = 32
REL_MAX_DIST = 128
CONV_CH = 256
CONV_WIDTH = 3
N_MEM_HEADS = 4
MIX_A = N_HEADS_A * HEAD_DIM
MIX_C = N_MEM_HEADS * HEAD_DIM
N_EXPERTS = 64
N_GROUPS = 8
GROUP_SIZE = N_EXPERTS // N_GROUPS
TOPK_GROUPS = 4
TOP_K = 8
D_EXPERT = 256
ROUTED_SCALE = 2.5
MOE_BLOCK = 256
DEPTH = 1
ALPHA = (2.0 * DEPTH) ** 0.25
LN_EPS = 1e-5
RMS_EPS = 1e-6

LANES = 128
SUBLANES = 8
QB = 128
F32_LOWEST = -3.4028234663852886e38
VMEM_LIMIT = 56 * 1024 * 1024
MXU_DTYPE = jnp.bfloat16
ROW_BLOCK = 1024

_NT = (((1,), (1,)), ((), ()))


def _dot(a, b):
    return jnp.dot(a, b, preferred_element_type=jnp.float32)


def _dot_nt(a, b):
    return lax.dot_general(a, b, _NT, preferred_element_type=jnp.float32)


def _cparams(sem):
    return pltpu.CompilerParams(dimension_semantics=sem, vmem_limit_bytes=VMEM_LIMIT)


def _bias_kernel(rb_ref, o_ref):
    s = lax.broadcasted_iota(jnp.int32, (QB, QB), 0)
    t = lax.broadcasted_iota(jnp.int32, (QB, QB), 1)
    max_exact = REL_BUCKETS // 2
    for tile in range(3):
        n = jnp.maximum(t - s + (2 - tile) * QB, 0)
        nf = jnp.maximum(n.astype(jnp.float32), 1.0)
        large = max_exact + (jnp.log(nf / max_exact) / math.log(REL_MAX_DIST / max_exact)
                             * (REL_BUCKETS - max_exact)).astype(jnp.int32)
        large = jnp.minimum(large, REL_BUCKETS - 1)
        bucket = jnp.where(n < max_exact, n, large)
        for h in range(N_HEADS_A):
            acc = jnp.zeros((QB, QB), jnp.float32)
            for b in range(REL_BUCKETS):
                acc = jnp.where(bucket == b, rb_ref[b, h], acc)
            o_ref[tile, h] = acc


def _bias_tiles(rel_bias):
    return pl.pallas_call(
        _bias_kernel,
        in_specs=[pl.BlockSpec(memory_space=pltpu.SMEM)],
        out_specs=pl.BlockSpec(memory_space=pltpu.VMEM),
        out_shape=jax.ShapeDtypeStruct((3, N_HEADS_A, QB, QB), jnp.float32),
        name="bias_tiles",
    )(rel_bias)


_MAIN_COLS = Q_RANK + KV_RANK + 3 * CONV_CH + MIX_C


def _proj_kernel(x_ref, mem_ref, wm_ref, ws_ref, qg_ref, kvg_ref, cw_ref, wmk_ref, wmv_ref,
                 cq_ref, ckv_ref, ckvt_ref, kidx_ref, iwt_ref, yb_ref, yc_ref,
                 carry_ref, mk_ref, mv_ref, *, tm):
    si = pl.program_id(1)

    @pl.when(si == 0)
    def _():
        carry_ref[...] = jnp.zeros_like(carry_ref)
        mb = mem_ref[0].astype(MXU_DTYPE)
        mk_ref[...] = _dot(mb, wmk_ref[...]).astype(MXU_DTYPE)
        mv_ref[...] = _dot(mb, wmv_ref[...]).astype(MXU_DTYPE)

    xb = x_ref[...].astype(MXU_DTYPE)
    p = _dot(xb, wm_ref[...])
    small = _dot(xb, ws_ref[...])

    o = 0
    cq = p[:, o:o + Q_RANK]; o += Q_RANK
    ckv = p[:, o:o + KV_RANK]; o += KV_RANK
    g_b = p[:, o:o + CONV_CH]; o += CONV_CH
    g_c = p[:, o:o + CONV_CH]; o += CONV_CH
    h_c = p[:, o:o + CONV_CH]; o += CONV_CH
    q_mem = p[:, o:o + MIX_C]

    cq = cq * lax.rsqrt(jnp.mean(cq * cq, axis=-1, keepdims=True) + RMS_EPS) * qg_ref[...]
    ckv = ckv * lax.rsqrt(jnp.mean(ckv * ckv, axis=-1, keepdims=True) + RMS_EPS) * kvg_ref[...]
    cq_ref[...] = cq.astype(MXU_DTYPE)
    ckv_b = ckv.astype(MXU_DTYPE)
    ckv_ref[...] = ckv_b
    ckvt_ref[0] = ckv.T.astype(MXU_DTYPE)

    kidx_ref[...] = small[:, :IDX_DIM].astype(MXU_DTYPE)
    small_t = small.T
    iwt_ref[0] = small_t[IDX_DIM:IDX_DIM + N_IDX_HEADS, :] * (N_IDX_HEADS ** -0.5 * IDX_DIM ** -0.5)

    u = g_c * h_c
    rows = lax.broadcasted_iota(jnp.int32, (tm, 1), 0)
    c6 = carry_ref[SUBLANES - 2:SUBLANES - 1, :]
    c7 = carry_ref[SUBLANES - 1:SUBLANES, :]
    u1 = jnp.where(rows == 0, c7, pltpu.roll(u, 1, 0))
    u2 = jnp.where(rows == 0, c6, jnp.where(rows == 1, c7, pltpu.roll(u, 2, 0)))
    y = cw_ref[0:1, :] * u2
    y = y + cw_ref[1:2, :] * u1
    y = y + cw_ref[2:3, :] * u
    yb_ref[...] = (g_b * y).astype(MXU_DTYPE)
    carry_ref[...] = u[tm - SUBLANES:, :]

    qm = q_mem.astype(MXU_DTYPE)
    outs = []
    for h in range(N_MEM_HEADS):
        sl = slice(h * HEAD_DIM, (h + 1) * HEAD_DIM)
        lg = _dot_nt(qm[:, sl], mk_ref[:, sl]) * (HEAD_DIM ** -0.5)
        lg = lg - jnp.max(lg, axis=-1, keepdims=True)
        e = jnp.exp(lg)
        pr = e / jnp.sum(e, axis=-1, keepdims=True)
        outs.append(_dot(pr.astype(MXU_DTYPE), mv_ref[:, sl]))
    yc_ref[...] = jnp.concatenate(outs, axis=-1).astype(MXU_DTYPE)


def _proj(x2, mem, w_main, w_small, q_g, kv_g, conv_w, w_mk, w_mv, B, S, tm):
    T, D = x2.shape
    n_mem = mem.shape[1]
    ns = S // tm
    row = lambda b, s: (b * ns + s, 0)
    const2 = lambda b, s: (0, 0)
    bf = MXU_DTYPE
    return pl.pallas_call(
        functools.partial(_proj_kernel, tm=tm),
        grid=(B, ns),
        in_specs=[
            pl.BlockSpec((tm, D), row),
            pl.BlockSpec((1, n_mem, D), lambda b, s: (b, 0, 0)),
            pl.BlockSpec(w_main.shape, const2),
            pl.BlockSpec(w_small.shape, const2),
            pl.BlockSpec(q_g.shape, const2),
            pl.BlockSpec(kv_g.shape, const2),
            pl.BlockSpec(conv_w.shape, const2),
            pl.BlockSpec(w_mk.shape, const2),
            pl.BlockSpec(w_mv.shape, const2),
        ],
        out_specs=[
            pl.BlockSpec((tm, Q_RANK), row),
            pl.BlockSpec((tm, KV_RANK), row),
            pl.BlockSpec((1, KV_RANK, tm), lambda b, s: (b, 0, s)),
            pl.BlockSpec((tm, IDX_DIM), row),
            pl.BlockSpec((1, N_IDX_HEADS, tm), lambda b, s: (b, 0, s)),
            pl.BlockSpec((tm, CONV_CH), row),
            pl.BlockSpec((tm, MIX_C), row),
        ],
        out_shape=[
            jax.ShapeDtypeStruct((T, Q_RANK), bf),
            jax.ShapeDtypeStruct((T, KV_RANK), bf),
            jax.ShapeDtypeStruct((B, KV_RANK, S), bf),
            jax.ShapeDtypeStruct((T, IDX_DIM), bf),
            jax.ShapeDtypeStruct((B, N_IDX_HEADS, S), jnp.float32),
            jax.ShapeDtypeStruct((T, CONV_CH), bf),
            jax.ShapeDtypeStruct((T, MIX_C), bf),
        ],
        scratch_shapes=[
            pltpu.VMEM((SUBLANES, CONV_CH), jnp.float32),
            pltpu.VMEM((n_mem, MIX_C), bf),
            pltpu.VMEM((n_mem, MIX_C), bf),
        ],
        compiler_params=_cparams(("arbitrary", "arbitrary")),
        name="proj",
    )(x2, mem, w_main, w_small, q_g, kv_g, conv_w, w_mk, w_mv)


def _key_to_f32(key):
    bits = jnp.where(key < 0, key ^ jnp.int32(0x7FFFFFFF), key)
    return pltpu.bitcast(bits, jnp.float32)


def _colsum8(v):
    return jnp.sum(v.reshape(QB // SUBLANES, SUBLANES, QB), axis=0)


def _colmax8(v):
    return jnp.max(v.reshape(QB // SUBLANES, SUBLANES, QB), axis=0)


UNROLL_WIDTHS = (8, 4, 2, 1)


def _dsa_kernel(cq_ref, iwt_ref, kidx_ref, ckv_ref, ckvt_ref, wqi_ref, wuq_ref, wuk_ref, wuvt_ref,
                bias_ref, o_ref, wfold_ref, qidx_ref, qlat_ref, score_ref, mask_ref, logit_ref, acc_ref,
                *, k_sel, idx_bits):
    i = pl.program_id(1)
    f32 = jnp.float32
    bf = MXU_DTYPE
    n_blocks = i + 1
    n_blocks = n_blocks + jnp.where((n_blocks % 4 == 3) & (n_blocks < pl.num_programs(1)), 1, 0)
    s_loc = lax.broadcasted_iota(jnp.int32, (QB, QB), 0)
    t_glob = i * QB + lax.broadcasted_iota(jnp.int32, (QB, QB), 1)

    def blk(jb):
        return pl.multiple_of(jb * QB, QB)

    def block_loop(fn, init):
        c, start = init, 0
        for width in UNROLL_WIDTHS:
            n = (n_blocks - start) // width
            c = lax.fori_loop(0, n, lambda it, c, w=width, s=start: fn(s + it * w, w, c), c)
            start = start + n * width
        return c

    @pl.when(i == 0)
    def _():
        for h in range(N_HEADS_A):
            wfold_ref[:, h * KV_RANK:(h + 1) * KV_RANK] = (
                _dot_nt(wuq_ref[:, h * HEAD_DIM:(h + 1) * HEAD_DIM], wuk_ref[h]) * (HEAD_DIM ** -0.5)).astype(bf)

    cq = cq_ref[...]
    q_idx = _dot(cq, wqi_ref[...]).astype(bf)
    q_lat = _dot(cq, wfold_ref[...]).astype(bf)
    for h in range(N_HEADS_A):
        qidx_ref[h * QB:(h + 1) * QB, :] = q_idx[:, h * IDX_DIM:(h + 1) * IDX_DIM]
        qlat_ref[h * QB:(h + 1) * QB, :] = q_lat[:, h * KV_RANK:(h + 1) * KV_RANK]
    iw = iwt_ref[0]

    def score_body(jb0, nb, c):
        d_blk = _dot_nt(kidx_ref[pl.ds(blk(jb0), nb * QB), :], qidx_ref[...])
        for sb in range(nb):
            off = blk(jb0 + sb)
            d_all = d_blk[sb * QB:(sb + 1) * QB, :]
            acc = jnp.maximum(d_all[:, 0:QB], 0.0) * iw[0:1, :]
            for h in range(1, N_IDX_HEADS):
                acc = acc + jnp.maximum(d_all[:, h * QB:(h + 1) * QB], 0.0) * iw[h:h + 1, :]
            score_ref[pl.ds(off, QB), :] = jnp.where(s_loc + off <= t_glob, acc + 0.0, F32_LOWEST)
        return c

    block_loop(score_body, 0)

    def count_where(pred):
        def body(jb0, nb, acc):
            for sb in range(nb):
                off = blk(jb0 + sb)
                acc = acc + _colsum8(jnp.where(pred(score_ref[pl.ds(off, QB), :], off), 1.0, 0.0))
            return acc
        acc = block_loop(body, jnp.zeros((SUBLANES, QB), f32))
        return jnp.sum(acc, axis=0, keepdims=True)

    kf = float(k_sel)

    def search():
        c0 = count_where(lambda sc, off: sc >= 0.0)
        cand0 = jnp.where(c0 >= kf, jnp.int32(0), jnp.int32(-2 ** 31))

        def bit_body(it, cand):
            trial = cand + lax.shift_left(jnp.int32(1), 30 - it)
            tf = _key_to_f32(trial)
            cnt = count_where(lambda sc, off: sc >= tf)
            return jnp.where(cnt >= kf, trial, cand)

        cand = lax.fori_loop(0, 31, bit_body, cand0)
        thr = _key_to_f32(cand)
        n_gt = count_where(lambda sc, off: sc > thr)
        n_eq = count_where(lambda sc, off: sc == thr)
        need = kf - n_gt

        def tie_search():
            def tbody(it, xcut):
                trial = xcut + lax.shift_left(jnp.int32(1), idx_bits - 1 - it)
                cnt = count_where(lambda sc, off: (sc == thr) & (s_loc + off < trial))
                return jnp.where(cnt < need, trial, xcut)
            return lax.fori_loop(0, idx_bits, tbody, jnp.zeros((1, QB), jnp.int32))

        any_extra = jnp.max(n_eq - need) > 0.0
        xcut = lax.cond(any_extra, tie_search, lambda: jnp.full((1, QB), 2 ** idx_bits - 1, jnp.int32))
        return thr, xcut

    def no_search():
        return jnp.full((1, QB), F32_LOWEST, f32), jnp.full((1, QB), 2 ** idx_bits - 1, jnp.int32)

    thr, xcut = lax.cond((i + 1) * QB > k_sel, search, no_search)

    def mask_body(jb0, nb, c):
        for sb in range(nb):
            off = blk(jb0 + sb)
            sc = score_ref[pl.ds(off, QB), :]
            s_glob = s_loc + off
            keep = ((sc > thr) | ((sc == thr) & (s_glob <= xcut))) & (s_glob <= t_glob)
            mask_ref[pl.ds(off, QB), :] = jnp.where(keep, 0.0, -jnp.inf)
        return c

    block_loop(mask_body, 0)

    def p1_body(jb0, nb, m8):
        m8 = list(m8)
        lg_blk = _dot_nt(ckv_ref[pl.ds(blk(jb0), nb * QB), :], qlat_ref[...])
        for sb in range(nb):
            off = blk(jb0 + sb)
            lg = lg_blk[sb * QB:(sb + 1) * QB, :]
            msk = mask_ref[pl.ds(off, QB), :]
            bsel = jnp.clip(jb0 + sb - i + 2, 0, 2)
            for h in range(N_HEADS_A):
                lgh = lg[:, h * QB:(h + 1) * QB] + bias_ref[bsel, h] + msk
                logit_ref[pl.ds(off, QB), h * QB:(h + 1) * QB] = lgh
                m8[h] = jnp.maximum(m8[h], _colmax8(lgh))
        return tuple(m8)

    m8 = block_loop(p1_body, tuple(jnp.full((SUBLANES, QB), -jnp.inf, f32) for _ in range(N_HEADS_A)))
    m_row = [jnp.max(m, axis=0, keepdims=True) for m in m8]

    acc_ref[...] = jnp.zeros_like(acc_ref)

    def p2_body(jb0, nb, l8):
        l8 = list(l8)
        off = blk(jb0)
        rows = nb * QB
        ps = []
        for h in range(N_HEADS_A):
            p = jnp.exp(logit_ref[pl.ds(off, rows), h * QB:(h + 1) * QB] - m_row[h])
            l8[h] = l8[h] + jnp.sum(p.reshape(rows // SUBLANES, SUBLANES, QB), axis=0)
            ps.append(p.astype(bf))
        acc_ref[...] += _dot(ckvt_ref[0, :, pl.ds(off, rows)], jnp.concatenate(ps, axis=1))
        return tuple(l8)

    l8 = block_loop(p2_body, tuple(jnp.zeros((SUBLANES, QB), f32) for _ in range(N_HEADS_A)))

    outs = []
    for h in range(N_HEADS_A):
        l_row = jnp.sum(l8[h], axis=0, keepdims=True)
        o_lat_t = (acc_ref[:, h * QB:(h + 1) * QB] / l_row).astype(bf)
        outs.append(_dot(wuvt_ref[h], o_lat_t))
    o_ref[...] = jnp.concatenate(outs, axis=0).T.astype(o_ref.dtype)


def _dsa(cq, iwt, kidx, ckv, ckvt, w_qidx, w_uq, w_uk_h, w_uvt_h, bias_tiles, B, S):
    T = cq.shape[0]
    assert S % QB == 0 and QB >= REL_MAX_DIST
    nq = S // QB
    k_sel = min(TOPK_MAX, S // 4)
    idx_bits = max(1, (S - 1).bit_length())
    c2 = lambda b, i: (0, 0)
    c3 = lambda b, i: (0, 0, 0)
    return pl.pallas_call(
        functools.partial(_dsa_kernel, k_sel=k_sel, idx_bits=idx_bits),
        grid=(B, nq),
        in_specs=[
            pl.BlockSpec((QB, Q_RANK), lambda b, i: (b * nq + i, 0)),
            pl.BlockSpec((1, N_IDX_HEADS, QB), lambda b, i: (b, 0, i)),
            pl.BlockSpec((S, IDX_DIM), lambda b, i: (b, 0)),
            pl.BlockSpec((S, KV_RANK), lambda b, i: (b, 0)),
            pl.BlockSpec((1, KV_RANK, S), lambda b, i: (b, 0, 0)),
            pl.BlockSpec(w_qidx.shape, c2),
            pl.BlockSpec(w_uq.shape, c2),
            pl.BlockSpec(w_uk_h.shape, c3),
            pl.BlockSpec(w_uvt_h.shape, c3),
            pl.BlockSpec(bias_tiles.shape, lambda b, i: (0, 0, 0, 0)),
        ],
        out_specs=pl.BlockSpec((QB, MIX_A), lambda b, i: (b * nq + i, 0)),
        out_shape=jax.ShapeDtypeStruct((T, MIX_A), MXU_DTYPE),
        scratch_shapes=[
            pltpu.VMEM((Q_RANK, N_HEADS_A * KV_RANK), MXU_DTYPE),
            pltpu.VMEM((N_IDX_HEADS * QB, IDX_DIM), MXU_DTYPE),
            pltpu.VMEM((N_HEADS_A * QB, KV_RANK), MXU_DTYPE),
            pltpu.VMEM((S, QB), jnp.float32),
            pltpu.VMEM((S, QB), jnp.float32),
            pltpu.VMEM((S, N_HEADS_A * QB), jnp.float32),
            pltpu.VMEM((KV_RANK, N_HEADS_A * QB), jnp.float32),
        ],
        compiler_params=_cparams(("arbitrary", "arbitrary")),
        name="dsa",
    )(cq, iwt, kidx, ckv, ckvt, w_qidx, w_uq, w_uk_h, w_uvt_h, bias_tiles)


def _layer_norm(xf, g, b):
    mu = jnp.mean(xf, axis=-1, keepdims=True)
    xc = xf - mu
    var = jnp.mean(xc * xc, axis=-1, keepdims=True)
    return xc * lax.rsqrt(var + LN_EPS) * g + b


def _rank_rows(v, n):
    ri = lax.broadcasted_iota(jnp.int32, v.shape, 0)
    rank = jnp.zeros(v.shape, jnp.float32)
    for r2 in range(n):
        row = v[r2:r2 + 1, :]
        beats = (row > v) | ((row == v) & (ri > r2))
        rank = rank + jnp.where(beats, 1.0, 0.0)
    return rank


def _pack_factor():
    return 4 // jnp.dtype(MXU_DTYPE).itemsize


def _pack_rows(x):
    if _pack_factor() == 1:
        return pltpu.bitcast(x, jnp.int32)
    half = x.shape[1] // 2
    b = pltpu.bitcast(x.astype(MXU_DTYPE).astype(jnp.float32), jnp.int32)
    return b[:, half:] | lax.shift_right_logical(b[:, :half], jnp.int32(16))


_HIGH_HALF = -(1 << 16)


def _unpack_rows_f32(p):
    if _pack_factor() == 1:
        return [pltpu.bitcast(p, jnp.float32)]
    lo = pltpu.bitcast(lax.shift_left(p, jnp.int32(16)), jnp.float32)
    hi = pltpu.bitcast(p & jnp.int32(_HIGH_HALF), jnp.float32)
    return [lo, hi]


def _unpack_rows(p):
    return [v.astype(MXU_DTYPE) for v in _unpack_rows_f32(p)]


def _mix_router_kernel(x_ref, ya_ref, yb_ref, yc_ref, wo_ref, g_ref, b_ref, wrt_ref, rb_ref, exp_ref,
                       x1_ref, x1p_ref, sel_ref, w_ref, pos_ref, cnt_ref, base_ref, *, tm):
    step = pl.program_id(0)
    f32 = jnp.float32

    @pl.when(step == 0)
    def _():
        base_ref[...] = jnp.zeros_like(base_ref)

    mix = _dot(ya_ref[...], wo_ref[0:MIX_A, :])
    mix = mix + _dot(yb_ref[...], wo_ref[MIX_A:MIX_A + CONV_CH, :])
    mix = mix + _dot(yc_ref[...], wo_ref[MIX_A + CONV_CH:, :])
    x1 = _layer_norm(ALPHA * x_ref[...] + mix, g_ref[...], b_ref[...])
    x1_ref[...] = x1
    x1p_ref[...] = _pack_rows(x1)

    lg = lax.dot_general(wrt_ref[...], x1, _NT, precision=lax.Precision.HIGHEST, preferred_element_type=f32)
    s = 1.0 / (1.0 + jnp.exp(-lg))
    sc = s + rb_ref[...]

    g3 = sc.reshape(N_GROUPS, GROUP_SIZE, tm)
    m1 = jnp.max(g3, axis=1, keepdims=True)
    is_m1 = g3 == m1
    n_m1 = jnp.sum(jnp.where(is_m1, 1.0, 0.0), axis=1, keepdims=True)
    m2 = jnp.max(jnp.where(is_m1, -jnp.inf, g3), axis=1, keepdims=True)
    gscore = (m1 + jnp.where(n_m1 > 1.0, m1, m2)).reshape(N_GROUPS, tm)
    gsel = jnp.where(_rank_rows(gscore, N_GROUPS) < float(TOPK_GROUPS), 1.0, 0.0)
    emask = _dot(exp_ref[...], gsel.astype(MXU_DTYPE)) > 0.5
    masked = jnp.where(emask, sc, -jnp.inf)
    sel = (_rank_rows(masked, N_EXPERTS) < float(TOP_K)) & emask
    self_ = jnp.where(sel, 1.0, 0.0)
    top_s = jnp.where(sel, s, 0.0)
    w = top_s / jnp.sum(top_s, axis=0, keepdims=True) * ROUTED_SCALE

    t_r = lax.broadcasted_iota(jnp.int32, (tm, tm), 0)
    t_c = lax.broadcasted_iota(jnp.int32, (tm, tm), 1)
    upper = jnp.where(t_r < t_c, 1.0, 0.0).astype(MXU_DTYPE)
    pref = _dot(self_.astype(MXU_DTYPE), upper)
    base = base_ref[...]
    sel_ref[...] = self_
    w_ref[...] = w
    pos_ref[...] = base + pref
    base = base + jnp.sum(self_, axis=1, keepdims=True)
    base_ref[...] = base
    cnt_ref[...] = jnp.broadcast_to(base, cnt_ref.shape)


def _mix_router(x2, ya, yb, yc, w_out, ln_g, ln_b, w_router_t, router_bias, tm):
    T, D = x2.shape
    E = N_EXPERTS
    expand = (jnp.arange(E)[:, None] // GROUP_SIZE == jnp.arange(N_GROUPS)[None, :]).astype(MXU_DTYPE)
    row = lambda i: (i, 0)
    col = lambda i: (0, i)
    c2 = lambda i: (0, 0)
    f32 = jnp.float32
    return pl.pallas_call(
        functools.partial(_mix_router_kernel, tm=tm),
        grid=(T // tm,),
        in_specs=[
            pl.BlockSpec((tm, D), row),
            pl.BlockSpec((tm, MIX_A), row),
            pl.BlockSpec((tm, CONV_CH), row),
            pl.BlockSpec((tm, MIX_C), row),
            pl.BlockSpec(w_out.shape, c2),
            pl.BlockSpec((1, D), c2),
            pl.BlockSpec((1, D), c2),
            pl.BlockSpec((E, D), c2),
            pl.BlockSpec((E, 1), c2),
            pl.BlockSpec((E, N_GROUPS), c2),
        ],
        out_specs=[
            pl.BlockSpec((tm, D), row),
            pl.BlockSpec((tm, D // _pack_factor()), row),
            pl.BlockSpec((E, tm), col),
            pl.BlockSpec((E, tm), col),
            pl.BlockSpec((E, tm), col),
            pl.BlockSpec((E, LANES), c2),
        ],
        out_shape=[
            jax.ShapeDtypeStruct((T, D), f32),
            jax.ShapeDtypeStruct((T, D // _pack_factor()), jnp.int32),
            jax.ShapeDtypeStruct((E, T), f32),
            jax.ShapeDtypeStruct((E, T), f32),
            jax.ShapeDtypeStruct((E, T), f32),
            jax.ShapeDtypeStruct((E, LANES), f32),
        ],
        scratch_shapes=[pltpu.VMEM((E, 1), f32)],
        compiler_params=_cparams(("arbitrary",)),
        name="mix_router",
    )(x2, ya, yb, yc, w_out, ln_g, ln_b, w_router_t, router_bias, expand)


def _compact_kernel(sel_ref, w_ref, pos_ref, pstart_ref, low_ref, dest_ref, wk_ref):
    sel = sel_ref[...]
    on = sel > 0.5
    rank = _dot(low_ref[...], sel.astype(MXU_DTYPE))
    row = pstart_ref[...] + pos_ref[...]
    w = w_ref[...]
    dests, ws = [], []
    for k in range(TOP_K):
        m = on & (rank == float(k))
        dests.append(jnp.sum(jnp.where(m, row, 0.0), axis=0, keepdims=True))
        ws.append(jnp.sum(jnp.where(m, w, 0.0), axis=0, keepdims=True))
    dest_ref[...] = jnp.concatenate(dests, axis=0).astype(jnp.int32)
    wk_ref[...] = jnp.concatenate(ws, axis=0)


def _compact(sel_t, w_t, pos_t, pad_start, tm):
    E, T = sel_t.shape
    lower = (jnp.arange(E)[None, :] < jnp.arange(E)[:, None]).astype(MXU_DTYPE)
    col = lambda i: (0, i)
    c2 = lambda i: (0, 0)
    return pl.pallas_call(
        _compact_kernel,
        grid=(T // tm,),
        in_specs=[pl.BlockSpec((E, tm), col), pl.BlockSpec((E, tm), col), pl.BlockSpec((E, tm), col),
                  pl.BlockSpec((E, 1), c2), pl.BlockSpec((E, E), c2)],
        out_specs=[pl.BlockSpec((TOP_K, tm), col), pl.BlockSpec((TOP_K, tm), col)],
        out_shape=[jax.ShapeDtypeStruct((TOP_K, T), jnp.int32), jax.ShapeDtypeStruct((TOP_K, T), jnp.float32)],
        compiler_params=_cparams(("arbitrary",)),
        name="route_compact",
    )(sel_t, w_t, pos_t, pad_start, lower)


def _row_copy(src, s, dst, d, sem):
    return pltpu.make_async_copy(src.at[pl.ds(s, 1)], dst.at[pl.ds(d, 1)], sem)


def _dispatch_kernel(flo_ref, fhi_ref, dest_ref, x_ref, xs_hbm, zero_ref, sem, zsem, *, td):
    step = pl.program_id(0)

    @pl.when(step == 0)
    def _():
        zero_ref[...] = jnp.zeros_like(zero_ref)

        def per_expert(fn):
            def ebody(e, c):
                lax.fori_loop(flo_ref[e], fhi_ref[e], lambda r, c2: (fn(r), c2)[1], 0)
                return c
            lax.fori_loop(0, N_EXPERTS, ebody, 0)

        per_expert(lambda r: _row_copy(zero_ref, 0, xs_hbm, r, zsem).start())
        per_expert(lambda r: _row_copy(zero_ref, 0, xs_hbm, r, zsem).wait())

    def issue(r, c):
        for k in range(TOP_K):
            _row_copy(x_ref, r, xs_hbm, dest_ref[k, r], sem).start()
        return c

    def drain(r, c):
        for k in range(TOP_K):
            _row_copy(x_ref, r, xs_hbm, dest_ref[k, r], sem).wait()
        return c

    lax.fori_loop(0, td, issue, 0)
    lax.fori_loop(0, td, drain, 0)


def _dispatch(dest_t, x1p, fill_lo, fill_hi, n_rows, td):
    T, W = x1p.shape
    return pl.pallas_call(
        functools.partial(_dispatch_kernel, td=td),
        grid_spec=pltpu.PrefetchScalarGridSpec(
            num_scalar_prefetch=2,
            grid=(T // td,),
            in_specs=[
                pl.BlockSpec((TOP_K, td), lambda i, lo, hi: (0, i), memory_space=pltpu.SMEM),
                pl.BlockSpec((td, W), lambda i, lo, hi: (i, 0)),
            ],
            out_specs=pl.BlockSpec(memory_space=pl.ANY),
            scratch_shapes=[pltpu.VMEM((SUBLANES, W), x1p.dtype),
                            pltpu.SemaphoreType.DMA, pltpu.SemaphoreType.DMA],
        ),
        out_shape=jax.ShapeDtypeStruct((n_rows, W), x1p.dtype),
        compiler_params=_cparams(("arbitrary",)),
        name="dispatch",
    )(fill_lo, fill_hi, dest_t, x1p)


def _silu(g):
    return g / (1.0 + jnp.exp(-g))


def _expert_kernel(be_ref, nv_ref, nu_ref, xs_ref, wg_ref, wu_ref, wd_ref, ys_ref, wgb_ref, wub_ref, wdb_ref):
    i = pl.program_id(0)

    @pl.when((i == 0) | (be_ref[i] != be_ref[jnp.maximum(i - 1, 0)]))
    def _():
        wgb_ref[...] = wg_ref[0].astype(MXU_DTYPE)
        wub_ref[...] = wu_ref[0].astype(MXU_DTYPE)
        wdb_ref[...] = wd_ref[0].astype(MXU_DTYPE)

    @pl.when(i < nu_ref[0])
    def _():
        live = lax.broadcasted_iota(jnp.int32, (ROW_BLOCK, 1), 0) < nv_ref[i]
        parts = [jnp.where(live, v, jnp.zeros_like(v)) for v in _unpack_rows(xs_ref[...])]
        dk = wgb_ref.shape[0] // len(parts)

        def proj(w_ref):
            acc = _dot(parts[0], w_ref[0:dk, :])
            for n in range(1, len(parts)):
                acc = acc + _dot(parts[n], w_ref[n * dk:(n + 1) * dk, :])
            return acc

        a = (_silu(proj(wgb_ref)) * proj(wub_ref)).astype(MXU_DTYPE)
        ys_ref[...] = _pack_rows(_dot(a, wdb_ref[...]))


def _experts(xs, block_e, block_valid, n_used, w_gate, w_up, w_down):
    n_rows, W = xs.shape
    D = w_gate.shape[1]
    n_blocks = n_rows // ROW_BLOCK
    blk = lambda i, be, nv, nu: (jnp.minimum(i, nu[0] - 1), 0)
    wsel = lambda i, be, nv, nu: (be[i], 0, 0)
    return pl.pallas_call(
        _expert_kernel,
        grid_spec=pltpu.PrefetchScalarGridSpec(
            num_scalar_prefetch=3,
            grid=(n_blocks,),
            in_specs=[
                pl.BlockSpec((ROW_BLOCK, W), blk),
                pl.BlockSpec((1, D, D_EXPERT), wsel),
                pl.BlockSpec((1, D, D_EXPERT), wsel),
                pl.BlockSpec((1, D_EXPERT, D), wsel),
            ],
            out_specs=pl.BlockSpec((ROW_BLOCK, W), blk),
            scratch_shapes=[pltpu.VMEM((D, D_EXPERT), MXU_DTYPE), pltpu.VMEM((D, D_EXPERT), MXU_DTYPE),
                            pltpu.VMEM((D_EXPERT, D), MXU_DTYPE)],
        ),
        out_shape=jax.ShapeDtypeStruct((n_rows, W), xs.dtype),
        compiler_params=_cparams(("arbitrary",)),
        name="experts",
    )(block_e, block_valid, n_used, xs, w_gate, w_up, w_down)


SC_CORES = 2
SC_SUBCORES = 16
SC_GATHER_ROWS = 64
COMBINE_CHUNKS = 4


def _sc_gather_rows(table, idx):
    n = idx.shape[0]
    w = table.shape[1]
    n_workers = SC_CORES * SC_SUBCORES
    per_worker = n // n_workers
    assert n % n_workers == 0 and per_worker % SC_GATHER_ROWS == 0
    mesh = plsc.VectorSubcoreMesh(core_axis_name="c", subcore_axis_name="s")

    @functools.partial(
        pl.kernel, mesh=mesh,
        out_type=jax.ShapeDtypeStruct((n, w), table.dtype),
        scratch_types=[
            pltpu.VMEM((2, SC_GATHER_ROWS), jnp.int32),
            pltpu.VMEM((2, SC_GATHER_ROWS, w), table.dtype),
            pltpu.SemaphoreType.DMA((2,)),
        ],
        name="sc_gather_rows",
    )
    def gather(table_hbm, idx_hbm, out_hbm, idx_v, rows_v, sem):
        wid = lax.axis_index("s") * SC_CORES + lax.axis_index("c")
        base = wid * per_worker
        n_steps = per_worker // SC_GATHER_ROWS

        def gather_copy(slot):
            return pltpu.make_async_copy(table_hbm.at[idx_v.at[slot]], rows_v.at[slot], sem.at[slot])

        def start(step, slot):
            pltpu.sync_copy(idx_hbm.at[pl.ds(base + step * SC_GATHER_ROWS, SC_GATHER_ROWS)], idx_v.at[slot])
            gather_copy(slot).start()

        start(0, 0)

        @pl.loop(0, n_steps, step=2)
        def _(g):
            for slot in range(2):
                step = g + slot

                @pl.when(step + 1 < n_steps)
                def _():
                    start(step + 1, 1 - slot)

                gather_copy(slot).wait()
                pltpu.sync_copy(rows_v.at[slot], out_hbm.at[pl.ds(base + step * SC_GATHER_ROWS, SC_GATHER_ROWS)])

    return gather(table, idx)


SC_SCATTER_ROWS = 64


def _sc_scatter_rows(rows, idx3, n_out):
    n_src, w = rows.shape
    n_chunks, n_dst, batch = idx3.shape
    n_workers = SC_CORES * SC_SUBCORES
    assert batch == SC_SCATTER_ROWS and n_chunks * batch == n_src and n_chunks % (2 * n_workers) == 0
    per_worker = n_chunks // n_workers
    mesh = plsc.VectorSubcoreMesh(core_axis_name="c", subcore_axis_name="s")

    @functools.partial(
        pl.kernel, mesh=mesh,
        out_type=jax.ShapeDtypeStruct((n_out, w), rows.dtype),
        scratch_types=[
            pltpu.VMEM((2, n_dst, batch), jnp.int32),
            pltpu.VMEM((2, batch, w), rows.dtype),
            pltpu.SemaphoreType.DMA((2,)),
            pltpu.SemaphoreType.DMA,
        ],
        name="sc_scatter_rows",
    )
    def scatter(rows_hbm, idx_hbm, out_hbm, idx_v, rows_v, load_sem, store_sem):
        wid = lax.axis_index("s") * SC_CORES + lax.axis_index("c")

        def load_copy(step, slot):
            c = wid * per_worker + step
            return pltpu.make_async_copy(rows_hbm.at[pl.ds(c * batch, batch)], rows_v.at[slot], load_sem.at[slot])

        def load(step, slot):
            pltpu.sync_copy(idx_hbm.at[wid * per_worker + step], idx_v.at[slot])
            load_copy(step, slot).start()

        def store_copy(slot, k):
            return pltpu.make_async_copy(rows_v.at[slot], out_hbm.at[idx_v.at[slot].at[k]], store_sem)

        load(0, 0)

        @pl.loop(0, per_worker, step=2)
        def _(g):
            for slot in range(2):
                step = g + slot

                @pl.when(step + 1 < per_worker)
                def _():
                    load(step + 1, 1 - slot)

                load_copy(step, slot).wait()
                for k in range(n_dst):
                    store_copy(slot, k).start()
                for k in range(n_dst):
                    store_copy(slot, k).wait()

    return scatter(rows, idx3)


def _shared_kernel(x1_ref, wsg_ref, wsu_ref, wsd_ref, o_ref):
    xb = x1_ref[...].astype(MXU_DTYPE)
    a = (_silu(_dot(xb, wsg_ref[...])) * _dot(xb, wsu_ref[...])).astype(MXU_DTYPE)
    o_ref[...] = _dot(a, wsd_ref[...])


def _shared_expert(x1, w_sg, w_su, w_sd, tm):
    T, D = x1.shape
    row = lambda i: (i, 0)
    c2 = lambda i: (0, 0)
    return pl.pallas_call(
        _shared_kernel,
        grid=(T // tm,),
        in_specs=[pl.BlockSpec((tm, D), row), pl.BlockSpec(w_sg.shape, c2), pl.BlockSpec(w_su.shape, c2),
                  pl.BlockSpec(w_sd.shape, c2)],
        out_specs=pl.BlockSpec((tm, D), row),
        out_shape=jax.ShapeDtypeStruct((T, D), jnp.float32),
        compiler_params=_cparams(("arbitrary",)),
        name="shared_expert",
    )(x1, w_sg, w_su, w_sd)


def _combine2_kernel(wk_ref, x1_ref, g_ref_rows, wsg_ref, wsu_ref, wsd_ref, g_ref, b_ref, o_ref):
    x1 = x1_ref[...]
    xb = x1.astype(MXU_DTYPE)
    a = (_silu(_dot(xb, wsg_ref[...])) * _dot(xb, wsu_ref[...])).astype(MXU_DTYPE)
    shared = _dot(a, wsd_ref[...])
    wk = wk_ref[...].T
    groups = [wk[:, 0:1] * v for v in _unpack_rows_f32(g_ref_rows[0])]
    for k in range(1, TOP_K):
        groups = [g + wk[:, k:k + 1] * v for g, v in zip(groups, _unpack_rows_f32(g_ref_rows[k]))]
    routed = jnp.concatenate(groups, axis=1)
    o_ref[...] = _layer_norm(ALPHA * x1 + (routed + shared), g_ref[...], b_ref[...])


def _combine2_kernel_into(wk_ref, x1_ref, g_ref_rows, wsg_ref, wsu_ref, wsd_ref, g_ref, b_ref, prev_ref, o_ref):
    del prev_ref
    _combine2_kernel(wk_ref, x1_ref, g_ref_rows, wsg_ref, wsu_ref, wsd_ref, g_ref, b_ref, o_ref)


def _combine2(wk_t, x1, gathered, w_sg, w_su, w_sd, ln_g, ln_b, tc, chunk, prev):
    T, D = x1.shape
    _, t_chunk, W = gathered.shape
    base = chunk * (t_chunk // tc)
    row = lambda i: (base + i, 0)
    c2 = lambda i: (0, 0)
    in_specs = [
        pl.BlockSpec((TOP_K, tc), lambda i: (0, base + i)),
        pl.BlockSpec((tc, D), row),
        pl.BlockSpec((TOP_K, tc, W), lambda i: (0, i, 0)),
        pl.BlockSpec(w_sg.shape, c2),
        pl.BlockSpec(w_su.shape, c2),
        pl.BlockSpec(w_sd.shape, c2),
        pl.BlockSpec((1, D), c2),
        pl.BlockSpec((1, D), c2),
    ]
    args = [wk_t, x1, gathered, w_sg, w_su, w_sd, ln_g, ln_b]
    if prev is None:
        body, aliases = _combine2_kernel, {}
    else:
        body, aliases = _combine2_kernel_into, {len(args): 0}
        in_specs.append(pl.BlockSpec(memory_space=pl.ANY))
        args.append(prev)
    return pl.pallas_call(
        body,
        grid=(t_chunk // tc,),
        in_specs=in_specs,
        out_specs=pl.BlockSpec((tc, D), row),
        out_shape=jax.ShapeDtypeStruct((T, D), jnp.float32),
        input_output_aliases=aliases,
        compiler_params=_cparams(("arbitrary",)),
        name="combine",
    )(*args)


def _combine_kernel(dest_ref, wk_ref, x1_ref, ys_hbm, wsg_ref, wsu_ref, wsd_ref, g_ref, b_ref,
                    o_ref, buf_ref, sem, *, tc):
    def issue(r, c):
        for k in range(TOP_K):
            _row_copy(ys_hbm, dest_ref[k, r], buf_ref.at[k], r, sem).start()
        return c

    def drain(r, c):
        for k in range(TOP_K):
            _row_copy(ys_hbm, dest_ref[k, r], buf_ref.at[k], r, sem).wait()
        return c

    lax.fori_loop(0, tc, issue, 0)
    x1 = x1_ref[...]
    xb = x1.astype(MXU_DTYPE)
    a = (_silu(_dot(xb, wsg_ref[...])) * _dot(xb, wsu_ref[...])).astype(MXU_DTYPE)
    shared = _dot(a, wsd_ref[...])
    lax.fori_loop(0, tc, drain, 0)
    wk = wk_ref[...]
    groups = [wk[:, 0:1] * v for v in _unpack_rows_f32(buf_ref[0])]
    for k in range(1, TOP_K):
        groups = [g + wk[:, k:k + 1] * v for g, v in zip(groups, _unpack_rows_f32(buf_ref[k]))]
    routed = jnp.concatenate(groups, axis=1)
    o_ref[...] = _layer_norm(ALPHA * x1 + (routed + shared), g_ref[...], b_ref[...])


def _combine(dest_t, wk, x1, ys, w_sg, w_su, w_sd, ln_g, ln_b, tc):
    T, D = x1.shape
    row = lambda i: (i, 0)
    c2 = lambda i: (0, 0)
    return pl.pallas_call(
        functools.partial(_combine_kernel, tc=tc),
        grid=(T // tc,),
        in_specs=[
            pl.BlockSpec((TOP_K, tc), lambda i: (0, i), memory_space=pltpu.SMEM),
            pl.BlockSpec((tc, TOP_K), row),
            pl.BlockSpec((tc, D), row),
            pl.BlockSpec(memory_space=pl.ANY),
            pl.BlockSpec(w_sg.shape, c2),
            pl.BlockSpec(w_su.shape, c2),
            pl.BlockSpec(w_sd.shape, c2),
            pl.BlockSpec((1, D), c2),
            pl.BlockSpec((1, D), c2),
        ],
        out_specs=pl.BlockSpec((tc, D), row),
        out_shape=jax.ShapeDtypeStruct((T, D), jnp.float32),
        scratch_shapes=[pltpu.VMEM((TOP_K, tc, ys.shape[1]), ys.dtype), pltpu.SemaphoreType.DMA],
        compiler_params=_cparams(("arbitrary",)),
        name="combine",
    )(dest_t, wk, x1, ys, w_sg, w_su, w_sd, ln_g, ln_b)


def _split_w_in(w_in):
    bf = MXU_DTYPE
    o_kv = Q_RANK
    o_ki = o_kv + KV_RANK
    o_iw = o_ki + IDX_DIM
    o_rest = o_iw + N_IDX_HEADS
    w_main = jnp.concatenate([w_in[:, :o_ki], w_in[:, o_rest:]], axis=1).astype(bf)
    w_small = jnp.pad(w_in[:, o_ki:o_rest], ((0, 0), (0, LANES - IDX_DIM - N_IDX_HEADS))).astype(bf)
    return w_main, w_small


def _stages(x, mem, w_in, q_norm_g, kv_norm_g, w_uq, w_uk, w_uv, w_qidx, rel_bias, conv_w, w_mem_k, w_mem_v, w_out, ln1_g, ln1_b, w_router, router_bias, w_e_gate, w_e_up, w_e_down, w_s_gate, w_s_up, w_s_down, ln2_g, ln2_b, upto=None):
    B, S, D = x.shape
    T = B * S
    bf = MXU_DTYPE
    l = 0
    res = {}
    x2 = x.reshape(T, D)
    w_main, w_small = _split_w_in(w_in[l])
    cq, ckv, ckvt, kidx, iwt, yb, yc = _proj(
        x2, mem, w_main, w_small, q_norm_g[l].reshape(1, -1), kv_norm_g[l].reshape(1, -1), conv_w[l],
        w_mem_k[l].astype(bf), w_mem_v[l].astype(bf), B, S, tm=min(512, S))
    res.update(c_q=cq, c_kv=ckv, k_idx=kidx, y_b=yb, y_c=yc,
               idx_w=jnp.swapaxes(iwt, 1, 2) / (N_IDX_HEADS ** -0.5 * IDX_DIM ** -0.5))
    if upto == "proj":
        return res
    bias_t = _bias_tiles(rel_bias)
    ya = _dsa(cq, iwt, kidx, ckv, ckvt,
              w_qidx[l].reshape(Q_RANK, -1).astype(bf), w_uq[l].reshape(Q_RANK, -1).astype(bf),
              jnp.transpose(w_uk[l], (1, 0, 2)).astype(bf), jnp.transpose(w_uv[l], (1, 2, 0)).astype(bf),
              bias_t, B, S)
    res.update(y_a=ya)
    if upto == "dsa":
        return res

    x1, x1p, sel_t, w_t, pos_t, cnt = _mix_router(
        x2, ya, yb, yc, w_out[l].astype(bf), ln1_g[l].reshape(1, -1), ln1_b[l].reshape(1, -1),
        w_router[l].T, router_bias[l].reshape(-1, 1), tm=min(512, T))
    res.update(x1=x1)

    counts = cnt[:, 0].astype(jnp.int32)
    padded = (counts + ROW_BLOCK - 1) // ROW_BLOCK * ROW_BLOCK
    pad_end = jnp.cumsum(padded)
    pad_start = pad_end - padded
    n_blocks = -(-(T * TOP_K) // ROW_BLOCK) + N_EXPERTS
    n_rows = n_blocks * ROW_BLOCK
    block_start = jnp.arange(n_blocks, dtype=jnp.int32) * ROW_BLOCK
    block_e = jnp.minimum(jnp.sum((pad_end[None, :] <= block_start[:, None]).astype(jnp.int32), axis=1),
                          N_EXPERTS - 1)
    n_used = (pad_end[-1:] // ROW_BLOCK).astype(jnp.int32)

    dest_t, wk_t = _compact(sel_t, w_t, pos_t, pad_start.astype(jnp.float32).reshape(-1, 1), tm=min(512, T))
    block_valid = jnp.clip((pad_start + counts)[block_e] - block_start, 0, ROW_BLOCK).astype(jnp.int32)
    bt = SC_SCATTER_ROWS
    idx3 = jnp.transpose(dest_t.reshape(TOP_K, T // bt, bt), (1, 0, 2))
    xs = _sc_scatter_rows(x1p, idx3, n_rows)
    ys = _experts(xs, block_e, block_valid, n_used, w_e_gate[l], w_e_up[l], w_e_down[l])
    n_chunks = COMBINE_CHUNKS if T % (COMBINE_CHUNKS * 256) == 0 else 1
    t_chunk = T // n_chunks
    out = None
    for c in range(n_chunks):
        idx_c = dest_t[:, c * t_chunk:(c + 1) * t_chunk].reshape(-1)
        gathered = _sc_gather_rows(ys, idx_c).reshape(TOP_K, t_chunk, -1)
        out = _combine2(wk_t, x1, gathered, w_s_gate[l].astype(bf), w_s_up[l].astype(bf), w_s_down[l].astype(bf),
                        ln2_g[l].reshape(1, -1), ln2_b[l].reshape(1, -1), tc=min(256, t_chunk), chunk=c, prev=out)
    res.update(out=out.reshape(B, S, D))
    return res


def kernel(x, mem, w_in, q_norm_g, kv_norm_g, w_uq, w_uk, w_uv, w_qidx, rel_bias, conv_w, w_mem_k, w_mem_v, w_out, ln1_g, ln1_b, w_router, router_bias, w_e_gate, w_e_up, w_e_down, w_s_gate, w_s_up, w_s_down, ln2_g, ln2_b):
    return _stages(x, mem, w_in, q_norm_g, kv_norm_g, w_uq, w_uk, w_uv, w_qidx, rel_bias, conv_w, w_mem_k, w_mem_v, w_out, ln1_g, ln1_b, w_router, router_bias, w_e_gate, w_e_up, w_e_down, w_s_gate, w_s_up, w_s_down, ln2_g, ln2_b)["out"]
```

```python
import functools
import math

import jax
import jax.numpy as jnp
from jax import lax
from jax.experimental import pallas as pl
from jax.experimental.pallas import tpu as pltpu
from jax.experimental.pallas import tpu_sc as plsc

N_HEADS_A = 8
HEAD_DIM = 64
Q_RANK = 256
KV_RANK = 128
N_IDX_HEADS = 8
IDX_DIM = 64
TOPK_MAX = 256
REL_BUCKETS = 32
REL_MAX_DIST = 128
CONV_CH = 256
CONV_WIDTH = 3
N_MEM_HEADS = 4
MIX_A = N_HEADS_A * HEAD_DIM
MIX_C = N_MEM_HEADS * HEAD_DIM
N_EXPERTS = 64
N_GROUPS = 8
GROUP_SIZE = N_EXPERTS // N_GROUPS
TOPK_GROUPS = 4
TOP_K = 8
D_EXPERT = 256
ROUTED_SCALE = 2.5
MOE_BLOCK = 256
DEPTH = 1
ALPHA = (2.0 * DEPTH) ** 0.25
LN_EPS = 1e-5
RMS_EPS = 1e-6

LANES = 128
SUBLANES = 8
QB = 128
F32_LOWEST = -3.4028234663852886e38
VMEM_LIMIT = 56 * 1024 * 1024
MXU_DTYPE = jnp.bfloat16
ROW_BLOCK = 2048
ROW_SUB = 512

_NT = (((1,), (1,)), ((), ()))


def _dot(a, b):
    return jnp.dot(a, b, preferred_element_type=jnp.float32)


def _dot_nt(a, b):
    return lax.dot_general(a, b, _NT, preferred_element_type=jnp.float32)


def _cparams(sem):
    return pltpu.CompilerParams(dimension_semantics=sem, vmem_limit_bytes=VMEM_LIMIT)


def _bias_kernel(rb_ref, o_ref):
    s = lax.broadcasted_iota(jnp.int32, (QB, QB), 0)
    t = lax.broadcasted_iota(jnp.int32, (QB, QB), 1)
    max_exact = REL_BUCKETS // 2
    for tile in range(3):
        n = jnp.maximum(t - s + (2 - tile) * QB, 0)
        nf = jnp.maximum(n.astype(jnp.float32), 1.0)
        large = max_exact + (jnp.log(nf / max_exact) / math.log(REL_MAX_DIST / max_exact)
                             * (REL_BUCKETS - max_exact)).astype(jnp.int32)
        large = jnp.minimum(large, REL_BUCKETS - 1)
        bucket = jnp.where(n < max_exact, n, large)
        for h in range(N_HEADS_A):
            acc = jnp.zeros((QB, QB), jnp.float32)
            for b in range(REL_BUCKETS):
                acc = jnp.where(bucket == b, rb_ref[b, h], acc)
            o_ref[tile, h] = acc


def _bias_tiles(rel_bias):
    return pl.pallas_call(
        _bias_kernel,
        in_specs=[pl.BlockSpec(memory_space=pltpu.SMEM)],
        out_specs=pl.BlockSpec(memory_space=pltpu.VMEM),
        out_shape=jax.ShapeDtypeStruct((3, N_HEADS_A, QB, QB), jnp.float32),
        name="bias_tiles",
    )(rel_bias)


_MAIN_COLS = Q_RANK + KV_RANK + 3 * CONV_CH + MIX_C


def _proj_kernel(x_ref, mem_ref, wm_ref, ws_ref, qg_ref, kvg_ref, cw_ref, wmk_ref, wmv_ref,
                 cq_ref, ckv_ref, ckvt_ref, kidx_ref, iwt_ref, yb_ref, yc_ref,
                 carry_ref, mk_ref, mv_ref, *, tm):
    si = pl.program_id(1)

    @pl.when(si == 0)
    def _():
        carry_ref[...] = jnp.zeros_like(carry_ref)
        mb = mem_ref[0].astype(MXU_DTYPE)
        mk_ref[...] = _dot(mb, wmk_ref[...]).astype(MXU_DTYPE)
        mv_ref[...] = _dot(mb, wmv_ref[...]).astype(MXU_DTYPE)

    xb = x_ref[...].astype(MXU_DTYPE)
    p = _dot(xb, wm_ref[...])
    small = _dot(xb, ws_ref[...])

    o = 0
    cq = p[:, o:o + Q_RANK]; o += Q_RANK
    ckv = p[:, o:o + KV_RANK]; o += KV_RANK
    g_b = p[:, o:o + CONV_CH]; o += CONV_CH
    g_c = p[:, o:o + CONV_CH]; o += CONV_CH
    h_c = p[:, o:o + CONV_CH]; o += CONV_CH
    q_mem = p[:, o:o + MIX_C]

    cq = cq * lax.rsqrt(jnp.mean(cq * cq, axis=-1, keepdims=True) + RMS_EPS) * qg_ref[...]
    ckv = ckv * lax.rsqrt(jnp.mean(ckv * ckv, axis=-1, keepdims=True) + RMS_EPS) * kvg_ref[...]
    cq_ref[...] = cq.astype(MXU_DTYPE)
    ckv_b = ckv.astype(MXU_DTYPE)
    ckv_ref[...] = ckv_b
    ckvt_ref[0] = ckv.T.astype(MXU_DTYPE)

    kidx_ref[...] = small[:, :IDX_DIM].astype(MXU_DTYPE)
    small_t = small.T
    iwt_ref[0] = small_t[IDX_DIM:IDX_DIM + N_IDX_HEADS, :] * (N_IDX_HEADS ** -0.5 * IDX_DIM ** -0.5)

    u = g_c * h_c
    rows = lax.broadcasted_iota(jnp.int32, (tm, 1), 0)
    c6 = carry_ref[SUBLANES - 2:SUBLANES - 1, :]
    c7 = carry_ref[SUBLANES - 1:SUBLANES, :]
    u1 = jnp.where(rows == 0, c7, pltpu.roll(u, 1, 0))
    u2 = jnp.where(rows == 0, c6, jnp.where(rows == 1, c7, pltpu.roll(u, 2, 0)))
    y = cw_ref[0:1, :] * u2
    y = y + cw_ref[1:2, :] * u1
    y = y + cw_ref[2:3, :] * u
    yb_ref[...] = (g_b * y).astype(MXU_DTYPE)
    carry_ref[...] = u[tm - SUBLANES:, :]

    qm = q_mem.astype(MXU_DTYPE)
    outs = []
    for h in range(N_MEM_HEADS):
        sl = slice(h * HEAD_DIM, (h + 1) * HEAD_DIM)
        lg = _dot_nt(qm[:, sl], mk_ref[:, sl]) * (HEAD_DIM ** -0.5)
        lg = lg - jnp.max(lg, axis=-1, keepdims=True)
        e = jnp.exp(lg)
        pr = e / jnp.sum(e, axis=-1, keepdims=True)
        outs.append(_dot(pr.astype(MXU_DTYPE), mv_ref[:, sl]))
    yc_ref[...] = jnp.concatenate(outs, axis=-1).astype(MXU_DTYPE)


def _proj(x2, mem, w_main, w_small, q_g, kv_g, conv_w, w_mk, w_mv, B, S, tm):
    T, D = x2.shape
    n_mem = mem.shape[1]
    ns = S // tm
    row = lambda b, s: (b * ns + s, 0)
    const2 = lambda b, s: (0, 0)
    bf = MXU_DTYPE
    return pl.pallas_call(
        functools.partial(_proj_kernel, tm=tm),
        grid=(B, ns),
        in_specs=[
            pl.BlockSpec((tm, D), row),
            pl.BlockSpec((1, n_mem, D), lambda b, s: (b, 0, 0)),
            pl.BlockSpec(w_main.shape, const2),
            pl.BlockSpec(w_small.shape, const2),
            pl.BlockSpec(q_g.shape, const2),
            pl.BlockSpec(kv_g.shape, const2),
            pl.BlockSpec(conv_w.shape, const2),
            pl.BlockSpec(w_mk.shape, const2),
            pl.BlockSpec(w_mv.shape, const2),
        ],
        out_specs=[
            pl.BlockSpec((tm, Q_RANK), row),
            pl.BlockSpec((tm, KV_RANK), row),
            pl.BlockSpec((1, KV_RANK, tm), lambda b, s: (b, 0, s)),
            pl.BlockSpec((tm, IDX_DIM), row),
            pl.BlockSpec((1, N_IDX_HEADS, tm), lambda b, s: (b, 0, s)),
            pl.BlockSpec((tm, CONV_CH), row),
            pl.BlockSpec((tm, MIX_C), row),
        ],
        out_shape=[
            jax.ShapeDtypeStruct((T, Q_RANK), bf),
            jax.ShapeDtypeStruct((T, KV_RANK), bf),
            jax.ShapeDtypeStruct((B, KV_RANK, S), bf),
            jax.ShapeDtypeStruct((T, IDX_DIM), bf),
            jax.ShapeDtypeStruct((B, N_IDX_HEADS, S), jnp.float32),
            jax.ShapeDtypeStruct((T, CONV_CH), bf),
            jax.ShapeDtypeStruct((T, MIX_C), bf),
        ],
        scratch_shapes=[
            pltpu.VMEM((SUBLANES, CONV_CH), jnp.float32),
            pltpu.VMEM((n_mem, MIX_C), bf),
            pltpu.VMEM((n_mem, MIX_C), bf),
        ],
        compiler_params=_cparams(("arbitrary", "arbitrary")),
        name="proj",
    )(x2, mem, w_main, w_small, q_g, kv_g, conv_w, w_mk, w_mv)


def _key_to_f32(key):
    bits = jnp.where(key < 0, key ^ jnp.int32(0x7FFFFFFF), key)
    return pltpu.bitcast(bits, jnp.float32)


def _colsum8(v):
    return jnp.sum(v.reshape(QB // SUBLANES, SUBLANES, QB), axis=0)


def _colmax8(v):
    return jnp.max(v.reshape(QB // SUBLANES, SUBLANES, QB), axis=0)


UNROLL_WIDTHS = (8, 4, 2, 1)


def _dsa_kernel(cq_ref, iwt_ref, kidx_ref, ckv_ref, ckvt_ref, wqi_ref, wuq_ref, wuk_ref, wuvt_ref,
                bias_ref, o_ref, wfold_ref, qidx_ref, qlat_ref, score_ref, mask_ref, logit_ref, acc_ref,
                *, k_sel, idx_bits):
    i = pl.program_id(1)
    f32 = jnp.float32
    bf = MXU_DTYPE
    n_blocks = i + 1
    n_blocks = n_blocks + jnp.where((n_blocks % 4 == 3) & (n_blocks < pl.num_programs(1)), 1, 0)
    s_loc = lax.broadcasted_iota(jnp.int32, (QB, QB), 0)
    t_glob = i * QB + lax.broadcasted_iota(jnp.int32, (QB, QB), 1)

    def blk(jb):
        return pl.multiple_of(jb * QB, QB)

    def block_loop(fn, init):
        c, start = init, 0
        for width in UNROLL_WIDTHS:
            n = (n_blocks - start) // width
            c = lax.fori_loop(0, n, lambda it, c, w=width, s=start: fn(s + it * w, w, c), c)
            start = start + n * width
        return c

    @pl.when(i == 0)
    def _():
        for h in range(N_HEADS_A):
            wfold_ref[:, h * KV_RANK:(h + 1) * KV_RANK] = (
                _dot_nt(wuq_ref[:, h * HEAD_DIM:(h + 1) * HEAD_DIM], wuk_ref[h]) * (HEAD_DIM ** -0.5)).astype(bf)

    cq = cq_ref[...]
    q_idx = _dot(cq, wqi_ref[...]).astype(bf)
    q_lat = _dot(cq, wfold_ref[...]).astype(bf)
    for h in range(N_HEADS_A):
        qidx_ref[h * QB:(h + 1) * QB, :] = q_idx[:, h * IDX_DIM:(h + 1) * IDX_DIM]
        qlat_ref[h * QB:(h + 1) * QB, :] = q_lat[:, h * KV_RANK:(h + 1) * KV_RANK]
    iw = iwt_ref[0]

    def score_body(jb0, nb, c):
        d_blk = _dot_nt(kidx_ref[pl.ds(blk(jb0), nb * QB), :], qidx_ref[...])
        for sb in range(nb):
            off = blk(jb0 + sb)
            d_all = d_blk[sb * QB:(sb + 1) * QB, :]
            acc = jnp.maximum(d_all[:, 0:QB], 0.0) * iw[0:1, :]
            for h in range(1, N_IDX_HEADS):
                acc = acc + jnp.maximum(d_all[:, h * QB:(h + 1) * QB], 0.0) * iw[h:h + 1, :]
            score_ref[pl.ds(off, QB), :] = jnp.where(s_loc + off <= t_glob, acc + 0.0, F32_LOWEST)
        return c

    block_loop(score_body, 0)

    def count_where(pred):
        def body(jb0, nb, acc):
            for sb in range(nb):
                off = blk(jb0 + sb)
                acc = acc + _colsum8(jnp.where(pred(score_ref[pl.ds(off, QB), :], off), 1.0, 0.0))
            return acc
        acc = block_loop(body, jnp.zeros((SUBLANES, QB), f32))
        return jnp.sum(acc, axis=0, keepdims=True)

    kf = float(k_sel)

    def search():
        c0 = count_where(lambda sc, off: sc >= 0.0)
        cand0 = jnp.where(c0 >= kf, jnp.int32(0), jnp.int32(-2 ** 31))

        def bit_body(it, cand):
            trial = cand + lax.shift_left(jnp.int32(1), 30 - it)
            tf = _key_to_f32(trial)
            cnt = count_where(lambda sc, off: sc >= tf)
            return jnp.where(cnt >= kf, trial, cand)

        cand = lax.fori_loop(0, 31, bit_body, cand0)
        thr = _key_to_f32(cand)
        n_gt = count_where(lambda sc, off: sc > thr)
        n_eq = count_where(lambda sc, off: sc == thr)
        need = kf - n_gt

        def tie_search():
            def tbody(it, xcut):
                trial = xcut + lax.shift_left(jnp.int32(1), idx_bits - 1 - it)
                cnt = count_where(lambda sc, off: (sc == thr) & (s_loc + off < trial))
                return jnp.where(cnt < need, trial, xcut)
            return lax.fori_loop(0, idx_bits, tbody, jnp.zeros((1, QB), jnp.int32))

        any_extra = jnp.max(n_eq - need) > 0.0
        xcut = lax.cond(any_extra, tie_search, lambda: jnp.full((1, QB), 2 ** idx_bits - 1, jnp.int32))
        return thr, xcut

    def no_search():
        return jnp.full((1, QB), F32_LOWEST, f32), jnp.full((1, QB), 2 ** idx_bits - 1, jnp.int32)

    thr, xcut = lax.cond((i + 1) * QB > k_sel, search, no_search)

    def mask_body(jb0, nb, c):
        for sb in range(nb):
            off = blk(jb0 + sb)
            sc = score_ref[pl.ds(off, QB), :]
            s_glob = s_loc + off
            keep = ((sc > thr) | ((sc == thr) & (s_glob <= xcut))) & (s_glob <= t_glob)
            mask_ref[pl.ds(off, QB), :] = jnp.where(keep, 0.0, -jnp.inf)
        return c

    block_loop(mask_body, 0)

    def p1_body(jb0, nb, m8):
        m8 = list(m8)
        lg_blk = _dot_nt(ckv_ref[pl.ds(blk(jb0), nb * QB), :], qlat_ref[...])
        for sb in range(nb):
            off = blk(jb0 + sb)
            lg = lg_blk[sb * QB:(sb + 1) * QB, :]
            msk = mask_ref[pl.ds(off, QB), :]
            bsel = jnp.clip(jb0 + sb - i + 2, 0, 2)
            for h in range(N_HEADS_A):
                lgh = lg[:, h * QB:(h + 1) * QB] + bias_ref[bsel, h] + msk
                logit_ref[pl.ds(off, QB), h * QB:(h + 1) * QB] = lgh
                m8[h] = jnp.maximum(m8[h], _colmax8(lgh))
        return tuple(m8)

    m8 = block_loop(p1_body, tuple(jnp.full((SUBLANES, QB), -jnp.inf, f32) for _ in range(N_HEADS_A)))
    m_row = [jnp.max(m, axis=0, keepdims=True) for m in m8]

    acc_ref[...] = jnp.zeros_like(acc_ref)

    def p2_body(jb0, nb, l8):
        l8 = list(l8)
        off = blk(jb0)
        rows = nb * QB
        ps = []
        for h in range(N_HEADS_A):
            p = jnp.exp(logit_ref[pl.ds(off, rows), h * QB:(h + 1) * QB] - m_row[h])
            l8[h] = l8[h] + jnp.sum(p.reshape(rows // SUBLANES, SUBLANES, QB), axis=0)
            ps.append(p.astype(bf))
        acc_ref[...] += _dot(ckvt_ref[0, :, pl.ds(off, rows)], jnp.concatenate(ps, axis=1))
        return tuple(l8)

    l8 = block_loop(p2_body, tuple(jnp.zeros((SUBLANES, QB), f32) for _ in range(N_HEADS_A)))

    outs = []
    for h in range(N_HEADS_A):
        l_row = jnp.sum(l8[h], axis=0, keepdims=True)
        o_lat_t = (acc_ref[:, h * QB:(h + 1) * QB] / l_row).astype(bf)
        outs.append(_dot(wuvt_ref[h], o_lat_t))
    o_ref[...] = jnp.concatenate(outs, axis=0).T.astype(o_ref.dtype)


def _dsa(cq, iwt, kidx, ckv, ckvt, w_qidx, w_uq, w_uk_h, w_uvt_h, bias_tiles, B, S):
    T = cq.shape[0]
    assert S % QB == 0 and QB >= REL_MAX_DIST
    nq = S // QB
    k_sel = min(TOPK_MAX, S // 4)
    idx_bits = max(1, (S - 1).bit_length())
    c2 = lambda b, i: (0, 0)
    c3 = lambda b, i: (0, 0, 0)
    return pl.pallas_call(
        functools.partial(_dsa_kernel, k_sel=k_sel, idx_bits=idx_bits),
        grid=(B, nq),
        in_specs=[
            pl.BlockSpec((QB, Q_RANK), lambda b, i: (b * nq + i, 0)),
            pl.BlockSpec((1, N_IDX_HEADS, QB), lambda b, i: (b, 0, i)),
            pl.BlockSpec((S, IDX_DIM), lambda b, i: (b, 0)),
            pl.BlockSpec((S, KV_RANK), lambda b, i: (b, 0)),
            pl.BlockSpec((1, KV_RANK, S), lambda b, i: (b, 0, 0)),
            pl.BlockSpec(w_qidx.shape, c2),
            pl.BlockSpec(w_uq.shape, c2),
            pl.BlockSpec(w_uk_h.shape, c3),
            pl.BlockSpec(w_uvt_h.shape, c3),
            pl.BlockSpec(bias_tiles.shape, lambda b, i: (0, 0, 0, 0)),
        ],
        out_specs=pl.BlockSpec((QB, MIX_A), lambda b, i: (b * nq + i, 0)),
        out_shape=jax.ShapeDtypeStruct((T, MIX_A), MXU_DTYPE),
        scratch_shapes=[
            pltpu.VMEM((Q_RANK, N_HEADS_A * KV_RANK), MXU_DTYPE),
            pltpu.VMEM((N_IDX_HEADS * QB, IDX_DIM), MXU_DTYPE),
            pltpu.VMEM((N_HEADS_A * QB, KV_RANK), MXU_DTYPE),
            pltpu.VMEM((S, QB), jnp.float32),
            pltpu.VMEM((S, QB), jnp.float32),
            pltpu.VMEM((S, N_HEADS_A * QB), jnp.float32),
            pltpu.VMEM((KV_RANK, N_HEADS_A * QB), jnp.float32),
        ],
        compiler_params=_cparams(("arbitrary", "arbitrary")),
        name="dsa",
    )(cq, iwt, kidx, ckv, ckvt, w_qidx, w_uq, w_uk_h, w_uvt_h, bias_tiles)


def _layer_norm(xf, g, b):
    mu = jnp.mean(xf, axis=-1, keepdims=True)
    xc = xf - mu
    var = jnp.mean(xc * xc, axis=-1, keepdims=True)
    return xc * lax.rsqrt(var + LN_EPS) * g + b


def _rank_rows(v, n):
    ri = lax.broadcasted_iota(jnp.int32, v.shape, 0)
    rank = jnp.zeros(v.shape, jnp.float32)
    for r2 in range(n):
        row = v[r2:r2 + 1, :]
        beats = (row > v) | ((row == v) & (ri > r2))
        rank = rank + jnp.where(beats, 1.0, 0.0)
    return rank


def _pack_factor():
    return 4 // jnp.dtype(MXU_DTYPE).itemsize


def _pack_rows(x):
    if _pack_factor() == 1:
        return pltpu.bitcast(x, jnp.int32)
    half = x.shape[1] // 2
    b = pltpu.bitcast(x.astype(MXU_DTYPE).astype(jnp.float32), jnp.int32)
    return b[:, half:] | lax.shift_right_logical(b[:, :half], jnp.int32(16))


_HIGH_HALF = -(1 << 16)


def _unpack_rows_f32(p):
    if _pack_factor() == 1:
        return [pltpu.bitcast(p, jnp.float32)]
    lo = pltpu.bitcast(lax.shift_left(p, jnp.int32(16)), jnp.float32)
    hi = pltpu.bitcast(p & jnp.int32(_HIGH_HALF), jnp.float32)
    return [lo, hi]


def _unpack_rows(p):
    return [v.astype(MXU_DTYPE) for v in _unpack_rows_f32(p)]


def _mix_router_kernel(x_ref, ya_ref, yb_ref, yc_ref, wo_ref, g_ref, b_ref, wrt_ref, rb_ref, exp_ref,
                       x1_ref, x1p_ref, sel_ref, w_ref, pos_ref, cnt_ref, base_ref, *, tm):
    step = pl.program_id(0)
    f32 = jnp.float32

    @pl.when(step == 0)
    def _():
        base_ref[...] = jnp.zeros_like(base_ref)

    mix = _dot(ya_ref[...], wo_ref[0:MIX_A, :])
    mix = mix + _dot(yb_ref[...], wo_ref[MIX_A:MIX_A + CONV_CH, :])
    mix = mix + _dot(yc_ref[...], wo_ref[MIX_A + CONV_CH:, :])
    x1 = _layer_norm(ALPHA * x_ref[...] + mix, g_ref[...], b_ref[...])
    x1_ref[...] = x1
    x1p_ref[...] = _pack_rows(x1)

    lg = lax.dot_general(wrt_ref[...], x1, _NT, precision=lax.Precision.HIGHEST, preferred_element_type=f32)
    s = 1.0 / (1.0 + jnp.exp(-lg))
    sc = s + rb_ref[...]

    g3 = sc.reshape(N_GROUPS, GROUP_SIZE, tm)
    m1 = jnp.max(g3, axis=1, keepdims=True)
    is_m1 = g3 == m1
    n_m1 = jnp.sum(jnp.where(is_m1, 1.0, 0.0), axis=1, keepdims=True)
    m2 = jnp.max(jnp.where(is_m1, -jnp.inf, g3), axis=1, keepdims=True)
    gscore = (m1 + jnp.where(n_m1 > 1.0, m1, m2)).reshape(N_GROUPS, tm)
    gsel = jnp.where(_rank_rows(gscore, N_GROUPS) < float(TOPK_GROUPS), 1.0, 0.0)
    emask = _dot(exp_ref[...], gsel.astype(MXU_DTYPE)) > 0.5
    masked = jnp.where(emask, sc, -jnp.inf)
    sel = (_rank_rows(masked, N_EXPERTS) < float(TOP_K)) & emask
    self_ = jnp.where(sel, 1.0, 0.0)
    top_s = jnp.where(sel, s, 0.0)
    w = top_s / jnp.sum(top_s, axis=0, keepdims=True) * ROUTED_SCALE

    t_r = lax.broadcasted_iota(jnp.int32, (tm, tm), 0)
    t_c = lax.broadcasted_iota(jnp.int32, (tm, tm), 1)
    upper = jnp.where(t_r < t_c, 1.0, 0.0).astype(MXU_DTYPE)
    pref = _dot(self_.astype(MXU_DTYPE), upper)
    base = base_ref[...]
    sel_ref[...] = self_
    w_ref[...] = w
    pos_ref[...] = base + pref
    base = base + jnp.sum(self_, axis=1, keepdims=True)
    base_ref[...] = base
    cnt_ref[...] = jnp.broadcast_to(base, cnt_ref.shape)


def _mix_router(x2, ya, yb, yc, w_out, ln_g, ln_b, w_router_t, router_bias, tm):
    T, D = x2.shape
    E = N_EXPERTS
    expand = (jnp.arange(E)[:, None] // GROUP_SIZE == jnp.arange(N_GROUPS)[None, :]).astype(MXU_DTYPE)
    row = lambda i: (i, 0)
    col = lambda i: (0, i)
    c2 = lambda i: (0, 0)
    f32 = jnp.float32
    return pl.pallas_call(
        functools.partial(_mix_router_kernel, tm=tm),
        grid=(T // tm,),
        in_specs=[
            pl.BlockSpec((tm, D), row),
            pl.BlockSpec((tm, MIX_A), row),
            pl.BlockSpec((tm, CONV_CH), row),
            pl.BlockSpec((tm, MIX_C), row),
            pl.BlockSpec(w_out.shape, c2),
            pl.BlockSpec((1, D), c2),
            pl.BlockSpec((1, D), c2),
            pl.BlockSpec((E, D), c2),
            pl.BlockSpec((E, 1), c2),
            pl.BlockSpec((E, N_GROUPS), c2),
        ],
        out_specs=[
            pl.BlockSpec((tm, D), row),
            pl.BlockSpec((tm, D // _pack_factor()), row),
            pl.BlockSpec((E, tm), col),
            pl.BlockSpec((E, tm), col),
            pl.BlockSpec((E, tm), col),
            pl.BlockSpec((E, LANES), c2),
        ],
        out_shape=[
            jax.ShapeDtypeStruct((T, D), f32),
            jax.ShapeDtypeStruct((T, D // _pack_factor()), jnp.int32),
            jax.ShapeDtypeStruct((E, T), f32),
            jax.ShapeDtypeStruct((E, T), f32),
            jax.ShapeDtypeStruct((E, T), f32),
            jax.ShapeDtypeStruct((E, LANES), f32),
        ],
        scratch_shapes=[pltpu.VMEM((E, 1), f32)],
        compiler_params=_cparams(("arbitrary",)),
        name="mix_router",
    )(x2, ya, yb, yc, w_out, ln_g, ln_b, w_router_t, router_bias, expand)


def _compact_kernel(sel_ref, w_ref, pos_ref, pstart_ref, low_ref, dest_ref, wk_ref):
    sel = sel_ref[...]
    on = sel > 0.5
    rank = _dot(low_ref[...], sel.astype(MXU_DTYPE))
    row = pstart_ref[...] + pos_ref[...]
    w = w_ref[...]
    dests, ws = [], []
    for k in range(TOP_K):
        m = on & (rank == float(k))
        dests.append(jnp.sum(jnp.where(m, row, 0.0), axis=0, keepdims=True))
        ws.append(jnp.sum(jnp.where(m, w, 0.0), axis=0, keepdims=True))
    dest_ref[...] = jnp.concatenate(dests, axis=0).astype(jnp.int32)
    wk_ref[...] = jnp.concatenate(ws, axis=0)


def _compact(sel_t, w_t, pos_t, pad_start, tm):
    E, T = sel_t.shape
    lower = (jnp.arange(E)[None, :] < jnp.arange(E)[:, None]).astype(MXU_DTYPE)
    col = lambda i: (0, i)
    c2 = lambda i: (0, 0)
    return pl.pallas_call(
        _compact_kernel,
        grid=(T // tm,),
        in_specs=[pl.BlockSpec((E, tm), col), pl.BlockSpec((E, tm), col), pl.BlockSpec((E, tm), col),
                  pl.BlockSpec((E, 1), c2), pl.BlockSpec((E, E), c2)],
        out_specs=[pl.BlockSpec((TOP_K, tm), col), pl.BlockSpec((TOP_K, tm), col)],
        out_shape=[jax.ShapeDtypeStruct((TOP_K, T), jnp.int32), jax.ShapeDtypeStruct((TOP_K, T), jnp.float32)],
        compiler_params=_cparams(("arbitrary",)),
        name="route_compact",
    )(sel_t, w_t, pos_t, pad_start, lower)


def _row_copy(src, s, dst, d, sem):
    return pltpu.make_async_copy(src.at[pl.ds(s, 1)], dst.at[pl.ds(d, 1)], sem)


def _dispatch_kernel(flo_ref, fhi_ref, dest_ref, x_ref, xs_hbm, zero_ref, sem, zsem, *, td):
    step = pl.program_id(0)

    @pl.when(step == 0)
    def _():
        zero_ref[...] = jnp.zeros_like(zero_ref)

        def per_expert(fn):
            def ebody(e, c):
                lax.fori_loop(flo_ref[e], fhi_ref[e], lambda r, c2: (fn(r), c2)[1], 0)
                return c
            lax.fori_loop(0, N_EXPERTS, ebody, 0)

        per_expert(lambda r: _row_copy(zero_ref, 0, xs_hbm, r, zsem).start())
        per_expert(lambda r: _row_copy(zero_ref, 0, xs_hbm, r, zsem).wait())

    def issue(r, c):
        for k in range(TOP_K):
            _row_copy(x_ref, r, xs_hbm, dest_ref[k, r], sem).start()
        return c

    def drain(r, c):
        for k in range(TOP_K):
            _row_copy(x_ref, r, xs_hbm, dest_ref[k, r], sem).wait()
        return c

    lax.fori_loop(0, td, issue, 0)
    lax.fori_loop(0, td, drain, 0)


def _dispatch(dest_t, x1p, fill_lo, fill_hi, n_rows, td):
    T, W = x1p.shape
    return pl.pallas_call(
        functools.partial(_dispatch_kernel, td=td),
        grid_spec=pltpu.PrefetchScalarGridSpec(
            num_scalar_prefetch=2,
            grid=(T // td,),
            in_specs=[
                pl.BlockSpec((TOP_K, td), lambda i, lo, hi: (0, i), memory_space=pltpu.SMEM),
                pl.BlockSpec((td, W), lambda i, lo, hi: (i, 0)),
            ],
            out_specs=pl.BlockSpec(memory_space=pl.ANY),
            scratch_shapes=[pltpu.VMEM((SUBLANES, W), x1p.dtype),
                            pltpu.SemaphoreType.DMA, pltpu.SemaphoreType.DMA],
        ),
        out_shape=jax.ShapeDtypeStruct((n_rows, W), x1p.dtype),
        compiler_params=_cparams(("arbitrary",)),
        name="dispatch",
    )(fill_lo, fill_hi, dest_t, x1p)


def _silu(g):
    return g / (1.0 + jnp.exp(-g))


def _expert_kernel(be_ref, nv_ref, nu_ref, xs_ref, wg_ref, wu_ref, wd_ref, ys_ref, wgb_ref, wub_ref, wdb_ref):
    i = pl.program_id(0)

    @pl.when((i == 0) | (be_ref[i] != be_ref[jnp.maximum(i - 1, 0)]))
    def _():
        wgb_ref[...] = wg_ref[0].astype(MXU_DTYPE)
        wub_ref[...] = wu_ref[0].astype(MXU_DTYPE)
        wdb_ref[...] = wd_ref[0].astype(MXU_DTYPE)

    n_live = nv_ref[i]

    for sb in range(ROW_BLOCK // ROW_SUB):
        @pl.when(n_live > sb * ROW_SUB)
        def _(sb=sb):
            rows = pl.ds(sb * ROW_SUB, ROW_SUB)
            live = lax.broadcasted_iota(jnp.int32, (ROW_SUB, 1), 0) + sb * ROW_SUB < n_live
            parts = [jnp.where(live, v, jnp.zeros_like(v)) for v in _unpack_rows(xs_ref[rows, :])]
            dk = wgb_ref.shape[0] // len(parts)

            def proj(w_ref):
                acc = _dot(parts[0], w_ref[0:dk, :])
                for n in range(1, len(parts)):
                    acc = acc + _dot(parts[n], w_ref[n * dk:(n + 1) * dk, :])
                return acc

            a = (_silu(proj(wgb_ref)) * proj(wub_ref)).astype(MXU_DTYPE)
            ys_ref[rows, :] = _pack_rows(_dot(a, wdb_ref[...]))


def _experts(xs, block_e, block_valid, n_used, w_gate, w_up, w_down):
    n_rows, W = xs.shape
    D = w_gate.shape[1]
    n_blocks = n_rows // ROW_BLOCK
    blk = lambda i, be, nv, nu: (jnp.minimum(i, nu[0] - 1), 0)
    wsel = lambda i, be, nv, nu: (be[i], 0, 0)
    return pl.pallas_call(
        _expert_kernel,
        grid_spec=pltpu.PrefetchScalarGridSpec(
            num_scalar_prefetch=3,
            grid=(n_blocks,),
            in_specs=[
                pl.BlockSpec((ROW_BLOCK, W), blk),
                pl.BlockSpec((1, D, D_EXPERT), wsel),
                pl.BlockSpec((1, D, D_EXPERT), wsel),
                pl.BlockSpec((1, D_EXPERT, D), wsel),
            ],
            out_specs=pl.BlockSpec((ROW_BLOCK, W), blk),
            scratch_shapes=[pltpu.VMEM((D, D_EXPERT), MXU_DTYPE), pltpu.VMEM((D, D_EXPERT), MXU_DTYPE),
                            pltpu.VMEM((D_EXPERT, D), MXU_DTYPE)],
        ),
        out_shape=jax.ShapeDtypeStruct((n_rows, W), xs.dtype),
        compiler_params=_cparams(("arbitrary",)),
        name="experts",
    )(block_e, block_valid, n_used, xs, w_gate, w_up, w_down)


SC_CORES = 2
SC_SUBCORES = 16
SC_GATHER_ROWS = 64
COMBINE_CHUNKS = 4


def _sc_gather_rows(table, idx):
    n = idx.shape[0]
    w = table.shape[1]
    n_workers = SC_CORES * SC_SUBCORES
    per_worker = n // n_workers
    assert n % n_workers == 0 and per_worker % SC_GATHER_ROWS == 0
    mesh = plsc.VectorSubcoreMesh(core_axis_name="c", subcore_axis_name="s")

    @functools.partial(
        pl.kernel, mesh=mesh,
        out_type=jax.ShapeDtypeStruct((n, w), table.dtype),
        scratch_types=[
            pltpu.VMEM((2, SC_GATHER_ROWS), jnp.int32),
            pltpu.VMEM((2, SC_GATHER_ROWS, w), table.dtype),
            pltpu.SemaphoreType.DMA((2,)),
        ],
        name="sc_gather_rows",
    )
    def gather(table_hbm, idx_hbm, out_hbm, idx_v, rows_v, sem):
        wid = lax.axis_index("s") * SC_CORES + lax.axis_index("c")
        base = wid * per_worker
        n_steps = per_worker // SC_GATHER_ROWS

        def gather_copy(slot):
            return pltpu.make_async_copy(table_hbm.at[idx_v.at[slot]], rows_v.at[slot], sem.at[slot])

        def start(step, slot):
            pltpu.sync_copy(idx_hbm.at[pl.ds(base + step * SC_GATHER_ROWS, SC_GATHER_ROWS)], idx_v.at[slot])
            gather_copy(slot).start()

        start(0, 0)

        @pl.loop(0, n_steps, step=2)
        def _(g):
            for slot in range(2):
                step = g + slot

                @pl.when(step + 1 < n_steps)
                def _():
                    start(step + 1, 1 - slot)

                gather_copy(slot).wait()
                pltpu.sync_copy(rows_v.at[slot], out_hbm.at[pl.ds(base + step * SC_GATHER_ROWS, SC_GATHER_ROWS)])

    return gather(table, idx)


SC_SCATTER_ROWS = 64


def _sc_scatter_rows(rows, idx3, n_out):
    n_src, w = rows.shape
    n_chunks, n_dst, batch = idx3.shape
    n_workers = SC_CORES * SC_SUBCORES
    assert batch == SC_SCATTER_ROWS and n_chunks * batch == n_src and n_chunks % (2 * n_workers) == 0
    per_worker = n_chunks // n_workers
    mesh = plsc.VectorSubcoreMesh(core_axis_name="c", subcore_axis_name="s")

    @functools.partial(
        pl.kernel, mesh=mesh,
        out_type=jax.ShapeDtypeStruct((n_out, w), rows.dtype),
        scratch_types=[
            pltpu.VMEM((2, n_dst, batch), jnp.int32),
            pltpu.VMEM((2, batch, w), rows.dtype),
            pltpu.SemaphoreType.DMA((2,)),
            pltpu.SemaphoreType.DMA,
        ],
        name="sc_scatter_rows",
    )
    def scatter(rows_hbm, idx_hbm, out_hbm, idx_v, rows_v, load_sem, store_sem):
        wid = lax.axis_index("s") * SC_CORES + lax.axis_index("c")

        def load_copy(step, slot):
            c = wid * per_worker + step
            return pltpu.make_async_copy(rows_hbm.at[pl.ds(c * batch, batch)], rows_v.at[slot], load_sem.at[slot])

        def load(step, slot):
            pltpu.sync_copy(idx_hbm.at[wid * per_worker + step], idx_v.at[slot])
            load_copy(step, slot).start()

        def store_copy(slot, k):
            return pltpu.make_async_copy(rows_v.at[slot], out_hbm.at[idx_v.at[slot].at[k]], store_sem)

        load(0, 0)

        @pl.loop(0, per_worker, step=2)
        def _(g):
            for slot in range(2):
                step = g + slot

                @pl.when(step + 1 < per_worker)
                def _():
                    load(step + 1, 1 - slot)

                load_copy(step, slot).wait()
                for k in range(n_dst):
                    store_copy(slot, k).start()
                for k in range(n_dst):
                    store_copy(slot, k).wait()

    return scatter(rows, idx3)


def _shared_kernel(x1_ref, wsg_ref, wsu_ref, wsd_ref, o_ref):
    xb = x1_ref[...].astype(MXU_DTYPE)
    a = (_silu(_dot(xb, wsg_ref[...])) * _dot(xb, wsu_ref[...])).astype(MXU_DTYPE)
    o_ref[...] = _dot(a, wsd_ref[...])


def _shared_expert(x1, w_sg, w_su, w_sd, tm):
    T, D = x1.shape
    row = lambda i: (i, 0)
    c2 = lambda i: (0, 0)
    return pl.pallas_call(
        _shared_kernel,
        grid=(T // tm,),
        in_specs=[pl.BlockSpec((tm, D), row), pl.BlockSpec(w_sg.shape, c2), pl.BlockSpec(w_su.shape, c2),
                  pl.BlockSpec(w_sd.shape, c2)],
        out_specs=pl.BlockSpec((tm, D), row),
        out_shape=jax.ShapeDtypeStruct((T, D), jnp.float32),
        compiler_params=_cparams(("arbitrary",)),
        name="shared_expert",
    )(x1, w_sg, w_su, w_sd)


def _combine2_kernel(wk_ref, x1_ref, g_ref_rows, wsg_ref, wsu_ref, wsd_ref, g_ref, b_ref, o_ref):
    x1 = x1_ref[...]
    xb = x1.astype(MXU_DTYPE)
    a = (_silu(_dot(xb, wsg_ref[...])) * _dot(xb, wsu_ref[...])).astype(MXU_DTYPE)
    shared = _dot(a, wsd_ref[...])
    wk = wk_ref[...].T
    groups = [wk[:, 0:1] * v for v in _unpack_rows_f32(g_ref_rows[0])]
    for k in range(1, TOP_K):
        groups = [g + wk[:, k:k + 1] * v for g, v in zip(groups, _unpack_rows_f32(g_ref_rows[k]))]
    routed = jnp.concatenate(groups, axis=1)
    o_ref[...] = _layer_norm(ALPHA * x1 + (routed + shared), g_ref[...], b_ref[...])


def _combine2_kernel_into(wk_ref, x1_ref, g_ref_rows, wsg_ref, wsu_ref, wsd_ref, g_ref, b_ref, prev_ref, o_ref):
    del prev_ref
    _combine2_kernel(wk_ref, x1_ref, g_ref_rows, wsg_ref, wsu_ref, wsd_ref, g_ref, b_ref, o_ref)


def _combine2(wk_t, x1, gathered, w_sg, w_su, w_sd, ln_g, ln_b, tc, chunk, prev):
    T, D = x1.shape
    _, t_chunk, W = gathered.shape
    base = chunk * (t_chunk // tc)
    row = lambda i: (base + i, 0)
    c2 = lambda i: (0, 0)
    in_specs = [
        pl.BlockSpec((TOP_K, tc), lambda i: (0, base + i)),
        pl.BlockSpec((tc, D), row),
        pl.BlockSpec((TOP_K, tc, W), lambda i: (0, i, 0)),
        pl.BlockSpec(w_sg.shape, c2),
        pl.BlockSpec(w_su.shape, c2),
        pl.BlockSpec(w_sd.shape, c2),
        pl.BlockSpec((1, D), c2),
        pl.BlockSpec((1, D), c2),
    ]
    args = [wk_t, x1, gathered, w_sg, w_su, w_sd, ln_g, ln_b]
    if prev is None:
        body, aliases = _combine2_kernel, {}
    else:
        body, aliases = _combine2_kernel_into, {len(args): 0}
        in_specs.append(pl.BlockSpec(memory_space=pl.ANY))
        args.append(prev)
    return pl.pallas_call(
        body,
        grid=(t_chunk // tc,),
        in_specs=in_specs,
        out_specs=pl.BlockSpec((tc, D), row),
        out_shape=jax.ShapeDtypeStruct((T, D), jnp.float32),
        input_output_aliases=aliases,
        compiler_params=_cparams(("arbitrary",)),
        name="combine",
    )(*args)


def _combine_kernel(dest_ref, wk_ref, x1_ref, ys_hbm, wsg_ref, wsu_ref, wsd_ref, g_ref, b_ref,
                    o_ref, buf_ref, sem, *, tc):
    def issue(r, c):
        for k in range(TOP_K):
            _row_copy(ys_hbm, dest_ref[k, r], buf_ref.at[k], r, sem).start()
        return c

    def drain(r, c):
        for k in range(TOP_K):
            _row_copy(ys_hbm, dest_ref[k, r], buf_ref.at[k], r, sem).wait()
        return c

    lax.fori_loop(0, tc, issue, 0)
    x1 = x1_ref[...]
    xb = x1.astype(MXU_DTYPE)
    a = (_silu(_dot(xb, wsg_ref[...])) * _dot(xb, wsu_ref[...])).astype(MXU_DTYPE)
    shared = _dot(a, wsd_ref[...])
    lax.fori_loop(0, tc, drain, 0)
    wk = wk_ref[...]
    groups = [wk[:, 0:1] * v for v in _unpack_rows_f32(buf_ref[0])]
    for k in range(1, TOP_K):
        groups = [g + wk[:, k:k + 1] * v for g, v in zip(groups, _unpack_rows_f32(buf_ref[k]))]
    routed = jnp.concatenate(groups, axis=1)
    o_ref[...] = _layer_norm(ALPHA * x1 + (routed + shared), g_ref[...], b_ref[...])


def _combine(dest_t, wk, x1, ys, w_sg, w_su, w_sd, ln_g, ln_b, tc):
    T, D = x1.shape
    row = lambda i: (i, 0)
    c2 = lambda i: (0, 0)
    return pl.pallas_call(
        functools.partial(_combine_kernel, tc=tc),
        grid=(T // tc,),
        in_specs=[
            pl.BlockSpec((TOP_K, tc), lambda i: (0, i), memory_space=pltpu.SMEM),
            pl.BlockSpec((tc, TOP_K), row),
            pl.BlockSpec((tc, D), row),
            pl.BlockSpec(memory_space=pl.ANY),
            pl.BlockSpec(w_sg.shape, c2),
            pl.BlockSpec(w_su.shape, c2),
            pl.BlockSpec(w_sd.shape, c2),
            pl.BlockSpec((1, D), c2),
            pl.BlockSpec((1, D), c2),
        ],
        out_specs=pl.BlockSpec((tc, D), row),
        out_shape=jax.ShapeDtypeStruct((T, D), jnp.float32),
        scratch_shapes=[pltpu.VMEM((TOP_K, tc, ys.shape[1]), ys.dtype), pltpu.SemaphoreType.DMA],
        compiler_params=_cparams(("arbitrary",)),
        name="combine",
    )(dest_t, wk, x1, ys, w_sg, w_su, w_sd, ln_g, ln_b)


def _split_w_in(w_in):
    bf = MXU_DTYPE
    o_kv = Q_RANK
    o_ki = o_kv + KV_RANK
    o_iw = o_ki + IDX_DIM
    o_rest = o_iw + N_IDX_HEADS
    w_main = jnp.concatenate([w_in[:, :o_ki], w_in[:, o_rest:]], axis=1).astype(bf)
    w_small = jnp.pad(w_in[:, o_ki:o_rest], ((0, 0), (0, LANES - IDX_DIM - N_IDX_HEADS))).astype(bf)
    return w_main, w_small


def _stages(x, mem, w_in, q_norm_g, kv_norm_g, w_uq, w_uk, w_uv, w_qidx, rel_bias, conv_w, w_mem_k, w_mem_v, w_out, ln1_g, ln1_b, w_router, router_bias, w_e_gate, w_e_up, w_e_down, w_s_gate, w_s_up, w_s_down, ln2_g, ln2_b, upto=None):
    B, S, D = x.shape
    T = B * S
    bf = MXU_DTYPE
    l = 0
    res = {}
    x2 = x.reshape(T, D)
    w_main, w_small = _split_w_in(w_in[l])
    cq, ckv, ckvt, kidx, iwt, yb, yc = _proj(
        x2, mem, w_main, w_small, q_norm_g[l].reshape(1, -1), kv_norm_g[l].reshape(1, -1), conv_w[l],
        w_mem_k[l].astype(bf), w_mem_v[l].astype(bf), B, S, tm=min(512, S))
    res.update(c_q=cq, c_kv=ckv, k_idx=kidx, y_b=yb, y_c=yc,
               idx_w=jnp.swapaxes(iwt, 1, 2) / (N_IDX_HEADS ** -0.5 * IDX_DIM ** -0.5))
    if upto == "proj":
        return res
    bias_t = _bias_tiles(rel_bias)
    ya = _dsa(cq, iwt, kidx, ckv, ckvt,
              w_qidx[l].reshape(Q_RANK, -1).astype(bf), w_uq[l].reshape(Q_RANK, -1).astype(bf),
              jnp.transpose(w_uk[l], (1, 0, 2)).astype(bf), jnp.transpose(w_uv[l], (1, 2, 0)).astype(bf),
              bias_t, B, S)
    res.update(y_a=ya)
    if upto == "dsa":
        return res

    x1, x1p, sel_t, w_t, pos_t, cnt = _mix_router(
        x2, ya, yb, yc, w_out[l].astype(bf), ln1_g[l].reshape(1, -1), ln1_b[l].reshape(1, -1),
        w_router[l].T, router_bias[l].reshape(-1, 1), tm=min(512, T))
    res.update(x1=x1)

    counts = cnt[:, 0].astype(jnp.int32)
    padded = (counts + ROW_BLOCK - 1) // ROW_BLOCK * ROW_BLOCK
    pad_end = jnp.cumsum(padded)
    pad_start = pad_end - padded
    n_blocks = -(-(T * TOP_K) // ROW_BLOCK) + N_EXPERTS
    n_rows = n_blocks * ROW_BLOCK
    block_start = jnp.arange(n_blocks, dtype=jnp.int32) * ROW_BLOCK
    block_e = jnp.minimum(jnp.sum((pad_end[None, :] <= block_start[:, None]).astype(jnp.int32), axis=1),
                          N_EXPERTS - 1)
    n_used = (pad_end[-1:] // ROW_BLOCK).astype(jnp.int32)

    dest_t, wk_t = _compact(sel_t, w_t, pos_t, pad_start.astype(jnp.float32).reshape(-1, 1), tm=min(512, T))
    block_valid = jnp.clip((pad_start + counts)[block_e] - block_start, 0, ROW_BLOCK).astype(jnp.int32)
    bt = SC_SCATTER_ROWS
    idx3 = jnp.transpose(dest_t.reshape(TOP_K, T // bt, bt), (1, 0, 2))
    xs = _sc_scatter_rows(x1p, idx3, n_rows)
    ys = _experts(xs, block_e, block_valid, n_used, w_e_gate[l], w_e_up[l], w_e_down[l])
    n_chunks = COMBINE_CHUNKS if T % (COMBINE_CHUNKS * 256) == 0 else 1
    t_chunk = T // n_chunks
    out = None
    for c in range(n_chunks):
        idx_c = dest_t[:, c * t_chunk:(c + 1) * t_chunk].reshape(-1)
        gathered = _sc_gather_rows(ys, idx_c).reshape(TOP_K, t_chunk, -1)
        out = _combine2(wk_t, x1, gathered, w_s_gate[l].astype(bf), w_s_up[l].astype(bf), w_s_down[l].astype(bf),
                        ln2_g[l].reshape(1, -1), ln2_b[l].reshape(1, -1), tc=min(256, t_chunk), chunk=c, prev=out)
    res.update(out=out.reshape(B, S, D))
    return res


def kernel(x, mem, w_in, q_norm_g, kv_norm_g, w_uq, w_uk, w_uv, w_qidx, rel_bias, conv_w, w_mem_k, w_mem_v, w_out, ln1_g, ln1_b, w_router, router_bias, w_e_gate, w_e_up, w_e_down, w_s_gate, w_s_up, w_s_down, ln2_g, ln2_b):
    return _stages(x, mem, w_in, q_norm_g, kv_norm_g, w_uq, w_uk, w_uv, w_qidx, rel_bias, conv_w, w_mem_k, w_mem_v, w_out, ln1_g, ln1_b, w_router, router_bias, w_e_gate, w_e_up, w_e_down, w_s_gate, w_s_up, w_s_down, ln2_g, ln2_b)["out"]
```

```python
import functools
import math

import jax
import jax.numpy as jnp
from jax import lax
from jax.experimental import pallas as pl
from jax.experimental.pallas import tpu as pltpu
from jax.experimental.pallas import tpu_sc as plsc

N_HEADS_A = 8
HEAD_DIM = 64
Q_RANK = 256
KV_RANK = 128
N_IDX_HEADS = 8
IDX_DIM = 64
TOPK_MAX = 256
REL_BUCKETS = 32
REL_MAX_DIST = 128
CONV_CH = 256
CONV_WIDTH = 3
N_MEM_HEADS = 4
MIX_A = N_HEADS_A * HEAD_DIM
MIX_C = N_MEM_HEADS * HEAD_DIM
N_EXPERTS = 64
N_GROUPS = 8
GROUP_SIZE = N_EXPERTS // N_GROUPS
TOPK_GROUPS = 4
TOP_K = 8
D_EXPERT = 256
ROUTED_SCALE = 2.5
MOE_BLOCK = 256
DEPTH = 1
ALPHA = (2.0 * DEPTH) ** 0.25
LN_EPS = 1e-5
RMS_EPS = 1e-6

LANES = 128
SUBLANES = 8
QB = 128
F32_LOWEST = -3.4028234663852886e38
VMEM_LIMIT = 56 * 1024 * 1024
MXU_DTYPE = jnp.bfloat16
ROW_BLOCK = 1024
ROW_SUB = 512

_NT = (((1,), (1,)), ((), ()))


def _dot(a, b):
    return jnp.dot(a, b, preferred_element_type=jnp.float32)


def _dot_nt(a, b):
    return lax.dot_general(a, b, _NT, preferred_element_type=jnp.float32)


def _cparams(sem):
    return pltpu.CompilerParams(dimension_semantics=sem, vmem_limit_bytes=VMEM_LIMIT)


def _bias_kernel(rb_ref, o_ref):
    s = lax.broadcasted_iota(jnp.int32, (QB, QB), 0)
    t = lax.broadcasted_iota(jnp.int32, (QB, QB), 1)
    max_exact = REL_BUCKETS // 2
    for tile in range(3):
        n = jnp.maximum(t - s + (2 - tile) * QB, 0)
        nf = jnp.maximum(n.astype(jnp.float32), 1.0)
        large = max_exact + (jnp.log(nf / max_exact) / math.log(REL_MAX_DIST / max_exact)
                             * (REL_BUCKETS - max_exact)).astype(jnp.int32)
        large = jnp.minimum(large, REL_BUCKETS - 1)
        bucket = jnp.where(n < max_exact, n, large)
        for h in range(N_HEADS_A):
            acc = jnp.zeros((QB, QB), jnp.float32)
            for b in range(REL_BUCKETS):
                acc = jnp.where(bucket == b, rb_ref[b, h], acc)
            o_ref[tile, h] = acc


def _bias_tiles(rel_bias):
    return pl.pallas_call(
        _bias_kernel,
        in_specs=[pl.BlockSpec(memory_space=pltpu.SMEM)],
        out_specs=pl.BlockSpec(memory_space=pltpu.VMEM),
        out_shape=jax.ShapeDtypeStruct((3, N_HEADS_A, QB, QB), jnp.float32),
        name="bias_tiles",
    )(rel_bias)


_MAIN_COLS = Q_RANK + KV_RANK + 3 * CONV_CH + MIX_C


def _proj_kernel(x_ref, mem_ref, wm_ref, ws_ref, qg_ref, kvg_ref, cw_ref, wmk_ref, wmv_ref,
                 cq_ref, ckv_ref, ckvt_ref, kidx_ref, iwt_ref, yb_ref, yc_ref,
                 carry_ref, mk_ref, mv_ref, *, tm):
    si = pl.program_id(1)

    @pl.when(si == 0)
    def _():
        carry_ref[...] = jnp.zeros_like(carry_ref)
        mb = mem_ref[0].astype(MXU_DTYPE)
        mk_ref[...] = _dot(mb, wmk_ref[...]).astype(MXU_DTYPE)
        mv_ref[...] = _dot(mb, wmv_ref[...]).astype(MXU_DTYPE)

    xb = x_ref[...].astype(MXU_DTYPE)
    p = _dot(xb, wm_ref[...])
    small = _dot(xb, ws_ref[...])

    o = 0
    cq = p[:, o:o + Q_RANK]; o += Q_RANK
    ckv = p[:, o:o + KV_RANK]; o += KV_RANK
    g_b = p[:, o:o + CONV_CH]; o += CONV_CH
    g_c = p[:, o:o + CONV_CH]; o += CONV_CH
    h_c = p[:, o:o + CONV_CH]; o += CONV_CH
    q_mem = p[:, o:o + MIX_C]

    cq = cq * lax.rsqrt(jnp.mean(cq * cq, axis=-1, keepdims=True) + RMS_EPS) * qg_ref[...]
    ckv = ckv * lax.rsqrt(jnp.mean(ckv * ckv, axis=-1, keepdims=True) + RMS_EPS) * kvg_ref[...]
    cq_ref[...] = cq.astype(MXU_DTYPE)
    ckv_b = ckv.astype(MXU_DTYPE)
    ckv_ref[...] = ckv_b
    ckvt_ref[0] = ckv.T.astype(MXU_DTYPE)

    kidx_ref[...] = small[:, :IDX_DIM].astype(MXU_DTYPE)
    small_t = small.T
    iwt_ref[0] = small_t[IDX_DIM:IDX_DIM + N_IDX_HEADS, :] * (N_IDX_HEADS ** -0.5 * IDX_DIM ** -0.5)

    u = g_c * h_c
    rows = lax.broadcasted_iota(jnp.int32, (tm, 1), 0)
    c6 = carry_ref[SUBLANES - 2:SUBLANES - 1, :]
    c7 = carry_ref[SUBLANES - 1:SUBLANES, :]
    u1 = jnp.where(rows == 0, c7, pltpu.roll(u, 1, 0))
    u2 = jnp.where(rows == 0, c6, jnp.where(rows == 1, c7, pltpu.roll(u, 2, 0)))
    y = cw_ref[0:1, :] * u2
    y = y + cw_ref[1:2, :] * u1
    y = y + cw_ref[2:3, :] * u
    yb_ref[...] = (g_b * y).astype(MXU_DTYPE)
    carry_ref[...] = u[tm - SUBLANES:, :]

    qm = q_mem.astype(MXU_DTYPE)
    outs = []
    for h in range(N_MEM_HEADS):
        sl = slice(h * HEAD_DIM, (h + 1) * HEAD_DIM)
        lg = _dot_nt(qm[:, sl], mk_ref[:, sl]) * (HEAD_DIM ** -0.5)
        lg = lg - jnp.max(lg, axis=-1, keepdims=True)
        e = jnp.exp(lg)
        pr = e / jnp.sum(e, axis=-1, keepdims=True)
        outs.append(_dot(pr.astype(MXU_DTYPE), mv_ref[:, sl]))
    yc_ref[...] = jnp.concatenate(outs, axis=-1).astype(MXU_DTYPE)


def _proj(x2, mem, w_main, w_small, q_g, kv_g, conv_w, w_mk, w_mv, B, S, tm):
    T, D = x2.shape
    n_mem = mem.shape[1]
    ns = S // tm
    row = lambda b, s: (b * ns + s, 0)
    const2 = lambda b, s: (0, 0)
    bf = MXU_DTYPE
    return pl.pallas_call(
        functools.partial(_proj_kernel, tm=tm),
        grid=(B, ns),
        in_specs=[
            pl.BlockSpec((tm, D), row),
            pl.BlockSpec((1, n_mem, D), lambda b, s: (b, 0, 0)),
            pl.BlockSpec(w_main.shape, const2),
            pl.BlockSpec(w_small.shape, const2),
            pl.BlockSpec(q_g.shape, const2),
            pl.BlockSpec(kv_g.shape, const2),
            pl.BlockSpec(conv_w.shape, const2),
            pl.BlockSpec(w_mk.shape, const2),
            pl.BlockSpec(w_mv.shape, const2),
        ],
        out_specs=[
            pl.BlockSpec((tm, Q_RANK), row),
            pl.BlockSpec((tm, KV_RANK), row),
            pl.BlockSpec((1, KV_RANK, tm), lambda b, s: (b, 0, s)),
            pl.BlockSpec((tm, IDX_DIM), row),
            pl.BlockSpec((1, N_IDX_HEADS, tm), lambda b, s: (b, 0, s)),
            pl.BlockSpec((tm, CONV_CH), row),
            pl.BlockSpec((tm, MIX_C), row),
        ],
        out_shape=[
            jax.ShapeDtypeStruct((T, Q_RANK), bf),
            jax.ShapeDtypeStruct((T, KV_RANK), bf),
            jax.ShapeDtypeStruct((B, KV_RANK, S), bf),
            jax.ShapeDtypeStruct((T, IDX_DIM), bf),
            jax.ShapeDtypeStruct((B, N_IDX_HEADS, S), jnp.float32),
            jax.ShapeDtypeStruct((T, CONV_CH), bf),
            jax.ShapeDtypeStruct((T, MIX_C), bf),
        ],
        scratch_shapes=[
            pltpu.VMEM((SUBLANES, CONV_CH), jnp.float32),
            pltpu.VMEM((n_mem, MIX_C), bf),
            pltpu.VMEM((n_mem, MIX_C), bf),
        ],
        compiler_params=_cparams(("arbitrary", "arbitrary")),
        name="proj",
    )(x2, mem, w_main, w_small, q_g, kv_g, conv_w, w_mk, w_mv)


def _key_to_f32(key):
    bits = jnp.where(key < 0, key ^ jnp.int32(0x7FFFFFFF), key)
    return pltpu.bitcast(bits, jnp.float32)


def _colsum8(v):
    return jnp.sum(v.reshape(QB // SUBLANES, SUBLANES, QB), axis=0)


def _colmax8(v):
    return jnp.max(v.reshape(QB // SUBLANES, SUBLANES, QB), axis=0)


UNROLL_WIDTHS = (8, 4, 2, 1)


def _dsa_kernel(cq_ref, iwt_ref, kidx_ref, ckv_ref, ckvt_ref, wqi_ref, wuq_ref, wuk_ref, wuvt_ref,
                bias_ref, o_ref, wfold_ref, qidx_ref, qlat_ref, score_ref, mask_ref, logit_ref, acc_ref,
                *, k_sel, idx_bits):
    i = pl.program_id(1)
    f32 = jnp.float32
    bf = MXU_DTYPE
    n_blocks = i + 1
    n_blocks = n_blocks + jnp.where((n_blocks % 4 == 3) & (n_blocks < pl.num_programs(1)), 1, 0)
    s_loc = lax.broadcasted_iota(jnp.int32, (QB, QB), 0)
    t_glob = i * QB + lax.broadcasted_iota(jnp.int32, (QB, QB), 1)

    def blk(jb):
        return pl.multiple_of(jb * QB, QB)

    def block_loop(fn, init):
        c, start = init, 0
        for width in UNROLL_WIDTHS:
            n = (n_blocks - start) // width
            c = lax.fori_loop(0, n, lambda it, c, w=width, s=start: fn(s + it * w, w, c), c)
            start = start + n * width
        return c

    @pl.when(i == 0)
    def _():
        for h in range(N_HEADS_A):
            wfold_ref[:, h * KV_RANK:(h + 1) * KV_RANK] = (
                _dot_nt(wuq_ref[:, h * HEAD_DIM:(h + 1) * HEAD_DIM], wuk_ref[h]) * (HEAD_DIM ** -0.5)).astype(bf)

    cq = cq_ref[...]
    q_idx = _dot(cq, wqi_ref[...]).astype(bf)
    q_lat = _dot(cq, wfold_ref[...]).astype(bf)
    for h in range(N_HEADS_A):
        qidx_ref[h * QB:(h + 1) * QB, :] = q_idx[:, h * IDX_DIM:(h + 1) * IDX_DIM]
        qlat_ref[h * QB:(h + 1) * QB, :] = q_lat[:, h * KV_RANK:(h + 1) * KV_RANK]
    iw = iwt_ref[0]

    def score_body(jb0, nb, c):
        d_blk = _dot_nt(kidx_ref[pl.ds(blk(jb0), nb * QB), :], qidx_ref[...])
        for sb in range(nb):
            off = blk(jb0 + sb)
            d_all = d_blk[sb * QB:(sb + 1) * QB, :]
            acc = jnp.maximum(d_all[:, 0:QB], 0.0) * iw[0:1, :]
            for h in range(1, N_IDX_HEADS):
                acc = acc + jnp.maximum(d_all[:, h * QB:(h + 1) * QB], 0.0) * iw[h:h + 1, :]
            score_ref[pl.ds(off, QB), :] = jnp.where(s_loc + off <= t_glob, acc + 0.0, F32_LOWEST)
        return c

    block_loop(score_body, 0)

    def count_where(pred):
        def body(jb0, nb, acc):
            for sb in range(nb):
                off = blk(jb0 + sb)
                acc = acc + _colsum8(jnp.where(pred(score_ref[pl.ds(off, QB), :], off), 1.0, 0.0))
            return acc
        acc = block_loop(body, jnp.zeros((SUBLANES, QB), f32))
        return jnp.sum(acc, axis=0, keepdims=True)

    kf = float(k_sel)

    def search():
        c0 = count_where(lambda sc, off: sc >= 0.0)
        cand0 = jnp.where(c0 >= kf, jnp.int32(0), jnp.int32(-2 ** 31))

        def bit_body(it, cand):
            trial = cand + lax.shift_left(jnp.int32(1), 30 - it)
            tf = _key_to_f32(trial)
            cnt = count_where(lambda sc, off: sc >= tf)
            return jnp.where(cnt >= kf, trial, cand)

        cand = lax.fori_loop(0, 31, bit_body, cand0)
        thr = _key_to_f32(cand)
        n_gt = count_where(lambda sc, off: sc > thr)
        n_eq = count_where(lambda sc, off: sc == thr)
        need = kf - n_gt

        def tie_search():
            def tbody(it, xcut):
                trial = xcut + lax.shift_left(jnp.int32(1), idx_bits - 1 - it)
                cnt = count_where(lambda sc, off: (sc == thr) & (s_loc + off < trial))
                return jnp.where(cnt < need, trial, xcut)
            return lax.fori_loop(0, idx_bits, tbody, jnp.zeros((1, QB), jnp.int32))

        any_extra = jnp.max(n_eq - need) > 0.0
        xcut = lax.cond(any_extra, tie_search, lambda: jnp.full((1, QB), 2 ** idx_bits - 1, jnp.int32))
        return thr, xcut

    def no_search():
        return jnp.full((1, QB), F32_LOWEST, f32), jnp.full((1, QB), 2 ** idx_bits - 1, jnp.int32)

    thr, xcut = lax.cond((i + 1) * QB > k_sel, search, no_search)

    def mask_body(jb0, nb, c):
        for sb in range(nb):
            off = blk(jb0 + sb)
            sc = score_ref[pl.ds(off, QB), :]
            s_glob = s_loc + off
            keep = ((sc > thr) | ((sc == thr) & (s_glob <= xcut))) & (s_glob <= t_glob)
            mask_ref[pl.ds(off, QB), :] = jnp.where(keep, 0.0, -jnp.inf)
        return c

    block_loop(mask_body, 0)

    def p1_body(jb0, nb, m8):
        m8 = list(m8)
        lg_blk = _dot_nt(ckv_ref[pl.ds(blk(jb0), nb * QB), :], qlat_ref[...])
        for sb in range(nb):
            off = blk(jb0 + sb)
            lg = lg_blk[sb * QB:(sb + 1) * QB, :]
            msk = mask_ref[pl.ds(off, QB), :]
            bsel = jnp.clip(jb0 + sb - i + 2, 0, 2)
            for h in range(N_HEADS_A):
                lgh = lg[:, h * QB:(h + 1) * QB] + bias_ref[bsel, h] + msk
                logit_ref[pl.ds(off, QB), h * QB:(h + 1) * QB] = lgh
                m8[h] = jnp.maximum(m8[h], _colmax8(lgh))
        return tuple(m8)

    m8 = block_loop(p1_body, tuple(jnp.full((SUBLANES, QB), -jnp.inf, f32) for _ in range(N_HEADS_A)))
    m_row = [jnp.max(m, axis=0, keepdims=True) for m in m8]

    acc_ref[...] = jnp.zeros_like(acc_ref)

    def p2_body(jb0, nb, l8):
        l8 = list(l8)
        off = blk(jb0)
        rows = nb * QB
        ps = []
        for h in range(N_HEADS_A):
            p = jnp.exp(logit_ref[pl.ds(off, rows), h * QB:(h + 1) * QB] - m_row[h])
            l8[h] = l8[h] + jnp.sum(p.reshape(rows // SUBLANES, SUBLANES, QB), axis=0)
            ps.append(p.astype(bf))
        acc_ref[...] += _dot(ckvt_ref[0, :, pl.ds(off, rows)], jnp.concatenate(ps, axis=1))
        return tuple(l8)

    l8 = block_loop(p2_body, tuple(jnp.zeros((SUBLANES, QB), f32) for _ in range(N_HEADS_A)))

    outs = []
    for h in range(N_HEADS_A):
        l_row = jnp.sum(l8[h], axis=0, keepdims=True)
        o_lat_t = (acc_ref[:, h * QB:(h + 1) * QB] / l_row).astype(bf)
        outs.append(_dot(wuvt_ref[h], o_lat_t))
    o_ref[...] = jnp.concatenate(outs, axis=0).T.astype(o_ref.dtype)


def _dsa(cq, iwt, kidx, ckv, ckvt, w_qidx, w_uq, w_uk_h, w_uvt_h, bias_tiles, B, S):
    T = cq.shape[0]
    assert S % QB == 0 and QB >= REL_MAX_DIST
    nq = S // QB
    k_sel = min(TOPK_MAX, S // 4)
    idx_bits = max(1, (S - 1).bit_length())
    c2 = lambda b, i: (0, 0)
    c3 = lambda b, i: (0, 0, 0)
    return pl.pallas_call(
        functools.partial(_dsa_kernel, k_sel=k_sel, idx_bits=idx_bits),
        grid=(B, nq),
        in_specs=[
            pl.BlockSpec((QB, Q_RANK), lambda b, i: (b * nq + i, 0)),
            pl.BlockSpec((1, N_IDX_HEADS, QB), lambda b, i: (b, 0, i)),
            pl.BlockSpec((S, IDX_DIM), lambda b, i: (b, 0)),
            pl.BlockSpec((S, KV_RANK), lambda b, i: (b, 0)),
            pl.BlockSpec((1, KV_RANK, S), lambda b, i: (b, 0, 0)),
            pl.BlockSpec(w_qidx.shape, c2),
            pl.BlockSpec(w_uq.shape, c2),
            pl.BlockSpec(w_uk_h.shape, c3),
            pl.BlockSpec(w_uvt_h.shape, c3),
            pl.BlockSpec(bias_tiles.shape, lambda b, i: (0, 0, 0, 0)),
        ],
        out_specs=pl.BlockSpec((QB, MIX_A), lambda b, i: (b * nq + i, 0)),
        out_shape=jax.ShapeDtypeStruct((T, MIX_A), MXU_DTYPE),
        scratch_shapes=[
            pltpu.VMEM((Q_RANK, N_HEADS_A * KV_RANK), MXU_DTYPE),
            pltpu.VMEM((N_IDX_HEADS * QB, IDX_DIM), MXU_DTYPE),
            pltpu.VMEM((N_HEADS_A * QB, KV_RANK), MXU_DTYPE),
            pltpu.VMEM((S, QB), jnp.float32),
            pltpu.VMEM((S, QB), jnp.float32),
            pltpu.VMEM((S, N_HEADS_A * QB), jnp.float32),
            pltpu.VMEM((KV_RANK, N_HEADS_A * QB), jnp.float32),
        ],
        compiler_params=_cparams(("arbitrary", "arbitrary")),
        name="dsa",
    )(cq, iwt, kidx, ckv, ckvt, w_qidx, w_uq, w_uk_h, w_uvt_h, bias_tiles)


def _layer_norm(xf, g, b):
    mu = jnp.mean(xf, axis=-1, keepdims=True)
    xc = xf - mu
    var = jnp.mean(xc * xc, axis=-1, keepdims=True)
    return xc * lax.rsqrt(var + LN_EPS) * g + b


def _rank_rows(v, n):
    ri = lax.broadcasted_iota(jnp.int32, v.shape, 0)
    rank = jnp.zeros(v.shape, jnp.float32)
    for r2 in range(n):
        row = v[r2:r2 + 1, :]
        beats = (row > v) | ((row == v) & (ri > r2))
        rank = rank + jnp.where(beats, 1.0, 0.0)
    return rank


def _pack_factor():
    return 4 // jnp.dtype(MXU_DTYPE).itemsize


def _pack_rows(x):
    if _pack_factor() == 1:
        return pltpu.bitcast(x, jnp.int32)
    half = x.shape[1] // 2
    b = pltpu.bitcast(x.astype(MXU_DTYPE).astype(jnp.float32), jnp.int32)
    return b[:, half:] | lax.shift_right_logical(b[:, :half], jnp.int32(16))


_HIGH_HALF = -(1 << 16)


def _unpack_rows_f32(p):
    if _pack_factor() == 1:
        return [pltpu.bitcast(p, jnp.float32)]
    lo = pltpu.bitcast(lax.shift_left(p, jnp.int32(16)), jnp.float32)
    hi = pltpu.bitcast(p & jnp.int32(_HIGH_HALF), jnp.float32)
    return [lo, hi]


def _unpack_rows(p):
    return [v.astype(MXU_DTYPE) for v in _unpack_rows_f32(p)]


def _mix_router_kernel(x_ref, ya_ref, yb_ref, yc_ref, wo_ref, g_ref, b_ref, wrt_ref, rb_ref, exp_ref,
                       x1_ref, x1p_ref, sel_ref, w_ref, pos_ref, cnt_ref, base_ref, *, tm):
    step = pl.program_id(0)
    f32 = jnp.float32

    @pl.when(step == 0)
    def _():
        base_ref[...] = jnp.zeros_like(base_ref)

    mix = _dot(ya_ref[...], wo_ref[0:MIX_A, :])
    mix = mix + _dot(yb_ref[...], wo_ref[MIX_A:MIX_A + CONV_CH, :])
    mix = mix + _dot(yc_ref[...], wo_ref[MIX_A + CONV_CH:, :])
    x1 = _layer_norm(ALPHA * x_ref[...] + mix, g_ref[...], b_ref[...])
    x1_ref[...] = x1
    x1p_ref[...] = _pack_rows(x1)

    lg = lax.dot_general(wrt_ref[...], x1, _NT, precision=lax.Precision.HIGHEST, preferred_element_type=f32)
    s = 1.0 / (1.0 + jnp.exp(-lg))
    sc = s + rb_ref[...]

    g3 = sc.reshape(N_GROUPS, GROUP_SIZE, tm)
    m1 = jnp.max(g3, axis=1, keepdims=True)
    is_m1 = g3 == m1
    n_m1 = jnp.sum(jnp.where(is_m1, 1.0, 0.0), axis=1, keepdims=True)
    m2 = jnp.max(jnp.where(is_m1, -jnp.inf, g3), axis=1, keepdims=True)
    gscore = (m1 + jnp.where(n_m1 > 1.0, m1, m2)).reshape(N_GROUPS, tm)
    gsel = jnp.where(_rank_rows(gscore, N_GROUPS) < float(TOPK_GROUPS), 1.0, 0.0)
    emask = _dot(exp_ref[...], gsel.astype(MXU_DTYPE)) > 0.5
    masked = jnp.where(emask, sc, -jnp.inf)
    sel = (_rank_rows(masked, N_EXPERTS) < float(TOP_K)) & emask
    self_ = jnp.where(sel, 1.0, 0.0)
    top_s = jnp.where(sel, s, 0.0)
    w = top_s / jnp.sum(top_s, axis=0, keepdims=True) * ROUTED_SCALE

    t_r = lax.broadcasted_iota(jnp.int32, (tm, tm), 0)
    t_c = lax.broadcasted_iota(jnp.int32, (tm, tm), 1)
    upper = jnp.where(t_r < t_c, 1.0, 0.0).astype(MXU_DTYPE)
    pref = _dot(self_.astype(MXU_DTYPE), upper)
    base = base_ref[...]
    sel_ref[...] = self_
    w_ref[...] = w
    pos_ref[...] = base + pref
    base = base + jnp.sum(self_, axis=1, keepdims=True)
    base_ref[...] = base
    cnt_ref[...] = jnp.broadcast_to(base, cnt_ref.shape)


def _mix_router(x2, ya, yb, yc, w_out, ln_g, ln_b, w_router_t, router_bias, tm):
    T, D = x2.shape
    E = N_EXPERTS
    expand = (jnp.arange(E)[:, None] // GROUP_SIZE == jnp.arange(N_GROUPS)[None, :]).astype(MXU_DTYPE)
    row = lambda i: (i, 0)
    col = lambda i: (0, i)
    c2 = lambda i: (0, 0)
    f32 = jnp.float32
    return pl.pallas_call(
        functools.partial(_mix_router_kernel, tm=tm),
        grid=(T // tm,),
        in_specs=[
            pl.BlockSpec((tm, D), row),
            pl.BlockSpec((tm, MIX_A), row),
            pl.BlockSpec((tm, CONV_CH), row),
            pl.BlockSpec((tm, MIX_C), row),
            pl.BlockSpec(w_out.shape, c2),
            pl.BlockSpec((1, D), c2),
            pl.BlockSpec((1, D), c2),
            pl.BlockSpec((E, D), c2),
            pl.BlockSpec((E, 1), c2),
            pl.BlockSpec((E, N_GROUPS), c2),
        ],
        out_specs=[
            pl.BlockSpec((tm, D), row),
            pl.BlockSpec((tm, D // _pack_factor()), row),
            pl.BlockSpec((E, tm), col),
            pl.BlockSpec((E, tm), col),
            pl.BlockSpec((E, tm), col),
            pl.BlockSpec((E, LANES), c2),
        ],
        out_shape=[
            jax.ShapeDtypeStruct((T, D), f32),
            jax.ShapeDtypeStruct((T, D // _pack_factor()), jnp.int32),
            jax.ShapeDtypeStruct((E, T), f32),
            jax.ShapeDtypeStruct((E, T), f32),
            jax.ShapeDtypeStruct((E, T), f32),
            jax.ShapeDtypeStruct((E, LANES), f32),
        ],
        scratch_shapes=[pltpu.VMEM((E, 1), f32)],
        compiler_params=_cparams(("arbitrary",)),
        name="mix_router",
    )(x2, ya, yb, yc, w_out, ln_g, ln_b, w_router_t, router_bias, expand)


def _compact_kernel(sel_ref, w_ref, pos_ref, pstart_ref, low_ref, dest_ref, wk_ref):
    sel = sel_ref[...]
    on = sel > 0.5
    rank = _dot(low_ref[...], sel.astype(MXU_DTYPE))
    row = pstart_ref[...] + pos_ref[...]
    w = w_ref[...]
    dests, ws = [], []
    for k in range(TOP_K):
        m = on & (rank == float(k))
        dests.append(jnp.sum(jnp.where(m, row, 0.0), axis=0, keepdims=True))
        ws.append(jnp.sum(jnp.where(m, w, 0.0), axis=0, keepdims=True))
    dest_ref[...] = jnp.concatenate(dests, axis=0).astype(jnp.int32)
    wk_ref[...] = jnp.concatenate(ws, axis=0)


def _compact(sel_t, w_t, pos_t, pad_start, tm):
    E, T = sel_t.shape
    lower = (jnp.arange(E)[None, :] < jnp.arange(E)[:, None]).astype(MXU_DTYPE)
    col = lambda i: (0, i)
    c2 = lambda i: (0, 0)
    return pl.pallas_call(
        _compact_kernel,
        grid=(T // tm,),
        in_specs=[pl.BlockSpec((E, tm), col), pl.BlockSpec((E, tm), col), pl.BlockSpec((E, tm), col),
                  pl.BlockSpec((E, 1), c2), pl.BlockSpec((E, E), c2)],
        out_specs=[pl.BlockSpec((TOP_K, tm), col), pl.BlockSpec((TOP_K, tm), col)],
        out_shape=[jax.ShapeDtypeStruct((TOP_K, T), jnp.int32), jax.ShapeDtypeStruct((TOP_K, T), jnp.float32)],
        compiler_params=_cparams(("arbitrary",)),
        name="route_compact",
    )(sel_t, w_t, pos_t, pad_start, lower)


def _row_copy(src, s, dst, d, sem):
    return pltpu.make_async_copy(src.at[pl.ds(s, 1)], dst.at[pl.ds(d, 1)], sem)


def _dispatch_kernel(flo_ref, fhi_ref, dest_ref, x_ref, xs_hbm, zero_ref, sem, zsem, *, td):
    step = pl.program_id(0)

    @pl.when(step == 0)
    def _():
        zero_ref[...] = jnp.zeros_like(zero_ref)

        def per_expert(fn):
            def ebody(e, c):
                lax.fori_loop(flo_ref[e], fhi_ref[e], lambda r, c2: (fn(r), c2)[1], 0)
                return c
            lax.fori_loop(0, N_EXPERTS, ebody, 0)

        per_expert(lambda r: _row_copy(zero_ref, 0, xs_hbm, r, zsem).start())
        per_expert(lambda r: _row_copy(zero_ref, 0, xs_hbm, r, zsem).wait())

    def issue(r, c):
        for k in range(TOP_K):
            _row_copy(x_ref, r, xs_hbm, dest_ref[k, r], sem).start()
        return c

    def drain(r, c):
        for k in range(TOP_K):
            _row_copy(x_ref, r, xs_hbm, dest_ref[k, r], sem).wait()
        return c

    lax.fori_loop(0, td, issue, 0)
    lax.fori_loop(0, td, drain, 0)


def _dispatch(dest_t, x1p, fill_lo, fill_hi, n_rows, td):
    T, W = x1p.shape
    return pl.pallas_call(
        functools.partial(_dispatch_kernel, td=td),
        grid_spec=pltpu.PrefetchScalarGridSpec(
            num_scalar_prefetch=2,
            grid=(T // td,),
            in_specs=[
                pl.BlockSpec((TOP_K, td), lambda i, lo, hi: (0, i), memory_space=pltpu.SMEM),
                pl.BlockSpec((td, W), lambda i, lo, hi: (i, 0)),
            ],
            out_specs=pl.BlockSpec(memory_space=pl.ANY),
            scratch_shapes=[pltpu.VMEM((SUBLANES, W), x1p.dtype),
                            pltpu.SemaphoreType.DMA, pltpu.SemaphoreType.DMA],
        ),
        out_shape=jax.ShapeDtypeStruct((n_rows, W), x1p.dtype),
        compiler_params=_cparams(("arbitrary",)),
        name="dispatch",
    )(fill_lo, fill_hi, dest_t, x1p)


def _silu(g):
    return g / (1.0 + jnp.exp(-g))


def _expert_kernel(be_ref, nv_ref, nu_ref, xs_ref, wg_ref, wu_ref, wd_ref, ys_ref, wgb_ref, wub_ref, wdb_ref):
    i = pl.program_id(0)

    @pl.when((i == 0) | (be_ref[i] != be_ref[jnp.maximum(i - 1, 0)]))
    def _():
        wgb_ref[...] = wg_ref[0].astype(MXU_DTYPE)
        wub_ref[...] = wu_ref[0].astype(MXU_DTYPE)
        wdb_ref[...] = wd_ref[0].astype(MXU_DTYPE)

    n_live = nv_ref[i]

    for sb in range(ROW_BLOCK // ROW_SUB):
        @pl.when(n_live > sb * ROW_SUB)
        def _(sb=sb):
            rows = pl.ds(sb * ROW_SUB, ROW_SUB)
            live = lax.broadcasted_iota(jnp.int32, (ROW_SUB, 1), 0) + sb * ROW_SUB < n_live
            parts = [jnp.where(live, v, jnp.zeros_like(v)) for v in _unpack_rows(xs_ref[rows, :])]
            dk = wgb_ref.shape[0] // len(parts)

            def proj(w_ref):
                acc = _dot(parts[0], w_ref[0:dk, :])
                for n in range(1, len(parts)):
                    acc = acc + _dot(parts[n], w_ref[n * dk:(n + 1) * dk, :])
                return acc

            a = (_silu(proj(wgb_ref)) * proj(wub_ref)).astype(MXU_DTYPE)
            ys_ref[rows, :] = _pack_rows(_dot(a, wdb_ref[...]))


def _experts(xs, block_e, block_valid, n_used, w_gate, w_up, w_down):
    n_rows, W = xs.shape
    D = w_gate.shape[1]
    n_blocks = n_rows // ROW_BLOCK
    blk = lambda i, be, nv, nu: (jnp.minimum(i, nu[0] - 1), 0)
    wsel = lambda i, be, nv, nu: (be[i], 0, 0)
    return pl.pallas_call(
        _expert_kernel,
        grid_spec=pltpu.PrefetchScalarGridSpec(
            num_scalar_prefetch=3,
            grid=(n_blocks,),
            in_specs=[
                pl.BlockSpec((ROW_BLOCK, W), blk),
                pl.BlockSpec((1, D, D_EXPERT), wsel),
                pl.BlockSpec((1, D, D_EXPERT), wsel),
                pl.BlockSpec((1, D_EXPERT, D), wsel),
            ],
            out_specs=pl.BlockSpec((ROW_BLOCK, W), blk),
            scratch_shapes=[pltpu.VMEM((D, D_EXPERT), MXU_DTYPE), pltpu.VMEM((D, D_EXPERT), MXU_DTYPE),
                            pltpu.VMEM((D_EXPERT, D), MXU_DTYPE)],
        ),
        out_shape=jax.ShapeDtypeStruct((n_rows, W), xs.dtype),
        compiler_params=_cparams(("arbitrary",)),
        name="experts",
    )(block_e, block_valid, n_used, xs, w_gate, w_up, w_down)


SC_CORES = 2
SC_SUBCORES = 16
SC_GATHER_ROWS = 64
COMBINE_CHUNKS = 4


def _sc_gather_rows(table, idx):
    n = idx.shape[0]
    w = table.shape[1]
    n_workers = SC_CORES * SC_SUBCORES
    per_worker = n // n_workers
    assert n % n_workers == 0 and per_worker % SC_GATHER_ROWS == 0
    mesh = plsc.VectorSubcoreMesh(core_axis_name="c", subcore_axis_name="s")

    @functools.partial(
        pl.kernel, mesh=mesh,
        out_type=jax.ShapeDtypeStruct((n, w), table.dtype),
        scratch_types=[
            pltpu.VMEM((2, SC_GATHER_ROWS), jnp.int32),
            pltpu.VMEM((2, SC_GATHER_ROWS, w), table.dtype),
            pltpu.SemaphoreType.DMA((2,)),
        ],
        name="sc_gather_rows",
    )
    def gather(table_hbm, idx_hbm, out_hbm, idx_v, rows_v, sem):
        wid = lax.axis_index("s") * SC_CORES + lax.axis_index("c")
        base = wid * per_worker
        n_steps = per_worker // SC_GATHER_ROWS

        def gather_copy(slot):
            return pltpu.make_async_copy(table_hbm.at[idx_v.at[slot]], rows_v.at[slot], sem.at[slot])

        def start(step, slot):
            pltpu.sync_copy(idx_hbm.at[pl.ds(base + step * SC_GATHER_ROWS, SC_GATHER_ROWS)], idx_v.at[slot])
            gather_copy(slot).start()

        start(0, 0)

        @pl.loop(0, n_steps, step=2)
        def _(g):
            for slot in range(2):
                step = g + slot

                @pl.when(step + 1 < n_steps)
                def _():
                    start(step + 1, 1 - slot)

                gather_copy(slot).wait()
                pltpu.sync_copy(rows_v.at[slot], out_hbm.at[pl.ds(base + step * SC_GATHER_ROWS, SC_GATHER_ROWS)])

    return gather(table, idx)


SC_SCATTER_ROWS = 64


def _sc_scatter_rows(rows, idx3, n_out):
    n_src, w = rows.shape
    n_chunks, n_dst, batch = idx3.shape
    n_workers = SC_CORES * SC_SUBCORES
    assert batch == SC_SCATTER_ROWS and n_chunks * batch == n_src and n_chunks % (2 * n_workers) == 0
    per_worker = n_chunks // n_workers
    mesh = plsc.VectorSubcoreMesh(core_axis_name="c", subcore_axis_name="s")

    @functools.partial(
        pl.kernel, mesh=mesh,
        out_type=jax.ShapeDtypeStruct((n_out, w), rows.dtype),
        scratch_types=[
            pltpu.VMEM((2, n_dst, batch), jnp.int32),
            pltpu.VMEM((2, batch, w), rows.dtype),
            pltpu.SemaphoreType.DMA((2,)),
            pltpu.SemaphoreType.DMA,
        ],
        name="sc_scatter_rows",
    )
    def scatter(rows_hbm, idx_hbm, out_hbm, idx_v, rows_v, load_sem, store_sem):
        wid = lax.axis_index("s") * SC_CORES + lax.axis_index("c")

        def load_copy(step, slot):
            c = wid * per_worker + step
            return pltpu.make_async_copy(rows_hbm.at[pl.ds(c * batch, batch)], rows_v.at[slot], load_sem.at[slot])

        def load(step, slot):
            pltpu.sync_copy(idx_hbm.at[wid * per_worker + step], idx_v.at[slot])
            load_copy(step, slot).start()

        def store_copy(slot, k):
            return pltpu.make_async_copy(rows_v.at[slot], out_hbm.at[idx_v.at[slot].at[k]], store_sem)

        load(0, 0)

        @pl.loop(0, per_worker, step=2)
        def _(g):
            for slot in range(2):
                step = g + slot

                @pl.when(step + 1 < per_worker)
                def _():
                    load(step + 1, 1 - slot)

                load_copy(step, slot).wait()
                for k in range(n_dst):
                    store_copy(slot, k).start()
                for k in range(n_dst):
                    store_copy(slot, k).wait()

    return scatter(rows, idx3)


def _shared_kernel(x1_ref, wsg_ref, wsu_ref, wsd_ref, o_ref):
    xb = x1_ref[...].astype(MXU_DTYPE)
    a = (_silu(_dot(xb, wsg_ref[...])) * _dot(xb, wsu_ref[...])).astype(MXU_DTYPE)
    o_ref[...] = _dot(a, wsd_ref[...])


def _shared_expert(x1, w_sg, w_su, w_sd, tm):
    T, D = x1.shape
    row = lambda i: (i, 0)
    c2 = lambda i: (0, 0)
    return pl.pallas_call(
        _shared_kernel,
        grid=(T // tm,),
        in_specs=[pl.BlockSpec((tm, D), row), pl.BlockSpec(w_sg.shape, c2), pl.BlockSpec(w_su.shape, c2),
                  pl.BlockSpec(w_sd.shape, c2)],
        out_specs=pl.BlockSpec((tm, D), row),
        out_shape=jax.ShapeDtypeStruct((T, D), jnp.float32),
        compiler_params=_cparams(("arbitrary",)),
        name="shared_expert",
    )(x1, w_sg, w_su, w_sd)


def _combine2_kernel(wk_ref, x1_ref, g_ref_rows, wsg_ref, wsu_ref, wsd_ref, g_ref, b_ref, o_ref):
    x1 = x1_ref[...]
    xb = x1.astype(MXU_DTYPE)
    a = (_silu(_dot(xb, wsg_ref[...])) * _dot(xb, wsu_ref[...])).astype(MXU_DTYPE)
    shared = _dot(a, wsd_ref[...])
    wk = wk_ref[...].T
    groups = [wk[:, 0:1] * v for v in _unpack_rows_f32(g_ref_rows[0])]
    for k in range(1, TOP_K):
        groups = [g + wk[:, k:k + 1] * v for g, v in zip(groups, _unpack_rows_f32(g_ref_rows[k]))]
    routed = jnp.concatenate(groups, axis=1)
    o_ref[...] = _layer_norm(ALPHA * x1 + (routed + shared), g_ref[...], b_ref[...])


def _combine2_kernel_into(wk_ref, x1_ref, g_ref_rows, wsg_ref, wsu_ref, wsd_ref, g_ref, b_ref, prev_ref, o_ref):
    del prev_ref
    _combine2_kernel(wk_ref, x1_ref, g_ref_rows, wsg_ref, wsu_ref, wsd_ref, g_ref, b_ref, o_ref)


def _combine2(wk_t, x1, gathered, w_sg, w_su, w_sd, ln_g, ln_b, tc, chunk, prev):
    T, D = x1.shape
    _, t_chunk, W = gathered.shape
    base = chunk * (t_chunk // tc)
    row = lambda i: (base + i, 0)
    c2 = lambda i: (0, 0)
    in_specs = [
        pl.BlockSpec((TOP_K, tc), lambda i: (0, base + i)),
        pl.BlockSpec((tc, D), row),
        pl.BlockSpec((TOP_K, tc, W), lambda i: (0, i, 0)),
        pl.BlockSpec(w_sg.shape, c2),
        pl.BlockSpec(w_su.shape, c2),
        pl.BlockSpec(w_sd.shape, c2),
        pl.BlockSpec((1, D), c2),
        pl.BlockSpec((1, D), c2),
    ]
    args = [wk_t, x1, gathered, w_sg, w_su, w_sd, ln_g, ln_b]
    if prev is None:
        body, aliases = _combine2_kernel, {}
    else:
        body, aliases = _combine2_kernel_into, {len(args): 0}
        in_specs.append(pl.BlockSpec(memory_space=pl.ANY))
        args.append(prev)
    return pl.pallas_call(
        body,
        grid=(t_chunk // tc,),
        in_specs=in_specs,
        out_specs=pl.BlockSpec((tc, D), row),
        out_shape=jax.ShapeDtypeStruct((T, D), jnp.float32),
        input_output_aliases=aliases,
        compiler_params=_cparams(("arbitrary",)),
        name="combine",
    )(*args)


def _combine_kernel(dest_ref, wk_ref, x1_ref, ys_hbm, wsg_ref, wsu_ref, wsd_ref, g_ref, b_ref,
                    o_ref, buf_ref, sem, *, tc):
    def issue(r, c):
        for k in range(TOP_K):
            _row_copy(ys_hbm, dest_ref[k, r], buf_ref.at[k], r, sem).start()
        return c

    def drain(r, c):
        for k in range(TOP_K):
            _row_copy(ys_hbm, dest_ref[k, r], buf_ref.at[k], r, sem).wait()
        return c

    lax.fori_loop(0, tc, issue, 0)
    x1 = x1_ref[...]
    xb = x1.astype(MXU_DTYPE)
    a = (_silu(_dot(xb, wsg_ref[...])) * _dot(xb, wsu_ref[...])).astype(MXU_DTYPE)
    shared = _dot(a, wsd_ref[...])
    lax.fori_loop(0, tc, drain, 0)
    wk = wk_ref[...]
    groups = [wk[:, 0:1] * v for v in _unpack_rows_f32(buf_ref[0])]
    for k in range(1, TOP_K):
        groups = [g + wk[:, k:k + 1] * v for g, v in zip(groups, _unpack_rows_f32(buf_ref[k]))]
    routed = jnp.concatenate(groups, axis=1)
    o_ref[...] = _layer_norm(ALPHA * x1 + (routed + shared), g_ref[...], b_ref[...])


def _combine(dest_t, wk, x1, ys, w_sg, w_su, w_sd, ln_g, ln_b, tc):
    T, D = x1.shape
    row = lambda i: (i, 0)
    c2 = lambda i: (0, 0)
    return pl.pallas_call(
        functools.partial(_combine_kernel, tc=tc),
        grid=(T // tc,),
        in_specs=[
            pl.BlockSpec((TOP_K, tc), lambda i: (0, i), memory_space=pltpu.SMEM),
            pl.BlockSpec((tc, TOP_K), row),
            pl.BlockSpec((tc, D), row),
            pl.BlockSpec(memory_space=pl.ANY),
            pl.BlockSpec(w_sg.shape, c2),
            pl.BlockSpec(w_su.shape, c2),
            pl.BlockSpec(w_sd.shape, c2),
            pl.BlockSpec((1, D), c2),
            pl.BlockSpec((1, D), c2),
        ],
        out_specs=pl.BlockSpec((tc, D), row),
        out_shape=jax.ShapeDtypeStruct((T, D), jnp.float32),
        scratch_shapes=[pltpu.VMEM((TOP_K, tc, ys.shape[1]), ys.dtype), pltpu.SemaphoreType.DMA],
        compiler_params=_cparams(("arbitrary",)),
        name="combine",
    )(dest_t, wk, x1, ys, w_sg, w_su, w_sd, ln_g, ln_b)


def _split_w_in(w_in):
    bf = MXU_DTYPE
    o_kv = Q_RANK
    o_ki = o_kv + KV_RANK
    o_iw = o_ki + IDX_DIM
    o_rest = o_iw + N_IDX_HEADS
    w_main = jnp.concatenate([w_in[:, :o_ki], w_in[:, o_rest:]], axis=1).astype(bf)
    w_small = jnp.pad(w_in[:, o_ki:o_rest], ((0, 0), (0, LANES - IDX_DIM - N_IDX_HEADS))).astype(bf)
    return w_main, w_small


def _stages(x, mem, w_in, q_norm_g, kv_norm_g, w_uq, w_uk, w_uv, w_qidx, rel_bias, conv_w, w_mem_k, w_mem_v, w_out, ln1_g, ln1_b, w_router, router_bias, w_e_gate, w_e_up, w_e_down, w_s_gate, w_s_up, w_s_down, ln2_g, ln2_b, upto=None):
    B, S, D = x.shape
    T = B * S
    bf = MXU_DTYPE
    l = 0
    res = {}
    x2 = x.reshape(T, D)
    w_main, w_small = _split_w_in(w_in[l])
    cq, ckv, ckvt, kidx, iwt, yb, yc = _proj(
        x2, mem, w_main, w_small, q_norm_g[l].reshape(1, -1), kv_norm_g[l].reshape(1, -1), conv_w[l],
        w_mem_k[l].astype(bf), w_mem_v[l].astype(bf), B, S, tm=min(512, S))
    res.update(c_q=cq, c_kv=ckv, k_idx=kidx, y_b=yb, y_c=yc,
               idx_w=jnp.swapaxes(iwt, 1, 2) / (N_IDX_HEADS ** -0.5 * IDX_DIM ** -0.5))
    if upto == "proj":
        return res
    bias_t = _bias_tiles(rel_bias)
    ya = _dsa(cq, iwt, kidx, ckv, ckvt,
              w_qidx[l].reshape(Q_RANK, -1).astype(bf), w_uq[l].reshape(Q_RANK, -1).astype(bf),
              jnp.transpose(w_uk[l], (1, 0, 2)).astype(bf), jnp.transpose(w_uv[l], (1, 2, 0)).astype(bf),
              bias_t, B, S)
    res.update(y_a=ya)
    if upto == "dsa":
        return res

    x1, x1p, sel_t, w_t, pos_t, cnt = _mix_router(
        x2, ya, yb, yc, w_out[l].astype(bf), ln1_g[l].reshape(1, -1), ln1_b[l].reshape(1, -1),
        w_router[l].T, router_bias[l].reshape(-1, 1), tm=min(512, T))
    res.update(x1=x1)

    counts = cnt[:, 0].astype(jnp.int32)
    padded = (counts + ROW_BLOCK - 1) // ROW_BLOCK * ROW_BLOCK
    pad_end = jnp.cumsum(padded)
    pad_start = pad_end - padded
    n_blocks = -(-(T * TOP_K) // ROW_BLOCK) + N_EXPERTS
    n_rows = n_blocks * ROW_BLOCK
    block_start = jnp.arange(n_blocks, dtype=jnp.int32) * ROW_BLOCK
    block_e = jnp.minimum(jnp.sum((pad_end[None, :] <= block_start[:, None]).astype(jnp.int32), axis=1),
                          N_EXPERTS - 1)
    n_used = (pad_end[-1:] // ROW_BLOCK).astype(jnp.int32)

    dest_t, wk_t = _compact(sel_t, w_t, pos_t, pad_start.astype(jnp.float32).reshape(-1, 1), tm=min(512, T))
    block_valid = jnp.clip((pad_start + counts)[block_e] - block_start, 0, ROW_BLOCK).astype(jnp.int32)
    bt = SC_SCATTER_ROWS
    idx3 = jnp.transpose(dest_t.reshape(TOP_K, T // bt, bt), (1, 0, 2))
    xs = _sc_scatter_rows(x1p, idx3, n_rows)
    ys = _experts(xs, block_e, block_valid, n_used, w_e_gate[l], w_e_up[l], w_e_down[l])
    n_chunks = COMBINE_CHUNKS if T % (COMBINE_CHUNKS * 256) == 0 else 1
    t_chunk = T // n_chunks
    out = None
    for c in range(n_chunks):
        idx_c = dest_t[:, c * t_chunk:(c + 1) * t_chunk].reshape(-1)
        gathered = _sc_gather_rows(ys, idx_c).reshape(TOP_K, t_chunk, -1)
        out = _combine2(wk_t, x1, gathered, w_s_gate[l].astype(bf), w_s_up[l].astype(bf), w_s_down[l].astype(bf),
                        ln2_g[l].reshape(1, -1), ln2_b[l].reshape(1, -1), tc=min(256, t_chunk), chunk=c, prev=out)
    res.update(out=out.reshape(B, S, D))
    return res


def kernel(x, mem, w_in, q_norm_g, kv_norm_g, w_uq, w_uk, w_uv, w_qidx, rel_bias, conv_w, w_mem_k, w_mem_v, w_out, ln1_g, ln1_b, w_router, router_bias, w_e_gate, w_e_up, w_e_down, w_s_gate, w_s_up, w_s_down, ln2_g, ln2_b):
    return _stages(x, mem, w_in, q_norm_g, kv_norm_g, w_uq, w_uk, w_uv, w_qidx, rel_bias, conv_w, w_mem_k, w_mem_v, w_out, ln1_g, ln1_b, w_router, router_bias, w_e_gate, w_e_up, w_e_down, w_s_gate, w_s_up, w_s_down, ln2_g, ln2_b)["out"]
```

```python
import functools
import math

import jax
import jax.numpy as jnp
from jax import lax
from jax.experimental import pallas as pl
from jax.experimental.pallas import tpu as pltpu
from jax.experimental.pallas import tpu_sc as plsc

N_HEADS_A = 8
HEAD_DIM = 64
Q_RANK = 256
KV_RANK = 128
N_IDX_HEADS = 8
IDX_DIM = 64
TOPK_MAX = 256
REL_BUCKETS = 32
REL_MAX_DIST = 128
CONV_CH = 256
CONV_WIDTH = 3
N_MEM_HEADS = 4
MIX_A = N_HEADS_A * HEAD_DIM
MIX_C = N_MEM_HEADS * HEAD_DIM
N_EXPERTS = 64
N_GROUPS = 8
GROUP_SIZE = N_EXPERTS // N_GROUPS
TOPK_GROUPS = 4
TOP_K = 8
D_EXPERT = 256
ROUTED_SCALE = 2.5
MOE_BLOCK = 256
DEPTH = 1
ALPHA = (2.0 * DEPTH) ** 0.25
LN_EPS = 1e-5
RMS_EPS = 1e-6
LOG2_E = math.log2(math.e)

LANES = 128
SUBLANES = 8
QB = 128
F32_LOWEST = -3.4028234663852886e38
VMEM_LIMIT = 56 * 1024 * 1024
MXU_DTYPE = jnp.bfloat16
ROW_BLOCK = 1024

_NT = (((1,), (1,)), ((), ()))


def _dot(a, b):
    return jnp.dot(a, b, preferred_element_type=jnp.float32)


def _dot_nt(a, b):
    return lax.dot_general(a, b, _NT, preferred_element_type=jnp.float32)


def _cparams(sem):
    return pltpu.CompilerParams(dimension_semantics=sem, vmem_limit_bytes=VMEM_LIMIT)


def _bias_kernel(rb_ref, o_ref):
    s = lax.broadcasted_iota(jnp.int32, (QB, QB), 0)
    t = lax.broadcasted_iota(jnp.int32, (QB, QB), 1)
    max_exact = REL_BUCKETS // 2
    for tile in range(3):
        n = jnp.maximum(t - s + (2 - tile) * QB, 0)
        nf = jnp.maximum(n.astype(jnp.float32), 1.0)
        large = max_exact + (jnp.log(nf / max_exact) / math.log(REL_MAX_DIST / max_exact)
                             * (REL_BUCKETS - max_exact)).astype(jnp.int32)
        large = jnp.minimum(large, REL_BUCKETS - 1)
        bucket = jnp.where(n < max_exact, n, large)
        for h in range(N_HEADS_A):
            acc = jnp.zeros((QB, QB), jnp.float32)
            for b in range(REL_BUCKETS):
                acc = jnp.where(bucket == b, rb_ref[b, h], acc)
            o_ref[tile, h] = acc * LOG2_E


def _bias_tiles(rel_bias):
    return pl.pallas_call(
        _bias_kernel,
        in_specs=[pl.BlockSpec(memory_space=pltpu.SMEM)],
        out_specs=pl.BlockSpec(memory_space=pltpu.VMEM),
        out_shape=jax.ShapeDtypeStruct((3, N_HEADS_A, QB, QB), jnp.float32),
        name="bias_tiles",
    )(rel_bias)


_MAIN_COLS = Q_RANK + KV_RANK + 3 * CONV_CH + MIX_C


def _proj_kernel(x_ref, mem_ref, wm_ref, ws_ref, qg_ref, kvg_ref, cw_ref, wmk_ref, wmv_ref,
                 cq_ref, ckv_ref, ckvt_ref, kidx_ref, iwt_ref, yb_ref, yc_ref,
                 carry_ref, mk_ref, mv_ref, *, tm):
    si = pl.program_id(1)

    @pl.when(si == 0)
    def _():
        carry_ref[...] = jnp.zeros_like(carry_ref)
        mb = mem_ref[0].astype(MXU_DTYPE)
        mk_ref[...] = _dot(mb, wmk_ref[...]).astype(MXU_DTYPE)
        mv_ref[...] = _dot(mb, wmv_ref[...]).astype(MXU_DTYPE)

    xb = x_ref[...].astype(MXU_DTYPE)
    p = _dot(xb, wm_ref[...])
    small = _dot(xb, ws_ref[...])

    o = 0
    cq = p[:, o:o + Q_RANK]; o += Q_RANK
    ckv = p[:, o:o + KV_RANK]; o += KV_RANK
    g_b = p[:, o:o + CONV_CH]; o += CONV_CH
    g_c = p[:, o:o + CONV_CH]; o += CONV_CH
    h_c = p[:, o:o + CONV_CH]; o += CONV_CH
    q_mem = p[:, o:o + MIX_C]

    cq = cq * lax.rsqrt(jnp.mean(cq * cq, axis=-1, keepdims=True) + RMS_EPS) * qg_ref[...]
    ckv = ckv * lax.rsqrt(jnp.mean(ckv * ckv, axis=-1, keepdims=True) + RMS_EPS) * kvg_ref[...]
    cq_ref[...] = cq.astype(MXU_DTYPE)
    ckv_b = ckv.astype(MXU_DTYPE)
    ckv_ref[...] = ckv_b
    ckvt_ref[0] = ckv.T.astype(MXU_DTYPE)

    kidx_ref[...] = small[:, :IDX_DIM].astype(MXU_DTYPE)
    small_t = small.T
    iwt_ref[0] = small_t[IDX_DIM:IDX_DIM + N_IDX_HEADS, :] * (N_IDX_HEADS ** -0.5 * IDX_DIM ** -0.5)

    u = g_c * h_c
    rows = lax.broadcasted_iota(jnp.int32, (tm, 1), 0)
    c6 = carry_ref[SUBLANES - 2:SUBLANES - 1, :]
    c7 = carry_ref[SUBLANES - 1:SUBLANES, :]
    u1 = jnp.where(rows == 0, c7, pltpu.roll(u, 1, 0))
    u2 = jnp.where(rows == 0, c6, jnp.where(rows == 1, c7, pltpu.roll(u, 2, 0)))
    y = cw_ref[0:1, :] * u2
    y = y + cw_ref[1:2, :] * u1
    y = y + cw_ref[2:3, :] * u
    yb_ref[...] = (g_b * y).astype(MXU_DTYPE)
    carry_ref[...] = u[tm - SUBLANES:, :]

    qm = q_mem.astype(MXU_DTYPE)
    outs = []
    for h in range(N_MEM_HEADS):
        sl = slice(h * HEAD_DIM, (h + 1) * HEAD_DIM)
        lg = _dot_nt(qm[:, sl], mk_ref[:, sl]) * (HEAD_DIM ** -0.5)
        lg = lg - jnp.max(lg, axis=-1, keepdims=True)
        e = jnp.exp(lg)
        pr = e / jnp.sum(e, axis=-1, keepdims=True)
        outs.append(_dot(pr.astype(MXU_DTYPE), mv_ref[:, sl]))
    yc_ref[...] = jnp.concatenate(outs, axis=-1).astype(MXU_DTYPE)


def _proj(x2, mem, w_main, w_small, q_g, kv_g, conv_w, w_mk, w_mv, B, S, tm):
    T, D = x2.shape
    n_mem = mem.shape[1]
    ns = S // tm
    row = lambda b, s: (b * ns + s, 0)
    const2 = lambda b, s: (0, 0)
    bf = MXU_DTYPE
    return pl.pallas_call(
        functools.partial(_proj_kernel, tm=tm),
        grid=(B, ns),
        in_specs=[
            pl.BlockSpec((tm, D), row),
            pl.BlockSpec((1, n_mem, D), lambda b, s: (b, 0, 0)),
            pl.BlockSpec(w_main.shape, const2),
            pl.BlockSpec(w_small.shape, const2),
            pl.BlockSpec(q_g.shape, const2),
            pl.BlockSpec(kv_g.shape, const2),
            pl.BlockSpec(conv_w.shape, const2),
            pl.BlockSpec(w_mk.shape, const2),
            pl.BlockSpec(w_mv.shape, const2),
        ],
        out_specs=[
            pl.BlockSpec((tm, Q_RANK), row),
            pl.BlockSpec((tm, KV_RANK), row),
            pl.BlockSpec((1, KV_RANK, tm), lambda b, s: (b, 0, s)),
            pl.BlockSpec((tm, IDX_DIM), row),
            pl.BlockSpec((1, N_IDX_HEADS, tm), lambda b, s: (b, 0, s)),
            pl.BlockSpec((tm, CONV_CH), row),
            pl.BlockSpec((tm, MIX_C), row),
        ],
        out_shape=[
            jax.ShapeDtypeStruct((T, Q_RANK), bf),
            jax.ShapeDtypeStruct((T, KV_RANK), bf),
            jax.ShapeDtypeStruct((B, KV_RANK, S), bf),
            jax.ShapeDtypeStruct((T, IDX_DIM), bf),
            jax.ShapeDtypeStruct((B, N_IDX_HEADS, S), jnp.float32),
            jax.ShapeDtypeStruct((T, CONV_CH), bf),
            jax.ShapeDtypeStruct((T, MIX_C), bf),
        ],
        scratch_shapes=[
            pltpu.VMEM((SUBLANES, CONV_CH), jnp.float32),
            pltpu.VMEM((n_mem, MIX_C), bf),
            pltpu.VMEM((n_mem, MIX_C), bf),
        ],
        compiler_params=_cparams(("arbitrary", "arbitrary")),
        name="proj",
    )(x2, mem, w_main, w_small, q_g, kv_g, conv_w, w_mk, w_mv)


def _key_to_f32(key):
    bits = jnp.where(key < 0, key ^ jnp.int32(0x7FFFFFFF), key)
    return pltpu.bitcast(bits, jnp.float32)


def _colsum8(v):
    return jnp.sum(v.reshape(QB // SUBLANES, SUBLANES, QB), axis=0)


def _colmax8(v):
    return jnp.max(v.reshape(QB // SUBLANES, SUBLANES, QB), axis=0)


UNROLL_WIDTHS = (8, 4, 2, 1)


def _dsa_kernel(cq_ref, iwt_ref, kidx_ref, ckv_ref, ckvt_ref, wqi_ref, wuq_ref, wuk_ref, wuvt_ref,
                bias_ref, o_ref, wfold_ref, qidx_ref, qlat_ref, score_ref, mask_ref, logit_ref, acc_ref,
                *, k_sel, idx_bits):
    i = pl.program_id(1)
    f32 = jnp.float32
    bf = MXU_DTYPE
    n_blocks = i + 1
    n_blocks = n_blocks + jnp.where((n_blocks % 4 == 3) & (n_blocks < pl.num_programs(1)), 1, 0)
    s_loc = lax.broadcasted_iota(jnp.int32, (QB, QB), 0)
    t_glob = i * QB + lax.broadcasted_iota(jnp.int32, (QB, QB), 1)

    def blk(jb):
        return pl.multiple_of(jb * QB, QB)

    def block_loop(fn, init):
        c, start = init, 0
        for width in UNROLL_WIDTHS:
            n = (n_blocks - start) // width
            c = lax.fori_loop(0, n, lambda it, c, w=width, s=start: fn(s + it * w, w, c), c)
            start = start + n * width
        return c

    @pl.when(i == 0)
    def _():
        for h in range(N_HEADS_A):
            wfold_ref[:, h * KV_RANK:(h + 1) * KV_RANK] = (
                _dot_nt(wuq_ref[:, h * HEAD_DIM:(h + 1) * HEAD_DIM], wuk_ref[h])
                * (HEAD_DIM ** -0.5 * LOG2_E)).astype(bf)

    cq = cq_ref[...]
    q_idx = _dot(cq, wqi_ref[...]).astype(bf)
    q_lat = _dot(cq, wfold_ref[...]).astype(bf)
    for h in range(N_HEADS_A):
        qidx_ref[h * QB:(h + 1) * QB, :] = q_idx[:, h * IDX_DIM:(h + 1) * IDX_DIM]
        qlat_ref[h * QB:(h + 1) * QB, :] = q_lat[:, h * KV_RANK:(h + 1) * KV_RANK]
    iw = iwt_ref[0]

    def score_body(jb0, nb, c):
        d_blk = _dot_nt(kidx_ref[pl.ds(blk(jb0), nb * QB), :], qidx_ref[...])
        for sb in range(nb):
            off = blk(jb0 + sb)
            d_all = d_blk[sb * QB:(sb + 1) * QB, :]
            acc = jnp.maximum(d_all[:, 0:QB], 0.0) * iw[0:1, :]
            for h in range(1, N_IDX_HEADS):
                acc = acc + jnp.maximum(d_all[:, h * QB:(h + 1) * QB], 0.0) * iw[h:h + 1, :]
            score_ref[pl.ds(off, QB), :] = jnp.where(s_loc + off <= t_glob, acc + 0.0, F32_LOWEST)
        return c

    block_loop(score_body, 0)

    def count_where(pred):
        def body(jb0, nb, acc):
            for sb in range(nb):
                off = blk(jb0 + sb)
                acc = acc + _colsum8(jnp.where(pred(score_ref[pl.ds(off, QB), :], off), 1.0, 0.0))
            return acc
        acc = block_loop(body, jnp.zeros((SUBLANES, QB), f32))
        return jnp.sum(acc, axis=0, keepdims=True)

    kf = float(k_sel)

    def search():
        c0 = count_where(lambda sc, off: sc >= 0.0)
        cand0 = jnp.where(c0 >= kf, jnp.int32(0), jnp.int32(-2 ** 31))

        def bit_body(it, cand):
            trial = cand + lax.shift_left(jnp.int32(1), 30 - it)
            tf = _key_to_f32(trial)
            cnt = count_where(lambda sc, off: sc >= tf)
            return jnp.where(cnt >= kf, trial, cand)

        cand = lax.fori_loop(0, 31, bit_body, cand0)
        thr = _key_to_f32(cand)
        n_gt = count_where(lambda sc, off: sc > thr)
        n_eq = count_where(lambda sc, off: sc == thr)
        need = kf - n_gt

        def tie_search():
            def tbody(it, xcut):
                trial = xcut + lax.shift_left(jnp.int32(1), idx_bits - 1 - it)
                cnt = count_where(lambda sc, off: (sc == thr) & (s_loc + off < trial))
                return jnp.where(cnt < need, trial, xcut)
            return lax.fori_loop(0, idx_bits, tbody, jnp.zeros((1, QB), jnp.int32))

        any_extra = jnp.max(n_eq - need) > 0.0
        xcut = lax.cond(any_extra, tie_search, lambda: jnp.full((1, QB), 2 ** idx_bits - 1, jnp.int32))
        return thr, xcut

    def no_search():
        return jnp.full((1, QB), F32_LOWEST, f32), jnp.full((1, QB), 2 ** idx_bits - 1, jnp.int32)

    thr, xcut = lax.cond((i + 1) * QB > k_sel, search, no_search)

    def mask_body(jb0, nb, c):
        for sb in range(nb):
            off = blk(jb0 + sb)
            sc = score_ref[pl.ds(off, QB), :]
            s_glob = s_loc + off
            keep = ((sc > thr) | ((sc == thr) & (s_glob <= xcut))) & (s_glob <= t_glob)
            mask_ref[pl.ds(off, QB), :] = jnp.where(keep, 0.0, -jnp.inf)
        return c

    block_loop(mask_body, 0)

    def p1_body(jb0, nb, m8):
        m8 = list(m8)
        lg_blk = _dot_nt(ckv_ref[pl.ds(blk(jb0), nb * QB), :], qlat_ref[...])
        for sb in range(nb):
            off = blk(jb0 + sb)
            lg = lg_blk[sb * QB:(sb + 1) * QB, :]
            msk = mask_ref[pl.ds(off, QB), :]
            bsel = jnp.clip(jb0 + sb - i + 2, 0, 2)
            for h in range(N_HEADS_A):
                lgh = lg[:, h * QB:(h + 1) * QB] + bias_ref[bsel, h] + msk
                logit_ref[pl.ds(off, QB), h * QB:(h + 1) * QB] = lgh
                m8[h] = jnp.maximum(m8[h], _colmax8(lgh))
        return tuple(m8)

    m8 = block_loop(p1_body, tuple(jnp.full((SUBLANES, QB), -jnp.inf, f32) for _ in range(N_HEADS_A)))
    m_row = [jnp.max(m, axis=0, keepdims=True) for m in m8]

    acc_ref[...] = jnp.zeros_like(acc_ref)

    def p2_body(jb0, nb, l8):
        l8 = list(l8)
        off = blk(jb0)
        rows = nb * QB
        ps = []
        for h in range(N_HEADS_A):
            p = jnp.exp2(logit_ref[pl.ds(off, rows), h * QB:(h + 1) * QB] - m_row[h])
            l8[h] = l8[h] + jnp.sum(p.reshape(rows // SUBLANES, SUBLANES, QB), axis=0)
            ps.append(p.astype(bf))
        acc_ref[...] += _dot(ckvt_ref[0, :, pl.ds(off, rows)], jnp.concatenate(ps, axis=1))
        return tuple(l8)

    l8 = block_loop(p2_body, tuple(jnp.zeros((SUBLANES, QB), f32) for _ in range(N_HEADS_A)))

    outs = []
    for h in range(N_HEADS_A):
        l_row = jnp.sum(l8[h], axis=0, keepdims=True)
        o_lat_t = (acc_ref[:, h * QB:(h + 1) * QB] / l_row).astype(bf)
        outs.append(_dot(wuvt_ref[h], o_lat_t))
    o_ref[...] = jnp.concatenate(outs, axis=0).T.astype(o_ref.dtype)


def _dsa(cq, iwt, kidx, ckv, ckvt, w_qidx, w_uq, w_uk_h, w_uvt_h, bias_tiles, B, S):
    T = cq.shape[0]
    assert S % QB == 0 and QB >= REL_MAX_DIST
    nq = S // QB
    k_sel = min(TOPK_MAX, S // 4)
    idx_bits = max(1, (S - 1).bit_length())
    c2 = lambda b, i: (0, 0)
    c3 = lambda b, i: (0, 0, 0)
    return pl.pallas_call(
        functools.partial(_dsa_kernel, k_sel=k_sel, idx_bits=idx_bits),
        grid=(B, nq),
        in_specs=[
            pl.BlockSpec((QB, Q_RANK), lambda b, i: (b * nq + i, 0)),
            pl.BlockSpec((1, N_IDX_HEADS, QB), lambda b, i: (b, 0, i)),
            pl.BlockSpec((S, IDX_DIM), lambda b, i: (b, 0)),
            pl.BlockSpec((S, KV_RANK), lambda b, i: (b, 0)),
            pl.BlockSpec((1, KV_RANK, S), lambda b, i: (b, 0, 0)),
            pl.BlockSpec(w_qidx.shape, c2),
            pl.BlockSpec(w_uq.shape, c2),
            pl.BlockSpec(w_uk_h.shape, c3),
            pl.BlockSpec(w_uvt_h.shape, c3),
            pl.BlockSpec(bias_tiles.shape, lambda b, i: (0, 0, 0, 0)),
        ],
        out_specs=pl.BlockSpec((QB, MIX_A), lambda b, i: (b * nq + i, 0)),
        out_shape=jax.ShapeDtypeStruct((T, MIX_A), MXU_DTYPE),
        scratch_shapes=[
            pltpu.VMEM((Q_RANK, N_HEADS_A * KV_RANK), MXU_DTYPE),
            pltpu.VMEM((N_IDX_HEADS * QB, IDX_DIM), MXU_DTYPE),
            pltpu.VMEM((N_HEADS_A * QB, KV_RANK), MXU_DTYPE),
            pltpu.VMEM((S, QB), jnp.float32),
            pltpu.VMEM((S, QB), jnp.float32),
            pltpu.VMEM((S, N_HEADS_A * QB), jnp.float32),
            pltpu.VMEM((KV_RANK, N_HEADS_A * QB), jnp.float32),
        ],
        compiler_params=_cparams(("arbitrary", "arbitrary")),
        name="dsa",
    )(cq, iwt, kidx, ckv, ckvt, w_qidx, w_uq, w_uk_h, w_uvt_h, bias_tiles)


def _layer_norm(xf, g, b):
    mu = jnp.mean(xf, axis=-1, keepdims=True)
    xc = xf - mu
    var = jnp.mean(xc * xc, axis=-1, keepdims=True)
    return xc * lax.rsqrt(var + LN_EPS) * g + b


def _rank_rows(v, n):
    ri = lax.broadcasted_iota(jnp.int32, v.shape, 0)
    rank = jnp.zeros(v.shape, jnp.float32)
    for r2 in range(n):
        row = v[r2:r2 + 1, :]
        beats = (row > v) | ((row == v) & (ri > r2))
        rank = rank + jnp.where(beats, 1.0, 0.0)
    return rank


def _top_rows(v, k):
    n = v.shape[0]
    ri = lax.broadcasted_iota(jnp.int32, v.shape, 0)
    sel = jnp.zeros(v.shape, jnp.float32)
    for _ in range(k):
        m = jnp.max(v, axis=0, keepdims=True)
        first = jnp.min(jnp.where(v == m, ri, n), axis=0, keepdims=True)
        pick = ri == first
        sel = jnp.where(pick, 1.0, sel)
        v = jnp.where(pick, -jnp.inf, v)
    return sel > 0.5


def _pack_factor():
    return 4 // jnp.dtype(MXU_DTYPE).itemsize


def _pack_rows(x):
    if _pack_factor() == 1:
        return pltpu.bitcast(x, jnp.int32)
    half = x.shape[1] // 2
    b = pltpu.bitcast(x.astype(MXU_DTYPE).astype(jnp.float32), jnp.int32)
    return b[:, half:] | lax.shift_right_logical(b[:, :half], jnp.int32(16))


_HIGH_HALF = -(1 << 16)


def _unpack_rows_f32(p):
    if _pack_factor() == 1:
        return [pltpu.bitcast(p, jnp.float32)]
    lo = pltpu.bitcast(lax.shift_left(p, jnp.int32(16)), jnp.float32)
    hi = pltpu.bitcast(p & jnp.int32(_HIGH_HALF), jnp.float32)
    return [lo, hi]


def _unpack_rows(p):
    return [v.astype(MXU_DTYPE) for v in _unpack_rows_f32(p)]


def _mix_router_kernel(x_ref, ya_ref, yb_ref, yc_ref, wo_ref, g_ref, b_ref, wrt_ref, rb_ref, exp_ref,
                       x1_ref, x1p_ref, sel_ref, w_ref, pos_ref, cnt_ref, base_ref, *, tm):
    step = pl.program_id(0)
    f32 = jnp.float32

    @pl.when(step == 0)
    def _():
        base_ref[...] = jnp.zeros_like(base_ref)

    mix = _dot(ya_ref[...], wo_ref[0:MIX_A, :])
    mix = mix + _dot(yb_ref[...], wo_ref[MIX_A:MIX_A + CONV_CH, :])
    mix = mix + _dot(yc_ref[...], wo_ref[MIX_A + CONV_CH:, :])
    x1 = _layer_norm(ALPHA * x_ref[...] + mix, g_ref[...], b_ref[...])
    x1_ref[...] = x1
    x1p_ref[...] = _pack_rows(x1)

    lg = lax.dot_general(wrt_ref[...], x1, _NT, precision=lax.Precision.HIGHEST, preferred_element_type=f32)
    s = 1.0 / (1.0 + jnp.exp(-lg))
    sc = s + rb_ref[...]

    g3 = sc.reshape(N_GROUPS, GROUP_SIZE, tm)
    m1 = jnp.max(g3, axis=1, keepdims=True)
    is_m1 = g3 == m1
    n_m1 = jnp.sum(jnp.where(is_m1, 1.0, 0.0), axis=1, keepdims=True)
    m2 = jnp.max(jnp.where(is_m1, -jnp.inf, g3), axis=1, keepdims=True)
    gscore = (m1 + jnp.where(n_m1 > 1.0, m1, m2)).reshape(N_GROUPS, tm)
    gsel = jnp.where(_rank_rows(gscore, N_GROUPS) < float(TOPK_GROUPS), 1.0, 0.0)
    emask = _dot(exp_ref[...], gsel.astype(MXU_DTYPE)) > 0.5
    masked = jnp.where(emask, sc, -jnp.inf)
    sel = _top_rows(masked, TOP_K) & emask
    self_ = jnp.where(sel, 1.0, 0.0)
    top_s = jnp.where(sel, s, 0.0)
    w = top_s / jnp.sum(top_s, axis=0, keepdims=True) * ROUTED_SCALE

    t_r = lax.broadcasted_iota(jnp.int32, (tm, tm), 0)
    t_c = lax.broadcasted_iota(jnp.int32, (tm, tm), 1)
    upper = jnp.where(t_r < t_c, 1.0, 0.0).astype(MXU_DTYPE)
    pref = _dot(self_.astype(MXU_DTYPE), upper)
    base = base_ref[...]
    sel_ref[...] = self_
    w_ref[...] = w
    pos_ref[...] = base + pref
    base = base + jnp.sum(self_, axis=1, keepdims=True)
    base_ref[...] = base
    cnt_ref[...] = jnp.broadcast_to(base, cnt_ref.shape)


def _mix_router(x2, ya, yb, yc, w_out, ln_g, ln_b, w_router_t, router_bias, tm):
    T, D = x2.shape
    E = N_EXPERTS
    expand = (jnp.arange(E)[:, None] // GROUP_SIZE == jnp.arange(N_GROUPS)[None, :]).astype(MXU_DTYPE)
    row = lambda i: (i, 0)
    col = lambda i: (0, i)
    c2 = lambda i: (0, 0)
    f32 = jnp.float32
    return pl.pallas_call(
        functools.partial(_mix_router_kernel, tm=tm),
        grid=(T // tm,),
        in_specs=[
            pl.BlockSpec((tm, D), row),
            pl.BlockSpec((tm, MIX_A), row),
            pl.BlockSpec((tm, CONV_CH), row),
            pl.BlockSpec((tm, MIX_C), row),
            pl.BlockSpec(w_out.shape, c2),
            pl.BlockSpec((1, D), c2),
            pl.BlockSpec((1, D), c2),
            pl.BlockSpec((E, D), c2),
            pl.BlockSpec((E, 1), c2),
            pl.BlockSpec((E, N_GROUPS), c2),
        ],
        out_specs=[
            pl.BlockSpec((tm, D), row),
            pl.BlockSpec((tm, D // _pack_factor()), row),
            pl.BlockSpec((E, tm), col),
            pl.BlockSpec((E, tm), col),
            pl.BlockSpec((E, tm), col),
            pl.BlockSpec((E, LANES), c2),
        ],
        out_shape=[
            jax.ShapeDtypeStruct((T, D), f32),
            jax.ShapeDtypeStruct((T, D // _pack_factor()), jnp.int32),
            jax.ShapeDtypeStruct((E, T), f32),
            jax.ShapeDtypeStruct((E, T), f32),
            jax.ShapeDtypeStruct((E, T), f32),
            jax.ShapeDtypeStruct((E, LANES), f32),
        ],
        scratch_shapes=[pltpu.VMEM((E, 1), f32)],
        compiler_params=_cparams(("arbitrary",)),
        name="mix_router",
    )(x2, ya, yb, yc, w_out, ln_g, ln_b, w_router_t, router_bias, expand)


def _compact_kernel(sel_ref, w_ref, pos_ref, pstart_ref, low_ref, dest_ref, wk_ref):
    sel = sel_ref[...]
    on = sel > 0.5
    rank = _dot(low_ref[...], sel.astype(MXU_DTYPE))
    row = pstart_ref[...] + pos_ref[...]
    w = w_ref[...]
    dests, ws = [], []
    for k in range(TOP_K):
        m = on & (rank == float(k))
        dests.append(jnp.sum(jnp.where(m, row, 0.0), axis=0, keepdims=True))
        ws.append(jnp.sum(jnp.where(m, w, 0.0), axis=0, keepdims=True))
    dest_ref[...] = jnp.concatenate(dests, axis=0).astype(jnp.int32)
    wk_ref[...] = jnp.concatenate(ws, axis=0)


def _compact(sel_t, w_t, pos_t, pad_start, tm):
    E, T = sel_t.shape
    lower = (jnp.arange(E)[None, :] < jnp.arange(E)[:, None]).astype(MXU_DTYPE)
    col = lambda i: (0, i)
    c2 = lambda i: (0, 0)
    return pl.pallas_call(
        _compact_kernel,
        grid=(T // tm,),
        in_specs=[pl.BlockSpec((E, tm), col), pl.BlockSpec((E, tm), col), pl.BlockSpec((E, tm), col),
                  pl.BlockSpec((E, 1), c2), pl.BlockSpec((E, E), c2)],
        out_specs=[pl.BlockSpec((TOP_K, tm), col), pl.BlockSpec((TOP_K, tm), col)],
        out_shape=[jax.ShapeDtypeStruct((TOP_K, T), jnp.int32), jax.ShapeDtypeStruct((TOP_K, T), jnp.float32)],
        compiler_params=_cparams(("arbitrary",)),
        name="route_compact",
    )(sel_t, w_t, pos_t, pad_start, lower)


def _row_copy(src, s, dst, d, sem):
    return pltpu.make_async_copy(src.at[pl.ds(s, 1)], dst.at[pl.ds(d, 1)], sem)


def _dispatch_kernel(flo_ref, fhi_ref, dest_ref, x_ref, xs_hbm, zero_ref, sem, zsem, *, td):
    step = pl.program_id(0)

    @pl.when(step == 0)
    def _():
        zero_ref[...] = jnp.zeros_like(zero_ref)

        def per_expert(fn):
            def ebody(e, c):
                lax.fori_loop(flo_ref[e], fhi_ref[e], lambda r, c2: (fn(r), c2)[1], 0)
                return c
            lax.fori_loop(0, N_EXPERTS, ebody, 0)

        per_expert(lambda r: _row_copy(zero_ref, 0, xs_hbm, r, zsem).start())
        per_expert(lambda r: _row_copy(zero_ref, 0, xs_hbm, r, zsem).wait())

    def issue(r, c):
        for k in range(TOP_K):
            _row_copy(x_ref, r, xs_hbm, dest_ref[k, r], sem).start()
        return c

    def drain(r, c):
        for k in range(TOP_K):
            _row_copy(x_ref, r, xs_hbm, dest_ref[k, r], sem).wait()
        return c

    lax.fori_loop(0, td, issue, 0)
    lax.fori_loop(0, td, drain, 0)


def _dispatch(dest_t, x1p, fill_lo, fill_hi, n_rows, td):
    T, W = x1p.shape
    return pl.pallas_call(
        functools.partial(_dispatch_kernel, td=td),
        grid_spec=pltpu.PrefetchScalarGridSpec(
            num_scalar_prefetch=2,
            grid=(T // td,),
            in_specs=[
                pl.BlockSpec((TOP_K, td), lambda i, lo, hi: (0, i), memory_space=pltpu.SMEM),
                pl.BlockSpec((td, W), lambda i, lo, hi: (i, 0)),
            ],
            out_specs=pl.BlockSpec(memory_space=pl.ANY),
            scratch_shapes=[pltpu.VMEM((SUBLANES, W), x1p.dtype),
                            pltpu.SemaphoreType.DMA, pltpu.SemaphoreType.DMA],
        ),
        out_shape=jax.ShapeDtypeStruct((n_rows, W), x1p.dtype),
        compiler_params=_cparams(("arbitrary",)),
        name="dispatch",
    )(fill_lo, fill_hi, dest_t, x1p)


def _silu(g):
    return g / (1.0 + jnp.exp(-g))


def _expert_kernel(be_ref, nv_ref, nu_ref, xs_ref, wg_ref, wu_ref, wd_ref, ys_ref, wgb_ref, wub_ref, wdb_ref):
    i = pl.program_id(0)

    @pl.when((i == 0) | (be_ref[i] != be_ref[jnp.maximum(i - 1, 0)]))
    def _():
        wgb_ref[...] = wg_ref[0].astype(MXU_DTYPE)
        wub_ref[...] = wu_ref[0].astype(MXU_DTYPE)
        wdb_ref[...] = wd_ref[0].astype(MXU_DTYPE)

    @pl.when(i < nu_ref[0])
    def _():
        live = lax.broadcasted_iota(jnp.int32, (ROW_BLOCK, 1), 0) < nv_ref[i]
        parts = [jnp.where(live, v, jnp.zeros_like(v)) for v in _unpack_rows(xs_ref[...])]
        dk = wgb_ref.shape[0] // len(parts)

        def proj(w_ref):
            acc = _dot(parts[0], w_ref[0:dk, :])
            for n in range(1, len(parts)):
                acc = acc + _dot(parts[n], w_ref[n * dk:(n + 1) * dk, :])
            return acc

        a = (_silu(proj(wgb_ref)) * proj(wub_ref)).astype(MXU_DTYPE)
        ys_ref[...] = _pack_rows(_dot(a, wdb_ref[...]))


def _experts(xs, block_e, block_valid, n_used, w_gate, w_up, w_down):
    n_rows, W = xs.shape
    D = w_gate.shape[1]
    n_blocks = n_rows // ROW_BLOCK
    blk = lambda i, be, nv, nu: (jnp.minimum(i, nu[0] - 1), 0)
    wsel = lambda i, be, nv, nu: (be[i], 0, 0)
    return pl.pallas_call(
        _expert_kernel,
        grid_spec=pltpu.PrefetchScalarGridSpec(
            num_scalar_prefetch=3,
            grid=(n_blocks,),
            in_specs=[
                pl.BlockSpec((ROW_BLOCK, W), blk),
                pl.BlockSpec((1, D, D_EXPERT), wsel),
                pl.BlockSpec((1, D, D_EXPERT), wsel),
                pl.BlockSpec((1, D_EXPERT, D), wsel),
            ],
            out_specs=pl.BlockSpec((ROW_BLOCK, W), blk),
            scratch_shapes=[pltpu.VMEM((D, D_EXPERT), MXU_DTYPE), pltpu.VMEM((D, D_EXPERT), MXU_DTYPE),
                            pltpu.VMEM((D_EXPERT, D), MXU_DTYPE)],
        ),
        out_shape=jax.ShapeDtypeStruct((n_rows, W), xs.dtype),
        compiler_params=_cparams(("arbitrary",)),
        name="experts",
    )(block_e, block_valid, n_used, xs, w_gate, w_up, w_down)


SC_CORES = 2
SC_SUBCORES = 16
SC_GATHER_ROWS = 64
COMBINE_CHUNKS = 4


def _sc_gather_rows(table, idx):
    n = idx.shape[0]
    w = table.shape[1]
    n_workers = SC_CORES * SC_SUBCORES
    per_worker = n // n_workers
    assert n % n_workers == 0 and per_worker % SC_GATHER_ROWS == 0
    mesh = plsc.VectorSubcoreMesh(core_axis_name="c", subcore_axis_name="s")

    @functools.partial(
        pl.kernel, mesh=mesh,
        out_type=jax.ShapeDtypeStruct((n, w), table.dtype),
        scratch_types=[
            pltpu.VMEM((2, SC_GATHER_ROWS), jnp.int32),
            pltpu.VMEM((2, SC_GATHER_ROWS, w), table.dtype),
            pltpu.SemaphoreType.DMA((2,)),
        ],
        name="sc_gather_rows",
    )
    def gather(table_hbm, idx_hbm, out_hbm, idx_v, rows_v, sem):
        wid = lax.axis_index("s") * SC_CORES + lax.axis_index("c")
        base = wid * per_worker
        n_steps = per_worker // SC_GATHER_ROWS

        def gather_copy(slot):
            return pltpu.make_async_copy(table_hbm.at[idx_v.at[slot]], rows_v.at[slot], sem.at[slot])

        def start(step, slot):
            pltpu.sync_copy(idx_hbm.at[pl.ds(base + step * SC_GATHER_ROWS, SC_GATHER_ROWS)], idx_v.at[slot])
            gather_copy(slot).start()

        start(0, 0)

        @pl.loop(0, n_steps, step=2)
        def _(g):
            for slot in range(2):
                step = g + slot

                @pl.when(step + 1 < n_steps)
                def _():
                    start(step + 1, 1 - slot)

                gather_copy(slot).wait()
                pltpu.sync_copy(rows_v.at[slot], out_hbm.at[pl.ds(base + step * SC_GATHER_ROWS, SC_GATHER_ROWS)])

    return gather(table, idx)


SC_SCATTER_ROWS = 64


def _sc_scatter_rows(rows, idx3, n_out):
    n_src, w = rows.shape
    n_chunks, n_dst, batch = idx3.shape
    n_workers = SC_CORES * SC_SUBCORES
    assert batch == SC_SCATTER_ROWS and n_chunks * batch == n_src and n_chunks % (2 * n_workers) == 0
    per_worker = n_chunks // n_workers
    mesh = plsc.VectorSubcoreMesh(core_axis_name="c", subcore_axis_name="s")

    @functools.partial(
        pl.kernel, mesh=mesh,
        out_type=jax.ShapeDtypeStruct((n_out, w), rows.dtype),
        scratch_types=[
            pltpu.VMEM((2, n_dst, batch), jnp.int32),
            pltpu.VMEM((2, batch, w), rows.dtype),
            pltpu.SemaphoreType.DMA((2,)),
            pltpu.SemaphoreType.DMA,
        ],
        name="sc_scatter_rows",
    )
    def scatter(rows_hbm, idx_hbm, out_hbm, idx_v, rows_v, load_sem, store_sem):
        wid = lax.axis_index("s") * SC_CORES + lax.axis_index("c")

        def load_copy(step, slot):
            c = wid * per_worker + step
            return pltpu.make_async_copy(rows_hbm.at[pl.ds(c * batch, batch)], rows_v.at[slot], load_sem.at[slot])

        def load(step, slot):
            pltpu.sync_copy(idx_hbm.at[wid * per_worker + step], idx_v.at[slot])
            load_copy(step, slot).start()

        def store_copy(slot, k):
            return pltpu.make_async_copy(rows_v.at[slot], out_hbm.at[idx_v.at[slot].at[k]], store_sem)

        load(0, 0)

        @pl.loop(0, per_worker, step=2)
        def _(g):
            for slot in range(2):
                step = g + slot

                @pl.when(step + 1 < per_worker)
                def _():
                    load(step + 1, 1 - slot)

                load_copy(step, slot).wait()
                for k in range(n_dst):
                    store_copy(slot, k).start()
                for k in range(n_dst):
                    store_copy(slot, k).wait()

    return scatter(rows, idx3)


def _shared_kernel(x1_ref, wsg_ref, wsu_ref, wsd_ref, o_ref):
    xb = x1_ref[...].astype(MXU_DTYPE)
    a = (_silu(_dot(xb, wsg_ref[...])) * _dot(xb, wsu_ref[...])).astype(MXU_DTYPE)
    o_ref[...] = _dot(a, wsd_ref[...])


def _shared_expert(x1, w_sg, w_su, w_sd, tm):
    T, D = x1.shape
    row = lambda i: (i, 0)
    c2 = lambda i: (0, 0)
    return pl.pallas_call(
        _shared_kernel,
        grid=(T // tm,),
        in_specs=[pl.BlockSpec((tm, D), row), pl.BlockSpec(w_sg.shape, c2), pl.BlockSpec(w_su.shape, c2),
                  pl.BlockSpec(w_sd.shape, c2)],
        out_specs=pl.BlockSpec((tm, D), row),
        out_shape=jax.ShapeDtypeStruct((T, D), jnp.float32),
        compiler_params=_cparams(("arbitrary",)),
        name="shared_expert",
    )(x1, w_sg, w_su, w_sd)


def _combine2_kernel(wk_ref, x1_ref, g_ref_rows, wsg_ref, wsu_ref, wsd_ref, g_ref, b_ref, o_ref):
    x1 = x1_ref[...]
    xb = x1.astype(MXU_DTYPE)
    a = (_silu(_dot(xb, wsg_ref[...])) * _dot(xb, wsu_ref[...])).astype(MXU_DTYPE)
    shared = _dot(a, wsd_ref[...])
    wk = wk_ref[...].T
    groups = [wk[:, 0:1] * v for v in _unpack_rows_f32(g_ref_rows[0])]
    for k in range(1, TOP_K):
        groups = [g + wk[:, k:k + 1] * v for g, v in zip(groups, _unpack_rows_f32(g_ref_rows[k]))]
    routed = jnp.concatenate(groups, axis=1)
    o_ref[...] = _layer_norm(ALPHA * x1 + (routed + shared), g_ref[...], b_ref[...])


def _combine2_kernel_into(wk_ref, x1_ref, g_ref_rows, wsg_ref, wsu_ref, wsd_ref, g_ref, b_ref, prev_ref, o_ref):
    del prev_ref
    _combine2_kernel(wk_ref, x1_ref, g_ref_rows, wsg_ref, wsu_ref, wsd_ref, g_ref, b_ref, o_ref)


def _combine2(wk_t, x1, gathered, w_sg, w_su, w_sd, ln_g, ln_b, tc, chunk, prev):
    T, D = x1.shape
    _, t_chunk, W = gathered.shape
    base = chunk * (t_chunk // tc)
    row = lambda i: (base + i, 0)
    c2 = lambda i: (0, 0)
    in_specs = [
        pl.BlockSpec((TOP_K, tc), lambda i: (0, base + i)),
        pl.BlockSpec((tc, D), row),
        pl.BlockSpec((TOP_K, tc, W), lambda i: (0, i, 0)),
        pl.BlockSpec(w_sg.shape, c2),
        pl.BlockSpec(w_su.shape, c2),
        pl.BlockSpec(w_sd.shape, c2),
        pl.BlockSpec((1, D), c2),
        pl.BlockSpec((1, D), c2),
    ]
    args = [wk_t, x1, gathered, w_sg, w_su, w_sd, ln_g, ln_b]
    if prev is None:
        body, aliases = _combine2_kernel, {}
    else:
        body, aliases = _combine2_kernel_into, {len(args): 0}
        in_specs.append(pl.BlockSpec(memory_space=pl.ANY))
        args.append(prev)
    return pl.pallas_call(
        body,
        grid=(t_chunk // tc,),
        in_specs=in_specs,
        out_specs=pl.BlockSpec((tc, D), row),
        out_shape=jax.ShapeDtypeStruct((T, D), jnp.float32),
        input_output_aliases=aliases,
        compiler_params=_cparams(("arbitrary",)),
        name="combine",
    )(*args)


def _combine_kernel(dest_ref, wk_ref, x1_ref, ys_hbm, wsg_ref, wsu_ref, wsd_ref, g_ref, b_ref,
                    o_ref, buf_ref, sem, *, tc):
    def issue(r, c):
        for k in range(TOP_K):
            _row_copy(ys_hbm, dest_ref[k, r], buf_ref.at[k], r, sem).start()
        return c

    def drain(r, c):
        for k in range(TOP_K):
            _row_copy(ys_hbm, dest_ref[k, r], buf_ref.at[k], r, sem).wait()
        return c

    lax.fori_loop(0, tc, issue, 0)
    x1 = x1_ref[...]
    xb = x1.astype(MXU_DTYPE)
    a = (_silu(_dot(xb, wsg_ref[...])) * _dot(xb, wsu_ref[...])).astype(MXU_DTYPE)
    shared = _dot(a, wsd_ref[...])
    lax.fori_loop(0, tc, drain, 0)
    wk = wk_ref[...]
    groups = [wk[:, 0:1] * v for v in _unpack_rows_f32(buf_ref[0])]
    for k in range(1, TOP_K):
        groups = [g + wk[:, k:k + 1] * v for g, v in zip(groups, _unpack_rows_f32(buf_ref[k]))]
    routed = jnp.concatenate(groups, axis=1)
    o_ref[...] = _layer_norm(ALPHA * x1 + (routed + shared), g_ref[...], b_ref[...])


def _combine(dest_t, wk, x1, ys, w_sg, w_su, w_sd, ln_g, ln_b, tc):
    T, D = x1.shape
    row = lambda i: (i, 0)
    c2 = lambda i: (0, 0)
    return pl.pallas_call(
        functools.partial(_combine_kernel, tc=tc),
        grid=(T // tc,),
        in_specs=[
            pl.BlockSpec((TOP_K, tc), lambda i: (0, i), memory_space=pltpu.SMEM),
            pl.BlockSpec((tc, TOP_K), row),
            pl.BlockSpec((tc, D), row),
            pl.BlockSpec(memory_space=pl.ANY),
            pl.BlockSpec(w_sg.shape, c2),
            pl.BlockSpec(w_su.shape, c2),
            pl.BlockSpec(w_sd.shape, c2),
            pl.BlockSpec((1, D), c2),
            pl.BlockSpec((1, D), c2),
        ],
        out_specs=pl.BlockSpec((tc, D), row),
        out_shape=jax.ShapeDtypeStruct((T, D), jnp.float32),
        scratch_shapes=[pltpu.VMEM((TOP_K, tc, ys.shape[1]), ys.dtype), pltpu.SemaphoreType.DMA],
        compiler_params=_cparams(("arbitrary",)),
        name="combine",
    )(dest_t, wk, x1, ys, w_sg, w_su, w_sd, ln_g, ln_b)


def _split_w_in(w_in):
    bf = MXU_DTYPE
    o_kv = Q_RANK
    o_ki = o_kv + KV_RANK
    o_iw = o_ki + IDX_DIM
    o_rest = o_iw + N_IDX_HEADS
    w_main = jnp.concatenate([w_in[:, :o_ki], w_in[:, o_rest:]], axis=1).astype(bf)
    w_small = jnp.pad(w_in[:, o_ki:o_rest], ((0, 0), (0, LANES - IDX_DIM - N_IDX_HEADS))).astype(bf)
    return w_main, w_small


def _stages(x, mem, w_in, q_norm_g, kv_norm_g, w_uq, w_uk, w_uv, w_qidx, rel_bias, conv_w, w_mem_k, w_mem_v, w_out, ln1_g, ln1_b, w_router, router_bias, w_e_gate, w_e_up, w_e_down, w_s_gate, w_s_up, w_s_down, ln2_g, ln2_b, upto=None):
    B, S, D = x.shape
    T = B * S
    bf = MXU_DTYPE
    l = 0
    res = {}
    x2 = x.reshape(T, D)
    w_main, w_small = _split_w_in(w_in[l])
    cq, ckv, ckvt, kidx, iwt, yb, yc = _proj(
        x2, mem, w_main, w_small, q_norm_g[l].reshape(1, -1), kv_norm_g[l].reshape(1, -1), conv_w[l],
        w_mem_k[l].astype(bf), w_mem_v[l].astype(bf), B, S, tm=min(512, S))
    res.update(c_q=cq, c_kv=ckv, k_idx=kidx, y_b=yb, y_c=yc,
               idx_w=jnp.swapaxes(iwt, 1, 2) / (N_IDX_HEADS ** -0.5 * IDX_DIM ** -0.5))
    if upto == "proj":
        return res
    bias_t = _bias_tiles(rel_bias)
    ya = _dsa(cq, iwt, kidx, ckv, ckvt,
              w_qidx[l].reshape(Q_RANK, -1).astype(bf), w_uq[l].reshape(Q_RANK, -1).astype(bf),
              jnp.transpose(w_uk[l], (1, 0, 2)).astype(bf), jnp.transpose(w_uv[l], (1, 2, 0)).astype(bf),
              bias_t, B, S)
    res.update(y_a=ya)
    if upto == "dsa":
        return res

    x1, x1p, sel_t, w_t, pos_t, cnt = _mix_router(
        x2, ya, yb, yc, w_out[l].astype(bf), ln1_g[l].reshape(1, -1), ln1_b[l].reshape(1, -1),
        w_router[l].T, router_bias[l].reshape(-1, 1), tm=min(512, T))
    res.update(x1=x1)

    counts = cnt[:, 0].astype(jnp.int32)
    padded = (counts + ROW_BLOCK - 1) // ROW_BLOCK * ROW_BLOCK
    pad_end = jnp.cumsum(padded)
    pad_start = pad_end - padded
    n_blocks = -(-(T * TOP_K) // ROW_BLOCK) + N_EXPERTS
    n_rows = n_blocks * ROW_BLOCK
    block_start = jnp.arange(n_blocks, dtype=jnp.int32) * ROW_BLOCK
    block_e = jnp.minimum(jnp.sum((pad_end[None, :] <= block_start[:, None]).astype(jnp.int32), axis=1),
                          N_EXPERTS - 1)
    n_used = (pad_end[-1:] // ROW_BLOCK).astype(jnp.int32)

    dest_t, wk_t = _compact(sel_t, w_t, pos_t, pad_start.astype(jnp.float32).reshape(-1, 1), tm=min(512, T))
    block_valid = jnp.clip((pad_start + counts)[block_e] - block_start, 0, ROW_BLOCK).astype(jnp.int32)
    bt = SC_SCATTER_ROWS
    idx3 = jnp.transpose(dest_t.reshape(TOP_K, T // bt, bt), (1, 0, 2))
    xs = _sc_scatter_rows(x1p, idx3, n_rows)
    ys = _experts(xs, block_e, block_valid, n_used, w_e_gate[l], w_e_up[l], w_e_down[l])
    n_chunks = COMBINE_CHUNKS if T % (COMBINE_CHUNKS * 256) == 0 else 1
    t_chunk = T // n_chunks
    out = None
    for c in range(n_chunks):
        idx_c = dest_t[:, c * t_chunk:(c + 1) * t_chunk].reshape(-1)
        gathered = _sc_gather_rows(ys, idx_c).reshape(TOP_K, t_chunk, -1)
        out = _combine2(wk_t, x1, gathered, w_s_gate[l].astype(bf), w_s_up[l].astype(bf), w_s_down[l].astype(bf),
                        ln2_g[l].reshape(1, -1), ln2_b[l].reshape(1, -1), tc=min(256, t_chunk), chunk=c, prev=out)
    res.update(out=out.reshape(B, S, D))
    return res


def kernel(x, mem, w_in, q_norm_g, kv_norm_g, w_uq, w_uk, w_uv, w_qidx, rel_bias, conv_w, w_mem_k, w_mem_v, w_out, ln1_g, ln1_b, w_router, router_bias, w_e_gate, w_e_up, w_e_down, w_s_gate, w_s_up, w_s_down, ln2_g, ln2_b):
    return _stages(x, mem, w_in, q_norm_g, kv_norm_g, w_uq, w_uk, w_uv, w_qidx, rel_bias, conv_w, w_mem_k, w_mem_v, w_out, ln1_g, ln1_b, w_router, router_bias, w_e_gate, w_e_up, w_e_down, w_s_gate, w_s_up, w_s_down, ln2_g, ln2_b)["out"]
```

```python
import functools
import math

import jax
import jax.numpy as jnp
from jax import lax
from jax.experimental import pallas as pl
from jax.experimental.pallas import tpu as pltpu
from jax.experimental.pallas import tpu_sc as plsc

N_HEADS_A = 8
HEAD_DIM = 64
Q_RANK = 256
KV_RANK = 128
N_IDX_HEADS = 8
IDX_DIM = 64
TOPK_MAX = 256
REL_BUCKETS = 32
REL_MAX_DIST = 128
CONV_CH = 256
CONV_WIDTH = 3
N_MEM_HEADS = 4
MIX_A = N_HEADS_A * HEAD_DIM
MIX_C = N_MEM_HEADS * HEAD_DIM
N_EXPERTS = 64
N_GROUPS = 8
GROUP_SIZE = N_EXPERTS // N_GROUPS
TOPK_GROUPS = 4
TOP_K = 8
D_EXPERT = 256
ROUTED_SCALE = 2.5
DEPTH = 1
ALPHA = (2.0 * DEPTH) ** 0.25
LN_EPS = 1e-5
RMS_EPS = 1e-6
LOG2_E = math.log2(math.e)

LANES = 128
SUBLANES = 8
QB = 128
F32_LOWEST = -3.4028234663852886e38
VMEM_LIMIT = 56 * 1024 * 1024
MXU_DTYPE = jnp.bfloat16
ROW_BLOCK = 1024

_NT = (((1,), (1,)), ((), ()))


def _dot(a, b):
    return jnp.dot(a, b, preferred_element_type=jnp.float32)


def _dot_nt(a, b):
    return lax.dot_general(a, b, _NT, preferred_element_type=jnp.float32)


def _cparams(sem):
    return pltpu.CompilerParams(dimension_semantics=sem, vmem_limit_bytes=VMEM_LIMIT)


def _bias_kernel(rb_ref, o_ref):
    s = lax.broadcasted_iota(jnp.int32, (QB, QB), 0)
    t = lax.broadcasted_iota(jnp.int32, (QB, QB), 1)
    max_exact = REL_BUCKETS // 2
    for tile in range(3):
        n = jnp.maximum(t - s + (2 - tile) * QB, 0)
        nf = jnp.maximum(n.astype(jnp.float32), 1.0)
        large = max_exact + (jnp.log(nf / max_exact) / math.log(REL_MAX_DIST / max_exact)
                             * (REL_BUCKETS - max_exact)).astype(jnp.int32)
        large = jnp.minimum(large, REL_BUCKETS - 1)
        bucket = jnp.where(n < max_exact, n, large)
        for h in range(N_HEADS_A):
            acc = jnp.zeros((QB, QB), jnp.float32)
            for b in range(REL_BUCKETS):
                acc = jnp.where(bucket == b, rb_ref[b, h], acc)
            o_ref[tile, h] = acc * LOG2_E


def _bias_tiles(rel_bias):
    return pl.pallas_call(
        _bias_kernel,
        in_specs=[pl.BlockSpec(memory_space=pltpu.SMEM)],
        out_specs=pl.BlockSpec(memory_space=pltpu.VMEM),
        out_shape=jax.ShapeDtypeStruct((3, N_HEADS_A, QB, QB), jnp.float32),
        name="bias_tiles",
    )(rel_bias)


def _proj_kernel(x_ref, mem_ref, wm_ref, ws_ref, qg_ref, kvg_ref, cw_ref, wmk_ref, wmv_ref,
                 cq_ref, ckv_ref, ckvt_ref, kidx_ref, iwt_ref, yb_ref, yc_ref,
                 carry_ref, mk_ref, mv_ref, *, tm):
    si = pl.program_id(1)

    @pl.when(si == 0)
    def _():
        carry_ref[...] = jnp.zeros_like(carry_ref)
        mb = mem_ref[0].astype(MXU_DTYPE)
        mk_ref[...] = _dot(mb, wmk_ref[...]).astype(MXU_DTYPE)
        mv_ref[...] = _dot(mb, wmv_ref[...]).astype(MXU_DTYPE)

    xb = x_ref[...].astype(MXU_DTYPE)
    p = _dot(xb, wm_ref[...])
    small = _dot(xb, ws_ref[...])

    o = 0
    cq = p[:, o:o + Q_RANK]; o += Q_RANK
    ckv = p[:, o:o + KV_RANK]; o += KV_RANK
    g_b = p[:, o:o + CONV_CH]; o += CONV_CH
    g_c = p[:, o:o + CONV_CH]; o += CONV_CH
    h_c = p[:, o:o + CONV_CH]; o += CONV_CH
    q_mem = p[:, o:o + MIX_C]

    cq = cq * lax.rsqrt(jnp.mean(cq * cq, axis=-1, keepdims=True) + RMS_EPS) * qg_ref[...]
    ckv = ckv * lax.rsqrt(jnp.mean(ckv * ckv, axis=-1, keepdims=True) + RMS_EPS) * kvg_ref[...]
    cq_ref[...] = cq.astype(MXU_DTYPE)
    ckv_b = ckv.astype(MXU_DTYPE)
    ckv_ref[...] = ckv_b
    ckvt_ref[0] = ckv.T.astype(MXU_DTYPE)

    kidx_ref[...] = small[:, :IDX_DIM].astype(MXU_DTYPE)
    small_t = small.T
    iwt_ref[0] = small_t[IDX_DIM:IDX_DIM + N_IDX_HEADS, :] * (N_IDX_HEADS ** -0.5 * IDX_DIM ** -0.5)

    u = g_c * h_c
    rows = lax.broadcasted_iota(jnp.int32, (tm, 1), 0)
    c6 = carry_ref[SUBLANES - 2:SUBLANES - 1, :]
    c7 = carry_ref[SUBLANES - 1:SUBLANES, :]
    u1 = jnp.where(rows == 0, c7, pltpu.roll(u, 1, 0))
    u2 = jnp.where(rows == 0, c6, jnp.where(rows == 1, c7, pltpu.roll(u, 2, 0)))
    y = cw_ref[0:1, :] * u2
    y = y + cw_ref[1:2, :] * u1
    y = y + cw_ref[2:3, :] * u
    yb_ref[...] = (g_b * y).astype(MXU_DTYPE)
    carry_ref[...] = u[tm - SUBLANES:, :]

    qm = q_mem.astype(MXU_DTYPE)
    outs = []
    for h in range(N_MEM_HEADS):
        sl = slice(h * HEAD_DIM, (h + 1) * HEAD_DIM)
        lg = _dot_nt(qm[:, sl], mk_ref[:, sl]) * (HEAD_DIM ** -0.5)
        lg = lg - jnp.max(lg, axis=-1, keepdims=True)
        e = jnp.exp(lg)
        pr = e / jnp.sum(e, axis=-1, keepdims=True)
        outs.append(_dot(pr.astype(MXU_DTYPE), mv_ref[:, sl]))
    yc_ref[...] = jnp.concatenate(outs, axis=-1).astype(MXU_DTYPE)


def _proj(x2, mem, w_main, w_small, q_g, kv_g, conv_w, w_mk, w_mv, B, S, tm):
    T, D = x2.shape
    n_mem = mem.shape[1]
    ns = S // tm
    row = lambda b, s: (b * ns + s, 0)
    const2 = lambda b, s: (0, 0)
    bf = MXU_DTYPE
    return pl.pallas_call(
        functools.partial(_proj_kernel, tm=tm),
        grid=(B, ns),
        in_specs=[
            pl.BlockSpec((tm, D), row),
            pl.BlockSpec((1, n_mem, D), lambda b, s: (b, 0, 0)),
            pl.BlockSpec(w_main.shape, const2),
            pl.BlockSpec(w_small.shape, const2),
            pl.BlockSpec(q_g.shape, const2),
            pl.BlockSpec(kv_g.shape, const2),
            pl.BlockSpec(conv_w.shape, const2),
            pl.BlockSpec(w_mk.shape, const2),
            pl.BlockSpec(w_mv.shape, const2),
        ],
        out_specs=[
            pl.BlockSpec((tm, Q_RANK), row),
            pl.BlockSpec((tm, KV_RANK), row),
            pl.BlockSpec((1, KV_RANK, tm), lambda b, s: (b, 0, s)),
            pl.BlockSpec((tm, IDX_DIM), row),
            pl.BlockSpec((1, N_IDX_HEADS, tm), lambda b, s: (b, 0, s)),
            pl.BlockSpec((tm, CONV_CH), row),
            pl.BlockSpec((tm, MIX_C), row),
        ],
        out_shape=[
            jax.ShapeDtypeStruct((T, Q_RANK), bf),
            jax.ShapeDtypeStruct((T, KV_RANK), bf),
            jax.ShapeDtypeStruct((B, KV_RANK, S), bf),
            jax.ShapeDtypeStruct((T, IDX_DIM), bf),
            jax.ShapeDtypeStruct((B, N_IDX_HEADS, S), jnp.float32),
            jax.ShapeDtypeStruct((T, CONV_CH), bf),
            jax.ShapeDtypeStruct((T, MIX_C), bf),
        ],
        scratch_shapes=[
            pltpu.VMEM((SUBLANES, CONV_CH), jnp.float32),
            pltpu.VMEM((n_mem, MIX_C), bf),
            pltpu.VMEM((n_mem, MIX_C), bf),
        ],
        compiler_params=_cparams(("arbitrary", "arbitrary")),
        name="proj",
    )(x2, mem, w_main, w_small, q_g, kv_g, conv_w, w_mk, w_mv)


def _key_to_f32(key):
    bits = jnp.where(key < 0, key ^ jnp.int32(0x7FFFFFFF), key)
    return pltpu.bitcast(bits, jnp.float32)


def _colsum8(v):
    return jnp.sum(v.reshape(QB // SUBLANES, SUBLANES, QB), axis=0)


def _colmax8(v):
    return jnp.max(v.reshape(QB // SUBLANES, SUBLANES, QB), axis=0)


UNROLL_WIDTHS = (8, 4, 2, 1)


def _dsa_kernel(cq_ref, iwt_ref, kidx_ref, ckv_ref, ckvt_ref, wqi_ref, wuq_ref, wuk_ref, wuvt_ref,
                bias_ref, o_ref, wfold_ref, qidx_ref, qlat_ref, score_ref, mask_ref, logit_ref, acc_ref,
                *, k_sel, idx_bits):
    i = pl.program_id(1)
    f32 = jnp.float32
    bf = MXU_DTYPE
    n_blocks = i + 1
    n_blocks = n_blocks + jnp.where((n_blocks % 4 == 3) & (n_blocks < pl.num_programs(1)), 1, 0)
    s_loc = lax.broadcasted_iota(jnp.int32, (QB, QB), 0)
    t_glob = i * QB + lax.broadcasted_iota(jnp.int32, (QB, QB), 1)

    def blk(jb):
        return pl.multiple_of(jb * QB, QB)

    def block_loop(fn, init):
        c, start = init, 0
        for width in UNROLL_WIDTHS:
            n = (n_blocks - start) // width
            c = lax.fori_loop(0, n, lambda it, c, w=width, s=start: fn(s + it * w, w, c), c)
            start = start + n * width
        return c

    @pl.when(i == 0)
    def _():
        for h in range(N_HEADS_A):
            wfold_ref[:, h * KV_RANK:(h + 1) * KV_RANK] = (
                _dot_nt(wuq_ref[:, h * HEAD_DIM:(h + 1) * HEAD_DIM], wuk_ref[h])
                * (HEAD_DIM ** -0.5 * LOG2_E)).astype(bf)

    cq = cq_ref[...]
    q_idx = _dot(cq, wqi_ref[...]).astype(bf)
    q_lat = _dot(cq, wfold_ref[...]).astype(bf)
    for h in range(N_HEADS_A):
        qidx_ref[h * QB:(h + 1) * QB, :] = q_idx[:, h * IDX_DIM:(h + 1) * IDX_DIM]
        qlat_ref[h * QB:(h + 1) * QB, :] = q_lat[:, h * KV_RANK:(h + 1) * KV_RANK]
    iw = iwt_ref[0]

    def score_body(jb0, nb, c):
        d_blk = _dot_nt(kidx_ref[pl.ds(blk(jb0), nb * QB), :], qidx_ref[...])
        for sb in range(nb):
            off = blk(jb0 + sb)
            d_all = d_blk[sb * QB:(sb + 1) * QB, :]
            acc = jnp.maximum(d_all[:, 0:QB], 0.0) * iw[0:1, :]
            for h in range(1, N_IDX_HEADS):
                acc = acc + jnp.maximum(d_all[:, h * QB:(h + 1) * QB], 0.0) * iw[h:h + 1, :]
            score_ref[pl.ds(off, QB), :] = jnp.where(s_loc + off <= t_glob, acc + 0.0, F32_LOWEST)
        return c

    block_loop(score_body, 0)

    def count_where(pred):
        def body(jb0, nb, acc):
            for sb in range(nb):
                off = blk(jb0 + sb)
                acc = acc + _colsum8(jnp.where(pred(score_ref[pl.ds(off, QB), :], off), 1.0, 0.0))
            return acc
        acc = block_loop(body, jnp.zeros((SUBLANES, QB), f32))
        return jnp.sum(acc, axis=0, keepdims=True)

    kf = float(k_sel)

    def search():
        c0 = count_where(lambda sc, off: sc >= 0.0)
        cand0 = jnp.where(c0 >= kf, jnp.int32(0), jnp.int32(-2 ** 31))

        def bit_body(it, cand):
            trial = cand + lax.shift_left(jnp.int32(1), 30 - it)
            tf = _key_to_f32(trial)
            cnt = count_where(lambda sc, off: sc >= tf)
            return jnp.where(cnt >= kf, trial, cand)

        cand = lax.fori_loop(0, 31, bit_body, cand0)
        thr = _key_to_f32(cand)
        n_gt = count_where(lambda sc, off: sc > thr)
        n_eq = count_where(lambda sc, off: sc == thr)
        need = kf - n_gt

        def tie_search():
            def tbody(it, xcut):
                trial = xcut + lax.shift_left(jnp.int32(1), idx_bits - 1 - it)
                cnt = count_where(lambda sc, off: (sc == thr) & (s_loc + off < trial))
                return jnp.where(cnt < need, trial, xcut)
            return lax.fori_loop(0, idx_bits, tbody, jnp.zeros((1, QB), jnp.int32))

        any_extra = jnp.max(n_eq - need) > 0.0
        xcut = lax.cond(any_extra, tie_search, lambda: jnp.full((1, QB), 2 ** idx_bits - 1, jnp.int32))
        return thr, xcut

    def no_search():
        return jnp.full((1, QB), F32_LOWEST, f32), jnp.full((1, QB), 2 ** idx_bits - 1, jnp.int32)

    thr, xcut = lax.cond((i + 1) * QB > k_sel, search, no_search)

    def mask_body(jb0, nb, c):
        for sb in range(nb):
            off = blk(jb0 + sb)
            sc = score_ref[pl.ds(off, QB), :]
            s_glob = s_loc + off
            keep = ((sc > thr) | ((sc == thr) & (s_glob <= xcut))) & (s_glob <= t_glob)
            mask_ref[pl.ds(off, QB), :] = jnp.where(keep, 0.0, -jnp.inf)
        return c

    block_loop(mask_body, 0)

    def p1_body(jb0, nb, m8):
        m8 = list(m8)
        lg_blk = _dot_nt(ckv_ref[pl.ds(blk(jb0), nb * QB), :], qlat_ref[...])
        for sb in range(nb):
            off = blk(jb0 + sb)
            lg = lg_blk[sb * QB:(sb + 1) * QB, :]
            msk = mask_ref[pl.ds(off, QB), :]
            bsel = jnp.clip(jb0 + sb - i + 2, 0, 2)
            for h in range(N_HEADS_A):
                lgh = lg[:, h * QB:(h + 1) * QB] + bias_ref[bsel, h] + msk
                logit_ref[pl.ds(off, QB), h * QB:(h + 1) * QB] = lgh
                m8[h] = jnp.maximum(m8[h], _colmax8(lgh))
        return tuple(m8)

    m8 = block_loop(p1_body, tuple(jnp.full((SUBLANES, QB), -jnp.inf, f32) for _ in range(N_HEADS_A)))
    m_row = [jnp.max(m, axis=0, keepdims=True) for m in m8]

    acc_ref[...] = jnp.zeros_like(acc_ref)

    def p2_body(jb0, nb, l8):
        l8 = list(l8)
        off = blk(jb0)
        rows = nb * QB
        ps = []
        for h in range(N_HEADS_A):
            p = jnp.exp2(logit_ref[pl.ds(off, rows), h * QB:(h + 1) * QB] - m_row[h])
            l8[h] = l8[h] + jnp.sum(p.reshape(rows // SUBLANES, SUBLANES, QB), axis=0)
            ps.append(p.astype(bf))
        acc_ref[...] += _dot(ckvt_ref[0, :, pl.ds(off, rows)], jnp.concatenate(ps, axis=1))
        return tuple(l8)

    l8 = block_loop(p2_body, tuple(jnp.zeros((SUBLANES, QB), f32) for _ in range(N_HEADS_A)))

    outs = []
    for h in range(N_HEADS_A):
        l_row = jnp.sum(l8[h], axis=0, keepdims=True)
        o_lat_t = (acc_ref[:, h * QB:(h + 1) * QB] / l_row).astype(bf)
        outs.append(_dot(wuvt_ref[h], o_lat_t))
    o_ref[...] = jnp.concatenate(outs, axis=0).T.astype(o_ref.dtype)


def _dsa(cq, iwt, kidx, ckv, ckvt, w_qidx, w_uq, w_uk_h, w_uvt_h, bias_tiles, B, S):
    T = cq.shape[0]
    assert S % QB == 0 and QB >= REL_MAX_DIST
    nq = S // QB
    k_sel = min(TOPK_MAX, S // 4)
    idx_bits = max(1, (S - 1).bit_length())
    c2 = lambda b, i: (0, 0)
    c3 = lambda b, i: (0, 0, 0)
    return pl.pallas_call(
        functools.partial(_dsa_kernel, k_sel=k_sel, idx_bits=idx_bits),
        grid=(B, nq),
        in_specs=[
            pl.BlockSpec((QB, Q_RANK), lambda b, i: (b * nq + i, 0)),
            pl.BlockSpec((1, N_IDX_HEADS, QB), lambda b, i: (b, 0, i)),
            pl.BlockSpec((S, IDX_DIM), lambda b, i: (b, 0)),
            pl.BlockSpec((S, KV_RANK), lambda b, i: (b, 0)),
            pl.BlockSpec((1, KV_RANK, S), lambda b, i: (b, 0, 0)),
            pl.BlockSpec(w_qidx.shape, c2),
            pl.BlockSpec(w_uq.shape, c2),
            pl.BlockSpec(w_uk_h.shape, c3),
            pl.BlockSpec(w_uvt_h.shape, c3),
            pl.BlockSpec(bias_tiles.shape, lambda b, i: (0, 0, 0, 0)),
        ],
        out_specs=pl.BlockSpec((QB, MIX_A), lambda b, i: (b * nq + i, 0)),
        out_shape=jax.ShapeDtypeStruct((T, MIX_A), MXU_DTYPE),
        scratch_shapes=[
            pltpu.VMEM((Q_RANK, N_HEADS_A * KV_RANK), MXU_DTYPE),
            pltpu.VMEM((N_IDX_HEADS * QB, IDX_DIM), MXU_DTYPE),
            pltpu.VMEM((N_HEADS_A * QB, KV_RANK), MXU_DTYPE),
            pltpu.VMEM((S, QB), jnp.float32),
            pltpu.VMEM((S, QB), jnp.float32),
            pltpu.VMEM((S, N_HEADS_A * QB), jnp.float32),
            pltpu.VMEM((KV_RANK, N_HEADS_A * QB), jnp.float32),
        ],
        compiler_params=_cparams(("arbitrary", "arbitrary")),
        name="dsa",
    )(cq, iwt, kidx, ckv, ckvt, w_qidx, w_uq, w_uk_h, w_uvt_h, bias_tiles)


def _layer_norm(xf, g, b):
    mu = jnp.mean(xf, axis=-1, keepdims=True)
    xc = xf - mu
    var = jnp.mean(xc * xc, axis=-1, keepdims=True)
    return xc * lax.rsqrt(var + LN_EPS) * g + b


def _rank_rows(v, n):
    ri = lax.broadcasted_iota(jnp.int32, v.shape, 0)
    rank = jnp.zeros(v.shape, jnp.float32)
    for r2 in range(n):
        row = v[r2:r2 + 1, :]
        beats = (row > v) | ((row == v) & (ri > r2))
        rank = rank + jnp.where(beats, 1.0, 0.0)
    return rank


def _top_rows(v, k):
    n = v.shape[0]
    ri = lax.broadcasted_iota(jnp.int32, v.shape, 0)
    sel = jnp.zeros(v.shape, jnp.float32)
    for _ in range(k):
        m = jnp.max(v, axis=0, keepdims=True)
        first = jnp.min(jnp.where(v == m, ri, n), axis=0, keepdims=True)
        pick = ri == first
        sel = jnp.where(pick, 1.0, sel)
        v = jnp.where(pick, -jnp.inf, v)
    return sel > 0.5


def _pack_factor():
    return 4 // jnp.dtype(MXU_DTYPE).itemsize


def _pack_rows(x):
    if _pack_factor() == 1:
        return pltpu.bitcast(x, jnp.int32)
    half = x.shape[1] // 2
    b = pltpu.bitcast(x.astype(MXU_DTYPE).astype(jnp.float32), jnp.int32)
    return b[:, half:] | lax.shift_right_logical(b[:, :half], jnp.int32(16))


_HIGH_HALF = -(1 << 16)


def _unpack_rows_f32(p):
    if _pack_factor() == 1:
        return [pltpu.bitcast(p, jnp.float32)]
    lo = pltpu.bitcast(lax.shift_left(p, jnp.int32(16)), jnp.float32)
    hi = pltpu.bitcast(p & jnp.int32(_HIGH_HALF), jnp.float32)
    return [lo, hi]


def _unpack_rows(p):
    return [v.astype(MXU_DTYPE) for v in _unpack_rows_f32(p)]


def _mix_router_kernel(x_ref, ya_ref, yb_ref, yc_ref, wo_ref, g_ref, b_ref, wrt_ref, rb_ref, exp_ref,
                       x1_ref, x1p_ref, sel_ref, w_ref, pos_ref, cnt_ref, base_ref, *, tm):
    step = pl.program_id(0)
    f32 = jnp.float32

    @pl.when(step == 0)
    def _():
        base_ref[...] = jnp.zeros_like(base_ref)

    mix = _dot(ya_ref[...], wo_ref[0:MIX_A, :])
    mix = mix + _dot(yb_ref[...], wo_ref[MIX_A:MIX_A + CONV_CH, :])
    mix = mix + _dot(yc_ref[...], wo_ref[MIX_A + CONV_CH:, :])
    x1 = _layer_norm(ALPHA * x_ref[...] + mix, g_ref[...], b_ref[...])
    x1_ref[...] = x1
    x1p_ref[...] = _pack_rows(x1)

    lg = lax.dot_general(wrt_ref[...], x1, _NT, precision=lax.Precision.HIGHEST, preferred_element_type=f32)
    s = 1.0 / (1.0 + jnp.exp(-lg))
    sc = s + rb_ref[...]

    g3 = sc.reshape(N_GROUPS, GROUP_SIZE, tm)
    m1 = jnp.max(g3, axis=1, keepdims=True)
    is_m1 = g3 == m1
    n_m1 = jnp.sum(jnp.where(is_m1, 1.0, 0.0), axis=1, keepdims=True)
    m2 = jnp.max(jnp.where(is_m1, -jnp.inf, g3), axis=1, keepdims=True)
    gscore = (m1 + jnp.where(n_m1 > 1.0, m1, m2)).reshape(N_GROUPS, tm)
    gsel = jnp.where(_rank_rows(gscore, N_GROUPS) < float(TOPK_GROUPS), 1.0, 0.0)
    emask = _dot(exp_ref[...], gsel.astype(MXU_DTYPE)) > 0.5
    masked = jnp.where(emask, sc, -jnp.inf)
    sel = _top_rows(masked, TOP_K) & emask
    self_ = jnp.where(sel, 1.0, 0.0)
    top_s = jnp.where(sel, s, 0.0)
    w = top_s / jnp.sum(top_s, axis=0, keepdims=True) * ROUTED_SCALE

    t_r = lax.broadcasted_iota(jnp.int32, (tm, tm), 0)
    t_c = lax.broadcasted_iota(jnp.int32, (tm, tm), 1)
    upper = jnp.where(t_r < t_c, 1.0, 0.0).astype(MXU_DTYPE)
    pref = _dot(self_.astype(MXU_DTYPE), upper)
    base = base_ref[...]
    sel_ref[...] = self_
    w_ref[...] = w
    pos_ref[...] = base + pref
    base = base + jnp.sum(self_, axis=1, keepdims=True)
    base_ref[...] = base
    cnt_ref[...] = jnp.broadcast_to(base, cnt_ref.shape)


def _mix_router(x2, ya, yb, yc, w_out, ln_g, ln_b, w_router_t, router_bias, tm):
    T, D = x2.shape
    E = N_EXPERTS
    expand = (jnp.arange(E)[:, None] // GROUP_SIZE == jnp.arange(N_GROUPS)[None, :]).astype(MXU_DTYPE)
    row = lambda i: (i, 0)
    col = lambda i: (0, i)
    c2 = lambda i: (0, 0)
    f32 = jnp.float32
    return pl.pallas_call(
        functools.partial(_mix_router_kernel, tm=tm),
        grid=(T // tm,),
        in_specs=[
            pl.BlockSpec((tm, D), row),
            pl.BlockSpec((tm, MIX_A), row),
            pl.BlockSpec((tm, CONV_CH), row),
            pl.BlockSpec((tm, MIX_C), row),
            pl.BlockSpec(w_out.shape, c2),
            pl.BlockSpec((1, D), c2),
            pl.BlockSpec((1, D), c2),
            pl.BlockSpec((E, D), c2),
            pl.BlockSpec((E, 1), c2),
            pl.BlockSpec((E, N_GROUPS), c2),
        ],
        out_specs=[
            pl.BlockSpec((tm, D), row),
            pl.BlockSpec((tm, D // _pack_factor()), row),
            pl.BlockSpec((E, tm), col),
            pl.BlockSpec((E, tm), col),
            pl.BlockSpec((E, tm), col),
            pl.BlockSpec((E, LANES), c2),
        ],
        out_shape=[
            jax.ShapeDtypeStruct((T, D), f32),
            jax.ShapeDtypeStruct((T, D // _pack_factor()), jnp.int32),
            jax.ShapeDtypeStruct((E, T), f32),
            jax.ShapeDtypeStruct((E, T), f32),
            jax.ShapeDtypeStruct((E, T), f32),
            jax.ShapeDtypeStruct((E, LANES), f32),
        ],
        scratch_shapes=[pltpu.VMEM((E, 1), f32)],
        compiler_params=_cparams(("arbitrary",)),
        name="mix_router",
    )(x2, ya, yb, yc, w_out, ln_g, ln_b, w_router_t, router_bias, expand)


def _compact_kernel(sel_ref, w_ref, pos_ref, pstart_ref, low_ref, dest_ref, wk_ref):
    sel = sel_ref[...]
    on = sel > 0.5
    rank = _dot(low_ref[...], sel.astype(MXU_DTYPE))
    row = pstart_ref[...] + pos_ref[...]
    w = w_ref[...]
    dests, ws = [], []
    for k in range(TOP_K):
        m = on & (rank == float(k))
        dests.append(jnp.sum(jnp.where(m, row, 0.0), axis=0, keepdims=True))
        ws.append(jnp.sum(jnp.where(m, w, 0.0), axis=0, keepdims=True))
    dest_ref[...] = jnp.concatenate(dests, axis=0).astype(jnp.int32)
    wk_ref[...] = jnp.concatenate(ws, axis=0)


def _compact(sel_t, w_t, pos_t, pad_start, tm):
    E, T = sel_t.shape
    lower = (jnp.arange(E)[None, :] < jnp.arange(E)[:, None]).astype(MXU_DTYPE)
    col = lambda i: (0, i)
    c2 = lambda i: (0, 0)
    return pl.pallas_call(
        _compact_kernel,
        grid=(T // tm,),
        in_specs=[pl.BlockSpec((E, tm), col), pl.BlockSpec((E, tm), col), pl.BlockSpec((E, tm), col),
                  pl.BlockSpec((E, 1), c2), pl.BlockSpec((E, E), c2)],
        out_specs=[pl.BlockSpec((TOP_K, tm), col), pl.BlockSpec((TOP_K, tm), col)],
        out_shape=[jax.ShapeDtypeStruct((TOP_K, T), jnp.int32), jax.ShapeDtypeStruct((TOP_K, T), jnp.float32)],
        compiler_params=_cparams(("arbitrary",)),
        name="route_compact",
    )(sel_t, w_t, pos_t, pad_start, lower)


def _silu(g):
    return g / (1.0 + jnp.exp(-g))


def _expert_kernel(be_ref, nv_ref, nu_ref, xs_ref, wg_ref, wu_ref, wd_ref, ys_ref, wgb_ref, wub_ref, wdb_ref):
    i = pl.program_id(0)

    @pl.when((i == 0) | (be_ref[i] != be_ref[jnp.maximum(i - 1, 0)]))
    def _():
        wgb_ref[...] = wg_ref[0].astype(MXU_DTYPE)
        wub_ref[...] = wu_ref[0].astype(MXU_DTYPE)
        wdb_ref[...] = wd_ref[0].astype(MXU_DTYPE)

    @pl.when(i < nu_ref[0])
    def _():
        live = lax.broadcasted_iota(jnp.int32, (ROW_BLOCK, 1), 0) < nv_ref[i]
        parts = [jnp.where(live, v, jnp.zeros_like(v)) for v in _unpack_rows(xs_ref[...])]
        dk = wgb_ref.shape[0] // len(parts)

        def proj(w_ref):
            acc = _dot(parts[0], w_ref[0:dk, :])
            for n in range(1, len(parts)):
                acc = acc + _dot(parts[n], w_ref[n * dk:(n + 1) * dk, :])
            return acc

        a = (_silu(proj(wgb_ref)) * proj(wub_ref)).astype(MXU_DTYPE)
        ys_ref[...] = _pack_rows(_dot(a, wdb_ref[...]))


def _experts(xs, block_e, block_valid, n_used, w_gate, w_up, w_down):
    n_rows, W = xs.shape
    D = w_gate.shape[1]
    n_blocks = n_rows // ROW_BLOCK
    blk = lambda i, be, nv, nu: (jnp.minimum(i, nu[0] - 1), 0)
    wsel = lambda i, be, nv, nu: (be[i], 0, 0)
    return pl.pallas_call(
        _expert_kernel,
        grid_spec=pltpu.PrefetchScalarGridSpec(
            num_scalar_prefetch=3,
            grid=(n_blocks,),
            in_specs=[
                pl.BlockSpec((ROW_BLOCK, W), blk),
                pl.BlockSpec((1, D, D_EXPERT), wsel),
                pl.BlockSpec((1, D, D_EXPERT), wsel),
                pl.BlockSpec((1, D_EXPERT, D), wsel),
            ],
            out_specs=pl.BlockSpec((ROW_BLOCK, W), blk),
            scratch_shapes=[pltpu.VMEM((D, D_EXPERT), MXU_DTYPE), pltpu.VMEM((D, D_EXPERT), MXU_DTYPE),
                            pltpu.VMEM((D_EXPERT, D), MXU_DTYPE)],
        ),
        out_shape=jax.ShapeDtypeStruct((n_rows, W), xs.dtype),
        compiler_params=_cparams(("arbitrary",)),
        name="experts",
    )(block_e, block_valid, n_used, xs, w_gate, w_up, w_down)


SC_CORES = 2
SC_SUBCORES = 16
SC_GATHER_ROWS = 64
COMBINE_CHUNKS = 4


def _sc_gather_rows(table, idx):
    n = idx.shape[0]
    w = table.shape[1]
    n_workers = SC_CORES * SC_SUBCORES
    per_worker = n // n_workers
    assert n % n_workers == 0 and per_worker % SC_GATHER_ROWS == 0
    mesh = plsc.VectorSubcoreMesh(core_axis_name="c", subcore_axis_name="s")

    @functools.partial(
        pl.kernel, mesh=mesh,
        out_type=jax.ShapeDtypeStruct((n, w), table.dtype),
        scratch_types=[
            pltpu.VMEM((2, SC_GATHER_ROWS), jnp.int32),
            pltpu.VMEM((2, SC_GATHER_ROWS, w), table.dtype),
            pltpu.SemaphoreType.DMA((2,)),
        ],
        name="sc_gather_rows",
    )
    def gather(table_hbm, idx_hbm, out_hbm, idx_v, rows_v, sem):
        wid = lax.axis_index("s") * SC_CORES + lax.axis_index("c")
        base = wid * per_worker
        n_steps = per_worker // SC_GATHER_ROWS

        def gather_copy(slot):
            return pltpu.make_async_copy(table_hbm.at[idx_v.at[slot]], rows_v.at[slot], sem.at[slot])

        def start(step, slot):
            pltpu.sync_copy(idx_hbm.at[pl.ds(base + step * SC_GATHER_ROWS, SC_GATHER_ROWS)], idx_v.at[slot])
            gather_copy(slot).start()

        start(0, 0)

        @pl.loop(0, n_steps, step=2)
        def _(g):
            for slot in range(2):
                step = g + slot

                @pl.when(step + 1 < n_steps)
                def _():
                    start(step + 1, 1 - slot)

                gather_copy(slot).wait()
                pltpu.sync_copy(rows_v.at[slot], out_hbm.at[pl.ds(base + step * SC_GATHER_ROWS, SC_GATHER_ROWS)])

    return gather(table, idx)


SC_SCATTER_ROWS = 64


def _sc_scatter_rows(rows, idx3, n_out):
    n_src, w = rows.shape
    n_chunks, n_dst, batch = idx3.shape
    n_workers = SC_CORES * SC_SUBCORES
    assert batch == SC_SCATTER_ROWS and n_chunks * batch == n_src and n_chunks % (2 * n_workers) == 0
    per_worker = n_chunks // n_workers
    mesh = plsc.VectorSubcoreMesh(core_axis_name="c", subcore_axis_name="s")

    @functools.partial(
        pl.kernel, mesh=mesh,
        out_type=jax.ShapeDtypeStruct((n_out, w), rows.dtype),
        scratch_types=[
            pltpu.VMEM((2, n_dst, batch), jnp.int32),
            pltpu.VMEM((2, batch, w), rows.dtype),
            pltpu.SemaphoreType.DMA((2,)),
            pltpu.SemaphoreType.DMA,
        ],
        name="sc_scatter_rows",
    )
    def scatter(rows_hbm, idx_hbm, out_hbm, idx_v, rows_v, load_sem, store_sem):
        wid = lax.axis_index("s") * SC_CORES + lax.axis_index("c")

        def load_copy(step, slot):
            c = wid * per_worker + step
            return pltpu.make_async_copy(rows_hbm.at[pl.ds(c * batch, batch)], rows_v.at[slot], load_sem.at[slot])

        def load(step, slot):
            pltpu.sync_copy(idx_hbm.at[wid * per_worker + step], idx_v.at[slot])
            load_copy(step, slot).start()

        def store_copy(slot, k):
            return pltpu.make_async_copy(rows_v.at[slot], out_hbm.at[idx_v.at[slot].at[k]], store_sem)

        load(0, 0)

        @pl.loop(0, per_worker, step=2)
        def _(g):
            for slot in range(2):
                step = g + slot

                @pl.when(step + 1 < per_worker)
                def _():
                    load(step + 1, 1 - slot)

                load_copy(step, slot).wait()
                for k in range(n_dst):
                    store_copy(slot, k).start()
                for k in range(n_dst):
                    store_copy(slot, k).wait()

    return scatter(rows, idx3)


def _combine2_kernel(wk_ref, x1_ref, g_ref_rows, wsg_ref, wsu_ref, wsd_ref, g_ref, b_ref, o_ref):
    x1 = x1_ref[...]
    xb = x1.astype(MXU_DTYPE)
    a = (_silu(_dot(xb, wsg_ref[...])) * _dot(xb, wsu_ref[...])).astype(MXU_DTYPE)
    shared = _dot(a, wsd_ref[...])
    wk = wk_ref[...].T
    groups = [wk[:, 0:1] * v for v in _unpack_rows_f32(g_ref_rows[0])]
    for k in range(1, TOP_K):
        groups = [g + wk[:, k:k + 1] * v for g, v in zip(groups, _unpack_rows_f32(g_ref_rows[k]))]
    routed = jnp.concatenate(groups, axis=1)
    o_ref[...] = _layer_norm(ALPHA * x1 + (routed + shared), g_ref[...], b_ref[...])


def _combine2_kernel_into(wk_ref, x1_ref, g_ref_rows, wsg_ref, wsu_ref, wsd_ref, g_ref, b_ref, prev_ref, o_ref):
    del prev_ref
    _combine2_kernel(wk_ref, x1_ref, g_ref_rows, wsg_ref, wsu_ref, wsd_ref, g_ref, b_ref, o_ref)


def _combine2(wk_t, x1, gathered, w_sg, w_su, w_sd, ln_g, ln_b, tc, chunk, prev):
    T, D = x1.shape
    _, t_chunk, W = gathered.shape
    base = chunk * (t_chunk // tc)
    row = lambda i: (base + i, 0)
    c2 = lambda i: (0, 0)
    in_specs = [
        pl.BlockSpec((TOP_K, tc), lambda i: (0, base + i)),
        pl.BlockSpec((tc, D), row),
        pl.BlockSpec((TOP_K, tc, W), lambda i: (0, i, 0)),
        pl.BlockSpec(w_sg.shape, c2),
        pl.BlockSpec(w_su.shape, c2),
        pl.BlockSpec(w_sd.shape, c2),
        pl.BlockSpec((1, D), c2),
        pl.BlockSpec((1, D), c2),
    ]
    args = [wk_t, x1, gathered, w_sg, w_su, w_sd, ln_g, ln_b]
    if prev is None:
        body, aliases = _combine2_kernel, {}
    else:
        body, aliases = _combine2_kernel_into, {len(args): 0}
        in_specs.append(pl.BlockSpec(memory_space=pl.ANY))
        args.append(prev)
    return pl.pallas_call(
        body,
        grid=(t_chunk // tc,),
        in_specs=in_specs,
        out_specs=pl.BlockSpec((tc, D), row),
        out_shape=jax.ShapeDtypeStruct((T, D), jnp.float32),
        input_output_aliases=aliases,
        compiler_params=_cparams(("arbitrary",)),
        name="combine",
    )(*args)


def _split_w_in(w_in):
    bf = MXU_DTYPE
    o_kv = Q_RANK
    o_ki = o_kv + KV_RANK
    o_iw = o_ki + IDX_DIM
    o_rest = o_iw + N_IDX_HEADS
    w_main = jnp.concatenate([w_in[:, :o_ki], w_in[:, o_rest:]], axis=1).astype(bf)
    w_small = jnp.pad(w_in[:, o_ki:o_rest], ((0, 0), (0, LANES - IDX_DIM - N_IDX_HEADS))).astype(bf)
    return w_main, w_small


def _stages(x, mem, w_in, q_norm_g, kv_norm_g, w_uq, w_uk, w_uv, w_qidx, rel_bias, conv_w, w_mem_k, w_mem_v, w_out, ln1_g, ln1_b, w_router, router_bias, w_e_gate, w_e_up, w_e_down, w_s_gate, w_s_up, w_s_down, ln2_g, ln2_b, upto=None):
    B, S, D = x.shape
    T = B * S
    bf = MXU_DTYPE
    l = 0
    res = {}
    x2 = x.reshape(T, D)
    w_main, w_small = _split_w_in(w_in[l])
    cq, ckv, ckvt, kidx, iwt, yb, yc = _proj(
        x2, mem, w_main, w_small, q_norm_g[l].reshape(1, -1), kv_norm_g[l].reshape(1, -1), conv_w[l],
        w_mem_k[l].astype(bf), w_mem_v[l].astype(bf), B, S, tm=min(512, S))
    res.update(c_q=cq, c_kv=ckv, k_idx=kidx, y_b=yb, y_c=yc,
               idx_w=jnp.swapaxes(iwt, 1, 2) / (N_IDX_HEADS ** -0.5 * IDX_DIM ** -0.5))
    if upto == "proj":
        return res
    bias_t = _bias_tiles(rel_bias)
    ya = _dsa(cq, iwt, kidx, ckv, ckvt,
              w_qidx[l].reshape(Q_RANK, -1).astype(bf), w_uq[l].reshape(Q_RANK, -1).astype(bf),
              jnp.transpose(w_uk[l], (1, 0, 2)).astype(bf), jnp.transpose(w_uv[l], (1, 2, 0)).astype(bf),
              bias_t, B, S)
    res.update(y_a=ya)
    if upto == "dsa":
        return res

    x1, x1p, sel_t, w_t, pos_t, cnt = _mix_router(
        x2, ya, yb, yc, w_out[l].astype(bf), ln1_g[l].reshape(1, -1), ln1_b[l].reshape(1, -1),
        w_router[l].T, router_bias[l].reshape(-1, 1), tm=min(512, T))
    res.update(x1=x1)

    counts = cnt[:, 0].astype(jnp.int32)
    padded = (counts + ROW_BLOCK - 1) // ROW_BLOCK * ROW_BLOCK
    pad_end = jnp.cumsum(padded)
    pad_start = pad_end - padded
    n_blocks = -(-(T * TOP_K) // ROW_BLOCK) + N_EXPERTS
    n_rows = n_blocks * ROW_BLOCK
    block_start = jnp.arange(n_blocks, dtype=jnp.int32) * ROW_BLOCK
    block_e = jnp.minimum(jnp.sum((pad_end[None, :] <= block_start[:, None]).astype(jnp.int32), axis=1),
                          N_EXPERTS - 1)
    n_used = (pad_end[-1:] // ROW_BLOCK).astype(jnp.int32)

    dest_t, wk_t = _compact(sel_t, w_t, pos_t, pad_start.astype(jnp.float32).reshape(-1, 1), tm=min(512, T))
    block_valid = jnp.clip((pad_start + counts)[block_e] - block_start, 0, ROW_BLOCK).astype(jnp.int32)
    bt = SC_SCATTER_ROWS
    idx3 = jnp.transpose(dest_t.reshape(TOP_K, T // bt, bt), (1, 0, 2))
    xs = _sc_scatter_rows(x1p, idx3, n_rows)
    ys = _experts(xs, block_e, block_valid, n_used, w_e_gate[l], w_e_up[l], w_e_down[l])
    n_chunks = COMBINE_CHUNKS if T % (COMBINE_CHUNKS * 256) == 0 else 1
    t_chunk = T // n_chunks
    out = None
    for c in range(n_chunks):
        idx_c = dest_t[:, c * t_chunk:(c + 1) * t_chunk].reshape(-1)
        gathered = _sc_gather_rows(ys, idx_c).reshape(TOP_K, t_chunk, -1)
        out = _combine2(wk_t, x1, gathered, w_s_gate[l].astype(bf), w_s_up[l].astype(bf), w_s_down[l].astype(bf),
                        ln2_g[l].reshape(1, -1), ln2_b[l].reshape(1, -1), tc=min(256, t_chunk), chunk=c, prev=out)
    res.update(out=out.reshape(B, S, D))
    return res


def kernel(x, mem, w_in, q_norm_g, kv_norm_g, w_uq, w_uk, w_uv, w_qidx, rel_bias, conv_w, w_mem_k, w_mem_v, w_out, ln1_g, ln1_b, w_router, router_bias, w_e_gate, w_e_up, w_e_down, w_s_gate, w_s_up, w_s_down, ln2_g, ln2_b):
    return _stages(x, mem, w_in, q_norm_g, kv_norm_g, w_uq, w_uk, w_uv, w_qidx, rel_bias, conv_w, w_mem_k, w_mem_v, w_out, ln1_g, ln1_b, w_router, router_bias, w_e_gate, w_e_up, w_e_down, w_s_gate, w_s_up, w_s_down, ln2_g, ln2_b)["out"]
```

```python
import functools
import math

import jax
import jax.numpy as jnp
from jax import lax
from jax.experimental import pallas as pl
from jax.experimental.pallas import tpu as pltpu
from jax.experimental.pallas import tpu_sc as plsc

N_HEADS_A = 8
HEAD_DIM = 64
Q_RANK = 256
KV_RANK = 128
N_IDX_HEADS = 8
IDX_DIM = 64
TOPK_MAX = 256
REL_BUCKETS = 32
REL_MAX_DIST = 128
CONV_CH = 256
CONV_WIDTH = 3
N_MEM_HEADS = 4
MIX_A = N_HEADS_A * HEAD_DIM
MIX_C = N_MEM_HEADS * HEAD_DIM
N_EXPERTS = 64
N_GROUPS = 8
GROUP_SIZE = N_EXPERTS // N_GROUPS
TOPK_GROUPS = 4
TOP_K = 8
D_EXPERT = 256
ROUTED_SCALE = 2.5
DEPTH = 1
ALPHA = (2.0 * DEPTH) ** 0.25
LN_EPS = 1e-5
RMS_EPS = 1e-6
LOG2_E = math.log2(math.e)

LANES = 128
SUBLANES = 8
QB = 128
F32_LOWEST = -3.4028234663852886e38
VMEM_LIMIT = 56 * 1024 * 1024
MXU_DTYPE = jnp.bfloat16
ROW_BLOCK = 1024

_NT = (((1,), (1,)), ((), ()))


def _dot(a, b):
    return jnp.dot(a, b, preferred_element_type=jnp.float32)


def _dot_nt(a, b):
    return lax.dot_general(a, b, _NT, preferred_element_type=jnp.float32)


def _cparams(sem):
    return pltpu.CompilerParams(dimension_semantics=sem, vmem_limit_bytes=VMEM_LIMIT)


def _bias_kernel(rb_ref, o_ref):
    s = lax.broadcasted_iota(jnp.int32, (QB, QB), 0)
    t = lax.broadcasted_iota(jnp.int32, (QB, QB), 1)
    max_exact = REL_BUCKETS // 2
    for tile in range(3):
        n = jnp.maximum(t - s + (2 - tile) * QB, 0)
        nf = jnp.maximum(n.astype(jnp.float32), 1.0)
        large = max_exact + (jnp.log(nf / max_exact) / math.log(REL_MAX_DIST / max_exact)
                             * (REL_BUCKETS - max_exact)).astype(jnp.int32)
        large = jnp.minimum(large, REL_BUCKETS - 1)
        bucket = jnp.where(n < max_exact, n, large)
        for h in range(N_HEADS_A):
            acc = jnp.zeros((QB, QB), jnp.float32)
            for b in range(REL_BUCKETS):
                acc = jnp.where(bucket == b, rb_ref[b, h], acc)
            o_ref[tile, h] = acc * LOG2_E


def _bias_tiles(rel_bias):
    return pl.pallas_call(
        _bias_kernel,
        in_specs=[pl.BlockSpec(memory_space=pltpu.SMEM)],
        out_specs=pl.BlockSpec(memory_space=pltpu.VMEM),
        out_shape=jax.ShapeDtypeStruct((3, N_HEADS_A, QB, QB), jnp.float32),
        name="bias_tiles",
    )(rel_bias)


def _proj_kernel(x_ref, mem_ref, wm_ref, ws_ref, qg_ref, kvg_ref, cw_ref, wmk_ref, wmv_ref,
                 cq_ref, ckv_ref, ckvt_ref, kidx_ref, iwt_ref, yb_ref, yc_ref,
                 carry_ref, mk_ref, mv_ref, *, tm):
    si = pl.program_id(1)

    @pl.when(si == 0)
    def _():
        carry_ref[...] = jnp.zeros_like(carry_ref)
        mb = mem_ref[0].astype(MXU_DTYPE)
        mk_ref[...] = _dot(mb, wmk_ref[...]).astype(MXU_DTYPE)
        mv_ref[...] = _dot(mb, wmv_ref[...]).astype(MXU_DTYPE)

    xb = x_ref[...].astype(MXU_DTYPE)
    p = _dot(xb, wm_ref[...])
    small = _dot(xb, ws_ref[...])

    o = 0
    cq = p[:, o:o + Q_RANK]; o += Q_RANK
    ckv = p[:, o:o + KV_RANK]; o += KV_RANK
    g_b = p[:, o:o + CONV_CH]; o += CONV_CH
    g_c = p[:, o:o + CONV_CH]; o += CONV_CH
    h_c = p[:, o:o + CONV_CH]; o += CONV_CH
    q_mem = p[:, o:o + MIX_C]

    cq = cq * lax.rsqrt(jnp.mean(cq * cq, axis=-1, keepdims=True) + RMS_EPS) * qg_ref[...]
    ckv = ckv * lax.rsqrt(jnp.mean(ckv * ckv, axis=-1, keepdims=True) + RMS_EPS) * kvg_ref[...]
    cq_ref[...] = cq.astype(MXU_DTYPE)
    ckv_b = ckv.astype(MXU_DTYPE)
    ckv_ref[...] = ckv_b
    ckvt_ref[0] = ckv.T.astype(MXU_DTYPE)

    kidx_ref[...] = small[:, :IDX_DIM].astype(MXU_DTYPE)
    small_t = small.T
    iwt_ref[0] = small_t[IDX_DIM:IDX_DIM + N_IDX_HEADS, :] * (N_IDX_HEADS ** -0.5 * IDX_DIM ** -0.5)

    u = g_c * h_c
    rows = lax.broadcasted_iota(jnp.int32, (tm, 1), 0)
    c6 = carry_ref[SUBLANES - 2:SUBLANES - 1, :]
    c7 = carry_ref[SUBLANES - 1:SUBLANES, :]
    u1 = jnp.where(rows == 0, c7, pltpu.roll(u, 1, 0))
    u2 = jnp.where(rows == 0, c6, jnp.where(rows == 1, c7, pltpu.roll(u, 2, 0)))
    y = cw_ref[0:1, :] * u2
    y = y + cw_ref[1:2, :] * u1
    y = y + cw_ref[2:3, :] * u
    yb_ref[...] = (g_b * y).astype(MXU_DTYPE)
    carry_ref[...] = u[tm - SUBLANES:, :]

    qm = q_mem.astype(MXU_DTYPE)
    outs = []
    for h in range(N_MEM_HEADS):
        sl = slice(h * HEAD_DIM, (h + 1) * HEAD_DIM)
        lg = _dot_nt(qm[:, sl], mk_ref[:, sl]) * (HEAD_DIM ** -0.5)
        lg = lg - jnp.max(lg, axis=-1, keepdims=True)
        e = jnp.exp(lg)
        pr = e / jnp.sum(e, axis=-1, keepdims=True)
        outs.append(_dot(pr.astype(MXU_DTYPE), mv_ref[:, sl]))
    yc_ref[...] = jnp.concatenate(outs, axis=-1).astype(MXU_DTYPE)


def _proj(x2, mem, w_main, w_small, q_g, kv_g, conv_w, w_mk, w_mv, B, S, tm):
    T, D = x2.shape
    n_mem = mem.shape[1]
    ns = S // tm
    row = lambda b, s: (b * ns + s, 0)
    const2 = lambda b, s: (0, 0)
    bf = MXU_DTYPE
    return pl.pallas_call(
        functools.partial(_proj_kernel, tm=tm),
        grid=(B, ns),
        in_specs=[
            pl.BlockSpec((tm, D), row),
            pl.BlockSpec((1, n_mem, D), lambda b, s: (b, 0, 0)),
            pl.BlockSpec(w_main.shape, const2),
            pl.BlockSpec(w_small.shape, const2),
            pl.BlockSpec(q_g.shape, const2),
            pl.BlockSpec(kv_g.shape, const2),
            pl.BlockSpec(conv_w.shape, const2),
            pl.BlockSpec(w_mk.shape, const2),
            pl.BlockSpec(w_mv.shape, const2),
        ],
        out_specs=[
            pl.BlockSpec((tm, Q_RANK), row),
            pl.BlockSpec((tm, KV_RANK), row),
            pl.BlockSpec((1, KV_RANK, tm), lambda b, s: (b, 0, s)),
            pl.BlockSpec((tm, IDX_DIM), row),
            pl.BlockSpec((1, N_IDX_HEADS, tm), lambda b, s: (b, 0, s)),
            pl.BlockSpec((tm, CONV_CH), row),
            pl.BlockSpec((tm, MIX_C), row),
        ],
        out_shape=[
            jax.ShapeDtypeStruct((T, Q_RANK), bf),
            jax.ShapeDtypeStruct((T, KV_RANK), bf),
            jax.ShapeDtypeStruct((B, KV_RANK, S), bf),
            jax.ShapeDtypeStruct((T, IDX_DIM), bf),
            jax.ShapeDtypeStruct((B, N_IDX_HEADS, S), jnp.float32),
            jax.ShapeDtypeStruct((T, CONV_CH), bf),
            jax.ShapeDtypeStruct((T, MIX_C), bf),
        ],
        scratch_shapes=[
            pltpu.VMEM((SUBLANES, CONV_CH), jnp.float32),
            pltpu.VMEM((n_mem, MIX_C), bf),
            pltpu.VMEM((n_mem, MIX_C), bf),
        ],
        compiler_params=_cparams(("arbitrary", "arbitrary")),
        name="proj",
    )(x2, mem, w_main, w_small, q_g, kv_g, conv_w, w_mk, w_mv)


def _key_to_f32(key):
    bits = jnp.where(key < 0, key ^ jnp.int32(0x7FFFFFFF), key)
    return pltpu.bitcast(bits, jnp.float32)


def _colsum8(v):
    return jnp.sum(v.reshape(QB // SUBLANES, SUBLANES, QB), axis=0)


def _colmax8(v):
    return jnp.max(v.reshape(QB // SUBLANES, SUBLANES, QB), axis=0)


UNROLL_WIDTHS = (8, 4, 2, 1)


def _dsa_kernel(cq_ref, iwt_ref, kidx_ref, ckv_ref, ckvt_ref, wqi_ref, wuq_ref, wuk_ref, wuvt_ref,
                bias_ref, o_ref, wfold_ref, qidx_ref, qlat_ref, score_ref, mask_ref, logit_ref, acc_ref,
                *, k_sel, idx_bits):
    i = pl.program_id(1)
    f32 = jnp.float32
    bf = MXU_DTYPE
    n_blocks = i + 1
    n_blocks = n_blocks + jnp.where((n_blocks % 4 == 3) & (n_blocks < pl.num_programs(1)), 1, 0)
    s_loc = lax.broadcasted_iota(jnp.int32, (QB, QB), 0)
    t_glob = i * QB + lax.broadcasted_iota(jnp.int32, (QB, QB), 1)

    def blk(jb):
        return pl.multiple_of(jb * QB, QB)

    def block_loop(fn, init):
        c, start = init, 0
        for width in UNROLL_WIDTHS:
            n = (n_blocks - start) // width
            c = lax.fori_loop(0, n, lambda it, c, w=width, s=start: fn(s + it * w, w, c), c)
            start = start + n * width
        return c

    @pl.when(i == 0)
    def _():
        for h in range(N_HEADS_A):
            wfold_ref[:, h * KV_RANK:(h + 1) * KV_RANK] = (
                _dot_nt(wuq_ref[:, h * HEAD_DIM:(h + 1) * HEAD_DIM], wuk_ref[h])
                * (HEAD_DIM ** -0.5 * LOG2_E)).astype(bf)

    cq = cq_ref[...]
    q_idx = _dot(cq, wqi_ref[...]).astype(bf)
    q_lat = _dot(cq, wfold_ref[...]).astype(bf)
    for h in range(N_HEADS_A):
        qidx_ref[h * QB:(h + 1) * QB, :] = q_idx[:, h * IDX_DIM:(h + 1) * IDX_DIM]
        qlat_ref[h * QB:(h + 1) * QB, :] = q_lat[:, h * KV_RANK:(h + 1) * KV_RANK]
    iw = iwt_ref[0]

    def score_body(jb0, nb, c):
        d_blk = _dot_nt(kidx_ref[pl.ds(blk(jb0), nb * QB), :], qidx_ref[...])
        for sb in range(nb):
            off = blk(jb0 + sb)
            d_all = d_blk[sb * QB:(sb + 1) * QB, :]
            acc = jnp.maximum(d_all[:, 0:QB], 0.0) * iw[0:1, :]
            for h in range(1, N_IDX_HEADS):
                acc = acc + jnp.maximum(d_all[:, h * QB:(h + 1) * QB], 0.0) * iw[h:h + 1, :]
            score_ref[pl.ds(off, QB), :] = jnp.where(s_loc + off <= t_glob, acc + 0.0, F32_LOWEST)
        return c

    block_loop(score_body, 0)

    def count_where(pred):
        def body(jb0, nb, acc):
            for sb in range(nb):
                off = blk(jb0 + sb)
                acc = acc + _colsum8(jnp.where(pred(score_ref[pl.ds(off, QB), :], off), 1.0, 0.0))
            return acc
        acc = block_loop(body, jnp.zeros((SUBLANES, QB), f32))
        return jnp.sum(acc, axis=0, keepdims=True)

    kf = float(k_sel)

    def search():
        c0 = count_where(lambda sc, off: sc >= 0.0)
        cand0 = jnp.where(c0 >= kf, jnp.int32(0), jnp.int32(-2 ** 31))

        def bit_body(it, cand):
            trial = cand + lax.shift_left(jnp.int32(1), 30 - it)
            tf = _key_to_f32(trial)
            cnt = count_where(lambda sc, off: sc >= tf)
            return jnp.where(cnt >= kf, trial, cand)

        cand = lax.fori_loop(0, 31, bit_body, cand0)
        thr = _key_to_f32(cand)
        n_gt = count_where(lambda sc, off: sc > thr)
        n_eq = count_where(lambda sc, off: sc == thr)
        need = kf - n_gt

        def tie_search():
            def tbody(it, xcut):
                trial = xcut + lax.shift_left(jnp.int32(1), idx_bits - 1 - it)
                cnt = count_where(lambda sc, off: (sc == thr) & (s_loc + off < trial))
                return jnp.where(cnt < need, trial, xcut)
            return lax.fori_loop(0, idx_bits, tbody, jnp.zeros((1, QB), jnp.int32))

        any_extra = jnp.max(n_eq - need) > 0.0
        xcut = lax.cond(any_extra, tie_search, lambda: jnp.full((1, QB), 2 ** idx_bits - 1, jnp.int32))
        return thr, xcut

    def no_search():
        return jnp.full((1, QB), F32_LOWEST, f32), jnp.full((1, QB), 2 ** idx_bits - 1, jnp.int32)

    thr, xcut = lax.cond((i + 1) * QB > k_sel, search, no_search)

    def mask_body(jb0, nb, c):
        for sb in range(nb):
            off = blk(jb0 + sb)
            sc = score_ref[pl.ds(off, QB), :]
            s_glob = s_loc + off
            keep = ((sc > thr) | ((sc == thr) & (s_glob <= xcut))) & (s_glob <= t_glob)
            mask_ref[pl.ds(off, QB), :] = jnp.where(keep, 0.0, -jnp.inf)
        return c

    block_loop(mask_body, 0)

    def p1_body(jb0, nb, m8):
        m8 = list(m8)
        lg_blk = _dot_nt(ckv_ref[pl.ds(blk(jb0), nb * QB), :], qlat_ref[...])
        for sb in range(nb):
            off = blk(jb0 + sb)
            lg = lg_blk[sb * QB:(sb + 1) * QB, :]
            msk = mask_ref[pl.ds(off, QB), :]
            bsel = jnp.clip(jb0 + sb - i + 2, 0, 2)
            for h in range(N_HEADS_A):
                lgh = lg[:, h * QB:(h + 1) * QB] + bias_ref[bsel, h] + msk
                logit_ref[pl.ds(off, QB), h * QB:(h + 1) * QB] = lgh
                m8[h] = jnp.maximum(m8[h], _colmax8(lgh))
        return tuple(m8)

    m8 = block_loop(p1_body, tuple(jnp.full((SUBLANES, QB), -jnp.inf, f32) for _ in range(N_HEADS_A)))
    m_row = [jnp.max(m, axis=0, keepdims=True) for m in m8]

    acc_ref[...] = jnp.zeros_like(acc_ref)

    def p2_body(jb0, nb, l8):
        l8 = list(l8)
        off = blk(jb0)
        rows = nb * QB
        ps = []
        for h in range(N_HEADS_A):
            p = jnp.exp2(logit_ref[pl.ds(off, rows), h * QB:(h + 1) * QB] - m_row[h])
            l8[h] = l8[h] + jnp.sum(p.reshape(rows // SUBLANES, SUBLANES, QB), axis=0)
            ps.append(p.astype(bf))
        acc_ref[...] += _dot(ckvt_ref[0, :, pl.ds(off, rows)], jnp.concatenate(ps, axis=1))
        return tuple(l8)

    l8 = block_loop(p2_body, tuple(jnp.zeros((SUBLANES, QB), f32) for _ in range(N_HEADS_A)))

    outs = []
    for h in range(N_HEADS_A):
        l_row = jnp.sum(l8[h], axis=0, keepdims=True)
        o_lat_t = (acc_ref[:, h * QB:(h + 1) * QB] / l_row).astype(bf)
        outs.append(_dot(wuvt_ref[h], o_lat_t))
    o_ref[...] = jnp.concatenate(outs, axis=0).T.astype(o_ref.dtype)


def _dsa(cq, iwt, kidx, ckv, ckvt, w_qidx, w_uq, w_uk_h, w_uvt_h, bias_tiles, B, S):
    T = cq.shape[0]
    assert S % QB == 0 and QB >= REL_MAX_DIST
    nq = S // QB
    k_sel = min(TOPK_MAX, S // 4)
    idx_bits = max(1, (S - 1).bit_length())
    c2 = lambda b, i: (0, 0)
    c3 = lambda b, i: (0, 0, 0)
    return pl.pallas_call(
        functools.partial(_dsa_kernel, k_sel=k_sel, idx_bits=idx_bits),
        grid=(B, nq),
        in_specs=[
            pl.BlockSpec((QB, Q_RANK), lambda b, i: (b * nq + i, 0)),
            pl.BlockSpec((1, N_IDX_HEADS, QB), lambda b, i: (b, 0, i)),
            pl.BlockSpec((S, IDX_DIM), lambda b, i: (b, 0)),
            pl.BlockSpec((S, KV_RANK), lambda b, i: (b, 0)),
            pl.BlockSpec((1, KV_RANK, S), lambda b, i: (b, 0, 0)),
            pl.BlockSpec(w_qidx.shape, c2),
            pl.BlockSpec(w_uq.shape, c2),
            pl.BlockSpec(w_uk_h.shape, c3),
            pl.BlockSpec(w_uvt_h.shape, c3),
            pl.BlockSpec(bias_tiles.shape, lambda b, i: (0, 0, 0, 0)),
        ],
        out_specs=pl.BlockSpec((QB, MIX_A), lambda b, i: (b * nq + i, 0)),
        out_shape=jax.ShapeDtypeStruct((T, MIX_A), MXU_DTYPE),
        scratch_shapes=[
            pltpu.VMEM((Q_RANK, N_HEADS_A * KV_RANK), MXU_DTYPE),
            pltpu.VMEM((N_IDX_HEADS * QB, IDX_DIM), MXU_DTYPE),
            pltpu.VMEM((N_HEADS_A * QB, KV_RANK), MXU_DTYPE),
            pltpu.VMEM((S, QB), jnp.float32),
            pltpu.VMEM((S, QB), jnp.float32),
            pltpu.VMEM((S, N_HEADS_A * QB), jnp.float32),
            pltpu.VMEM((KV_RANK, N_HEADS_A * QB), jnp.float32),
        ],
        compiler_params=_cparams(("arbitrary", "arbitrary")),
        name="dsa",
    )(cq, iwt, kidx, ckv, ckvt, w_qidx, w_uq, w_uk_h, w_uvt_h, bias_tiles)


def _layer_norm(xf, g, b):
    mu = jnp.mean(xf, axis=-1, keepdims=True)
    xc = xf - mu
    var = jnp.mean(xc * xc, axis=-1, keepdims=True)
    return xc * lax.rsqrt(var + LN_EPS) * g + b


def _rank_rows(v, n):
    ri = lax.broadcasted_iota(jnp.int32, v.shape, 0)
    rank = jnp.zeros(v.shape, jnp.float32)
    for r2 in range(n):
        row = v[r2:r2 + 1, :]
        beats = (row > v) | ((row == v) & (ri > r2))
        rank = rank + jnp.where(beats, 1.0, 0.0)
    return rank


def _top_rows(v, k):
    n = v.shape[0]
    ri = lax.broadcasted_iota(jnp.int32, v.shape, 0)
    sel = jnp.zeros(v.shape, jnp.float32)
    for _ in range(k):
        m = jnp.max(v, axis=0, keepdims=True)
        first = jnp.min(jnp.where(v == m, ri, n), axis=0, keepdims=True)
        pick = ri == first
        sel = jnp.where(pick, 1.0, sel)
        v = jnp.where(pick, -jnp.inf, v)
    return sel > 0.5


def _pack_factor():
    return 4 // jnp.dtype(MXU_DTYPE).itemsize


def _pack_rows(x):
    if _pack_factor() == 1:
        return pltpu.bitcast(x, jnp.int32)
    half = x.shape[1] // 2
    b = pltpu.bitcast(x.astype(MXU_DTYPE).astype(jnp.float32), jnp.int32)
    return b[:, half:] | lax.shift_right_logical(b[:, :half], jnp.int32(16))


_HIGH_HALF = -(1 << 16)


def _unpack_rows_f32(p):
    if _pack_factor() == 1:
        return [pltpu.bitcast(p, jnp.float32)]
    lo = pltpu.bitcast(lax.shift_left(p, jnp.int32(16)), jnp.float32)
    hi = pltpu.bitcast(p & jnp.int32(_HIGH_HALF), jnp.float32)
    return [lo, hi]


def _unpack_rows(p):
    return [v.astype(MXU_DTYPE) for v in _unpack_rows_f32(p)]


def _mix_router_kernel(x_ref, ya_ref, yb_ref, yc_ref, wo_ref, g_ref, b_ref, wrt_ref, rb_ref, exp_ref,
                       x1_ref, x1p_ref, sel_ref, w_ref, pos_ref, cnt_ref, base_ref, *, tm):
    step = pl.program_id(0)
    f32 = jnp.float32

    @pl.when(step == 0)
    def _():
        base_ref[...] = jnp.zeros_like(base_ref)

    mix = _dot(ya_ref[...], wo_ref[0:MIX_A, :])
    mix = mix + _dot(yb_ref[...], wo_ref[MIX_A:MIX_A + CONV_CH, :])
    mix = mix + _dot(yc_ref[...], wo_ref[MIX_A + CONV_CH:, :])
    x1 = _layer_norm(ALPHA * x_ref[...] + mix, g_ref[...], b_ref[...])
    x1_ref[...] = x1
    x1p_ref[...] = _pack_rows(x1)

    lg = lax.dot_general(wrt_ref[...], x1, _NT, precision=lax.Precision.HIGHEST, preferred_element_type=f32)
    s = 1.0 / (1.0 + jnp.exp(-lg))
    sc = s + rb_ref[...]

    g3 = sc.reshape(N_GROUPS, GROUP_SIZE, tm)
    m1 = jnp.max(g3, axis=1, keepdims=True)
    is_m1 = g3 == m1
    n_m1 = jnp.sum(jnp.where(is_m1, 1.0, 0.0), axis=1, keepdims=True)
    m2 = jnp.max(jnp.where(is_m1, -jnp.inf, g3), axis=1, keepdims=True)
    gscore = (m1 + jnp.where(n_m1 > 1.0, m1, m2)).reshape(N_GROUPS, tm)
    gsel = jnp.where(_rank_rows(gscore, N_GROUPS) < float(TOPK_GROUPS), 1.0, 0.0)
    emask = _dot(exp_ref[...], gsel.astype(MXU_DTYPE)) > 0.5
    masked = jnp.where(emask, sc, -jnp.inf)
    sel = _top_rows(masked, TOP_K) & emask
    self_ = jnp.where(sel, 1.0, 0.0)
    top_s = jnp.where(sel, s, 0.0)
    w = top_s / jnp.sum(top_s, axis=0, keepdims=True) * ROUTED_SCALE

    t_r = lax.broadcasted_iota(jnp.int32, (tm, tm), 0)
    t_c = lax.broadcasted_iota(jnp.int32, (tm, tm), 1)
    upper = jnp.where(t_r < t_c, 1.0, 0.0).astype(MXU_DTYPE)
    pref = _dot(self_.astype(MXU_DTYPE), upper)
    base = base_ref[...]
    sel_ref[...] = self_
    w_ref[...] = w
    pos_ref[...] = base + pref
    base = base + jnp.sum(self_, axis=1, keepdims=True)
    base_ref[...] = base
    cnt_ref[...] = jnp.broadcast_to(base, cnt_ref.shape)


def _mix_router(x2, ya, yb, yc, w_out, ln_g, ln_b, w_router_t, router_bias, tm):
    T, D = x2.shape
    E = N_EXPERTS
    expand = (jnp.arange(E)[:, None] // GROUP_SIZE == jnp.arange(N_GROUPS)[None, :]).astype(MXU_DTYPE)
    row = lambda i: (i, 0)
    col = lambda i: (0, i)
    c2 = lambda i: (0, 0)
    f32 = jnp.float32
    return pl.pallas_call(
        functools.partial(_mix_router_kernel, tm=tm),
        grid=(T // tm,),
        in_specs=[
            pl.BlockSpec((tm, D), row),
            pl.BlockSpec((tm, MIX_A), row),
            pl.BlockSpec((tm, CONV_CH), row),
            pl.BlockSpec((tm, MIX_C), row),
            pl.BlockSpec(w_out.shape, c2),
            pl.BlockSpec((1, D), c2),
            pl.BlockSpec((1, D), c2),
            pl.BlockSpec((E, D), c2),
            pl.BlockSpec((E, 1), c2),
            pl.BlockSpec((E, N_GROUPS), c2),
        ],
        out_specs=[
            pl.BlockSpec((tm, D), row),
            pl.BlockSpec((tm, D // _pack_factor()), row),
            pl.BlockSpec((E, tm), col),
            pl.BlockSpec((E, tm), col),
            pl.BlockSpec((E, tm), col),
            pl.BlockSpec((E, LANES), c2),
        ],
        out_shape=[
            jax.ShapeDtypeStruct((T, D), f32),
            jax.ShapeDtypeStruct((T, D // _pack_factor()), jnp.int32),
            jax.ShapeDtypeStruct((E, T), f32),
            jax.ShapeDtypeStruct((E, T), f32),
            jax.ShapeDtypeStruct((E, T), f32),
            jax.ShapeDtypeStruct((E, LANES), f32),
        ],
        scratch_shapes=[pltpu.VMEM((E, 1), f32)],
        compiler_params=_cparams(("arbitrary",)),
        name="mix_router",
    )(x2, ya, yb, yc, w_out, ln_g, ln_b, w_router_t, router_bias, expand)


def _compact_kernel(sel_ref, w_ref, pos_ref, pstart_ref, low_ref, dest_ref, wk_ref):
    sel = sel_ref[...]
    on = sel > 0.5
    rank = _dot(low_ref[...], sel.astype(MXU_DTYPE))
    row = pstart_ref[...] + pos_ref[...]
    w = w_ref[...]
    dests, ws = [], []
    for k in range(TOP_K):
        m = on & (rank == float(k))
        dests.append(jnp.sum(jnp.where(m, row, 0.0), axis=0, keepdims=True))
        ws.append(jnp.sum(jnp.where(m, w, 0.0), axis=0, keepdims=True))
    dest_ref[...] = jnp.concatenate(dests, axis=0).astype(jnp.int32)
    wk_ref[...] = jnp.concatenate(ws, axis=0)


def _compact(sel_t, w_t, pos_t, pad_start, tm):
    E, T = sel_t.shape
    lower = (jnp.arange(E)[None, :] < jnp.arange(E)[:, None]).astype(MXU_DTYPE)
    col = lambda i: (0, i)
    c2 = lambda i: (0, 0)
    return pl.pallas_call(
        _compact_kernel,
        grid=(T // tm,),
        in_specs=[pl.BlockSpec((E, tm), col), pl.BlockSpec((E, tm), col), pl.BlockSpec((E, tm), col),
                  pl.BlockSpec((E, 1), c2), pl.BlockSpec((E, E), c2)],
        out_specs=[pl.BlockSpec((TOP_K, tm), col), pl.BlockSpec((TOP_K, tm), col)],
        out_shape=[jax.ShapeDtypeStruct((TOP_K, T), jnp.int32), jax.ShapeDtypeStruct((TOP_K, T), jnp.float32)],
        compiler_params=_cparams(("arbitrary",)),
        name="route_compact",
    )(sel_t, w_t, pos_t, pad_start, lower)


def _silu(g):
    return g / (1.0 + jnp.exp(-g))


def _expert_kernel(be_ref, nv_ref, nu_ref, xs_ref, wg_ref, wu_ref, wd_ref, ys_ref, wgb_ref, wub_ref, wdb_ref):
    i = pl.program_id(0)

    @pl.when((i == 0) | (be_ref[i] != be_ref[jnp.maximum(i - 1, 0)]))
    def _():
        wgb_ref[...] = wg_ref[0].astype(MXU_DTYPE)
        wub_ref[...] = wu_ref[0].astype(MXU_DTYPE)
        wdb_ref[...] = wd_ref[0].astype(MXU_DTYPE)

    @pl.when(i < nu_ref[0])
    def _():
        live = lax.broadcasted_iota(jnp.int32, (ROW_BLOCK, 1), 0) < nv_ref[i]
        parts = [jnp.where(live, v, jnp.zeros_like(v)) for v in _unpack_rows(xs_ref[...])]
        dk = wgb_ref.shape[0] // len(parts)

        def proj(w_ref):
            acc = _dot(parts[0], w_ref[0:dk, :])
            for n in range(1, len(parts)):
                acc = acc + _dot(parts[n], w_ref[n * dk:(n + 1) * dk, :])
            return acc

        a = (_silu(proj(wgb_ref)) * proj(wub_ref)).astype(MXU_DTYPE)
        ys_ref[...] = _pack_rows(_dot(a, wdb_ref[...]))


def _experts(xs, block_e, block_valid, n_used, w_gate, w_up, w_down):
    n_rows, W = xs.shape
    D = w_gate.shape[1]
    n_blocks = n_rows // ROW_BLOCK
    blk = lambda i, be, nv, nu: (jnp.minimum(i, nu[0] - 1), 0)
    wsel = lambda i, be, nv, nu: (be[i], 0, 0)
    return pl.pallas_call(
        _expert_kernel,
        grid_spec=pltpu.PrefetchScalarGridSpec(
            num_scalar_prefetch=3,
            grid=(n_blocks,),
            in_specs=[
                pl.BlockSpec((ROW_BLOCK, W), blk),
                pl.BlockSpec((1, D, D_EXPERT), wsel),
                pl.BlockSpec((1, D, D_EXPERT), wsel),
                pl.BlockSpec((1, D_EXPERT, D), wsel),
            ],
            out_specs=pl.BlockSpec((ROW_BLOCK, W), blk),
            scratch_shapes=[pltpu.VMEM((D, D_EXPERT), MXU_DTYPE), pltpu.VMEM((D, D_EXPERT), MXU_DTYPE),
                            pltpu.VMEM((D_EXPERT, D), MXU_DTYPE)],
        ),
        out_shape=jax.ShapeDtypeStruct((n_rows, W), xs.dtype),
        compiler_params=_cparams(("arbitrary",)),
        name="experts",
    )(block_e, block_valid, n_used, xs, w_gate, w_up, w_down)


SC_CORES = 2
SC_SUBCORES = 16
SC_GATHER_ROWS = 64
COMBINE_CHUNKS = 8


def _sc_gather_rows(table, idx):
    n = idx.shape[0]
    w = table.shape[1]
    n_workers = SC_CORES * SC_SUBCORES
    per_worker = n // n_workers
    assert n % n_workers == 0 and per_worker % SC_GATHER_ROWS == 0
    mesh = plsc.VectorSubcoreMesh(core_axis_name="c", subcore_axis_name="s")

    @functools.partial(
        pl.kernel, mesh=mesh,
        out_type=jax.ShapeDtypeStruct((n, w), table.dtype),
        scratch_types=[
            pltpu.VMEM((2, SC_GATHER_ROWS), jnp.int32),
            pltpu.VMEM((2, SC_GATHER_ROWS, w), table.dtype),
            pltpu.SemaphoreType.DMA((2,)),
        ],
        name="sc_gather_rows",
    )
    def gather(table_hbm, idx_hbm, out_hbm, idx_v, rows_v, sem):
        wid = lax.axis_index("s") * SC_CORES + lax.axis_index("c")
        base = wid * per_worker
        n_steps = per_worker // SC_GATHER_ROWS

        def gather_copy(slot):
            return pltpu.make_async_copy(table_hbm.at[idx_v.at[slot]], rows_v.at[slot], sem.at[slot])

        def start(step, slot):
            pltpu.sync_copy(idx_hbm.at[pl.ds(base + step * SC_GATHER_ROWS, SC_GATHER_ROWS)], idx_v.at[slot])
            gather_copy(slot).start()

        start(0, 0)

        @pl.loop(0, n_steps, step=2)
        def _(g):
            for slot in range(2):
                step = g + slot

                @pl.when(step + 1 < n_steps)
                def _():
                    start(step + 1, 1 - slot)

                gather_copy(slot).wait()
                pltpu.sync_copy(rows_v.at[slot], out_hbm.at[pl.ds(base + step * SC_GATHER_ROWS, SC_GATHER_ROWS)])

    return gather(table, idx)


SC_SCATTER_ROWS = 64


def _sc_scatter_rows(rows, idx3, n_out):
    n_src, w = rows.shape
    n_chunks, n_dst, batch = idx3.shape
    n_workers = SC_CORES * SC_SUBCORES
    assert batch == SC_SCATTER_ROWS and n_chunks * batch == n_src and n_chunks % (2 * n_workers) == 0
    per_worker = n_chunks // n_workers
    mesh = plsc.VectorSubcoreMesh(core_axis_name="c", subcore_axis_name="s")

    @functools.partial(
        pl.kernel, mesh=mesh,
        out_type=jax.ShapeDtypeStruct((n_out, w), rows.dtype),
        scratch_types=[
            pltpu.VMEM((2, n_dst, batch), jnp.int32),
            pltpu.VMEM((2, batch, w), rows.dtype),
            pltpu.SemaphoreType.DMA((2,)),
            pltpu.SemaphoreType.DMA,
        ],
        name="sc_scatter_rows",
    )
    def scatter(rows_hbm, idx_hbm, out_hbm, idx_v, rows_v, load_sem, store_sem):
        wid = lax.axis_index("s") * SC_CORES + lax.axis_index("c")

        def load_copy(step, slot):
            c = wid * per_worker + step
            return pltpu.make_async_copy(rows_hbm.at[pl.ds(c * batch, batch)], rows_v.at[slot], load_sem.at[slot])

        def load(step, slot):
            pltpu.sync_copy(idx_hbm.at[wid * per_worker + step], idx_v.at[slot])
            load_copy(step, slot).start()

        def store_copy(slot, k):
            return pltpu.make_async_copy(rows_v.at[slot], out_hbm.at[idx_v.at[slot].at[k]], store_sem)

        load(0, 0)

        @pl.loop(0, per_worker, step=2)
        def _(g):
            for slot in range(2):
                step = g + slot

                @pl.when(step + 1 < per_worker)
                def _():
                    load(step + 1, 1 - slot)

                load_copy(step, slot).wait()
                for k in range(n_dst):
                    store_copy(slot, k).start()
                for k in range(n_dst):
                    store_copy(slot, k).wait()

    return scatter(rows, idx3)


def _combine2_kernel(wk_ref, x1_ref, g_ref_rows, wsg_ref, wsu_ref, wsd_ref, g_ref, b_ref, o_ref):
    x1 = x1_ref[...]
    xb = x1.astype(MXU_DTYPE)
    a = (_silu(_dot(xb, wsg_ref[...])) * _dot(xb, wsu_ref[...])).astype(MXU_DTYPE)
    shared = _dot(a, wsd_ref[...])
    wk = wk_ref[...].T
    groups = [wk[:, 0:1] * v for v in _unpack_rows_f32(g_ref_rows[0])]
    for k in range(1, TOP_K):
        groups = [g + wk[:, k:k + 1] * v for g, v in zip(groups, _unpack_rows_f32(g_ref_rows[k]))]
    routed = jnp.concatenate(groups, axis=1)
    o_ref[...] = _layer_norm(ALPHA * x1 + (routed + shared), g_ref[...], b_ref[...])


def _combine2_kernel_into(wk_ref, x1_ref, g_ref_rows, wsg_ref, wsu_ref, wsd_ref, g_ref, b_ref, prev_ref, o_ref):
    del prev_ref
    _combine2_kernel(wk_ref, x1_ref, g_ref_rows, wsg_ref, wsu_ref, wsd_ref, g_ref, b_ref, o_ref)


def _combine2(wk_t, x1, gathered, w_sg, w_su, w_sd, ln_g, ln_b, tc, chunk, prev):
    T, D = x1.shape
    _, t_chunk, W = gathered.shape
    base = chunk * (t_chunk // tc)
    row = lambda i: (base + i, 0)
    c2 = lambda i: (0, 0)
    in_specs = [
        pl.BlockSpec((TOP_K, tc), lambda i: (0, base + i)),
        pl.BlockSpec((tc, D), row),
        pl.BlockSpec((TOP_K, tc, W), lambda i: (0, i, 0)),
        pl.BlockSpec(w_sg.shape, c2),
        pl.BlockSpec(w_su.shape, c2),
        pl.BlockSpec(w_sd.shape, c2),
        pl.BlockSpec((1, D), c2),
        pl.BlockSpec((1, D), c2),
    ]
    args = [wk_t, x1, gathered, w_sg, w_su, w_sd, ln_g, ln_b]
    if prev is None:
        body, aliases = _combine2_kernel, {}
    else:
        body, aliases = _combine2_kernel_into, {len(args): 0}
        in_specs.append(pl.BlockSpec(memory_space=pl.ANY))
        args.append(prev)
    return pl.pallas_call(
        body,
        grid=(t_chunk // tc,),
        in_specs=in_specs,
        out_specs=pl.BlockSpec((tc, D), row),
        out_shape=jax.ShapeDtypeStruct((T, D), jnp.float32),
        input_output_aliases=aliases,
        compiler_params=_cparams(("arbitrary",)),
        name="combine",
    )(*args)


def _split_w_in(w_in):
    bf = MXU_DTYPE
    o_kv = Q_RANK
    o_ki = o_kv + KV_RANK
    o_iw = o_ki + IDX_DIM
    o_rest = o_iw + N_IDX_HEADS
    w_main = jnp.concatenate([w_in[:, :o_ki], w_in[:, o_rest:]], axis=1).astype(bf)
    w_small = jnp.pad(w_in[:, o_ki:o_rest], ((0, 0), (0, LANES - IDX_DIM - N_IDX_HEADS))).astype(bf)
    return w_main, w_small


def _stages(x, mem, w_in, q_norm_g, kv_norm_g, w_uq, w_uk, w_uv, w_qidx, rel_bias, conv_w, w_mem_k, w_mem_v, w_out, ln1_g, ln1_b, w_router, router_bias, w_e_gate, w_e_up, w_e_down, w_s_gate, w_s_up, w_s_down, ln2_g, ln2_b, upto=None):
    B, S, D = x.shape
    T = B * S
    bf = MXU_DTYPE
    l = 0
    res = {}
    x2 = x.reshape(T, D)
    w_main, w_small = _split_w_in(w_in[l])
    cq, ckv, ckvt, kidx, iwt, yb, yc = _proj(
        x2, mem, w_main, w_small, q_norm_g[l].reshape(1, -1), kv_norm_g[l].reshape(1, -1), conv_w[l],
        w_mem_k[l].astype(bf), w_mem_v[l].astype(bf), B, S, tm=min(512, S))
    res.update(c_q=cq, c_kv=ckv, k_idx=kidx, y_b=yb, y_c=yc,
               idx_w=jnp.swapaxes(iwt, 1, 2) / (N_IDX_HEADS ** -0.5 * IDX_DIM ** -0.5))
    if upto == "proj":
        return res
    bias_t = _bias_tiles(rel_bias)
    ya = _dsa(cq, iwt, kidx, ckv, ckvt,
              w_qidx[l].reshape(Q_RANK, -1).astype(bf), w_uq[l].reshape(Q_RANK, -1).astype(bf),
              jnp.transpose(w_uk[l], (1, 0, 2)).astype(bf), jnp.transpose(w_uv[l], (1, 2, 0)).astype(bf),
              bias_t, B, S)
    res.update(y_a=ya)
    if upto == "dsa":
        return res

    x1, x1p, sel_t, w_t, pos_t, cnt = _mix_router(
        x2, ya, yb, yc, w_out[l].astype(bf), ln1_g[l].reshape(1, -1), ln1_b[l].reshape(1, -1),
        w_router[l].T, router_bias[l].reshape(-1, 1), tm=min(512, T))
    res.update(x1=x1)

    counts = cnt[:, 0].astype(jnp.int32)
    padded = (counts + ROW_BLOCK - 1) // ROW_BLOCK * ROW_BLOCK
    pad_end = jnp.cumsum(padded)
    pad_start = pad_end - padded
    n_blocks = -(-(T * TOP_K) // ROW_BLOCK) + N_EXPERTS
    n_rows = n_blocks * ROW_BLOCK
    block_start = jnp.arange(n_blocks, dtype=jnp.int32) * ROW_BLOCK
    block_e = jnp.minimum(jnp.sum((pad_end[None, :] <= block_start[:, None]).astype(jnp.int32), axis=1),
                          N_EXPERTS - 1)
    n_used = (pad_end[-1:] // ROW_BLOCK).astype(jnp.int32)

    dest_t, wk_t = _compact(sel_t, w_t, pos_t, pad_start.astype(jnp.float32).reshape(-1, 1), tm=min(512, T))
    block_valid = jnp.clip((pad_start + counts)[block_e] - block_start, 0, ROW_BLOCK).astype(jnp.int32)
    bt = SC_SCATTER_ROWS
    idx3 = jnp.transpose(dest_t.reshape(TOP_K, T // bt, bt), (1, 0, 2))
    xs = _sc_scatter_rows(x1p, idx3, n_rows)
    ys = _experts(xs, block_e, block_valid, n_used, w_e_gate[l], w_e_up[l], w_e_down[l])
    n_chunks = COMBINE_CHUNKS if T % (COMBINE_CHUNKS * 256) == 0 else 1
    t_chunk = T // n_chunks
    out = None
    for c in range(n_chunks):
        idx_c = dest_t[:, c * t_chunk:(c + 1) * t_chunk].reshape(-1)
        gathered = _sc_gather_rows(ys, idx_c).reshape(TOP_K, t_chunk, -1)
        out = _combine2(wk_t, x1, gathered, w_s_gate[l].astype(bf), w_s_up[l].astype(bf), w_s_down[l].astype(bf),
                        ln2_g[l].reshape(1, -1), ln2_b[l].reshape(1, -1), tc=min(256, t_chunk), chunk=c, prev=out)
    res.update(out=out.reshape(B, S, D))
    return res


def kernel(x, mem, w_in, q_norm_g, kv_norm_g, w_uq, w_uk, w_uv, w_qidx, rel_bias, conv_w, w_mem_k, w_mem_v, w_out, ln1_g, ln1_b, w_router, router_bias, w_e_gate, w_e_up, w_e_down, w_s_gate, w_s_up, w_s_down, ln2_g, ln2_b):
    return _stages(x, mem, w_in, q_norm_g, kv_norm_g, w_uq, w_uk, w_uv, w_qidx, rel_bias, conv_w, w_mem_k, w_mem_v, w_out, ln1_g, ln1_b, w_router, router_bias, w_e_gate, w_e_up, w_e_down, w_s_gate, w_s_up, w_s_down, ln2_g, ln2_b)["out"]
```

```python
import functools
import math

import jax
import jax.numpy as jnp
from jax import lax
from jax.experimental import pallas as pl
from jax.experimental.pallas import tpu as pltpu
from jax.experimental.pallas import tpu_sc as plsc

N_HEADS_A = 8
HEAD_DIM = 64
Q_RANK = 256
KV_RANK = 128
N_IDX_HEADS = 8
IDX_DIM = 64
TOPK_MAX = 256
REL_BUCKETS = 32
REL_MAX_DIST = 128
CONV_CH = 256
CONV_WIDTH = 3
N_MEM_HEADS = 4
MIX_A = N_HEADS_A * HEAD_DIM
MIX_C = N_MEM_HEADS * HEAD_DIM
N_EXPERTS = 64
N_GROUPS = 8
GROUP_SIZE = N_EXPERTS // N_GROUPS
TOPK_GROUPS = 4
TOP_K = 8
D_EXPERT = 256
ROUTED_SCALE = 2.5
DEPTH = 1
ALPHA = (2.0 * DEPTH) ** 0.25
LN_EPS = 1e-5
RMS_EPS = 1e-6
LOG2_E = math.log2(math.e)

LANES = 128
SUBLANES = 8
QB = 128
F32_LOWEST = -3.4028234663852886e38
VMEM_LIMIT = 56 * 1024 * 1024
MXU_DTYPE = jnp.bfloat16
ROW_BLOCK = 1024

_NT = (((1,), (1,)), ((), ()))


def _dot(a, b):
    return jnp.dot(a, b, preferred_element_type=jnp.float32)


def _dot_nt(a, b):
    return lax.dot_general(a, b, _NT, preferred_element_type=jnp.float32)


def _cparams(sem):
    return pltpu.CompilerParams(dimension_semantics=sem, vmem_limit_bytes=VMEM_LIMIT)


def _bias_kernel(rb_ref, o_ref):
    s = lax.broadcasted_iota(jnp.int32, (QB, QB), 0)
    t = lax.broadcasted_iota(jnp.int32, (QB, QB), 1)
    max_exact = REL_BUCKETS // 2
    for tile in range(3):
        n = jnp.maximum(t - s + (2 - tile) * QB, 0)
        nf = jnp.maximum(n.astype(jnp.float32), 1.0)
        large = max_exact + (jnp.log(nf / max_exact) / math.log(REL_MAX_DIST / max_exact)
                             * (REL_BUCKETS - max_exact)).astype(jnp.int32)
        large = jnp.minimum(large, REL_BUCKETS - 1)
        bucket = jnp.where(n < max_exact, n, large)
        for h in range(N_HEADS_A):
            acc = jnp.zeros((QB, QB), jnp.float32)
            for b in range(REL_BUCKETS):
                acc = jnp.where(bucket == b, rb_ref[b, h], acc)
            o_ref[tile, h] = acc * LOG2_E


def _bias_tiles(rel_bias):
    return pl.pallas_call(
        _bias_kernel,
        in_specs=[pl.BlockSpec(memory_space=pltpu.SMEM)],
        out_specs=pl.BlockSpec(memory_space=pltpu.VMEM),
        out_shape=jax.ShapeDtypeStruct((3, N_HEADS_A, QB, QB), jnp.float32),
        name="bias_tiles",
    )(rel_bias)


def _proj_kernel(x_ref, mem_ref, wm_ref, ws_ref, qg_ref, kvg_ref, cw_ref, wmk_ref, wmv_ref,
                 cq_ref, ckv_ref, ckvt_ref, kidx_ref, iwt_ref, yb_ref, yc_ref,
                 carry_ref, mk_ref, mv_ref, *, tm):
    si = pl.program_id(1)

    @pl.when(si == 0)
    def _():
        carry_ref[...] = jnp.zeros_like(carry_ref)
        mb = mem_ref[0].astype(MXU_DTYPE)
        mk_ref[...] = _dot(mb, wmk_ref[...]).astype(MXU_DTYPE)
        mv_ref[...] = _dot(mb, wmv_ref[...]).astype(MXU_DTYPE)

    xb = x_ref[...].astype(MXU_DTYPE)
    p = _dot(xb, wm_ref[...])
    small = _dot(xb, ws_ref[...])

    o = 0
    cq = p[:, o:o + Q_RANK]; o += Q_RANK
    ckv = p[:, o:o + KV_RANK]; o += KV_RANK
    g_b = p[:, o:o + CONV_CH]; o += CONV_CH
    g_c = p[:, o:o + CONV_CH]; o += CONV_CH
    h_c = p[:, o:o + CONV_CH]; o += CONV_CH
    q_mem = p[:, o:o + MIX_C]

    cq = cq * lax.rsqrt(jnp.mean(cq * cq, axis=-1, keepdims=True) + RMS_EPS) * qg_ref[...]
    ckv = ckv * lax.rsqrt(jnp.mean(ckv * ckv, axis=-1, keepdims=True) + RMS_EPS) * kvg_ref[...]
    cq_ref[...] = cq.astype(MXU_DTYPE)
    ckv_b = ckv.astype(MXU_DTYPE)
    ckv_ref[...] = ckv_b
    ckvt_ref[0] = ckv.T.astype(MXU_DTYPE)

    kidx_ref[...] = small[:, :IDX_DIM].astype(MXU_DTYPE)
    small_t = small.T
    iwt_ref[0] = small_t[IDX_DIM:IDX_DIM + N_IDX_HEADS, :] * (N_IDX_HEADS ** -0.5 * IDX_DIM ** -0.5)

    u = g_c * h_c
    rows = lax.broadcasted_iota(jnp.int32, (tm, 1), 0)
    c6 = carry_ref[SUBLANES - 2:SUBLANES - 1, :]
    c7 = carry_ref[SUBLANES - 1:SUBLANES, :]
    u1 = jnp.where(rows == 0, c7, pltpu.roll(u, 1, 0))
    u2 = jnp.where(rows == 0, c6, jnp.where(rows == 1, c7, pltpu.roll(u, 2, 0)))
    y = cw_ref[0:1, :] * u2
    y = y + cw_ref[1:2, :] * u1
    y = y + cw_ref[2:3, :] * u
    yb_ref[...] = (g_b * y).astype(MXU_DTYPE)
    carry_ref[...] = u[tm - SUBLANES:, :]

    qm = q_mem.astype(MXU_DTYPE)
    outs = []
    for h in range(N_MEM_HEADS):
        sl = slice(h * HEAD_DIM, (h + 1) * HEAD_DIM)
        lg = _dot_nt(qm[:, sl], mk_ref[:, sl]) * (HEAD_DIM ** -0.5)
        lg = lg - jnp.max(lg, axis=-1, keepdims=True)
        e = jnp.exp(lg)
        pr = e / jnp.sum(e, axis=-1, keepdims=True)
        outs.append(_dot(pr.astype(MXU_DTYPE), mv_ref[:, sl]))
    yc_ref[...] = jnp.concatenate(outs, axis=-1).astype(MXU_DTYPE)


def _proj(x2, mem, w_main, w_small, q_g, kv_g, conv_w, w_mk, w_mv, B, S, tm):
    T, D = x2.shape
    n_mem = mem.shape[1]
    ns = S // tm
    row = lambda b, s: (b * ns + s, 0)
    const2 = lambda b, s: (0, 0)
    bf = MXU_DTYPE
    return pl.pallas_call(
        functools.partial(_proj_kernel, tm=tm),
        grid=(B, ns),
        in_specs=[
            pl.BlockSpec((tm, D), row),
            pl.BlockSpec((1, n_mem, D), lambda b, s: (b, 0, 0)),
            pl.BlockSpec(w_main.shape, const2),
            pl.BlockSpec(w_small.shape, const2),
            pl.BlockSpec(q_g.shape, const2),
            pl.BlockSpec(kv_g.shape, const2),
            pl.BlockSpec(conv_w.shape, const2),
            pl.BlockSpec(w_mk.shape, const2),
            pl.BlockSpec(w_mv.shape, const2),
        ],
        out_specs=[
            pl.BlockSpec((tm, Q_RANK), row),
            pl.BlockSpec((tm, KV_RANK), row),
            pl.BlockSpec((1, KV_RANK, tm), lambda b, s: (b, 0, s)),
            pl.BlockSpec((tm, IDX_DIM), row),
            pl.BlockSpec((1, N_IDX_HEADS, tm), lambda b, s: (b, 0, s)),
            pl.BlockSpec((tm, CONV_CH), row),
            pl.BlockSpec((tm, MIX_C), row),
        ],
        out_shape=[
            jax.ShapeDtypeStruct((T, Q_RANK), bf),
            jax.ShapeDtypeStruct((T, KV_RANK), bf),
            jax.ShapeDtypeStruct((B, KV_RANK, S), bf),
            jax.ShapeDtypeStruct((T, IDX_DIM), bf),
            jax.ShapeDtypeStruct((B, N_IDX_HEADS, S), jnp.float32),
            jax.ShapeDtypeStruct((T, CONV_CH), bf),
            jax.ShapeDtypeStruct((T, MIX_C), bf),
        ],
        scratch_shapes=[
            pltpu.VMEM((SUBLANES, CONV_CH), jnp.float32),
            pltpu.VMEM((n_mem, MIX_C), bf),
            pltpu.VMEM((n_mem, MIX_C), bf),
        ],
        compiler_params=_cparams(("arbitrary", "arbitrary")),
        name="proj",
    )(x2, mem, w_main, w_small, q_g, kv_g, conv_w, w_mk, w_mv)


def _key_to_f32(key):
    bits = jnp.where(key < 0, key ^ jnp.int32(0x7FFFFFFF), key)
    return pltpu.bitcast(bits, jnp.float32)


def _colsum8(v):
    return jnp.sum(v.reshape(QB // SUBLANES, SUBLANES, QB), axis=0)


def _colmax8(v):
    return jnp.max(v.reshape(QB // SUBLANES, SUBLANES, QB), axis=0)


UNROLL_WIDTHS = (8, 4, 2, 1)


def _dsa_kernel(cq_ref, iwt_ref, kidx_ref, ckv_ref, ckvt_ref, wqi_ref, wuq_ref, wuk_ref, wuvt_ref,
                bias_ref, o_ref, wfold_ref, qidx_ref, qlat_ref, score_ref, mask_ref, logit_ref, acc_ref,
                *, k_sel, idx_bits):
    i = pl.program_id(1)
    f32 = jnp.float32
    bf = MXU_DTYPE
    n_blocks = i + 1
    n_blocks = n_blocks + jnp.where((n_blocks % 4 == 3) & (n_blocks < pl.num_programs(1)), 1, 0)
    s_loc = lax.broadcasted_iota(jnp.int32, (QB, QB), 0)
    t_glob = i * QB + lax.broadcasted_iota(jnp.int32, (QB, QB), 1)

    def blk(jb):
        return pl.multiple_of(jb * QB, QB)

    def block_loop(fn, init):
        c, start = init, 0
        for width in UNROLL_WIDTHS:
            n = (n_blocks - start) // width
            c = lax.fori_loop(0, n, lambda it, c, w=width, s=start: fn(s + it * w, w, c), c)
            start = start + n * width
        return c

    @pl.when(i == 0)
    def _():
        for h in range(N_HEADS_A):
            wfold_ref[:, h * KV_RANK:(h + 1) * KV_RANK] = (
                _dot_nt(wuq_ref[:, h * HEAD_DIM:(h + 1) * HEAD_DIM], wuk_ref[h])
                * (HEAD_DIM ** -0.5 * LOG2_E)).astype(bf)

    cq = cq_ref[...]
    q_idx = _dot(cq, wqi_ref[...]).astype(bf)
    q_lat = _dot(cq, wfold_ref[...]).astype(bf)
    for h in range(N_HEADS_A):
        qidx_ref[h * QB:(h + 1) * QB, :] = q_idx[:, h * IDX_DIM:(h + 1) * IDX_DIM]
        qlat_ref[h * QB:(h + 1) * QB, :] = q_lat[:, h * KV_RANK:(h + 1) * KV_RANK]
    iw = iwt_ref[0]

    def score_body(jb0, nb, c):
        d_blk = _dot_nt(kidx_ref[pl.ds(blk(jb0), nb * QB), :], qidx_ref[...])
        for sb in range(nb):
            off = blk(jb0 + sb)
            d_all = d_blk[sb * QB:(sb + 1) * QB, :]
            acc = jnp.maximum(d_all[:, 0:QB], 0.0) * iw[0:1, :]
            for h in range(1, N_IDX_HEADS):
                acc = acc + jnp.maximum(d_all[:, h * QB:(h + 1) * QB], 0.0) * iw[h:h + 1, :]
            score_ref[pl.ds(off, QB), :] = jnp.where(s_loc + off <= t_glob, acc + 0.0, F32_LOWEST)
        return c

    block_loop(score_body, 0)

    def count_where(pred):
        def body(jb0, nb, acc):
            for sb in range(nb):
                off = blk(jb0 + sb)
                acc = acc + _colsum8(jnp.where(pred(score_ref[pl.ds(off, QB), :], off), 1.0, 0.0))
            return acc
        acc = block_loop(body, jnp.zeros((SUBLANES, QB), f32))
        return jnp.sum(acc, axis=0, keepdims=True)

    kf = float(k_sel)

    def search():
        c0 = count_where(lambda sc, off: sc >= 0.0)
        cand0 = jnp.where(c0 >= kf, jnp.int32(0), jnp.int32(-2 ** 31))
        n_ge0 = jnp.where(c0 >= kf, c0, -1.0)

        def bit_body(it, carry):
            cand, n_ge = carry
            trial = cand + lax.shift_left(jnp.int32(1), 30 - it)
            tf = _key_to_f32(trial)
            cnt = count_where(lambda sc, off: sc >= tf)
            take = cnt >= kf
            return jnp.where(take, trial, cand), jnp.where(take, cnt, n_ge)

        cand, n_ge = lax.fori_loop(0, 31, bit_body, (cand0, n_ge0))
        thr = _key_to_f32(cand)
        keep_all_ties = jnp.full((1, QB), 2 ** idx_bits - 1, jnp.int32)

        def resolve_ties():
            n_gt = count_where(lambda sc, off: sc > thr)
            n_eq = count_where(lambda sc, off: sc == thr)
            need = kf - n_gt

            def tie_search():
                def tbody(it, xcut):
                    trial = xcut + lax.shift_left(jnp.int32(1), idx_bits - 1 - it)
                    cnt = count_where(lambda sc, off: (sc == thr) & (s_loc + off < trial))
                    return jnp.where(cnt < need, trial, xcut)
                return lax.fori_loop(0, idx_bits, tbody, jnp.zeros((1, QB), jnp.int32))

            return lax.cond(jnp.max(n_eq - need) > 0.0, tie_search, lambda: keep_all_ties)

        xcut = lax.cond(jnp.max(jnp.abs(n_ge - kf)) > 0.0, resolve_ties, lambda: keep_all_ties)
        return thr, xcut

    def no_search():
        return jnp.full((1, QB), F32_LOWEST, f32), jnp.full((1, QB), 2 ** idx_bits - 1, jnp.int32)

    thr, xcut = lax.cond((i + 1) * QB > k_sel, search, no_search)

    def mask_body(jb0, nb, c):
        for sb in range(nb):
            off = blk(jb0 + sb)
            sc = score_ref[pl.ds(off, QB), :]
            s_glob = s_loc + off
            keep = ((sc > thr) | ((sc == thr) & (s_glob <= xcut))) & (s_glob <= t_glob)
            mask_ref[pl.ds(off, QB), :] = jnp.where(keep, 0.0, -jnp.inf)
        return c

    block_loop(mask_body, 0)

    def p1_body(jb0, nb, m8):
        m8 = list(m8)
        lg_blk = _dot_nt(ckv_ref[pl.ds(blk(jb0), nb * QB), :], qlat_ref[...])
        for sb in range(nb):
            off = blk(jb0 + sb)
            lg = lg_blk[sb * QB:(sb + 1) * QB, :]
            msk = mask_ref[pl.ds(off, QB), :]
            bsel = jnp.clip(jb0 + sb - i + 2, 0, 2)
            for h in range(N_HEADS_A):
                lgh = lg[:, h * QB:(h + 1) * QB] + bias_ref[bsel, h] + msk
                logit_ref[pl.ds(off, QB), h * QB:(h + 1) * QB] = lgh
                m8[h] = jnp.maximum(m8[h], _colmax8(lgh))
        return tuple(m8)

    m8 = block_loop(p1_body, tuple(jnp.full((SUBLANES, QB), -jnp.inf, f32) for _ in range(N_HEADS_A)))
    m_row = [jnp.max(m, axis=0, keepdims=True) for m in m8]

    acc_ref[...] = jnp.zeros_like(acc_ref)

    def p2_body(jb0, nb, l8):
        l8 = list(l8)
        off = blk(jb0)
        rows = nb * QB
        ps = []
        for h in range(N_HEADS_A):
            p = jnp.exp2(logit_ref[pl.ds(off, rows), h * QB:(h + 1) * QB] - m_row[h])
            l8[h] = l8[h] + jnp.sum(p.reshape(rows // SUBLANES, SUBLANES, QB), axis=0)
            ps.append(p.astype(bf))
        acc_ref[...] += _dot(ckvt_ref[0, :, pl.ds(off, rows)], jnp.concatenate(ps, axis=1))
        return tuple(l8)

    l8 = block_loop(p2_body, tuple(jnp.zeros((SUBLANES, QB), f32) for _ in range(N_HEADS_A)))

    outs = []
    for h in range(N_HEADS_A):
        l_row = jnp.sum(l8[h], axis=0, keepdims=True)
        o_lat_t = (acc_ref[:, h * QB:(h + 1) * QB] / l_row).astype(bf)
        outs.append(_dot(wuvt_ref[h], o_lat_t))
    o_ref[...] = jnp.concatenate(outs, axis=0).T.astype(o_ref.dtype)


def _dsa(cq, iwt, kidx, ckv, ckvt, w_qidx, w_uq, w_uk_h, w_uvt_h, bias_tiles, B, S):
    T = cq.shape[0]
    assert S % QB == 0 and QB >= REL_MAX_DIST
    nq = S // QB
    k_sel = min(TOPK_MAX, S // 4)
    idx_bits = max(1, (S - 1).bit_length())
    c2 = lambda b, i: (0, 0)
    c3 = lambda b, i: (0, 0, 0)
    return pl.pallas_call(
        functools.partial(_dsa_kernel, k_sel=k_sel, idx_bits=idx_bits),
        grid=(B, nq),
        in_specs=[
            pl.BlockSpec((QB, Q_RANK), lambda b, i: (b * nq + i, 0)),
            pl.BlockSpec((1, N_IDX_HEADS, QB), lambda b, i: (b, 0, i)),
            pl.BlockSpec((S, IDX_DIM), lambda b, i: (b, 0)),
            pl.BlockSpec((S, KV_RANK), lambda b, i: (b, 0)),
            pl.BlockSpec((1, KV_RANK, S), lambda b, i: (b, 0, 0)),
            pl.BlockSpec(w_qidx.shape, c2),
            pl.BlockSpec(w_uq.shape, c2),
            pl.BlockSpec(w_uk_h.shape, c3),
            pl.BlockSpec(w_uvt_h.shape, c3),
            pl.BlockSpec(bias_tiles.shape, lambda b, i: (0, 0, 0, 0)),
        ],
        out_specs=pl.BlockSpec((QB, MIX_A), lambda b, i: (b * nq + i, 0)),
        out_shape=jax.ShapeDtypeStruct((T, MIX_A), MXU_DTYPE),
        scratch_shapes=[
            pltpu.VMEM((Q_RANK, N_HEADS_A * KV_RANK), MXU_DTYPE),
            pltpu.VMEM((N_IDX_HEADS * QB, IDX_DIM), MXU_DTYPE),
            pltpu.VMEM((N_HEADS_A * QB, KV_RANK), MXU_DTYPE),
            pltpu.VMEM((S, QB), jnp.float32),
            pltpu.VMEM((S, QB), jnp.float32),
            pltpu.VMEM((S, N_HEADS_A * QB), jnp.float32),
            pltpu.VMEM((KV_RANK, N_HEADS_A * QB), jnp.float32),
        ],
        compiler_params=_cparams(("arbitrary", "arbitrary")),
        name="dsa",
    )(cq, iwt, kidx, ckv, ckvt, w_qidx, w_uq, w_uk_h, w_uvt_h, bias_tiles)


def _layer_norm(xf, g, b):
    mu = jnp.mean(xf, axis=-1, keepdims=True)
    xc = xf - mu
    var = jnp.mean(xc * xc, axis=-1, keepdims=True)
    return xc * lax.rsqrt(var + LN_EPS) * g + b


def _rank_rows(v, n):
    ri = lax.broadcasted_iota(jnp.int32, v.shape, 0)
    rank = jnp.zeros(v.shape, jnp.float32)
    for r2 in range(n):
        row = v[r2:r2 + 1, :]
        beats = (row > v) | ((row == v) & (ri > r2))
        rank = rank + jnp.where(beats, 1.0, 0.0)
    return rank


def _top_rows(v, k):
    n = v.shape[0]
    ri = lax.broadcasted_iota(jnp.int32, v.shape, 0)
    sel = jnp.zeros(v.shape, jnp.float32)
    for _ in range(k):
        m = jnp.max(v, axis=0, keepdims=True)
        first = jnp.min(jnp.where(v == m, ri, n), axis=0, keepdims=True)
        pick = ri == first
        sel = jnp.where(pick, 1.0, sel)
        v = jnp.where(pick, -jnp.inf, v)
    return sel > 0.5


def _pack_factor():
    return 4 // jnp.dtype(MXU_DTYPE).itemsize


def _pack_rows(x):
    if _pack_factor() == 1:
        return pltpu.bitcast(x, jnp.int32)
    half = x.shape[1] // 2
    b = pltpu.bitcast(x.astype(MXU_DTYPE).astype(jnp.float32), jnp.int32)
    return b[:, half:] | lax.shift_right_logical(b[:, :half], jnp.int32(16))


_HIGH_HALF = -(1 << 16)


def _unpack_rows_f32(p):
    if _pack_factor() == 1:
        return [pltpu.bitcast(p, jnp.float32)]
    lo = pltpu.bitcast(lax.shift_left(p, jnp.int32(16)), jnp.float32)
    hi = pltpu.bitcast(p & jnp.int32(_HIGH_HALF), jnp.float32)
    return [lo, hi]


def _unpack_rows(p):
    return [v.astype(MXU_DTYPE) for v in _unpack_rows_f32(p)]


def _mix_router_kernel(x_ref, ya_ref, yb_ref, yc_ref, wo_ref, g_ref, b_ref, wrt_ref, rb_ref, exp_ref,
                       x1_ref, x1p_ref, sel_ref, w_ref, pos_ref, cnt_ref, base_ref, *, tm):
    step = pl.program_id(0)
    f32 = jnp.float32

    @pl.when(step == 0)
    def _():
        base_ref[...] = jnp.zeros_like(base_ref)

    mix = _dot(ya_ref[...], wo_ref[0:MIX_A, :])
    mix = mix + _dot(yb_ref[...], wo_ref[MIX_A:MIX_A + CONV_CH, :])
    mix = mix + _dot(yc_ref[...], wo_ref[MIX_A + CONV_CH:, :])
    x1 = _layer_norm(ALPHA * x_ref[...] + mix, g_ref[...], b_ref[...])
    x1_ref[...] = x1
    x1p_ref[...] = _pack_rows(x1)

    lg = lax.dot_general(wrt_ref[...], x1, _NT, precision=lax.Precision.HIGHEST, preferred_element_type=f32)
    s = 1.0 / (1.0 + jnp.exp(-lg))
    sc = s + rb_ref[...]

    g3 = sc.reshape(N_GROUPS, GROUP_SIZE, tm)
    m1 = jnp.max(g3, axis=1, keepdims=True)
    is_m1 = g3 == m1
    n_m1 = jnp.sum(jnp.where(is_m1, 1.0, 0.0), axis=1, keepdims=True)
    m2 = jnp.max(jnp.where(is_m1, -jnp.inf, g3), axis=1, keepdims=True)
    gscore = (m1 + jnp.where(n_m1 > 1.0, m1, m2)).reshape(N_GROUPS, tm)
    gsel = jnp.where(_rank_rows(gscore, N_GROUPS) < float(TOPK_GROUPS), 1.0, 0.0)
    emask = _dot(exp_ref[...], gsel.astype(MXU_DTYPE)) > 0.5
    masked = jnp.where(emask, sc, -jnp.inf)
    sel = _top_rows(masked, TOP_K) & emask
    self_ = jnp.where(sel, 1.0, 0.0)
    top_s = jnp.where(sel, s, 0.0)
    w = top_s / jnp.sum(top_s, axis=0, keepdims=True) * ROUTED_SCALE

    t_r = lax.broadcasted_iota(jnp.int32, (tm, tm), 0)
    t_c = lax.broadcasted_iota(jnp.int32, (tm, tm), 1)
    upper = jnp.where(t_r < t_c, 1.0, 0.0).astype(MXU_DTYPE)
    pref = _dot(self_.astype(MXU_DTYPE), upper)
    base = base_ref[...]
    sel_ref[...] = self_
    w_ref[...] = w
    pos_ref[...] = base + pref
    base = base + jnp.sum(self_, axis=1, keepdims=True)
    base_ref[...] = base
    cnt_ref[...] = jnp.broadcast_to(base, cnt_ref.shape)


def _mix_router(x2, ya, yb, yc, w_out, ln_g, ln_b, w_router_t, router_bias, tm):
    T, D = x2.shape
    E = N_EXPERTS
    expand = (jnp.arange(E)[:, None] // GROUP_SIZE == jnp.arange(N_GROUPS)[None, :]).astype(MXU_DTYPE)
    row = lambda i: (i, 0)
    col = lambda i: (0, i)
    c2 = lambda i: (0, 0)
    f32 = jnp.float32
    return pl.pallas_call(
        functools.partial(_mix_router_kernel, tm=tm),
        grid=(T // tm,),
        in_specs=[
            pl.BlockSpec((tm, D), row),
            pl.BlockSpec((tm, MIX_A), row),
            pl.BlockSpec((tm, CONV_CH), row),
            pl.BlockSpec((tm, MIX_C), row),
            pl.BlockSpec(w_out.shape, c2),
            pl.BlockSpec((1, D), c2),
            pl.BlockSpec((1, D), c2),
            pl.BlockSpec((E, D), c2),
            pl.BlockSpec((E, 1), c2),
            pl.BlockSpec((E, N_GROUPS), c2),
        ],
        out_specs=[
            pl.BlockSpec((tm, D), row),
            pl.BlockSpec((tm, D // _pack_factor()), row),
            pl.BlockSpec((E, tm), col),
            pl.BlockSpec((E, tm), col),
            pl.BlockSpec((E, tm), col),
            pl.BlockSpec((E, LANES), c2),
        ],
        out_shape=[
            jax.ShapeDtypeStruct((T, D), f32),
            jax.ShapeDtypeStruct((T, D // _pack_factor()), jnp.int32),
            jax.ShapeDtypeStruct((E, T), f32),
            jax.ShapeDtypeStruct((E, T), f32),
            jax.ShapeDtypeStruct((E, T), f32),
            jax.ShapeDtypeStruct((E, LANES), f32),
        ],
        scratch_shapes=[pltpu.VMEM((E, 1), f32)],
        compiler_params=_cparams(("arbitrary",)),
        name="mix_router",
    )(x2, ya, yb, yc, w_out, ln_g, ln_b, w_router_t, router_bias, expand)


def _compact_kernel(sel_ref, w_ref, pos_ref, pstart_ref, low_ref, dest_ref, wk_ref):
    sel = sel_ref[...]
    on = sel > 0.5
    rank = _dot(low_ref[...], sel.astype(MXU_DTYPE))
    row = pstart_ref[...] + pos_ref[...]
    w = w_ref[...]
    dests, ws = [], []
    for k in range(TOP_K):
        m = on & (rank == float(k))
        dests.append(jnp.sum(jnp.where(m, row, 0.0), axis=0, keepdims=True))
        ws.append(jnp.sum(jnp.where(m, w, 0.0), axis=0, keepdims=True))
    dest_ref[...] = jnp.concatenate(dests, axis=0).astype(jnp.int32)
    wk_ref[...] = jnp.concatenate(ws, axis=0)


def _compact(sel_t, w_t, pos_t, pad_start, tm):
    E, T = sel_t.shape
    lower = (jnp.arange(E)[None, :] < jnp.arange(E)[:, None]).astype(MXU_DTYPE)
    col = lambda i: (0, i)
    c2 = lambda i: (0, 0)
    return pl.pallas_call(
        _compact_kernel,
        grid=(T // tm,),
        in_specs=[pl.BlockSpec((E, tm), col), pl.BlockSpec((E, tm), col), pl.BlockSpec((E, tm), col),
                  pl.BlockSpec((E, 1), c2), pl.BlockSpec((E, E), c2)],
        out_specs=[pl.BlockSpec((TOP_K, tm), col), pl.BlockSpec((TOP_K, tm), col)],
        out_shape=[jax.ShapeDtypeStruct((TOP_K, T), jnp.int32), jax.ShapeDtypeStruct((TOP_K, T), jnp.float32)],
        compiler_params=_cparams(("arbitrary",)),
        name="route_compact",
    )(sel_t, w_t, pos_t, pad_start, lower)


def _silu(g):
    return g / (1.0 + jnp.exp(-g))


def _expert_kernel(be_ref, nv_ref, nu_ref, xs_ref, wg_ref, wu_ref, wd_ref, ys_ref, wgb_ref, wub_ref, wdb_ref):
    i = pl.program_id(0)

    @pl.when((i == 0) | (be_ref[i] != be_ref[jnp.maximum(i - 1, 0)]))
    def _():
        wgb_ref[...] = wg_ref[0].astype(MXU_DTYPE)
        wub_ref[...] = wu_ref[0].astype(MXU_DTYPE)
        wdb_ref[...] = wd_ref[0].astype(MXU_DTYPE)

    @pl.when(i < nu_ref[0])
    def _():
        live = lax.broadcasted_iota(jnp.int32, (ROW_BLOCK, 1), 0) < nv_ref[i]
        parts = [jnp.where(live, v, jnp.zeros_like(v)) for v in _unpack_rows(xs_ref[...])]
        dk = wgb_ref.shape[0] // len(parts)

        def proj(w_ref):
            acc = _dot(parts[0], w_ref[0:dk, :])
            for n in range(1, len(parts)):
                acc = acc + _dot(parts[n], w_ref[n * dk:(n + 1) * dk, :])
            return acc

        a = (_silu(proj(wgb_ref)) * proj(wub_ref)).astype(MXU_DTYPE)
        ys_ref[...] = _pack_rows(_dot(a, wdb_ref[...]))


def _experts(xs, block_e, block_valid, n_used, w_gate, w_up, w_down):
    n_rows, W = xs.shape
    D = w_gate.shape[1]
    n_blocks = n_rows // ROW_BLOCK
    blk = lambda i, be, nv, nu: (jnp.minimum(i, nu[0] - 1), 0)
    wsel = lambda i, be, nv, nu: (be[i], 0, 0)
    return pl.pallas_call(
        _expert_kernel,
        grid_spec=pltpu.PrefetchScalarGridSpec(
            num_scalar_prefetch=3,
            grid=(n_blocks,),
            in_specs=[
                pl.BlockSpec((ROW_BLOCK, W), blk),
                pl.BlockSpec((1, D, D_EXPERT), wsel),
                pl.BlockSpec((1, D, D_EXPERT), wsel),
                pl.BlockSpec((1, D_EXPERT, D), wsel),
            ],
            out_specs=pl.BlockSpec((ROW_BLOCK, W), blk),
            scratch_shapes=[pltpu.VMEM((D, D_EXPERT), MXU_DTYPE), pltpu.VMEM((D, D_EXPERT), MXU_DTYPE),
                            pltpu.VMEM((D_EXPERT, D), MXU_DTYPE)],
        ),
        out_shape=jax.ShapeDtypeStruct((n_rows, W), xs.dtype),
        compiler_params=_cparams(("arbitrary",)),
        name="experts",
    )(block_e, block_valid, n_used, xs, w_gate, w_up, w_down)


SC_CORES = 2
SC_SUBCORES = 16
SC_GATHER_ROWS = 64
COMBINE_CHUNKS = 8


def _sc_gather_rows(table, idx):
    n = idx.shape[0]
    w = table.shape[1]
    n_workers = SC_CORES * SC_SUBCORES
    per_worker = n // n_workers
    assert n % n_workers == 0 and per_worker % SC_GATHER_ROWS == 0
    mesh = plsc.VectorSubcoreMesh(core_axis_name="c", subcore_axis_name="s")

    @functools.partial(
        pl.kernel, mesh=mesh,
        out_type=jax.ShapeDtypeStruct((n, w), table.dtype),
        scratch_types=[
            pltpu.VMEM((2, SC_GATHER_ROWS), jnp.int32),
            pltpu.VMEM((2, SC_GATHER_ROWS, w), table.dtype),
            pltpu.SemaphoreType.DMA((2,)),
        ],
        name="sc_gather_rows",
    )
    def gather(table_hbm, idx_hbm, out_hbm, idx_v, rows_v, sem):
        wid = lax.axis_index("s") * SC_CORES + lax.axis_index("c")
        base = wid * per_worker
        n_steps = per_worker // SC_GATHER_ROWS

        def gather_copy(slot):
            return pltpu.make_async_copy(table_hbm.at[idx_v.at[slot]], rows_v.at[slot], sem.at[slot])

        def start(step, slot):
            pltpu.sync_copy(idx_hbm.at[pl.ds(base + step * SC_GATHER_ROWS, SC_GATHER_ROWS)], idx_v.at[slot])
            gather_copy(slot).start()

        start(0, 0)

        @pl.loop(0, n_steps, step=2)
        def _(g):
            for slot in range(2):
                step = g + slot

                @pl.when(step + 1 < n_steps)
                def _():
                    start(step + 1, 1 - slot)

                gather_copy(slot).wait()
                pltpu.sync_copy(rows_v.at[slot], out_hbm.at[pl.ds(base + step * SC_GATHER_ROWS, SC_GATHER_ROWS)])

    return gather(table, idx)


SC_SCATTER_ROWS = 64


def _sc_scatter_rows(rows, idx3, n_out):
    n_src, w = rows.shape
    n_chunks, n_dst, batch = idx3.shape
    n_workers = SC_CORES * SC_SUBCORES
    assert batch == SC_SCATTER_ROWS and n_chunks * batch == n_src and n_chunks % (2 * n_workers) == 0
    per_worker = n_chunks // n_workers
    mesh = plsc.VectorSubcoreMesh(core_axis_name="c", subcore_axis_name="s")

    @functools.partial(
        pl.kernel, mesh=mesh,
        out_type=jax.ShapeDtypeStruct((n_out, w), rows.dtype),
        scratch_types=[
            pltpu.VMEM((2, n_dst, batch), jnp.int32),
            pltpu.VMEM((2, batch, w), rows.dtype),
            pltpu.SemaphoreType.DMA((2,)),
            pltpu.SemaphoreType.DMA,
        ],
        name="sc_scatter_rows",
    )
    def scatter(rows_hbm, idx_hbm, out_hbm, idx_v, rows_v, load_sem, store_sem):
        wid = lax.axis_index("s") * SC_CORES + lax.axis_index("c")

        def load_copy(step, slot):
            c = wid * per_worker + step
            return pltpu.make_async_copy(rows_hbm.at[pl.ds(c * batch, batch)], rows_v.at[slot], load_sem.at[slot])

        def load(step, slot):
            pltpu.sync_copy(idx_hbm.at[wid * per_worker + step], idx_v.at[slot])
            load_copy(step, slot).start()

        def store_copy(slot, k):
            return pltpu.make_async_copy(rows_v.at[slot], out_hbm.at[idx_v.at[slot].at[k]], store_sem)

        load(0, 0)

        @pl.loop(0, per_worker, step=2)
        def _(g):
            for slot in range(2):
                step = g + slot

                @pl.when(step + 1 < per_worker)
                def _():
                    load(step + 1, 1 - slot)

                load_copy(step, slot).wait()
                for k in range(n_dst):
                    store_copy(slot, k).start()
                for k in range(n_dst):
                    store_copy(slot, k).wait()

    return scatter(rows, idx3)


def _combine2_kernel(wk_ref, x1_ref, g_ref_rows, wsg_ref, wsu_ref, wsd_ref, g_ref, b_ref, o_ref):
    x1 = x1_ref[...]
    xb = x1.astype(MXU_DTYPE)
    a = (_silu(_dot(xb, wsg_ref[...])) * _dot(xb, wsu_ref[...])).astype(MXU_DTYPE)
    shared = _dot(a, wsd_ref[...])
    wk = wk_ref[...].T
    groups = [wk[:, 0:1] * v for v in _unpack_rows_f32(g_ref_rows[0])]
    for k in range(1, TOP_K):
        groups = [g + wk[:, k:k + 1] * v for g, v in zip(groups, _unpack_rows_f32(g_ref_rows[k]))]
    routed = jnp.concatenate(groups, axis=1)
    o_ref[...] = _layer_norm(ALPHA * x1 + (routed + shared), g_ref[...], b_ref[...])


def _combine2_kernel_into(wk_ref, x1_ref, g_ref_rows, wsg_ref, wsu_ref, wsd_ref, g_ref, b_ref, prev_ref, o_ref):
    del prev_ref
    _combine2_kernel(wk_ref, x1_ref, g_ref_rows, wsg_ref, wsu_ref, wsd_ref, g_ref, b_ref, o_ref)


def _combine2(wk_t, x1, gathered, w_sg, w_su, w_sd, ln_g, ln_b, tc, chunk, prev):
    T, D = x1.shape
    _, t_chunk, W = gathered.shape
    base = chunk * (t_chunk // tc)
    row = lambda i: (base + i, 0)
    c2 = lambda i: (0, 0)
    in_specs = [
        pl.BlockSpec((TOP_K, tc), lambda i: (0, base + i)),
        pl.BlockSpec((tc, D), row),
        pl.BlockSpec((TOP_K, tc, W), lambda i: (0, i, 0)),
        pl.BlockSpec(w_sg.shape, c2),
        pl.BlockSpec(w_su.shape, c2),
        pl.BlockSpec(w_sd.shape, c2),
        pl.BlockSpec((1, D), c2),
        pl.BlockSpec((1, D), c2),
    ]
    args = [wk_t, x1, gathered, w_sg, w_su, w_sd, ln_g, ln_b]
    if prev is None:
        body, aliases = _combine2_kernel, {}
    else:
        body, aliases = _combine2_kernel_into, {len(args): 0}
        in_specs.append(pl.BlockSpec(memory_space=pl.ANY))
        args.append(prev)
    return pl.pallas_call(
        body,
        grid=(t_chunk // tc,),
        in_specs=in_specs,
        out_specs=pl.BlockSpec((tc, D), row),
        out_shape=jax.ShapeDtypeStruct((T, D), jnp.float32),
        input_output_aliases=aliases,
        compiler_params=_cparams(("arbitrary",)),
        name="combine",
    )(*args)


def _split_w_in(w_in):
    bf = MXU_DTYPE
    o_kv = Q_RANK
    o_ki = o_kv + KV_RANK
    o_iw = o_ki + IDX_DIM
    o_rest = o_iw + N_IDX_HEADS
    w_main = jnp.concatenate([w_in[:, :o_ki], w_in[:, o_rest:]], axis=1).astype(bf)
    w_small = jnp.pad(w_in[:, o_ki:o_rest], ((0, 0), (0, LANES - IDX_DIM - N_IDX_HEADS))).astype(bf)
    return w_main, w_small


def _stages(x, mem, w_in, q_norm_g, kv_norm_g, w_uq, w_uk, w_uv, w_qidx, rel_bias, conv_w, w_mem_k, w_mem_v, w_out, ln1_g, ln1_b, w_router, router_bias, w_e_gate, w_e_up, w_e_down, w_s_gate, w_s_up, w_s_down, ln2_g, ln2_b, upto=None):
    B, S, D = x.shape
    T = B * S
    bf = MXU_DTYPE
    l = 0
    res = {}
    x2 = x.reshape(T, D)
    w_main, w_small = _split_w_in(w_in[l])
    cq, ckv, ckvt, kidx, iwt, yb, yc = _proj(
        x2, mem, w_main, w_small, q_norm_g[l].reshape(1, -1), kv_norm_g[l].reshape(1, -1), conv_w[l],
        w_mem_k[l].astype(bf), w_mem_v[l].astype(bf), B, S, tm=min(512, S))
    res.update(c_q=cq, c_kv=ckv, k_idx=kidx, y_b=yb, y_c=yc,
               idx_w=jnp.swapaxes(iwt, 1, 2) / (N_IDX_HEADS ** -0.5 * IDX_DIM ** -0.5))
    if upto == "proj":
        return res
    bias_t = _bias_tiles(rel_bias)
    ya = _dsa(cq, iwt, kidx, ckv, ckvt,
              w_qidx[l].reshape(Q_RANK, -1).astype(bf), w_uq[l].reshape(Q_RANK, -1).astype(bf),
              jnp.transpose(w_uk[l], (1, 0, 2)).astype(bf), jnp.transpose(w_uv[l], (1, 2, 0)).astype(bf),
              bias_t, B, S)
    res.update(y_a=ya)
    if upto == "dsa":
        return res

    x1, x1p, sel_t, w_t, pos_t, cnt = _mix_router(
        x2, ya, yb, yc, w_out[l].astype(bf), ln1_g[l].reshape(1, -1), ln1_b[l].reshape(1, -1),
        w_router[l].T, router_bias[l].reshape(-1, 1), tm=min(512, T))
    res.update(x1=x1)

    counts = cnt[:, 0].astype(jnp.int32)
    padded = (counts + ROW_BLOCK - 1) // ROW_BLOCK * ROW_BLOCK
    pad_end = jnp.cumsum(padded)
    pad_start = pad_end - padded
    n_blocks = -(-(T * TOP_K) // ROW_BLOCK) + N_EXPERTS
    n_rows = n_blocks * ROW_BLOCK
    block_start = jnp.arange(n_blocks, dtype=jnp.int32) * ROW_BLOCK
    block_e = jnp.minimum(jnp.sum((pad_end[None, :] <= block_start[:, None]).astype(jnp.int32), axis=1),
                          N_EXPERTS - 1)
    n_used = (pad_end[-1:] // ROW_BLOCK).astype(jnp.int32)

    dest_t, wk_t = _compact(sel_t, w_t, pos_t, pad_start.astype(jnp.float32).reshape(-1, 1), tm=min(512, T))
    block_valid = jnp.clip((pad_start + counts)[block_e] - block_start, 0, ROW_BLOCK).astype(jnp.int32)
    bt = SC_SCATTER_ROWS
    idx3 = jnp.transpose(dest_t.reshape(TOP_K, T // bt, bt), (1, 0, 2))
    xs = _sc_scatter_rows(x1p, idx3, n_rows)
    ys = _experts(xs, block_e, block_valid, n_used, w_e_gate[l], w_e_up[l], w_e_down[l])
    n_chunks = COMBINE_CHUNKS if T % (COMBINE_CHUNKS * 256) == 0 else 1
    t_chunk = T // n_chunks
    out = None
    for c in range(n_chunks):
        idx_c = dest_t[:, c * t_chunk:(c + 1) * t_chunk].reshape(-1)
        gathered = _sc_gather_rows(ys, idx_c).reshape(TOP_K, t_chunk, -1)
        out = _combine2(wk_t, x1, gathered, w_s_gate[l].astype(bf), w_s_up[l].astype(bf), w_s_down[l].astype(bf),
                        ln2_g[l].reshape(1, -1), ln2_b[l].reshape(1, -1), tc=min(256, t_chunk), chunk=c, prev=out)
    res.update(out=out.reshape(B, S, D))
    return res


def kernel(x, mem, w_in, q_norm_g, kv_norm_g, w_uq, w_uk, w_uv, w_qidx, rel_bias, conv_w, w_mem_k, w_mem_v, w_out, ln1_g, ln1_b, w_router, router_bias, w_e_gate, w_e_up, w_e_down, w_s_gate, w_s_up, w_s_down, ln2_g, ln2_b):
    return _stages(x, mem, w_in, q_norm_g, kv_norm_g, w_uq, w_uk, w_uv, w_qidx, rel_bias, conv_w, w_mem_k, w_mem_v, w_out, ln1_g, ln1_b, w_router, router_bias, w_e_gate, w_e_up, w_e_down, w_s_gate, w_s_up, w_s_down, ln2_g, ln2_b)["out"]
```

```python
import functools
import math

import jax
import jax.numpy as jnp
from jax import lax
from jax.experimental import pallas as pl
from jax.experimental.pallas import tpu as pltpu
from jax.experimental.pallas import tpu_sc as plsc

N_HEADS_A = 8
HEAD_DIM = 64
Q_RANK = 256
KV_RANK = 128
N_IDX_HEADS = 8
IDX_DIM = 64
TOPK_MAX = 256
REL_BUCKETS = 32
REL_MAX_DIST = 128
CONV_CH = 256
CONV_WIDTH = 3
N_MEM_HEADS = 4
MIX_A = N_HEADS_A * HEAD_DIM
MIX_C = N_MEM_HEADS * HEAD_DIM
N_EXPERTS = 64
N_GROUPS = 8
GROUP_SIZE = N_EXPERTS // N_GROUPS
TOPK_GROUPS = 4
TOP_K = 8
D_EXPERT = 256
ROUTED_SCALE = 2.5
DEPTH = 1
ALPHA = (2.0 * DEPTH) ** 0.25
LN_EPS = 1e-5
RMS_EPS = 1e-6
LOG2_E = math.log2(math.e)

LANES = 128
SUBLANES = 8
QB = 128
F32_LOWEST = -3.4028234663852886e38
VMEM_LIMIT = 56 * 1024 * 1024
MXU_DTYPE = jnp.bfloat16
ROW_BLOCK = 1024

_NT = (((1,), (1,)), ((), ()))


def _dot(a, b):
    return jnp.dot(a, b, preferred_element_type=jnp.float32)


def _dot_nt(a, b):
    return lax.dot_general(a, b, _NT, preferred_element_type=jnp.float32)


def _cparams(sem):
    return pltpu.CompilerParams(dimension_semantics=sem, vmem_limit_bytes=VMEM_LIMIT)


def _bias_kernel(rb_ref, o_ref):
    s = lax.broadcasted_iota(jnp.int32, (QB, QB), 0)
    t = lax.broadcasted_iota(jnp.int32, (QB, QB), 1)
    max_exact = REL_BUCKETS // 2
    for tile in range(3):
        n = jnp.maximum(t - s + (2 - tile) * QB, 0)
        nf = jnp.maximum(n.astype(jnp.float32), 1.0)
        large = max_exact + (jnp.log(nf / max_exact) / math.log(REL_MAX_DIST / max_exact)
                             * (REL_BUCKETS - max_exact)).astype(jnp.int32)
        large = jnp.minimum(large, REL_BUCKETS - 1)
        bucket = jnp.where(n < max_exact, n, large)
        for h in range(N_HEADS_A):
            acc = jnp.zeros((QB, QB), jnp.float32)
            for b in range(REL_BUCKETS):
                acc = jnp.where(bucket == b, rb_ref[b, h], acc)
            o_ref[tile, h] = acc * LOG2_E


def _bias_tiles(rel_bias):
    return pl.pallas_call(
        _bias_kernel,
        in_specs=[pl.BlockSpec(memory_space=pltpu.SMEM)],
        out_specs=pl.BlockSpec(memory_space=pltpu.VMEM),
        out_shape=jax.ShapeDtypeStruct((3, N_HEADS_A, QB, QB), jnp.float32),
        name="bias_tiles",
    )(rel_bias)


def _proj_kernel(x_ref, mem_ref, wm_ref, ws_ref, qg_ref, kvg_ref, cw_ref, wmk_ref, wmv_ref,
                 cq_ref, ckv_ref, ckvt_ref, kidx_ref, iwt_ref, yb_ref, yc_ref,
                 carry_ref, mk_ref, mv_ref, *, tm):
    si = pl.program_id(1)

    @pl.when(si == 0)
    def _():
        carry_ref[...] = jnp.zeros_like(carry_ref)
        mb = mem_ref[0].astype(MXU_DTYPE)
        mk_ref[...] = _dot(mb, wmk_ref[...]).astype(MXU_DTYPE)
        mv_ref[...] = _dot(mb, wmv_ref[...]).astype(MXU_DTYPE)

    xb = x_ref[...].astype(MXU_DTYPE)
    p = _dot(xb, wm_ref[...])
    small = _dot(xb, ws_ref[...])

    o = 0
    cq = p[:, o:o + Q_RANK]; o += Q_RANK
    ckv = p[:, o:o + KV_RANK]; o += KV_RANK
    g_b = p[:, o:o + CONV_CH]; o += CONV_CH
    g_c = p[:, o:o + CONV_CH]; o += CONV_CH
    h_c = p[:, o:o + CONV_CH]; o += CONV_CH
    q_mem = p[:, o:o + MIX_C]

    cq = cq * lax.rsqrt(jnp.mean(cq * cq, axis=-1, keepdims=True) + RMS_EPS) * qg_ref[...]
    ckv = ckv * lax.rsqrt(jnp.mean(ckv * ckv, axis=-1, keepdims=True) + RMS_EPS) * kvg_ref[...]
    cq_ref[...] = cq.astype(MXU_DTYPE)
    ckv_b = ckv.astype(MXU_DTYPE)
    ckv_ref[...] = ckv_b
    ckvt_ref[0] = ckv.T.astype(MXU_DTYPE)

    kidx_ref[...] = small[:, :IDX_DIM].astype(MXU_DTYPE)
    small_t = small.T
    iwt_ref[0] = small_t[IDX_DIM:IDX_DIM + N_IDX_HEADS, :] * (N_IDX_HEADS ** -0.5 * IDX_DIM ** -0.5)

    u = g_c * h_c
    rows = lax.broadcasted_iota(jnp.int32, (tm, 1), 0)
    c6 = carry_ref[SUBLANES - 2:SUBLANES - 1, :]
    c7 = carry_ref[SUBLANES - 1:SUBLANES, :]
    u1 = jnp.where(rows == 0, c7, pltpu.roll(u, 1, 0))
    u2 = jnp.where(rows == 0, c6, jnp.where(rows == 1, c7, pltpu.roll(u, 2, 0)))
    y = cw_ref[0:1, :] * u2
    y = y + cw_ref[1:2, :] * u1
    y = y + cw_ref[2:3, :] * u
    yb_ref[...] = (g_b * y).astype(MXU_DTYPE)
    carry_ref[...] = u[tm - SUBLANES:, :]

    qm = q_mem.astype(MXU_DTYPE)
    outs = []
    for h in range(N_MEM_HEADS):
        sl = slice(h * HEAD_DIM, (h + 1) * HEAD_DIM)
        lg = _dot_nt(qm[:, sl], mk_ref[:, sl]) * (HEAD_DIM ** -0.5)
        lg = lg - jnp.max(lg, axis=-1, keepdims=True)
        e = jnp.exp(lg)
        pr = e / jnp.sum(e, axis=-1, keepdims=True)
        outs.append(_dot(pr.astype(MXU_DTYPE), mv_ref[:, sl]))
    yc_ref[...] = jnp.concatenate(outs, axis=-1).astype(MXU_DTYPE)


def _proj(x2, mem, w_main, w_small, q_g, kv_g, conv_w, w_mk, w_mv, B, S, tm):
    T, D = x2.shape
    n_mem = mem.shape[1]
    ns = S // tm
    row = lambda b, s: (b * ns + s, 0)
    const2 = lambda b, s: (0, 0)
    bf = MXU_DTYPE
    return pl.pallas_call(
        functools.partial(_proj_kernel, tm=tm),
        grid=(B, ns),
        in_specs=[
            pl.BlockSpec((tm, D), row),
            pl.BlockSpec((1, n_mem, D), lambda b, s: (b, 0, 0)),
            pl.BlockSpec(w_main.shape, const2),
            pl.BlockSpec(w_small.shape, const2),
            pl.BlockSpec(q_g.shape, const2),
            pl.BlockSpec(kv_g.shape, const2),
            pl.BlockSpec(conv_w.shape, const2),
            pl.BlockSpec(w_mk.shape, const2),
            pl.BlockSpec(w_mv.shape, const2),
        ],
        out_specs=[
            pl.BlockSpec((tm, Q_RANK), row),
            pl.BlockSpec((tm, KV_RANK), row),
            pl.BlockSpec((1, KV_RANK, tm), lambda b, s: (b, 0, s)),
            pl.BlockSpec((tm, IDX_DIM), row),
            pl.BlockSpec((1, N_IDX_HEADS, tm), lambda b, s: (b, 0, s)),
            pl.BlockSpec((tm, CONV_CH), row),
            pl.BlockSpec((tm, MIX_C), row),
        ],
        out_shape=[
            jax.ShapeDtypeStruct((T, Q_RANK), bf),
            jax.ShapeDtypeStruct((T, KV_RANK), bf),
            jax.ShapeDtypeStruct((B, KV_RANK, S), bf),
            jax.ShapeDtypeStruct((T, IDX_DIM), bf),
            jax.ShapeDtypeStruct((B, N_IDX_HEADS, S), jnp.float32),
            jax.ShapeDtypeStruct((T, CONV_CH), bf),
            jax.ShapeDtypeStruct((T, MIX_C), bf),
        ],
        scratch_shapes=[
            pltpu.VMEM((SUBLANES, CONV_CH), jnp.float32),
            pltpu.VMEM((n_mem, MIX_C), bf),
            pltpu.VMEM((n_mem, MIX_C), bf),
        ],
        compiler_params=_cparams(("arbitrary", "arbitrary")),
        name="proj",
    )(x2, mem, w_main, w_small, q_g, kv_g, conv_w, w_mk, w_mv)


def _key_to_f32(key):
    bits = jnp.where(key < 0, key ^ jnp.int32(0x7FFFFFFF), key)
    return pltpu.bitcast(bits, jnp.float32)


def _colsum8(v):
    return jnp.sum(v.reshape(QB // SUBLANES, SUBLANES, QB), axis=0)


def _colmax8(v):
    return jnp.max(v.reshape(QB // SUBLANES, SUBLANES, QB), axis=0)


UNROLL_WIDTHS = (8, 4, 2, 1)


def _dsa_kernel(cq_ref, iwt_ref, kidx_ref, ckv_ref, ckvt_ref, wqi_ref, wuq_ref, wuk_ref, wuvt_ref,
                bias_ref, o_ref, wfold_ref, qidx_ref, qlat_ref, score_ref, mask_ref, logit_ref, acc_ref,
                *, k_sel, idx_bits):
    i = pl.program_id(1)
    f32 = jnp.float32
    bf = MXU_DTYPE
    n_blocks = i + 1
    n_blocks = n_blocks + jnp.where((n_blocks % 4 == 3) & (n_blocks < pl.num_programs(1)), 1, 0)
    s_loc = lax.broadcasted_iota(jnp.int32, (QB, QB), 0)
    t_glob = i * QB + lax.broadcasted_iota(jnp.int32, (QB, QB), 1)

    def blk(jb):
        return pl.multiple_of(jb * QB, QB)

    def block_loop(fn, init):
        c, start = init, 0
        for width in UNROLL_WIDTHS:
            n = (n_blocks - start) // width
            c = lax.fori_loop(0, n, lambda it, c, w=width, s=start: fn(s + it * w, w, c), c)
            start = start + n * width
        return c

    @pl.when(i == 0)
    def _():
        for h in range(N_HEADS_A):
            wfold_ref[:, h * KV_RANK:(h + 1) * KV_RANK] = (
                _dot_nt(wuq_ref[:, h * HEAD_DIM:(h + 1) * HEAD_DIM], wuk_ref[h])
                * (HEAD_DIM ** -0.5 * LOG2_E)).astype(bf)

    cq = cq_ref[...]
    q_idx = _dot(cq, wqi_ref[...]).astype(bf)
    q_lat = _dot(cq, wfold_ref[...]).astype(bf)
    for h in range(N_HEADS_A):
        qidx_ref[h * QB:(h + 1) * QB, :] = q_idx[:, h * IDX_DIM:(h + 1) * IDX_DIM]
        qlat_ref[h * QB:(h + 1) * QB, :] = q_lat[:, h * KV_RANK:(h + 1) * KV_RANK]
    iw = iwt_ref[0]

    def score_body(jb0, nb, c):
        d_blk = _dot_nt(kidx_ref[pl.ds(blk(jb0), nb * QB), :], qidx_ref[...])
        for sb in range(nb):
            off = blk(jb0 + sb)
            d_all = d_blk[sb * QB:(sb + 1) * QB, :]
            acc = jnp.maximum(d_all[:, 0:QB], 0.0) * iw[0:1, :]
            for h in range(1, N_IDX_HEADS):
                acc = acc + jnp.maximum(d_all[:, h * QB:(h + 1) * QB], 0.0) * iw[h:h + 1, :]
            score_ref[pl.ds(off, QB), :] = jnp.where(s_loc + off <= t_glob, acc + 0.0, F32_LOWEST)
        return c

    block_loop(score_body, 0)

    def count_where(pred):
        def body(jb0, nb, acc):
            for sb in range(nb):
                off = blk(jb0 + sb)
                acc = acc + _colsum8(jnp.where(pred(score_ref[pl.ds(off, QB), :], off), 1.0, 0.0))
            return acc
        acc = block_loop(body, jnp.zeros((SUBLANES, QB), f32))
        return jnp.sum(acc, axis=0, keepdims=True)

    kf = float(k_sel)

    def search():
        c0 = count_where(lambda sc, off: sc >= 0.0)
        cand0 = jnp.where(c0 >= kf, jnp.int32(0), jnp.int32(-2 ** 31))
        n_ge0 = jnp.where(c0 >= kf, c0, -1.0)

        def bit_body(it, carry):
            cand, n_ge = carry
            trial = cand + lax.shift_left(jnp.int32(1), 30 - it)
            tf = _key_to_f32(trial)
            cnt = count_where(lambda sc, off: sc >= tf)
            take = cnt >= kf
            return jnp.where(take, trial, cand), jnp.where(take, cnt, n_ge)

        cand, n_ge = lax.fori_loop(0, 31, bit_body, (cand0, n_ge0))
        thr = _key_to_f32(cand)
        keep_all_ties = jnp.full((1, QB), 2 ** idx_bits - 1, jnp.int32)

        def resolve_ties():
            n_gt = count_where(lambda sc, off: sc > thr)
            n_eq = count_where(lambda sc, off: sc == thr)
            need = kf - n_gt

            def tie_search():
                def tbody(it, xcut):
                    trial = xcut + lax.shift_left(jnp.int32(1), idx_bits - 1 - it)
                    cnt = count_where(lambda sc, off: (sc == thr) & (s_loc + off < trial))
                    return jnp.where(cnt < need, trial, xcut)
                return lax.fori_loop(0, idx_bits, tbody, jnp.zeros((1, QB), jnp.int32))

            return lax.cond(jnp.max(n_eq - need) > 0.0, tie_search, lambda: keep_all_ties)

        xcut = lax.cond(jnp.max(jnp.abs(n_ge - kf)) > 0.0, resolve_ties, lambda: keep_all_ties)
        return thr, xcut

    def no_search():
        return jnp.full((1, QB), F32_LOWEST, f32), jnp.full((1, QB), 2 ** idx_bits - 1, jnp.int32)

    thr, xcut = lax.cond((i + 1) * QB > k_sel, search, no_search)

    def mask_body(jb0, nb, c):
        for sb in range(nb):
            off = blk(jb0 + sb)
            sc = score_ref[pl.ds(off, QB), :]
            s_glob = s_loc + off
            keep = ((sc > thr) | ((sc == thr) & (s_glob <= xcut))) & (s_glob <= t_glob)
            mask_ref[pl.ds(off, QB), :] = jnp.where(keep, 0.0, -jnp.inf)
        return c

    block_loop(mask_body, 0)

    acc_ref[...] = jnp.zeros_like(acc_ref)

    def att_body(jb0, nb, carry):
        m, l8 = list(carry[0]), list(carry[1])
        rows = nb * QB
        lg_blk = _dot_nt(ckv_ref[pl.ds(blk(jb0), rows), :], qlat_ref[...])
        blk_max = [None] * N_HEADS_A
        for sb in range(nb):
            off = blk(jb0 + sb)
            msk = mask_ref[pl.ds(off, QB), :]
            bsel = jnp.clip(jb0 + sb - i + 2, 0, 2)
            for h in range(N_HEADS_A):
                lgh = lg_blk[sb * QB:(sb + 1) * QB, h * QB:(h + 1) * QB] + bias_ref[bsel, h] + msk
                logit_ref[sb * QB:(sb + 1) * QB, h * QB:(h + 1) * QB] = lgh
                cm = _colmax8(lgh)
                blk_max[h] = cm if blk_max[h] is None else jnp.maximum(blk_max[h], cm)
        ps, scales = [], []
        for h in range(N_HEADS_A):
            m_new = jnp.maximum(m[h], jnp.max(blk_max[h], axis=0, keepdims=True))
            m_ref = jnp.where(m_new == -jnp.inf, 0.0, m_new)
            p = jnp.exp2(logit_ref[0:rows, h * QB:(h + 1) * QB] - m_ref)
            scale = jnp.exp2(m[h] - m_ref)
            l8[h] = l8[h] * scale + jnp.sum(p.reshape(rows // SUBLANES, SUBLANES, QB), axis=0)
            m[h] = m_new
            ps.append(p.astype(bf))
            scales.append(scale)
        pv = _dot(ckvt_ref[0, :, pl.ds(blk(jb0), rows)], jnp.concatenate(ps, axis=1))
        for h in range(N_HEADS_A):
            hs = slice(h * QB, (h + 1) * QB)
            acc_ref[:, hs] = acc_ref[:, hs] * scales[h] + pv[:, hs]
        return tuple(m), tuple(l8)

    _, l8 = block_loop(att_body, (tuple(jnp.full((1, QB), -jnp.inf, f32) for _ in range(N_HEADS_A)),
                                  tuple(jnp.zeros((SUBLANES, QB), f32) for _ in range(N_HEADS_A))))

    outs = []
    for h in range(N_HEADS_A):
        l_row = jnp.sum(l8[h], axis=0, keepdims=True)
        o_lat_t = (acc_ref[:, h * QB:(h + 1) * QB] / l_row).astype(bf)
        outs.append(_dot(wuvt_ref[h], o_lat_t))
    o_ref[...] = jnp.concatenate(outs, axis=0).T.astype(o_ref.dtype)


def _dsa(cq, iwt, kidx, ckv, ckvt, w_qidx, w_uq, w_uk_h, w_uvt_h, bias_tiles, B, S):
    T = cq.shape[0]
    assert S % QB == 0 and QB >= REL_MAX_DIST
    nq = S // QB
    k_sel = min(TOPK_MAX, S // 4)
    idx_bits = max(1, (S - 1).bit_length())
    c2 = lambda b, i: (0, 0)
    c3 = lambda b, i: (0, 0, 0)
    return pl.pallas_call(
        functools.partial(_dsa_kernel, k_sel=k_sel, idx_bits=idx_bits),
        grid=(B, nq),
        in_specs=[
            pl.BlockSpec((QB, Q_RANK), lambda b, i: (b * nq + i, 0)),
            pl.BlockSpec((1, N_IDX_HEADS, QB), lambda b, i: (b, 0, i)),
            pl.BlockSpec((S, IDX_DIM), lambda b, i: (b, 0)),
            pl.BlockSpec((S, KV_RANK), lambda b, i: (b, 0)),
            pl.BlockSpec((1, KV_RANK, S), lambda b, i: (b, 0, 0)),
            pl.BlockSpec(w_qidx.shape, c2),
            pl.BlockSpec(w_uq.shape, c2),
            pl.BlockSpec(w_uk_h.shape, c3),
            pl.BlockSpec(w_uvt_h.shape, c3),
            pl.BlockSpec(bias_tiles.shape, lambda b, i: (0, 0, 0, 0)),
        ],
        out_specs=pl.BlockSpec((QB, MIX_A), lambda b, i: (b * nq + i, 0)),
        out_shape=jax.ShapeDtypeStruct((T, MIX_A), MXU_DTYPE),
        scratch_shapes=[
            pltpu.VMEM((Q_RANK, N_HEADS_A * KV_RANK), MXU_DTYPE),
            pltpu.VMEM((N_IDX_HEADS * QB, IDX_DIM), MXU_DTYPE),
            pltpu.VMEM((N_HEADS_A * QB, KV_RANK), MXU_DTYPE),
            pltpu.VMEM((S, QB), jnp.float32),
            pltpu.VMEM((S, QB), jnp.float32),
            pltpu.VMEM((max(UNROLL_WIDTHS) * QB, N_HEADS_A * QB), jnp.float32),
            pltpu.VMEM((KV_RANK, N_HEADS_A * QB), jnp.float32),
        ],
        compiler_params=_cparams(("arbitrary", "arbitrary")),
        name="dsa",
    )(cq, iwt, kidx, ckv, ckvt, w_qidx, w_uq, w_uk_h, w_uvt_h, bias_tiles)


def _layer_norm(xf, g, b):
    mu = jnp.mean(xf, axis=-1, keepdims=True)
    xc = xf - mu
    var = jnp.mean(xc * xc, axis=-1, keepdims=True)
    return xc * lax.rsqrt(var + LN_EPS) * g + b


def _rank_rows(v, n):
    ri = lax.broadcasted_iota(jnp.int32, v.shape, 0)
    rank = jnp.zeros(v.shape, jnp.float32)
    for r2 in range(n):
        row = v[r2:r2 + 1, :]
        beats = (row > v) | ((row == v) & (ri > r2))
        rank = rank + jnp.where(beats, 1.0, 0.0)
    return rank


def _top_rows(v, k):
    n = v.shape[0]
    ri = lax.broadcasted_iota(jnp.int32, v.shape, 0)
    sel = jnp.zeros(v.shape, jnp.float32)
    for _ in range(k):
        m = jnp.max(v, axis=0, keepdims=True)
        first = jnp.min(jnp.where(v == m, ri, n), axis=0, keepdims=True)
        pick = ri == first
        sel = jnp.where(pick, 1.0, sel)
        v = jnp.where(pick, -jnp.inf, v)
    return sel > 0.5


def _pack_factor():
    return 4 // jnp.dtype(MXU_DTYPE).itemsize


def _pack_rows(x):
    if _pack_factor() == 1:
        return pltpu.bitcast(x, jnp.int32)
    half = x.shape[1] // 2
    b = pltpu.bitcast(x.astype(MXU_DTYPE).astype(jnp.float32), jnp.int32)
    return b[:, half:] | lax.shift_right_logical(b[:, :half], jnp.int32(16))


_HIGH_HALF = -(1 << 16)


def _unpack_rows_f32(p):
    if _pack_factor() == 1:
        return [pltpu.bitcast(p, jnp.float32)]
    lo = pltpu.bitcast(lax.shift_left(p, jnp.int32(16)), jnp.float32)
    hi = pltpu.bitcast(p & jnp.int32(_HIGH_HALF), jnp.float32)
    return [lo, hi]


def _unpack_rows(p):
    return [v.astype(MXU_DTYPE) for v in _unpack_rows_f32(p)]


def _mix_router_kernel(x_ref, ya_ref, yb_ref, yc_ref, wo_ref, g_ref, b_ref, wrt_ref, rb_ref, exp_ref,
                       x1_ref, x1p_ref, sel_ref, w_ref, pos_ref, cnt_ref, base_ref, *, tm):
    step = pl.program_id(0)
    f32 = jnp.float32

    @pl.when(step == 0)
    def _():
        base_ref[...] = jnp.zeros_like(base_ref)

    mix = _dot(ya_ref[...], wo_ref[0:MIX_A, :])
    mix = mix + _dot(yb_ref[...], wo_ref[MIX_A:MIX_A + CONV_CH, :])
    mix = mix + _dot(yc_ref[...], wo_ref[MIX_A + CONV_CH:, :])
    x1 = _layer_norm(ALPHA * x_ref[...] + mix, g_ref[...], b_ref[...])
    x1_ref[...] = x1
    x1p_ref[...] = _pack_rows(x1)

    lg = lax.dot_general(wrt_ref[...], x1, _NT, precision=lax.Precision.HIGHEST, preferred_element_type=f32)
    s = 1.0 / (1.0 + jnp.exp(-lg))
    sc = s + rb_ref[...]

    g3 = sc.reshape(N_GROUPS, GROUP_SIZE, tm)
    m1 = jnp.max(g3, axis=1, keepdims=True)
    is_m1 = g3 == m1
    n_m1 = jnp.sum(jnp.where(is_m1, 1.0, 0.0), axis=1, keepdims=True)
    m2 = jnp.max(jnp.where(is_m1, -jnp.inf, g3), axis=1, keepdims=True)
    gscore = (m1 + jnp.where(n_m1 > 1.0, m1, m2)).reshape(N_GROUPS, tm)
    gsel = jnp.where(_rank_rows(gscore, N_GROUPS) < float(TOPK_GROUPS), 1.0, 0.0)
    emask = _dot(exp_ref[...], gsel.astype(MXU_DTYPE)) > 0.5
    masked = jnp.where(emask, sc, -jnp.inf)
    sel = _top_rows(masked, TOP_K) & emask
    self_ = jnp.where(sel, 1.0, 0.0)
    top_s = jnp.where(sel, s, 0.0)
    w = top_s / jnp.sum(top_s, axis=0, keepdims=True) * ROUTED_SCALE

    t_r = lax.broadcasted_iota(jnp.int32, (tm, tm), 0)
    t_c = lax.broadcasted_iota(jnp.int32, (tm, tm), 1)
    upper = jnp.where(t_r < t_c, 1.0, 0.0).astype(MXU_DTYPE)
    pref = _dot(self_.astype(MXU_DTYPE), upper)
    base = base_ref[...]
    sel_ref[...] = self_
    w_ref[...] = w
    pos_ref[...] = base + pref
    base = base + jnp.sum(self_, axis=1, keepdims=True)
    base_ref[...] = base
    cnt_ref[...] = jnp.broadcast_to(base, cnt_ref.shape)


def _mix_router(x2, ya, yb, yc, w_out, ln_g, ln_b, w_router_t, router_bias, tm):
    T, D = x2.shape
    E = N_EXPERTS
    expand = (jnp.arange(E)[:, None] // GROUP_SIZE == jnp.arange(N_GROUPS)[None, :]).astype(MXU_DTYPE)
    row = lambda i: (i, 0)
    col = lambda i: (0, i)
    c2 = lambda i: (0, 0)
    f32 = jnp.float32
    return pl.pallas_call(
        functools.partial(_mix_router_kernel, tm=tm),
        grid=(T // tm,),
        in_specs=[
            pl.BlockSpec((tm, D), row),
            pl.BlockSpec((tm, MIX_A), row),
            pl.BlockSpec((tm, CONV_CH), row),
            pl.BlockSpec((tm, MIX_C), row),
            pl.BlockSpec(w_out.shape, c2),
            pl.BlockSpec((1, D), c2),
            pl.BlockSpec((1, D), c2),
            pl.BlockSpec((E, D), c2),
            pl.BlockSpec((E, 1), c2),
            pl.BlockSpec((E, N_GROUPS), c2),
        ],
        out_specs=[
            pl.BlockSpec((tm, D), row),
            pl.BlockSpec((tm, D // _pack_factor()), row),
            pl.BlockSpec((E, tm), col),
            pl.BlockSpec((E, tm), col),
            pl.BlockSpec((E, tm), col),
            pl.BlockSpec((E, LANES), c2),
        ],
        out_shape=[
            jax.ShapeDtypeStruct((T, D), f32),
            jax.ShapeDtypeStruct((T, D // _pack_factor()), jnp.int32),
            jax.ShapeDtypeStruct((E, T), f32),
            jax.ShapeDtypeStruct((E, T), f32),
            jax.ShapeDtypeStruct((E, T), f32),
            jax.ShapeDtypeStruct((E, LANES), f32),
        ],
        scratch_shapes=[pltpu.VMEM((E, 1), f32)],
        compiler_params=_cparams(("arbitrary",)),
        name="mix_router",
    )(x2, ya, yb, yc, w_out, ln_g, ln_b, w_router_t, router_bias, expand)


def _compact_kernel(sel_ref, w_ref, pos_ref, pstart_ref, low_ref, dest_ref, wk_ref):
    sel = sel_ref[...]
    on = sel > 0.5
    rank = _dot(low_ref[...], sel.astype(MXU_DTYPE))
    row = pstart_ref[...] + pos_ref[...]
    w = w_ref[...]
    dests, ws = [], []
    for k in range(TOP_K):
        m = on & (rank == float(k))
        dests.append(jnp.sum(jnp.where(m, row, 0.0), axis=0, keepdims=True))
        ws.append(jnp.sum(jnp.where(m, w, 0.0), axis=0, keepdims=True))
    dest_ref[...] = jnp.concatenate(dests, axis=0).astype(jnp.int32)
    wk_ref[...] = jnp.concatenate(ws, axis=0)


def _compact(sel_t, w_t, pos_t, pad_start, tm):
    E, T = sel_t.shape
    lower = (jnp.arange(E)[None, :] < jnp.arange(E)[:, None]).astype(MXU_DTYPE)
    col = lambda i: (0, i)
    c2 = lambda i: (0, 0)
    return pl.pallas_call(
        _compact_kernel,
        grid=(T // tm,),
        in_specs=[pl.BlockSpec((E, tm), col), pl.BlockSpec((E, tm), col), pl.BlockSpec((E, tm), col),
                  pl.BlockSpec((E, 1), c2), pl.BlockSpec((E, E), c2)],
        out_specs=[pl.BlockSpec((TOP_K, tm), col), pl.BlockSpec((TOP_K, tm), col)],
        out_shape=[jax.ShapeDtypeStruct((TOP_K, T), jnp.int32), jax.ShapeDtypeStruct((TOP_K, T), jnp.float32)],
        compiler_params=_cparams(("arbitrary",)),
        name="route_compact",
    )(sel_t, w_t, pos_t, pad_start, lower)


def _silu(g):
    return g / (1.0 + jnp.exp(-g))


def _expert_kernel(be_ref, nv_ref, nu_ref, xs_ref, wg_ref, wu_ref, wd_ref, ys_ref, wgb_ref, wub_ref, wdb_ref):
    i = pl.program_id(0)

    @pl.when((i == 0) | (be_ref[i] != be_ref[jnp.maximum(i - 1, 0)]))
    def _():
        wgb_ref[...] = wg_ref[0].astype(MXU_DTYPE)
        wub_ref[...] = wu_ref[0].astype(MXU_DTYPE)
        wdb_ref[...] = wd_ref[0].astype(MXU_DTYPE)

    @pl.when(i < nu_ref[0])
    def _():
        live = lax.broadcasted_iota(jnp.int32, (ROW_BLOCK, 1), 0) < nv_ref[i]
        parts = [jnp.where(live, v, jnp.zeros_like(v)) for v in _unpack_rows(xs_ref[...])]
        dk = wgb_ref.shape[0] // len(parts)

        def proj(w_ref):
            acc = _dot(parts[0], w_ref[0:dk, :])
            for n in range(1, len(parts)):
                acc = acc + _dot(parts[n], w_ref[n * dk:(n + 1) * dk, :])
            return acc

        a = (_silu(proj(wgb_ref)) * proj(wub_ref)).astype(MXU_DTYPE)
        ys_ref[...] = _pack_rows(_dot(a, wdb_ref[...]))


def _experts(xs, block_e, block_valid, n_used, w_gate, w_up, w_down):
    n_rows, W = xs.shape
    D = w_gate.shape[1]
    n_blocks = n_rows // ROW_BLOCK
    blk = lambda i, be, nv, nu: (jnp.minimum(i, nu[0] - 1), 0)
    wsel = lambda i, be, nv, nu: (be[i], 0, 0)
    return pl.pallas_call(
        _expert_kernel,
        grid_spec=pltpu.PrefetchScalarGridSpec(
            num_scalar_prefetch=3,
            grid=(n_blocks,),
            in_specs=[
                pl.BlockSpec((ROW_BLOCK, W), blk),
                pl.BlockSpec((1, D, D_EXPERT), wsel),
                pl.BlockSpec((1, D, D_EXPERT), wsel),
                pl.BlockSpec((1, D_EXPERT, D), wsel),
            ],
            out_specs=pl.BlockSpec((ROW_BLOCK, W), blk),
            scratch_shapes=[pltpu.VMEM((D, D_EXPERT), MXU_DTYPE), pltpu.VMEM((D, D_EXPERT), MXU_DTYPE),
                            pltpu.VMEM((D_EXPERT, D), MXU_DTYPE)],
        ),
        out_shape=jax.ShapeDtypeStruct((n_rows, W), xs.dtype),
        compiler_params=_cparams(("arbitrary",)),
        name="experts",
    )(block_e, block_valid, n_used, xs, w_gate, w_up, w_down)


SC_CORES = 2
SC_SUBCORES = 16
SC_GATHER_ROWS = 64
COMBINE_CHUNKS = 8


def _sc_gather_rows(table, idx):
    n = idx.shape[0]
    w = table.shape[1]
    n_workers = SC_CORES * SC_SUBCORES
    per_worker = n // n_workers
    assert n % n_workers == 0 and per_worker % SC_GATHER_ROWS == 0
    mesh = plsc.VectorSubcoreMesh(core_axis_name="c", subcore_axis_name="s")

    @functools.partial(
        pl.kernel, mesh=mesh,
        out_type=jax.ShapeDtypeStruct((n, w), table.dtype),
        scratch_types=[
            pltpu.VMEM((2, SC_GATHER_ROWS), jnp.int32),
            pltpu.VMEM((2, SC_GATHER_ROWS, w), table.dtype),
            pltpu.SemaphoreType.DMA((2,)),
        ],
        name="sc_gather_rows",
    )
    def gather(table_hbm, idx_hbm, out_hbm, idx_v, rows_v, sem):
        wid = lax.axis_index("s") * SC_CORES + lax.axis_index("c")
        base = wid * per_worker
        n_steps = per_worker // SC_GATHER_ROWS

        def gather_copy(slot):
            return pltpu.make_async_copy(table_hbm.at[idx_v.at[slot]], rows_v.at[slot], sem.at[slot])

        def start(step, slot):
            pltpu.sync_copy(idx_hbm.at[pl.ds(base + step * SC_GATHER_ROWS, SC_GATHER_ROWS)], idx_v.at[slot])
            gather_copy(slot).start()

        start(0, 0)

        @pl.loop(0, n_steps, step=2)
        def _(g):
            for slot in range(2):
                step = g + slot

                @pl.when(step + 1 < n_steps)
                def _():
                    start(step + 1, 1 - slot)

                gather_copy(slot).wait()
                pltpu.sync_copy(rows_v.at[slot], out_hbm.at[pl.ds(base + step * SC_GATHER_ROWS, SC_GATHER_ROWS)])

    return gather(table, idx)


SC_SCATTER_ROWS = 64


def _sc_scatter_rows(rows, idx3, n_out):
    n_src, w = rows.shape
    n_chunks, n_dst, batch = idx3.shape
    n_workers = SC_CORES * SC_SUBCORES
    assert batch == SC_SCATTER_ROWS and n_chunks * batch == n_src and n_chunks % (2 * n_workers) == 0
    per_worker = n_chunks // n_workers
    mesh = plsc.VectorSubcoreMesh(core_axis_name="c", subcore_axis_name="s")

    @functools.partial(
        pl.kernel, mesh=mesh,
        out_type=jax.ShapeDtypeStruct((n_out, w), rows.dtype),
        scratch_types=[
            pltpu.VMEM((2, n_dst, batch), jnp.int32),
            pltpu.VMEM((2, batch, w), rows.dtype),
            pltpu.SemaphoreType.DMA((2,)),
            pltpu.SemaphoreType.DMA,
        ],
        name="sc_scatter_rows",
    )
    def scatter(rows_hbm, idx_hbm, out_hbm, idx_v, rows_v, load_sem, store_sem):
        wid = lax.axis_index("s") * SC_CORES + lax.axis_index("c")

        def load_copy(step, slot):
            c = wid * per_worker + step
            return pltpu.make_async_copy(rows_hbm.at[pl.ds(c * batch, batch)], rows_v.at[slot], load_sem.at[slot])

        def load(step, slot):
            pltpu.sync_copy(idx_hbm.at[wid * per_worker + step], idx_v.at[slot])
            load_copy(step, slot).start()

        def store_copy(slot, k):
            return pltpu.make_async_copy(rows_v.at[slot], out_hbm.at[idx_v.at[slot].at[k]], store_sem)

        load(0, 0)

        @pl.loop(0, per_worker, step=2)
        def _(g):
            for slot in range(2):
                step = g + slot

                @pl.when(step + 1 < per_worker)
                def _():
                    load(step + 1, 1 - slot)

                load_copy(step, slot).wait()
                for k in range(n_dst):
                    store_copy(slot, k).start()
                for k in range(n_dst):
                    store_copy(slot, k).wait()

    return scatter(rows, idx3)


def _combine2_kernel(wk_ref, x1_ref, g_ref_rows, wsg_ref, wsu_ref, wsd_ref, g_ref, b_ref, o_ref):
    x1 = x1_ref[...]
    xb = x1.astype(MXU_DTYPE)
    a = (_silu(_dot(xb, wsg_ref[...])) * _dot(xb, wsu_ref[...])).astype(MXU_DTYPE)
    shared = _dot(a, wsd_ref[...])
    wk = wk_ref[...].T
    groups = [wk[:, 0:1] * v for v in _unpack_rows_f32(g_ref_rows[0])]
    for k in range(1, TOP_K):
        groups = [g + wk[:, k:k + 1] * v for g, v in zip(groups, _unpack_rows_f32(g_ref_rows[k]))]
    routed = jnp.concatenate(groups, axis=1)
    o_ref[...] = _layer_norm(ALPHA * x1 + (routed + shared), g_ref[...], b_ref[...])


def _combine2_kernel_into(wk_ref, x1_ref, g_ref_rows, wsg_ref, wsu_ref, wsd_ref, g_ref, b_ref, prev_ref, o_ref):
    del prev_ref
    _combine2_kernel(wk_ref, x1_ref, g_ref_rows, wsg_ref, wsu_ref, wsd_ref, g_ref, b_ref, o_ref)


def _combine2(wk_t, x1, gathered, w_sg, w_su, w_sd, ln_g, ln_b, tc, chunk, prev):
    T, D = x1.shape
    _, t_chunk, W = gathered.shape
    base = chunk * (t_chunk // tc)
    row = lambda i: (base + i, 0)
    c2 = lambda i: (0, 0)
    in_specs = [
        pl.BlockSpec((TOP_K, tc), lambda i: (0, base + i)),
        pl.BlockSpec((tc, D), row),
        pl.BlockSpec((TOP_K, tc, W), lambda i: (0, i, 0)),
        pl.BlockSpec(w_sg.shape, c2),
        pl.BlockSpec(w_su.shape, c2),
        pl.BlockSpec(w_sd.shape, c2),
        pl.BlockSpec((1, D), c2),
        pl.BlockSpec((1, D), c2),
    ]
    args = [wk_t, x1, gathered, w_sg, w_su, w_sd, ln_g, ln_b]
    if prev is None:
        body, aliases = _combine2_kernel, {}
    else:
        body, aliases = _combine2_kernel_into, {len(args): 0}
        in_specs.append(pl.BlockSpec(memory_space=pl.ANY))
        args.append(prev)
    return pl.pallas_call(
        body,
        grid=(t_chunk // tc,),
        in_specs=in_specs,
        out_specs=pl.BlockSpec((tc, D), row),
        out_shape=jax.ShapeDtypeStruct((T, D), jnp.float32),
        input_output_aliases=aliases,
        compiler_params=_cparams(("arbitrary",)),
        name="combine",
    )(*args)


def _split_w_in(w_in):
    bf = MXU_DTYPE
    o_kv = Q_RANK
    o_ki = o_kv + KV_RANK
    o_iw = o_ki + IDX_DIM
    o_rest = o_iw + N_IDX_HEADS
    w_main = jnp.concatenate([w_in[:, :o_ki], w_in[:, o_rest:]], axis=1).astype(bf)
    w_small = jnp.pad(w_in[:, o_ki:o_rest], ((0, 0), (0, LANES - IDX_DIM - N_IDX_HEADS))).astype(bf)
    return w_main, w_small


def _stages(x, mem, w_in, q_norm_g, kv_norm_g, w_uq, w_uk, w_uv, w_qidx, rel_bias, conv_w, w_mem_k, w_mem_v, w_out, ln1_g, ln1_b, w_router, router_bias, w_e_gate, w_e_up, w_e_down, w_s_gate, w_s_up, w_s_down, ln2_g, ln2_b, upto=None):
    B, S, D = x.shape
    T = B * S
    bf = MXU_DTYPE
    l = 0
    res = {}
    x2 = x.reshape(T, D)
    w_main, w_small = _split_w_in(w_in[l])
    cq, ckv, ckvt, kidx, iwt, yb, yc = _proj(
        x2, mem, w_main, w_small, q_norm_g[l].reshape(1, -1), kv_norm_g[l].reshape(1, -1), conv_w[l],
        w_mem_k[l].astype(bf), w_mem_v[l].astype(bf), B, S, tm=min(512, S))
    res.update(c_q=cq, c_kv=ckv, k_idx=kidx, y_b=yb, y_c=yc,
               idx_w=jnp.swapaxes(iwt, 1, 2) / (N_IDX_HEADS ** -0.5 * IDX_DIM ** -0.5))
    if upto == "proj":
        return res
    bias_t = _bias_tiles(rel_bias)
    ya = _dsa(cq, iwt, kidx, ckv, ckvt,
              w_qidx[l].reshape(Q_RANK, -1).astype(bf), w_uq[l].reshape(Q_RANK, -1).astype(bf),
              jnp.transpose(w_uk[l], (1, 0, 2)).astype(bf), jnp.transpose(w_uv[l], (1, 2, 0)).astype(bf),
              bias_t, B, S)
    res.update(y_a=ya)
    if upto == "dsa":
        return res

    x1, x1p, sel_t, w_t, pos_t, cnt = _mix_router(
        x2, ya, yb, yc, w_out[l].astype(bf), ln1_g[l].reshape(1, -1), ln1_b[l].reshape(1, -1),
        w_router[l].T, router_bias[l].reshape(-1, 1), tm=min(512, T))
    res.update(x1=x1)

    counts = cnt[:, 0].astype(jnp.int32)
    padded = (counts + ROW_BLOCK - 1) // ROW_BLOCK * ROW_BLOCK
    pad_end = jnp.cumsum(padded)
    pad_start = pad_end - padded
    n_blocks = -(-(T * TOP_K) // ROW_BLOCK) + N_EXPERTS
    n_rows = n_blocks * ROW_BLOCK
    block_start = jnp.arange(n_blocks, dtype=jnp.int32) * ROW_BLOCK
    block_e = jnp.minimum(jnp.sum((pad_end[None, :] <= block_start[:, None]).astype(jnp.int32), axis=1),
                          N_EXPERTS - 1)
    n_used = (pad_end[-1:] // ROW_BLOCK).astype(jnp.int32)

    dest_t, wk_t = _compact(sel_t, w_t, pos_t, pad_start.astype(jnp.float32).reshape(-1, 1), tm=min(512, T))
    block_valid = jnp.clip((pad_start + counts)[block_e] - block_start, 0, ROW_BLOCK).astype(jnp.int32)
    bt = SC_SCATTER_ROWS
    idx3 = jnp.transpose(dest_t.reshape(TOP_K, T // bt, bt), (1, 0, 2))
    xs = _sc_scatter_rows(x1p, idx3, n_rows)
    ys = _experts(xs, block_e, block_valid, n_used, w_e_gate[l], w_e_up[l], w_e_down[l])
    n_chunks = COMBINE_CHUNKS if T % (COMBINE_CHUNKS * 256) == 0 else 1
    t_chunk = T // n_chunks
    out = None
    for c in range(n_chunks):
        idx_c = dest_t[:, c * t_chunk:(c + 1) * t_chunk].reshape(-1)
        gathered = _sc_gather_rows(ys, idx_c).reshape(TOP_K, t_chunk, -1)
        out = _combine2(wk_t, x1, gathered, w_s_gate[l].astype(bf), w_s_up[l].astype(bf), w_s_down[l].astype(bf),
                        ln2_g[l].reshape(1, -1), ln2_b[l].reshape(1, -1), tc=min(256, t_chunk), chunk=c, prev=out)
    res.update(out=out.reshape(B, S, D))
    return res


def kernel(x, mem, w_in, q_norm_g, kv_norm_g, w_uq, w_uk, w_uv, w_qidx, rel_bias, conv_w, w_mem_k, w_mem_v, w_out, ln1_g, ln1_b, w_router, router_bias, w_e_gate, w_e_up, w_e_down, w_s_gate, w_s_up, w_s_down, ln2_g, ln2_b):
    return _stages(x, mem, w_in, q_norm_g, kv_norm_g, w_uq, w_uk, w_uv, w_qidx, rel_bias, conv_w, w_mem_k, w_mem_v, w_out, ln1_g, ln1_b, w_router, router_bias, w_e_gate, w_e_up, w_e_down, w_s_gate, w_s_up, w_s_down, ln2_g, ln2_b)["out"]
```

```python
import functools
import math

import jax
import jax.numpy as jnp
from jax import lax
from jax.experimental import pallas as pl
from jax.experimental.pallas import tpu as pltpu
from jax.experimental.pallas import tpu_sc as plsc

N_HEADS_A = 8
HEAD_DIM = 64
Q_RANK = 256
KV_RANK = 128
N_IDX_HEADS = 8
IDX_DIM = 64
TOPK_MAX = 256
REL_BUCKETS = 32
REL_MAX_DIST = 128
CONV_CH = 256
CONV_WIDTH = 3
N_MEM_HEADS = 4
MIX_A = N_HEADS_A * HEAD_DIM
MIX_C = N_MEM_HEADS * HEAD_DIM
N_EXPERTS = 64
N_GROUPS = 8
GROUP_SIZE = N_EXPERTS // N_GROUPS
TOPK_GROUPS = 4
TOP_K = 8
D_EXPERT = 256
ROUTED_SCALE = 2.5
DEPTH = 1
ALPHA = (2.0 * DEPTH) ** 0.25
LN_EPS = 1e-5
RMS_EPS = 1e-6
LOG2_E = math.log2(math.e)

LANES = 128
SUBLANES = 8
QB = 128
F32_LOWEST = -3.4028234663852886e38
VMEM_LIMIT = 56 * 1024 * 1024
MXU_DTYPE = jnp.bfloat16
ROW_BLOCK = 1024

_NT = (((1,), (1,)), ((), ()))


def _dot(a, b):
    return jnp.dot(a, b, preferred_element_type=jnp.float32)


def _dot_nt(a, b):
    return lax.dot_general(a, b, _NT, preferred_element_type=jnp.float32)


def _cparams(sem):
    return pltpu.CompilerParams(dimension_semantics=sem, vmem_limit_bytes=VMEM_LIMIT)


def _bias_kernel(rb_ref, o_ref):
    s = lax.broadcasted_iota(jnp.int32, (QB, QB), 0)
    t = lax.broadcasted_iota(jnp.int32, (QB, QB), 1)
    max_exact = REL_BUCKETS // 2
    for tile in range(3):
        n = jnp.maximum(t - s + (2 - tile) * QB, 0)
        nf = jnp.maximum(n.astype(jnp.float32), 1.0)
        large = max_exact + (jnp.log(nf / max_exact) / math.log(REL_MAX_DIST / max_exact)
                             * (REL_BUCKETS - max_exact)).astype(jnp.int32)
        large = jnp.minimum(large, REL_BUCKETS - 1)
        bucket = jnp.where(n < max_exact, n, large)
        for h in range(N_HEADS_A):
            acc = jnp.zeros((QB, QB), jnp.float32)
            for b in range(REL_BUCKETS):
                acc = jnp.where(bucket == b, rb_ref[b, h], acc)
            o_ref[tile, h] = acc * LOG2_E


def _bias_tiles(rel_bias):
    return pl.pallas_call(
        _bias_kernel,
        in_specs=[pl.BlockSpec(memory_space=pltpu.SMEM)],
        out_specs=pl.BlockSpec(memory_space=pltpu.VMEM),
        out_shape=jax.ShapeDtypeStruct((3, N_HEADS_A, QB, QB), jnp.float32),
        name="bias_tiles",
    )(rel_bias)


def _proj_kernel(x_ref, mem_ref, wm_ref, ws_ref, qg_ref, kvg_ref, cw_ref, wmk_ref, wmv_ref,
                 cq_ref, ckv_ref, ckvt_ref, kidx_ref, iwt_ref, yb_ref, yc_ref,
                 carry_ref, mk_ref, mv_ref, *, tm):
    si = pl.program_id(1)

    @pl.when(si == 0)
    def _():
        carry_ref[...] = jnp.zeros_like(carry_ref)
        mb = mem_ref[0].astype(MXU_DTYPE)
        mk_ref[...] = _dot(mb, wmk_ref[...]).astype(MXU_DTYPE)
        mv_ref[...] = _dot(mb, wmv_ref[...]).astype(MXU_DTYPE)

    xb = x_ref[...].astype(MXU_DTYPE)
    p = _dot(xb, wm_ref[...])
    small = _dot(xb, ws_ref[...])

    o = 0
    cq = p[:, o:o + Q_RANK]; o += Q_RANK
    ckv = p[:, o:o + KV_RANK]; o += KV_RANK
    g_b = p[:, o:o + CONV_CH]; o += CONV_CH
    g_c = p[:, o:o + CONV_CH]; o += CONV_CH
    h_c = p[:, o:o + CONV_CH]; o += CONV_CH
    q_mem = p[:, o:o + MIX_C]

    cq = cq * lax.rsqrt(jnp.mean(cq * cq, axis=-1, keepdims=True) + RMS_EPS) * qg_ref[...]
    ckv = ckv * lax.rsqrt(jnp.mean(ckv * ckv, axis=-1, keepdims=True) + RMS_EPS) * kvg_ref[...]
    cq_ref[...] = cq.astype(MXU_DTYPE)
    ckv_b = ckv.astype(MXU_DTYPE)
    ckv_ref[...] = ckv_b
    ckvt_ref[0] = ckv.T.astype(MXU_DTYPE)

    kidx_ref[...] = small[:, :IDX_DIM].astype(MXU_DTYPE)
    small_t = small.T
    iwt_ref[0] = small_t[IDX_DIM:IDX_DIM + N_IDX_HEADS, :] * (N_IDX_HEADS ** -0.5 * IDX_DIM ** -0.5)

    u = g_c * h_c
    rows = lax.broadcasted_iota(jnp.int32, (tm, 1), 0)
    c6 = carry_ref[SUBLANES - 2:SUBLANES - 1, :]
    c7 = carry_ref[SUBLANES - 1:SUBLANES, :]
    u1 = jnp.where(rows == 0, c7, pltpu.roll(u, 1, 0))
    u2 = jnp.where(rows == 0, c6, jnp.where(rows == 1, c7, pltpu.roll(u, 2, 0)))
    y = cw_ref[0:1, :] * u2
    y = y + cw_ref[1:2, :] * u1
    y = y + cw_ref[2:3, :] * u
    yb_ref[...] = (g_b * y).astype(MXU_DTYPE)
    carry_ref[...] = u[tm - SUBLANES:, :]

    qm = q_mem.astype(MXU_DTYPE)
    outs = []
    for h in range(N_MEM_HEADS):
        sl = slice(h * HEAD_DIM, (h + 1) * HEAD_DIM)
        lg = _dot_nt(qm[:, sl], mk_ref[:, sl]) * (HEAD_DIM ** -0.5)
        lg = lg - jnp.max(lg, axis=-1, keepdims=True)
        e = jnp.exp(lg)
        pr = e / jnp.sum(e, axis=-1, keepdims=True)
        outs.append(_dot(pr.astype(MXU_DTYPE), mv_ref[:, sl]))
    yc_ref[...] = jnp.concatenate(outs, axis=-1).astype(MXU_DTYPE)


def _proj(x2, mem, w_main, w_small, q_g, kv_g, conv_w, w_mk, w_mv, B, S, tm):
    T, D = x2.shape
    n_mem = mem.shape[1]
    ns = S // tm
    row = lambda b, s: (b * ns + s, 0)
    const2 = lambda b, s: (0, 0)
    bf = MXU_DTYPE
    return pl.pallas_call(
        functools.partial(_proj_kernel, tm=tm),
        grid=(B, ns),
        in_specs=[
            pl.BlockSpec((tm, D), row),
            pl.BlockSpec((1, n_mem, D), lambda b, s: (b, 0, 0)),
            pl.BlockSpec(w_main.shape, const2),
            pl.BlockSpec(w_small.shape, const2),
            pl.BlockSpec(q_g.shape, const2),
            pl.BlockSpec(kv_g.shape, const2),
            pl.BlockSpec(conv_w.shape, const2),
            pl.BlockSpec(w_mk.shape, const2),
            pl.BlockSpec(w_mv.shape, const2),
        ],
        out_specs=[
            pl.BlockSpec((tm, Q_RANK), row),
            pl.BlockSpec((tm, KV_RANK), row),
            pl.BlockSpec((1, KV_RANK, tm), lambda b, s: (b, 0, s)),
            pl.BlockSpec((tm, IDX_DIM), row),
            pl.BlockSpec((1, N_IDX_HEADS, tm), lambda b, s: (b, 0, s)),
            pl.BlockSpec((tm, CONV_CH), row),
            pl.BlockSpec((tm, MIX_C), row),
        ],
        out_shape=[
            jax.ShapeDtypeStruct((T, Q_RANK), bf),
            jax.ShapeDtypeStruct((T, KV_RANK), bf),
            jax.ShapeDtypeStruct((B, KV_RANK, S), bf),
            jax.ShapeDtypeStruct((T, IDX_DIM), bf),
            jax.ShapeDtypeStruct((B, N_IDX_HEADS, S), jnp.float32),
            jax.ShapeDtypeStruct((T, CONV_CH), bf),
            jax.ShapeDtypeStruct((T, MIX_C), bf),
        ],
        scratch_shapes=[
            pltpu.VMEM((SUBLANES, CONV_CH), jnp.float32),
            pltpu.VMEM((n_mem, MIX_C), bf),
            pltpu.VMEM((n_mem, MIX_C), bf),
        ],
        compiler_params=_cparams(("arbitrary", "arbitrary")),
        name="proj",
    )(x2, mem, w_main, w_small, q_g, kv_g, conv_w, w_mk, w_mv)


def _key_to_f32(key):
    bits = jnp.where(key < 0, key ^ jnp.int32(0x7FFFFFFF), key)
    return pltpu.bitcast(bits, jnp.float32)


def _colsum8(v):
    return jnp.sum(v.reshape(QB // SUBLANES, SUBLANES, QB), axis=0)


def _colmax8(v):
    return jnp.max(v.reshape(QB // SUBLANES, SUBLANES, QB), axis=0)


UNROLL_WIDTHS = (8, 4, 2, 1)


def _dsa_kernel(cq_ref, iwt_ref, kidx_ref, ckv_ref, ckvt_ref, wqi_ref, wuq_ref, wuk_ref, wuvt_ref,
                bias_ref, o_ref, wfold_ref, qidx_ref, qlat_ref, score_ref, logit_ref, acc_ref,
                *, k_sel, idx_bits):
    i = pl.program_id(1)
    f32 = jnp.float32
    bf = MXU_DTYPE
    n_blocks = i + 1
    n_blocks = n_blocks + jnp.where((n_blocks % 4 == 3) & (n_blocks < pl.num_programs(1)), 1, 0)
    s_loc = lax.broadcasted_iota(jnp.int32, (QB, QB), 0)
    t_glob = i * QB + lax.broadcasted_iota(jnp.int32, (QB, QB), 1)

    def blk(jb):
        return pl.multiple_of(jb * QB, QB)

    def block_loop(fn, init):
        c, start = init, 0
        for width in UNROLL_WIDTHS:
            n = (n_blocks - start) // width
            c = lax.fori_loop(0, n, lambda it, c, w=width, s=start: fn(s + it * w, w, c), c)
            start = start + n * width
        return c

    @pl.when(i == 0)
    def _():
        for h in range(N_HEADS_A):
            wfold_ref[:, h * KV_RANK:(h + 1) * KV_RANK] = (
                _dot_nt(wuq_ref[:, h * HEAD_DIM:(h + 1) * HEAD_DIM], wuk_ref[h])
                * (HEAD_DIM ** -0.5 * LOG2_E)).astype(bf)

    cq = cq_ref[...]
    q_idx = _dot(cq, wqi_ref[...]).astype(bf)
    q_lat = _dot(cq, wfold_ref[...]).astype(bf)
    for h in range(N_HEADS_A):
        qidx_ref[h * QB:(h + 1) * QB, :] = q_idx[:, h * IDX_DIM:(h + 1) * IDX_DIM]
        qlat_ref[h * QB:(h + 1) * QB, :] = q_lat[:, h * KV_RANK:(h + 1) * KV_RANK]
    iw = iwt_ref[0]

    def score_body(jb0, nb, c):
        d_blk = _dot_nt(kidx_ref[pl.ds(blk(jb0), nb * QB), :], qidx_ref[...])
        for sb in range(nb):
            off = blk(jb0 + sb)
            d_all = d_blk[sb * QB:(sb + 1) * QB, :]
            acc = jnp.maximum(d_all[:, 0:QB], 0.0) * iw[0:1, :]
            for h in range(1, N_IDX_HEADS):
                acc = acc + jnp.maximum(d_all[:, h * QB:(h + 1) * QB], 0.0) * iw[h:h + 1, :]
            score_ref[pl.ds(off, QB), :] = jnp.where(s_loc + off <= t_glob, acc + 0.0, F32_LOWEST)
        return c

    block_loop(score_body, 0)

    def count_where(pred):
        def body(jb0, nb, acc):
            for sb in range(nb):
                off = blk(jb0 + sb)
                acc = acc + _colsum8(jnp.where(pred(score_ref[pl.ds(off, QB), :], off), 1.0, 0.0))
            return acc
        acc = block_loop(body, jnp.zeros((SUBLANES, QB), f32))
        return jnp.sum(acc, axis=0, keepdims=True)

    kf = float(k_sel)

    def search():
        c0 = count_where(lambda sc, off: sc >= 0.0)
        cand0 = jnp.where(c0 >= kf, jnp.int32(0), jnp.int32(-2 ** 31))
        n_ge0 = jnp.where(c0 >= kf, c0, -1.0)

        def bit_body(it, carry):
            cand, n_ge = carry
            trial = cand + lax.shift_left(jnp.int32(1), 30 - it)
            tf = _key_to_f32(trial)
            cnt = count_where(lambda sc, off: sc >= tf)
            take = cnt >= kf
            return jnp.where(take, trial, cand), jnp.where(take, cnt, n_ge)

        cand, n_ge = lax.fori_loop(0, 31, bit_body, (cand0, n_ge0))
        thr = _key_to_f32(cand)
        keep_all_ties = jnp.full((1, QB), 2 ** idx_bits - 1, jnp.int32)

        def resolve_ties():
            n_gt = count_where(lambda sc, off: sc > thr)
            n_eq = count_where(lambda sc, off: sc == thr)
            need = kf - n_gt

            def tie_search():
                def tbody(it, xcut):
                    trial = xcut + lax.shift_left(jnp.int32(1), idx_bits - 1 - it)
                    cnt = count_where(lambda sc, off: (sc == thr) & (s_loc + off < trial))
                    return jnp.where(cnt < need, trial, xcut)
                return lax.fori_loop(0, idx_bits, tbody, jnp.zeros((1, QB), jnp.int32))

            return lax.cond(jnp.max(n_eq - need) > 0.0, tie_search, lambda: keep_all_ties)

        xcut = lax.cond(jnp.max(jnp.abs(n_ge - kf)) > 0.0, resolve_ties, lambda: keep_all_ties)
        return thr, xcut

    def no_search():
        return jnp.full((1, QB), F32_LOWEST, f32), jnp.full((1, QB), 2 ** idx_bits - 1, jnp.int32)

    thr, xcut = lax.cond((i + 1) * QB > k_sel, search, no_search)

    def selection_mask(off):
        sc = score_ref[pl.ds(off, QB), :]
        s_glob = s_loc + off
        keep = ((sc > thr) | ((sc == thr) & (s_glob <= xcut))) & (s_glob <= t_glob)
        return jnp.where(keep, 0.0, -jnp.inf)

    acc_ref[...] = jnp.zeros_like(acc_ref)

    def att_body(jb0, nb, carry):
        m, l8 = list(carry[0]), list(carry[1])
        rows = nb * QB
        lg_blk = _dot_nt(ckv_ref[pl.ds(blk(jb0), rows), :], qlat_ref[...])
        blk_max = [None] * N_HEADS_A
        for sb in range(nb):
            off = blk(jb0 + sb)
            msk = selection_mask(off)
            bsel = jnp.clip(jb0 + sb - i + 2, 0, 2)
            for h in range(N_HEADS_A):
                lgh = lg_blk[sb * QB:(sb + 1) * QB, h * QB:(h + 1) * QB] + bias_ref[bsel, h] + msk
                logit_ref[sb * QB:(sb + 1) * QB, h * QB:(h + 1) * QB] = lgh
                cm = _colmax8(lgh)
                blk_max[h] = cm if blk_max[h] is None else jnp.maximum(blk_max[h], cm)
        ps, scales = [], []
        for h in range(N_HEADS_A):
            m_new = jnp.maximum(m[h], jnp.max(blk_max[h], axis=0, keepdims=True))
            m_ref = jnp.where(m_new == -jnp.inf, 0.0, m_new)
            p = jnp.exp2(logit_ref[0:rows, h * QB:(h + 1) * QB] - m_ref)
            scale = jnp.exp2(m[h] - m_ref)
            l8[h] = l8[h] * scale + jnp.sum(p.reshape(rows // SUBLANES, SUBLANES, QB), axis=0)
            m[h] = m_new
            ps.append(p.astype(bf))
            scales.append(scale)
        pv = _dot(ckvt_ref[0, :, pl.ds(blk(jb0), rows)], jnp.concatenate(ps, axis=1))
        for h in range(N_HEADS_A):
            hs = slice(h * QB, (h + 1) * QB)
            acc_ref[:, hs] = acc_ref[:, hs] * scales[h] + pv[:, hs]
        return tuple(m), tuple(l8)

    _, l8 = block_loop(att_body, (tuple(jnp.full((1, QB), -jnp.inf, f32) for _ in range(N_HEADS_A)),
                                  tuple(jnp.zeros((SUBLANES, QB), f32) for _ in range(N_HEADS_A))))

    outs = []
    for h in range(N_HEADS_A):
        l_row = jnp.sum(l8[h], axis=0, keepdims=True)
        o_lat_t = (acc_ref[:, h * QB:(h + 1) * QB] / l_row).astype(bf)
        outs.append(_dot(wuvt_ref[h], o_lat_t))
    o_ref[...] = jnp.concatenate(outs, axis=0).T.astype(o_ref.dtype)


def _dsa(cq, iwt, kidx, ckv, ckvt, w_qidx, w_uq, w_uk_h, w_uvt_h, bias_tiles, B, S):
    T = cq.shape[0]
    assert S % QB == 0 and QB >= REL_MAX_DIST
    nq = S // QB
    k_sel = min(TOPK_MAX, S // 4)
    idx_bits = max(1, (S - 1).bit_length())
    c2 = lambda b, i: (0, 0)
    c3 = lambda b, i: (0, 0, 0)
    return pl.pallas_call(
        functools.partial(_dsa_kernel, k_sel=k_sel, idx_bits=idx_bits),
        grid=(B, nq),
        in_specs=[
            pl.BlockSpec((QB, Q_RANK), lambda b, i: (b * nq + i, 0)),
            pl.BlockSpec((1, N_IDX_HEADS, QB), lambda b, i: (b, 0, i)),
            pl.BlockSpec((S, IDX_DIM), lambda b, i: (b, 0)),
            pl.BlockSpec((S, KV_RANK), lambda b, i: (b, 0)),
            pl.BlockSpec((1, KV_RANK, S), lambda b, i: (b, 0, 0)),
            pl.BlockSpec(w_qidx.shape, c2),
            pl.BlockSpec(w_uq.shape, c2),
            pl.BlockSpec(w_uk_h.shape, c3),
            pl.BlockSpec(w_uvt_h.shape, c3),
            pl.BlockSpec(bias_tiles.shape, lambda b, i: (0, 0, 0, 0)),
        ],
        out_specs=pl.BlockSpec((QB, MIX_A), lambda b, i: (b * nq + i, 0)),
        out_shape=jax.ShapeDtypeStruct((T, MIX_A), MXU_DTYPE),
        scratch_shapes=[
            pltpu.VMEM((Q_RANK, N_HEADS_A * KV_RANK), MXU_DTYPE),
            pltpu.VMEM((N_IDX_HEADS * QB, IDX_DIM), MXU_DTYPE),
            pltpu.VMEM((N_HEADS_A * QB, KV_RANK), MXU_DTYPE),
            pltpu.VMEM((S, QB), jnp.float32),
            pltpu.VMEM((max(UNROLL_WIDTHS) * QB, N_HEADS_A * QB), jnp.float32),
            pltpu.VMEM((KV_RANK, N_HEADS_A * QB), jnp.float32),
        ],
        compiler_params=_cparams(("arbitrary", "arbitrary")),
        name="dsa",
    )(cq, iwt, kidx, ckv, ckvt, w_qidx, w_uq, w_uk_h, w_uvt_h, bias_tiles)


def _layer_norm(xf, g, b):
    mu = jnp.mean(xf, axis=-1, keepdims=True)
    xc = xf - mu
    var = jnp.mean(xc * xc, axis=-1, keepdims=True)
    return xc * lax.rsqrt(var + LN_EPS) * g + b


def _rank_rows(v, n):
    ri = lax.broadcasted_iota(jnp.int32, v.shape, 0)
    rank = jnp.zeros(v.shape, jnp.float32)
    for r2 in range(n):
        row = v[r2:r2 + 1, :]
        beats = (row > v) | ((row == v) & (ri > r2))
        rank = rank + jnp.where(beats, 1.0, 0.0)
    return rank


def _top_rows(v, k):
    n = v.shape[0]
    ri = lax.broadcasted_iota(jnp.int32, v.shape, 0)
    sel = jnp.zeros(v.shape, jnp.float32)
    for _ in range(k):
        m = jnp.max(v, axis=0, keepdims=True)
        first = jnp.min(jnp.where(v == m, ri, n), axis=0, keepdims=True)
        pick = ri == first
        sel = jnp.where(pick, 1.0, sel)
        v = jnp.where(pick, -jnp.inf, v)
    return sel > 0.5


def _pack_factor():
    return 4 // jnp.dtype(MXU_DTYPE).itemsize


def _pack_rows(x):
    if _pack_factor() == 1:
        return pltpu.bitcast(x, jnp.int32)
    half = x.shape[1] // 2
    b = pltpu.bitcast(x.astype(MXU_DTYPE).astype(jnp.float32), jnp.int32)
    return b[:, half:] | lax.shift_right_logical(b[:, :half], jnp.int32(16))


_HIGH_HALF = -(1 << 16)


def _unpack_rows_f32(p):
    if _pack_factor() == 1:
        return [pltpu.bitcast(p, jnp.float32)]
    lo = pltpu.bitcast(lax.shift_left(p, jnp.int32(16)), jnp.float32)
    hi = pltpu.bitcast(p & jnp.int32(_HIGH_HALF), jnp.float32)
    return [lo, hi]


def _unpack_rows(p):
    return [v.astype(MXU_DTYPE) for v in _unpack_rows_f32(p)]


def _mix_router_kernel(x_ref, ya_ref, yb_ref, yc_ref, wo_ref, g_ref, b_ref, wrt_ref, rb_ref, exp_ref,
                       x1_ref, x1p_ref, sel_ref, w_ref, pos_ref, cnt_ref, base_ref, *, tm):
    step = pl.program_id(0)
    f32 = jnp.float32

    @pl.when(step == 0)
    def _():
        base_ref[...] = jnp.zeros_like(base_ref)

    mix = _dot(ya_ref[...], wo_ref[0:MIX_A, :])
    mix = mix + _dot(yb_ref[...], wo_ref[MIX_A:MIX_A + CONV_CH, :])
    mix = mix + _dot(yc_ref[...], wo_ref[MIX_A + CONV_CH:, :])
    x1 = _layer_norm(ALPHA * x_ref[...] + mix, g_ref[...], b_ref[...])
    x1_ref[...] = x1
    x1p_ref[...] = _pack_rows(x1)

    lg = lax.dot_general(wrt_ref[...], x1, _NT, precision=lax.Precision.HIGHEST, preferred_element_type=f32)
    s = 1.0 / (1.0 + jnp.exp(-lg))
    sc = s + rb_ref[...]

    g3 = sc.reshape(N_GROUPS, GROUP_SIZE, tm)
    m1 = jnp.max(g3, axis=1, keepdims=True)
    is_m1 = g3 == m1
    n_m1 = jnp.sum(jnp.where(is_m1, 1.0, 0.0), axis=1, keepdims=True)
    m2 = jnp.max(jnp.where(is_m1, -jnp.inf, g3), axis=1, keepdims=True)
    gscore = (m1 + jnp.where(n_m1 > 1.0, m1, m2)).reshape(N_GROUPS, tm)
    gsel = jnp.where(_rank_rows(gscore, N_GROUPS) < float(TOPK_GROUPS), 1.0, 0.0)
    emask = _dot(exp_ref[...], gsel.astype(MXU_DTYPE)) > 0.5
    masked = jnp.where(emask, sc, -jnp.inf)
    sel = _top_rows(masked, TOP_K) & emask
    self_ = jnp.where(sel, 1.0, 0.0)
    top_s = jnp.where(sel, s, 0.0)
    w = top_s / jnp.sum(top_s, axis=0, keepdims=True) * ROUTED_SCALE

    t_r = lax.broadcasted_iota(jnp.int32, (tm, tm), 0)
    t_c = lax.broadcasted_iota(jnp.int32, (tm, tm), 1)
    upper = jnp.where(t_r < t_c, 1.0, 0.0).astype(MXU_DTYPE)
    pref = _dot(self_.astype(MXU_DTYPE), upper)
    base = base_ref[...]
    sel_ref[...] = self_
    w_ref[...] = w
    pos_ref[...] = base + pref
    base = base + jnp.sum(self_, axis=1, keepdims=True)
    base_ref[...] = base
    cnt_ref[...] = jnp.broadcast_to(base, cnt_ref.shape)


def _mix_router(x2, ya, yb, yc, w_out, ln_g, ln_b, w_router_t, router_bias, tm):
    T, D = x2.shape
    E = N_EXPERTS
    expand = (jnp.arange(E)[:, None] // GROUP_SIZE == jnp.arange(N_GROUPS)[None, :]).astype(MXU_DTYPE)
    row = lambda i: (i, 0)
    col = lambda i: (0, i)
    c2 = lambda i: (0, 0)
    f32 = jnp.float32
    return pl.pallas_call(
        functools.partial(_mix_router_kernel, tm=tm),
        grid=(T // tm,),
        in_specs=[
            pl.BlockSpec((tm, D), row),
            pl.BlockSpec((tm, MIX_A), row),
            pl.BlockSpec((tm, CONV_CH), row),
            pl.BlockSpec((tm, MIX_C), row),
            pl.BlockSpec(w_out.shape, c2),
            pl.BlockSpec((1, D), c2),
            pl.BlockSpec((1, D), c2),
            pl.BlockSpec((E, D), c2),
            pl.BlockSpec((E, 1), c2),
            pl.BlockSpec((E, N_GROUPS), c2),
        ],
        out_specs=[
            pl.BlockSpec((tm, D), row),
            pl.BlockSpec((tm, D // _pack_factor()), row),
            pl.BlockSpec((E, tm), col),
            pl.BlockSpec((E, tm), col),
            pl.BlockSpec((E, tm), col),
            pl.BlockSpec((E, LANES), c2),
        ],
        out_shape=[
            jax.ShapeDtypeStruct((T, D), f32),
            jax.ShapeDtypeStruct((T, D // _pack_factor()), jnp.int32),
            jax.ShapeDtypeStruct((E, T), f32),
            jax.ShapeDtypeStruct((E, T), f32),
            jax.ShapeDtypeStruct((E, T), f32),
            jax.ShapeDtypeStruct((E, LANES), f32),
        ],
        scratch_shapes=[pltpu.VMEM((E, 1), f32)],
        compiler_params=_cparams(("arbitrary",)),
        name="mix_router",
    )(x2, ya, yb, yc, w_out, ln_g, ln_b, w_router_t, router_bias, expand)


def _compact_kernel(sel_ref, w_ref, pos_ref, pstart_ref, low_ref, dest_ref, wk_ref):
    sel = sel_ref[...]
    on = sel > 0.5
    rank = _dot(low_ref[...], sel.astype(MXU_DTYPE))
    row = pstart_ref[...] + pos_ref[...]
    w = w_ref[...]
    dests, ws = [], []
    for k in range(TOP_K):
        m = on & (rank == float(k))
        dests.append(jnp.sum(jnp.where(m, row, 0.0), axis=0, keepdims=True))
        ws.append(jnp.sum(jnp.where(m, w, 0.0), axis=0, keepdims=True))
    dest_ref[...] = jnp.concatenate(dests, axis=0).astype(jnp.int32)
    wk_ref[...] = jnp.concatenate(ws, axis=0)


def _compact(sel_t, w_t, pos_t, pad_start, tm):
    E, T = sel_t.shape
    lower = (jnp.arange(E)[None, :] < jnp.arange(E)[:, None]).astype(MXU_DTYPE)
    col = lambda i: (0, i)
    c2 = lambda i: (0, 0)
    return pl.pallas_call(
        _compact_kernel,
        grid=(T // tm,),
        in_specs=[pl.BlockSpec((E, tm), col), pl.BlockSpec((E, tm), col), pl.BlockSpec((E, tm), col),
                  pl.BlockSpec((E, 1), c2), pl.BlockSpec((E, E), c2)],
        out_specs=[pl.BlockSpec((TOP_K, tm), col), pl.BlockSpec((TOP_K, tm), col)],
        out_shape=[jax.ShapeDtypeStruct((TOP_K, T), jnp.int32), jax.ShapeDtypeStruct((TOP_K, T), jnp.float32)],
        compiler_params=_cparams(("arbitrary",)),
        name="route_compact",
    )(sel_t, w_t, pos_t, pad_start, lower)


def _silu(g):
    return g / (1.0 + jnp.exp(-g))


def _expert_kernel(be_ref, nv_ref, nu_ref, xs_ref, wg_ref, wu_ref, wd_ref, ys_ref, wgb_ref, wub_ref, wdb_ref):
    i = pl.program_id(0)

    @pl.when((i == 0) | (be_ref[i] != be_ref[jnp.maximum(i - 1, 0)]))
    def _():
        wgb_ref[...] = wg_ref[0].astype(MXU_DTYPE)
        wub_ref[...] = wu_ref[0].astype(MXU_DTYPE)
        wdb_ref[...] = wd_ref[0].astype(MXU_DTYPE)

    @pl.when(i < nu_ref[0])
    def _():
        live = lax.broadcasted_iota(jnp.int32, (ROW_BLOCK, 1), 0) < nv_ref[i]
        parts = [jnp.where(live, v, jnp.zeros_like(v)) for v in _unpack_rows(xs_ref[...])]
        dk = wgb_ref.shape[0] // len(parts)

        def proj(w_ref):
            acc = _dot(parts[0], w_ref[0:dk, :])
            for n in range(1, len(parts)):
                acc = acc + _dot(parts[n], w_ref[n * dk:(n + 1) * dk, :])
            return acc

        a = (_silu(proj(wgb_ref)) * proj(wub_ref)).astype(MXU_DTYPE)
        ys_ref[...] = _pack_rows(_dot(a, wdb_ref[...]))


def _experts(xs, block_e, block_valid, n_used, w_gate, w_up, w_down):
    n_rows, W = xs.shape
    D = w_gate.shape[1]
    n_blocks = n_rows // ROW_BLOCK
    blk = lambda i, be, nv, nu: (jnp.minimum(i, nu[0] - 1), 0)
    wsel = lambda i, be, nv, nu: (be[i], 0, 0)
    return pl.pallas_call(
        _expert_kernel,
        grid_spec=pltpu.PrefetchScalarGridSpec(
            num_scalar_prefetch=3,
            grid=(n_blocks,),
            in_specs=[
                pl.BlockSpec((ROW_BLOCK, W), blk),
                pl.BlockSpec((1, D, D_EXPERT), wsel),
                pl.BlockSpec((1, D, D_EXPERT), wsel),
                pl.BlockSpec((1, D_EXPERT, D), wsel),
            ],
            out_specs=pl.BlockSpec((ROW_BLOCK, W), blk),
            scratch_shapes=[pltpu.VMEM((D, D_EXPERT), MXU_DTYPE), pltpu.VMEM((D, D_EXPERT), MXU_DTYPE),
                            pltpu.VMEM((D_EXPERT, D), MXU_DTYPE)],
        ),
        out_shape=jax.ShapeDtypeStruct((n_rows, W), xs.dtype),
        compiler_params=_cparams(("arbitrary",)),
        name="experts",
    )(block_e, block_valid, n_used, xs, w_gate, w_up, w_down)


SC_CORES = 2
SC_SUBCORES = 16
SC_GATHER_ROWS = 64
COMBINE_CHUNKS = 8


def _sc_gather_rows(table, idx):
    n = idx.shape[0]
    w = table.shape[1]
    n_workers = SC_CORES * SC_SUBCORES
    per_worker = n // n_workers
    assert n % n_workers == 0 and per_worker % SC_GATHER_ROWS == 0
    mesh = plsc.VectorSubcoreMesh(core_axis_name="c", subcore_axis_name="s")

    @functools.partial(
        pl.kernel, mesh=mesh,
        out_type=jax.ShapeDtypeStruct((n, w), table.dtype),
        scratch_types=[
            pltpu.VMEM((2, SC_GATHER_ROWS), jnp.int32),
            pltpu.VMEM((2, SC_GATHER_ROWS, w), table.dtype),
            pltpu.SemaphoreType.DMA((2,)),
        ],
        name="sc_gather_rows",
    )
    def gather(table_hbm, idx_hbm, out_hbm, idx_v, rows_v, sem):
        wid = lax.axis_index("s") * SC_CORES + lax.axis_index("c")
        base = wid * per_worker
        n_steps = per_worker // SC_GATHER_ROWS

        def gather_copy(slot):
            return pltpu.make_async_copy(table_hbm.at[idx_v.at[slot]], rows_v.at[slot], sem.at[slot])

        def start(step, slot):
            pltpu.sync_copy(idx_hbm.at[pl.ds(base + step * SC_GATHER_ROWS, SC_GATHER_ROWS)], idx_v.at[slot])
            gather_copy(slot).start()

        start(0, 0)

        @pl.loop(0, n_steps, step=2)
        def _(g):
            for slot in range(2):
                step = g + slot

                @pl.when(step + 1 < n_steps)
                def _():
                    start(step + 1, 1 - slot)

                gather_copy(slot).wait()
                pltpu.sync_copy(rows_v.at[slot], out_hbm.at[pl.ds(base + step * SC_GATHER_ROWS, SC_GATHER_ROWS)])

    return gather(table, idx)


SC_SCATTER_ROWS = 64


def _sc_scatter_rows(rows, idx3, n_out):
    n_src, w = rows.shape
    n_chunks, n_dst, batch = idx3.shape
    n_workers = SC_CORES * SC_SUBCORES
    assert batch == SC_SCATTER_ROWS and n_chunks * batch == n_src and n_chunks % (2 * n_workers) == 0
    per_worker = n_chunks // n_workers
    mesh = plsc.VectorSubcoreMesh(core_axis_name="c", subcore_axis_name="s")

    @functools.partial(
        pl.kernel, mesh=mesh,
        out_type=jax.ShapeDtypeStruct((n_out, w), rows.dtype),
        scratch_types=[
            pltpu.VMEM((2, n_dst, batch), jnp.int32),
            pltpu.VMEM((2, batch, w), rows.dtype),
            pltpu.SemaphoreType.DMA((2,)),
            pltpu.SemaphoreType.DMA,
        ],
        name="sc_scatter_rows",
    )
    def scatter(rows_hbm, idx_hbm, out_hbm, idx_v, rows_v, load_sem, store_sem):
        wid = lax.axis_index("s") * SC_CORES + lax.axis_index("c")

        def load_copy(step, slot):
            c = wid * per_worker + step
            return pltpu.make_async_copy(rows_hbm.at[pl.ds(c * batch, batch)], rows_v.at[slot], load_sem.at[slot])

        def load(step, slot):
            pltpu.sync_copy(idx_hbm.at[wid * per_worker + step], idx_v.at[slot])
            load_copy(step, slot).start()

        def store_copy(slot, k):
            return pltpu.make_async_copy(rows_v.at[slot], out_hbm.at[idx_v.at[slot].at[k]], store_sem)

        load(0, 0)

        @pl.loop(0, per_worker, step=2)
        def _(g):
            for slot in range(2):
                step = g + slot

                @pl.when(step + 1 < per_worker)
                def _():
                    load(step + 1, 1 - slot)

                load_copy(step, slot).wait()
                for k in range(n_dst):
                    store_copy(slot, k).start()
                for k in range(n_dst):
                    store_copy(slot, k).wait()

    return scatter(rows, idx3)


def _combine2_kernel(wk_ref, x1_ref, g_ref_rows, wsg_ref, wsu_ref, wsd_ref, g_ref, b_ref, o_ref):
    x1 = x1_ref[...]
    xb = x1.astype(MXU_DTYPE)
    a = (_silu(_dot(xb, wsg_ref[...])) * _dot(xb, wsu_ref[...])).astype(MXU_DTYPE)
    shared = _dot(a, wsd_ref[...])
    wk = wk_ref[...].T
    groups = [wk[:, 0:1] * v for v in _unpack_rows_f32(g_ref_rows[0])]
    for k in range(1, TOP_K):
        groups = [g + wk[:, k:k + 1] * v for g, v in zip(groups, _unpack_rows_f32(g_ref_rows[k]))]
    routed = jnp.concatenate(groups, axis=1)
    o_ref[...] = _layer_norm(ALPHA * x1 + (routed + shared), g_ref[...], b_ref[...])


def _combine2_kernel_into(wk_ref, x1_ref, g_ref_rows, wsg_ref, wsu_ref, wsd_ref, g_ref, b_ref, prev_ref, o_ref):
    del prev_ref
    _combine2_kernel(wk_ref, x1_ref, g_ref_rows, wsg_ref, wsu_ref, wsd_ref, g_ref, b_ref, o_ref)


def _combine2(wk_t, x1, gathered, w_sg, w_su, w_sd, ln_g, ln_b, tc, chunk, prev):
    T, D = x1.shape
    _, t_chunk, W = gathered.shape
    base = chunk * (t_chunk // tc)
    row = lambda i: (base + i, 0)
    c2 = lambda i: (0, 0)
    in_specs = [
        pl.BlockSpec((TOP_K, tc), lambda i: (0, base + i)),
        pl.BlockSpec((tc, D), row),
        pl.BlockSpec((TOP_K, tc, W), lambda i: (0, i, 0)),
        pl.BlockSpec(w_sg.shape, c2),
        pl.BlockSpec(w_su.shape, c2),
        pl.BlockSpec(w_sd.shape, c2),
        pl.BlockSpec((1, D), c2),
        pl.BlockSpec((1, D), c2),
    ]
    args = [wk_t, x1, gathered, w_sg, w_su, w_sd, ln_g, ln_b]
    if prev is None:
        body, aliases = _combine2_kernel, {}
    else:
        body, aliases = _combine2_kernel_into, {len(args): 0}
        in_specs.append(pl.BlockSpec(memory_space=pl.ANY))
        args.append(prev)
    return pl.pallas_call(
        body,
        grid=(t_chunk // tc,),
        in_specs=in_specs,
        out_specs=pl.BlockSpec((tc, D), row),
        out_shape=jax.ShapeDtypeStruct((T, D), jnp.float32),
        input_output_aliases=aliases,
        compiler_params=_cparams(("arbitrary",)),
        name="combine",
    )(*args)


def _split_w_in(w_in):
    bf = MXU_DTYPE
    o_kv = Q_RANK
    o_ki = o_kv + KV_RANK
    o_iw = o_ki + IDX_DIM
    o_rest = o_iw + N_IDX_HEADS
    w_main = jnp.concatenate([w_in[:, :o_ki], w_in[:, o_rest:]], axis=1).astype(bf)
    w_small = jnp.pad(w_in[:, o_ki:o_rest], ((0, 0), (0, LANES - IDX_DIM - N_IDX_HEADS))).astype(bf)
    return w_main, w_small


def _stages(x, mem, w_in, q_norm_g, kv_norm_g, w_uq, w_uk, w_uv, w_qidx, rel_bias, conv_w, w_mem_k, w_mem_v, w_out, ln1_g, ln1_b, w_router, router_bias, w_e_gate, w_e_up, w_e_down, w_s_gate, w_s_up, w_s_down, ln2_g, ln2_b, upto=None):
    B, S, D = x.shape
    T = B * S
    bf = MXU_DTYPE
    l = 0
    res = {}
    x2 = x.reshape(T, D)
    w_main, w_small = _split_w_in(w_in[l])
    cq, ckv, ckvt, kidx, iwt, yb, yc = _proj(
        x2, mem, w_main, w_small, q_norm_g[l].reshape(1, -1), kv_norm_g[l].reshape(1, -1), conv_w[l],
        w_mem_k[l].astype(bf), w_mem_v[l].astype(bf), B, S, tm=min(512, S))
    res.update(c_q=cq, c_kv=ckv, k_idx=kidx, y_b=yb, y_c=yc,
               idx_w=jnp.swapaxes(iwt, 1, 2) / (N_IDX_HEADS ** -0.5 * IDX_DIM ** -0.5))
    if upto == "proj":
        return res
    bias_t = _bias_tiles(rel_bias)
    ya = _dsa(cq, iwt, kidx, ckv, ckvt,
              w_qidx[l].reshape(Q_RANK, -1).astype(bf), w_uq[l].reshape(Q_RANK, -1).astype(bf),
              jnp.transpose(w_uk[l], (1, 0, 2)).astype(bf), jnp.transpose(w_uv[l], (1, 2, 0)).astype(bf),
              bias_t, B, S)
    res.update(y_a=ya)
    if upto == "dsa":
        return res

    x1, x1p, sel_t, w_t, pos_t, cnt = _mix_router(
        x2, ya, yb, yc, w_out[l].astype(bf), ln1_g[l].reshape(1, -1), ln1_b[l].reshape(1, -1),
        w_router[l].T, router_bias[l].reshape(-1, 1), tm=min(512, T))
    res.update(x1=x1)

    counts = cnt[:, 0].astype(jnp.int32)
    padded = (counts + ROW_BLOCK - 1) // ROW_BLOCK * ROW_BLOCK
    pad_end = jnp.cumsum(padded)
    pad_start = pad_end - padded
    n_blocks = -(-(T * TOP_K) // ROW_BLOCK) + N_EXPERTS
    n_rows = n_blocks * ROW_BLOCK
    block_start = jnp.arange(n_blocks, dtype=jnp.int32) * ROW_BLOCK
    block_e = jnp.minimum(jnp.sum((pad_end[None, :] <= block_start[:, None]).astype(jnp.int32), axis=1),
                          N_EXPERTS - 1)
    n_used = (pad_end[-1:] // ROW_BLOCK).astype(jnp.int32)

    dest_t, wk_t = _compact(sel_t, w_t, pos_t, pad_start.astype(jnp.float32).reshape(-1, 1), tm=min(512, T))
    block_valid = jnp.clip((pad_start + counts)[block_e] - block_start, 0, ROW_BLOCK).astype(jnp.int32)
    bt = SC_SCATTER_ROWS
    idx3 = jnp.transpose(dest_t.reshape(TOP_K, T // bt, bt), (1, 0, 2))
    xs = _sc_scatter_rows(x1p, idx3, n_rows)
    ys = _experts(xs, block_e, block_valid, n_used, w_e_gate[l], w_e_up[l], w_e_down[l])
    n_chunks = COMBINE_CHUNKS if T % (COMBINE_CHUNKS * 256) == 0 else 1
    t_chunk = T // n_chunks
    out = None
    for c in range(n_chunks):
        idx_c = dest_t[:, c * t_chunk:(c + 1) * t_chunk].reshape(-1)
        gathered = _sc_gather_rows(ys, idx_c).reshape(TOP_K, t_chunk, -1)
        out = _combine2(wk_t, x1, gathered, w_s_gate[l].astype(bf), w_s_up[l].astype(bf), w_s_down[l].astype(bf),
                        ln2_g[l].reshape(1, -1), ln2_b[l].reshape(1, -1), tc=min(256, t_chunk), chunk=c, prev=out)
    res.update(out=out.reshape(B, S, D))
    return res


def kernel(x, mem, w_in, q_norm_g, kv_norm_g, w_uq, w_uk, w_uv, w_qidx, rel_bias, conv_w, w_mem_k, w_mem_v, w_out, ln1_g, ln1_b, w_router, router_bias, w_e_gate, w_e_up, w_e_down, w_s_gate, w_s_up, w_s_down, ln2_g, ln2_b):
    return _stages(x, mem, w_in, q_norm_g, kv_norm_g, w_uq, w_uk, w_uv, w_qidx, rel_bias, conv_w, w_mem_k, w_mem_v, w_out, ln1_g, ln1_b, w_router, router_bias, w_e_gate, w_e_up, w_e_down, w_s_gate, w_s_up, w_s_down, ln2_g, ln2_b)["out"]
```

```python
import functools
import math

import jax
import jax.numpy as jnp
from jax import lax
from jax.experimental import pallas as pl
from jax.experimental.pallas import tpu as pltpu
from jax.experimental.pallas import tpu_sc as plsc

N_HEADS_A = 8
HEAD_DIM = 64
Q_RANK = 256
KV_RANK = 128
N_IDX_HEADS = 8
IDX_DIM = 64
TOPK_MAX = 256
REL_BUCKETS = 32
REL_MAX_DIST = 128
CONV_CH = 256
CONV_WIDTH = 3
N_MEM_HEADS = 4
MIX_A = N_HEADS_A * HEAD_DIM
MIX_C = N_MEM_HEADS * HEAD_DIM
N_EXPERTS = 64
N_GROUPS = 8
GROUP_SIZE = N_EXPERTS // N_GROUPS
TOPK_GROUPS = 4
TOP_K = 8
D_EXPERT = 256
ROUTED_SCALE = 2.5
DEPTH = 1
ALPHA = (2.0 * DEPTH) ** 0.25
LN_EPS = 1e-5
RMS_EPS = 1e-6
LOG2_E = math.log2(math.e)

LANES = 128
SUBLANES = 8
QB = 128
F32_LOWEST = -3.4028234663852886e38
VMEM_LIMIT = 56 * 1024 * 1024
MXU_DTYPE = jnp.bfloat16
ROW_BLOCK = 1024

_NT = (((1,), (1,)), ((), ()))


def _dot(a, b):
    return jnp.dot(a, b, preferred_element_type=jnp.float32)


def _dot_nt(a, b):
    return lax.dot_general(a, b, _NT, preferred_element_type=jnp.float32)


def _cparams(sem):
    return pltpu.CompilerParams(dimension_semantics=sem, vmem_limit_bytes=VMEM_LIMIT)


def _bias_kernel(rb_ref, o_ref):
    s = lax.broadcasted_iota(jnp.int32, (QB, QB), 0)
    t = lax.broadcasted_iota(jnp.int32, (QB, QB), 1)
    max_exact = REL_BUCKETS // 2
    for tile in range(3):
        n = jnp.maximum(t - s + (2 - tile) * QB, 0)
        nf = jnp.maximum(n.astype(jnp.float32), 1.0)
        large = max_exact + (jnp.log(nf / max_exact) / math.log(REL_MAX_DIST / max_exact)
                             * (REL_BUCKETS - max_exact)).astype(jnp.int32)
        large = jnp.minimum(large, REL_BUCKETS - 1)
        bucket = jnp.where(n < max_exact, n, large)
        for h in range(N_HEADS_A):
            acc = jnp.zeros((QB, QB), jnp.float32)
            for b in range(REL_BUCKETS):
                acc = jnp.where(bucket == b, rb_ref[b, h], acc)
            o_ref[tile, h] = acc * LOG2_E


def _bias_tiles(rel_bias):
    return pl.pallas_call(
        _bias_kernel,
        in_specs=[pl.BlockSpec(memory_space=pltpu.SMEM)],
        out_specs=pl.BlockSpec(memory_space=pltpu.VMEM),
        out_shape=jax.ShapeDtypeStruct((3, N_HEADS_A, QB, QB), jnp.float32),
        name="bias_tiles",
    )(rel_bias)


def _proj_kernel(x_ref, mem_ref, wm_ref, qg_ref, kvg_ref, cw_ref, wmk_ref, wmv_ref,
                 cq_ref, ckv_ref, ckvt_ref, kidx_ref, iwt_ref, yb_ref, yc_ref,
                 carry_ref, mk_ref, mv_ref, *, tm):
    si = pl.program_id(1)

    @pl.when(si == 0)
    def _():
        carry_ref[...] = jnp.zeros_like(carry_ref)
        mb = mem_ref[0].astype(MXU_DTYPE)
        mk_ref[...] = _dot(mb, wmk_ref[...]).astype(MXU_DTYPE)
        mv_ref[...] = _dot(mb, wmv_ref[...]).astype(MXU_DTYPE)

    xb = x_ref[...].astype(MXU_DTYPE)
    p = _dot(xb, wm_ref[...])
    small = p[:, p.shape[1] - LANES:]

    o = 0
    cq = p[:, o:o + Q_RANK]; o += Q_RANK
    ckv = p[:, o:o + KV_RANK]; o += KV_RANK
    g_b = p[:, o:o + CONV_CH]; o += CONV_CH
    g_c = p[:, o:o + CONV_CH]; o += CONV_CH
    h_c = p[:, o:o + CONV_CH]; o += CONV_CH
    q_mem = p[:, o:o + MIX_C]

    cq = cq * lax.rsqrt(jnp.mean(cq * cq, axis=-1, keepdims=True) + RMS_EPS) * qg_ref[...]
    ckv = ckv * lax.rsqrt(jnp.mean(ckv * ckv, axis=-1, keepdims=True) + RMS_EPS) * kvg_ref[...]
    cq_ref[...] = cq.astype(MXU_DTYPE)
    ckv_b = ckv.astype(MXU_DTYPE)
    ckv_ref[...] = ckv_b
    ckvt_ref[0] = ckv.T.astype(MXU_DTYPE)

    kidx_ref[...] = small[:, :IDX_DIM].astype(MXU_DTYPE)
    small_t = small.T
    iwt_ref[0] = small_t[IDX_DIM:IDX_DIM + N_IDX_HEADS, :] * (N_IDX_HEADS ** -0.5 * IDX_DIM ** -0.5)

    u = g_c * h_c
    rows = lax.broadcasted_iota(jnp.int32, (tm, 1), 0)
    c6 = carry_ref[SUBLANES - 2:SUBLANES - 1, :]
    c7 = carry_ref[SUBLANES - 1:SUBLANES, :]
    u1 = jnp.where(rows == 0, c7, pltpu.roll(u, 1, 0))
    u2 = jnp.where(rows == 0, c6, jnp.where(rows == 1, c7, pltpu.roll(u, 2, 0)))
    y = cw_ref[0:1, :] * u2
    y = y + cw_ref[1:2, :] * u1
    y = y + cw_ref[2:3, :] * u
    yb_ref[...] = (g_b * y).astype(MXU_DTYPE)
    carry_ref[...] = u[tm - SUBLANES:, :]

    qm = q_mem.astype(MXU_DTYPE)
    outs = []
    for h in range(N_MEM_HEADS):
        sl = slice(h * HEAD_DIM, (h + 1) * HEAD_DIM)
        lg = _dot_nt(qm[:, sl], mk_ref[:, sl]) * (HEAD_DIM ** -0.5)
        lg = lg - jnp.max(lg, axis=-1, keepdims=True)
        e = jnp.exp(lg)
        pr = e / jnp.sum(e, axis=-1, keepdims=True)
        outs.append(_dot(pr.astype(MXU_DTYPE), mv_ref[:, sl]))
    yc_ref[...] = jnp.concatenate(outs, axis=-1).astype(MXU_DTYPE)


def _proj(x2, mem, w_main, q_g, kv_g, conv_w, w_mk, w_mv, B, S, tm):
    T, D = x2.shape
    n_mem = mem.shape[1]
    ns = S // tm
    row = lambda b, s: (b * ns + s, 0)
    const2 = lambda b, s: (0, 0)
    bf = MXU_DTYPE
    return pl.pallas_call(
        functools.partial(_proj_kernel, tm=tm),
        grid=(B, ns),
        in_specs=[
            pl.BlockSpec((tm, D), row),
            pl.BlockSpec((1, n_mem, D), lambda b, s: (b, 0, 0)),
            pl.BlockSpec(w_main.shape, const2),
            pl.BlockSpec(q_g.shape, const2),
            pl.BlockSpec(kv_g.shape, const2),
            pl.BlockSpec(conv_w.shape, const2),
            pl.BlockSpec(w_mk.shape, const2),
            pl.BlockSpec(w_mv.shape, const2),
        ],
        out_specs=[
            pl.BlockSpec((tm, Q_RANK), row),
            pl.BlockSpec((tm, KV_RANK), row),
            pl.BlockSpec((1, KV_RANK, tm), lambda b, s: (b, 0, s)),
            pl.BlockSpec((tm, IDX_DIM), row),
            pl.BlockSpec((1, N_IDX_HEADS, tm), lambda b, s: (b, 0, s)),
            pl.BlockSpec((tm, CONV_CH), row),
            pl.BlockSpec((tm, MIX_C), row),
        ],
        out_shape=[
            jax.ShapeDtypeStruct((T, Q_RANK), bf),
            jax.ShapeDtypeStruct((T, KV_RANK), bf),
            jax.ShapeDtypeStruct((B, KV_RANK, S), bf),
            jax.ShapeDtypeStruct((T, IDX_DIM), bf),
            jax.ShapeDtypeStruct((B, N_IDX_HEADS, S), jnp.float32),
            jax.ShapeDtypeStruct((T, CONV_CH), bf),
            jax.ShapeDtypeStruct((T, MIX_C), bf),
        ],
        scratch_shapes=[
            pltpu.VMEM((SUBLANES, CONV_CH), jnp.float32),
            pltpu.VMEM((n_mem, MIX_C), bf),
            pltpu.VMEM((n_mem, MIX_C), bf),
        ],
        compiler_params=_cparams(("arbitrary", "arbitrary")),
        name="proj",
    )(x2, mem, w_main, q_g, kv_g, conv_w, w_mk, w_mv)


def _key_to_f32(key):
    bits = jnp.where(key < 0, key ^ jnp.int32(0x7FFFFFFF), key)
    return pltpu.bitcast(bits, jnp.float32)


def _colsum8(v):
    return jnp.sum(v.reshape(QB // SUBLANES, SUBLANES, QB), axis=0)


def _colmax8(v):
    return jnp.max(v.reshape(QB // SUBLANES, SUBLANES, QB), axis=0)


UNROLL_WIDTHS = (8, 4, 2, 1)


def _dsa_kernel(cq_ref, iwt_ref, kidx_ref, ckv_ref, ckvt_ref, wqi_ref, wuq_ref, wuk_ref, wuvt_ref,
                bias_ref, o_ref, wfold_ref, qidx_ref, qlat_ref, score_ref, logit_ref, acc_ref,
                *, k_sel, idx_bits):
    i = pl.program_id(1)
    f32 = jnp.float32
    bf = MXU_DTYPE
    n_blocks = i + 1
    n_blocks = n_blocks + jnp.where((n_blocks % 4 == 3) & (n_blocks < pl.num_programs(1)), 1, 0)
    s_loc = lax.broadcasted_iota(jnp.int32, (QB, QB), 0)
    t_glob = i * QB + lax.broadcasted_iota(jnp.int32, (QB, QB), 1)

    def blk(jb):
        return pl.multiple_of(jb * QB, QB)

    def block_loop(fn, init):
        c, start = init, 0
        for width in UNROLL_WIDTHS:
            n = (n_blocks - start) // width
            c = lax.fori_loop(0, n, lambda it, c, w=width, s=start: fn(s + it * w, w, c), c)
            start = start + n * width
        return c

    @pl.when(i == 0)
    def _():
        for h in range(N_HEADS_A):
            wfold_ref[:, h * KV_RANK:(h + 1) * KV_RANK] = (
                _dot_nt(wuq_ref[:, h * HEAD_DIM:(h + 1) * HEAD_DIM], wuk_ref[h])
                * (HEAD_DIM ** -0.5 * LOG2_E)).astype(bf)

    cq = cq_ref[...]
    q_idx = _dot(cq, wqi_ref[...]).astype(bf)
    q_lat = _dot(cq, wfold_ref[...]).astype(bf)
    for h in range(N_HEADS_A):
        qidx_ref[h * QB:(h + 1) * QB, :] = q_idx[:, h * IDX_DIM:(h + 1) * IDX_DIM]
        qlat_ref[h * QB:(h + 1) * QB, :] = q_lat[:, h * KV_RANK:(h + 1) * KV_RANK]
    iw = iwt_ref[0]

    def score_body(jb0, nb, n_pos8):
        d_blk = _dot_nt(kidx_ref[pl.ds(blk(jb0), nb * QB), :], qidx_ref[...])
        for sb in range(nb):
            off = blk(jb0 + sb)
            d_all = d_blk[sb * QB:(sb + 1) * QB, :]
            acc = jnp.maximum(d_all[:, 0:QB], 0.0) * iw[0:1, :]
            for h in range(1, N_IDX_HEADS):
                acc = acc + jnp.maximum(d_all[:, h * QB:(h + 1) * QB], 0.0) * iw[h:h + 1, :]
            sc = jnp.where(s_loc + off <= t_glob, acc + 0.0, F32_LOWEST)
            score_ref[pl.ds(off, QB), :] = sc
            n_pos8 = n_pos8 + _colsum8(jnp.where(sc >= 0.0, 1.0, 0.0))
        return n_pos8

    n_pos8 = block_loop(score_body, jnp.zeros((SUBLANES, QB), f32))

    def count_where(pred):
        def body(jb0, nb, acc):
            for sb in range(nb):
                off = blk(jb0 + sb)
                acc = acc + _colsum8(jnp.where(pred(score_ref[pl.ds(off, QB), :], off), 1.0, 0.0))
            return acc
        acc = block_loop(body, jnp.zeros((SUBLANES, QB), f32))
        return jnp.sum(acc, axis=0, keepdims=True)

    kf = float(k_sel)

    def search():
        c0 = jnp.sum(n_pos8, axis=0, keepdims=True)
        cand0 = jnp.where(c0 >= kf, jnp.int32(0), jnp.int32(-2 ** 31))
        n_ge0 = jnp.where(c0 >= kf, c0, -1.0)

        def bit_body(it, carry):
            cand, n_ge = carry
            trial = cand + lax.shift_left(jnp.int32(1), 30 - it)
            tf = _key_to_f32(trial)
            cnt = count_where(lambda sc, off: sc >= tf)
            take = cnt >= kf
            return jnp.where(take, trial, cand), jnp.where(take, cnt, n_ge)

        cand, n_ge = lax.fori_loop(0, 31, bit_body, (cand0, n_ge0))
        thr = _key_to_f32(cand)
        keep_all_ties = jnp.full((1, QB), 2 ** idx_bits - 1, jnp.int32)

        def resolve_ties():
            n_gt = count_where(lambda sc, off: sc > thr)
            n_eq = count_where(lambda sc, off: sc == thr)
            need = kf - n_gt

            def tie_search():
                def tbody(it, xcut):
                    trial = xcut + lax.shift_left(jnp.int32(1), idx_bits - 1 - it)
                    cnt = count_where(lambda sc, off: (sc == thr) & (s_loc + off < trial))
                    return jnp.where(cnt < need, trial, xcut)
                return lax.fori_loop(0, idx_bits, tbody, jnp.zeros((1, QB), jnp.int32))

            return lax.cond(jnp.max(n_eq - need) > 0.0, tie_search, lambda: keep_all_ties)

        xcut = lax.cond(jnp.max(jnp.abs(n_ge - kf)) > 0.0, resolve_ties, lambda: keep_all_ties)
        return thr, xcut

    def no_search():
        return jnp.full((1, QB), F32_LOWEST, f32), jnp.full((1, QB), 2 ** idx_bits - 1, jnp.int32)

    thr, xcut = lax.cond((i + 1) * QB > k_sel, search, no_search)

    def selection_mask(off):
        sc = score_ref[pl.ds(off, QB), :]
        s_glob = s_loc + off
        keep = ((sc > thr) | ((sc == thr) & (s_glob <= xcut))) & (s_glob <= t_glob)
        return jnp.where(keep, 0.0, -jnp.inf)

    acc_ref[...] = jnp.zeros_like(acc_ref)

    def att_body(jb0, nb, carry):
        m, l8 = list(carry[0]), list(carry[1])
        rows = nb * QB
        lg_blk = _dot_nt(ckv_ref[pl.ds(blk(jb0), rows), :], qlat_ref[...])
        blk_max = [None] * N_HEADS_A
        for sb in range(nb):
            off = blk(jb0 + sb)
            msk = selection_mask(off)
            bsel = jnp.clip(jb0 + sb - i + 2, 0, 2)
            for h in range(N_HEADS_A):
                lgh = lg_blk[sb * QB:(sb + 1) * QB, h * QB:(h + 1) * QB] + bias_ref[bsel, h] + msk
                logit_ref[sb * QB:(sb + 1) * QB, h * QB:(h + 1) * QB] = lgh
                cm = _colmax8(lgh)
                blk_max[h] = cm if blk_max[h] is None else jnp.maximum(blk_max[h], cm)
        ps, scales = [], []
        for h in range(N_HEADS_A):
            m_new = jnp.maximum(m[h], jnp.max(blk_max[h], axis=0, keepdims=True))
            m_ref = jnp.where(m_new == -jnp.inf, 0.0, m_new)
            p = jnp.exp2(logit_ref[0:rows, h * QB:(h + 1) * QB] - m_ref)
            scale = jnp.exp2(m[h] - m_ref)
            l8[h] = l8[h] * scale + jnp.sum(p.reshape(rows // SUBLANES, SUBLANES, QB), axis=0)
            m[h] = m_new
            ps.append(p.astype(bf))
            scales.append(scale)
        pv = _dot(ckvt_ref[0, :, pl.ds(blk(jb0), rows)], jnp.concatenate(ps, axis=1))
        for h in range(N_HEADS_A):
            hs = slice(h * QB, (h + 1) * QB)
            acc_ref[:, hs] = acc_ref[:, hs] * scales[h] + pv[:, hs]
        return tuple(m), tuple(l8)

    _, l8 = block_loop(att_body, (tuple(jnp.full((1, QB), -jnp.inf, f32) for _ in range(N_HEADS_A)),
                                  tuple(jnp.zeros((SUBLANES, QB), f32) for _ in range(N_HEADS_A))))

    outs = []
    for h in range(N_HEADS_A):
        l_row = jnp.sum(l8[h], axis=0, keepdims=True)
        o_lat_t = (acc_ref[:, h * QB:(h + 1) * QB] / l_row).astype(bf)
        outs.append(_dot(wuvt_ref[h], o_lat_t))
    o_ref[...] = jnp.concatenate(outs, axis=0).T.astype(o_ref.dtype)


def _dsa(cq, iwt, kidx, ckv, ckvt, w_qidx, w_uq, w_uk_h, w_uvt_h, bias_tiles, B, S):
    T = cq.shape[0]
    assert S % QB == 0 and QB >= REL_MAX_DIST
    nq = S // QB
    k_sel = min(TOPK_MAX, S // 4)
    idx_bits = max(1, (S - 1).bit_length())
    c2 = lambda b, i: (0, 0)
    c3 = lambda b, i: (0, 0, 0)
    return pl.pallas_call(
        functools.partial(_dsa_kernel, k_sel=k_sel, idx_bits=idx_bits),
        grid=(B, nq),
        in_specs=[
            pl.BlockSpec((QB, Q_RANK), lambda b, i: (b * nq + i, 0)),
            pl.BlockSpec((1, N_IDX_HEADS, QB), lambda b, i: (b, 0, i)),
            pl.BlockSpec((S, IDX_DIM), lambda b, i: (b, 0)),
            pl.BlockSpec((S, KV_RANK), lambda b, i: (b, 0)),
            pl.BlockSpec((1, KV_RANK, S), lambda b, i: (b, 0, 0)),
            pl.BlockSpec(w_qidx.shape, c2),
            pl.BlockSpec(w_uq.shape, c2),
            pl.BlockSpec(w_uk_h.shape, c3),
            pl.BlockSpec(w_uvt_h.shape, c3),
            pl.BlockSpec(bias_tiles.shape, lambda b, i: (0, 0, 0, 0)),
        ],
        out_specs=pl.BlockSpec((QB, MIX_A), lambda b, i: (b * nq + i, 0)),
        out_shape=jax.ShapeDtypeStruct((T, MIX_A), MXU_DTYPE),
        scratch_shapes=[
            pltpu.VMEM((Q_RANK, N_HEADS_A * KV_RANK), MXU_DTYPE),
            pltpu.VMEM((N_IDX_HEADS * QB, IDX_DIM), MXU_DTYPE),
            pltpu.VMEM((N_HEADS_A * QB, KV_RANK), MXU_DTYPE),
            pltpu.VMEM((S, QB), jnp.float32),
            pltpu.VMEM((max(UNROLL_WIDTHS) * QB, N_HEADS_A * QB), jnp.float32),
            pltpu.VMEM((KV_RANK, N_HEADS_A * QB), jnp.float32),
        ],
        compiler_params=_cparams(("arbitrary", "arbitrary")),
        name="dsa",
    )(cq, iwt, kidx, ckv, ckvt, w_qidx, w_uq, w_uk_h, w_uvt_h, bias_tiles)


def _layer_norm(xf, g, b):
    mu = jnp.mean(xf, axis=-1, keepdims=True)
    xc = xf - mu
    var = jnp.mean(xc * xc, axis=-1, keepdims=True)
    return xc * lax.rsqrt(var + LN_EPS) * g + b


def _rank_rows(v, n):
    ri = lax.broadcasted_iota(jnp.int32, v.shape, 0)
    rank = jnp.zeros(v.shape, jnp.float32)
    for r2 in range(n):
        row = v[r2:r2 + 1, :]
        beats = (row > v) | ((row == v) & (ri > r2))
        rank = rank + jnp.where(beats, 1.0, 0.0)
    return rank


def _top_rows(v, k):
    n = v.shape[0]
    ri = lax.broadcasted_iota(jnp.int32, v.shape, 0)
    sel = jnp.zeros(v.shape, jnp.float32)
    for _ in range(k):
        m = jnp.max(v, axis=0, keepdims=True)
        first = jnp.min(jnp.where(v == m, ri, n), axis=0, keepdims=True)
        pick = ri == first
        sel = jnp.where(pick, 1.0, sel)
        v = jnp.where(pick, -jnp.inf, v)
    return sel > 0.5


def _pack_factor():
    return 4 // jnp.dtype(MXU_DTYPE).itemsize


def _pack_rows(x):
    if _pack_factor() == 1:
        return pltpu.bitcast(x, jnp.int32)
    half = x.shape[1] // 2
    b = pltpu.bitcast(x.astype(MXU_DTYPE).astype(jnp.float32), jnp.int32)
    return b[:, half:] | lax.shift_right_logical(b[:, :half], jnp.int32(16))


_HIGH_HALF = -(1 << 16)


def _unpack_rows_f32(p):
    if _pack_factor() == 1:
        return [pltpu.bitcast(p, jnp.float32)]
    lo = pltpu.bitcast(lax.shift_left(p, jnp.int32(16)), jnp.float32)
    hi = pltpu.bitcast(p & jnp.int32(_HIGH_HALF), jnp.float32)
    return [lo, hi]


def _unpack_rows(p):
    return [v.astype(MXU_DTYPE) for v in _unpack_rows_f32(p)]


def _mix_router_kernel(x_ref, ya_ref, yb_ref, yc_ref, wo_ref, g_ref, b_ref, wrt_ref, rb_ref, exp_ref,
                       x1_ref, x1p_ref, sel_ref, w_ref, pos_ref, cnt_ref, base_ref, *, tm):
    step = pl.program_id(0)
    f32 = jnp.float32

    @pl.when(step == 0)
    def _():
        base_ref[...] = jnp.zeros_like(base_ref)

    mix = _dot(ya_ref[...], wo_ref[0:MIX_A, :])
    mix = mix + _dot(yb_ref[...], wo_ref[MIX_A:MIX_A + CONV_CH, :])
    mix = mix + _dot(yc_ref[...], wo_ref[MIX_A + CONV_CH:, :])
    x1 = _layer_norm(ALPHA * x_ref[...] + mix, g_ref[...], b_ref[...])
    x1_ref[...] = x1
    x1p_ref[...] = _pack_rows(x1)

    lg = lax.dot_general(wrt_ref[...], x1, _NT, precision=lax.Precision.HIGHEST, preferred_element_type=f32)
    s = 1.0 / (1.0 + jnp.exp(-lg))
    sc = s + rb_ref[...]

    g3 = sc.reshape(N_GROUPS, GROUP_SIZE, tm)
    m1 = jnp.max(g3, axis=1, keepdims=True)
    is_m1 = g3 == m1
    n_m1 = jnp.sum(jnp.where(is_m1, 1.0, 0.0), axis=1, keepdims=True)
    m2 = jnp.max(jnp.where(is_m1, -jnp.inf, g3), axis=1, keepdims=True)
    gscore = (m1 + jnp.where(n_m1 > 1.0, m1, m2)).reshape(N_GROUPS, tm)
    gsel = jnp.where(_rank_rows(gscore, N_GROUPS) < float(TOPK_GROUPS), 1.0, 0.0)
    emask = _dot(exp_ref[...], gsel.astype(MXU_DTYPE)) > 0.5
    masked = jnp.where(emask, sc, -jnp.inf)
    sel = _top_rows(masked, TOP_K) & emask
    self_ = jnp.where(sel, 1.0, 0.0)
    top_s = jnp.where(sel, s, 0.0)
    w = top_s / jnp.sum(top_s, axis=0, keepdims=True) * ROUTED_SCALE

    t_r = lax.broadcasted_iota(jnp.int32, (tm, tm), 0)
    t_c = lax.broadcasted_iota(jnp.int32, (tm, tm), 1)
    upper = jnp.where(t_r < t_c, 1.0, 0.0).astype(MXU_DTYPE)
    pref = _dot(self_.astype(MXU_DTYPE), upper)
    base = base_ref[...]
    sel_ref[...] = self_
    w_ref[...] = w
    pos_ref[...] = base + pref
    base = base + jnp.sum(self_, axis=1, keepdims=True)
    base_ref[...] = base
    cnt_ref[...] = jnp.broadcast_to(base, cnt_ref.shape)


def _mix_router(x2, ya, yb, yc, w_out, ln_g, ln_b, w_router_t, router_bias, tm):
    T, D = x2.shape
    E = N_EXPERTS
    expand = (jnp.arange(E)[:, None] // GROUP_SIZE == jnp.arange(N_GROUPS)[None, :]).astype(MXU_DTYPE)
    row = lambda i: (i, 0)
    col = lambda i: (0, i)
    c2 = lambda i: (0, 0)
    f32 = jnp.float32
    return pl.pallas_call(
        functools.partial(_mix_router_kernel, tm=tm),
        grid=(T // tm,),
        in_specs=[
            pl.BlockSpec((tm, D), row),
            pl.BlockSpec((tm, MIX_A), row),
            pl.BlockSpec((tm, CONV_CH), row),
            pl.BlockSpec((tm, MIX_C), row),
            pl.BlockSpec(w_out.shape, c2),
            pl.BlockSpec((1, D), c2),
            pl.BlockSpec((1, D), c2),
            pl.BlockSpec((E, D), c2),
            pl.BlockSpec((E, 1), c2),
            pl.BlockSpec((E, N_GROUPS), c2),
        ],
        out_specs=[
            pl.BlockSpec((tm, D), row),
            pl.BlockSpec((tm, D // _pack_factor()), row),
            pl.BlockSpec((E, tm), col),
            pl.BlockSpec((E, tm), col),
            pl.BlockSpec((E, tm), col),
            pl.BlockSpec((E, LANES), c2),
        ],
        out_shape=[
            jax.ShapeDtypeStruct((T, D), f32),
            jax.ShapeDtypeStruct((T, D // _pack_factor()), jnp.int32),
            jax.ShapeDtypeStruct((E, T), f32),
            jax.ShapeDtypeStruct((E, T), f32),
            jax.ShapeDtypeStruct((E, T), f32),
            jax.ShapeDtypeStruct((E, LANES), f32),
        ],
        scratch_shapes=[pltpu.VMEM((E, 1), f32)],
        compiler_params=_cparams(("arbitrary",)),
        name="mix_router",
    )(x2, ya, yb, yc, w_out, ln_g, ln_b, w_router_t, router_bias, expand)


def _compact_kernel(sel_ref, w_ref, pos_ref, pstart_ref, low_ref, dest_ref, wk_ref):
    sel = sel_ref[...]
    on = sel > 0.5
    rank = _dot(low_ref[...], sel.astype(MXU_DTYPE))
    row = pstart_ref[...] + pos_ref[...]
    w = w_ref[...]
    dests, ws = [], []
    for k in range(TOP_K):
        m = on & (rank == float(k))
        dests.append(jnp.sum(jnp.where(m, row, 0.0), axis=0, keepdims=True))
        ws.append(jnp.sum(jnp.where(m, w, 0.0), axis=0, keepdims=True))
    dest_ref[...] = jnp.concatenate(dests, axis=0).astype(jnp.int32)
    wk_ref[...] = jnp.concatenate(ws, axis=0)


def _compact(sel_t, w_t, pos_t, pad_start, tm):
    E, T = sel_t.shape
    lower = (jnp.arange(E)[None, :] < jnp.arange(E)[:, None]).astype(MXU_DTYPE)
    col = lambda i: (0, i)
    c2 = lambda i: (0, 0)
    return pl.pallas_call(
        _compact_kernel,
        grid=(T // tm,),
        in_specs=[pl.BlockSpec((E, tm), col), pl.BlockSpec((E, tm), col), pl.BlockSpec((E, tm), col),
                  pl.BlockSpec((E, 1), c2), pl.BlockSpec((E, E), c2)],
        out_specs=[pl.BlockSpec((TOP_K, tm), col), pl.BlockSpec((TOP_K, tm), col)],
        out_shape=[jax.ShapeDtypeStruct((TOP_K, T), jnp.int32), jax.ShapeDtypeStruct((TOP_K, T), jnp.float32)],
        compiler_params=_cparams(("arbitrary",)),
        name="route_compact",
    )(sel_t, w_t, pos_t, pad_start, lower)


def _silu(g):
    return g / (1.0 + jnp.exp(-g))


def _expert_kernel(be_ref, nv_ref, nu_ref, xs_ref, wg_ref, wu_ref, wd_ref, ys_ref, wgb_ref, wub_ref, wdb_ref):
    i = pl.program_id(0)

    @pl.when((i == 0) | (be_ref[i] != be_ref[jnp.maximum(i - 1, 0)]))
    def _():
        wgb_ref[...] = wg_ref[0].astype(MXU_DTYPE)
        wub_ref[...] = wu_ref[0].astype(MXU_DTYPE)
        wdb_ref[...] = wd_ref[0].astype(MXU_DTYPE)

    @pl.when(i < nu_ref[0])
    def _():
        live = lax.broadcasted_iota(jnp.int32, (ROW_BLOCK, 1), 0) < nv_ref[i]
        parts = [jnp.where(live, v, jnp.zeros_like(v)) for v in _unpack_rows(xs_ref[...])]
        dk = wgb_ref.shape[0] // len(parts)

        def proj(w_ref):
            acc = _dot(parts[0], w_ref[0:dk, :])
            for n in range(1, len(parts)):
                acc = acc + _dot(parts[n], w_ref[n * dk:(n + 1) * dk, :])
            return acc

        a = (_silu(proj(wgb_ref)) * proj(wub_ref)).astype(MXU_DTYPE)
        ys_ref[...] = _pack_rows(_dot(a, wdb_ref[...]))


def _experts(xs, block_e, block_valid, n_used, w_gate, w_up, w_down):
    n_rows, W = xs.shape
    D = w_gate.shape[1]
    n_blocks = n_rows // ROW_BLOCK
    blk = lambda i, be, nv, nu: (jnp.minimum(i, nu[0] - 1), 0)
    wsel = lambda i, be, nv, nu: (be[i], 0, 0)
    return pl.pallas_call(
        _expert_kernel,
        grid_spec=pltpu.PrefetchScalarGridSpec(
            num_scalar_prefetch=3,
            grid=(n_blocks,),
            in_specs=[
                pl.BlockSpec((ROW_BLOCK, W), blk),
                pl.BlockSpec((1, D, D_EXPERT), wsel),
                pl.BlockSpec((1, D, D_EXPERT), wsel),
                pl.BlockSpec((1, D_EXPERT, D), wsel),
            ],
            out_specs=pl.BlockSpec((ROW_BLOCK, W), blk),
            scratch_shapes=[pltpu.VMEM((D, D_EXPERT), MXU_DTYPE), pltpu.VMEM((D, D_EXPERT), MXU_DTYPE),
                            pltpu.VMEM((D_EXPERT, D), MXU_DTYPE)],
        ),
        out_shape=jax.ShapeDtypeStruct((n_rows, W), xs.dtype),
        compiler_params=_cparams(("arbitrary",)),
        name="experts",
    )(block_e, block_valid, n_used, xs, w_gate, w_up, w_down)


SC_CORES = 2
SC_SUBCORES = 16
SC_GATHER_ROWS = 64
COMBINE_CHUNKS = 8


def _sc_gather_rows(table, idx):
    n = idx.shape[0]
    w = table.shape[1]
    n_workers = SC_CORES * SC_SUBCORES
    per_worker = n // n_workers
    assert n % n_workers == 0 and per_worker % SC_GATHER_ROWS == 0
    mesh = plsc.VectorSubcoreMesh(core_axis_name="c", subcore_axis_name="s")

    @functools.partial(
        pl.kernel, mesh=mesh,
        out_type=jax.ShapeDtypeStruct((n, w), table.dtype),
        scratch_types=[
            pltpu.VMEM((2, SC_GATHER_ROWS), jnp.int32),
            pltpu.VMEM((2, SC_GATHER_ROWS, w), table.dtype),
            pltpu.SemaphoreType.DMA((2,)),
        ],
        name="sc_gather_rows",
    )
    def gather(table_hbm, idx_hbm, out_hbm, idx_v, rows_v, sem):
        wid = lax.axis_index("s") * SC_CORES + lax.axis_index("c")
        base = wid * per_worker
        n_steps = per_worker // SC_GATHER_ROWS

        def gather_copy(slot):
            return pltpu.make_async_copy(table_hbm.at[idx_v.at[slot]], rows_v.at[slot], sem.at[slot])

        def start(step, slot):
            pltpu.sync_copy(idx_hbm.at[pl.ds(base + step * SC_GATHER_ROWS, SC_GATHER_ROWS)], idx_v.at[slot])
            gather_copy(slot).start()

        start(0, 0)

        @pl.loop(0, n_steps, step=2)
        def _(g):
            for slot in range(2):
                step = g + slot

                @pl.when(step + 1 < n_steps)
                def _():
                    start(step + 1, 1 - slot)

                gather_copy(slot).wait()
                pltpu.sync_copy(rows_v.at[slot], out_hbm.at[pl.ds(base + step * SC_GATHER_ROWS, SC_GATHER_ROWS)])

    return gather(table, idx)


SC_SCATTER_ROWS = 64


def _sc_scatter_rows(rows, idx3, n_out):
    n_src, w = rows.shape
    n_chunks, n_dst, batch = idx3.shape
    n_workers = SC_CORES * SC_SUBCORES
    assert batch == SC_SCATTER_ROWS and n_chunks * batch == n_src and n_chunks % (2 * n_workers) == 0
    per_worker = n_chunks // n_workers
    mesh = plsc.VectorSubcoreMesh(core_axis_name="c", subcore_axis_name="s")

    @functools.partial(
        pl.kernel, mesh=mesh,
        out_type=jax.ShapeDtypeStruct((n_out, w), rows.dtype),
        scratch_types=[
            pltpu.VMEM((2, n_dst, batch), jnp.int32),
            pltpu.VMEM((2, batch, w), rows.dtype),
            pltpu.SemaphoreType.DMA((2,)),
            pltpu.SemaphoreType.DMA,
        ],
        name="sc_scatter_rows",
    )
    def scatter(rows_hbm, idx_hbm, out_hbm, idx_v, rows_v, load_sem, store_sem):
        wid = lax.axis_index("s") * SC_CORES + lax.axis_index("c")

        def load_copy(step, slot):
            c = wid * per_worker + step
            return pltpu.make_async_copy(rows_hbm.at[pl.ds(c * batch, batch)], rows_v.at[slot], load_sem.at[slot])

        def load(step, slot):
            pltpu.sync_copy(idx_hbm.at[wid * per_worker + step], idx_v.at[slot])
            load_copy(step, slot).start()

        def store_copy(slot, k):
            return pltpu.make_async_copy(rows_v.at[slot], out_hbm.at[idx_v.at[slot].at[k]], store_sem)

        load(0, 0)

        @pl.loop(0, per_worker, step=2)
        def _(g):
            for slot in range(2):
                step = g + slot

                @pl.when(step + 1 < per_worker)
                def _():
                    load(step + 1, 1 - slot)

                load_copy(step, slot).wait()
                for k in range(n_dst):
                    store_copy(slot, k).start()
                for k in range(n_dst):
                    store_copy(slot, k).wait()

    return scatter(rows, idx3)


def _combine2_kernel(wk_ref, x1_ref, g_ref_rows, wsg_ref, wsu_ref, wsd_ref, g_ref, b_ref, o_ref):
    x1 = x1_ref[...]
    xb = x1.astype(MXU_DTYPE)
    a = (_silu(_dot(xb, wsg_ref[...])) * _dot(xb, wsu_ref[...])).astype(MXU_DTYPE)
    shared = _dot(a, wsd_ref[...])
    wk = wk_ref[...].T
    groups = [wk[:, 0:1] * v for v in _unpack_rows_f32(g_ref_rows[0])]
    for k in range(1, TOP_K):
        groups = [g + wk[:, k:k + 1] * v for g, v in zip(groups, _unpack_rows_f32(g_ref_rows[k]))]
    routed = jnp.concatenate(groups, axis=1)
    o_ref[...] = _layer_norm(ALPHA * x1 + (routed + shared), g_ref[...], b_ref[...])


def _combine2_kernel_into(wk_ref, x1_ref, g_ref_rows, wsg_ref, wsu_ref, wsd_ref, g_ref, b_ref, prev_ref, o_ref):
    del prev_ref
    _combine2_kernel(wk_ref, x1_ref, g_ref_rows, wsg_ref, wsu_ref, wsd_ref, g_ref, b_ref, o_ref)


def _combine2(wk_t, x1, gathered, w_sg, w_su, w_sd, ln_g, ln_b, tc, chunk, prev):
    T, D = x1.shape
    _, t_chunk, W = gathered.shape
    base = chunk * (t_chunk // tc)
    row = lambda i: (base + i, 0)
    c2 = lambda i: (0, 0)
    in_specs = [
        pl.BlockSpec((TOP_K, tc), lambda i: (0, base + i)),
        pl.BlockSpec((tc, D), row),
        pl.BlockSpec((TOP_K, tc, W), lambda i: (0, i, 0)),
        pl.BlockSpec(w_sg.shape, c2),
        pl.BlockSpec(w_su.shape, c2),
        pl.BlockSpec(w_sd.shape, c2),
        pl.BlockSpec((1, D), c2),
        pl.BlockSpec((1, D), c2),
    ]
    args = [wk_t, x1, gathered, w_sg, w_su, w_sd, ln_g, ln_b]
    if prev is None:
        body, aliases = _combine2_kernel, {}
    else:
        body, aliases = _combine2_kernel_into, {len(args): 0}
        in_specs.append(pl.BlockSpec(memory_space=pl.ANY))
        args.append(prev)
    return pl.pallas_call(
        body,
        grid=(t_chunk // tc,),
        in_specs=in_specs,
        out_specs=pl.BlockSpec((tc, D), row),
        out_shape=jax.ShapeDtypeStruct((T, D), jnp.float32),
        input_output_aliases=aliases,
        compiler_params=_cparams(("arbitrary",)),
        name="combine",
    )(*args)


def _split_w_in(w_in):
    o_kv = Q_RANK
    o_ki = o_kv + KV_RANK
    o_iw = o_ki + IDX_DIM
    o_rest = o_iw + N_IDX_HEADS
    w_small = jnp.pad(w_in[:, o_ki:o_rest], ((0, 0), (0, LANES - IDX_DIM - N_IDX_HEADS)))
    return jnp.concatenate([w_in[:, :o_ki], w_in[:, o_rest:], w_small], axis=1).astype(MXU_DTYPE)


def _stages(x, mem, w_in, q_norm_g, kv_norm_g, w_uq, w_uk, w_uv, w_qidx, rel_bias, conv_w, w_mem_k, w_mem_v, w_out, ln1_g, ln1_b, w_router, router_bias, w_e_gate, w_e_up, w_e_down, w_s_gate, w_s_up, w_s_down, ln2_g, ln2_b, upto=None):
    B, S, D = x.shape
    T = B * S
    bf = MXU_DTYPE
    l = 0
    res = {}
    x2 = x.reshape(T, D)
    cq, ckv, ckvt, kidx, iwt, yb, yc = _proj(
        x2, mem, _split_w_in(w_in[l]), q_norm_g[l].reshape(1, -1), kv_norm_g[l].reshape(1, -1), conv_w[l],
        w_mem_k[l].astype(bf), w_mem_v[l].astype(bf), B, S, tm=min(512, S))
    res.update(c_q=cq, c_kv=ckv, k_idx=kidx, y_b=yb, y_c=yc,
               idx_w=jnp.swapaxes(iwt, 1, 2) / (N_IDX_HEADS ** -0.5 * IDX_DIM ** -0.5))
    if upto == "proj":
        return res
    bias_t = _bias_tiles(rel_bias)
    ya = _dsa(cq, iwt, kidx, ckv, ckvt,
              w_qidx[l].reshape(Q_RANK, -1).astype(bf), w_uq[l].reshape(Q_RANK, -1).astype(bf),
              jnp.transpose(w_uk[l], (1, 0, 2)).astype(bf), jnp.transpose(w_uv[l], (1, 2, 0)).astype(bf),
              bias_t, B, S)
    res.update(y_a=ya)
    if upto == "dsa":
        return res

    x1, x1p, sel_t, w_t, pos_t, cnt = _mix_router(
        x2, ya, yb, yc, w_out[l].astype(bf), ln1_g[l].reshape(1, -1), ln1_b[l].reshape(1, -1),
        w_router[l].T, router_bias[l].reshape(-1, 1), tm=min(512, T))
    res.update(x1=x1)

    counts = cnt[:, 0].astype(jnp.int32)
    padded = (counts + ROW_BLOCK - 1) // ROW_BLOCK * ROW_BLOCK
    pad_end = jnp.cumsum(padded)
    pad_start = pad_end - padded
    n_blocks = -(-(T * TOP_K) // ROW_BLOCK) + N_EXPERTS
    n_rows = n_blocks * ROW_BLOCK
    block_start = jnp.arange(n_blocks, dtype=jnp.int32) * ROW_BLOCK
    block_e = jnp.minimum(jnp.sum((pad_end[None, :] <= block_start[:, None]).astype(jnp.int32), axis=1),
                          N_EXPERTS - 1)
    n_used = (pad_end[-1:] // ROW_BLOCK).astype(jnp.int32)

    dest_t, wk_t = _compact(sel_t, w_t, pos_t, pad_start.astype(jnp.float32).reshape(-1, 1), tm=min(512, T))
    block_valid = jnp.clip((pad_start + counts)[block_e] - block_start, 0, ROW_BLOCK).astype(jnp.int32)
    bt = SC_SCATTER_ROWS
    idx3 = jnp.transpose(dest_t.reshape(TOP_K, T // bt, bt), (1, 0, 2))
    xs = _sc_scatter_rows(x1p, idx3, n_rows)
    ys = _experts(xs, block_e, block_valid, n_used, w_e_gate[l], w_e_up[l], w_e_down[l])
    n_chunks = COMBINE_CHUNKS if T % (COMBINE_CHUNKS * 256) == 0 else 1
    t_chunk = T // n_chunks
    out = None
    for c in range(n_chunks):
        idx_c = dest_t[:, c * t_chunk:(c + 1) * t_chunk].reshape(-1)
        gathered = _sc_gather_rows(ys, idx_c).reshape(TOP_K, t_chunk, -1)
        out = _combine2(wk_t, x1, gathered, w_s_gate[l].astype(bf), w_s_up[l].astype(bf), w_s_down[l].astype(bf),
                        ln2_g[l].reshape(1, -1), ln2_b[l].reshape(1, -1), tc=min(256, t_chunk), chunk=c, prev=out)
    res.update(out=out.reshape(B, S, D))
    return res


def kernel(x, mem, w_in, q_norm_g, kv_norm_g, w_uq, w_uk, w_uv, w_qidx, rel_bias, conv_w, w_mem_k, w_mem_v, w_out, ln1_g, ln1_b, w_router, router_bias, w_e_gate, w_e_up, w_e_down, w_s_gate, w_s_up, w_s_down, ln2_g, ln2_b):
    return _stages(x, mem, w_in, q_norm_g, kv_norm_g, w_uq, w_uk, w_uv, w_qidx, rel_bias, conv_w, w_mem_k, w_mem_v, w_out, ln1_g, ln1_b, w_router, router_bias, w_e_gate, w_e_up, w_e_down, w_s_gate, w_s_up, w_s_down, ln2_g, ln2_b)["out"]
```

```python
import functools
import math

import jax
import jax.numpy as jnp
from jax import lax
from jax.experimental import pallas as pl
from jax.experimental.pallas import tpu as pltpu
from jax.experimental.pallas import tpu_sc as plsc

N_HEADS_A = 8
HEAD_DIM = 64
Q_RANK = 256
KV_RANK = 128
N_IDX_HEADS = 8
IDX_DIM = 64
TOPK_MAX = 256
REL_BUCKETS = 32
REL_MAX_DIST = 128
CONV_CH = 256
CONV_WIDTH = 3
N_MEM_HEADS = 4
MIX_A = N_HEADS_A * HEAD_DIM
MIX_C = N_MEM_HEADS * HEAD_DIM
N_EXPERTS = 64
N_GROUPS = 8
GROUP_SIZE = N_EXPERTS // N_GROUPS
TOPK_GROUPS = 4
TOP_K = 8
D_EXPERT = 256
ROUTED_SCALE = 2.5
DEPTH = 1
ALPHA = (2.0 * DEPTH) ** 0.25
LN_EPS = 1e-5
RMS_EPS = 1e-6
LOG2_E = math.log2(math.e)

LANES = 128
SUBLANES = 8
QB = 128
F32_LOWEST = -3.4028234663852886e38
VMEM_LIMIT = 56 * 1024 * 1024
MXU_DTYPE = jnp.bfloat16
ROW_BLOCK = 1024

_NT = (((1,), (1,)), ((), ()))


def _dot(a, b):
    return jnp.dot(a, b, preferred_element_type=jnp.float32)


def _dot_nt(a, b):
    return lax.dot_general(a, b, _NT, preferred_element_type=jnp.float32)


def _cparams(sem):
    return pltpu.CompilerParams(dimension_semantics=sem, vmem_limit_bytes=VMEM_LIMIT)


def _bias_kernel(rb_ref, o_ref):
    s = lax.broadcasted_iota(jnp.int32, (QB, QB), 0)
    t = lax.broadcasted_iota(jnp.int32, (QB, QB), 1)
    max_exact = REL_BUCKETS // 2
    for tile in range(3):
        n = jnp.maximum(t - s + (2 - tile) * QB, 0)
        nf = jnp.maximum(n.astype(jnp.float32), 1.0)
        large = max_exact + (jnp.log(nf / max_exact) / math.log(REL_MAX_DIST / max_exact)
                             * (REL_BUCKETS - max_exact)).astype(jnp.int32)
        large = jnp.minimum(large, REL_BUCKETS - 1)
        bucket = jnp.where(n < max_exact, n, large)
        for h in range(N_HEADS_A):
            acc = jnp.zeros((QB, QB), jnp.float32)
            for b in range(REL_BUCKETS):
                acc = jnp.where(bucket == b, rb_ref[b, h], acc)
            o_ref[tile, h] = acc * LOG2_E


def _bias_tiles(rel_bias):
    return pl.pallas_call(
        _bias_kernel,
        in_specs=[pl.BlockSpec(memory_space=pltpu.SMEM)],
        out_specs=pl.BlockSpec(memory_space=pltpu.VMEM),
        out_shape=jax.ShapeDtypeStruct((3, N_HEADS_A, QB, QB), jnp.float32),
        name="bias_tiles",
    )(rel_bias)


def _proj_kernel(x_ref, mem_ref, wm_ref, qg_ref, kvg_ref, cw_ref, wmk_ref, wmv_ref,
                 cq_ref, ckv_ref, ckvt_ref, kidx_ref, iwt_ref, yb_ref, yc_ref,
                 carry_ref, mk_ref, mv_ref, *, tm):
    si = pl.program_id(1)

    @pl.when(si == 0)
    def _():
        carry_ref[...] = jnp.zeros_like(carry_ref)
        mb = mem_ref[0].astype(MXU_DTYPE)
        mk_ref[...] = _dot(mb, wmk_ref[...]).astype(MXU_DTYPE)
        mv_ref[...] = _dot(mb, wmv_ref[...]).astype(MXU_DTYPE)

    xb = x_ref[...].astype(MXU_DTYPE)
    p = _dot(xb, wm_ref[...])
    small = p[:, p.shape[1] - LANES:]

    o = 0
    cq = p[:, o:o + Q_RANK]; o += Q_RANK
    ckv = p[:, o:o + KV_RANK]; o += KV_RANK
    g_b = p[:, o:o + CONV_CH]; o += CONV_CH
    g_c = p[:, o:o + CONV_CH]; o += CONV_CH
    h_c = p[:, o:o + CONV_CH]; o += CONV_CH
    q_mem = p[:, o:o + MIX_C]

    cq = cq * lax.rsqrt(jnp.mean(cq * cq, axis=-1, keepdims=True) + RMS_EPS) * qg_ref[...]
    ckv = ckv * lax.rsqrt(jnp.mean(ckv * ckv, axis=-1, keepdims=True) + RMS_EPS) * kvg_ref[...]
    cq_ref[...] = cq.astype(MXU_DTYPE)
    ckv_b = ckv.astype(MXU_DTYPE)
    ckv_ref[...] = ckv_b
    ckvt_ref[0] = ckv.T.astype(MXU_DTYPE)

    kidx_ref[...] = small[:, :IDX_DIM].astype(MXU_DTYPE)
    small_t = small.T
    iwt_ref[0] = small_t[IDX_DIM:IDX_DIM + N_IDX_HEADS, :] * (N_IDX_HEADS ** -0.5 * IDX_DIM ** -0.5)

    u = g_c * h_c
    rows = lax.broadcasted_iota(jnp.int32, (tm, 1), 0)
    c6 = carry_ref[SUBLANES - 2:SUBLANES - 1, :]
    c7 = carry_ref[SUBLANES - 1:SUBLANES, :]
    u1 = jnp.where(rows == 0, c7, pltpu.roll(u, 1, 0))
    u2 = jnp.where(rows == 0, c6, jnp.where(rows == 1, c7, pltpu.roll(u, 2, 0)))
    y = cw_ref[0:1, :] * u2
    y = y + cw_ref[1:2, :] * u1
    y = y + cw_ref[2:3, :] * u
    yb_ref[...] = (g_b * y).astype(MXU_DTYPE)
    carry_ref[...] = u[tm - SUBLANES:, :]

    qm = q_mem.astype(MXU_DTYPE)
    outs = []
    for h in range(N_MEM_HEADS):
        sl = slice(h * HEAD_DIM, (h + 1) * HEAD_DIM)
        lg = _dot_nt(qm[:, sl], mk_ref[:, sl]) * (HEAD_DIM ** -0.5)
        lg = lg - jnp.max(lg, axis=-1, keepdims=True)
        e = jnp.exp(lg)
        pr = e / jnp.sum(e, axis=-1, keepdims=True)
        outs.append(_dot(pr.astype(MXU_DTYPE), mv_ref[:, sl]))
    yc_ref[...] = jnp.concatenate(outs, axis=-1).astype(MXU_DTYPE)


def _proj(x2, mem, w_main, q_g, kv_g, conv_w, w_mk, w_mv, B, S, tm):
    T, D = x2.shape
    n_mem = mem.shape[1]
    ns = S // tm
    row = lambda b, s: (b * ns + s, 0)
    const2 = lambda b, s: (0, 0)
    bf = MXU_DTYPE
    return pl.pallas_call(
        functools.partial(_proj_kernel, tm=tm),
        grid=(B, ns),
        in_specs=[
            pl.BlockSpec((tm, D), row),
            pl.BlockSpec((1, n_mem, D), lambda b, s: (b, 0, 0)),
            pl.BlockSpec(w_main.shape, const2),
            pl.BlockSpec(q_g.shape, const2),
            pl.BlockSpec(kv_g.shape, const2),
            pl.BlockSpec(conv_w.shape, const2),
            pl.BlockSpec(w_mk.shape, const2),
            pl.BlockSpec(w_mv.shape, const2),
        ],
        out_specs=[
            pl.BlockSpec((tm, Q_RANK), row),
            pl.BlockSpec((tm, KV_RANK), row),
            pl.BlockSpec((1, KV_RANK, tm), lambda b, s: (b, 0, s)),
            pl.BlockSpec((tm, IDX_DIM), row),
            pl.BlockSpec((1, N_IDX_HEADS, tm), lambda b, s: (b, 0, s)),
            pl.BlockSpec((tm, CONV_CH), row),
            pl.BlockSpec((tm, MIX_C), row),
        ],
        out_shape=[
            jax.ShapeDtypeStruct((T, Q_RANK), bf),
            jax.ShapeDtypeStruct((T, KV_RANK), bf),
            jax.ShapeDtypeStruct((B, KV_RANK, S), bf),
            jax.ShapeDtypeStruct((T, IDX_DIM), bf),
            jax.ShapeDtypeStruct((B, N_IDX_HEADS, S), jnp.float32),
            jax.ShapeDtypeStruct((T, CONV_CH), bf),
            jax.ShapeDtypeStruct((T, MIX_C), bf),
        ],
        scratch_shapes=[
            pltpu.VMEM((SUBLANES, CONV_CH), jnp.float32),
            pltpu.VMEM((n_mem, MIX_C), bf),
            pltpu.VMEM((n_mem, MIX_C), bf),
        ],
        compiler_params=_cparams(("arbitrary", "arbitrary")),
        name="proj",
    )(x2, mem, w_main, q_g, kv_g, conv_w, w_mk, w_mv)


def _key_to_f32(key):
    bits = jnp.where(key < 0, key ^ jnp.int32(0x7FFFFFFF), key)
    return pltpu.bitcast(bits, jnp.float32)


def _colsum8(v):
    return jnp.sum(v.reshape(QB // SUBLANES, SUBLANES, QB), axis=0)


def _colmax8(v):
    return jnp.max(v.reshape(QB // SUBLANES, SUBLANES, QB), axis=0)


UNROLL_WIDTHS = (8, 4, 2, 1)


def _dsa_kernel(cq_ref, iwt_ref, kidx_ref, ckv_ref, ckvt_ref, wqi_ref, wuq_ref, wuk_ref, wuvt_ref,
                bias_ref, o_ref, wfold_ref, qidx_ref, qlat_ref, score_ref, logit_ref, acc_ref,
                *, k_sel, idx_bits):
    i = pl.program_id(1)
    f32 = jnp.float32
    bf = MXU_DTYPE
    n_blocks = i + 1
    n_blocks = n_blocks + jnp.where((n_blocks % 4 == 3) & (n_blocks < pl.num_programs(1)), 1, 0)
    s_loc = lax.broadcasted_iota(jnp.int32, (QB, QB), 0)
    t_glob = i * QB + lax.broadcasted_iota(jnp.int32, (QB, QB), 1)

    def blk(jb):
        return pl.multiple_of(jb * QB, QB)

    def block_loop(fn, init):
        c, start = init, 0
        for width in UNROLL_WIDTHS:
            n = (n_blocks - start) // width
            c = lax.fori_loop(0, n, lambda it, c, w=width, s=start: fn(s + it * w, w, c), c)
            start = start + n * width
        return c

    @pl.when(i == 0)
    def _():
        for h in range(N_HEADS_A):
            wfold_ref[:, h * KV_RANK:(h + 1) * KV_RANK] = (
                _dot_nt(wuq_ref[:, h * HEAD_DIM:(h + 1) * HEAD_DIM], wuk_ref[h])
                * (HEAD_DIM ** -0.5 * LOG2_E)).astype(bf)

    cq = cq_ref[...]
    q_idx = _dot(cq, wqi_ref[...]).astype(bf)
    q_lat = _dot(cq, wfold_ref[...]).astype(bf)
    for h in range(N_HEADS_A):
        qidx_ref[h * QB:(h + 1) * QB, :] = q_idx[:, h * IDX_DIM:(h + 1) * IDX_DIM]
        qlat_ref[h * QB:(h + 1) * QB, :] = q_lat[:, h * KV_RANK:(h + 1) * KV_RANK]
    iw = iwt_ref[0]

    def score_body(jb0, nb, n_pos8):
        d_blk = _dot_nt(kidx_ref[pl.ds(blk(jb0), nb * QB), :], qidx_ref[...])
        for sb in range(nb):
            off = blk(jb0 + sb)
            d_all = d_blk[sb * QB:(sb + 1) * QB, :]
            acc = jnp.maximum(d_all[:, 0:QB], 0.0) * iw[0:1, :]
            for h in range(1, N_IDX_HEADS):
                acc = acc + jnp.maximum(d_all[:, h * QB:(h + 1) * QB], 0.0) * iw[h:h + 1, :]
            sc = jnp.where(s_loc + off <= t_glob, acc + 0.0, F32_LOWEST)
            score_ref[pl.ds(off, QB), :] = sc
            n_pos8 = n_pos8 + _colsum8(jnp.where(sc >= 0.0, 1.0, 0.0))
        return n_pos8

    n_pos8 = block_loop(score_body, jnp.zeros((SUBLANES, QB), f32))

    def count_where(pred):
        def body(jb0, nb, acc):
            for sb in range(nb):
                off = blk(jb0 + sb)
                acc = acc + _colsum8(jnp.where(pred(score_ref[pl.ds(off, QB), :], off), 1.0, 0.0))
            return acc
        acc = block_loop(body, jnp.zeros((SUBLANES, QB), f32))
        return jnp.sum(acc, axis=0, keepdims=True)

    kf = float(k_sel)

    def search():
        c0 = jnp.sum(n_pos8, axis=0, keepdims=True)
        cand0 = jnp.where(c0 >= kf, jnp.int32(0), jnp.int32(-2 ** 31))
        n_ge0 = jnp.where(c0 >= kf, c0, -1.0)

        def bit_body(it, carry):
            cand, n_ge = carry
            trial = cand + lax.shift_left(jnp.int32(1), 30 - it)
            tf = _key_to_f32(trial)
            cnt = count_where(lambda sc, off: sc >= tf)
            take = cnt >= kf
            return jnp.where(take, trial, cand), jnp.where(take, cnt, n_ge)

        cand, n_ge = lax.fori_loop(0, 31, bit_body, (cand0, n_ge0))
        thr = _key_to_f32(cand)
        keep_all_ties = jnp.full((1, QB), 2 ** idx_bits - 1, jnp.int32)

        def resolve_ties():
            n_gt = count_where(lambda sc, off: sc > thr)
            n_eq = count_where(lambda sc, off: sc == thr)
            need = kf - n_gt

            def tie_search():
                def tbody(it, xcut):
                    trial = xcut + lax.shift_left(jnp.int32(1), idx_bits - 1 - it)
                    cnt = count_where(lambda sc, off: (sc == thr) & (s_loc + off < trial))
                    return jnp.where(cnt < need, trial, xcut)
                return lax.fori_loop(0, idx_bits, tbody, jnp.zeros((1, QB), jnp.int32))

            return lax.cond(jnp.max(n_eq - need) > 0.0, tie_search, lambda: keep_all_ties)

        xcut = lax.cond(jnp.max(jnp.abs(n_ge - kf)) > 0.0, resolve_ties, lambda: keep_all_ties)
        return thr, xcut

    def no_search():
        return jnp.full((1, QB), F32_LOWEST, f32), jnp.full((1, QB), 2 ** idx_bits - 1, jnp.int32)

    thr, xcut = lax.cond((i + 1) * QB > k_sel, search, no_search)

    def selection_mask(off):
        sc = score_ref[pl.ds(off, QB), :]
        s_glob = s_loc + off
        keep = ((sc > thr) | ((sc == thr) & (s_glob <= xcut))) & (s_glob <= t_glob)
        return jnp.where(keep, 0.0, -jnp.inf)

    acc_ref[...] = jnp.zeros_like(acc_ref)

    def att_body(jb0, nb, carry):
        m, l8 = list(carry[0]), list(carry[1])
        rows = nb * QB
        lg_blk = _dot_nt(ckv_ref[pl.ds(blk(jb0), rows), :], qlat_ref[...])
        blk_max = [None] * N_HEADS_A
        for sb in range(nb):
            off = blk(jb0 + sb)
            msk = selection_mask(off)
            bsel = jnp.clip(jb0 + sb - i + 2, 0, 2)
            for h in range(N_HEADS_A):
                lgh = lg_blk[sb * QB:(sb + 1) * QB, h * QB:(h + 1) * QB] + bias_ref[bsel, h] + msk
                logit_ref[sb * QB:(sb + 1) * QB, h * QB:(h + 1) * QB] = lgh
                cm = _colmax8(lgh)
                blk_max[h] = cm if blk_max[h] is None else jnp.maximum(blk_max[h], cm)
        ps, scales = [], []
        for h in range(N_HEADS_A):
            m_new = jnp.maximum(m[h], jnp.max(blk_max[h], axis=0, keepdims=True))
            m_ref = jnp.where(m_new == -jnp.inf, 0.0, m_new)
            p = jnp.exp2(logit_ref[0:rows, h * QB:(h + 1) * QB] - m_ref)
            scale = jnp.exp2(m[h] - m_ref)
            l8[h] = l8[h] * scale + jnp.sum(p.reshape(rows // SUBLANES, SUBLANES, QB), axis=0)
            m[h] = m_new
            ps.append(p.astype(bf))
            scales.append(scale)
        pv = _dot(ckvt_ref[0, :, pl.ds(blk(jb0), rows)], jnp.concatenate(ps, axis=1))
        for h in range(N_HEADS_A):
            hs = slice(h * QB, (h + 1) * QB)
            acc_ref[:, hs] = acc_ref[:, hs] * scales[h] + pv[:, hs]
        return tuple(m), tuple(l8)

    _, l8 = block_loop(att_body, (tuple(jnp.full((1, QB), -jnp.inf, f32) for _ in range(N_HEADS_A)),
                                  tuple(jnp.zeros((SUBLANES, QB), f32) for _ in range(N_HEADS_A))))

    outs = []
    for h in range(N_HEADS_A):
        l_row = jnp.sum(l8[h], axis=0, keepdims=True)
        o_lat_t = (acc_ref[:, h * QB:(h + 1) * QB] / l_row).astype(bf)
        outs.append(_dot(wuvt_ref[h], o_lat_t))
    o_ref[...] = jnp.concatenate(outs, axis=0).T.astype(o_ref.dtype)


def _dsa(cq, iwt, kidx, ckv, ckvt, w_qidx, w_uq, w_uk_h, w_uvt_h, bias_tiles, B, S):
    T = cq.shape[0]
    assert S % QB == 0 and QB >= REL_MAX_DIST
    nq = S // QB
    k_sel = min(TOPK_MAX, S // 4)
    idx_bits = max(1, (S - 1).bit_length())
    c2 = lambda b, i: (0, 0)
    c3 = lambda b, i: (0, 0, 0)
    return pl.pallas_call(
        functools.partial(_dsa_kernel, k_sel=k_sel, idx_bits=idx_bits),
        grid=(B, nq),
        in_specs=[
            pl.BlockSpec((QB, Q_RANK), lambda b, i: (b * nq + i, 0)),
            pl.BlockSpec((1, N_IDX_HEADS, QB), lambda b, i: (b, 0, i)),
            pl.BlockSpec((S, IDX_DIM), lambda b, i: (b, 0)),
            pl.BlockSpec((S, KV_RANK), lambda b, i: (b, 0)),
            pl.BlockSpec((1, KV_RANK, S), lambda b, i: (b, 0, 0)),
            pl.BlockSpec(w_qidx.shape, c2),
            pl.BlockSpec(w_uq.shape, c2),
            pl.BlockSpec(w_uk_h.shape, c3),
            pl.BlockSpec(w_uvt_h.shape, c3),
            pl.BlockSpec(bias_tiles.shape, lambda b, i: (0, 0, 0, 0)),
        ],
        out_specs=pl.BlockSpec((QB, MIX_A), lambda b, i: (b * nq + i, 0)),
        out_shape=jax.ShapeDtypeStruct((T, MIX_A), MXU_DTYPE),
        scratch_shapes=[
            pltpu.VMEM((Q_RANK, N_HEADS_A * KV_RANK), MXU_DTYPE),
            pltpu.VMEM((N_IDX_HEADS * QB, IDX_DIM), MXU_DTYPE),
            pltpu.VMEM((N_HEADS_A * QB, KV_RANK), MXU_DTYPE),
            pltpu.VMEM((S, QB), jnp.float32),
            pltpu.VMEM((max(UNROLL_WIDTHS) * QB, N_HEADS_A * QB), jnp.float32),
            pltpu.VMEM((KV_RANK, N_HEADS_A * QB), jnp.float32),
        ],
        compiler_params=_cparams(("arbitrary", "arbitrary")),
        name="dsa",
    )(cq, iwt, kidx, ckv, ckvt, w_qidx, w_uq, w_uk_h, w_uvt_h, bias_tiles)


def _layer_norm(xf, g, b):
    mu = jnp.mean(xf, axis=-1, keepdims=True)
    xc = xf - mu
    var = jnp.mean(xc * xc, axis=-1, keepdims=True)
    return xc * lax.rsqrt(var + LN_EPS) * g + b


def _rank_rows(v, n):
    ri = lax.broadcasted_iota(jnp.int32, v.shape, 0)
    rank = jnp.zeros(v.shape, jnp.float32)
    for r2 in range(n):
        row = v[r2:r2 + 1, :]
        beats = (row > v) | ((row == v) & (ri > r2))
        rank = rank + jnp.where(beats, 1.0, 0.0)
    return rank


def _top_rows(v, k):
    n = v.shape[0]
    ri = lax.broadcasted_iota(jnp.int32, v.shape, 0)
    sel = jnp.zeros(v.shape, jnp.float32)
    for _ in range(k):
        m = jnp.max(v, axis=0, keepdims=True)
        first = jnp.min(jnp.where(v == m, ri, n), axis=0, keepdims=True)
        pick = ri == first
        sel = jnp.where(pick, 1.0, sel)
        v = jnp.where(pick, -jnp.inf, v)
    return sel > 0.5


def _pack_factor():
    return 4 // jnp.dtype(MXU_DTYPE).itemsize


def _pack_rows(x):
    if _pack_factor() == 1:
        return pltpu.bitcast(x, jnp.int32)
    half = x.shape[1] // 2
    b = pltpu.bitcast(x.astype(MXU_DTYPE).astype(jnp.float32), jnp.int32)
    return b[:, half:] | lax.shift_right_logical(b[:, :half], jnp.int32(16))


_HIGH_HALF = -(1 << 16)


def _unpack_rows_f32(p):
    if _pack_factor() == 1:
        return [pltpu.bitcast(p, jnp.float32)]
    lo = pltpu.bitcast(lax.shift_left(p, jnp.int32(16)), jnp.float32)
    hi = pltpu.bitcast(p & jnp.int32(_HIGH_HALF), jnp.float32)
    return [lo, hi]


def _unpack_rows(p):
    return [v.astype(MXU_DTYPE) for v in _unpack_rows_f32(p)]


def _mix_router_kernel(x_ref, ya_ref, yb_ref, yc_ref, wo_ref, g_ref, b_ref, wrt_ref, rb_ref, exp_ref,
                       x1_ref, x1p_ref, sel_ref, w_ref, pos_ref, cnt_ref, base_ref, *, tm):
    step = pl.program_id(0)
    f32 = jnp.float32

    @pl.when(step == 0)
    def _():
        base_ref[...] = jnp.zeros_like(base_ref)

    mix = _dot(ya_ref[...], wo_ref[0:MIX_A, :])
    mix = mix + _dot(yb_ref[...], wo_ref[MIX_A:MIX_A + CONV_CH, :])
    mix = mix + _dot(yc_ref[...], wo_ref[MIX_A + CONV_CH:, :])
    x1 = _layer_norm(ALPHA * x_ref[...] + mix, g_ref[...], b_ref[...])
    x1_ref[...] = x1
    x1p_ref[...] = _pack_rows(x1)

    lg = lax.dot_general(wrt_ref[...], x1, _NT, precision=lax.Precision.HIGHEST, preferred_element_type=f32)
    s = 1.0 / (1.0 + jnp.exp(-lg))
    sc = s + rb_ref[...]

    g3 = sc.reshape(N_GROUPS, GROUP_SIZE, tm)
    m1 = jnp.max(g3, axis=1, keepdims=True)
    is_m1 = g3 == m1
    n_m1 = jnp.sum(jnp.where(is_m1, 1.0, 0.0), axis=1, keepdims=True)
    m2 = jnp.max(jnp.where(is_m1, -jnp.inf, g3), axis=1, keepdims=True)
    gscore = (m1 + jnp.where(n_m1 > 1.0, m1, m2)).reshape(N_GROUPS, tm)
    gsel = jnp.where(_rank_rows(gscore, N_GROUPS) < float(TOPK_GROUPS), 1.0, 0.0)
    emask = _dot(exp_ref[...], gsel.astype(MXU_DTYPE)) > 0.5
    masked = jnp.where(emask, sc, -jnp.inf)
    sel = _top_rows(masked, TOP_K) & emask
    self_ = jnp.where(sel, 1.0, 0.0)
    top_s = jnp.where(sel, s, 0.0)
    w = top_s / jnp.sum(top_s, axis=0, keepdims=True) * ROUTED_SCALE

    t_r = lax.broadcasted_iota(jnp.int32, (tm, tm), 0)
    t_c = lax.broadcasted_iota(jnp.int32, (tm, tm), 1)
    upper = jnp.where(t_r < t_c, 1.0, 0.0).astype(MXU_DTYPE)
    pref = _dot(self_.astype(MXU_DTYPE), upper)
    base = base_ref[...]
    sel_ref[...] = self_
    w_ref[...] = w
    pos_ref[...] = base + pref
    base = base + jnp.sum(self_, axis=1, keepdims=True)
    base_ref[...] = base
    cnt_ref[...] = jnp.broadcast_to(base, cnt_ref.shape)


def _mix_router(x2, ya, yb, yc, w_out, ln_g, ln_b, w_router_t, router_bias, tm):
    T, D = x2.shape
    E = N_EXPERTS
    expand = (jnp.arange(E)[:, None] // GROUP_SIZE == jnp.arange(N_GROUPS)[None, :]).astype(MXU_DTYPE)
    row = lambda i: (i, 0)
    col = lambda i: (0, i)
    c2 = lambda i: (0, 0)
    f32 = jnp.float32
    return pl.pallas_call(
        functools.partial(_mix_router_kernel, tm=tm),
        grid=(T // tm,),
        in_specs=[
            pl.BlockSpec((tm, D), row),
            pl.BlockSpec((tm, MIX_A), row),
            pl.BlockSpec((tm, CONV_CH), row),
            pl.BlockSpec((tm, MIX_C), row),
            pl.BlockSpec(w_out.shape, c2),
            pl.BlockSpec((1, D), c2),
            pl.BlockSpec((1, D), c2),
            pl.BlockSpec((E, D), c2),
            pl.BlockSpec((E, 1), c2),
            pl.BlockSpec((E, N_GROUPS), c2),
        ],
        out_specs=[
            pl.BlockSpec((tm, D), row),
            pl.BlockSpec((tm, D // _pack_factor()), row),
            pl.BlockSpec((E, tm), col),
            pl.BlockSpec((E, tm), col),
            pl.BlockSpec((E, tm), col),
            pl.BlockSpec((E, LANES), c2),
        ],
        out_shape=[
            jax.ShapeDtypeStruct((T, D), f32),
            jax.ShapeDtypeStruct((T, D // _pack_factor()), jnp.int32),
            jax.ShapeDtypeStruct((E, T), f32),
            jax.ShapeDtypeStruct((E, T), f32),
            jax.ShapeDtypeStruct((E, T), f32),
            jax.ShapeDtypeStruct((E, LANES), f32),
        ],
        scratch_shapes=[pltpu.VMEM((E, 1), f32)],
        compiler_params=_cparams(("arbitrary",)),
        name="mix_router",
    )(x2, ya, yb, yc, w_out, ln_g, ln_b, w_router_t, router_bias, expand)


def _compact_kernel(sel_ref, w_ref, pos_ref, pstart_ref, low_ref, dest_ref, wk_ref):
    sel = sel_ref[...]
    on = sel > 0.5
    rank = _dot(low_ref[...], sel.astype(MXU_DTYPE))
    row = pstart_ref[...] + pos_ref[...]
    w = w_ref[...]
    dests, ws = [], []
    for k in range(TOP_K):
        m = on & (rank == float(k))
        dests.append(jnp.sum(jnp.where(m, row, 0.0), axis=0, keepdims=True))
        ws.append(jnp.sum(jnp.where(m, w, 0.0), axis=0, keepdims=True))
    dest_ref[...] = jnp.concatenate(dests, axis=0).astype(jnp.int32)
    wk_ref[...] = jnp.concatenate(ws, axis=0)


def _compact(sel_t, w_t, pos_t, pad_start, tm):
    E, T = sel_t.shape
    lower = (jnp.arange(E)[None, :] < jnp.arange(E)[:, None]).astype(MXU_DTYPE)
    col = lambda i: (0, i)
    c2 = lambda i: (0, 0)
    return pl.pallas_call(
        _compact_kernel,
        grid=(T // tm,),
        in_specs=[pl.BlockSpec((E, tm), col), pl.BlockSpec((E, tm), col), pl.BlockSpec((E, tm), col),
                  pl.BlockSpec((E, 1), c2), pl.BlockSpec((E, E), c2)],
        out_specs=[pl.BlockSpec((TOP_K, tm), col), pl.BlockSpec((TOP_K, tm), col)],
        out_shape=[jax.ShapeDtypeStruct((TOP_K, T), jnp.int32), jax.ShapeDtypeStruct((TOP_K, T), jnp.float32)],
        compiler_params=_cparams(("arbitrary",)),
        name="route_compact",
    )(sel_t, w_t, pos_t, pad_start, lower)


def _silu(g):
    return g / (1.0 + jnp.exp(-g))


def _expert_kernel(be_ref, nv_ref, nu_ref, xs_ref, wg_ref, wu_ref, wd_ref, ys_ref, wgb_ref, wub_ref, wdb_ref):
    i = pl.program_id(0)

    @pl.when((i == 0) | (be_ref[i] != be_ref[jnp.maximum(i - 1, 0)]))
    def _():
        wgb_ref[...] = wg_ref[0].astype(MXU_DTYPE)
        wub_ref[...] = wu_ref[0].astype(MXU_DTYPE)
        wdb_ref[...] = wd_ref[0].astype(MXU_DTYPE)

    @pl.when(i < nu_ref[0])
    def _():
        live = lax.broadcasted_iota(jnp.int32, (ROW_BLOCK, 1), 0) < nv_ref[i]
        parts = [jnp.where(live, v, jnp.zeros_like(v)) for v in _unpack_rows(xs_ref[...])]
        dk = wgb_ref.shape[0] // len(parts)

        def proj(w_ref):
            acc = _dot(parts[0], w_ref[0:dk, :])
            for n in range(1, len(parts)):
                acc = acc + _dot(parts[n], w_ref[n * dk:(n + 1) * dk, :])
            return acc

        a = (_silu(proj(wgb_ref)) * proj(wub_ref)).astype(MXU_DTYPE)
        ys_ref[...] = _pack_rows(_dot(a, wdb_ref[...]))


def _experts(xs, block_e, block_valid, n_used, w_gate, w_up, w_down):
    n_rows, W = xs.shape
    D = w_gate.shape[1]
    n_blocks = n_rows // ROW_BLOCK
    blk = lambda i, be, nv, nu: (jnp.minimum(i, nu[0] - 1), 0)
    wsel = lambda i, be, nv, nu: (be[i], 0, 0)
    return pl.pallas_call(
        _expert_kernel,
        grid_spec=pltpu.PrefetchScalarGridSpec(
            num_scalar_prefetch=3,
            grid=(n_blocks,),
            in_specs=[
                pl.BlockSpec((ROW_BLOCK, W), blk),
                pl.BlockSpec((1, D, D_EXPERT), wsel),
                pl.BlockSpec((1, D, D_EXPERT), wsel),
                pl.BlockSpec((1, D_EXPERT, D), wsel),
            ],
            out_specs=pl.BlockSpec((ROW_BLOCK, W), blk),
            scratch_shapes=[pltpu.VMEM((D, D_EXPERT), MXU_DTYPE), pltpu.VMEM((D, D_EXPERT), MXU_DTYPE),
                            pltpu.VMEM((D_EXPERT, D), MXU_DTYPE)],
        ),
        out_shape=jax.ShapeDtypeStruct((n_rows, W), xs.dtype),
        compiler_params=_cparams(("arbitrary",)),
        name="experts",
    )(block_e, block_valid, n_used, xs, w_gate, w_up, w_down)


SC_CORES = 2
SC_SUBCORES = 16
SC_GATHER_ROWS = 64
COMBINE_CHUNKS = 8


def _sc_gather_rows(table, idx):
    n = idx.shape[0]
    w = table.shape[1]
    n_workers = SC_CORES * SC_SUBCORES
    per_worker = n // n_workers
    assert n % n_workers == 0 and per_worker % SC_GATHER_ROWS == 0
    mesh = plsc.VectorSubcoreMesh(core_axis_name="c", subcore_axis_name="s")

    @functools.partial(
        pl.kernel, mesh=mesh,
        out_type=jax.ShapeDtypeStruct((n, w), table.dtype),
        scratch_types=[
            pltpu.VMEM((2, SC_GATHER_ROWS), jnp.int32),
            pltpu.VMEM((2, SC_GATHER_ROWS, w), table.dtype),
            pltpu.SemaphoreType.DMA((2,)),
        ],
        name="sc_gather_rows",
    )
    def gather(table_hbm, idx_hbm, out_hbm, idx_v, rows_v, sem):
        wid = lax.axis_index("s") * SC_CORES + lax.axis_index("c")
        base = wid * per_worker
        n_steps = per_worker // SC_GATHER_ROWS

        def gather_copy(slot):
            return pltpu.make_async_copy(table_hbm.at[idx_v.at[slot]], rows_v.at[slot], sem.at[slot])

        def start(step, slot):
            pltpu.sync_copy(idx_hbm.at[pl.ds(base + step * SC_GATHER_ROWS, SC_GATHER_ROWS)], idx_v.at[slot])
            gather_copy(slot).start()

        start(0, 0)

        @pl.loop(0, n_steps, step=2)
        def _(g):
            for slot in range(2):
                step = g + slot

                @pl.when(step + 1 < n_steps)
                def _():
                    start(step + 1, 1 - slot)

                gather_copy(slot).wait()
                pltpu.sync_copy(rows_v.at[slot], out_hbm.at[pl.ds(base + step * SC_GATHER_ROWS, SC_GATHER_ROWS)])

    return gather(table, idx)


SC_SCATTER_ROWS = 64


def _sc_scatter_rows(rows, idx3, n_out):
    n_src, w = rows.shape
    n_chunks, n_dst, batch = idx3.shape
    n_workers = SC_CORES * SC_SUBCORES
    assert batch == SC_SCATTER_ROWS and n_chunks * batch == n_src and n_chunks % (2 * n_workers) == 0
    per_worker = n_chunks // n_workers
    mesh = plsc.VectorSubcoreMesh(core_axis_name="c", subcore_axis_name="s")

    @functools.partial(
        pl.kernel, mesh=mesh,
        out_type=jax.ShapeDtypeStruct((n_out, w), rows.dtype),
        scratch_types=[
            pltpu.VMEM((2, n_dst, batch), jnp.int32),
            pltpu.VMEM((2, batch, w), rows.dtype),
            pltpu.SemaphoreType.DMA((2,)),
            pltpu.SemaphoreType.DMA,
        ],
        name="sc_scatter_rows",
    )
    def scatter(rows_hbm, idx_hbm, out_hbm, idx_v, rows_v, load_sem, store_sem):
        wid = lax.axis_index("s") * SC_CORES + lax.axis_index("c")

        def load_copy(step, slot):
            c = wid * per_worker + step
            return pltpu.make_async_copy(rows_hbm.at[pl.ds(c * batch, batch)], rows_v.at[slot], load_sem.at[slot])

        def load(step, slot):
            pltpu.sync_copy(idx_hbm.at[wid * per_worker + step], idx_v.at[slot])
            load_copy(step, slot).start()

        def store_copy(slot, k):
            return pltpu.make_async_copy(rows_v.at[slot], out_hbm.at[idx_v.at[slot].at[k]], store_sem)

        load(0, 0)

        @pl.loop(0, per_worker, step=2)
        def _(g):
            for slot in range(2):
                step = g + slot

                @pl.when(step + 1 < per_worker)
                def _():
                    load(step + 1, 1 - slot)

                load_copy(step, slot).wait()
                for k in range(n_dst):
                    store_copy(slot, k).start()
                for k in range(n_dst):
                    store_copy(slot, k).wait()

    return scatter(rows, idx3)


def _combine2_kernel(wk_ref, x1_ref, g_ref_rows, wsg_ref, wsu_ref, wsd_ref, g_ref, b_ref, o_ref):
    x1 = x1_ref[...]
    xb = x1.astype(MXU_DTYPE)
    a = (_silu(_dot(xb, wsg_ref[...])) * _dot(xb, wsu_ref[...])).astype(MXU_DTYPE)
    shared = _dot(a, wsd_ref[...])
    wk = wk_ref[...].T
    groups = [wk[:, 0:1] * v for v in _unpack_rows_f32(g_ref_rows[0])]
    for k in range(1, TOP_K):
        groups = [g + wk[:, k:k + 1] * v for g, v in zip(groups, _unpack_rows_f32(g_ref_rows[k]))]
    routed = jnp.concatenate(groups, axis=1)
    o_ref[...] = _layer_norm(ALPHA * x1 + (routed + shared), g_ref[...], b_ref[...])


def _combine2_kernel_into(wk_ref, x1_ref, g_ref_rows, wsg_ref, wsu_ref, wsd_ref, g_ref, b_ref, prev_ref, o_ref):
    del prev_ref
    _combine2_kernel(wk_ref, x1_ref, g_ref_rows, wsg_ref, wsu_ref, wsd_ref, g_ref, b_ref, o_ref)


def _combine2(wk_t, x1, gathered, w_sg, w_su, w_sd, ln_g, ln_b, tc, chunk, prev):
    T, D = x1.shape
    _, t_chunk, W = gathered.shape
    base = chunk * (t_chunk // tc)
    row = lambda i: (base + i, 0)
    c2 = lambda i: (0, 0)
    in_specs = [
        pl.BlockSpec((TOP_K, tc), lambda i: (0, base + i)),
        pl.BlockSpec((tc, D), row),
        pl.BlockSpec((TOP_K, tc, W), lambda i: (0, i, 0)),
        pl.BlockSpec(w_sg.shape, c2),
        pl.BlockSpec(w_su.shape, c2),
        pl.BlockSpec(w_sd.shape, c2),
        pl.BlockSpec((1, D), c2),
        pl.BlockSpec((1, D), c2),
    ]
    args = [wk_t, x1, gathered, w_sg, w_su, w_sd, ln_g, ln_b]
    if prev is None:
        body, aliases = _combine2_kernel, {}
    else:
        body, aliases = _combine2_kernel_into, {len(args): 0}
        in_specs.append(pl.BlockSpec(memory_space=pl.ANY))
        args.append(prev)
    return pl.pallas_call(
        body,
        grid=(t_chunk // tc,),
        in_specs=in_specs,
        out_specs=pl.BlockSpec((tc, D), row),
        out_shape=jax.ShapeDtypeStruct((T, D), jnp.float32),
        input_output_aliases=aliases,
        compiler_params=_cparams(("arbitrary",)),
        name="combine",
    )(*args)


def _split_w_in(w_in):
    o_kv = Q_RANK
    o_ki = o_kv + KV_RANK
    o_iw = o_ki + IDX_DIM
    o_rest = o_iw + N_IDX_HEADS
    w_small = jnp.pad(w_in[:, o_ki:o_rest], ((0, 0), (0, LANES - IDX_DIM - N_IDX_HEADS)))
    return jnp.concatenate([w_in[:, :o_ki], w_in[:, o_rest:], w_small], axis=1).astype(MXU_DTYPE)


def _stages(x, mem, w_in, q_norm_g, kv_norm_g, w_uq, w_uk, w_uv, w_qidx, rel_bias, conv_w, w_mem_k, w_mem_v, w_out, ln1_g, ln1_b, w_router, router_bias, w_e_gate, w_e_up, w_e_down, w_s_gate, w_s_up, w_s_down, ln2_g, ln2_b):
    B, S, D = x.shape
    T = B * S
    bf = MXU_DTYPE
    assert w_in.shape[0] == DEPTH == 1, "single-layer stack"
    l = 0
    res = {}
    x2 = x.reshape(T, D)
    cq, ckv, ckvt, kidx, iwt, yb, yc = _proj(
        x2, mem, _split_w_in(w_in[l]), q_norm_g[l].reshape(1, -1), kv_norm_g[l].reshape(1, -1), conv_w[l],
        w_mem_k[l].astype(bf), w_mem_v[l].astype(bf), B, S, tm=min(512, S))
    res.update(c_q=cq, c_kv=ckv, k_idx=kidx, y_b=yb, y_c=yc,
               idx_w=jnp.swapaxes(iwt, 1, 2) / (N_IDX_HEADS ** -0.5 * IDX_DIM ** -0.5))
    bias_t = _bias_tiles(rel_bias)
    ya = _dsa(cq, iwt, kidx, ckv, ckvt,
              w_qidx[l].reshape(Q_RANK, -1).astype(bf), w_uq[l].reshape(Q_RANK, -1).astype(bf),
              jnp.transpose(w_uk[l], (1, 0, 2)).astype(bf), jnp.transpose(w_uv[l], (1, 2, 0)).astype(bf),
              bias_t, B, S)
    res.update(y_a=ya)

    x1, x1p, sel_t, w_t, pos_t, cnt = _mix_router(
        x2, ya, yb, yc, w_out[l].astype(bf), ln1_g[l].reshape(1, -1), ln1_b[l].reshape(1, -1),
        w_router[l].T, router_bias[l].reshape(-1, 1), tm=min(512, T))
    res.update(x1=x1)

    counts = cnt[:, 0].astype(jnp.int32)
    padded = (counts + ROW_BLOCK - 1) // ROW_BLOCK * ROW_BLOCK
    pad_end = jnp.cumsum(padded)
    pad_start = pad_end - padded
    n_blocks = -(-(T * TOP_K) // ROW_BLOCK) + N_EXPERTS
    n_rows = n_blocks * ROW_BLOCK
    block_start = jnp.arange(n_blocks, dtype=jnp.int32) * ROW_BLOCK
    block_e = jnp.minimum(jnp.sum((pad_end[None, :] <= block_start[:, None]).astype(jnp.int32), axis=1),
                          N_EXPERTS - 1)
    n_used = (pad_end[-1:] // ROW_BLOCK).astype(jnp.int32)

    dest_t, wk_t = _compact(sel_t, w_t, pos_t, pad_start.astype(jnp.float32).reshape(-1, 1), tm=min(512, T))
    block_valid = jnp.clip((pad_start + counts)[block_e] - block_start, 0, ROW_BLOCK).astype(jnp.int32)
    bt = SC_SCATTER_ROWS
    idx3 = jnp.transpose(dest_t.reshape(TOP_K, T // bt, bt), (1, 0, 2))
    xs = _sc_scatter_rows(x1p, idx3, n_rows)
    ys = _experts(xs, block_e, block_valid, n_used, w_e_gate[l], w_e_up[l], w_e_down[l])
    n_chunks = COMBINE_CHUNKS if T % (COMBINE_CHUNKS * 256) == 0 else 1
    t_chunk = T // n_chunks
    out = None
    for c in range(n_chunks):
        idx_c = dest_t[:, c * t_chunk:(c + 1) * t_chunk].reshape(-1)
        gathered = _sc_gather_rows(ys, idx_c).reshape(TOP_K, t_chunk, -1)
        out = _combine2(wk_t, x1, gathered, w_s_gate[l].astype(bf), w_s_up[l].astype(bf), w_s_down[l].astype(bf),
                        ln2_g[l].reshape(1, -1), ln2_b[l].reshape(1, -1), tc=min(256, t_chunk), chunk=c, prev=out)
    res.update(out=out.reshape(B, S, D))
    return res


def kernel(x, mem, w_in, q_norm_g, kv_norm_g, w_uq, w_uk, w_uv, w_qidx, rel_bias, conv_w, w_mem_k, w_mem_v, w_out, ln1_g, ln1_b, w_router, router_bias, w_e_gate, w_e_up, w_e_down, w_s_gate, w_s_up, w_s_down, ln2_g, ln2_b):
    return _stages(x, mem, w_in, q_norm_g, kv_norm_g, w_uq, w_uk, w_uv, w_qidx, rel_bias, conv_w, w_mem_k, w_mem_v, w_out, ln1_g, ln1_b, w_router, router_bias, w_e_gate, w_e_up, w_e_down, w_s_gate, w_s_up, w_s_down, ln2_g, ln2_b)["out"]
```

```python
import functools
import math

import jax
import jax.numpy as jnp
from jax import lax
from jax.experimental import pallas as pl
from jax.experimental.pallas import tpu as pltpu
from jax.experimental.pallas import tpu_sc as plsc

N_HEADS_A = 8
HEAD_DIM = 64
Q_RANK = 256
KV_RANK = 128
N_IDX_HEADS = 8
IDX_DIM = 64
TOPK_MAX = 256
REL_BUCKETS = 32
REL_MAX_DIST = 128
CONV_CH = 256
CONV_WIDTH = 3
N_MEM_HEADS = 4
MIX_A = N_HEADS_A * HEAD_DIM
MIX_C = N_MEM_HEADS * HEAD_DIM
N_EXPERTS = 64
N_GROUPS = 8
GROUP_SIZE = N_EXPERTS // N_GROUPS
TOPK_GROUPS = 4
TOP_K = 8
D_EXPERT = 256
ROUTED_SCALE = 2.5
DEPTH = 1
ALPHA = (2.0 * DEPTH) ** 0.25
LN_EPS = 1e-5
RMS_EPS = 1e-6
LOG2_E = math.log2(math.e)

LANES = 128
SUBLANES = 8
QB = 128
F32_LOWEST = -3.4028234663852886e38
VMEM_LIMIT = 56 * 1024 * 1024
MXU_DTYPE = jnp.bfloat16
ROW_BLOCK = 1024

_NT = (((1,), (1,)), ((), ()))


def _dot(a, b):
    return jnp.dot(a, b, preferred_element_type=jnp.float32)


def _dot_nt(a, b):
    return lax.dot_general(a, b, _NT, preferred_element_type=jnp.float32)


def _cparams(sem):
    return pltpu.CompilerParams(dimension_semantics=sem, vmem_limit_bytes=VMEM_LIMIT)


def _bias_kernel(rb_ref, o_ref):
    s = lax.broadcasted_iota(jnp.int32, (QB, QB), 0)
    t = lax.broadcasted_iota(jnp.int32, (QB, QB), 1)
    max_exact = REL_BUCKETS // 2
    for tile in range(3):
        n = jnp.maximum(t - s + (2 - tile) * QB, 0)
        nf = jnp.maximum(n.astype(jnp.float32), 1.0)
        large = max_exact + (jnp.log(nf / max_exact) / math.log(REL_MAX_DIST / max_exact)
                             * (REL_BUCKETS - max_exact)).astype(jnp.int32)
        large = jnp.minimum(large, REL_BUCKETS - 1)
        bucket = jnp.where(n < max_exact, n, large)
        for h in range(N_HEADS_A):
            acc = jnp.zeros((QB, QB), jnp.float32)
            for b in range(REL_BUCKETS):
                acc = jnp.where(bucket == b, rb_ref[b, h], acc)
            o_ref[tile, h] = acc * LOG2_E


def _bias_tiles(rel_bias):
    return pl.pallas_call(
        _bias_kernel,
        in_specs=[pl.BlockSpec(memory_space=pltpu.SMEM)],
        out_specs=pl.BlockSpec(memory_space=pltpu.VMEM),
        out_shape=jax.ShapeDtypeStruct((3, N_HEADS_A, QB, QB), jnp.float32),
        name="bias_tiles",
    )(rel_bias)


def _proj_kernel(x_ref, mem_ref, wm_ref, qg_ref, kvg_ref, cw_ref, wmk_ref, wmv_ref,
                 cq_ref, ckv_ref, ckvt_ref, kidx_ref, iwt_ref, yb_ref, yc_ref,
                 carry_ref, mk_ref, mv_ref, *, tm):
    si = pl.program_id(1)

    @pl.when(si == 0)
    def _():
        carry_ref[...] = jnp.zeros_like(carry_ref)
        mb = mem_ref[0].astype(MXU_DTYPE)
        mk_ref[...] = _dot(mb, wmk_ref[...]).astype(MXU_DTYPE)
        mv_ref[...] = _dot(mb, wmv_ref[...]).astype(MXU_DTYPE)

    xb = x_ref[...].astype(MXU_DTYPE)
    p = _dot(xb, wm_ref[...])
    small = p[:, p.shape[1] - LANES:]

    o = 0
    cq = p[:, o:o + Q_RANK]; o += Q_RANK
    ckv = p[:, o:o + KV_RANK]; o += KV_RANK
    g_b = p[:, o:o + CONV_CH]; o += CONV_CH
    g_c = p[:, o:o + CONV_CH]; o += CONV_CH
    h_c = p[:, o:o + CONV_CH]; o += CONV_CH
    q_mem = p[:, o:o + MIX_C]

    cq = cq * lax.rsqrt(jnp.mean(cq * cq, axis=-1, keepdims=True) + RMS_EPS) * qg_ref[...]
    ckv = ckv * lax.rsqrt(jnp.mean(ckv * ckv, axis=-1, keepdims=True) + RMS_EPS) * kvg_ref[...]
    cq_ref[...] = cq.astype(MXU_DTYPE)
    ckv_b = ckv.astype(MXU_DTYPE)
    ckv_ref[...] = ckv_b
    ckvt_ref[0] = ckv.T.astype(MXU_DTYPE)

    kidx_ref[...] = small[:, :IDX_DIM].astype(MXU_DTYPE)
    small_t = small.T
    iwt_ref[0] = small_t[IDX_DIM:IDX_DIM + N_IDX_HEADS, :] * (N_IDX_HEADS ** -0.5 * IDX_DIM ** -0.5)

    u = g_c * h_c
    rows = lax.broadcasted_iota(jnp.int32, (tm, 1), 0)
    c6 = carry_ref[SUBLANES - 2:SUBLANES - 1, :]
    c7 = carry_ref[SUBLANES - 1:SUBLANES, :]
    u1 = jnp.where(rows == 0, c7, pltpu.roll(u, 1, 0))
    u2 = jnp.where(rows == 0, c6, jnp.where(rows == 1, c7, pltpu.roll(u, 2, 0)))
    y = cw_ref[0:1, :] * u2
    y = y + cw_ref[1:2, :] * u1
    y = y + cw_ref[2:3, :] * u
    yb_ref[...] = (g_b * y).astype(MXU_DTYPE)
    carry_ref[...] = u[tm - SUBLANES:, :]

    qm = q_mem.astype(MXU_DTYPE)
    outs = []
    for h in range(N_MEM_HEADS):
        sl = slice(h * HEAD_DIM, (h + 1) * HEAD_DIM)
        lg = _dot_nt(qm[:, sl], mk_ref[:, sl]) * (HEAD_DIM ** -0.5)
        lg = lg - jnp.max(lg, axis=-1, keepdims=True)
        e = jnp.exp(lg)
        pr = e / jnp.sum(e, axis=-1, keepdims=True)
        outs.append(_dot(pr.astype(MXU_DTYPE), mv_ref[:, sl]))
    yc_ref[...] = jnp.concatenate(outs, axis=-1).astype(MXU_DTYPE)


def _proj(x2, mem, w_main, q_g, kv_g, conv_w, w_mk, w_mv, B, S, tm):
    T, D = x2.shape
    n_mem = mem.shape[1]
    ns = S // tm
    row = lambda b, s: (b * ns + s, 0)
    const2 = lambda b, s: (0, 0)
    bf = MXU_DTYPE
    return pl.pallas_call(
        functools.partial(_proj_kernel, tm=tm),
        grid=(B, ns),
        in_specs=[
            pl.BlockSpec((tm, D), row),
            pl.BlockSpec((1, n_mem, D), lambda b, s: (b, 0, 0)),
            pl.BlockSpec(w_main.shape, const2),
            pl.BlockSpec(q_g.shape, const2),
            pl.BlockSpec(kv_g.shape, const2),
            pl.BlockSpec(conv_w.shape, const2),
            pl.BlockSpec(w_mk.shape, const2),
            pl.BlockSpec(w_mv.shape, const2),
        ],
        out_specs=[
            pl.BlockSpec((tm, Q_RANK), row),
            pl.BlockSpec((tm, KV_RANK), row),
            pl.BlockSpec((1, KV_RANK, tm), lambda b, s: (b, 0, s)),
            pl.BlockSpec((tm, IDX_DIM), row),
            pl.BlockSpec((1, N_IDX_HEADS, tm), lambda b, s: (b, 0, s)),
            pl.BlockSpec((tm, CONV_CH), row),
            pl.BlockSpec((tm, MIX_C), row),
        ],
        out_shape=[
            jax.ShapeDtypeStruct((T, Q_RANK), bf),
            jax.ShapeDtypeStruct((T, KV_RANK), bf),
            jax.ShapeDtypeStruct((B, KV_RANK, S), bf),
            jax.ShapeDtypeStruct((T, IDX_DIM), bf),
            jax.ShapeDtypeStruct((B, N_IDX_HEADS, S), jnp.float32),
            jax.ShapeDtypeStruct((T, CONV_CH), bf),
            jax.ShapeDtypeStruct((T, MIX_C), bf),
        ],
        scratch_shapes=[
            pltpu.VMEM((SUBLANES, CONV_CH), jnp.float32),
            pltpu.VMEM((n_mem, MIX_C), bf),
            pltpu.VMEM((n_mem, MIX_C), bf),
        ],
        compiler_params=_cparams(("arbitrary", "arbitrary")),
        name="proj",
    )(x2, mem, w_main, q_g, kv_g, conv_w, w_mk, w_mv)


def _key_to_f32(key):
    bits = jnp.where(key < 0, key ^ jnp.int32(0x7FFFFFFF), key)
    return pltpu.bitcast(bits, jnp.float32)


def _colsum8(v):
    return jnp.sum(v.reshape(QB // SUBLANES, SUBLANES, QB), axis=0)


def _colmax8(v):
    return jnp.max(v.reshape(QB // SUBLANES, SUBLANES, QB), axis=0)


UNROLL_WIDTHS = (8, 4, 2, 1)


def _dsa_kernel(cq_ref, iwt_ref, kidx_ref, ckv_ref, ckvt_ref, wqi_ref, wuq_ref, wuk_ref, wuvt_ref,
                bias_ref, o_ref, wfold_ref, qidx_ref, qlat_ref, score_ref, logit_ref, acc_ref,
                *, k_sel, idx_bits):
    i = pl.program_id(1)
    f32 = jnp.float32
    bf = MXU_DTYPE
    n_blocks = i + 1
    n_blocks = n_blocks + jnp.where((n_blocks % 4 == 3) & (n_blocks < pl.num_programs(1)), 1, 0)
    s_loc = lax.broadcasted_iota(jnp.int32, (QB, QB), 0)
    t_glob = i * QB + lax.broadcasted_iota(jnp.int32, (QB, QB), 1)

    def blk(jb):
        return pl.multiple_of(jb * QB, QB)

    def block_loop(fn, init):
        c, start = init, 0
        for width in UNROLL_WIDTHS:
            n = (n_blocks - start) // width
            c = lax.fori_loop(0, n, lambda it, c, w=width, s=start: fn(s + it * w, w, c), c)
            start = start + n * width
        return c

    @pl.when(i == 0)
    def _():
        for h in range(N_HEADS_A):
            wfold_ref[:, h * KV_RANK:(h + 1) * KV_RANK] = (
                _dot_nt(wuq_ref[:, h * HEAD_DIM:(h + 1) * HEAD_DIM], wuk_ref[h])
                * (HEAD_DIM ** -0.5 * LOG2_E)).astype(bf)

    cq = cq_ref[...]
    q_idx = _dot(cq, wqi_ref[...]).astype(bf)
    q_lat = _dot(cq, wfold_ref[...]).astype(bf)
    for h in range(N_HEADS_A):
        qidx_ref[h * QB:(h + 1) * QB, :] = q_idx[:, h * IDX_DIM:(h + 1) * IDX_DIM]
        qlat_ref[h * QB:(h + 1) * QB, :] = q_lat[:, h * KV_RANK:(h + 1) * KV_RANK]
    iw = iwt_ref[0]

    def score_body(jb0, nb, n_pos8):
        d_blk = _dot_nt(kidx_ref[pl.ds(blk(jb0), nb * QB), :], qidx_ref[...])
        for sb in range(nb):
            off = blk(jb0 + sb)
            d_all = d_blk[sb * QB:(sb + 1) * QB, :]
            acc = jnp.maximum(d_all[:, 0:QB], 0.0) * iw[0:1, :]
            for h in range(1, N_IDX_HEADS):
                acc = acc + jnp.maximum(d_all[:, h * QB:(h + 1) * QB], 0.0) * iw[h:h + 1, :]
            sc = jnp.where(s_loc + off <= t_glob, acc + 0.0, F32_LOWEST)
            score_ref[pl.ds(off, QB), :] = sc
            n_pos8 = n_pos8 + _colsum8(jnp.where(sc >= 0.0, 1.0, 0.0))
        return n_pos8

    n_pos8 = block_loop(score_body, jnp.zeros((SUBLANES, QB), f32))

    def count_where(pred):
        def body(jb0, nb, acc):
            for sb in range(nb):
                off = blk(jb0 + sb)
                acc = acc + _colsum8(jnp.where(pred(score_ref[pl.ds(off, QB), :], off), 1.0, 0.0))
            return acc
        acc = block_loop(body, jnp.zeros((SUBLANES, QB), f32))
        return jnp.sum(acc, axis=0, keepdims=True)

    kf = float(k_sel)

    def search():
        c0 = jnp.sum(n_pos8, axis=0, keepdims=True)
        cand0 = jnp.where(c0 >= kf, jnp.int32(0), jnp.int32(-2 ** 31))
        n_ge0 = jnp.where(c0 >= kf, c0, -1.0)

        def bit_body(it, carry):
            cand, n_ge = carry
            trial = cand + lax.shift_left(jnp.int32(1), 30 - it)
            tf = _key_to_f32(trial)
            cnt = count_where(lambda sc, off: sc >= tf)
            take = cnt >= kf
            return jnp.where(take, trial, cand), jnp.where(take, cnt, n_ge)

        cand, n_ge = lax.fori_loop(0, 31, bit_body, (cand0, n_ge0))
        thr = _key_to_f32(cand)
        keep_all_ties = jnp.full((1, QB), 2 ** idx_bits - 1, jnp.int32)

        def resolve_ties():
            n_gt = count_where(lambda sc, off: sc > thr)
            n_eq = count_where(lambda sc, off: sc == thr)
            need = kf - n_gt

            def tie_search():
                def tbody(it, xcut):
                    trial = xcut + lax.shift_left(jnp.int32(1), idx_bits - 1 - it)
                    cnt = count_where(lambda sc, off: (sc == thr) & (s_loc + off < trial))
                    return jnp.where(cnt < need, trial, xcut)
                return lax.fori_loop(0, idx_bits, tbody, jnp.zeros((1, QB), jnp.int32))

            return lax.cond(jnp.max(n_eq - need) > 0.0, tie_search, lambda: keep_all_ties)

        xcut = lax.cond(jnp.max(jnp.abs(n_ge - kf)) > 0.0, resolve_ties, lambda: keep_all_ties)
        return thr, xcut

    def no_search():
        return jnp.full((1, QB), F32_LOWEST, f32), jnp.full((1, QB), 2 ** idx_bits - 1, jnp.int32)

    thr, xcut = lax.cond((i + 1) * QB > k_sel, search, no_search)

    def selection_mask(off):
        sc = score_ref[pl.ds(off, QB), :]
        s_glob = s_loc + off
        keep = ((sc > thr) | ((sc == thr) & (s_glob <= xcut))) & (s_glob <= t_glob)
        return jnp.where(keep, 0.0, -jnp.inf)

    acc_ref[...] = jnp.zeros_like(acc_ref)

    def att_body(jb0, nb, carry):
        m, l8 = list(carry[0]), list(carry[1])
        rows = nb * QB
        lg_blk = _dot_nt(ckv_ref[pl.ds(blk(jb0), rows), :], qlat_ref[...])
        blk_max = [None] * N_HEADS_A
        for sb in range(nb):
            off = blk(jb0 + sb)
            msk = selection_mask(off)
            bsel = jnp.clip(jb0 + sb - i + 2, 0, 2)
            for h in range(N_HEADS_A):
                lgh = lg_blk[sb * QB:(sb + 1) * QB, h * QB:(h + 1) * QB] + bias_ref[bsel, h] + msk
                logit_ref[sb * QB:(sb + 1) * QB, h * QB:(h + 1) * QB] = lgh
                cm = _colmax8(lgh)
                blk_max[h] = cm if blk_max[h] is None else jnp.maximum(blk_max[h], cm)
        ps, scales = [], []
        for h in range(N_HEADS_A):
            m_new = jnp.maximum(m[h], jnp.max(blk_max[h], axis=0, keepdims=True))
            m_ref = jnp.where(m_new == -jnp.inf, 0.0, m_new)
            p = jnp.exp2(logit_ref[0:rows, h * QB:(h + 1) * QB] - m_ref)
            scale = jnp.exp2(m[h] - m_ref)
            l8[h] = l8[h] * scale + jnp.sum(p.reshape(rows // SUBLANES, SUBLANES, QB), axis=0)
            m[h] = m_new
            ps.append(p.astype(bf))
            scales.append(scale)
        pv = _dot(ckvt_ref[0, :, pl.ds(blk(jb0), rows)], jnp.concatenate(ps, axis=1))
        for h in range(N_HEADS_A):
            hs = slice(h * QB, (h + 1) * QB)
            acc_ref[:, hs] = acc_ref[:, hs] * scales[h] + pv[:, hs]
        return tuple(m), tuple(l8)

    _, l8 = block_loop(att_body, (tuple(jnp.full((1, QB), -jnp.inf, f32) for _ in range(N_HEADS_A)),
                                  tuple(jnp.zeros((SUBLANES, QB), f32) for _ in range(N_HEADS_A))))

    outs = []
    for h in range(N_HEADS_A):
        l_row = jnp.sum(l8[h], axis=0, keepdims=True)
        o_lat_t = (acc_ref[:, h * QB:(h + 1) * QB] / l_row).astype(bf)
        outs.append(_dot(wuvt_ref[h], o_lat_t))
    o_ref[...] = jnp.concatenate(outs, axis=0).T.astype(o_ref.dtype)


def _dsa(cq, iwt, kidx, ckv, ckvt, w_qidx, w_uq, w_uk_h, w_uvt_h, bias_tiles, B, S):
    T = cq.shape[0]
    assert S % QB == 0 and QB >= REL_MAX_DIST
    nq = S // QB
    k_sel = min(TOPK_MAX, S // 4)
    idx_bits = max(1, (S - 1).bit_length())
    c2 = lambda b, i: (0, 0)
    c3 = lambda b, i: (0, 0, 0)
    return pl.pallas_call(
        functools.partial(_dsa_kernel, k_sel=k_sel, idx_bits=idx_bits),
        grid=(B, nq),
        in_specs=[
            pl.BlockSpec((QB, Q_RANK), lambda b, i: (b * nq + i, 0)),
            pl.BlockSpec((1, N_IDX_HEADS, QB), lambda b, i: (b, 0, i)),
            pl.BlockSpec((S, IDX_DIM), lambda b, i: (b, 0)),
            pl.BlockSpec((S, KV_RANK), lambda b, i: (b, 0)),
            pl.BlockSpec((1, KV_RANK, S), lambda b, i: (b, 0, 0)),
            pl.BlockSpec(w_qidx.shape, c2),
            pl.BlockSpec(w_uq.shape, c2),
            pl.BlockSpec(w_uk_h.shape, c3),
            pl.BlockSpec(w_uvt_h.shape, c3),
            pl.BlockSpec(bias_tiles.shape, lambda b, i: (0, 0, 0, 0)),
        ],
        out_specs=pl.BlockSpec((QB, MIX_A), lambda b, i: (b * nq + i, 0)),
        out_shape=jax.ShapeDtypeStruct((T, MIX_A), MXU_DTYPE),
        scratch_shapes=[
            pltpu.VMEM((Q_RANK, N_HEADS_A * KV_RANK), MXU_DTYPE),
            pltpu.VMEM((N_IDX_HEADS * QB, IDX_DIM), MXU_DTYPE),
            pltpu.VMEM((N_HEADS_A * QB, KV_RANK), MXU_DTYPE),
            pltpu.VMEM((S, QB), jnp.float32),
            pltpu.VMEM((max(UNROLL_WIDTHS) * QB, N_HEADS_A * QB), jnp.float32),
            pltpu.VMEM((KV_RANK, N_HEADS_A * QB), jnp.float32),
        ],
        compiler_params=_cparams(("arbitrary", "arbitrary")),
        name="dsa",
    )(cq, iwt, kidx, ckv, ckvt, w_qidx, w_uq, w_uk_h, w_uvt_h, bias_tiles)


def _layer_norm(xf, g, b):
    mu = jnp.mean(xf, axis=-1, keepdims=True)
    xc = xf - mu
    var = jnp.mean(xc * xc, axis=-1, keepdims=True)
    return xc * lax.rsqrt(var + LN_EPS) * g + b


def _rank_rows(v, n):
    ri = lax.broadcasted_iota(jnp.int32, v.shape, 0)
    rank = jnp.zeros(v.shape, jnp.float32)
    for r2 in range(n):
        row = v[r2:r2 + 1, :]
        beats = (row > v) | ((row == v) & (ri > r2))
        rank = rank + jnp.where(beats, 1.0, 0.0)
    return rank


def _top_rows(v, k):
    n = v.shape[0]
    ri = lax.broadcasted_iota(jnp.int32, v.shape, 0)
    sel = jnp.zeros(v.shape, jnp.float32)
    for _ in range(k):
        m = jnp.max(v, axis=0, keepdims=True)
        first = jnp.min(jnp.where(v == m, ri, n), axis=0, keepdims=True)
        pick = ri == first
        sel = jnp.where(pick, 1.0, sel)
        v = jnp.where(pick, -jnp.inf, v)
    return sel > 0.5


def _pack_factor():
    return 4 // jnp.dtype(MXU_DTYPE).itemsize


def _pack_rows(x):
    if _pack_factor() == 1:
        return pltpu.bitcast(x, jnp.int32)
    half = x.shape[1] // 2
    b = pltpu.bitcast(x.astype(MXU_DTYPE).astype(jnp.float32), jnp.int32)
    return b[:, half:] | lax.shift_right_logical(b[:, :half], jnp.int32(16))


_HIGH_HALF = -(1 << 16)


def _unpack_rows_f32(p):
    if _pack_factor() == 1:
        return [pltpu.bitcast(p, jnp.float32)]
    lo = pltpu.bitcast(lax.shift_left(p, jnp.int32(16)), jnp.float32)
    hi = pltpu.bitcast(p & jnp.int32(_HIGH_HALF), jnp.float32)
    return [lo, hi]


def _unpack_rows(p):
    return [v.astype(MXU_DTYPE) for v in _unpack_rows_f32(p)]


def _mix_router_kernel(x_ref, ya_ref, yb_ref, yc_ref, wo_ref, g_ref, b_ref, wrt_ref, rb_ref, exp_ref,
                       x1_ref, x1p_ref, sel_ref, w_ref, pos_ref, cnt_ref, base_ref, *, tm):
    step = pl.program_id(0)
    f32 = jnp.float32

    @pl.when(step == 0)
    def _():
        base_ref[...] = jnp.zeros_like(base_ref)

    mix = _dot(ya_ref[...], wo_ref[0:MIX_A, :])
    mix = mix + _dot(yb_ref[...], wo_ref[MIX_A:MIX_A + CONV_CH, :])
    mix = mix + _dot(yc_ref[...], wo_ref[MIX_A + CONV_CH:, :])
    x1 = _layer_norm(ALPHA * x_ref[...] + mix, g_ref[...], b_ref[...])
    x1_ref[...] = x1
    x1p_ref[...] = _pack_rows(x1)

    lg = lax.dot_general(wrt_ref[...], x1, _NT, precision=lax.Precision.HIGHEST, preferred_element_type=f32)
    s = 1.0 / (1.0 + jnp.exp(-lg))
    sc = s + rb_ref[...]

    g3 = sc.reshape(N_GROUPS, GROUP_SIZE, tm)
    m1 = jnp.max(g3, axis=1, keepdims=True)
    is_m1 = g3 == m1
    n_m1 = jnp.sum(jnp.where(is_m1, 1.0, 0.0), axis=1, keepdims=True)
    m2 = jnp.max(jnp.where(is_m1, -jnp.inf, g3), axis=1, keepdims=True)
    gscore = (m1 + jnp.where(n_m1 > 1.0, m1, m2)).reshape(N_GROUPS, tm)
    gsel = jnp.where(_rank_rows(gscore, N_GROUPS) < float(TOPK_GROUPS), 1.0, 0.0)
    emask = _dot(exp_ref[...], gsel.astype(MXU_DTYPE)) > 0.5
    masked = jnp.where(emask, sc, -jnp.inf)
    sel = _top_rows(masked, TOP_K) & emask
    self_ = jnp.where(sel, 1.0, 0.0)
    top_s = jnp.where(sel, s, 0.0)
    w = top_s / jnp.sum(top_s, axis=0, keepdims=True) * ROUTED_SCALE

    t_r = lax.broadcasted_iota(jnp.int32, (tm, tm), 0)
    t_c = lax.broadcasted_iota(jnp.int32, (tm, tm), 1)
    upper = jnp.where(t_r < t_c, 1.0, 0.0).astype(MXU_DTYPE)
    pref = _dot(self_.astype(MXU_DTYPE), upper)
    base = base_ref[...]
    sel_ref[...] = self_
    w_ref[...] = w
    pos_ref[...] = base + pref
    base = base + jnp.sum(self_, axis=1, keepdims=True)
    base_ref[...] = base
    cnt_ref[...] = jnp.broadcast_to(base, cnt_ref.shape)


def _mix_router(x2, ya, yb, yc, w_out, ln_g, ln_b, w_router_t, router_bias, tm):
    T, D = x2.shape
    E = N_EXPERTS
    expand = (jnp.arange(E)[:, None] // GROUP_SIZE == jnp.arange(N_GROUPS)[None, :]).astype(MXU_DTYPE)
    row = lambda i: (i, 0)
    col = lambda i: (0, i)
    c2 = lambda i: (0, 0)
    f32 = jnp.float32
    return pl.pallas_call(
        functools.partial(_mix_router_kernel, tm=tm),
        grid=(T // tm,),
        in_specs=[
            pl.BlockSpec((tm, D), row),
            pl.BlockSpec((tm, MIX_A), row),
            pl.BlockSpec((tm, CONV_CH), row),
            pl.BlockSpec((tm, MIX_C), row),
            pl.BlockSpec(w_out.shape, c2),
            pl.BlockSpec((1, D), c2),
            pl.BlockSpec((1, D), c2),
            pl.BlockSpec((E, D), c2),
            pl.BlockSpec((E, 1), c2),
            pl.BlockSpec((E, N_GROUPS), c2),
        ],
        out_specs=[
            pl.BlockSpec((tm, D), row),
            pl.BlockSpec((tm, D // _pack_factor()), row),
            pl.BlockSpec((E, tm), col),
            pl.BlockSpec((E, tm), col),
            pl.BlockSpec((E, tm), col),
            pl.BlockSpec((E, LANES), c2),
        ],
        out_shape=[
            jax.ShapeDtypeStruct((T, D), f32),
            jax.ShapeDtypeStruct((T, D // _pack_factor()), jnp.int32),
            jax.ShapeDtypeStruct((E, T), f32),
            jax.ShapeDtypeStruct((E, T), f32),
            jax.ShapeDtypeStruct((E, T), f32),
            jax.ShapeDtypeStruct((E, LANES), f32),
        ],
        scratch_shapes=[pltpu.VMEM((E, 1), f32)],
        compiler_params=_cparams(("arbitrary",)),
        name="mix_router",
    )(x2, ya, yb, yc, w_out, ln_g, ln_b, w_router_t, router_bias, expand)


def _compact_kernel(sel_ref, w_ref, pos_ref, pstart_ref, low_ref, dest_ref, wk_ref):
    sel = sel_ref[...]
    on = sel > 0.5
    rank = _dot(low_ref[...], sel.astype(MXU_DTYPE))
    row = pstart_ref[...] + pos_ref[...]
    w = w_ref[...]
    dests, ws = [], []
    for k in range(TOP_K):
        m = on & (rank == float(k))
        dests.append(jnp.sum(jnp.where(m, row, 0.0), axis=0, keepdims=True))
        ws.append(jnp.sum(jnp.where(m, w, 0.0), axis=0, keepdims=True))
    dest_ref[...] = jnp.concatenate(dests, axis=0).astype(jnp.int32)
    wk_ref[...] = jnp.concatenate(ws, axis=0)


def _compact(sel_t, w_t, pos_t, pad_start, tm):
    E, T = sel_t.shape
    lower = (jnp.arange(E)[None, :] < jnp.arange(E)[:, None]).astype(MXU_DTYPE)
    col = lambda i: (0, i)
    c2 = lambda i: (0, 0)
    return pl.pallas_call(
        _compact_kernel,
        grid=(T // tm,),
        in_specs=[pl.BlockSpec((E, tm), col), pl.BlockSpec((E, tm), col), pl.BlockSpec((E, tm), col),
                  pl.BlockSpec((E, 1), c2), pl.BlockSpec((E, E), c2)],
        out_specs=[pl.BlockSpec((TOP_K, tm), col), pl.BlockSpec((TOP_K, tm), col)],
        out_shape=[jax.ShapeDtypeStruct((TOP_K, T), jnp.int32), jax.ShapeDtypeStruct((TOP_K, T), jnp.float32)],
        compiler_params=_cparams(("arbitrary",)),
        name="route_compact",
    )(sel_t, w_t, pos_t, pad_start, lower)


def _silu(g):
    return g / (1.0 + jnp.exp(-g))


def _expert_kernel(be_ref, nv_ref, nu_ref, xs_ref, wg_ref, wu_ref, wd_ref, ys_ref, wgb_ref, wub_ref, wdb_ref):
    i = pl.program_id(0)

    @pl.when((i == 0) | (be_ref[i] != be_ref[jnp.maximum(i - 1, 0)]))
    def _():
        wgb_ref[...] = wg_ref[0].astype(MXU_DTYPE)
        wub_ref[...] = wu_ref[0].astype(MXU_DTYPE)
        wdb_ref[...] = wd_ref[0].astype(MXU_DTYPE)

    @pl.when(i < nu_ref[0])
    def _():
        live = lax.broadcasted_iota(jnp.int32, (ROW_BLOCK, 1), 0) < nv_ref[i]
        parts = [jnp.where(live, v, jnp.zeros_like(v)) for v in _unpack_rows(xs_ref[...])]
        dk = wgb_ref.shape[0] // len(parts)

        def proj(w_ref):
            acc = _dot(parts[0], w_ref[0:dk, :])
            for n in range(1, len(parts)):
                acc = acc + _dot(parts[n], w_ref[n * dk:(n + 1) * dk, :])
            return acc

        a = (_silu(proj(wgb_ref)) * proj(wub_ref)).astype(MXU_DTYPE)
        ys_ref[...] = _pack_rows(_dot(a, wdb_ref[...]))


def _experts(xs, block_e, block_valid, n_used, w_gate, w_up, w_down):
    n_rows, W = xs.shape
    D = w_gate.shape[1]
    n_blocks = n_rows // ROW_BLOCK
    blk = lambda i, be, nv, nu: (jnp.minimum(i, nu[0] - 1), 0)
    wsel = lambda i, be, nv, nu: (be[i], 0, 0)
    return pl.pallas_call(
        _expert_kernel,
        grid_spec=pltpu.PrefetchScalarGridSpec(
            num_scalar_prefetch=3,
            grid=(n_blocks,),
            in_specs=[
                pl.BlockSpec((ROW_BLOCK, W), blk),
                pl.BlockSpec((1, D, D_EXPERT), wsel),
                pl.BlockSpec((1, D, D_EXPERT), wsel),
                pl.BlockSpec((1, D_EXPERT, D), wsel),
            ],
            out_specs=pl.BlockSpec((ROW_BLOCK, W), blk),
            scratch_shapes=[pltpu.VMEM((D, D_EXPERT), MXU_DTYPE), pltpu.VMEM((D, D_EXPERT), MXU_DTYPE),
                            pltpu.VMEM((D_EXPERT, D), MXU_DTYPE)],
        ),
        out_shape=jax.ShapeDtypeStruct((n_rows, W), xs.dtype),
        compiler_params=_cparams(("arbitrary",)),
        name="experts",
    )(block_e, block_valid, n_used, xs, w_gate, w_up, w_down)


SC_CORES = 2
SC_SUBCORES = 16
SC_GATHER_ROWS = 64
COMBINE_CHUNKS = 8


def _sc_gather_rows(table, idx):
    n = idx.shape[0]
    w = table.shape[1]
    n_workers = SC_CORES * SC_SUBCORES
    per_worker = n // n_workers
    assert n % n_workers == 0 and per_worker % SC_GATHER_ROWS == 0
    mesh = plsc.VectorSubcoreMesh(core_axis_name="c", subcore_axis_name="s")

    @functools.partial(
        pl.kernel, mesh=mesh,
        out_type=jax.ShapeDtypeStruct((n, w), table.dtype),
        scratch_types=[
            pltpu.VMEM((2, SC_GATHER_ROWS), jnp.int32),
            pltpu.VMEM((2, SC_GATHER_ROWS, w), table.dtype),
            pltpu.SemaphoreType.DMA((2,)),
        ],
        name="sc_gather_rows",
    )
    def gather(table_hbm, idx_hbm, out_hbm, idx_v, rows_v, sem):
        wid = lax.axis_index("s") * SC_CORES + lax.axis_index("c")
        base = wid * per_worker
        n_steps = per_worker // SC_GATHER_ROWS

        def gather_copy(slot):
            return pltpu.make_async_copy(table_hbm.at[idx_v.at[slot]], rows_v.at[slot], sem.at[slot])

        def start(step, slot):
            pltpu.sync_copy(idx_hbm.at[pl.ds(base + step * SC_GATHER_ROWS, SC_GATHER_ROWS)], idx_v.at[slot])
            gather_copy(slot).start()

        start(0, 0)

        @pl.loop(0, n_steps, step=2)
        def _(g):
            for slot in range(2):
                step = g + slot

                @pl.when(step + 1 < n_steps)
                def _():
                    start(step + 1, 1 - slot)

                gather_copy(slot).wait()
                pltpu.sync_copy(rows_v.at[slot], out_hbm.at[pl.ds(base + step * SC_GATHER_ROWS, SC_GATHER_ROWS)])

    return gather(table, idx)


SC_SCATTER_ROWS = 64


def _sc_scatter_rows(rows, idx3, n_out):
    n_src, w = rows.shape
    n_chunks, n_dst, batch = idx3.shape
    n_workers = SC_CORES * SC_SUBCORES
    assert batch == SC_SCATTER_ROWS and n_chunks * batch == n_src and n_chunks % (2 * n_workers) == 0
    per_worker = n_chunks // n_workers
    mesh = plsc.VectorSubcoreMesh(core_axis_name="c", subcore_axis_name="s")

    @functools.partial(
        pl.kernel, mesh=mesh,
        out_type=jax.ShapeDtypeStruct((n_out, w), rows.dtype),
        scratch_types=[
            pltpu.VMEM((2, n_dst, batch), jnp.int32),
            pltpu.VMEM((2, batch, w), rows.dtype),
            pltpu.SemaphoreType.DMA((2,)),
            pltpu.SemaphoreType.DMA((2,)),
            pltpu.SemaphoreType.DMA,
        ],
        name="sc_scatter_rows",
    )
    def scatter(rows_hbm, idx_hbm, out_hbm, idx_v, rows_v, load_sem, idx_sem, store_sem):
        wid = lax.axis_index("s") * SC_CORES + lax.axis_index("c")

        def load_copy(step, slot):
            c = wid * per_worker + step
            return pltpu.make_async_copy(rows_hbm.at[pl.ds(c * batch, batch)], rows_v.at[slot], load_sem.at[slot])

        def idx_copy(step, slot):
            return pltpu.make_async_copy(idx_hbm.at[wid * per_worker + step], idx_v.at[slot], idx_sem.at[slot])

        def load(step, slot):
            idx_copy(step, slot).start()
            load_copy(step, slot).start()

        def store_copy(slot, k):
            return pltpu.make_async_copy(rows_v.at[slot], out_hbm.at[idx_v.at[slot].at[k]], store_sem)

        load(0, 0)

        @pl.loop(0, per_worker, step=2)
        def _(g):
            for slot in range(2):
                step = g + slot

                @pl.when(step + 1 < per_worker)
                def _():
                    load(step + 1, 1 - slot)

                idx_copy(step, slot).wait()
                load_copy(step, slot).wait()
                for k in range(n_dst):
                    store_copy(slot, k).start()
                for k in range(n_dst):
                    store_copy(slot, k).wait()

    return scatter(rows, idx3)


def _combine2_kernel(wk_ref, x1_ref, g_ref_rows, wsg_ref, wsu_ref, wsd_ref, g_ref, b_ref, o_ref):
    x1 = x1_ref[...]
    xb = x1.astype(MXU_DTYPE)
    a = (_silu(_dot(xb, wsg_ref[...])) * _dot(xb, wsu_ref[...])).astype(MXU_DTYPE)
    shared = _dot(a, wsd_ref[...])
    wk = wk_ref[...].T
    groups = [wk[:, 0:1] * v for v in _unpack_rows_f32(g_ref_rows[0])]
    for k in range(1, TOP_K):
        groups = [g + wk[:, k:k + 1] * v for g, v in zip(groups, _unpack_rows_f32(g_ref_rows[k]))]
    routed = jnp.concatenate(groups, axis=1)
    o_ref[...] = _layer_norm(ALPHA * x1 + (routed + shared), g_ref[...], b_ref[...])


def _combine2_kernel_into(wk_ref, x1_ref, g_ref_rows, wsg_ref, wsu_ref, wsd_ref, g_ref, b_ref, prev_ref, o_ref):
    del prev_ref
    _combine2_kernel(wk_ref, x1_ref, g_ref_rows, wsg_ref, wsu_ref, wsd_ref, g_ref, b_ref, o_ref)


def _combine2(wk_t, x1, gathered, w_sg, w_su, w_sd, ln_g, ln_b, tc, chunk, prev):
    T, D = x1.shape
    _, t_chunk, W = gathered.shape
    base = chunk * (t_chunk // tc)
    row = lambda i: (base + i, 0)
    c2 = lambda i: (0, 0)
    in_specs = [
        pl.BlockSpec((TOP_K, tc), lambda i: (0, base + i)),
        pl.BlockSpec((tc, D), row),
        pl.BlockSpec((TOP_K, tc, W), lambda i: (0, i, 0)),
        pl.BlockSpec(w_sg.shape, c2),
        pl.BlockSpec(w_su.shape, c2),
        pl.BlockSpec(w_sd.shape, c2),
        pl.BlockSpec((1, D), c2),
        pl.BlockSpec((1, D), c2),
    ]
    args = [wk_t, x1, gathered, w_sg, w_su, w_sd, ln_g, ln_b]
    if prev is None:
        body, aliases = _combine2_kernel, {}
    else:
        body, aliases = _combine2_kernel_into, {len(args): 0}
        in_specs.append(pl.BlockSpec(memory_space=pl.ANY))
        args.append(prev)
    return pl.pallas_call(
        body,
        grid=(t_chunk // tc,),
        in_specs=in_specs,
        out_specs=pl.BlockSpec((tc, D), row),
        out_shape=jax.ShapeDtypeStruct((T, D), jnp.float32),
        input_output_aliases=aliases,
        compiler_params=_cparams(("arbitrary",)),
        name="combine",
    )(*args)


def _split_w_in(w_in):
    o_kv = Q_RANK
    o_ki = o_kv + KV_RANK
    o_iw = o_ki + IDX_DIM
    o_rest = o_iw + N_IDX_HEADS
    w_small = jnp.pad(w_in[:, o_ki:o_rest], ((0, 0), (0, LANES - IDX_DIM - N_IDX_HEADS)))
    return jnp.concatenate([w_in[:, :o_ki], w_in[:, o_rest:], w_small], axis=1).astype(MXU_DTYPE)


def _stages(x, mem, w_in, q_norm_g, kv_norm_g, w_uq, w_uk, w_uv, w_qidx, rel_bias, conv_w, w_mem_k, w_mem_v, w_out, ln1_g, ln1_b, w_router, router_bias, w_e_gate, w_e_up, w_e_down, w_s_gate, w_s_up, w_s_down, ln2_g, ln2_b):
    B, S, D = x.shape
    T = B * S
    bf = MXU_DTYPE
    assert w_in.shape[0] == DEPTH == 1, "single-layer stack"
    l = 0
    res = {}
    x2 = x.reshape(T, D)
    cq, ckv, ckvt, kidx, iwt, yb, yc = _proj(
        x2, mem, _split_w_in(w_in[l]), q_norm_g[l].reshape(1, -1), kv_norm_g[l].reshape(1, -1), conv_w[l],
        w_mem_k[l].astype(bf), w_mem_v[l].astype(bf), B, S, tm=min(512, S))
    res.update(c_q=cq, c_kv=ckv, k_idx=kidx, y_b=yb, y_c=yc,
               idx_w=jnp.swapaxes(iwt, 1, 2) / (N_IDX_HEADS ** -0.5 * IDX_DIM ** -0.5))
    bias_t = _bias_tiles(rel_bias)
    ya = _dsa(cq, iwt, kidx, ckv, ckvt,
              w_qidx[l].reshape(Q_RANK, -1).astype(bf), w_uq[l].reshape(Q_RANK, -1).astype(bf),
              jnp.transpose(w_uk[l], (1, 0, 2)).astype(bf), jnp.transpose(w_uv[l], (1, 2, 0)).astype(bf),
              bias_t, B, S)
    res.update(y_a=ya)

    x1, x1p, sel_t, w_t, pos_t, cnt = _mix_router(
        x2, ya, yb, yc, w_out[l].astype(bf), ln1_g[l].reshape(1, -1), ln1_b[l].reshape(1, -1),
        w_router[l].T, router_bias[l].reshape(-1, 1), tm=min(512, T))
    res.update(x1=x1)

    counts = cnt[:, 0].astype(jnp.int32)
    padded = (counts + ROW_BLOCK - 1) // ROW_BLOCK * ROW_BLOCK
    pad_end = jnp.cumsum(padded)
    pad_start = pad_end - padded
    n_blocks = -(-(T * TOP_K) // ROW_BLOCK) + N_EXPERTS
    n_rows = n_blocks * ROW_BLOCK
    block_start = jnp.arange(n_blocks, dtype=jnp.int32) * ROW_BLOCK
    block_e = jnp.minimum(jnp.sum((pad_end[None, :] <= block_start[:, None]).astype(jnp.int32), axis=1),
                          N_EXPERTS - 1)
    n_used = (pad_end[-1:] // ROW_BLOCK).astype(jnp.int32)

    dest_t, wk_t = _compact(sel_t, w_t, pos_t, pad_start.astype(jnp.float32).reshape(-1, 1), tm=min(512, T))
    block_valid = jnp.clip((pad_start + counts)[block_e] - block_start, 0, ROW_BLOCK).astype(jnp.int32)
    bt = SC_SCATTER_ROWS
    idx3 = jnp.transpose(dest_t.reshape(TOP_K, T // bt, bt), (1, 0, 2))
    xs = _sc_scatter_rows(x1p, idx3, n_rows)
    ys = _experts(xs, block_e, block_valid, n_used, w_e_gate[l], w_e_up[l], w_e_down[l])
    n_chunks = COMBINE_CHUNKS if T % (COMBINE_CHUNKS * 256) == 0 else 1
    t_chunk = T // n_chunks
    out = None
    for c in range(n_chunks):
        idx_c = dest_t[:, c * t_chunk:(c + 1) * t_chunk].reshape(-1)
        gathered = _sc_gather_rows(ys, idx_c).reshape(TOP_K, t_chunk, -1)
        out = _combine2(wk_t, x1, gathered, w_s_gate[l].astype(bf), w_s_up[l].astype(bf), w_s_down[l].astype(bf),
                        ln2_g[l].reshape(1, -1), ln2_b[l].reshape(1, -1), tc=min(256, t_chunk), chunk=c, prev=out)
    res.update(out=out.reshape(B, S, D))
    return res


def kernel(x, mem, w_in, q_norm_g, kv_norm_g, w_uq, w_uk, w_uv, w_qidx, rel_bias, conv_w, w_mem_k, w_mem_v, w_out, ln1_g, ln1_b, w_router, router_bias, w_e_gate, w_e_up, w_e_down, w_s_gate, w_s_up, w_s_down, ln2_g, ln2_b):
    return _stages(x, mem, w_in, q_norm_g, kv_norm_g, w_uq, w_uk, w_uv, w_qidx, rel_bias, conv_w, w_mem_k, w_mem_v, w_out, ln1_g, ln1_b, w_router, router_bias, w_e_gate, w_e_up, w_e_down, w_s_gate, w_s_up, w_s_down, ln2_g, ln2_b)["out"]
```

```python
import functools
import math

import jax
import jax.numpy as jnp
from jax import lax
from jax.experimental import pallas as pl
from jax.experimental.pallas import tpu as pltpu
from jax.experimental.pallas import tpu_sc as plsc

N_HEADS_A = 8
HEAD_DIM = 64
Q_RANK = 256
KV_RANK = 128
N_IDX_HEADS = 8
IDX_DIM = 64
TOPK_MAX = 256
REL_BUCKETS = 32
REL_MAX_DIST = 128
CONV_CH = 256
CONV_WIDTH = 3
N_MEM_HEADS = 4
MIX_A = N_HEADS_A * HEAD_DIM
MIX_C = N_MEM_HEADS * HEAD_DIM
N_EXPERTS = 64
N_GROUPS = 8
GROUP_SIZE = N_EXPERTS // N_GROUPS
TOPK_GROUPS = 4
TOP_K = 8
D_EXPERT = 256
ROUTED_SCALE = 2.5
DEPTH = 1
ALPHA = (2.0 * DEPTH) ** 0.25
LN_EPS = 1e-5
RMS_EPS = 1e-6
LOG2_E = math.log2(math.e)

LANES = 128
SUBLANES = 8
QB = 128
F32_LOWEST = -3.4028234663852886e38
VMEM_LIMIT = 56 * 1024 * 1024
MXU_DTYPE = jnp.bfloat16
ROW_BLOCK = 1024

_NT = (((1,), (1,)), ((), ()))


def _dot(a, b):
    return jnp.dot(a, b, preferred_element_type=jnp.float32)


def _dot_nt(a, b):
    return lax.dot_general(a, b, _NT, preferred_element_type=jnp.float32)


def _cparams(sem):
    return pltpu.CompilerParams(dimension_semantics=sem, vmem_limit_bytes=VMEM_LIMIT)


def _bias_kernel(rb_ref, o_ref):
    s = lax.broadcasted_iota(jnp.int32, (QB, QB), 0)
    t = lax.broadcasted_iota(jnp.int32, (QB, QB), 1)
    max_exact = REL_BUCKETS // 2
    for tile in range(3):
        n = jnp.maximum(t - s + (2 - tile) * QB, 0)
        nf = jnp.maximum(n.astype(jnp.float32), 1.0)
        large = max_exact + (jnp.log(nf / max_exact) / math.log(REL_MAX_DIST / max_exact)
                             * (REL_BUCKETS - max_exact)).astype(jnp.int32)
        large = jnp.minimum(large, REL_BUCKETS - 1)
        bucket = jnp.where(n < max_exact, n, large)
        for h in range(N_HEADS_A):
            acc = jnp.zeros((QB, QB), jnp.float32)
            for b in range(REL_BUCKETS):
                acc = jnp.where(bucket == b, rb_ref[b, h], acc)
            o_ref[tile, h] = acc * LOG2_E


def _bias_tiles(rel_bias):
    return pl.pallas_call(
        _bias_kernel,
        in_specs=[pl.BlockSpec(memory_space=pltpu.SMEM)],
        out_specs=pl.BlockSpec(memory_space=pltpu.VMEM),
        out_shape=jax.ShapeDtypeStruct((3, N_HEADS_A, QB, QB), jnp.float32),
        name="bias_tiles",
    )(rel_bias)


def _proj_kernel(x_ref, mem_ref, wm_ref, qg_ref, kvg_ref, cw_ref, wmk_ref, wmv_ref,
                 cq_ref, ckv_ref, ckvt_ref, kidx_ref, iwt_ref, yb_ref, yc_ref,
                 carry_ref, mk_ref, mv_ref, *, tm):
    si = pl.program_id(1)

    @pl.when(si == 0)
    def _():
        carry_ref[...] = jnp.zeros_like(carry_ref)
        mb = mem_ref[0].astype(MXU_DTYPE)
        mk_ref[...] = _dot(mb, wmk_ref[...]).astype(MXU_DTYPE)
        mv_ref[...] = _dot(mb, wmv_ref[...]).astype(MXU_DTYPE)

    xb = x_ref[...].astype(MXU_DTYPE)
    p = _dot(xb, wm_ref[...])
    small = p[:, p.shape[1] - LANES:]

    o = 0
    cq = p[:, o:o + Q_RANK]; o += Q_RANK
    ckv = p[:, o:o + KV_RANK]; o += KV_RANK
    g_b = p[:, o:o + CONV_CH]; o += CONV_CH
    g_c = p[:, o:o + CONV_CH]; o += CONV_CH
    h_c = p[:, o:o + CONV_CH]; o += CONV_CH
    q_mem = p[:, o:o + MIX_C]

    cq = cq * lax.rsqrt(jnp.mean(cq * cq, axis=-1, keepdims=True) + RMS_EPS) * qg_ref[...]
    ckv = ckv * lax.rsqrt(jnp.mean(ckv * ckv, axis=-1, keepdims=True) + RMS_EPS) * kvg_ref[...]
    cq_ref[...] = cq.astype(MXU_DTYPE)
    ckv_b = ckv.astype(MXU_DTYPE)
    ckv_ref[...] = ckv_b
    ckvt_ref[0] = ckv.T.astype(MXU_DTYPE)

    kidx_ref[...] = small[:, :IDX_DIM].astype(MXU_DTYPE)
    small_t = small.T
    iwt_ref[0] = small_t[IDX_DIM:IDX_DIM + N_IDX_HEADS, :] * (N_IDX_HEADS ** -0.5 * IDX_DIM ** -0.5)

    u = g_c * h_c
    rows = lax.broadcasted_iota(jnp.int32, (tm, 1), 0)
    c6 = carry_ref[SUBLANES - 2:SUBLANES - 1, :]
    c7 = carry_ref[SUBLANES - 1:SUBLANES, :]
    u1 = jnp.where(rows == 0, c7, pltpu.roll(u, 1, 0))
    u2 = jnp.where(rows == 0, c6, jnp.where(rows == 1, c7, pltpu.roll(u, 2, 0)))
    y = cw_ref[0:1, :] * u2
    y = y + cw_ref[1:2, :] * u1
    y = y + cw_ref[2:3, :] * u
    yb_ref[...] = (g_b * y).astype(MXU_DTYPE)
    carry_ref[...] = u[tm - SUBLANES:, :]

    qm = q_mem.astype(MXU_DTYPE)
    outs = []
    for h in range(N_MEM_HEADS):
        sl = slice(h * HEAD_DIM, (h + 1) * HEAD_DIM)
        lg = _dot_nt(qm[:, sl], mk_ref[:, sl]) * (HEAD_DIM ** -0.5)
        lg = lg - jnp.max(lg, axis=-1, keepdims=True)
        e = jnp.exp(lg)
        pr = e / jnp.sum(e, axis=-1, keepdims=True)
        outs.append(_dot(pr.astype(MXU_DTYPE), mv_ref[:, sl]))
    yc_ref[...] = jnp.concatenate(outs, axis=-1).astype(MXU_DTYPE)


def _proj(x2, mem, w_main, q_g, kv_g, conv_w, w_mk, w_mv, B, S, tm):
    T, D = x2.shape
    n_mem = mem.shape[1]
    ns = S // tm
    row = lambda b, s: (b * ns + s, 0)
    const2 = lambda b, s: (0, 0)
    bf = MXU_DTYPE
    return pl.pallas_call(
        functools.partial(_proj_kernel, tm=tm),
        grid=(B, ns),
        in_specs=[
            pl.BlockSpec((tm, D), row),
            pl.BlockSpec((1, n_mem, D), lambda b, s: (b, 0, 0)),
            pl.BlockSpec(w_main.shape, const2),
            pl.BlockSpec(q_g.shape, const2),
            pl.BlockSpec(kv_g.shape, const2),
            pl.BlockSpec(conv_w.shape, const2),
            pl.BlockSpec(w_mk.shape, const2),
            pl.BlockSpec(w_mv.shape, const2),
        ],
        out_specs=[
            pl.BlockSpec((tm, Q_RANK), row),
            pl.BlockSpec((tm, KV_RANK), row),
            pl.BlockSpec((1, KV_RANK, tm), lambda b, s: (b, 0, s)),
            pl.BlockSpec((tm, IDX_DIM), row),
            pl.BlockSpec((1, N_IDX_HEADS, tm), lambda b, s: (b, 0, s)),
            pl.BlockSpec((tm, CONV_CH), row),
            pl.BlockSpec((tm, MIX_C), row),
        ],
        out_shape=[
            jax.ShapeDtypeStruct((T, Q_RANK), bf),
            jax.ShapeDtypeStruct((T, KV_RANK), bf),
            jax.ShapeDtypeStruct((B, KV_RANK, S), bf),
            jax.ShapeDtypeStruct((T, IDX_DIM), bf),
            jax.ShapeDtypeStruct((B, N_IDX_HEADS, S), jnp.float32),
            jax.ShapeDtypeStruct((T, CONV_CH), bf),
            jax.ShapeDtypeStruct((T, MIX_C), bf),
        ],
        scratch_shapes=[
            pltpu.VMEM((SUBLANES, CONV_CH), jnp.float32),
            pltpu.VMEM((n_mem, MIX_C), bf),
            pltpu.VMEM((n_mem, MIX_C), bf),
        ],
        compiler_params=_cparams(("arbitrary", "arbitrary")),
        name="proj",
    )(x2, mem, w_main, q_g, kv_g, conv_w, w_mk, w_mv)


def _key_to_f32(key):
    bits = jnp.where(key < 0, key ^ jnp.int32(0x7FFFFFFF), key)
    return pltpu.bitcast(bits, jnp.float32)


def _colsum8(v):
    return jnp.sum(v.reshape(QB // SUBLANES, SUBLANES, QB), axis=0)


def _colmax8(v):
    return jnp.max(v.reshape(QB // SUBLANES, SUBLANES, QB), axis=0)


UNROLL_WIDTHS = (8, 4, 2, 1)


def _dsa_kernel(cq_ref, iwt_ref, kidx_ref, ckv_ref, ckvt_ref, wqi_ref, wuq_ref, wuk_ref, wuvt_ref,
                bias_ref, o_ref, wfold_ref, qidx_ref, qlat_ref, score_ref, logit_ref, acc_ref,
                *, k_sel, idx_bits):
    i = pl.program_id(1)
    f32 = jnp.float32
    bf = MXU_DTYPE
    n_blocks = i + 1
    n_blocks = n_blocks + jnp.where((n_blocks % 4 == 3) & (n_blocks < pl.num_programs(1)), 1, 0)
    s_loc = lax.broadcasted_iota(jnp.int32, (QB, QB), 0)
    t_glob = i * QB + lax.broadcasted_iota(jnp.int32, (QB, QB), 1)

    def blk(jb):
        return pl.multiple_of(jb * QB, QB)

    def block_loop(fn, init):
        c, start = init, 0
        for width in UNROLL_WIDTHS:
            n = (n_blocks - start) // width
            c = lax.fori_loop(0, n, lambda it, c, w=width, s=start: fn(s + it * w, w, c), c)
            start = start + n * width
        return c

    @pl.when(i == 0)
    def _():
        for h in range(N_HEADS_A):
            wfold_ref[:, h * KV_RANK:(h + 1) * KV_RANK] = (
                _dot_nt(wuq_ref[:, h * HEAD_DIM:(h + 1) * HEAD_DIM], wuk_ref[h])
                * (HEAD_DIM ** -0.5 * LOG2_E)).astype(bf)

    cq = cq_ref[...]
    q_idx = _dot(cq, wqi_ref[...]).astype(bf)
    q_lat = _dot(cq, wfold_ref[...]).astype(bf)
    for h in range(N_HEADS_A):
        qidx_ref[h * QB:(h + 1) * QB, :] = q_idx[:, h * IDX_DIM:(h + 1) * IDX_DIM]
        qlat_ref[h * QB:(h + 1) * QB, :] = q_lat[:, h * KV_RANK:(h + 1) * KV_RANK]
    iw = iwt_ref[0]

    def score_body(jb0, nb, n_pos8):
        d_blk = _dot_nt(kidx_ref[pl.ds(blk(jb0), nb * QB), :], qidx_ref[...])
        for sb in range(nb):
            off = blk(jb0 + sb)
            d_all = d_blk[sb * QB:(sb + 1) * QB, :]
            acc = jnp.maximum(d_all[:, 0:QB], 0.0) * iw[0:1, :]
            for h in range(1, N_IDX_HEADS):
                acc = acc + jnp.maximum(d_all[:, h * QB:(h + 1) * QB], 0.0) * iw[h:h + 1, :]
            sc = jnp.where(s_loc + off <= t_glob, acc + 0.0, F32_LOWEST)
            score_ref[pl.ds(off, QB), :] = sc
            n_pos8 = n_pos8 + _colsum8(jnp.where(sc >= 0.0, 1.0, 0.0))
        return n_pos8

    n_pos8 = block_loop(score_body, jnp.zeros((SUBLANES, QB), f32))

    def count_where(pred):
        def body(jb0, nb, acc):
            for sb in range(nb):
                off = blk(jb0 + sb)
                acc = acc + _colsum8(jnp.where(pred(score_ref[pl.ds(off, QB), :], off), 1.0, 0.0))
            return acc
        acc = block_loop(body, jnp.zeros((SUBLANES, QB), f32))
        return jnp.sum(acc, axis=0, keepdims=True)

    kf = float(k_sel)

    def search():
        c0 = jnp.sum(n_pos8, axis=0, keepdims=True)
        cand0 = jnp.where(c0 >= kf, jnp.int32(0), jnp.int32(-2 ** 31))
        n_ge0 = jnp.where(c0 >= kf, c0, -1.0)

        def bit_body(it, carry):
            cand, n_ge = carry
            trial = cand + lax.shift_left(jnp.int32(1), 30 - it)
            tf = _key_to_f32(trial)
            cnt = count_where(lambda sc, off: sc >= tf)
            take = cnt >= kf
            return jnp.where(take, trial, cand), jnp.where(take, cnt, n_ge)

        cand, n_ge = lax.fori_loop(0, 31, bit_body, (cand0, n_ge0))
        thr = _key_to_f32(cand)
        keep_all_ties = jnp.full((1, QB), 2 ** idx_bits - 1, jnp.int32)

        def resolve_ties():
            n_gt = count_where(lambda sc, off: sc > thr)
            n_eq = count_where(lambda sc, off: sc == thr)
            need = kf - n_gt

            def tie_search():
                def tbody(it, xcut):
                    trial = xcut + lax.shift_left(jnp.int32(1), idx_bits - 1 - it)
                    cnt = count_where(lambda sc, off: (sc == thr) & (s_loc + off < trial))
                    return jnp.where(cnt < need, trial, xcut)
                return lax.fori_loop(0, idx_bits, tbody, jnp.zeros((1, QB), jnp.int32))

            return lax.cond(jnp.max(n_eq - need) > 0.0, tie_search, lambda: keep_all_ties)

        xcut = lax.cond(jnp.max(jnp.abs(n_ge - kf)) > 0.0, resolve_ties, lambda: keep_all_ties)
        return thr, xcut

    def no_search():
        return jnp.full((1, QB), F32_LOWEST, f32), jnp.full((1, QB), 2 ** idx_bits - 1, jnp.int32)

    thr, xcut = lax.cond((i + 1) * QB > k_sel, search, no_search)

    def selection_mask(off):
        sc = score_ref[pl.ds(off, QB), :]
        s_glob = s_loc + off
        keep = ((sc > thr) | ((sc == thr) & (s_glob <= xcut))) & (s_glob <= t_glob)
        return jnp.where(keep, 0.0, -jnp.inf)

    acc_ref[...] = jnp.zeros_like(acc_ref)

    def att_body(jb0, nb, carry):
        m, l8 = list(carry[0]), list(carry[1])
        rows = nb * QB
        lg_blk = _dot_nt(ckv_ref[pl.ds(blk(jb0), rows), :], qlat_ref[...])
        blk_max = [None] * N_HEADS_A
        for sb in range(nb):
            off = blk(jb0 + sb)
            msk = selection_mask(off)
            bsel = jnp.clip(jb0 + sb - i + 2, 0, 2)
            for h in range(N_HEADS_A):
                lgh = lg_blk[sb * QB:(sb + 1) * QB, h * QB:(h + 1) * QB] + bias_ref[bsel, h] + msk
                logit_ref[sb * QB:(sb + 1) * QB, h * QB:(h + 1) * QB] = lgh
                cm = _colmax8(lgh)
                blk_max[h] = cm if blk_max[h] is None else jnp.maximum(blk_max[h], cm)
        ps, scales = [], []
        for h in range(N_HEADS_A):
            m_new = jnp.maximum(m[h], jnp.max(blk_max[h], axis=0, keepdims=True))
            m_ref = jnp.where(m_new == -jnp.inf, 0.0, m_new)
            p = jnp.exp2(logit_ref[0:rows, h * QB:(h + 1) * QB] - m_ref)
            scale = jnp.exp2(m[h] - m_ref)
            l8[h] = l8[h] * scale + jnp.sum(p.reshape(rows // SUBLANES, SUBLANES, QB), axis=0)
            m[h] = m_new
            ps.append(p.astype(bf))
            scales.append(scale)
        pv = _dot(ckvt_ref[0, :, pl.ds(blk(jb0), rows)], jnp.concatenate(ps, axis=1))
        for h in range(N_HEADS_A):
            hs = slice(h * QB, (h + 1) * QB)
            acc_ref[:, hs] = acc_ref[:, hs] * scales[h] + pv[:, hs]
        return tuple(m), tuple(l8)

    _, l8 = block_loop(att_body, (tuple(jnp.full((1, QB), -jnp.inf, f32) for _ in range(N_HEADS_A)),
                                  tuple(jnp.zeros((SUBLANES, QB), f32) for _ in range(N_HEADS_A))))

    outs = []
    for h in range(N_HEADS_A):
        l_row = jnp.sum(l8[h], axis=0, keepdims=True)
        o_lat_t = (acc_ref[:, h * QB:(h + 1) * QB] / l_row).astype(bf)
        outs.append(_dot(wuvt_ref[h], o_lat_t))
    o_ref[...] = jnp.concatenate(outs, axis=0).T.astype(o_ref.dtype)


def _dsa(cq, iwt, kidx, ckv, ckvt, w_qidx, w_uq, w_uk_h, w_uvt_h, bias_tiles, B, S):
    T = cq.shape[0]
    assert S % QB == 0 and QB >= REL_MAX_DIST
    nq = S // QB
    k_sel = min(TOPK_MAX, S // 4)
    idx_bits = max(1, (S - 1).bit_length())
    c2 = lambda b, i: (0, 0)
    c3 = lambda b, i: (0, 0, 0)
    return pl.pallas_call(
        functools.partial(_dsa_kernel, k_sel=k_sel, idx_bits=idx_bits),
        grid=(B, nq),
        in_specs=[
            pl.BlockSpec((QB, Q_RANK), lambda b, i: (b * nq + i, 0)),
            pl.BlockSpec((1, N_IDX_HEADS, QB), lambda b, i: (b, 0, i)),
            pl.BlockSpec((S, IDX_DIM), lambda b, i: (b, 0)),
            pl.BlockSpec((S, KV_RANK), lambda b, i: (b, 0)),
            pl.BlockSpec((1, KV_RANK, S), lambda b, i: (b, 0, 0)),
            pl.BlockSpec(w_qidx.shape, c2),
            pl.BlockSpec(w_uq.shape, c2),
            pl.BlockSpec(w_uk_h.shape, c3),
            pl.BlockSpec(w_uvt_h.shape, c3),
            pl.BlockSpec(bias_tiles.shape, lambda b, i: (0, 0, 0, 0)),
        ],
        out_specs=pl.BlockSpec((QB, MIX_A), lambda b, i: (b * nq + i, 0)),
        out_shape=jax.ShapeDtypeStruct((T, MIX_A), MXU_DTYPE),
        scratch_shapes=[
            pltpu.VMEM((Q_RANK, N_HEADS_A * KV_RANK), MXU_DTYPE),
            pltpu.VMEM((N_IDX_HEADS * QB, IDX_DIM), MXU_DTYPE),
            pltpu.VMEM((N_HEADS_A * QB, KV_RANK), MXU_DTYPE),
            pltpu.VMEM((S, QB), jnp.float32),
            pltpu.VMEM((max(UNROLL_WIDTHS) * QB, N_HEADS_A * QB), jnp.float32),
            pltpu.VMEM((KV_RANK, N_HEADS_A * QB), jnp.float32),
        ],
        compiler_params=_cparams(("arbitrary", "arbitrary")),
        name="dsa",
    )(cq, iwt, kidx, ckv, ckvt, w_qidx, w_uq, w_uk_h, w_uvt_h, bias_tiles)


def _layer_norm(xf, g, b):
    mu = jnp.mean(xf, axis=-1, keepdims=True)
    xc = xf - mu
    var = jnp.mean(xc * xc, axis=-1, keepdims=True)
    return xc * lax.rsqrt(var + LN_EPS) * g + b


def _rank_rows(v, n):
    ri = lax.broadcasted_iota(jnp.int32, v.shape, 0)
    rank = jnp.zeros(v.shape, jnp.float32)
    for r2 in range(n):
        row = v[r2:r2 + 1, :]
        beats = (row > v) | ((row == v) & (ri > r2))
        rank = rank + jnp.where(beats, 1.0, 0.0)
    return rank


def _top_rows(v, k):
    n = v.shape[0]
    ri = lax.broadcasted_iota(jnp.int32, v.shape, 0)
    sel = jnp.zeros(v.shape, jnp.float32)
    for _ in range(k):
        m = jnp.max(v, axis=0, keepdims=True)
        first = jnp.min(jnp.where(v == m, ri, n), axis=0, keepdims=True)
        pick = ri == first
        sel = jnp.where(pick, 1.0, sel)
        v = jnp.where(pick, -jnp.inf, v)
    return sel > 0.5


def _pack_factor():
    return 4 // jnp.dtype(MXU_DTYPE).itemsize


def _pack_rows(x):
    if _pack_factor() == 1:
        return pltpu.bitcast(x, jnp.int32)
    half = x.shape[1] // 2
    b = pltpu.bitcast(x.astype(MXU_DTYPE).astype(jnp.float32), jnp.int32)
    return b[:, half:] | lax.shift_right_logical(b[:, :half], jnp.int32(16))


_HIGH_HALF = -(1 << 16)


def _unpack_rows_f32(p):
    if _pack_factor() == 1:
        return [pltpu.bitcast(p, jnp.float32)]
    lo = pltpu.bitcast(lax.shift_left(p, jnp.int32(16)), jnp.float32)
    hi = pltpu.bitcast(p & jnp.int32(_HIGH_HALF), jnp.float32)
    return [lo, hi]


def _unpack_rows(p):
    return [v.astype(MXU_DTYPE) for v in _unpack_rows_f32(p)]


def _mix_router_kernel(x_ref, ya_ref, yb_ref, yc_ref, wo_ref, g_ref, b_ref, wrt_ref, rb_ref, exp_ref,
                       x1_ref, x1p_ref, sel_ref, w_ref, pos_ref, cnt_ref, base_ref, *, tm):
    step = pl.program_id(0)
    f32 = jnp.float32

    @pl.when(step == 0)
    def _():
        base_ref[...] = jnp.zeros_like(base_ref)

    mix = _dot(ya_ref[...], wo_ref[0:MIX_A, :])
    mix = mix + _dot(yb_ref[...], wo_ref[MIX_A:MIX_A + CONV_CH, :])
    mix = mix + _dot(yc_ref[...], wo_ref[MIX_A + CONV_CH:, :])
    x1 = _layer_norm(ALPHA * x_ref[...] + mix, g_ref[...], b_ref[...])
    x1_ref[...] = x1
    x1p_ref[...] = _pack_rows(x1)

    lg = lax.dot_general(wrt_ref[...], x1, _NT, precision=lax.Precision.HIGHEST, preferred_element_type=f32)
    s = 1.0 / (1.0 + jnp.exp(-lg))
    sc = s + rb_ref[...]

    g3 = sc.reshape(N_GROUPS, GROUP_SIZE, tm)
    m1 = jnp.max(g3, axis=1, keepdims=True)
    is_m1 = g3 == m1
    n_m1 = jnp.sum(jnp.where(is_m1, 1.0, 0.0), axis=1, keepdims=True)
    m2 = jnp.max(jnp.where(is_m1, -jnp.inf, g3), axis=1, keepdims=True)
    gscore = (m1 + jnp.where(n_m1 > 1.0, m1, m2)).reshape(N_GROUPS, tm)
    gsel = jnp.where(_rank_rows(gscore, N_GROUPS) < float(TOPK_GROUPS), 1.0, 0.0)
    emask = _dot(exp_ref[...], gsel.astype(MXU_DTYPE)) > 0.5
    masked = jnp.where(emask, sc, -jnp.inf)
    sel = _top_rows(masked, TOP_K) & emask
    self_ = jnp.where(sel, 1.0, 0.0)
    top_s = jnp.where(sel, s, 0.0)
    w = top_s / jnp.sum(top_s, axis=0, keepdims=True) * ROUTED_SCALE

    t_r = lax.broadcasted_iota(jnp.int32, (tm, tm), 0)
    t_c = lax.broadcasted_iota(jnp.int32, (tm, tm), 1)
    upper = jnp.where(t_r < t_c, 1.0, 0.0).astype(MXU_DTYPE)
    pref = _dot(self_.astype(MXU_DTYPE), upper)
    base = base_ref[...]
    sel_ref[...] = self_
    w_ref[...] = w
    pos_ref[...] = base + pref
    base = base + jnp.sum(self_, axis=1, keepdims=True)
    base_ref[...] = base
    cnt_ref[...] = jnp.broadcast_to(base, cnt_ref.shape)


def _mix_router(x2, ya, yb, yc, w_out, ln_g, ln_b, w_router_t, router_bias, tm):
    T, D = x2.shape
    E = N_EXPERTS
    expand = (jnp.arange(E)[:, None] // GROUP_SIZE == jnp.arange(N_GROUPS)[None, :]).astype(MXU_DTYPE)
    row = lambda i: (i, 0)
    col = lambda i: (0, i)
    c2 = lambda i: (0, 0)
    f32 = jnp.float32
    return pl.pallas_call(
        functools.partial(_mix_router_kernel, tm=tm),
        grid=(T // tm,),
        in_specs=[
            pl.BlockSpec((tm, D), row),
            pl.BlockSpec((tm, MIX_A), row),
            pl.BlockSpec((tm, CONV_CH), row),
            pl.BlockSpec((tm, MIX_C), row),
            pl.BlockSpec(w_out.shape, c2),
            pl.BlockSpec((1, D), c2),
            pl.BlockSpec((1, D), c2),
            pl.BlockSpec((E, D), c2),
            pl.BlockSpec((E, 1), c2),
            pl.BlockSpec((E, N_GROUPS), c2),
        ],
        out_specs=[
            pl.BlockSpec((tm, D), row),
            pl.BlockSpec((tm, D // _pack_factor()), row),
            pl.BlockSpec((E, tm), col),
            pl.BlockSpec((E, tm), col),
            pl.BlockSpec((E, tm), col),
            pl.BlockSpec((E, LANES), c2),
        ],
        out_shape=[
            jax.ShapeDtypeStruct((T, D), f32),
            jax.ShapeDtypeStruct((T, D // _pack_factor()), jnp.int32),
            jax.ShapeDtypeStruct((E, T), f32),
            jax.ShapeDtypeStruct((E, T), f32),
            jax.ShapeDtypeStruct((E, T), f32),
            jax.ShapeDtypeStruct((E, LANES), f32),
        ],
        scratch_shapes=[pltpu.VMEM((E, 1), f32)],
        compiler_params=_cparams(("arbitrary",)),
        name="mix_router",
    )(x2, ya, yb, yc, w_out, ln_g, ln_b, w_router_t, router_bias, expand)


def _compact_kernel(sel_ref, w_ref, pos_ref, pstart_ref, low_ref, dest_ref, wk_ref):
    sel = sel_ref[...]
    on = sel > 0.5
    rank = _dot(low_ref[...], sel.astype(MXU_DTYPE))
    row = pstart_ref[...] + pos_ref[...]
    w = w_ref[...]
    dests, ws = [], []
    for k in range(TOP_K):
        m = on & (rank == float(k))
        dests.append(jnp.sum(jnp.where(m, row, 0.0), axis=0, keepdims=True))
        ws.append(jnp.sum(jnp.where(m, w, 0.0), axis=0, keepdims=True))
    dest_ref[...] = jnp.concatenate(dests, axis=0).astype(jnp.int32)
    wk_ref[...] = jnp.concatenate(ws, axis=0)


def _compact(sel_t, w_t, pos_t, pad_start, tm):
    E, T = sel_t.shape
    lower = (jnp.arange(E)[None, :] < jnp.arange(E)[:, None]).astype(MXU_DTYPE)
    col = lambda i: (0, i)
    c2 = lambda i: (0, 0)
    return pl.pallas_call(
        _compact_kernel,
        grid=(T // tm,),
        in_specs=[pl.BlockSpec((E, tm), col), pl.BlockSpec((E, tm), col), pl.BlockSpec((E, tm), col),
                  pl.BlockSpec((E, 1), c2), pl.BlockSpec((E, E), c2)],
        out_specs=[pl.BlockSpec((TOP_K, tm), col), pl.BlockSpec((TOP_K, tm), col)],
        out_shape=[jax.ShapeDtypeStruct((TOP_K, T), jnp.int32), jax.ShapeDtypeStruct((TOP_K, T), jnp.float32)],
        compiler_params=_cparams(("arbitrary",)),
        name="route_compact",
    )(sel_t, w_t, pos_t, pad_start, lower)


def _silu(g):
    return g / (1.0 + jnp.exp(-g))


def _expert_kernel(be_ref, nv_ref, nu_ref, xs_ref, wg_ref, wu_ref, wd_ref, ys_ref, wgb_ref, wub_ref, wdb_ref):
    i = pl.program_id(0)

    @pl.when((i == 0) | (be_ref[i] != be_ref[jnp.maximum(i - 1, 0)]))
    def _():
        wgb_ref[...] = wg_ref[0].astype(MXU_DTYPE)
        wub_ref[...] = wu_ref[0].astype(MXU_DTYPE)
        wdb_ref[...] = wd_ref[0].astype(MXU_DTYPE)

    @pl.when(i < nu_ref[0])
    def _():
        live = lax.broadcasted_iota(jnp.int32, (ROW_BLOCK, 1), 0) < nv_ref[i]
        parts = [jnp.where(live, v, jnp.zeros_like(v)) for v in _unpack_rows(xs_ref[...])]
        dk = wgb_ref.shape[0] // len(parts)

        def proj(w_ref):
            acc = _dot(parts[0], w_ref[0:dk, :])
            for n in range(1, len(parts)):
                acc = acc + _dot(parts[n], w_ref[n * dk:(n + 1) * dk, :])
            return acc

        a = (_silu(proj(wgb_ref)) * proj(wub_ref)).astype(MXU_DTYPE)
        ys_ref[...] = _pack_rows(_dot(a, wdb_ref[...]))


def _experts(xs, block_e, block_valid, n_used, w_gate, w_up, w_down):
    n_rows, W = xs.shape
    D = w_gate.shape[1]
    n_blocks = n_rows // ROW_BLOCK
    blk = lambda i, be, nv, nu: (jnp.minimum(i, nu[0] - 1), 0)
    wsel = lambda i, be, nv, nu: (be[i], 0, 0)
    return pl.pallas_call(
        _expert_kernel,
        grid_spec=pltpu.PrefetchScalarGridSpec(
            num_scalar_prefetch=3,
            grid=(n_blocks,),
            in_specs=[
                pl.BlockSpec((ROW_BLOCK, W), blk),
                pl.BlockSpec((1, D, D_EXPERT), wsel),
                pl.BlockSpec((1, D, D_EXPERT), wsel),
                pl.BlockSpec((1, D_EXPERT, D), wsel),
            ],
            out_specs=pl.BlockSpec((ROW_BLOCK, W), blk),
            scratch_shapes=[pltpu.VMEM((D, D_EXPERT), MXU_DTYPE), pltpu.VMEM((D, D_EXPERT), MXU_DTYPE),
                            pltpu.VMEM((D_EXPERT, D), MXU_DTYPE)],
        ),
        out_shape=jax.ShapeDtypeStruct((n_rows, W), xs.dtype),
        compiler_params=_cparams(("arbitrary",)),
        name="experts",
    )(block_e, block_valid, n_used, xs, w_gate, w_up, w_down)


SC_CORES = 2
SC_SUBCORES = 16
SC_GATHER_ROWS = 64
COMBINE_CHUNKS = 8


def _sc_gather_rows(table, idx):
    n = idx.shape[0]
    w = table.shape[1]
    n_workers = SC_CORES * SC_SUBCORES
    per_worker = n // n_workers
    assert n % n_workers == 0 and per_worker % SC_GATHER_ROWS == 0
    mesh = plsc.VectorSubcoreMesh(core_axis_name="c", subcore_axis_name="s")

    @functools.partial(
        pl.kernel, mesh=mesh,
        out_type=jax.ShapeDtypeStruct((n, w), table.dtype),
        scratch_types=[
            pltpu.VMEM((2, SC_GATHER_ROWS), jnp.int32),
            pltpu.VMEM((2, SC_GATHER_ROWS, w), table.dtype),
            pltpu.SemaphoreType.DMA((2,)),
        ],
        name="sc_gather_rows",
    )
    def gather(table_hbm, idx_hbm, out_hbm, idx_v, rows_v, sem):
        wid = lax.axis_index("s") * SC_CORES + lax.axis_index("c")
        base = wid * per_worker
        n_steps = per_worker // SC_GATHER_ROWS

        def gather_copy(slot):
            return pltpu.make_async_copy(table_hbm.at[idx_v.at[slot]], rows_v.at[slot], sem.at[slot])

        def start(step, slot):
            pltpu.sync_copy(idx_hbm.at[pl.ds(base + step * SC_GATHER_ROWS, SC_GATHER_ROWS)], idx_v.at[slot])
            gather_copy(slot).start()

        start(0, 0)

        @pl.loop(0, n_steps, step=2)
        def _(g):
            for slot in range(2):
                step = g + slot

                @pl.when(step + 1 < n_steps)
                def _():
                    start(step + 1, 1 - slot)

                gather_copy(slot).wait()
                pltpu.sync_copy(rows_v.at[slot], out_hbm.at[pl.ds(base + step * SC_GATHER_ROWS, SC_GATHER_ROWS)])

    return gather(table, idx)


SC_SCATTER_ROWS = 64


def _sc_scatter_rows(rows, idx3, n_out):
    n_src, w = rows.shape
    n_chunks, n_dst, batch = idx3.shape
    n_workers = SC_CORES * SC_SUBCORES
    assert batch == SC_SCATTER_ROWS and n_chunks * batch == n_src and n_chunks % (2 * n_workers) == 0
    per_worker = n_chunks // n_workers
    mesh = plsc.VectorSubcoreMesh(core_axis_name="c", subcore_axis_name="s")

    @functools.partial(
        pl.kernel, mesh=mesh,
        out_type=jax.ShapeDtypeStruct((n_out, w), rows.dtype),
        scratch_types=[
            pltpu.VMEM((2, n_dst, batch), jnp.int32),
            pltpu.VMEM((2, batch, w), rows.dtype),
            pltpu.SemaphoreType.DMA((2,)),
            pltpu.SemaphoreType.DMA,
        ],
        name="sc_scatter_rows",
    )
    def scatter(rows_hbm, idx_hbm, out_hbm, idx_v, rows_v, load_sem, store_sem):
        wid = lax.axis_index("s") * SC_CORES + lax.axis_index("c")

        def load_copy(step, slot):
            c = wid * per_worker + step
            return pltpu.make_async_copy(rows_hbm.at[pl.ds(c * batch, batch)], rows_v.at[slot], load_sem.at[slot])

        def load(step, slot):
            pltpu.sync_copy(idx_hbm.at[wid * per_worker + step], idx_v.at[slot])
            load_copy(step, slot).start()

        def store_copy(slot, k):
            return pltpu.make_async_copy(rows_v.at[slot], out_hbm.at[idx_v.at[slot].at[k]], store_sem)

        load(0, 0)

        @pl.loop(0, per_worker, step=2)
        def _(g):
            for slot in range(2):
                step = g + slot

                @pl.when(step + 1 < per_worker)
                def _():
                    load(step + 1, 1 - slot)

                load_copy(step, slot).wait()
                for k in range(n_dst):
                    store_copy(slot, k).start()
                for k in range(n_dst):
                    store_copy(slot, k).wait()

    return scatter(rows, idx3)


def _combine2_kernel(wk_ref, x1_ref, g_ref_rows, wsg_ref, wsu_ref, wsd_ref, g_ref, b_ref, o_ref):
    x1 = x1_ref[...]
    xb = x1.astype(MXU_DTYPE)
    a = (_silu(_dot(xb, wsg_ref[...])) * _dot(xb, wsu_ref[...])).astype(MXU_DTYPE)
    shared = _dot(a, wsd_ref[...])
    wk = wk_ref[...].T
    groups = [wk[:, 0:1] * v for v in _unpack_rows_f32(g_ref_rows[0])]
    for k in range(1, TOP_K):
        groups = [g + wk[:, k:k + 1] * v for g, v in zip(groups, _unpack_rows_f32(g_ref_rows[k]))]
    routed = jnp.concatenate(groups, axis=1)
    o_ref[...] = _layer_norm(ALPHA * x1 + (routed + shared), g_ref[...], b_ref[...])


def _combine2_kernel_into(wk_ref, x1_ref, g_ref_rows, wsg_ref, wsu_ref, wsd_ref, g_ref, b_ref, prev_ref, o_ref):
    del prev_ref
    _combine2_kernel(wk_ref, x1_ref, g_ref_rows, wsg_ref, wsu_ref, wsd_ref, g_ref, b_ref, o_ref)


def _combine2(wk_t, x1, gathered, w_sg, w_su, w_sd, ln_g, ln_b, tc, chunk, prev):
    T, D = x1.shape
    _, t_chunk, W = gathered.shape
    base = chunk * (t_chunk // tc)
    row = lambda i: (base + i, 0)
    c2 = lambda i: (0, 0)
    in_specs = [
        pl.BlockSpec((TOP_K, tc), lambda i: (0, base + i)),
        pl.BlockSpec((tc, D), row),
        pl.BlockSpec((TOP_K, tc, W), lambda i: (0, i, 0)),
        pl.BlockSpec(w_sg.shape, c2),
        pl.BlockSpec(w_su.shape, c2),
        pl.BlockSpec(w_sd.shape, c2),
        pl.BlockSpec((1, D), c2),
        pl.BlockSpec((1, D), c2),
    ]
    args = [wk_t, x1, gathered, w_sg, w_su, w_sd, ln_g, ln_b]
    if prev is None:
        body, aliases = _combine2_kernel, {}
    else:
        body, aliases = _combine2_kernel_into, {len(args): 0}
        in_specs.append(pl.BlockSpec(memory_space=pl.ANY))
        args.append(prev)
    return pl.pallas_call(
        body,
        grid=(t_chunk // tc,),
        in_specs=in_specs,
        out_specs=pl.BlockSpec((tc, D), row),
        out_shape=jax.ShapeDtypeStruct((T, D), jnp.float32),
        input_output_aliases=aliases,
        compiler_params=_cparams(("arbitrary",)),
        name="combine",
    )(*args)


def _split_w_in(w_in):
    o_kv = Q_RANK
    o_ki = o_kv + KV_RANK
    o_iw = o_ki + IDX_DIM
    o_rest = o_iw + N_IDX_HEADS
    w_small = jnp.pad(w_in[:, o_ki:o_rest], ((0, 0), (0, LANES - IDX_DIM - N_IDX_HEADS)))
    return jnp.concatenate([w_in[:, :o_ki], w_in[:, o_rest:], w_small], axis=1).astype(MXU_DTYPE)


def _stages(x, mem, w_in, q_norm_g, kv_norm_g, w_uq, w_uk, w_uv, w_qidx, rel_bias, conv_w, w_mem_k, w_mem_v, w_out, ln1_g, ln1_b, w_router, router_bias, w_e_gate, w_e_up, w_e_down, w_s_gate, w_s_up, w_s_down, ln2_g, ln2_b):
    B, S, D = x.shape
    T = B * S
    bf = MXU_DTYPE
    assert w_in.shape[0] == DEPTH == 1, "single-layer stack"
    l = 0
    res = {}
    x2 = x.reshape(T, D)
    cq, ckv, ckvt, kidx, iwt, yb, yc = _proj(
        x2, mem, _split_w_in(w_in[l]), q_norm_g[l].reshape(1, -1), kv_norm_g[l].reshape(1, -1), conv_w[l],
        w_mem_k[l].astype(bf), w_mem_v[l].astype(bf), B, S, tm=min(1024, S))
    res.update(c_q=cq, c_kv=ckv, k_idx=kidx, y_b=yb, y_c=yc,
               idx_w=jnp.swapaxes(iwt, 1, 2) / (N_IDX_HEADS ** -0.5 * IDX_DIM ** -0.5))
    bias_t = _bias_tiles(rel_bias)
    ya = _dsa(cq, iwt, kidx, ckv, ckvt,
              w_qidx[l].reshape(Q_RANK, -1).astype(bf), w_uq[l].reshape(Q_RANK, -1).astype(bf),
              jnp.transpose(w_uk[l], (1, 0, 2)).astype(bf), jnp.transpose(w_uv[l], (1, 2, 0)).astype(bf),
              bias_t, B, S)
    res.update(y_a=ya)

    x1, x1p, sel_t, w_t, pos_t, cnt = _mix_router(
        x2, ya, yb, yc, w_out[l].astype(bf), ln1_g[l].reshape(1, -1), ln1_b[l].reshape(1, -1),
        w_router[l].T, router_bias[l].reshape(-1, 1), tm=min(512, T))
    res.update(x1=x1)

    counts = cnt[:, 0].astype(jnp.int32)
    padded = (counts + ROW_BLOCK - 1) // ROW_BLOCK * ROW_BLOCK
    pad_end = jnp.cumsum(padded)
    pad_start = pad_end - padded
    n_blocks = -(-(T * TOP_K) // ROW_BLOCK) + N_EXPERTS
    n_rows = n_blocks * ROW_BLOCK
    block_start = jnp.arange(n_blocks, dtype=jnp.int32) * ROW_BLOCK
    block_e = jnp.minimum(jnp.sum((pad_end[None, :] <= block_start[:, None]).astype(jnp.int32), axis=1),
                          N_EXPERTS - 1)
    n_used = (pad_end[-1:] // ROW_BLOCK).astype(jnp.int32)

    dest_t, wk_t = _compact(sel_t, w_t, pos_t, pad_start.astype(jnp.float32).reshape(-1, 1), tm=min(512, T))
    block_valid = jnp.clip((pad_start + counts)[block_e] - block_start, 0, ROW_BLOCK).astype(jnp.int32)
    bt = SC_SCATTER_ROWS
    idx3 = jnp.transpose(dest_t.reshape(TOP_K, T // bt, bt), (1, 0, 2))
    xs = _sc_scatter_rows(x1p, idx3, n_rows)
    ys = _experts(xs, block_e, block_valid, n_used, w_e_gate[l], w_e_up[l], w_e_down[l])
    n_chunks = COMBINE_CHUNKS if T % (COMBINE_CHUNKS * 512) == 0 else 1
    t_chunk = T // n_chunks
    out = None
    for c in range(n_chunks):
        idx_c = dest_t[:, c * t_chunk:(c + 1) * t_chunk].reshape(-1)
        gathered = _sc_gather_rows(ys, idx_c).reshape(TOP_K, t_chunk, -1)
        out = _combine2(wk_t, x1, gathered, w_s_gate[l].astype(bf), w_s_up[l].astype(bf), w_s_down[l].astype(bf),
                        ln2_g[l].reshape(1, -1), ln2_b[l].reshape(1, -1), tc=min(512, t_chunk), chunk=c, prev=out)
    res.update(out=out.reshape(B, S, D))
    return res


def kernel(x, mem, w_in, q_norm_g, kv_norm_g, w_uq, w_uk, w_uv, w_qidx, rel_bias, conv_w, w_mem_k, w_mem_v, w_out, ln1_g, ln1_b, w_router, router_bias, w_e_gate, w_e_up, w_e_down, w_s_gate, w_s_up, w_s_down, ln2_g, ln2_b):
    return _stages(x, mem, w_in, q_norm_g, kv_norm_g, w_uq, w_uk, w_uv, w_qidx, rel_bias, conv_w, w_mem_k, w_mem_v, w_out, ln1_g, ln1_b, w_router, router_bias, w_e_gate, w_e_up, w_e_down, w_s_gate, w_s_up, w_s_down, ln2_g, ln2_b)["out"]
```

```python
import functools
import math

import jax
import jax.numpy as jnp
from jax import lax
from jax.experimental import pallas as pl
from jax.experimental.pallas import tpu as pltpu
from jax.experimental.pallas import tpu_sc as plsc

N_HEADS_A = 8
HEAD_DIM = 64
Q_RANK = 256
KV_RANK = 128
N_IDX_HEADS = 8
IDX_DIM = 64
TOPK_MAX = 256
REL_BUCKETS = 32
REL_MAX_DIST = 128
CONV_CH = 256
CONV_WIDTH = 3
N_MEM_HEADS = 4
MIX_A = N_HEADS_A * HEAD_DIM
MIX_C = N_MEM_HEADS * HEAD_DIM
N_EXPERTS = 64
N_GROUPS = 8
GROUP_SIZE = N_EXPERTS // N_GROUPS
TOPK_GROUPS = 4
TOP_K = 8
D_EXPERT = 256
ROUTED_SCALE = 2.5
DEPTH = 1
ALPHA = (2.0 * DEPTH) ** 0.25
LN_EPS = 1e-5
RMS_EPS = 1e-6
LOG2_E = math.log2(math.e)

LANES = 128
SUBLANES = 8
QB = 128
F32_LOWEST = -3.4028234663852886e38
VMEM_LIMIT = 56 * 1024 * 1024
MXU_DTYPE = jnp.bfloat16
ROW_BLOCK = 1024

_NT = (((1,), (1,)), ((), ()))


def _dot(a, b):
    return jnp.dot(a, b, preferred_element_type=jnp.float32)


def _dot_nt(a, b):
    return lax.dot_general(a, b, _NT, preferred_element_type=jnp.float32)


def _cparams(sem):
    return pltpu.CompilerParams(dimension_semantics=sem, vmem_limit_bytes=VMEM_LIMIT)


def _bias_kernel(rb_ref, o_ref):
    s = lax.broadcasted_iota(jnp.int32, (QB, QB), 0)
    t = lax.broadcasted_iota(jnp.int32, (QB, QB), 1)
    max_exact = REL_BUCKETS // 2
    for tile in range(3):
        n = jnp.maximum(t - s + (2 - tile) * QB, 0)
        nf = jnp.maximum(n.astype(jnp.float32), 1.0)
        large = max_exact + (jnp.log(nf / max_exact) / math.log(REL_MAX_DIST / max_exact)
                             * (REL_BUCKETS - max_exact)).astype(jnp.int32)
        large = jnp.minimum(large, REL_BUCKETS - 1)
        bucket = jnp.where(n < max_exact, n, large)
        for h in range(N_HEADS_A):
            acc = jnp.zeros((QB, QB), jnp.float32)
            for b in range(REL_BUCKETS):
                acc = jnp.where(bucket == b, rb_ref[b, h], acc)
            o_ref[tile, h] = acc * LOG2_E


def _bias_tiles(rel_bias):
    return pl.pallas_call(
        _bias_kernel,
        in_specs=[pl.BlockSpec(memory_space=pltpu.SMEM)],
        out_specs=pl.BlockSpec(memory_space=pltpu.VMEM),
        out_shape=jax.ShapeDtypeStruct((3, N_HEADS_A, QB, QB), jnp.float32),
        name="bias_tiles",
    )(rel_bias)


def _proj_kernel(x_ref, mem_ref, wm_ref, qg_ref, kvg_ref, cw_ref, wmk_ref, wmv_ref,
                 cq_ref, ckv_ref, ckvt_ref, kidx_ref, iwt_ref, yb_ref, yc_ref,
                 carry_ref, mk_ref, mv_ref, *, tm):
    si = pl.program_id(1)

    @pl.when(si == 0)
    def _():
        carry_ref[...] = jnp.zeros_like(carry_ref)
        mb = mem_ref[0].astype(MXU_DTYPE)
        mk_ref[...] = _dot(mb, wmk_ref[...]).astype(MXU_DTYPE)
        mv_ref[...] = _dot(mb, wmv_ref[...]).astype(MXU_DTYPE)

    xb = x_ref[...].astype(MXU_DTYPE)
    p = _dot(xb, wm_ref[...])
    small = p[:, p.shape[1] - LANES:]

    o = 0
    cq = p[:, o:o + Q_RANK]; o += Q_RANK
    ckv = p[:, o:o + KV_RANK]; o += KV_RANK
    g_b = p[:, o:o + CONV_CH]; o += CONV_CH
    g_c = p[:, o:o + CONV_CH]; o += CONV_CH
    h_c = p[:, o:o + CONV_CH]; o += CONV_CH
    q_mem = p[:, o:o + MIX_C]

    cq = cq * lax.rsqrt(jnp.mean(cq * cq, axis=-1, keepdims=True) + RMS_EPS) * qg_ref[...]
    ckv = ckv * lax.rsqrt(jnp.mean(ckv * ckv, axis=-1, keepdims=True) + RMS_EPS) * kvg_ref[...]
    cq_ref[...] = cq.astype(MXU_DTYPE)
    ckv_b = ckv.astype(MXU_DTYPE)
    ckv_ref[...] = ckv_b
    ckvt_ref[0] = ckv.T.astype(MXU_DTYPE)

    kidx_ref[...] = small[:, :IDX_DIM].astype(MXU_DTYPE)
    small_t = small.T
    iwt_ref[0] = small_t[IDX_DIM:IDX_DIM + N_IDX_HEADS, :] * (N_IDX_HEADS ** -0.5 * IDX_DIM ** -0.5)

    u = g_c * h_c
    rows = lax.broadcasted_iota(jnp.int32, (tm, 1), 0)
    c6 = carry_ref[SUBLANES - 2:SUBLANES - 1, :]
    c7 = carry_ref[SUBLANES - 1:SUBLANES, :]
    u1 = jnp.where(rows == 0, c7, pltpu.roll(u, 1, 0))
    u2 = jnp.where(rows == 0, c6, jnp.where(rows == 1, c7, pltpu.roll(u, 2, 0)))
    y = cw_ref[0:1, :] * u2
    y = y + cw_ref[1:2, :] * u1
    y = y + cw_ref[2:3, :] * u
    yb_ref[...] = (g_b * y).astype(MXU_DTYPE)
    carry_ref[...] = u[tm - SUBLANES:, :]

    qm = q_mem.astype(MXU_DTYPE)
    outs = []
    for h in range(N_MEM_HEADS):
        sl = slice(h * HEAD_DIM, (h + 1) * HEAD_DIM)
        lg = _dot_nt(qm[:, sl], mk_ref[:, sl]) * (HEAD_DIM ** -0.5)
        lg = lg - jnp.max(lg, axis=-1, keepdims=True)
        e = jnp.exp(lg)
        pr = e / jnp.sum(e, axis=-1, keepdims=True)
        outs.append(_dot(pr.astype(MXU_DTYPE), mv_ref[:, sl]))
    yc_ref[...] = jnp.concatenate(outs, axis=-1).astype(MXU_DTYPE)


def _proj(x2, mem, w_main, q_g, kv_g, conv_w, w_mk, w_mv, B, S, tm):
    T, D = x2.shape
    n_mem = mem.shape[1]
    ns = S // tm
    row = lambda b, s: (b * ns + s, 0)
    const2 = lambda b, s: (0, 0)
    bf = MXU_DTYPE
    return pl.pallas_call(
        functools.partial(_proj_kernel, tm=tm),
        grid=(B, ns),
        in_specs=[
            pl.BlockSpec((tm, D), row),
            pl.BlockSpec((1, n_mem, D), lambda b, s: (b, 0, 0)),
            pl.BlockSpec(w_main.shape, const2),
            pl.BlockSpec(q_g.shape, const2),
            pl.BlockSpec(kv_g.shape, const2),
            pl.BlockSpec(conv_w.shape, const2),
            pl.BlockSpec(w_mk.shape, const2),
            pl.BlockSpec(w_mv.shape, const2),
        ],
        out_specs=[
            pl.BlockSpec((tm, Q_RANK), row),
            pl.BlockSpec((tm, KV_RANK), row),
            pl.BlockSpec((1, KV_RANK, tm), lambda b, s: (b, 0, s)),
            pl.BlockSpec((tm, IDX_DIM), row),
            pl.BlockSpec((1, N_IDX_HEADS, tm), lambda b, s: (b, 0, s)),
            pl.BlockSpec((tm, CONV_CH), row),
            pl.BlockSpec((tm, MIX_C), row),
        ],
        out_shape=[
            jax.ShapeDtypeStruct((T, Q_RANK), bf),
            jax.ShapeDtypeStruct((T, KV_RANK), bf),
            jax.ShapeDtypeStruct((B, KV_RANK, S), bf),
            jax.ShapeDtypeStruct((T, IDX_DIM), bf),
            jax.ShapeDtypeStruct((B, N_IDX_HEADS, S), jnp.float32),
            jax.ShapeDtypeStruct((T, CONV_CH), bf),
            jax.ShapeDtypeStruct((T, MIX_C), bf),
        ],
        scratch_shapes=[
            pltpu.VMEM((SUBLANES, CONV_CH), jnp.float32),
            pltpu.VMEM((n_mem, MIX_C), bf),
            pltpu.VMEM((n_mem, MIX_C), bf),
        ],
        compiler_params=_cparams(("arbitrary", "arbitrary")),
        name="proj",
    )(x2, mem, w_main, q_g, kv_g, conv_w, w_mk, w_mv)


def _key_to_f32(key):
    bits = jnp.where(key < 0, key ^ jnp.int32(0x7FFFFFFF), key)
    return pltpu.bitcast(bits, jnp.float32)


def _colsum8(v):
    return jnp.sum(v.reshape(QB // SUBLANES, SUBLANES, QB), axis=0)


def _colmax8(v):
    return jnp.max(v.reshape(QB // SUBLANES, SUBLANES, QB), axis=0)


UNROLL_WIDTHS = (8, 4, 2, 1)


def _dsa_kernel(cq_ref, iwt_ref, kidx_ref, ckv_ref, ckvt_ref, wqi_ref, wuq_ref, wuk_ref, wuvt_ref,
                bias_ref, o_ref, wfold_ref, qidx_ref, qlat_ref, score_ref, logit_ref, acc_ref,
                *, k_sel, idx_bits):
    i = pl.program_id(1)
    f32 = jnp.float32
    bf = MXU_DTYPE
    n_blocks = i + 1
    n_blocks = n_blocks + jnp.where((n_blocks % 4 == 3) & (n_blocks < pl.num_programs(1)), 1, 0)
    s_loc = lax.broadcasted_iota(jnp.int32, (QB, QB), 0)
    t_glob = i * QB + lax.broadcasted_iota(jnp.int32, (QB, QB), 1)

    def blk(jb):
        return pl.multiple_of(jb * QB, QB)

    def block_loop(fn, init):
        c, start = init, 0
        for width in UNROLL_WIDTHS:
            n = (n_blocks - start) // width
            c = lax.fori_loop(0, n, lambda it, c, w=width, s=start: fn(s + it * w, w, c), c)
            start = start + n * width
        return c

    @pl.when(i == 0)
    def _():
        for h in range(N_HEADS_A):
            wfold_ref[:, h * KV_RANK:(h + 1) * KV_RANK] = (
                _dot_nt(wuq_ref[:, h * HEAD_DIM:(h + 1) * HEAD_DIM], wuk_ref[h])
                * (HEAD_DIM ** -0.5 * LOG2_E)).astype(bf)

    cq = cq_ref[...]
    q_idx = _dot(cq, wqi_ref[...]).astype(bf)
    q_lat = _dot(cq, wfold_ref[...]).astype(bf)
    for h in range(N_HEADS_A):
        qidx_ref[h * QB:(h + 1) * QB, :] = q_idx[:, h * IDX_DIM:(h + 1) * IDX_DIM]
        qlat_ref[h * QB:(h + 1) * QB, :] = q_lat[:, h * KV_RANK:(h + 1) * KV_RANK]
    iw = iwt_ref[0]

    def score_body(jb0, nb, n_pos8):
        d_blk = _dot_nt(kidx_ref[pl.ds(blk(jb0), nb * QB), :], qidx_ref[...])
        for sb in range(nb):
            off = blk(jb0 + sb)
            d_all = d_blk[sb * QB:(sb + 1) * QB, :]
            acc = jnp.maximum(d_all[:, 0:QB], 0.0) * iw[0:1, :]
            for h in range(1, N_IDX_HEADS):
                acc = acc + jnp.maximum(d_all[:, h * QB:(h + 1) * QB], 0.0) * iw[h:h + 1, :]
            sc = jnp.where(s_loc + off <= t_glob, acc + 0.0, F32_LOWEST)
            score_ref[pl.ds(off, QB), :] = sc
            n_pos8 = n_pos8 + _colsum8(jnp.where(sc >= 0.0, 1.0, 0.0))
        return n_pos8

    n_pos8 = block_loop(score_body, jnp.zeros((SUBLANES, QB), f32))

    def count_where(pred):
        def body(jb0, nb, acc):
            for sb in range(nb):
                off = blk(jb0 + sb)
                acc = acc + _colsum8(jnp.where(pred(score_ref[pl.ds(off, QB), :], off), 1.0, 0.0))
            return acc
        acc = block_loop(body, jnp.zeros((SUBLANES, QB), f32))
        return jnp.sum(acc, axis=0, keepdims=True)

    kf = float(k_sel)

    def search():
        c0 = jnp.sum(n_pos8, axis=0, keepdims=True)
        cand0 = jnp.where(c0 >= kf, jnp.int32(0), jnp.int32(-2 ** 31))
        n_ge0 = jnp.where(c0 >= kf, c0, -1.0)

        def bit_body(it, carry):
            cand, n_ge = carry
            trial = cand + lax.shift_left(jnp.int32(1), 30 - it)
            tf = _key_to_f32(trial)
            cnt = count_where(lambda sc, off: sc >= tf)
            take = cnt >= kf
            return jnp.where(take, trial, cand), jnp.where(take, cnt, n_ge)

        cand, n_ge = lax.fori_loop(0, 31, bit_body, (cand0, n_ge0))
        thr = _key_to_f32(cand)
        keep_all_ties = jnp.full((1, QB), 2 ** idx_bits - 1, jnp.int32)

        def resolve_ties():
            n_gt = count_where(lambda sc, off: sc > thr)
            n_eq = count_where(lambda sc, off: sc == thr)
            need = kf - n_gt

            def tie_search():
                def tbody(it, xcut):
                    trial = xcut + lax.shift_left(jnp.int32(1), idx_bits - 1 - it)
                    cnt = count_where(lambda sc, off: (sc == thr) & (s_loc + off < trial))
                    return jnp.where(cnt < need, trial, xcut)
                return lax.fori_loop(0, idx_bits, tbody, jnp.zeros((1, QB), jnp.int32))

            return lax.cond(jnp.max(n_eq - need) > 0.0, tie_search, lambda: keep_all_ties)

        xcut = lax.cond(jnp.max(jnp.abs(n_ge - kf)) > 0.0, resolve_ties, lambda: keep_all_ties)
        return thr, xcut

    def no_search():
        return jnp.full((1, QB), F32_LOWEST, f32), jnp.full((1, QB), 2 ** idx_bits - 1, jnp.int32)

    thr, xcut = lax.cond((i + 1) * QB > k_sel, search, no_search)

    def selection_mask(off):
        sc = score_ref[pl.ds(off, QB), :]
        s_glob = s_loc + off
        keep = ((sc > thr) | ((sc == thr) & (s_glob <= xcut))) & (s_glob <= t_glob)
        return jnp.where(keep, 0.0, -jnp.inf)

    acc_ref[...] = jnp.zeros_like(acc_ref)

    def att_body(jb0, nb, carry):
        m, l8 = list(carry[0]), list(carry[1])
        rows = nb * QB
        lg_blk = _dot_nt(ckv_ref[pl.ds(blk(jb0), rows), :], qlat_ref[...])
        blk_max = [None] * N_HEADS_A
        for sb in range(nb):
            off = blk(jb0 + sb)
            msk = selection_mask(off)
            bsel = jnp.clip(jb0 + sb - i + 2, 0, 2)
            for h in range(N_HEADS_A):
                lgh = lg_blk[sb * QB:(sb + 1) * QB, h * QB:(h + 1) * QB] + bias_ref[bsel, h] + msk
                logit_ref[sb * QB:(sb + 1) * QB, h * QB:(h + 1) * QB] = lgh
                cm = _colmax8(lgh)
                blk_max[h] = cm if blk_max[h] is None else jnp.maximum(blk_max[h], cm)
        ps, scales = [], []
        for h in range(N_HEADS_A):
            m_new = jnp.maximum(m[h], jnp.max(blk_max[h], axis=0, keepdims=True))
            m_ref = jnp.where(m_new == -jnp.inf, 0.0, m_new)
            p = jnp.exp2(logit_ref[0:rows, h * QB:(h + 1) * QB] - m_ref)
            scale = jnp.exp2(m[h] - m_ref)
            l8[h] = l8[h] * scale + jnp.sum(p.reshape(rows // SUBLANES, SUBLANES, QB), axis=0)
            m[h] = m_new
            ps.append(p.astype(bf))
            scales.append(scale)
        pv = _dot(ckvt_ref[0, :, pl.ds(blk(jb0), rows)], jnp.concatenate(ps, axis=1))
        for h in range(N_HEADS_A):
            hs = slice(h * QB, (h + 1) * QB)
            acc_ref[:, hs] = acc_ref[:, hs] * scales[h] + pv[:, hs]
        return tuple(m), tuple(l8)

    _, l8 = block_loop(att_body, (tuple(jnp.full((1, QB), -jnp.inf, f32) for _ in range(N_HEADS_A)),
                                  tuple(jnp.zeros((SUBLANES, QB), f32) for _ in range(N_HEADS_A))))

    outs = []
    for h in range(N_HEADS_A):
        l_row = jnp.sum(l8[h], axis=0, keepdims=True)
        o_lat_t = (acc_ref[:, h * QB:(h + 1) * QB] / l_row).astype(bf)
        outs.append(_dot(wuvt_ref[h], o_lat_t))
    o_ref[...] = jnp.concatenate(outs, axis=0).T.astype(o_ref.dtype)


def _dsa(cq, iwt, kidx, ckv, ckvt, w_qidx, w_uq, w_uk_h, w_uvt_h, bias_tiles, B, S):
    T = cq.shape[0]
    assert S % QB == 0 and QB >= REL_MAX_DIST
    nq = S // QB
    k_sel = min(TOPK_MAX, S // 4)
    idx_bits = max(1, (S - 1).bit_length())
    c2 = lambda b, i: (0, 0)
    c3 = lambda b, i: (0, 0, 0)
    return pl.pallas_call(
        functools.partial(_dsa_kernel, k_sel=k_sel, idx_bits=idx_bits),
        grid=(B, nq),
        in_specs=[
            pl.BlockSpec((QB, Q_RANK), lambda b, i: (b * nq + i, 0)),
            pl.BlockSpec((1, N_IDX_HEADS, QB), lambda b, i: (b, 0, i)),
            pl.BlockSpec((S, IDX_DIM), lambda b, i: (b, 0)),
            pl.BlockSpec((S, KV_RANK), lambda b, i: (b, 0)),
            pl.BlockSpec((1, KV_RANK, S), lambda b, i: (b, 0, 0)),
            pl.BlockSpec(w_qidx.shape, c2),
            pl.BlockSpec(w_uq.shape, c2),
            pl.BlockSpec(w_uk_h.shape, c3),
            pl.BlockSpec(w_uvt_h.shape, c3),
            pl.BlockSpec(bias_tiles.shape, lambda b, i: (0, 0, 0, 0)),
        ],
        out_specs=pl.BlockSpec((QB, MIX_A), lambda b, i: (b * nq + i, 0)),
        out_shape=jax.ShapeDtypeStruct((T, MIX_A), MXU_DTYPE),
        scratch_shapes=[
            pltpu.VMEM((Q_RANK, N_HEADS_A * KV_RANK), MXU_DTYPE),
            pltpu.VMEM((N_IDX_HEADS * QB, IDX_DIM), MXU_DTYPE),
            pltpu.VMEM((N_HEADS_A * QB, KV_RANK), MXU_DTYPE),
            pltpu.VMEM((S, QB), jnp.float32),
            pltpu.VMEM((max(UNROLL_WIDTHS) * QB, N_HEADS_A * QB), jnp.float32),
            pltpu.VMEM((KV_RANK, N_HEADS_A * QB), jnp.float32),
        ],
        compiler_params=_cparams(("arbitrary", "arbitrary")),
        name="dsa",
    )(cq, iwt, kidx, ckv, ckvt, w_qidx, w_uq, w_uk_h, w_uvt_h, bias_tiles)


def _layer_norm(xf, g, b):
    mu = jnp.mean(xf, axis=-1, keepdims=True)
    xc = xf - mu
    var = jnp.mean(xc * xc, axis=-1, keepdims=True)
    return xc * lax.rsqrt(var + LN_EPS) * g + b


def _rank_rows(v, n):
    ri = lax.broadcasted_iota(jnp.int32, v.shape, 0)
    rank = jnp.zeros(v.shape, jnp.float32)
    for r2 in range(n):
        row = v[r2:r2 + 1, :]
        beats = (row > v) | ((row == v) & (ri > r2))
        rank = rank + jnp.where(beats, 1.0, 0.0)
    return rank


def _top_rows(v, k):
    n = v.shape[0]
    ri = lax.broadcasted_iota(jnp.int32, v.shape, 0)
    sel = jnp.zeros(v.shape, jnp.float32)
    for _ in range(k):
        m = jnp.max(v, axis=0, keepdims=True)
        first = jnp.min(jnp.where(v == m, ri, n), axis=0, keepdims=True)
        pick = ri == first
        sel = jnp.where(pick, 1.0, sel)
        v = jnp.where(pick, -jnp.inf, v)
    return sel > 0.5


def _pack_factor():
    return 4 // jnp.dtype(MXU_DTYPE).itemsize


def _pack_rows(x):
    if _pack_factor() == 1:
        return pltpu.bitcast(x, jnp.int32)
    half = x.shape[1] // 2
    b = pltpu.bitcast(x.astype(MXU_DTYPE).astype(jnp.float32), jnp.int32)
    return b[:, half:] | lax.shift_right_logical(b[:, :half], jnp.int32(16))


_HIGH_HALF = -(1 << 16)


def _unpack_rows_f32(p):
    if _pack_factor() == 1:
        return [pltpu.bitcast(p, jnp.float32)]
    lo = pltpu.bitcast(lax.shift_left(p, jnp.int32(16)), jnp.float32)
    hi = pltpu.bitcast(p & jnp.int32(_HIGH_HALF), jnp.float32)
    return [lo, hi]


def _unpack_rows(p):
    return [v.astype(MXU_DTYPE) for v in _unpack_rows_f32(p)]


def _mix_router_kernel(x_ref, ya_ref, yb_ref, yc_ref, wo_ref, g_ref, b_ref, wrt_ref, rb_ref, exp_ref,
                       x1_ref, x1p_ref, sel_ref, w_ref, pos_ref, cnt_ref, base_ref, *, tm):
    step = pl.program_id(0)
    f32 = jnp.float32

    @pl.when(step == 0)
    def _():
        base_ref[...] = jnp.zeros_like(base_ref)

    mix = _dot(ya_ref[...], wo_ref[0:MIX_A, :])
    mix = mix + _dot(yb_ref[...], wo_ref[MIX_A:MIX_A + CONV_CH, :])
    mix = mix + _dot(yc_ref[...], wo_ref[MIX_A + CONV_CH:, :])
    x1 = _layer_norm(ALPHA * x_ref[...] + mix, g_ref[...], b_ref[...])
    x1_ref[...] = x1
    x1p_ref[...] = _pack_rows(x1)

    lg = lax.dot_general(wrt_ref[...], x1, _NT, precision=lax.Precision.HIGHEST, preferred_element_type=f32)
    s = 1.0 / (1.0 + jnp.exp(-lg))
    sc = s + rb_ref[...]

    g3 = sc.reshape(N_GROUPS, GROUP_SIZE, tm)
    m1 = jnp.max(g3, axis=1, keepdims=True)
    is_m1 = g3 == m1
    n_m1 = jnp.sum(jnp.where(is_m1, 1.0, 0.0), axis=1, keepdims=True)
    m2 = jnp.max(jnp.where(is_m1, -jnp.inf, g3), axis=1, keepdims=True)
    gscore = (m1 + jnp.where(n_m1 > 1.0, m1, m2)).reshape(N_GROUPS, tm)
    gsel = jnp.where(_rank_rows(gscore, N_GROUPS) < float(TOPK_GROUPS), 1.0, 0.0)
    emask = _dot(exp_ref[...], gsel.astype(MXU_DTYPE)) > 0.5
    masked = jnp.where(emask, sc, -jnp.inf)
    sel = _top_rows(masked, TOP_K) & emask
    self_ = jnp.where(sel, 1.0, 0.0)
    top_s = jnp.where(sel, s, 0.0)
    w = top_s / jnp.sum(top_s, axis=0, keepdims=True) * ROUTED_SCALE

    t_r = lax.broadcasted_iota(jnp.int32, (tm, tm), 0)
    t_c = lax.broadcasted_iota(jnp.int32, (tm, tm), 1)
    upper = jnp.where(t_r < t_c, 1.0, 0.0).astype(MXU_DTYPE)
    pref = _dot(self_.astype(MXU_DTYPE), upper)
    base = base_ref[...]
    sel_ref[...] = self_
    w_ref[...] = w
    pos_ref[...] = base + pref
    base = base + jnp.sum(self_, axis=1, keepdims=True)
    base_ref[...] = base
    cnt_ref[...] = jnp.broadcast_to(base, cnt_ref.shape)


def _mix_router(x2, ya, yb, yc, w_out, ln_g, ln_b, w_router_t, router_bias, tm):
    T, D = x2.shape
    E = N_EXPERTS
    expand = (jnp.arange(E)[:, None] // GROUP_SIZE == jnp.arange(N_GROUPS)[None, :]).astype(MXU_DTYPE)
    row = lambda i: (i, 0)
    col = lambda i: (0, i)
    c2 = lambda i: (0, 0)
    f32 = jnp.float32
    return pl.pallas_call(
        functools.partial(_mix_router_kernel, tm=tm),
        grid=(T // tm,),
        in_specs=[
            pl.BlockSpec((tm, D), row),
            pl.BlockSpec((tm, MIX_A), row),
            pl.BlockSpec((tm, CONV_CH), row),
            pl.BlockSpec((tm, MIX_C), row),
            pl.BlockSpec(w_out.shape, c2),
            pl.BlockSpec((1, D), c2),
            pl.BlockSpec((1, D), c2),
            pl.BlockSpec((E, D), c2),
            pl.BlockSpec((E, 1), c2),
            pl.BlockSpec((E, N_GROUPS), c2),
        ],
        out_specs=[
            pl.BlockSpec((tm, D), row),
            pl.BlockSpec((tm, D // _pack_factor()), row),
            pl.BlockSpec((E, tm), col),
            pl.BlockSpec((E, tm), col),
            pl.BlockSpec((E, tm), col),
            pl.BlockSpec((E, LANES), c2),
        ],
        out_shape=[
            jax.ShapeDtypeStruct((T, D), f32),
            jax.ShapeDtypeStruct((T, D // _pack_factor()), jnp.int32),
            jax.ShapeDtypeStruct((E, T), f32),
            jax.ShapeDtypeStruct((E, T), f32),
            jax.ShapeDtypeStruct((E, T), f32),
            jax.ShapeDtypeStruct((E, LANES), f32),
        ],
        scratch_shapes=[pltpu.VMEM((E, 1), f32)],
        compiler_params=_cparams(("arbitrary",)),
        name="mix_router",
    )(x2, ya, yb, yc, w_out, ln_g, ln_b, w_router_t, router_bias, expand)


def _compact_kernel(sel_ref, w_ref, pos_ref, pstart_ref, low_ref, dest_ref, wk_ref):
    sel = sel_ref[...]
    on = sel > 0.5
    rank = _dot(low_ref[...], sel.astype(MXU_DTYPE))
    row = pstart_ref[...] + pos_ref[...]
    w = w_ref[...]
    dests, ws = [], []
    for k in range(TOP_K):
        m = on & (rank == float(k))
        dests.append(jnp.sum(jnp.where(m, row, 0.0), axis=0, keepdims=True))
        ws.append(jnp.sum(jnp.where(m, w, 0.0), axis=0, keepdims=True))
    dest_ref[...] = jnp.concatenate(dests, axis=0).astype(jnp.int32)
    wk_ref[...] = jnp.concatenate(ws, axis=0)


def _compact(sel_t, w_t, pos_t, pad_start, tm):
    E, T = sel_t.shape
    lower = (jnp.arange(E)[None, :] < jnp.arange(E)[:, None]).astype(MXU_DTYPE)
    col = lambda i: (0, i)
    c2 = lambda i: (0, 0)
    return pl.pallas_call(
        _compact_kernel,
        grid=(T // tm,),
        in_specs=[pl.BlockSpec((E, tm), col), pl.BlockSpec((E, tm), col), pl.BlockSpec((E, tm), col),
                  pl.BlockSpec((E, 1), c2), pl.BlockSpec((E, E), c2)],
        out_specs=[pl.BlockSpec((TOP_K, tm), col), pl.BlockSpec((TOP_K, tm), col)],
        out_shape=[jax.ShapeDtypeStruct((TOP_K, T), jnp.int32), jax.ShapeDtypeStruct((TOP_K, T), jnp.float32)],
        compiler_params=_cparams(("arbitrary",)),
        name="route_compact",
    )(sel_t, w_t, pos_t, pad_start, lower)


def _silu(g):
    return g / (1.0 + jnp.exp(-g))


def _expert_kernel(be_ref, nv_ref, nu_ref, xs_ref, wg_ref, wu_ref, wd_ref, ys_ref, wgb_ref, wub_ref, wdb_ref):
    i = pl.program_id(0)

    @pl.when((i == 0) | (be_ref[i] != be_ref[jnp.maximum(i - 1, 0)]))
    def _():
        wgb_ref[...] = wg_ref[0].astype(MXU_DTYPE)
        wub_ref[...] = wu_ref[0].astype(MXU_DTYPE)
        wdb_ref[...] = wd_ref[0].astype(MXU_DTYPE)

    @pl.when(i < nu_ref[0])
    def _():
        live = lax.broadcasted_iota(jnp.int32, (ROW_BLOCK, 1), 0) < nv_ref[i]
        parts = [jnp.where(live, v, jnp.zeros_like(v)) for v in _unpack_rows(xs_ref[...])]
        dk = wgb_ref.shape[0] // len(parts)

        def proj(w_ref):
            acc = _dot(parts[0], w_ref[0:dk, :])
            for n in range(1, len(parts)):
                acc = acc + _dot(parts[n], w_ref[n * dk:(n + 1) * dk, :])
            return acc

        a = (_silu(proj(wgb_ref)) * proj(wub_ref)).astype(MXU_DTYPE)
        ys_ref[...] = _pack_rows(_dot(a, wdb_ref[...]))


def _experts(xs, block_e, block_valid, n_used, w_gate, w_up, w_down):
    n_rows, W = xs.shape
    D = w_gate.shape[1]
    n_blocks = n_rows // ROW_BLOCK
    blk = lambda i, be, nv, nu: (jnp.minimum(i, nu[0] - 1), 0)
    wsel = lambda i, be, nv, nu: (be[i], 0, 0)
    return pl.pallas_call(
        _expert_kernel,
        grid_spec=pltpu.PrefetchScalarGridSpec(
            num_scalar_prefetch=3,
            grid=(n_blocks,),
            in_specs=[
                pl.BlockSpec((ROW_BLOCK, W), blk),
                pl.BlockSpec((1, D, D_EXPERT), wsel),
                pl.BlockSpec((1, D, D_EXPERT), wsel),
                pl.BlockSpec((1, D_EXPERT, D), wsel),
            ],
            out_specs=pl.BlockSpec((ROW_BLOCK, W), blk),
            scratch_shapes=[pltpu.VMEM((D, D_EXPERT), MXU_DTYPE), pltpu.VMEM((D, D_EXPERT), MXU_DTYPE),
                            pltpu.VMEM((D_EXPERT, D), MXU_DTYPE)],
        ),
        out_shape=jax.ShapeDtypeStruct((n_rows, W), xs.dtype),
        compiler_params=_cparams(("arbitrary",)),
        name="experts",
    )(block_e, block_valid, n_used, xs, w_gate, w_up, w_down)


SC_CORES = 2
SC_SUBCORES = 16
SC_GATHER_ROWS = 64
COMBINE_CHUNKS = 8


def _sc_gather_rows(table, idx):
    n = idx.shape[0]
    w = table.shape[1]
    n_workers = SC_CORES * SC_SUBCORES
    per_worker = n // n_workers
    assert n % n_workers == 0 and per_worker % SC_GATHER_ROWS == 0
    mesh = plsc.VectorSubcoreMesh(core_axis_name="c", subcore_axis_name="s")

    @functools.partial(
        pl.kernel, mesh=mesh,
        out_type=jax.ShapeDtypeStruct((n, w), table.dtype),
        scratch_types=[
            pltpu.VMEM((2, SC_GATHER_ROWS), jnp.int32),
            pltpu.VMEM((2, SC_GATHER_ROWS, w), table.dtype),
            pltpu.SemaphoreType.DMA((2,)),
        ],
        name="sc_gather_rows",
    )
    def gather(table_hbm, idx_hbm, out_hbm, idx_v, rows_v, sem):
        wid = lax.axis_index("s") * SC_CORES + lax.axis_index("c")
        base = wid * per_worker
        n_steps = per_worker // SC_GATHER_ROWS

        def gather_copy(slot):
            return pltpu.make_async_copy(table_hbm.at[idx_v.at[slot]], rows_v.at[slot], sem.at[slot])

        def start(step, slot):
            pltpu.sync_copy(idx_hbm.at[pl.ds(base + step * SC_GATHER_ROWS, SC_GATHER_ROWS)], idx_v.at[slot])
            gather_copy(slot).start()

        start(0, 0)

        @pl.loop(0, n_steps, step=2)
        def _(g):
            for slot in range(2):
                step = g + slot

                @pl.when(step + 1 < n_steps)
                def _():
                    start(step + 1, 1 - slot)

                gather_copy(slot).wait()
                pltpu.sync_copy(rows_v.at[slot], out_hbm.at[pl.ds(base + step * SC_GATHER_ROWS, SC_GATHER_ROWS)])

    return gather(table, idx)


SC_SCATTER_ROWS = 64


def _sc_scatter_rows(rows, idx3, n_out):
    n_src, w = rows.shape
    n_chunks, n_dst, batch = idx3.shape
    n_workers = SC_CORES * SC_SUBCORES
    assert batch == SC_SCATTER_ROWS and n_chunks * batch == n_src and n_chunks % (2 * n_workers) == 0
    per_worker = n_chunks // n_workers
    mesh = plsc.VectorSubcoreMesh(core_axis_name="c", subcore_axis_name="s")

    @functools.partial(
        pl.kernel, mesh=mesh,
        out_type=jax.ShapeDtypeStruct((n_out, w), rows.dtype),
        scratch_types=[
            pltpu.VMEM((2, n_dst, batch), jnp.int32),
            pltpu.VMEM((2, batch, w), rows.dtype),
            pltpu.SemaphoreType.DMA((2,)),
            pltpu.SemaphoreType.DMA,
        ],
        name="sc_scatter_rows",
    )
    def scatter(rows_hbm, idx_hbm, out_hbm, idx_v, rows_v, load_sem, store_sem):
        wid = lax.axis_index("s") * SC_CORES + lax.axis_index("c")

        def load_copy(step, slot):
            c = wid * per_worker + step
            return pltpu.make_async_copy(rows_hbm.at[pl.ds(c * batch, batch)], rows_v.at[slot], load_sem.at[slot])

        def load(step, slot):
            pltpu.sync_copy(idx_hbm.at[wid * per_worker + step], idx_v.at[slot])
            load_copy(step, slot).start()

        def store_copy(slot, k):
            return pltpu.make_async_copy(rows_v.at[slot], out_hbm.at[idx_v.at[slot].at[k]], store_sem)

        load(0, 0)

        @pl.loop(0, per_worker, step=2)
        def _(g):
            for slot in range(2):
                step = g + slot

                @pl.when(step + 1 < per_worker)
                def _():
                    load(step + 1, 1 - slot)

                load_copy(step, slot).wait()
                for k in range(n_dst):
                    store_copy(slot, k).start()
                for k in range(n_dst):
                    store_copy(slot, k).wait()

    return scatter(rows, idx3)


def _combine2_kernel(wk_ref, x1_ref, g_ref_rows, wsg_ref, wsu_ref, wsd_ref, g_ref, b_ref, o_ref):
    x1 = x1_ref[...]
    xb = x1.astype(MXU_DTYPE)
    a = (_silu(_dot(xb, wsg_ref[...])) * _dot(xb, wsu_ref[...])).astype(MXU_DTYPE)
    shared = _dot(a, wsd_ref[...])
    wk = wk_ref[...].T
    groups = [wk[:, 0:1] * v for v in _unpack_rows_f32(g_ref_rows[0])]
    for k in range(1, TOP_K):
        groups = [g + wk[:, k:k + 1] * v for g, v in zip(groups, _unpack_rows_f32(g_ref_rows[k]))]
    routed = jnp.concatenate(groups, axis=1)
    o_ref[...] = _layer_norm(ALPHA * x1 + (routed + shared), g_ref[...], b_ref[...])


def _combine2_kernel_into(wk_ref, x1_ref, g_ref_rows, wsg_ref, wsu_ref, wsd_ref, g_ref, b_ref, prev_ref, o_ref):
    del prev_ref
    _combine2_kernel(wk_ref, x1_ref, g_ref_rows, wsg_ref, wsu_ref, wsd_ref, g_ref, b_ref, o_ref)


def _combine2(wk_t, x1, gathered, w_sg, w_su, w_sd, ln_g, ln_b, tc, chunk, prev):
    T, D = x1.shape
    _, t_chunk, W = gathered.shape
    base = chunk * (t_chunk // tc)
    row = lambda i: (base + i, 0)
    c2 = lambda i: (0, 0)
    in_specs = [
        pl.BlockSpec((TOP_K, tc), lambda i: (0, base + i)),
        pl.BlockSpec((tc, D), row),
        pl.BlockSpec((TOP_K, tc, W), lambda i: (0, i, 0)),
        pl.BlockSpec(w_sg.shape, c2),
        pl.BlockSpec(w_su.shape, c2),
        pl.BlockSpec(w_sd.shape, c2),
        pl.BlockSpec((1, D), c2),
        pl.BlockSpec((1, D), c2),
    ]
    args = [wk_t, x1, gathered, w_sg, w_su, w_sd, ln_g, ln_b]
    if prev is None:
        body, aliases = _combine2_kernel, {}
    else:
        body, aliases = _combine2_kernel_into, {len(args): 0}
        in_specs.append(pl.BlockSpec(memory_space=pl.ANY))
        args.append(prev)
    return pl.pallas_call(
        body,
        grid=(t_chunk // tc,),
        in_specs=in_specs,
        out_specs=pl.BlockSpec((tc, D), row),
        out_shape=jax.ShapeDtypeStruct((T, D), jnp.float32),
        input_output_aliases=aliases,
        compiler_params=_cparams(("arbitrary",)),
        name="combine",
    )(*args)


def _split_w_in(w_in):
    o_kv = Q_RANK
    o_ki = o_kv + KV_RANK
    o_iw = o_ki + IDX_DIM
    o_rest = o_iw + N_IDX_HEADS
    w_small = jnp.pad(w_in[:, o_ki:o_rest], ((0, 0), (0, LANES - IDX_DIM - N_IDX_HEADS)))
    return jnp.concatenate([w_in[:, :o_ki], w_in[:, o_rest:], w_small], axis=1).astype(MXU_DTYPE)


def _stages(x, mem, w_in, q_norm_g, kv_norm_g, w_uq, w_uk, w_uv, w_qidx, rel_bias, conv_w, w_mem_k, w_mem_v, w_out, ln1_g, ln1_b, w_router, router_bias, w_e_gate, w_e_up, w_e_down, w_s_gate, w_s_up, w_s_down, ln2_g, ln2_b):
    B, S, D = x.shape
    T = B * S
    bf = MXU_DTYPE
    assert w_in.shape[0] == DEPTH == 1, "single-layer stack"
    l = 0
    res = {}
    x2 = x.reshape(T, D)
    cq, ckv, ckvt, kidx, iwt, yb, yc = _proj(
        x2, mem, _split_w_in(w_in[l]), q_norm_g[l].reshape(1, -1), kv_norm_g[l].reshape(1, -1), conv_w[l],
        w_mem_k[l].astype(bf), w_mem_v[l].astype(bf), B, S, tm=min(2048, S))
    res.update(c_q=cq, c_kv=ckv, k_idx=kidx, y_b=yb, y_c=yc,
               idx_w=jnp.swapaxes(iwt, 1, 2) / (N_IDX_HEADS ** -0.5 * IDX_DIM ** -0.5))
    bias_t = _bias_tiles(rel_bias)
    ya = _dsa(cq, iwt, kidx, ckv, ckvt,
              w_qidx[l].reshape(Q_RANK, -1).astype(bf), w_uq[l].reshape(Q_RANK, -1).astype(bf),
              jnp.transpose(w_uk[l], (1, 0, 2)).astype(bf), jnp.transpose(w_uv[l], (1, 2, 0)).astype(bf),
              bias_t, B, S)
    res.update(y_a=ya)

    x1, x1p, sel_t, w_t, pos_t, cnt = _mix_router(
        x2, ya, yb, yc, w_out[l].astype(bf), ln1_g[l].reshape(1, -1), ln1_b[l].reshape(1, -1),
        w_router[l].T, router_bias[l].reshape(-1, 1), tm=min(512, T))
    res.update(x1=x1)

    counts = cnt[:, 0].astype(jnp.int32)
    padded = (counts + ROW_BLOCK - 1) // ROW_BLOCK * ROW_BLOCK
    pad_end = jnp.cumsum(padded)
    pad_start = pad_end - padded
    n_blocks = -(-(T * TOP_K) // ROW_BLOCK) + N_EXPERTS
    n_rows = n_blocks * ROW_BLOCK
    block_start = jnp.arange(n_blocks, dtype=jnp.int32) * ROW_BLOCK
    block_e = jnp.minimum(jnp.sum((pad_end[None, :] <= block_start[:, None]).astype(jnp.int32), axis=1),
                          N_EXPERTS - 1)
    n_used = (pad_end[-1:] // ROW_BLOCK).astype(jnp.int32)

    dest_t, wk_t = _compact(sel_t, w_t, pos_t, pad_start.astype(jnp.float32).reshape(-1, 1), tm=min(512, T))
    block_valid = jnp.clip((pad_start + counts)[block_e] - block_start, 0, ROW_BLOCK).astype(jnp.int32)
    bt = SC_SCATTER_ROWS
    idx3 = jnp.transpose(dest_t.reshape(TOP_K, T // bt, bt), (1, 0, 2))
    xs = _sc_scatter_rows(x1p, idx3, n_rows)
    ys = _experts(xs, block_e, block_valid, n_used, w_e_gate[l], w_e_up[l], w_e_down[l])
    n_chunks = COMBINE_CHUNKS if T % (COMBINE_CHUNKS * 512) == 0 else 1
    t_chunk = T // n_chunks
    out = None
    for c in range(n_chunks):
        idx_c = dest_t[:, c * t_chunk:(c + 1) * t_chunk].reshape(-1)
        gathered = _sc_gather_rows(ys, idx_c).reshape(TOP_K, t_chunk, -1)
        out = _combine2(wk_t, x1, gathered, w_s_gate[l].astype(bf), w_s_up[l].astype(bf), w_s_down[l].astype(bf),
                        ln2_g[l].reshape(1, -1), ln2_b[l].reshape(1, -1), tc=min(512, t_chunk), chunk=c, prev=out)
    res.update(out=out.reshape(B, S, D))
    return res


def kernel(x, mem, w_in, q_norm_g, kv_norm_g, w_uq, w_uk, w_uv, w_qidx, rel_bias, conv_w, w_mem_k, w_mem_v, w_out, ln1_g, ln1_b, w_router, router_bias, w_e_gate, w_e_up, w_e_down, w_s_gate, w_s_up, w_s_down, ln2_g, ln2_b):
    return _stages(x, mem, w_in, q_norm_g, kv_norm_g, w_uq, w_uk, w_uv, w_qidx, rel_bias, conv_w, w_mem_k, w_mem_v, w_out, ln1_g, ln1_b, w_router, router_bias, w_e_gate, w_e_up, w_e_down, w_s_gate, w_s_up, w_s_down, ln2_g, ln2_b)["out"]
```

```python
import functools
import math

import jax
import jax.numpy as jnp
from jax import lax
from jax.experimental import pallas as pl
from jax.experimental.pallas import tpu as pltpu
from jax.experimental.pallas import tpu_sc as plsc

N_HEADS_A = 8
HEAD_DIM = 64
Q_RANK = 256
KV_RANK = 128
N_IDX_HEADS = 8
IDX_DIM = 64
TOPK_MAX = 256
REL_BUCKETS = 32
REL_MAX_DIST = 128
CONV_CH = 256
CONV_WIDTH = 3
N_MEM_HEADS = 4
MIX_A = N_HEADS_A * HEAD_DIM
MIX_C = N_MEM_HEADS * HEAD_DIM
N_EXPERTS = 64
N_GROUPS = 8
GROUP_SIZE = N_EXPERTS // N_GROUPS
TOPK_GROUPS = 4
TOP_K = 8
D_EXPERT = 256
ROUTED_SCALE = 2.5
DEPTH = 1
ALPHA = (2.0 * DEPTH) ** 0.25
LN_EPS = 1e-5
RMS_EPS = 1e-6
LOG2_E = math.log2(math.e)

LANES = 128
SUBLANES = 8
QB = 128
F32_LOWEST = -3.4028234663852886e38
VMEM_LIMIT = 56 * 1024 * 1024
MXU_DTYPE = jnp.bfloat16
ROW_BLOCK = 1024

_NT = (((1,), (1,)), ((), ()))


def _dot(a, b):
    return jnp.dot(a, b, preferred_element_type=jnp.float32)


def _dot_nt(a, b):
    return lax.dot_general(a, b, _NT, preferred_element_type=jnp.float32)


def _cparams(sem):
    return pltpu.CompilerParams(dimension_semantics=sem, vmem_limit_bytes=VMEM_LIMIT)


def _bias_kernel(rb_ref, o_ref):
    s = lax.broadcasted_iota(jnp.int32, (QB, QB), 0)
    t = lax.broadcasted_iota(jnp.int32, (QB, QB), 1)
    max_exact = REL_BUCKETS // 2
    for tile in range(3):
        n = jnp.maximum(t - s + (2 - tile) * QB, 0)
        nf = jnp.maximum(n.astype(jnp.float32), 1.0)
        large = max_exact + (jnp.log(nf / max_exact) / math.log(REL_MAX_DIST / max_exact)
                             * (REL_BUCKETS - max_exact)).astype(jnp.int32)
        large = jnp.minimum(large, REL_BUCKETS - 1)
        bucket = jnp.where(n < max_exact, n, large)
        for h in range(N_HEADS_A):
            acc = jnp.zeros((QB, QB), jnp.float32)
            for b in range(REL_BUCKETS):
                acc = jnp.where(bucket == b, rb_ref[b, h], acc)
            o_ref[tile, h] = acc * LOG2_E


def _bias_tiles(rel_bias):
    return pl.pallas_call(
        _bias_kernel,
        in_specs=[pl.BlockSpec(memory_space=pltpu.SMEM)],
        out_specs=pl.BlockSpec(memory_space=pltpu.VMEM),
        out_shape=jax.ShapeDtypeStruct((3, N_HEADS_A, QB, QB), jnp.float32),
        name="bias_tiles",
    )(rel_bias)


def _proj_kernel(x_ref, mem_ref, wm_ref, qg_ref, kvg_ref, cw_ref, wmk_ref, wmv_ref,
                 cq_ref, ckv_ref, ckvt_ref, kidx_ref, iwt_ref, yb_ref, yc_ref,
                 carry_ref, mk_ref, mv_ref, *, tm):
    si = pl.program_id(1)

    @pl.when(si == 0)
    def _():
        carry_ref[...] = jnp.zeros_like(carry_ref)
        mb = mem_ref[0].astype(MXU_DTYPE)
        mk_ref[...] = _dot(mb, wmk_ref[...]).astype(MXU_DTYPE)
        mv_ref[...] = _dot(mb, wmv_ref[...]).astype(MXU_DTYPE)

    xb = x_ref[...].astype(MXU_DTYPE)
    p = _dot(xb, wm_ref[...])
    small = p[:, p.shape[1] - LANES:]

    o = 0
    cq = p[:, o:o + Q_RANK]; o += Q_RANK
    ckv = p[:, o:o + KV_RANK]; o += KV_RANK
    g_b = p[:, o:o + CONV_CH]; o += CONV_CH
    g_c = p[:, o:o + CONV_CH]; o += CONV_CH
    h_c = p[:, o:o + CONV_CH]; o += CONV_CH
    q_mem = p[:, o:o + MIX_C]

    cq = cq * lax.rsqrt(jnp.mean(cq * cq, axis=-1, keepdims=True) + RMS_EPS) * qg_ref[...]
    ckv = ckv * lax.rsqrt(jnp.mean(ckv * ckv, axis=-1, keepdims=True) + RMS_EPS) * kvg_ref[...]
    cq_ref[...] = cq.astype(MXU_DTYPE)
    ckv_b = ckv.astype(MXU_DTYPE)
    ckv_ref[...] = ckv_b
    ckvt_ref[0] = ckv.T.astype(MXU_DTYPE)

    kidx_ref[...] = small[:, :IDX_DIM].astype(MXU_DTYPE)
    small_t = small.T
    iwt_ref[0] = small_t[IDX_DIM:IDX_DIM + N_IDX_HEADS, :] * (N_IDX_HEADS ** -0.5 * IDX_DIM ** -0.5)

    u = g_c * h_c
    rows = lax.broadcasted_iota(jnp.int32, (tm, 1), 0)
    c6 = carry_ref[SUBLANES - 2:SUBLANES - 1, :]
    c7 = carry_ref[SUBLANES - 1:SUBLANES, :]
    u1 = jnp.where(rows == 0, c7, pltpu.roll(u, 1, 0))
    u2 = jnp.where(rows == 0, c6, jnp.where(rows == 1, c7, pltpu.roll(u, 2, 0)))
    y = cw_ref[0:1, :] * u2
    y = y + cw_ref[1:2, :] * u1
    y = y + cw_ref[2:3, :] * u
    yb_ref[...] = (g_b * y).astype(MXU_DTYPE)
    carry_ref[...] = u[tm - SUBLANES:, :]

    qm = q_mem.astype(MXU_DTYPE)
    outs = []
    for h in range(N_MEM_HEADS):
        sl = slice(h * HEAD_DIM, (h + 1) * HEAD_DIM)
        lg = _dot_nt(qm[:, sl], mk_ref[:, sl]) * (HEAD_DIM ** -0.5)
        lg = lg - jnp.max(lg, axis=-1, keepdims=True)
        e = jnp.exp(lg)
        pr = e / jnp.sum(e, axis=-1, keepdims=True)
        outs.append(_dot(pr.astype(MXU_DTYPE), mv_ref[:, sl]))
    yc_ref[...] = jnp.concatenate(outs, axis=-1).astype(MXU_DTYPE)


def _proj(x2, mem, w_main, q_g, kv_g, conv_w, w_mk, w_mv, B, S, tm):
    T, D = x2.shape
    n_mem = mem.shape[1]
    ns = S // tm
    row = lambda b, s: (b * ns + s, 0)
    const2 = lambda b, s: (0, 0)
    bf = MXU_DTYPE
    return pl.pallas_call(
        functools.partial(_proj_kernel, tm=tm),
        grid=(B, ns),
        in_specs=[
            pl.BlockSpec((tm, D), row),
            pl.BlockSpec((1, n_mem, D), lambda b, s: (b, 0, 0)),
            pl.BlockSpec(w_main.shape, const2),
            pl.BlockSpec(q_g.shape, const2),
            pl.BlockSpec(kv_g.shape, const2),
            pl.BlockSpec(conv_w.shape, const2),
            pl.BlockSpec(w_mk.shape, const2),
            pl.BlockSpec(w_mv.shape, const2),
        ],
        out_specs=[
            pl.BlockSpec((tm, Q_RANK), row),
            pl.BlockSpec((tm, KV_RANK), row),
            pl.BlockSpec((1, KV_RANK, tm), lambda b, s: (b, 0, s)),
            pl.BlockSpec((tm, IDX_DIM), row),
            pl.BlockSpec((1, N_IDX_HEADS, tm), lambda b, s: (b, 0, s)),
            pl.BlockSpec((tm, CONV_CH), row),
            pl.BlockSpec((tm, MIX_C), row),
        ],
        out_shape=[
            jax.ShapeDtypeStruct((T, Q_RANK), bf),
            jax.ShapeDtypeStruct((T, KV_RANK), bf),
            jax.ShapeDtypeStruct((B, KV_RANK, S), bf),
            jax.ShapeDtypeStruct((T, IDX_DIM), bf),
            jax.ShapeDtypeStruct((B, N_IDX_HEADS, S), jnp.float32),
            jax.ShapeDtypeStruct((T, CONV_CH), bf),
            jax.ShapeDtypeStruct((T, MIX_C), bf),
        ],
        scratch_shapes=[
            pltpu.VMEM((SUBLANES, CONV_CH), jnp.float32),
            pltpu.VMEM((n_mem, MIX_C), bf),
            pltpu.VMEM((n_mem, MIX_C), bf),
        ],
        compiler_params=_cparams(("arbitrary", "arbitrary")),
        name="proj",
    )(x2, mem, w_main, q_g, kv_g, conv_w, w_mk, w_mv)


def _key_to_f32(key):
    bits = jnp.where(key < 0, key ^ jnp.int32(0x7FFFFFFF), key)
    return pltpu.bitcast(bits, jnp.float32)


def _colsum8(v):
    return jnp.sum(v.reshape(QB // SUBLANES, SUBLANES, QB), axis=0)


def _colmax8(v):
    return jnp.max(v.reshape(QB // SUBLANES, SUBLANES, QB), axis=0)


UNROLL_WIDTHS = (8, 4, 2, 1)


def _dsa_kernel(cq_ref, iwt_ref, kidx_ref, ckv_ref, ckvt_ref, wqi_ref, wuq_ref, wuk_ref, wuvt_ref,
                bias_ref, o_ref, wfold_ref, qidx_ref, qlat_ref, score_ref, logit_ref, acc_ref,
                *, k_sel, idx_bits):
    i = pl.program_id(1)
    f32 = jnp.float32
    bf = MXU_DTYPE
    n_blocks = i + 1
    n_blocks = n_blocks + jnp.where((n_blocks % 4 == 3) & (n_blocks < pl.num_programs(1)), 1, 0)
    s_loc = lax.broadcasted_iota(jnp.int32, (QB, QB), 0)
    t_glob = i * QB + lax.broadcasted_iota(jnp.int32, (QB, QB), 1)

    def blk(jb):
        return pl.multiple_of(jb * QB, QB)

    def block_loop(fn, init):
        c, start = init, 0
        for width in UNROLL_WIDTHS:
            n = (n_blocks - start) // width
            c = lax.fori_loop(0, n, lambda it, c, w=width, s=start: fn(s + it * w, w, c), c)
            start = start + n * width
        return c

    @pl.when(i == 0)
    def _():
        for h in range(N_HEADS_A):
            wfold_ref[:, h * KV_RANK:(h + 1) * KV_RANK] = (
                _dot_nt(wuq_ref[:, h * HEAD_DIM:(h + 1) * HEAD_DIM], wuk_ref[h])
                * (HEAD_DIM ** -0.5 * LOG2_E)).astype(bf)

    cq = cq_ref[...]
    q_idx = _dot(cq, wqi_ref[...]).astype(bf)
    q_lat = _dot(cq, wfold_ref[...]).astype(bf)
    for h in range(N_HEADS_A):
        qidx_ref[h * QB:(h + 1) * QB, :] = q_idx[:, h * IDX_DIM:(h + 1) * IDX_DIM]
        qlat_ref[h * QB:(h + 1) * QB, :] = q_lat[:, h * KV_RANK:(h + 1) * KV_RANK]
    iw = iwt_ref[0]

    def score_body(jb0, nb, n_pos8):
        d_blk = _dot_nt(kidx_ref[pl.ds(blk(jb0), nb * QB), :], qidx_ref[...])
        for sb in range(nb):
            off = blk(jb0 + sb)
            d_all = d_blk[sb * QB:(sb + 1) * QB, :]
            acc = jnp.maximum(d_all[:, 0:QB], 0.0) * iw[0:1, :]
            for h in range(1, N_IDX_HEADS):
                acc = acc + jnp.maximum(d_all[:, h * QB:(h + 1) * QB], 0.0) * iw[h:h + 1, :]
            sc = jnp.where(s_loc + off <= t_glob, acc + 0.0, F32_LOWEST)
            score_ref[pl.ds(off, QB), :] = sc
            n_pos8 = n_pos8 + _colsum8(jnp.where(sc >= 0.0, 1.0, 0.0))
        return n_pos8

    n_pos8 = block_loop(score_body, jnp.zeros((SUBLANES, QB), f32))

    def count_where(pred):
        def body(jb0, nb, acc):
            for sb in range(nb):
                off = blk(jb0 + sb)
                acc = acc + _colsum8(jnp.where(pred(score_ref[pl.ds(off, QB), :], off), 1.0, 0.0))
            return acc
        acc = block_loop(body, jnp.zeros((SUBLANES, QB), f32))
        return jnp.sum(acc, axis=0, keepdims=True)

    kf = float(k_sel)

    def search():
        c0 = jnp.sum(n_pos8, axis=0, keepdims=True)
        cand0 = jnp.where(c0 >= kf, jnp.int32(0), jnp.int32(-2 ** 31))
        n_ge0 = jnp.where(c0 >= kf, c0, -1.0)

        def bit_body(it, carry):
            cand, n_ge = carry
            trial = cand + lax.shift_left(jnp.int32(1), 30 - it)
            tf = _key_to_f32(trial)
            cnt = count_where(lambda sc, off: sc >= tf)
            take = cnt >= kf
            return jnp.where(take, trial, cand), jnp.where(take, cnt, n_ge)

        cand, n_ge = lax.fori_loop(0, 31, bit_body, (cand0, n_ge0))
        thr = _key_to_f32(cand)
        keep_all_ties = jnp.full((1, QB), 2 ** idx_bits - 1, jnp.int32)

        def resolve_ties():
            n_gt = count_where(lambda sc, off: sc > thr)
            n_eq = count_where(lambda sc, off: sc == thr)
            need = kf - n_gt

            def tie_search():
                def tbody(it, xcut):
                    trial = xcut + lax.shift_left(jnp.int32(1), idx_bits - 1 - it)
                    cnt = count_where(lambda sc, off: (sc == thr) & (s_loc + off < trial))
                    return jnp.where(cnt < need, trial, xcut)
                return lax.fori_loop(0, idx_bits, tbody, jnp.zeros((1, QB), jnp.int32))

            return lax.cond(jnp.max(n_eq - need) > 0.0, tie_search, lambda: keep_all_ties)

        xcut = lax.cond(jnp.max(jnp.abs(n_ge - kf)) > 0.0, resolve_ties, lambda: keep_all_ties)
        return thr, xcut

    def no_search():
        return jnp.full((1, QB), F32_LOWEST, f32), jnp.full((1, QB), 2 ** idx_bits - 1, jnp.int32)

    thr, xcut = lax.cond((i + 1) * QB > k_sel, search, no_search)

    def selection_mask(off):
        sc = score_ref[pl.ds(off, QB), :]
        s_glob = s_loc + off
        keep = ((sc > thr) | ((sc == thr) & (s_glob <= xcut))) & (s_glob <= t_glob)
        return jnp.where(keep, 0.0, -jnp.inf)

    acc_ref[...] = jnp.zeros_like(acc_ref)

    def att_body(jb0, nb, carry):
        m, l8 = list(carry[0]), list(carry[1])
        rows = nb * QB
        lg_blk = _dot_nt(ckv_ref[pl.ds(blk(jb0), rows), :], qlat_ref[...])
        blk_max = [None] * N_HEADS_A
        for sb in range(nb):
            off = blk(jb0 + sb)
            msk = selection_mask(off)
            bsel = jnp.clip(jb0 + sb - i + 2, 0, 2)
            for h in range(N_HEADS_A):
                lgh = lg_blk[sb * QB:(sb + 1) * QB, h * QB:(h + 1) * QB] + bias_ref[bsel, h] + msk
                logit_ref[sb * QB:(sb + 1) * QB, h * QB:(h + 1) * QB] = lgh
                cm = _colmax8(lgh)
                blk_max[h] = cm if blk_max[h] is None else jnp.maximum(blk_max[h], cm)
        ps, scales = [], []
        for h in range(N_HEADS_A):
            m_new = jnp.maximum(m[h], jnp.max(blk_max[h], axis=0, keepdims=True))
            m_ref = jnp.where(m_new == -jnp.inf, 0.0, m_new)
            p = jnp.exp2(logit_ref[0:rows, h * QB:(h + 1) * QB] - m_ref)
            scale = jnp.exp2(m[h] - m_ref)
            l8[h] = l8[h] * scale + jnp.sum(p.reshape(rows // SUBLANES, SUBLANES, QB), axis=0)
            m[h] = m_new
            ps.append(p.astype(bf))
            scales.append(scale)
        pv = _dot(ckvt_ref[0, :, pl.ds(blk(jb0), rows)], jnp.concatenate(ps, axis=1))
        for h in range(N_HEADS_A):
            hs = slice(h * QB, (h + 1) * QB)
            acc_ref[:, hs] = acc_ref[:, hs] * scales[h] + pv[:, hs]
        return tuple(m), tuple(l8)

    _, l8 = block_loop(att_body, (tuple(jnp.full((1, QB), -jnp.inf, f32) for _ in range(N_HEADS_A)),
                                  tuple(jnp.zeros((SUBLANES, QB), f32) for _ in range(N_HEADS_A))))

    outs = []
    for h in range(N_HEADS_A):
        l_row = jnp.sum(l8[h], axis=0, keepdims=True)
        o_lat_t = (acc_ref[:, h * QB:(h + 1) * QB] / l_row).astype(bf)
        outs.append(_dot(wuvt_ref[h], o_lat_t))
    o_ref[...] = jnp.concatenate(outs, axis=0).T.astype(o_ref.dtype)


def _dsa(cq, iwt, kidx, ckv, ckvt, w_qidx, w_uq, w_uk_h, w_uvt_h, bias_tiles, B, S):
    T = cq.shape[0]
    assert S % QB == 0 and QB >= REL_MAX_DIST
    nq = S // QB
    k_sel = min(TOPK_MAX, S // 4)
    idx_bits = max(1, (S - 1).bit_length())
    c2 = lambda b, i: (0, 0)
    c3 = lambda b, i: (0, 0, 0)
    return pl.pallas_call(
        functools.partial(_dsa_kernel, k_sel=k_sel, idx_bits=idx_bits),
        grid=(B, nq),
        in_specs=[
            pl.BlockSpec((QB, Q_RANK), lambda b, i: (b * nq + i, 0)),
            pl.BlockSpec((1, N_IDX_HEADS, QB), lambda b, i: (b, 0, i)),
            pl.BlockSpec((S, IDX_DIM), lambda b, i: (b, 0)),
            pl.BlockSpec((S, KV_RANK), lambda b, i: (b, 0)),
            pl.BlockSpec((1, KV_RANK, S), lambda b, i: (b, 0, 0)),
            pl.BlockSpec(w_qidx.shape, c2),
            pl.BlockSpec(w_uq.shape, c2),
            pl.BlockSpec(w_uk_h.shape, c3),
            pl.BlockSpec(w_uvt_h.shape, c3),
            pl.BlockSpec(bias_tiles.shape, lambda b, i: (0, 0, 0, 0)),
        ],
        out_specs=pl.BlockSpec((QB, MIX_A), lambda b, i: (b * nq + i, 0)),
        out_shape=jax.ShapeDtypeStruct((T, MIX_A), MXU_DTYPE),
        scratch_shapes=[
            pltpu.VMEM((Q_RANK, N_HEADS_A * KV_RANK), MXU_DTYPE),
            pltpu.VMEM((N_IDX_HEADS * QB, IDX_DIM), MXU_DTYPE),
            pltpu.VMEM((N_HEADS_A * QB, KV_RANK), MXU_DTYPE),
            pltpu.VMEM((S, QB), jnp.float32),
            pltpu.VMEM((max(UNROLL_WIDTHS) * QB, N_HEADS_A * QB), jnp.float32),
            pltpu.VMEM((KV_RANK, N_HEADS_A * QB), jnp.float32),
        ],
        compiler_params=_cparams(("arbitrary", "arbitrary")),
        name="dsa",
    )(cq, iwt, kidx, ckv, ckvt, w_qidx, w_uq, w_uk_h, w_uvt_h, bias_tiles)


def _layer_norm(xf, g, b):
    mu = jnp.mean(xf, axis=-1, keepdims=True)
    xc = xf - mu
    var = jnp.mean(xc * xc, axis=-1, keepdims=True)
    return xc * lax.rsqrt(var + LN_EPS) * g + b


def _rank_rows(v, n):
    ri = lax.broadcasted_iota(jnp.int32, v.shape, 0)
    rank = jnp.zeros(v.shape, jnp.float32)
    for r2 in range(n):
        row = v[r2:r2 + 1, :]
        beats = (row > v) | ((row == v) & (ri > r2))
        rank = rank + jnp.where(beats, 1.0, 0.0)
    return rank


def _top_rows(v, k):
    n = v.shape[0]
    ri = lax.broadcasted_iota(jnp.int32, v.shape, 0)
    sel = jnp.zeros(v.shape, jnp.float32)
    for _ in range(k):
        m = jnp.max(v, axis=0, keepdims=True)
        first = jnp.min(jnp.where(v == m, ri, n), axis=0, keepdims=True)
        pick = ri == first
        sel = jnp.where(pick, 1.0, sel)
        v = jnp.where(pick, -jnp.inf, v)
    return sel > 0.5


def _pack_factor():
    return 4 // jnp.dtype(MXU_DTYPE).itemsize


def _pack_rows(x):
    if _pack_factor() == 1:
        return pltpu.bitcast(x, jnp.int32)
    half = x.shape[1] // 2
    b = pltpu.bitcast(x.astype(MXU_DTYPE).astype(jnp.float32), jnp.int32)
    return b[:, half:] | lax.shift_right_logical(b[:, :half], jnp.int32(16))


_HIGH_HALF = -(1 << 16)


def _unpack_rows_f32(p):
    if _pack_factor() == 1:
        return [pltpu.bitcast(p, jnp.float32)]
    lo = pltpu.bitcast(lax.shift_left(p, jnp.int32(16)), jnp.float32)
    hi = pltpu.bitcast(p & jnp.int32(_HIGH_HALF), jnp.float32)
    return [lo, hi]


def _unpack_rows(p):
    return [v.astype(MXU_DTYPE) for v in _unpack_rows_f32(p)]


def _mix_router_kernel(x_ref, ya_ref, yb_ref, yc_ref, wo_ref, g_ref, b_ref, wrt_ref, rb_ref, exp_ref,
                       x1_ref, x1p_ref, sel_ref, w_ref, pos_ref, cnt_ref, base_ref, *, tm):
    step = pl.program_id(0)
    f32 = jnp.float32

    @pl.when(step == 0)
    def _():
        base_ref[...] = jnp.zeros_like(base_ref)

    mix = _dot(ya_ref[...], wo_ref[0:MIX_A, :])
    mix = mix + _dot(yb_ref[...], wo_ref[MIX_A:MIX_A + CONV_CH, :])
    mix = mix + _dot(yc_ref[...], wo_ref[MIX_A + CONV_CH:, :])
    x1 = _layer_norm(ALPHA * x_ref[...] + mix, g_ref[...], b_ref[...])
    x1_ref[...] = x1
    x1p_ref[...] = _pack_rows(x1)

    lg = lax.dot_general(wrt_ref[...], x1, _NT, precision=lax.Precision.HIGHEST, preferred_element_type=f32)
    s = 1.0 / (1.0 + jnp.exp(-lg))
    sc = s + rb_ref[...]

    g3 = sc.reshape(N_GROUPS, GROUP_SIZE, tm)
    m1 = jnp.max(g3, axis=1, keepdims=True)
    is_m1 = g3 == m1
    n_m1 = jnp.sum(jnp.where(is_m1, 1.0, 0.0), axis=1, keepdims=True)
    m2 = jnp.max(jnp.where(is_m1, -jnp.inf, g3), axis=1, keepdims=True)
    gscore = (m1 + jnp.where(n_m1 > 1.0, m1, m2)).reshape(N_GROUPS, tm)
    gsel = jnp.where(_rank_rows(gscore, N_GROUPS) < float(TOPK_GROUPS), 1.0, 0.0)
    emask = _dot(exp_ref[...], gsel.astype(MXU_DTYPE)) > 0.5
    masked = jnp.where(emask, sc, -jnp.inf)
    sel = _top_rows(masked, TOP_K) & emask
    self_ = jnp.where(sel, 1.0, 0.0)
    top_s = jnp.where(sel, s, 0.0)
    w = top_s / jnp.sum(top_s, axis=0, keepdims=True) * ROUTED_SCALE

    t_r = lax.broadcasted_iota(jnp.int32, (tm, tm), 0)
    t_c = lax.broadcasted_iota(jnp.int32, (tm, tm), 1)
    upper = jnp.where(t_r < t_c, 1.0, 0.0).astype(MXU_DTYPE)
    pref = _dot(self_.astype(MXU_DTYPE), upper)
    base = base_ref[...]
    sel_ref[...] = self_
    w_ref[...] = w
    pos_ref[...] = base + pref
    base = base + jnp.sum(self_, axis=1, keepdims=True)
    base_ref[...] = base
    cnt_ref[...] = jnp.broadcast_to(base, cnt_ref.shape)


def _mix_router(x2, ya, yb, yc, w_out, ln_g, ln_b, w_router_t, router_bias, tm):
    T, D = x2.shape
    E = N_EXPERTS
    expand = (jnp.arange(E)[:, None] // GROUP_SIZE == jnp.arange(N_GROUPS)[None, :]).astype(MXU_DTYPE)
    row = lambda i: (i, 0)
    col = lambda i: (0, i)
    c2 = lambda i: (0, 0)
    f32 = jnp.float32
    return pl.pallas_call(
        functools.partial(_mix_router_kernel, tm=tm),
        grid=(T // tm,),
        in_specs=[
            pl.BlockSpec((tm, D), row),
            pl.BlockSpec((tm, MIX_A), row),
            pl.BlockSpec((tm, CONV_CH), row),
            pl.BlockSpec((tm, MIX_C), row),
            pl.BlockSpec(w_out.shape, c2),
            pl.BlockSpec((1, D), c2),
            pl.BlockSpec((1, D), c2),
            pl.BlockSpec((E, D), c2),
            pl.BlockSpec((E, 1), c2),
            pl.BlockSpec((E, N_GROUPS), c2),
        ],
        out_specs=[
            pl.BlockSpec((tm, D), row),
            pl.BlockSpec((tm, D // _pack_factor()), row),
            pl.BlockSpec((E, tm), col),
            pl.BlockSpec((E, tm), col),
            pl.BlockSpec((E, tm), col),
            pl.BlockSpec((E, LANES), c2),
        ],
        out_shape=[
            jax.ShapeDtypeStruct((T, D), f32),
            jax.ShapeDtypeStruct((T, D // _pack_factor()), jnp.int32),
            jax.ShapeDtypeStruct((E, T), f32),
            jax.ShapeDtypeStruct((E, T), f32),
            jax.ShapeDtypeStruct((E, T), f32),
            jax.ShapeDtypeStruct((E, LANES), f32),
        ],
        scratch_shapes=[pltpu.VMEM((E, 1), f32)],
        compiler_params=_cparams(("arbitrary",)),
        name="mix_router",
    )(x2, ya, yb, yc, w_out, ln_g, ln_b, w_router_t, router_bias, expand)


def _compact_kernel(sel_ref, w_ref, pos_ref, pstart_ref, low_ref, dest_ref, wk_ref):
    sel = sel_ref[...]
    on = sel > 0.5
    rank = _dot(low_ref[...], sel.astype(MXU_DTYPE))
    row = pstart_ref[...] + pos_ref[...]
    w = w_ref[...]
    dests, ws = [], []
    for k in range(TOP_K):
        m = on & (rank == float(k))
        dests.append(jnp.sum(jnp.where(m, row, 0.0), axis=0, keepdims=True))
        ws.append(jnp.sum(jnp.where(m, w, 0.0), axis=0, keepdims=True))
    dest_ref[...] = jnp.concatenate(dests, axis=0).astype(jnp.int32)
    wk_ref[...] = jnp.concatenate(ws, axis=0)


def _compact(sel_t, w_t, pos_t, pad_start, tm):
    E, T = sel_t.shape
    lower = (jnp.arange(E)[None, :] < jnp.arange(E)[:, None]).astype(MXU_DTYPE)
    col = lambda i: (0, i)
    c2 = lambda i: (0, 0)
    return pl.pallas_call(
        _compact_kernel,
        grid=(T // tm,),
        in_specs=[pl.BlockSpec((E, tm), col), pl.BlockSpec((E, tm), col), pl.BlockSpec((E, tm), col),
                  pl.BlockSpec((E, 1), c2), pl.BlockSpec((E, E), c2)],
        out_specs=[pl.BlockSpec((TOP_K, tm), col), pl.BlockSpec((TOP_K, tm), col)],
        out_shape=[jax.ShapeDtypeStruct((TOP_K, T), jnp.int32), jax.ShapeDtypeStruct((TOP_K, T), jnp.float32)],
        compiler_params=_cparams(("arbitrary",)),
        name="route_compact",
    )(sel_t, w_t, pos_t, pad_start, lower)


def _silu(g):
    return g / (1.0 + jnp.exp(-g))


def _expert_kernel(be_ref, nv_ref, nu_ref, xs_ref, wg_ref, wu_ref, wd_ref, ys_ref, wgb_ref, wub_ref, wdb_ref):
    i = pl.program_id(0)

    @pl.when((i == 0) | (be_ref[i] != be_ref[jnp.maximum(i - 1, 0)]))
    def _():
        wgb_ref[...] = wg_ref[0].astype(MXU_DTYPE)
        wub_ref[...] = wu_ref[0].astype(MXU_DTYPE)
        wdb_ref[...] = wd_ref[0].astype(MXU_DTYPE)

    @pl.when(i < nu_ref[0])
    def _():
        live = lax.broadcasted_iota(jnp.int32, (ROW_BLOCK, 1), 0) < nv_ref[i]
        parts = [jnp.where(live, v, jnp.zeros_like(v)) for v in _unpack_rows(xs_ref[...])]
        dk = wgb_ref.shape[0] // len(parts)

        def proj(w_ref):
            acc = _dot(parts[0], w_ref[0:dk, :])
            for n in range(1, len(parts)):
                acc = acc + _dot(parts[n], w_ref[n * dk:(n + 1) * dk, :])
            return acc

        a = (_silu(proj(wgb_ref)) * proj(wub_ref)).astype(MXU_DTYPE)
        ys_ref[...] = _pack_rows(_dot(a, wdb_ref[...]))


def _experts(xs, block_e, block_valid, n_used, w_gate, w_up, w_down):
    n_rows, W = xs.shape
    D = w_gate.shape[1]
    n_blocks = n_rows // ROW_BLOCK
    blk = lambda i, be, nv, nu: (jnp.minimum(i, nu[0] - 1), 0)
    wsel = lambda i, be, nv, nu: (be[i], 0, 0)
    return pl.pallas_call(
        _expert_kernel,
        grid_spec=pltpu.PrefetchScalarGridSpec(
            num_scalar_prefetch=3,
            grid=(n_blocks,),
            in_specs=[
                pl.BlockSpec((ROW_BLOCK, W), blk),
                pl.BlockSpec((1, D, D_EXPERT), wsel),
                pl.BlockSpec((1, D, D_EXPERT), wsel),
                pl.BlockSpec((1, D_EXPERT, D), wsel),
            ],
            out_specs=pl.BlockSpec((ROW_BLOCK, W), blk),
            scratch_shapes=[pltpu.VMEM((D, D_EXPERT), MXU_DTYPE), pltpu.VMEM((D, D_EXPERT), MXU_DTYPE),
                            pltpu.VMEM((D_EXPERT, D), MXU_DTYPE)],
        ),
        out_shape=jax.ShapeDtypeStruct((n_rows, W), xs.dtype),
        compiler_params=_cparams(("arbitrary",)),
        name="experts",
    )(block_e, block_valid, n_used, xs, w_gate, w_up, w_down)


SC_CORES = 2
SC_SUBCORES = 16
SC_GATHER_ROWS = 64
COMBINE_CHUNKS = 8


def _sc_gather_rows(table, idx):
    n = idx.shape[0]
    w = table.shape[1]
    n_workers = SC_CORES * SC_SUBCORES
    per_worker = n // n_workers
    assert n % n_workers == 0 and per_worker % SC_GATHER_ROWS == 0
    mesh = plsc.VectorSubcoreMesh(core_axis_name="c", subcore_axis_name="s")

    @functools.partial(
        pl.kernel, mesh=mesh,
        out_type=jax.ShapeDtypeStruct((n, w), table.dtype),
        scratch_types=[
            pltpu.VMEM((2, SC_GATHER_ROWS), jnp.int32),
            pltpu.VMEM((2, SC_GATHER_ROWS, w), table.dtype),
            pltpu.SemaphoreType.DMA((2,)),
        ],
        name="sc_gather_rows",
    )
    def gather(table_hbm, idx_hbm, out_hbm, idx_v, rows_v, sem):
        wid = lax.axis_index("s") * SC_CORES + lax.axis_index("c")
        base = wid * per_worker
        n_steps = per_worker // SC_GATHER_ROWS

        def gather_copy(slot):
            return pltpu.make_async_copy(table_hbm.at[idx_v.at[slot]], rows_v.at[slot], sem.at[slot])

        def start(step, slot):
            pltpu.sync_copy(idx_hbm.at[pl.ds(base + step * SC_GATHER_ROWS, SC_GATHER_ROWS)], idx_v.at[slot])
            gather_copy(slot).start()

        start(0, 0)

        @pl.loop(0, n_steps, step=2)
        def _(g):
            for slot in range(2):
                step = g + slot

                @pl.when(step + 1 < n_steps)
                def _():
                    start(step + 1, 1 - slot)

                gather_copy(slot).wait()
                pltpu.sync_copy(rows_v.at[slot], out_hbm.at[pl.ds(base + step * SC_GATHER_ROWS, SC_GATHER_ROWS)])

    return gather(table, idx)


SC_SCATTER_ROWS = 64


def _sc_scatter_rows(rows, idx3, n_out):
    n_src, w = rows.shape
    n_chunks, n_dst, batch = idx3.shape
    n_workers = SC_CORES * SC_SUBCORES
    assert batch == SC_SCATTER_ROWS and n_chunks * batch == n_src and n_chunks % (2 * n_workers) == 0
    per_worker = n_chunks // n_workers
    mesh = plsc.VectorSubcoreMesh(core_axis_name="c", subcore_axis_name="s")

    @functools.partial(
        pl.kernel, mesh=mesh,
        out_type=jax.ShapeDtypeStruct((n_out, w), rows.dtype),
        scratch_types=[
            pltpu.VMEM((2, n_dst, batch), jnp.int32),
            pltpu.VMEM((2, batch, w), rows.dtype),
            pltpu.SemaphoreType.DMA((2,)),
            pltpu.SemaphoreType.DMA,
        ],
        name="sc_scatter_rows",
    )
    def scatter(rows_hbm, idx_hbm, out_hbm, idx_v, rows_v, load_sem, store_sem):
        wid = lax.axis_index("s") * SC_CORES + lax.axis_index("c")

        def load_copy(step, slot):
            c = wid * per_worker + step
            return pltpu.make_async_copy(rows_hbm.at[pl.ds(c * batch, batch)], rows_v.at[slot], load_sem.at[slot])

        def load(step, slot):
            pltpu.sync_copy(idx_hbm.at[wid * per_worker + step], idx_v.at[slot])
            load_copy(step, slot).start()

        def store_copy(slot, k):
            return pltpu.make_async_copy(rows_v.at[slot], out_hbm.at[idx_v.at[slot].at[k]], store_sem)

        load(0, 0)

        @pl.loop(0, per_worker, step=2)
        def _(g):
            for slot in range(2):
                step = g + slot

                @pl.when(step + 1 < per_worker)
                def _():
                    load(step + 1, 1 - slot)

                load_copy(step, slot).wait()
                for k in range(n_dst):
                    store_copy(slot, k).start()
                for k in range(n_dst):
                    store_copy(slot, k).wait()

    return scatter(rows, idx3)


def _combine2_kernel(wk_ref, x1_ref, g_ref_rows, wsg_ref, wsu_ref, wsd_ref, g_ref, b_ref, o_ref):
    x1 = x1_ref[...]
    xb = x1.astype(MXU_DTYPE)
    a = (_silu(_dot(xb, wsg_ref[...])) * _dot(xb, wsu_ref[...])).astype(MXU_DTYPE)
    shared = _dot(a, wsd_ref[...])
    wk = wk_ref[...].T
    groups = [wk[:, 0:1] * v for v in _unpack_rows_f32(g_ref_rows[0])]
    for k in range(1, TOP_K):
        groups = [g + wk[:, k:k + 1] * v for g, v in zip(groups, _unpack_rows_f32(g_ref_rows[k]))]
    routed = jnp.concatenate(groups, axis=1)
    o_ref[...] = _layer_norm(ALPHA * x1 + (routed + shared), g_ref[...], b_ref[...])


def _combine2_kernel_into(wk_ref, x1_ref, g_ref_rows, wsg_ref, wsu_ref, wsd_ref, g_ref, b_ref, prev_ref, o_ref):
    del prev_ref
    _combine2_kernel(wk_ref, x1_ref, g_ref_rows, wsg_ref, wsu_ref, wsd_ref, g_ref, b_ref, o_ref)


def _combine2(wk_t, x1, gathered, w_sg, w_su, w_sd, ln_g, ln_b, tc, chunk, prev):
    T, D = x1.shape
    _, t_chunk, W = gathered.shape
    base = chunk * (t_chunk // tc)
    row = lambda i: (base + i, 0)
    c2 = lambda i: (0, 0)
    in_specs = [
        pl.BlockSpec((TOP_K, tc), lambda i: (0, base + i)),
        pl.BlockSpec((tc, D), row),
        pl.BlockSpec((TOP_K, tc, W), lambda i: (0, i, 0)),
        pl.BlockSpec(w_sg.shape, c2),
        pl.BlockSpec(w_su.shape, c2),
        pl.BlockSpec(w_sd.shape, c2),
        pl.BlockSpec((1, D), c2),
        pl.BlockSpec((1, D), c2),
    ]
    args = [wk_t, x1, gathered, w_sg, w_su, w_sd, ln_g, ln_b]
    if prev is None:
        body, aliases = _combine2_kernel, {}
    else:
        body, aliases = _combine2_kernel_into, {len(args): 0}
        in_specs.append(pl.BlockSpec(memory_space=pl.ANY))
        args.append(prev)
    return pl.pallas_call(
        body,
        grid=(t_chunk // tc,),
        in_specs=in_specs,
        out_specs=pl.BlockSpec((tc, D), row),
        out_shape=jax.ShapeDtypeStruct((T, D), jnp.float32),
        input_output_aliases=aliases,
        compiler_params=_cparams(("arbitrary",)),
        name="combine",
    )(*args)


def _split_w_in(w_in):
    o_kv = Q_RANK
    o_ki = o_kv + KV_RANK
    o_iw = o_ki + IDX_DIM
    o_rest = o_iw + N_IDX_HEADS
    w_small = jnp.pad(w_in[:, o_ki:o_rest], ((0, 0), (0, LANES - IDX_DIM - N_IDX_HEADS)))
    return jnp.concatenate([w_in[:, :o_ki], w_in[:, o_rest:], w_small], axis=1).astype(MXU_DTYPE)


def _stages(x, mem, w_in, q_norm_g, kv_norm_g, w_uq, w_uk, w_uv, w_qidx, rel_bias, conv_w, w_mem_k, w_mem_v, w_out, ln1_g, ln1_b, w_router, router_bias, w_e_gate, w_e_up, w_e_down, w_s_gate, w_s_up, w_s_down, ln2_g, ln2_b):
    B, S, D = x.shape
    T = B * S
    bf = MXU_DTYPE
    assert w_in.shape[0] == DEPTH == 1, "single-layer stack"
    l = 0
    res = {}
    x2 = x.reshape(T, D)
    cq, ckv, ckvt, kidx, iwt, yb, yc = _proj(
        x2, mem, _split_w_in(w_in[l]), q_norm_g[l].reshape(1, -1), kv_norm_g[l].reshape(1, -1), conv_w[l],
        w_mem_k[l].astype(bf), w_mem_v[l].astype(bf), B, S, tm=min(1024, S))
    res.update(c_q=cq, c_kv=ckv, k_idx=kidx, y_b=yb, y_c=yc,
               idx_w=jnp.swapaxes(iwt, 1, 2) / (N_IDX_HEADS ** -0.5 * IDX_DIM ** -0.5))
    bias_t = _bias_tiles(rel_bias)
    ya = _dsa(cq, iwt, kidx, ckv, ckvt,
              w_qidx[l].reshape(Q_RANK, -1).astype(bf), w_uq[l].reshape(Q_RANK, -1).astype(bf),
              jnp.transpose(w_uk[l], (1, 0, 2)).astype(bf), jnp.transpose(w_uv[l], (1, 2, 0)).astype(bf),
              bias_t, B, S)
    res.update(y_a=ya)

    x1, x1p, sel_t, w_t, pos_t, cnt = _mix_router(
        x2, ya, yb, yc, w_out[l].astype(bf), ln1_g[l].reshape(1, -1), ln1_b[l].reshape(1, -1),
        w_router[l].T, router_bias[l].reshape(-1, 1), tm=min(1024, T))
    res.update(x1=x1)

    counts = cnt[:, 0].astype(jnp.int32)
    padded = (counts + ROW_BLOCK - 1) // ROW_BLOCK * ROW_BLOCK
    pad_end = jnp.cumsum(padded)
    pad_start = pad_end - padded
    n_blocks = -(-(T * TOP_K) // ROW_BLOCK) + N_EXPERTS
    n_rows = n_blocks * ROW_BLOCK
    block_start = jnp.arange(n_blocks, dtype=jnp.int32) * ROW_BLOCK
    block_e = jnp.minimum(jnp.sum((pad_end[None, :] <= block_start[:, None]).astype(jnp.int32), axis=1),
                          N_EXPERTS - 1)
    n_used = (pad_end[-1:] // ROW_BLOCK).astype(jnp.int32)

    dest_t, wk_t = _compact(sel_t, w_t, pos_t, pad_start.astype(jnp.float32).reshape(-1, 1), tm=min(2048, T))
    block_valid = jnp.clip((pad_start + counts)[block_e] - block_start, 0, ROW_BLOCK).astype(jnp.int32)
    bt = SC_SCATTER_ROWS
    idx3 = jnp.transpose(dest_t.reshape(TOP_K, T // bt, bt), (1, 0, 2))
    xs = _sc_scatter_rows(x1p, idx3, n_rows)
    ys = _experts(xs, block_e, block_valid, n_used, w_e_gate[l], w_e_up[l], w_e_down[l])
    n_chunks = COMBINE_CHUNKS if T % (COMBINE_CHUNKS * 512) == 0 else 1
    t_chunk = T // n_chunks
    out = None
    for c in range(n_chunks):
        idx_c = dest_t[:, c * t_chunk:(c + 1) * t_chunk].reshape(-1)
        gathered = _sc_gather_rows(ys, idx_c).reshape(TOP_K, t_chunk, -1)
        out = _combine2(wk_t, x1, gathered, w_s_gate[l].astype(bf), w_s_up[l].astype(bf), w_s_down[l].astype(bf),
                        ln2_g[l].reshape(1, -1), ln2_b[l].reshape(1, -1), tc=min(512, t_chunk), chunk=c, prev=out)
    res.update(out=out.reshape(B, S, D))
    return res


def kernel(x, mem, w_in, q_norm_g, kv_norm_g, w_uq, w_uk, w_uv, w_qidx, rel_bias, conv_w, w_mem_k, w_mem_v, w_out, ln1_g, ln1_b, w_router, router_bias, w_e_gate, w_e_up, w_e_down, w_s_gate, w_s_up, w_s_down, ln2_g, ln2_b):
    return _stages(x, mem, w_in, q_norm_g, kv_norm_g, w_uq, w_uk, w_uv, w_qidx, rel_bias, conv_w, w_mem_k, w_mem_v, w_out, ln1_g, ln1_b, w_router, router_bias, w_e_gate, w_e_up, w_e_down, w_s_gate, w_s_up, w_s_down, ln2_g, ln2_b)["out"]
```

```python
import functools
import math

import jax
import jax.numpy as jnp
from jax import lax
from jax.experimental import pallas as pl
from jax.experimental.pallas import tpu as pltpu
from jax.experimental.pallas import tpu_sc as plsc

N_HEADS_A = 8
HEAD_DIM = 64
Q_RANK = 256
KV_RANK = 128
N_IDX_HEADS = 8
IDX_DIM = 64
TOPK_MAX = 256
REL_BUCKETS = 32
REL_MAX_DIST = 128
CONV_CH = 256
CONV_WIDTH = 3
N_MEM_HEADS = 4
MIX_A = N_HEADS_A * HEAD_DIM
MIX_C = N_MEM_HEADS * HEAD_DIM
N_EXPERTS = 64
N_GROUPS = 8
GROUP_SIZE = N_EXPERTS // N_GROUPS
TOPK_GROUPS = 4
TOP_K = 8
D_EXPERT = 256
ROUTED_SCALE = 2.5
DEPTH = 1
ALPHA = (2.0 * DEPTH) ** 0.25
LN_EPS = 1e-5
RMS_EPS = 1e-6
LOG2_E = math.log2(math.e)

LANES = 128
SUBLANES = 8
QB = 128
F32_LOWEST = -3.4028234663852886e38
VMEM_LIMIT = 56 * 1024 * 1024
MXU_DTYPE = jnp.bfloat16
ROW_BLOCK = 1024

_NT = (((1,), (1,)), ((), ()))


def _dot(a, b):
    return jnp.dot(a, b, preferred_element_type=jnp.float32)


def _dot_nt(a, b):
    return lax.dot_general(a, b, _NT, preferred_element_type=jnp.float32)


def _cparams(sem):
    return pltpu.CompilerParams(dimension_semantics=sem, vmem_limit_bytes=VMEM_LIMIT)


def _bias_kernel(rb_ref, o_ref):
    s = lax.broadcasted_iota(jnp.int32, (QB, QB), 0)
    t = lax.broadcasted_iota(jnp.int32, (QB, QB), 1)
    max_exact = REL_BUCKETS // 2
    for tile in range(3):
        n = jnp.maximum(t - s + (2 - tile) * QB, 0)
        nf = jnp.maximum(n.astype(jnp.float32), 1.0)
        large = max_exact + (jnp.log(nf / max_exact) / math.log(REL_MAX_DIST / max_exact)
                             * (REL_BUCKETS - max_exact)).astype(jnp.int32)
        large = jnp.minimum(large, REL_BUCKETS - 1)
        bucket = jnp.where(n < max_exact, n, large)
        for h in range(N_HEADS_A):
            acc = jnp.zeros((QB, QB), jnp.float32)
            for b in range(REL_BUCKETS):
                acc = jnp.where(bucket == b, rb_ref[b, h], acc)
            o_ref[tile, h] = acc * LOG2_E


def _bias_tiles(rel_bias):
    return pl.pallas_call(
        _bias_kernel,
        in_specs=[pl.BlockSpec(memory_space=pltpu.SMEM)],
        out_specs=pl.BlockSpec(memory_space=pltpu.VMEM),
        out_shape=jax.ShapeDtypeStruct((3, N_HEADS_A, QB, QB), jnp.float32),
        name="bias_tiles",
    )(rel_bias)


def _proj_kernel(x_ref, mem_ref, wm_ref, qg_ref, kvg_ref, cw_ref, wmk_ref, wmv_ref,
                 cq_ref, ckv_ref, ckvt_ref, kidx_ref, iwt_ref, yb_ref, yc_ref,
                 carry_ref, mk_ref, mv_ref, *, tm):
    si = pl.program_id(1)

    @pl.when(si == 0)
    def _():
        carry_ref[...] = jnp.zeros_like(carry_ref)
        mb = mem_ref[0].astype(MXU_DTYPE)
        mk_ref[...] = _dot(mb, wmk_ref[...]).astype(MXU_DTYPE)
        mv_ref[...] = _dot(mb, wmv_ref[...]).astype(MXU_DTYPE)

    xb = x_ref[...].astype(MXU_DTYPE)
    p = _dot(xb, wm_ref[...])
    small = p[:, p.shape[1] - LANES:]

    o = 0
    cq = p[:, o:o + Q_RANK]; o += Q_RANK
    ckv = p[:, o:o + KV_RANK]; o += KV_RANK
    g_b = p[:, o:o + CONV_CH]; o += CONV_CH
    g_c = p[:, o:o + CONV_CH]; o += CONV_CH
    h_c = p[:, o:o + CONV_CH]; o += CONV_CH
    q_mem = p[:, o:o + MIX_C]

    cq = cq * lax.rsqrt(jnp.mean(cq * cq, axis=-1, keepdims=True) + RMS_EPS) * qg_ref[...]
    ckv = ckv * lax.rsqrt(jnp.mean(ckv * ckv, axis=-1, keepdims=True) + RMS_EPS) * kvg_ref[...]
    cq_ref[...] = cq.astype(MXU_DTYPE)
    ckv_b = ckv.astype(MXU_DTYPE)
    ckv_ref[...] = ckv_b
    ckvt_ref[0] = ckv.T.astype(MXU_DTYPE)

    kidx_ref[...] = small[:, :IDX_DIM].astype(MXU_DTYPE)
    small_t = small.T
    iwt_ref[0] = small_t[IDX_DIM:IDX_DIM + N_IDX_HEADS, :] * (N_IDX_HEADS ** -0.5 * IDX_DIM ** -0.5)

    u = g_c * h_c
    rows = lax.broadcasted_iota(jnp.int32, (tm, 1), 0)
    c6 = carry_ref[SUBLANES - 2:SUBLANES - 1, :]
    c7 = carry_ref[SUBLANES - 1:SUBLANES, :]
    u1 = jnp.where(rows == 0, c7, pltpu.roll(u, 1, 0))
    u2 = jnp.where(rows == 0, c6, jnp.where(rows == 1, c7, pltpu.roll(u, 2, 0)))
    y = cw_ref[0:1, :] * u2
    y = y + cw_ref[1:2, :] * u1
    y = y + cw_ref[2:3, :] * u
    yb_ref[...] = (g_b * y).astype(MXU_DTYPE)
    carry_ref[...] = u[tm - SUBLANES:, :]

    qm = q_mem.astype(MXU_DTYPE)
    outs = []
    for h in range(N_MEM_HEADS):
        sl = slice(h * HEAD_DIM, (h + 1) * HEAD_DIM)
        lg = _dot_nt(qm[:, sl], mk_ref[:, sl]) * (HEAD_DIM ** -0.5)
        lg = lg - jnp.max(lg, axis=-1, keepdims=True)
        e = jnp.exp(lg)
        pr = e / jnp.sum(e, axis=-1, keepdims=True)
        outs.append(_dot(pr.astype(MXU_DTYPE), mv_ref[:, sl]))
    yc_ref[...] = jnp.concatenate(outs, axis=-1).astype(MXU_DTYPE)


def _proj(x2, mem, w_main, q_g, kv_g, conv_w, w_mk, w_mv, B, S, tm):
    T, D = x2.shape
    n_mem = mem.shape[1]
    ns = S // tm
    row = lambda b, s: (b * ns + s, 0)
    const2 = lambda b, s: (0, 0)
    bf = MXU_DTYPE
    return pl.pallas_call(
        functools.partial(_proj_kernel, tm=tm),
        grid=(B, ns),
        in_specs=[
            pl.BlockSpec((tm, D), row),
            pl.BlockSpec((1, n_mem, D), lambda b, s: (b, 0, 0)),
            pl.BlockSpec(w_main.shape, const2),
            pl.BlockSpec(q_g.shape, const2),
            pl.BlockSpec(kv_g.shape, const2),
            pl.BlockSpec(conv_w.shape, const2),
            pl.BlockSpec(w_mk.shape, const2),
            pl.BlockSpec(w_mv.shape, const2),
        ],
        out_specs=[
            pl.BlockSpec((tm, Q_RANK), row),
            pl.BlockSpec((tm, KV_RANK), row),
            pl.BlockSpec((1, KV_RANK, tm), lambda b, s: (b, 0, s)),
            pl.BlockSpec((tm, IDX_DIM), row),
            pl.BlockSpec((1, N_IDX_HEADS, tm), lambda b, s: (b, 0, s)),
            pl.BlockSpec((tm, CONV_CH), row),
            pl.BlockSpec((tm, MIX_C), row),
        ],
        out_shape=[
            jax.ShapeDtypeStruct((T, Q_RANK), bf),
            jax.ShapeDtypeStruct((T, KV_RANK), bf),
            jax.ShapeDtypeStruct((B, KV_RANK, S), bf),
            jax.ShapeDtypeStruct((T, IDX_DIM), bf),
            jax.ShapeDtypeStruct((B, N_IDX_HEADS, S), jnp.float32),
            jax.ShapeDtypeStruct((T, CONV_CH), bf),
            jax.ShapeDtypeStruct((T, MIX_C), bf),
        ],
        scratch_shapes=[
            pltpu.VMEM((SUBLANES, CONV_CH), jnp.float32),
            pltpu.VMEM((n_mem, MIX_C), bf),
            pltpu.VMEM((n_mem, MIX_C), bf),
        ],
        compiler_params=_cparams(("arbitrary", "arbitrary")),
        name="proj",
    )(x2, mem, w_main, q_g, kv_g, conv_w, w_mk, w_mv)


def _key_to_f32(key):
    bits = jnp.where(key < 0, key ^ jnp.int32(0x7FFFFFFF), key)
    return pltpu.bitcast(bits, jnp.float32)


def _colsum8(v):
    return jnp.sum(v.reshape(QB // SUBLANES, SUBLANES, QB), axis=0)


def _colmax8(v):
    return jnp.max(v.reshape(QB // SUBLANES, SUBLANES, QB), axis=0)


UNROLL_WIDTHS = (8, 4, 2, 1)


def _dsa_kernel(cq_ref, iwt_ref, kidx_ref, ckv_ref, ckvt_ref, wqi_ref, wuq_ref, wuk_ref, wuvt_ref,
                bias_ref, o_ref, wfold_ref, qidx_ref, qlat_ref, score_ref, logit_ref, acc_ref,
                *, k_sel, idx_bits):
    i = pl.program_id(1)
    f32 = jnp.float32
    bf = MXU_DTYPE
    n_blocks = i + 1
    n_blocks = n_blocks + jnp.where((n_blocks % 4 == 3) & (n_blocks < pl.num_programs(1)), 1, 0)
    s_loc = lax.broadcasted_iota(jnp.int32, (QB, QB), 0)
    t_glob = i * QB + lax.broadcasted_iota(jnp.int32, (QB, QB), 1)

    def blk(jb):
        return pl.multiple_of(jb * QB, QB)

    def block_loop(fn, init):
        c, start = init, 0
        for width in UNROLL_WIDTHS:
            n = (n_blocks - start) // width
            c = lax.fori_loop(0, n, lambda it, c, w=width, s=start: fn(s + it * w, w, c), c)
            start = start + n * width
        return c

    @pl.when(i == 0)
    def _():
        for h in range(N_HEADS_A):
            wfold_ref[:, h * KV_RANK:(h + 1) * KV_RANK] = (
                _dot_nt(wuq_ref[:, h * HEAD_DIM:(h + 1) * HEAD_DIM], wuk_ref[h])
                * (HEAD_DIM ** -0.5 * LOG2_E)).astype(bf)

    cq = cq_ref[...]
    q_idx = _dot(cq, wqi_ref[...]).astype(bf)
    q_lat = _dot(cq, wfold_ref[...]).astype(bf)
    for h in range(N_HEADS_A):
        qidx_ref[h * QB:(h + 1) * QB, :] = q_idx[:, h * IDX_DIM:(h + 1) * IDX_DIM]
        qlat_ref[h * QB:(h + 1) * QB, :] = q_lat[:, h * KV_RANK:(h + 1) * KV_RANK]
    iw = iwt_ref[0]

    def score_body(jb0, nb, n_pos8):
        d_blk = _dot_nt(kidx_ref[pl.ds(blk(jb0), nb * QB), :], qidx_ref[...])
        for sb in range(nb):
            off = blk(jb0 + sb)
            d_all = d_blk[sb * QB:(sb + 1) * QB, :]
            acc = jnp.maximum(d_all[:, 0:QB], 0.0) * iw[0:1, :]
            for h in range(1, N_IDX_HEADS):
                acc = acc + jnp.maximum(d_all[:, h * QB:(h + 1) * QB], 0.0) * iw[h:h + 1, :]
            sc = jnp.where(s_loc + off <= t_glob, acc + 0.0, F32_LOWEST)
            score_ref[pl.ds(off, QB), :] = sc
            n_pos8 = n_pos8 + _colsum8(jnp.where(sc >= 0.0, 1.0, 0.0))
        return n_pos8

    n_pos8 = block_loop(score_body, jnp.zeros((SUBLANES, QB), f32))

    def count_where(pred):
        def body(jb0, nb, acc):
            for sb in range(nb):
                off = blk(jb0 + sb)
                acc = acc + _colsum8(jnp.where(pred(score_ref[pl.ds(off, QB), :], off), 1.0, 0.0))
            return acc
        acc = block_loop(body, jnp.zeros((SUBLANES, QB), f32))
        return jnp.sum(acc, axis=0, keepdims=True)

    kf = float(k_sel)

    def search():
        c0 = jnp.sum(n_pos8, axis=0, keepdims=True)
        cand0 = jnp.where(c0 >= kf, jnp.int32(0), jnp.int32(-2 ** 31))
        n_ge0 = jnp.where(c0 >= kf, c0, -1.0)

        def bit_body(it, carry):
            cand, n_ge = carry
            trial = cand + lax.shift_left(jnp.int32(1), 30 - it)
            tf = _key_to_f32(trial)
            cnt = count_where(lambda sc, off: sc >= tf)
            take = cnt >= kf
            return jnp.where(take, trial, cand), jnp.where(take, cnt, n_ge)

        cand, n_ge = lax.fori_loop(0, 31, bit_body, (cand0, n_ge0))
        thr = _key_to_f32(cand)
        keep_all_ties = jnp.full((1, QB), 2 ** idx_bits - 1, jnp.int32)

        def resolve_ties():
            n_gt = count_where(lambda sc, off: sc > thr)
            n_eq = count_where(lambda sc, off: sc == thr)
            need = kf - n_gt

            def tie_search():
                def tbody(it, xcut):
                    trial = xcut + lax.shift_left(jnp.int32(1), idx_bits - 1 - it)
                    cnt = count_where(lambda sc, off: (sc == thr) & (s_loc + off < trial))
                    return jnp.where(cnt < need, trial, xcut)
                return lax.fori_loop(0, idx_bits, tbody, jnp.zeros((1, QB), jnp.int32))

            return lax.cond(jnp.max(n_eq - need) > 0.0, tie_search, lambda: keep_all_ties)

        xcut = lax.cond(jnp.max(jnp.abs(n_ge - kf)) > 0.0, resolve_ties, lambda: keep_all_ties)
        return thr, xcut

    def no_search():
        return jnp.full((1, QB), F32_LOWEST, f32), jnp.full((1, QB), 2 ** idx_bits - 1, jnp.int32)

    thr, xcut = lax.cond((i + 1) * QB > k_sel, search, no_search)

    def selection_mask(off):
        sc = score_ref[pl.ds(off, QB), :]
        s_glob = s_loc + off
        keep = ((sc > thr) | ((sc == thr) & (s_glob <= xcut))) & (s_glob <= t_glob)
        return jnp.where(keep, 0.0, -jnp.inf)

    acc_ref[...] = jnp.zeros_like(acc_ref)

    def att_body(jb0, nb, carry):
        m, l8 = list(carry[0]), list(carry[1])
        rows = nb * QB
        lg_blk = _dot_nt(ckv_ref[pl.ds(blk(jb0), rows), :], qlat_ref[...])
        blk_max = [None] * N_HEADS_A
        for sb in range(nb):
            off = blk(jb0 + sb)
            msk = selection_mask(off)
            bsel = jnp.clip(jb0 + sb - i + 2, 0, 2)
            for h in range(N_HEADS_A):
                lgh = lg_blk[sb * QB:(sb + 1) * QB, h * QB:(h + 1) * QB] + bias_ref[bsel, h] + msk
                logit_ref[sb * QB:(sb + 1) * QB, h * QB:(h + 1) * QB] = lgh
                cm = _colmax8(lgh)
                blk_max[h] = cm if blk_max[h] is None else jnp.maximum(blk_max[h], cm)
        ps, scales = [], []
        for h in range(N_HEADS_A):
            m_new = jnp.maximum(m[h], jnp.max(blk_max[h], axis=0, keepdims=True))
            m_ref = jnp.where(m_new == -jnp.inf, 0.0, m_new)
            p = jnp.exp2(logit_ref[0:rows, h * QB:(h + 1) * QB] - m_ref)
            scale = jnp.exp2(m[h] - m_ref)
            l8[h] = l8[h] * scale + jnp.sum(p.reshape(rows // SUBLANES, SUBLANES, QB), axis=0)
            m[h] = m_new
            ps.append(p.astype(bf))
            scales.append(scale)
        pv = _dot(ckvt_ref[0, :, pl.ds(blk(jb0), rows)], jnp.concatenate(ps, axis=1))
        for h in range(N_HEADS_A):
            hs = slice(h * QB, (h + 1) * QB)
            acc_ref[:, hs] = acc_ref[:, hs] * scales[h] + pv[:, hs]
        return tuple(m), tuple(l8)

    _, l8 = block_loop(att_body, (tuple(jnp.full((1, QB), -jnp.inf, f32) for _ in range(N_HEADS_A)),
                                  tuple(jnp.zeros((SUBLANES, QB), f32) for _ in range(N_HEADS_A))))

    outs = []
    for h in range(N_HEADS_A):
        l_row = jnp.sum(l8[h], axis=0, keepdims=True)
        o_lat_t = (acc_ref[:, h * QB:(h + 1) * QB] / l_row).astype(bf)
        outs.append(_dot(wuvt_ref[h], o_lat_t))
    o_ref[...] = jnp.concatenate(outs, axis=0).T.astype(o_ref.dtype)


def _dsa(cq, iwt, kidx, ckv, ckvt, w_qidx, w_uq, w_uk_h, w_uvt_h, bias_tiles, B, S):
    T = cq.shape[0]
    assert S % QB == 0 and QB >= REL_MAX_DIST
    nq = S // QB
    k_sel = min(TOPK_MAX, S // 4)
    idx_bits = max(1, (S - 1).bit_length())
    c2 = lambda b, i: (0, 0)
    c3 = lambda b, i: (0, 0, 0)
    return pl.pallas_call(
        functools.partial(_dsa_kernel, k_sel=k_sel, idx_bits=idx_bits),
        grid=(B, nq),
        in_specs=[
            pl.BlockSpec((QB, Q_RANK), lambda b, i: (b * nq + i, 0)),
            pl.BlockSpec((1, N_IDX_HEADS, QB), lambda b, i: (b, 0, i)),
            pl.BlockSpec((S, IDX_DIM), lambda b, i: (b, 0)),
            pl.BlockSpec((S, KV_RANK), lambda b, i: (b, 0)),
            pl.BlockSpec((1, KV_RANK, S), lambda b, i: (b, 0, 0)),
            pl.BlockSpec(w_qidx.shape, c2),
            pl.BlockSpec(w_uq.shape, c2),
            pl.BlockSpec(w_uk_h.shape, c3),
            pl.BlockSpec(w_uvt_h.shape, c3),
            pl.BlockSpec(bias_tiles.shape, lambda b, i: (0, 0, 0, 0)),
        ],
        out_specs=pl.BlockSpec((QB, MIX_A), lambda b, i: (b * nq + i, 0)),
        out_shape=jax.ShapeDtypeStruct((T, MIX_A), MXU_DTYPE),
        scratch_shapes=[
            pltpu.VMEM((Q_RANK, N_HEADS_A * KV_RANK), MXU_DTYPE),
            pltpu.VMEM((N_IDX_HEADS * QB, IDX_DIM), MXU_DTYPE),
            pltpu.VMEM((N_HEADS_A * QB, KV_RANK), MXU_DTYPE),
            pltpu.VMEM((S, QB), jnp.float32),
            pltpu.VMEM((max(UNROLL_WIDTHS) * QB, N_HEADS_A * QB), jnp.float32),
            pltpu.VMEM((KV_RANK, N_HEADS_A * QB), jnp.float32),
        ],
        compiler_params=_cparams(("arbitrary", "arbitrary")),
        name="dsa",
    )(cq, iwt, kidx, ckv, ckvt, w_qidx, w_uq, w_uk_h, w_uvt_h, bias_tiles)


def _layer_norm(xf, g, b):
    mu = jnp.mean(xf, axis=-1, keepdims=True)
    xc = xf - mu
    var = jnp.mean(xc * xc, axis=-1, keepdims=True)
    return xc * lax.rsqrt(var + LN_EPS) * g + b


def _rank_rows(v, n):
    ri = lax.broadcasted_iota(jnp.int32, v.shape, 0)
    rank = jnp.zeros(v.shape, jnp.float32)
    for r2 in range(n):
        row = v[r2:r2 + 1, :]
        beats = (row > v) | ((row == v) & (ri > r2))
        rank = rank + jnp.where(beats, 1.0, 0.0)
    return rank


def _top_rows(v, k):
    n = v.shape[0]
    ri = lax.broadcasted_iota(jnp.int32, v.shape, 0)
    sel = jnp.zeros(v.shape, jnp.float32)
    for _ in range(k):
        m = jnp.max(v, axis=0, keepdims=True)
        first = jnp.min(jnp.where(v == m, ri, n), axis=0, keepdims=True)
        pick = ri == first
        sel = jnp.where(pick, 1.0, sel)
        v = jnp.where(pick, -jnp.inf, v)
    return sel > 0.5


def _pack_factor():
    return 4 // jnp.dtype(MXU_DTYPE).itemsize


def _pack_rows(x):
    if _pack_factor() == 1:
        return pltpu.bitcast(x, jnp.int32)
    half = x.shape[1] // 2
    b = pltpu.bitcast(x.astype(MXU_DTYPE).astype(jnp.float32), jnp.int32)
    return b[:, half:] | lax.shift_right_logical(b[:, :half], jnp.int32(16))


_HIGH_HALF = -(1 << 16)


def _unpack_rows_f32(p):
    if _pack_factor() == 1:
        return [pltpu.bitcast(p, jnp.float32)]
    lo = pltpu.bitcast(lax.shift_left(p, jnp.int32(16)), jnp.float32)
    hi = pltpu.bitcast(p & jnp.int32(_HIGH_HALF), jnp.float32)
    return [lo, hi]


def _unpack_rows(p):
    return [v.astype(MXU_DTYPE) for v in _unpack_rows_f32(p)]


def _mix_router_kernel(x_ref, ya_ref, yb_ref, yc_ref, wo_ref, g_ref, b_ref, wrt_ref, rb_ref, exp_ref,
                       x1_ref, x1p_ref, sel_ref, w_ref, pos_ref, cnt_ref, base_ref, *, tm):
    step = pl.program_id(0)
    f32 = jnp.float32

    @pl.when(step == 0)
    def _():
        base_ref[...] = jnp.zeros_like(base_ref)

    mix = _dot(ya_ref[...], wo_ref[0:MIX_A, :])
    mix = mix + _dot(yb_ref[...], wo_ref[MIX_A:MIX_A + CONV_CH, :])
    mix = mix + _dot(yc_ref[...], wo_ref[MIX_A + CONV_CH:, :])
    x1 = _layer_norm(ALPHA * x_ref[...] + mix, g_ref[...], b_ref[...])
    x1_ref[...] = x1
    x1p_ref[...] = _pack_rows(x1)

    lg = lax.dot_general(wrt_ref[...], x1, _NT, precision=lax.Precision.HIGHEST, preferred_element_type=f32)
    s = 1.0 / (1.0 + jnp.exp(-lg))
    sc = s + rb_ref[...]

    g3 = sc.reshape(N_GROUPS, GROUP_SIZE, tm)
    m1 = jnp.max(g3, axis=1, keepdims=True)
    is_m1 = g3 == m1
    n_m1 = jnp.sum(jnp.where(is_m1, 1.0, 0.0), axis=1, keepdims=True)
    m2 = jnp.max(jnp.where(is_m1, -jnp.inf, g3), axis=1, keepdims=True)
    gscore = (m1 + jnp.where(n_m1 > 1.0, m1, m2)).reshape(N_GROUPS, tm)
    gsel = jnp.where(_rank_rows(gscore, N_GROUPS) < float(TOPK_GROUPS), 1.0, 0.0)
    emask = _dot(exp_ref[...], gsel.astype(MXU_DTYPE)) > 0.5
    masked = jnp.where(emask, sc, -jnp.inf)
    sel = _top_rows(masked, TOP_K) & emask
    self_ = jnp.where(sel, 1.0, 0.0)
    top_s = jnp.where(sel, s, 0.0)
    w = top_s / jnp.sum(top_s, axis=0, keepdims=True) * ROUTED_SCALE

    t_r = lax.broadcasted_iota(jnp.int32, (tm, tm), 0)
    t_c = lax.broadcasted_iota(jnp.int32, (tm, tm), 1)
    upper = jnp.where(t_r < t_c, 1.0, 0.0).astype(MXU_DTYPE)
    pref = _dot(self_.astype(MXU_DTYPE), upper)
    base = base_ref[...]
    sel_ref[...] = self_
    w_ref[...] = w
    pos_ref[...] = base + pref
    base = base + jnp.sum(self_, axis=1, keepdims=True)
    base_ref[...] = base
    cnt_ref[...] = jnp.broadcast_to(base, cnt_ref.shape)


def _mix_router(x2, ya, yb, yc, w_out, ln_g, ln_b, w_router_t, router_bias, tm):
    T, D = x2.shape
    E = N_EXPERTS
    expand = (jnp.arange(E)[:, None] // GROUP_SIZE == jnp.arange(N_GROUPS)[None, :]).astype(MXU_DTYPE)
    row = lambda i: (i, 0)
    col = lambda i: (0, i)
    c2 = lambda i: (0, 0)
    f32 = jnp.float32
    return pl.pallas_call(
        functools.partial(_mix_router_kernel, tm=tm),
        grid=(T // tm,),
        in_specs=[
            pl.BlockSpec((tm, D), row),
            pl.BlockSpec((tm, MIX_A), row),
            pl.BlockSpec((tm, CONV_CH), row),
            pl.BlockSpec((tm, MIX_C), row),
            pl.BlockSpec(w_out.shape, c2),
            pl.BlockSpec((1, D), c2),
            pl.BlockSpec((1, D), c2),
            pl.BlockSpec((E, D), c2),
            pl.BlockSpec((E, 1), c2),
            pl.BlockSpec((E, N_GROUPS), c2),
        ],
        out_specs=[
            pl.BlockSpec((tm, D), row),
            pl.BlockSpec((tm, D // _pack_factor()), row),
            pl.BlockSpec((E, tm), col),
            pl.BlockSpec((E, tm), col),
            pl.BlockSpec((E, tm), col),
            pl.BlockSpec((E, LANES), c2),
        ],
        out_shape=[
            jax.ShapeDtypeStruct((T, D), f32),
            jax.ShapeDtypeStruct((T, D // _pack_factor()), jnp.int32),
            jax.ShapeDtypeStruct((E, T), f32),
            jax.ShapeDtypeStruct((E, T), f32),
            jax.ShapeDtypeStruct((E, T), f32),
            jax.ShapeDtypeStruct((E, LANES), f32),
        ],
        scratch_shapes=[pltpu.VMEM((E, 1), f32)],
        compiler_params=_cparams(("arbitrary",)),
        name="mix_router",
    )(x2, ya, yb, yc, w_out, ln_g, ln_b, w_router_t, router_bias, expand)


def _compact_kernel(sel_ref, w_ref, pos_ref, pstart_ref, low_ref, dest_ref, wk_ref):
    sel = sel_ref[...]
    on = sel > 0.5
    rank = _dot(low_ref[...], sel.astype(MXU_DTYPE))
    row = pstart_ref[...] + pos_ref[...]
    w = w_ref[...]
    dests, ws = [], []
    for k in range(TOP_K):
        m = on & (rank == float(k))
        dests.append(jnp.sum(jnp.where(m, row, 0.0), axis=0, keepdims=True))
        ws.append(jnp.sum(jnp.where(m, w, 0.0), axis=0, keepdims=True))
    dest_ref[...] = jnp.concatenate(dests, axis=0).astype(jnp.int32)
    wk_ref[...] = jnp.concatenate(ws, axis=0)


def _compact(sel_t, w_t, pos_t, pad_start, tm):
    E, T = sel_t.shape
    lower = (jnp.arange(E)[None, :] < jnp.arange(E)[:, None]).astype(MXU_DTYPE)
    col = lambda i: (0, i)
    c2 = lambda i: (0, 0)
    return pl.pallas_call(
        _compact_kernel,
        grid=(T // tm,),
        in_specs=[pl.BlockSpec((E, tm), col), pl.BlockSpec((E, tm), col), pl.BlockSpec((E, tm), col),
                  pl.BlockSpec((E, 1), c2), pl.BlockSpec((E, E), c2)],
        out_specs=[pl.BlockSpec((TOP_K, tm), col), pl.BlockSpec((TOP_K, tm), col)],
        out_shape=[jax.ShapeDtypeStruct((TOP_K, T), jnp.int32), jax.ShapeDtypeStruct((TOP_K, T), jnp.float32)],
        compiler_params=_cparams(("arbitrary",)),
        name="route_compact",
    )(sel_t, w_t, pos_t, pad_start, lower)


def _silu(g):
    return g / (1.0 + jnp.exp(-g))


def _expert_kernel(be_ref, nv_ref, nu_ref, xs_ref, wg_ref, wu_ref, wd_ref, ys_ref, wgb_ref, wub_ref, wdb_ref):
    i = pl.program_id(0)

    @pl.when((i == 0) | (be_ref[i] != be_ref[jnp.maximum(i - 1, 0)]))
    def _():
        wgb_ref[...] = wg_ref[0].astype(MXU_DTYPE)
        wub_ref[...] = wu_ref[0].astype(MXU_DTYPE)
        wdb_ref[...] = wd_ref[0].astype(MXU_DTYPE)

    @pl.when(i < nu_ref[0])
    def _():
        live = lax.broadcasted_iota(jnp.int32, (ROW_BLOCK, 1), 0) < nv_ref[i]
        parts = [jnp.where(live, v, jnp.zeros_like(v)) for v in _unpack_rows(xs_ref[...])]
        dk = wgb_ref.shape[0] // len(parts)

        def proj(w_ref):
            acc = _dot(parts[0], w_ref[0:dk, :])
            for n in range(1, len(parts)):
                acc = acc + _dot(parts[n], w_ref[n * dk:(n + 1) * dk, :])
            return acc

        a = (_silu(proj(wgb_ref)) * proj(wub_ref)).astype(MXU_DTYPE)
        ys_ref[...] = _pack_rows(_dot(a, wdb_ref[...]))


def _experts(xs, block_e, block_valid, n_used, w_gate, w_up, w_down):
    n_rows, W = xs.shape
    D = w_gate.shape[1]
    n_blocks = n_rows // ROW_BLOCK
    blk = lambda i, be, nv, nu: (jnp.minimum(i, nu[0] - 1), 0)
    wsel = lambda i, be, nv, nu: (be[i], 0, 0)
    return pl.pallas_call(
        _expert_kernel,
        grid_spec=pltpu.PrefetchScalarGridSpec(
            num_scalar_prefetch=3,
            grid=(n_blocks,),
            in_specs=[
                pl.BlockSpec((ROW_BLOCK, W), blk),
                pl.BlockSpec((1, D, D_EXPERT), wsel),
                pl.BlockSpec((1, D, D_EXPERT), wsel),
                pl.BlockSpec((1, D_EXPERT, D), wsel),
            ],
            out_specs=pl.BlockSpec((ROW_BLOCK, W), blk),
            scratch_shapes=[pltpu.VMEM((D, D_EXPERT), MXU_DTYPE), pltpu.VMEM((D, D_EXPERT), MXU_DTYPE),
                            pltpu.VMEM((D_EXPERT, D), MXU_DTYPE)],
        ),
        out_shape=jax.ShapeDtypeStruct((n_rows, W), xs.dtype),
        compiler_params=_cparams(("arbitrary",)),
        name="experts",
    )(block_e, block_valid, n_used, xs, w_gate, w_up, w_down)


SC_CORES = 2
SC_SUBCORES = 16
SC_GATHER_ROWS = 64
COMBINE_CHUNKS = 8


def _sc_gather_rows(table, idx):
    n = idx.shape[0]
    w = table.shape[1]
    n_workers = SC_CORES * SC_SUBCORES
    per_worker = n // n_workers
    assert n % n_workers == 0 and per_worker % SC_GATHER_ROWS == 0
    mesh = plsc.VectorSubcoreMesh(core_axis_name="c", subcore_axis_name="s")

    @functools.partial(
        pl.kernel, mesh=mesh,
        out_type=jax.ShapeDtypeStruct((n, w), table.dtype),
        scratch_types=[
            pltpu.VMEM((2, SC_GATHER_ROWS), jnp.int32),
            pltpu.VMEM((2, SC_GATHER_ROWS, w), table.dtype),
            pltpu.SemaphoreType.DMA((2,)),
        ],
        name="sc_gather_rows",
    )
    def gather(table_hbm, idx_hbm, out_hbm, idx_v, rows_v, sem):
        wid = lax.axis_index("s") * SC_CORES + lax.axis_index("c")
        base = wid * per_worker
        n_steps = per_worker // SC_GATHER_ROWS

        def gather_copy(slot):
            return pltpu.make_async_copy(table_hbm.at[idx_v.at[slot]], rows_v.at[slot], sem.at[slot])

        def start(step, slot):
            pltpu.sync_copy(idx_hbm.at[pl.ds(base + step * SC_GATHER_ROWS, SC_GATHER_ROWS)], idx_v.at[slot])
            gather_copy(slot).start()

        start(0, 0)

        @pl.loop(0, n_steps, step=2)
        def _(g):
            for slot in range(2):
                step = g + slot

                @pl.when(step + 1 < n_steps)
                def _():
                    start(step + 1, 1 - slot)

                gather_copy(slot).wait()
                pltpu.sync_copy(rows_v.at[slot], out_hbm.at[pl.ds(base + step * SC_GATHER_ROWS, SC_GATHER_ROWS)])

    return gather(table, idx)


SC_SCATTER_ROWS = 64


def _sc_scatter_rows(rows, idx3, n_out):
    n_src, w = rows.shape
    n_chunks, n_dst, batch = idx3.shape
    n_workers = SC_CORES * SC_SUBCORES
    assert batch == SC_SCATTER_ROWS and n_chunks * batch == n_src and n_chunks % (2 * n_workers) == 0
    per_worker = n_chunks // n_workers
    mesh = plsc.VectorSubcoreMesh(core_axis_name="c", subcore_axis_name="s")

    @functools.partial(
        pl.kernel, mesh=mesh,
        out_type=jax.ShapeDtypeStruct((n_out, w), rows.dtype),
        scratch_types=[
            pltpu.VMEM((2, n_dst, batch), jnp.int32),
            pltpu.VMEM((2, batch, w), rows.dtype),
            pltpu.SemaphoreType.DMA((2,)),
            pltpu.SemaphoreType.DMA,
        ],
        name="sc_scatter_rows",
    )
    def scatter(rows_hbm, idx_hbm, out_hbm, idx_v, rows_v, load_sem, store_sem):
        wid = lax.axis_index("s") * SC_CORES + lax.axis_index("c")

        def load_copy(step, slot):
            c = wid * per_worker + step
            return pltpu.make_async_copy(rows_hbm.at[pl.ds(c * batch, batch)], rows_v.at[slot], load_sem.at[slot])

        def load(step, slot):
            pltpu.sync_copy(idx_hbm.at[wid * per_worker + step], idx_v.at[slot])
            load_copy(step, slot).start()

        def store_copy(slot, k):
            return pltpu.make_async_copy(rows_v.at[slot], out_hbm.at[idx_v.at[slot].at[k]], store_sem)

        load(0, 0)

        @pl.loop(0, per_worker, step=2)
        def _(g):
            for slot in range(2):
                step = g + slot

                @pl.when(step + 1 < per_worker)
                def _():
                    load(step + 1, 1 - slot)

                load_copy(step, slot).wait()
                for k in range(n_dst):
                    store_copy(slot, k).start()
                for k in range(n_dst):
                    store_copy(slot, k).wait()

    return scatter(rows, idx3)


def _combine2_kernel(wk_ref, x1_ref, g_ref_rows, wsg_ref, wsu_ref, wsd_ref, g_ref, b_ref, o_ref):
    x1 = x1_ref[...]
    xb = x1.astype(MXU_DTYPE)
    a = (_silu(_dot(xb, wsg_ref[...])) * _dot(xb, wsu_ref[...])).astype(MXU_DTYPE)
    shared = _dot(a, wsd_ref[...])
    wk = wk_ref[...].T
    groups = [wk[:, 0:1] * v for v in _unpack_rows_f32(g_ref_rows[0])]
    for k in range(1, TOP_K):
        groups = [g + wk[:, k:k + 1] * v for g, v in zip(groups, _unpack_rows_f32(g_ref_rows[k]))]
    routed = jnp.concatenate(groups, axis=1)
    o_ref[...] = _layer_norm(ALPHA * x1 + (routed + shared), g_ref[...], b_ref[...])


def _combine2_kernel_into(wk_ref, x1_ref, g_ref_rows, wsg_ref, wsu_ref, wsd_ref, g_ref, b_ref, prev_ref, o_ref):
    del prev_ref
    _combine2_kernel(wk_ref, x1_ref, g_ref_rows, wsg_ref, wsu_ref, wsd_ref, g_ref, b_ref, o_ref)


def _combine2(wk_t, x1, gathered, w_sg, w_su, w_sd, ln_g, ln_b, tc, chunk, prev):
    T, D = x1.shape
    _, t_chunk, W = gathered.shape
    base = chunk * (t_chunk // tc)
    row = lambda i: (base + i, 0)
    c2 = lambda i: (0, 0)
    in_specs = [
        pl.BlockSpec((TOP_K, tc), lambda i: (0, base + i)),
        pl.BlockSpec((tc, D), row),
        pl.BlockSpec((TOP_K, tc, W), lambda i: (0, i, 0)),
        pl.BlockSpec(w_sg.shape, c2),
        pl.BlockSpec(w_su.shape, c2),
        pl.BlockSpec(w_sd.shape, c2),
        pl.BlockSpec((1, D), c2),
        pl.BlockSpec((1, D), c2),
    ]
    args = [wk_t, x1, gathered, w_sg, w_su, w_sd, ln_g, ln_b]
    if prev is None:
        body, aliases = _combine2_kernel, {}
    else:
        body, aliases = _combine2_kernel_into, {len(args): 0}
        in_specs.append(pl.BlockSpec(memory_space=pl.ANY))
        args.append(prev)
    return pl.pallas_call(
        body,
        grid=(t_chunk // tc,),
        in_specs=in_specs,
        out_specs=pl.BlockSpec((tc, D), row),
        out_shape=jax.ShapeDtypeStruct((T, D), jnp.float32),
        input_output_aliases=aliases,
        compiler_params=_cparams(("arbitrary",)),
        name="combine",
    )(*args)


def _split_w_in(w_in):
    o_kv = Q_RANK
    o_ki = o_kv + KV_RANK
    o_iw = o_ki + IDX_DIM
    o_rest = o_iw + N_IDX_HEADS
    w_small = jnp.pad(w_in[:, o_ki:o_rest], ((0, 0), (0, LANES - IDX_DIM - N_IDX_HEADS)))
    return jnp.concatenate([w_in[:, :o_ki], w_in[:, o_rest:], w_small], axis=1).astype(MXU_DTYPE)


def _stages(x, mem, w_in, q_norm_g, kv_norm_g, w_uq, w_uk, w_uv, w_qidx, rel_bias, conv_w, w_mem_k, w_mem_v, w_out, ln1_g, ln1_b, w_router, router_bias, w_e_gate, w_e_up, w_e_down, w_s_gate, w_s_up, w_s_down, ln2_g, ln2_b):
    B, S, D = x.shape
    T = B * S
    bf = MXU_DTYPE
    assert w_in.shape[0] == DEPTH == 1, "single-layer stack"
    l = 0
    res = {}
    x2 = x.reshape(T, D)
    cq, ckv, ckvt, kidx, iwt, yb, yc = _proj(
        x2, mem, _split_w_in(w_in[l]), q_norm_g[l].reshape(1, -1), kv_norm_g[l].reshape(1, -1), conv_w[l],
        w_mem_k[l].astype(bf), w_mem_v[l].astype(bf), B, S, tm=min(1024, S))
    res.update(c_q=cq, c_kv=ckv, k_idx=kidx, y_b=yb, y_c=yc,
               idx_w=jnp.swapaxes(iwt, 1, 2) / (N_IDX_HEADS ** -0.5 * IDX_DIM ** -0.5))
    bias_t = _bias_tiles(rel_bias)
    ya = _dsa(cq, iwt, kidx, ckv, ckvt,
              w_qidx[l].reshape(Q_RANK, -1).astype(bf), w_uq[l].reshape(Q_RANK, -1).astype(bf),
              jnp.transpose(w_uk[l], (1, 0, 2)).astype(bf), jnp.transpose(w_uv[l], (1, 2, 0)).astype(bf),
              bias_t, B, S)
    res.update(y_a=ya)

    x1, x1p, sel_t, w_t, pos_t, cnt = _mix_router(
        x2, ya, yb, yc, w_out[l].astype(bf), ln1_g[l].reshape(1, -1), ln1_b[l].reshape(1, -1),
        w_router[l].T, router_bias[l].reshape(-1, 1), tm=min(1024, T))
    res.update(x1=x1)

    counts = cnt[:, 0].astype(jnp.int32)
    padded = (counts + ROW_BLOCK - 1) // ROW_BLOCK * ROW_BLOCK
    pad_end = jnp.cumsum(padded)
    pad_start = pad_end - padded
    n_blocks = -(-(T * TOP_K) // ROW_BLOCK) + N_EXPERTS
    n_rows = n_blocks * ROW_BLOCK
    block_start = jnp.arange(n_blocks, dtype=jnp.int32) * ROW_BLOCK
    block_e = jnp.minimum(jnp.sum((pad_end[None, :] <= block_start[:, None]).astype(jnp.int32), axis=1),
                          N_EXPERTS - 1)
    n_used = (pad_end[-1:] // ROW_BLOCK).astype(jnp.int32)

    dest_t, wk_t = _compact(sel_t, w_t, pos_t, pad_start.astype(jnp.float32).reshape(-1, 1), tm=min(8192, T))
    block_valid = jnp.clip((pad_start + counts)[block_e] - block_start, 0, ROW_BLOCK).astype(jnp.int32)
    bt = SC_SCATTER_ROWS
    idx3 = jnp.transpose(dest_t.reshape(TOP_K, T // bt, bt), (1, 0, 2))
    xs = _sc_scatter_rows(x1p, idx3, n_rows)
    ys = _experts(xs, block_e, block_valid, n_used, w_e_gate[l], w_e_up[l], w_e_down[l])
    n_chunks = COMBINE_CHUNKS if T % (COMBINE_CHUNKS * 512) == 0 else 1
    t_chunk = T // n_chunks
    out = None
    for c in range(n_chunks):
        idx_c = dest_t[:, c * t_chunk:(c + 1) * t_chunk].reshape(-1)
        gathered = _sc_gather_rows(ys, idx_c).reshape(TOP_K, t_chunk, -1)
        out = _combine2(wk_t, x1, gathered, w_s_gate[l].astype(bf), w_s_up[l].astype(bf), w_s_down[l].astype(bf),
                        ln2_g[l].reshape(1, -1), ln2_b[l].reshape(1, -1), tc=min(512, t_chunk), chunk=c, prev=out)
    res.update(out=out.reshape(B, S, D))
    return res


def kernel(x, mem, w_in, q_norm_g, kv_norm_g, w_uq, w_uk, w_uv, w_qidx, rel_bias, conv_w, w_mem_k, w_mem_v, w_out, ln1_g, ln1_b, w_router, router_bias, w_e_gate, w_e_up, w_e_down, w_s_gate, w_s_up, w_s_down, ln2_g, ln2_b):
    return _stages(x, mem, w_in, q_norm_g, kv_norm_g, w_uq, w_uk, w_uv, w_qidx, rel_bias, conv_w, w_mem_k, w_mem_v, w_out, ln1_g, ln1_b, w_router, router_bias, w_e_gate, w_e_up, w_e_down, w_s_gate, w_s_up, w_s_down, ln2_g, ln2_b)["out"]
```

```python
import functools
import math

import jax
import jax.numpy as jnp
from jax import lax
from jax.experimental import pallas as pl
from jax.experimental.pallas import tpu as pltpu
from jax.experimental.pallas import tpu_sc as plsc

N_HEADS_A = 8
HEAD_DIM = 64
Q_RANK = 256
KV_RANK = 128
N_IDX_HEADS = 8
IDX_DIM = 64
TOPK_MAX = 256
REL_BUCKETS = 32
REL_MAX_DIST = 128
CONV_CH = 256
CONV_WIDTH = 3
N_MEM_HEADS = 4
MIX_A = N_HEADS_A * HEAD_DIM
MIX_C = N_MEM_HEADS * HEAD_DIM
N_EXPERTS = 64
N_GROUPS = 8
GROUP_SIZE = N_EXPERTS // N_GROUPS
TOPK_GROUPS = 4
TOP_K = 8
D_EXPERT = 256
ROUTED_SCALE = 2.5
DEPTH = 1
ALPHA = (2.0 * DEPTH) ** 0.25
LN_EPS = 1e-5
RMS_EPS = 1e-6
LOG2_E = math.log2(math.e)

LANES = 128
SUBLANES = 8
QB = 128
F32_LOWEST = -3.4028234663852886e38
VMEM_LIMIT = 56 * 1024 * 1024
MXU_DTYPE = jnp.bfloat16
ROW_BLOCK = 1024

_NT = (((1,), (1,)), ((), ()))


def _dot(a, b):
    return jnp.dot(a, b, preferred_element_type=jnp.float32)


def _dot_nt(a, b):
    return lax.dot_general(a, b, _NT, preferred_element_type=jnp.float32)


def _cparams(sem):
    return pltpu.CompilerParams(dimension_semantics=sem, vmem_limit_bytes=VMEM_LIMIT)


def _bias_kernel(rb_ref, o_ref):
    s = lax.broadcasted_iota(jnp.int32, (QB, QB), 0)
    t = lax.broadcasted_iota(jnp.int32, (QB, QB), 1)
    max_exact = REL_BUCKETS // 2
    for tile in range(3):
        n = jnp.maximum(t - s + (2 - tile) * QB, 0)
        nf = jnp.maximum(n.astype(jnp.float32), 1.0)
        large = max_exact + (jnp.log(nf / max_exact) / math.log(REL_MAX_DIST / max_exact)
                             * (REL_BUCKETS - max_exact)).astype(jnp.int32)
        large = jnp.minimum(large, REL_BUCKETS - 1)
        bucket = jnp.where(n < max_exact, n, large)
        for h in range(N_HEADS_A):
            acc = jnp.zeros((QB, QB), jnp.float32)
            for b in range(REL_BUCKETS):
                acc = jnp.where(bucket == b, rb_ref[b, h], acc)
            o_ref[tile, h] = acc * LOG2_E


def _bias_tiles(rel_bias):
    return pl.pallas_call(
        _bias_kernel,
        in_specs=[pl.BlockSpec(memory_space=pltpu.SMEM)],
        out_specs=pl.BlockSpec(memory_space=pltpu.VMEM),
        out_shape=jax.ShapeDtypeStruct((3, N_HEADS_A, QB, QB), jnp.float32),
        name="bias_tiles",
    )(rel_bias)


def _proj_kernel(x_ref, mem_ref, wm_ref, qg_ref, kvg_ref, cw_ref, wmk_ref, wmv_ref,
                 cq_ref, ckv_ref, ckvt_ref, kidx_ref, iwt_ref, yb_ref, yc_ref,
                 carry_ref, mk_ref, mv_ref, *, tm):
    si = pl.program_id(1)

    @pl.when(si == 0)
    def _():
        carry_ref[...] = jnp.zeros_like(carry_ref)
        mb = mem_ref[0].astype(MXU_DTYPE)
        mk_ref[...] = _dot(mb, wmk_ref[...]).astype(MXU_DTYPE)
        mv_ref[...] = _dot(mb, wmv_ref[...]).astype(MXU_DTYPE)

    xb = x_ref[...].astype(MXU_DTYPE)
    p = _dot(xb, wm_ref[...])
    small = p[:, p.shape[1] - LANES:]

    o = 0
    cq = p[:, o:o + Q_RANK]; o += Q_RANK
    ckv = p[:, o:o + KV_RANK]; o += KV_RANK
    g_b = p[:, o:o + CONV_CH]; o += CONV_CH
    g_c = p[:, o:o + CONV_CH]; o += CONV_CH
    h_c = p[:, o:o + CONV_CH]; o += CONV_CH
    q_mem = p[:, o:o + MIX_C]

    cq = cq * lax.rsqrt(jnp.mean(cq * cq, axis=-1, keepdims=True) + RMS_EPS) * qg_ref[...]
    ckv = ckv * lax.rsqrt(jnp.mean(ckv * ckv, axis=-1, keepdims=True) + RMS_EPS) * kvg_ref[...]
    cq_ref[...] = cq.astype(MXU_DTYPE)
    ckv_b = ckv.astype(MXU_DTYPE)
    ckv_ref[...] = ckv_b
    ckvt_ref[0] = ckv.T.astype(MXU_DTYPE)

    kidx_ref[...] = small[:, :IDX_DIM].astype(MXU_DTYPE)
    small_t = small.T
    iwt_ref[0] = small_t[IDX_DIM:IDX_DIM + N_IDX_HEADS, :] * (N_IDX_HEADS ** -0.5 * IDX_DIM ** -0.5)

    u = g_c * h_c
    rows = lax.broadcasted_iota(jnp.int32, (tm, 1), 0)
    c6 = carry_ref[SUBLANES - 2:SUBLANES - 1, :]
    c7 = carry_ref[SUBLANES - 1:SUBLANES, :]
    u1 = jnp.where(rows == 0, c7, pltpu.roll(u, 1, 0))
    u2 = jnp.where(rows == 0, c6, jnp.where(rows == 1, c7, pltpu.roll(u, 2, 0)))
    y = cw_ref[0:1, :] * u2
    y = y + cw_ref[1:2, :] * u1
    y = y + cw_ref[2:3, :] * u
    yb_ref[...] = (g_b * y).astype(MXU_DTYPE)
    carry_ref[...] = u[tm - SUBLANES:, :]

    qm = q_mem.astype(MXU_DTYPE)
    outs = []
    for h in range(N_MEM_HEADS):
        sl = slice(h * HEAD_DIM, (h + 1) * HEAD_DIM)
        lg = _dot_nt(qm[:, sl], mk_ref[:, sl]) * (HEAD_DIM ** -0.5)
        lg = lg - jnp.max(lg, axis=-1, keepdims=True)
        e = jnp.exp(lg)
        pr = e / jnp.sum(e, axis=-1, keepdims=True)
        outs.append(_dot(pr.astype(MXU_DTYPE), mv_ref[:, sl]))
    yc_ref[...] = jnp.concatenate(outs, axis=-1).astype(MXU_DTYPE)


def _proj(x2, mem, w_main, q_g, kv_g, conv_w, w_mk, w_mv, B, S, tm):
    T, D = x2.shape
    n_mem = mem.shape[1]
    ns = S // tm
    row = lambda b, s: (b * ns + s, 0)
    const2 = lambda b, s: (0, 0)
    bf = MXU_DTYPE
    return pl.pallas_call(
        functools.partial(_proj_kernel, tm=tm),
        grid=(B, ns),
        in_specs=[
            pl.BlockSpec((tm, D), row),
            pl.BlockSpec((1, n_mem, D), lambda b, s: (b, 0, 0)),
            pl.BlockSpec(w_main.shape, const2),
            pl.BlockSpec(q_g.shape, const2),
            pl.BlockSpec(kv_g.shape, const2),
            pl.BlockSpec(conv_w.shape, const2),
            pl.BlockSpec(w_mk.shape, const2),
            pl.BlockSpec(w_mv.shape, const2),
        ],
        out_specs=[
            pl.BlockSpec((tm, Q_RANK), row),
            pl.BlockSpec((tm, KV_RANK), row),
            pl.BlockSpec((1, KV_RANK, tm), lambda b, s: (b, 0, s)),
            pl.BlockSpec((tm, IDX_DIM), row),
            pl.BlockSpec((1, N_IDX_HEADS, tm), lambda b, s: (b, 0, s)),
            pl.BlockSpec((tm, CONV_CH), row),
            pl.BlockSpec((tm, MIX_C), row),
        ],
        out_shape=[
            jax.ShapeDtypeStruct((T, Q_RANK), bf),
            jax.ShapeDtypeStruct((T, KV_RANK), bf),
            jax.ShapeDtypeStruct((B, KV_RANK, S), bf),
            jax.ShapeDtypeStruct((T, IDX_DIM), bf),
            jax.ShapeDtypeStruct((B, N_IDX_HEADS, S), jnp.float32),
            jax.ShapeDtypeStruct((T, CONV_CH), bf),
            jax.ShapeDtypeStruct((T, MIX_C), bf),
        ],
        scratch_shapes=[
            pltpu.VMEM((SUBLANES, CONV_CH), jnp.float32),
            pltpu.VMEM((n_mem, MIX_C), bf),
            pltpu.VMEM((n_mem, MIX_C), bf),
        ],
        compiler_params=_cparams(("arbitrary", "arbitrary")),
        name="proj",
    )(x2, mem, w_main, q_g, kv_g, conv_w, w_mk, w_mv)


def _key_to_f32(key):
    bits = jnp.where(key < 0, key ^ jnp.int32(0x7FFFFFFF), key)
    return pltpu.bitcast(bits, jnp.float32)


def _colsum8(v):
    return jnp.sum(v.reshape(QB // SUBLANES, SUBLANES, QB), axis=0)


def _colmax8(v):
    return jnp.max(v.reshape(QB // SUBLANES, SUBLANES, QB), axis=0)


UNROLL_WIDTHS = (8, 4, 2, 1)


def _dsa_kernel(cq_ref, iwt_ref, kidx_ref, ckv_ref, ckvt_ref, wqi_ref, wuq_ref, wuk_ref, wuvt_ref,
                bias_ref, o_ref, wfold_ref, qidx_ref, qlat_ref, score_ref, logit_ref, acc_ref,
                *, k_sel, idx_bits):
    i = pl.program_id(1)
    f32 = jnp.float32
    bf = MXU_DTYPE
    n_blocks = i + 1
    n_blocks = n_blocks + jnp.where((n_blocks % 4 == 3) & (n_blocks < pl.num_programs(1)), 1, 0)
    s_loc = lax.broadcasted_iota(jnp.int32, (QB, QB), 0)
    t_glob = i * QB + lax.broadcasted_iota(jnp.int32, (QB, QB), 1)

    def blk(jb):
        return pl.multiple_of(jb * QB, QB)

    def block_loop(fn, init):
        c, start = init, 0
        for width in UNROLL_WIDTHS:
            n = (n_blocks - start) // width
            c = lax.fori_loop(0, n, lambda it, c, w=width, s=start: fn(s + it * w, w, c), c)
            start = start + n * width
        return c

    @pl.when(i == 0)
    def _():
        for h in range(N_HEADS_A):
            wfold_ref[:, h * KV_RANK:(h + 1) * KV_RANK] = (
                _dot_nt(wuq_ref[:, h * HEAD_DIM:(h + 1) * HEAD_DIM], wuk_ref[h])
                * (HEAD_DIM ** -0.5 * LOG2_E)).astype(bf)

    cq = cq_ref[...]
    q_idx = _dot(cq, wqi_ref[...]).astype(bf)
    q_lat = _dot(cq, wfold_ref[...]).astype(bf)
    for h in range(N_HEADS_A):
        qidx_ref[h * QB:(h + 1) * QB, :] = q_idx[:, h * IDX_DIM:(h + 1) * IDX_DIM]
        qlat_ref[h * QB:(h + 1) * QB, :] = q_lat[:, h * KV_RANK:(h + 1) * KV_RANK]
    iw = iwt_ref[0]

    def score_body(jb0, nb, n_pos8):
        d_blk = _dot_nt(kidx_ref[pl.ds(blk(jb0), nb * QB), :], qidx_ref[...])
        for sb in range(nb):
            off = blk(jb0 + sb)
            d_all = d_blk[sb * QB:(sb + 1) * QB, :]
            acc = jnp.maximum(d_all[:, 0:QB], 0.0) * iw[0:1, :]
            for h in range(1, N_IDX_HEADS):
                acc = acc + jnp.maximum(d_all[:, h * QB:(h + 1) * QB], 0.0) * iw[h:h + 1, :]
            sc = jnp.where(s_loc + off <= t_glob, acc + 0.0, F32_LOWEST)
            score_ref[pl.ds(off, QB), :] = sc
            n_pos8 = n_pos8 + _colsum8(jnp.where(sc >= 0.0, 1.0, 0.0))
        return n_pos8

    n_pos8 = block_loop(score_body, jnp.zeros((SUBLANES, QB), f32))

    def count_where(pred):
        def body(jb0, nb, acc):
            for sb in range(nb):
                off = blk(jb0 + sb)
                acc = acc + _colsum8(jnp.where(pred(score_ref[pl.ds(off, QB), :], off), 1.0, 0.0))
            return acc
        acc = block_loop(body, jnp.zeros((SUBLANES, QB), f32))
        return jnp.sum(acc, axis=0, keepdims=True)

    kf = float(k_sel)

    def search():
        c0 = jnp.sum(n_pos8, axis=0, keepdims=True)
        cand0 = jnp.where(c0 >= kf, jnp.int32(0), jnp.int32(-2 ** 31))
        n_ge0 = jnp.where(c0 >= kf, c0, -1.0)

        def bit_body(it, carry):
            cand, n_ge = carry
            trial = cand + lax.shift_left(jnp.int32(1), 30 - it)
            tf = _key_to_f32(trial)
            cnt = count_where(lambda sc, off: sc >= tf)
            take = cnt >= kf
            return jnp.where(take, trial, cand), jnp.where(take, cnt, n_ge)

        cand, n_ge = lax.fori_loop(0, 31, bit_body, (cand0, n_ge0))
        thr = _key_to_f32(cand)
        keep_all_ties = jnp.full((1, QB), 2 ** idx_bits - 1, jnp.int32)

        def resolve_ties():
            n_gt = count_where(lambda sc, off: sc > thr)
            n_eq = count_where(lambda sc, off: sc == thr)
            need = kf - n_gt

            def tie_search():
                def tbody(it, xcut):
                    trial = xcut + lax.shift_left(jnp.int32(1), idx_bits - 1 - it)
                    cnt = count_where(lambda sc, off: (sc == thr) & (s_loc + off < trial))
                    return jnp.where(cnt < need, trial, xcut)
                return lax.fori_loop(0, idx_bits, tbody, jnp.zeros((1, QB), jnp.int32))

            return lax.cond(jnp.max(n_eq - need) > 0.0, tie_search, lambda: keep_all_ties)

        plain = jnp.max(jnp.abs(n_ge - kf)) == 0.0
        xcut = lax.cond(plain, lambda: keep_all_ties, resolve_ties)
        return thr, xcut, plain

    def no_search():
        return (jnp.full((1, QB), F32_LOWEST, f32), jnp.full((1, QB), 2 ** idx_bits - 1, jnp.int32),
                jnp.zeros((), jnp.bool_))

    thr, xcut, plain = lax.cond((i + 1) * QB > k_sel, search, no_search)

    def general_mask(off):
        sc = score_ref[pl.ds(off, QB), :]
        s_glob = s_loc + off
        keep = ((sc > thr) | ((sc == thr) & (s_glob <= xcut))) & (s_glob <= t_glob)
        return jnp.where(keep, 0.0, -jnp.inf)

    def plain_mask(off):
        return jnp.where(score_ref[pl.ds(off, QB), :] >= thr, 0.0, -jnp.inf)

    acc_ref[...] = jnp.zeros_like(acc_ref)

    def att_body(selection_mask, jb0, nb, carry):
        m, l8 = list(carry[0]), list(carry[1])
        rows = nb * QB
        lg_blk = _dot_nt(ckv_ref[pl.ds(blk(jb0), rows), :], qlat_ref[...])
        blk_max = [None] * N_HEADS_A
        for sb in range(nb):
            off = blk(jb0 + sb)
            msk = selection_mask(off)
            bsel = jnp.clip(jb0 + sb - i + 2, 0, 2)
            for h in range(N_HEADS_A):
                lgh = lg_blk[sb * QB:(sb + 1) * QB, h * QB:(h + 1) * QB] + bias_ref[bsel, h] + msk
                logit_ref[sb * QB:(sb + 1) * QB, h * QB:(h + 1) * QB] = lgh
                cm = _colmax8(lgh)
                blk_max[h] = cm if blk_max[h] is None else jnp.maximum(blk_max[h], cm)
        ps, scales = [], []
        for h in range(N_HEADS_A):
            m_new = jnp.maximum(m[h], jnp.max(blk_max[h], axis=0, keepdims=True))
            m_ref = jnp.where(m_new == -jnp.inf, 0.0, m_new)
            p = jnp.exp2(logit_ref[0:rows, h * QB:(h + 1) * QB] - m_ref)
            scale = jnp.exp2(m[h] - m_ref)
            l8[h] = l8[h] * scale + jnp.sum(p.reshape(rows // SUBLANES, SUBLANES, QB), axis=0)
            m[h] = m_new
            ps.append(p.astype(bf))
            scales.append(scale)
        pv = _dot(ckvt_ref[0, :, pl.ds(blk(jb0), rows)], jnp.concatenate(ps, axis=1))
        for h in range(N_HEADS_A):
            hs = slice(h * QB, (h + 1) * QB)
            acc_ref[:, hs] = acc_ref[:, hs] * scales[h] + pv[:, hs]
        return tuple(m), tuple(l8)

    carry0 = (tuple(jnp.full((1, QB), -jnp.inf, f32) for _ in range(N_HEADS_A)),
              tuple(jnp.zeros((SUBLANES, QB), f32) for _ in range(N_HEADS_A)))
    _, l8 = lax.cond(plain,
                     lambda: block_loop(functools.partial(att_body, plain_mask), carry0),
                     lambda: block_loop(functools.partial(att_body, general_mask), carry0))

    outs = []
    for h in range(N_HEADS_A):
        l_row = jnp.sum(l8[h], axis=0, keepdims=True)
        o_lat_t = (acc_ref[:, h * QB:(h + 1) * QB] / l_row).astype(bf)
        outs.append(_dot(wuvt_ref[h], o_lat_t))
    o_ref[...] = jnp.concatenate(outs, axis=0).T.astype(o_ref.dtype)


def _dsa(cq, iwt, kidx, ckv, ckvt, w_qidx, w_uq, w_uk_h, w_uvt_h, bias_tiles, B, S):
    T = cq.shape[0]
    assert S % QB == 0 and QB >= REL_MAX_DIST
    nq = S // QB
    k_sel = min(TOPK_MAX, S // 4)
    idx_bits = max(1, (S - 1).bit_length())
    c2 = lambda b, i: (0, 0)
    c3 = lambda b, i: (0, 0, 0)
    return pl.pallas_call(
        functools.partial(_dsa_kernel, k_sel=k_sel, idx_bits=idx_bits),
        grid=(B, nq),
        in_specs=[
            pl.BlockSpec((QB, Q_RANK), lambda b, i: (b * nq + i, 0)),
            pl.BlockSpec((1, N_IDX_HEADS, QB), lambda b, i: (b, 0, i)),
            pl.BlockSpec((S, IDX_DIM), lambda b, i: (b, 0)),
            pl.BlockSpec((S, KV_RANK), lambda b, i: (b, 0)),
            pl.BlockSpec((1, KV_RANK, S), lambda b, i: (b, 0, 0)),
            pl.BlockSpec(w_qidx.shape, c2),
            pl.BlockSpec(w_uq.shape, c2),
            pl.BlockSpec(w_uk_h.shape, c3),
            pl.BlockSpec(w_uvt_h.shape, c3),
            pl.BlockSpec(bias_tiles.shape, lambda b, i: (0, 0, 0, 0)),
        ],
        out_specs=pl.BlockSpec((QB, MIX_A), lambda b, i: (b * nq + i, 0)),
        out_shape=jax.ShapeDtypeStruct((T, MIX_A), MXU_DTYPE),
        scratch_shapes=[
            pltpu.VMEM((Q_RANK, N_HEADS_A * KV_RANK), MXU_DTYPE),
            pltpu.VMEM((N_IDX_HEADS * QB, IDX_DIM), MXU_DTYPE),
            pltpu.VMEM((N_HEADS_A * QB, KV_RANK), MXU_DTYPE),
            pltpu.VMEM((S, QB), jnp.float32),
            pltpu.VMEM((max(UNROLL_WIDTHS) * QB, N_HEADS_A * QB), jnp.float32),
            pltpu.VMEM((KV_RANK, N_HEADS_A * QB), jnp.float32),
        ],
        compiler_params=_cparams(("arbitrary", "arbitrary")),
        name="dsa",
    )(cq, iwt, kidx, ckv, ckvt, w_qidx, w_uq, w_uk_h, w_uvt_h, bias_tiles)


def _layer_norm(xf, g, b):
    mu = jnp.mean(xf, axis=-1, keepdims=True)
    xc = xf - mu
    var = jnp.mean(xc * xc, axis=-1, keepdims=True)
    return xc * lax.rsqrt(var + LN_EPS) * g + b


def _rank_rows(v, n):
    ri = lax.broadcasted_iota(jnp.int32, v.shape, 0)
    rank = jnp.zeros(v.shape, jnp.float32)
    for r2 in range(n):
        row = v[r2:r2 + 1, :]
        beats = (row > v) | ((row == v) & (ri > r2))
        rank = rank + jnp.where(beats, 1.0, 0.0)
    return rank


def _top_rows(v, k):
    n = v.shape[0]
    ri = lax.broadcasted_iota(jnp.int32, v.shape, 0)
    sel = jnp.zeros(v.shape, jnp.float32)
    for _ in range(k):
        m = jnp.max(v, axis=0, keepdims=True)
        first = jnp.min(jnp.where(v == m, ri, n), axis=0, keepdims=True)
        pick = ri == first
        sel = jnp.where(pick, 1.0, sel)
        v = jnp.where(pick, -jnp.inf, v)
    return sel > 0.5


def _pack_factor():
    return 4 // jnp.dtype(MXU_DTYPE).itemsize


def _pack_rows(x):
    if _pack_factor() == 1:
        return pltpu.bitcast(x, jnp.int32)
    half = x.shape[1] // 2
    b = pltpu.bitcast(x.astype(MXU_DTYPE).astype(jnp.float32), jnp.int32)
    return b[:, half:] | lax.shift_right_logical(b[:, :half], jnp.int32(16))


_HIGH_HALF = -(1 << 16)


def _unpack_rows_f32(p):
    if _pack_factor() == 1:
        return [pltpu.bitcast(p, jnp.float32)]
    lo = pltpu.bitcast(lax.shift_left(p, jnp.int32(16)), jnp.float32)
    hi = pltpu.bitcast(p & jnp.int32(_HIGH_HALF), jnp.float32)
    return [lo, hi]


def _unpack_rows(p):
    return [v.astype(MXU_DTYPE) for v in _unpack_rows_f32(p)]


def _mix_router_kernel(x_ref, ya_ref, yb_ref, yc_ref, wo_ref, g_ref, b_ref, wrt_ref, rb_ref, exp_ref,
                       x1_ref, x1p_ref, sel_ref, w_ref, pos_ref, cnt_ref, base_ref, *, tm):
    step = pl.program_id(0)
    f32 = jnp.float32

    @pl.when(step == 0)
    def _():
        base_ref[...] = jnp.zeros_like(base_ref)

    mix = _dot(ya_ref[...], wo_ref[0:MIX_A, :])
    mix = mix + _dot(yb_ref[...], wo_ref[MIX_A:MIX_A + CONV_CH, :])
    mix = mix + _dot(yc_ref[...], wo_ref[MIX_A + CONV_CH:, :])
    x1 = _layer_norm(ALPHA * x_ref[...] + mix, g_ref[...], b_ref[...])
    x1_ref[...] = x1
    x1p_ref[...] = _pack_rows(x1)

    lg = lax.dot_general(wrt_ref[...], x1, _NT, precision=lax.Precision.HIGHEST, preferred_element_type=f32)
    s = 1.0 / (1.0 + jnp.exp(-lg))
    sc = s + rb_ref[...]

    g3 = sc.reshape(N_GROUPS, GROUP_SIZE, tm)
    m1 = jnp.max(g3, axis=1, keepdims=True)
    is_m1 = g3 == m1
    n_m1 = jnp.sum(jnp.where(is_m1, 1.0, 0.0), axis=1, keepdims=True)
    m2 = jnp.max(jnp.where(is_m1, -jnp.inf, g3), axis=1, keepdims=True)
    gscore = (m1 + jnp.where(n_m1 > 1.0, m1, m2)).reshape(N_GROUPS, tm)
    gsel = jnp.where(_rank_rows(gscore, N_GROUPS) < float(TOPK_GROUPS), 1.0, 0.0)
    emask = _dot(exp_ref[...], gsel.astype(MXU_DTYPE)) > 0.5
    masked = jnp.where(emask, sc, -jnp.inf)
    sel = _top_rows(masked, TOP_K) & emask
    self_ = jnp.where(sel, 1.0, 0.0)
    top_s = jnp.where(sel, s, 0.0)
    w = top_s / jnp.sum(top_s, axis=0, keepdims=True) * ROUTED_SCALE

    t_r = lax.broadcasted_iota(jnp.int32, (tm, tm), 0)
    t_c = lax.broadcasted_iota(jnp.int32, (tm, tm), 1)
    upper = jnp.where(t_r < t_c, 1.0, 0.0).astype(MXU_DTYPE)
    pref = _dot(self_.astype(MXU_DTYPE), upper)
    base = base_ref[...]
    sel_ref[...] = self_
    w_ref[...] = w
    pos_ref[...] = base + pref
    base = base + jnp.sum(self_, axis=1, keepdims=True)
    base_ref[...] = base
    cnt_ref[...] = jnp.broadcast_to(base, cnt_ref.shape)


def _mix_router(x2, ya, yb, yc, w_out, ln_g, ln_b, w_router_t, router_bias, tm):
    T, D = x2.shape
    E = N_EXPERTS
    expand = (jnp.arange(E)[:, None] // GROUP_SIZE == jnp.arange(N_GROUPS)[None, :]).astype(MXU_DTYPE)
    row = lambda i: (i, 0)
    col = lambda i: (0, i)
    c2 = lambda i: (0, 0)
    f32 = jnp.float32
    return pl.pallas_call(
        functools.partial(_mix_router_kernel, tm=tm),
        grid=(T // tm,),
        in_specs=[
            pl.BlockSpec((tm, D), row),
            pl.BlockSpec((tm, MIX_A), row),
            pl.BlockSpec((tm, CONV_CH), row),
            pl.BlockSpec((tm, MIX_C), row),
            pl.BlockSpec(w_out.shape, c2),
            pl.BlockSpec((1, D), c2),
            pl.BlockSpec((1, D), c2),
            pl.BlockSpec((E, D), c2),
            pl.BlockSpec((E, 1), c2),
            pl.BlockSpec((E, N_GROUPS), c2),
        ],
        out_specs=[
            pl.BlockSpec((tm, D), row),
            pl.BlockSpec((tm, D // _pack_factor()), row),
            pl.BlockSpec((E, tm), col),
            pl.BlockSpec((E, tm), col),
            pl.BlockSpec((E, tm), col),
            pl.BlockSpec((E, LANES), c2),
        ],
        out_shape=[
            jax.ShapeDtypeStruct((T, D), f32),
            jax.ShapeDtypeStruct((T, D // _pack_factor()), jnp.int32),
            jax.ShapeDtypeStruct((E, T), f32),
            jax.ShapeDtypeStruct((E, T), f32),
            jax.ShapeDtypeStruct((E, T), f32),
            jax.ShapeDtypeStruct((E, LANES), f32),
        ],
        scratch_shapes=[pltpu.VMEM((E, 1), f32)],
        compiler_params=_cparams(("arbitrary",)),
        name="mix_router",
    )(x2, ya, yb, yc, w_out, ln_g, ln_b, w_router_t, router_bias, expand)


def _compact_kernel(sel_ref, w_ref, pos_ref, pstart_ref, low_ref, dest_ref, wk_ref):
    sel = sel_ref[...]
    on = sel > 0.5
    rank = _dot(low_ref[...], sel.astype(MXU_DTYPE))
    row = pstart_ref[...] + pos_ref[...]
    w = w_ref[...]
    dests, ws = [], []
    for k in range(TOP_K):
        m = on & (rank == float(k))
        dests.append(jnp.sum(jnp.where(m, row, 0.0), axis=0, keepdims=True))
        ws.append(jnp.sum(jnp.where(m, w, 0.0), axis=0, keepdims=True))
    dest_ref[...] = jnp.concatenate(dests, axis=0).astype(jnp.int32)
    wk_ref[...] = jnp.concatenate(ws, axis=0)


def _compact(sel_t, w_t, pos_t, pad_start, tm):
    E, T = sel_t.shape
    lower = (jnp.arange(E)[None, :] < jnp.arange(E)[:, None]).astype(MXU_DTYPE)
    col = lambda i: (0, i)
    c2 = lambda i: (0, 0)
    return pl.pallas_call(
        _compact_kernel,
        grid=(T // tm,),
        in_specs=[pl.BlockSpec((E, tm), col), pl.BlockSpec((E, tm), col), pl.BlockSpec((E, tm), col),
                  pl.BlockSpec((E, 1), c2), pl.BlockSpec((E, E), c2)],
        out_specs=[pl.BlockSpec((TOP_K, tm), col), pl.BlockSpec((TOP_K, tm), col)],
        out_shape=[jax.ShapeDtypeStruct((TOP_K, T), jnp.int32), jax.ShapeDtypeStruct((TOP_K, T), jnp.float32)],
        compiler_params=_cparams(("arbitrary",)),
        name="route_compact",
    )(sel_t, w_t, pos_t, pad_start, lower)


def _silu(g):
    return g / (1.0 + jnp.exp(-g))


def _expert_kernel(be_ref, nv_ref, nu_ref, xs_ref, wg_ref, wu_ref, wd_ref, ys_ref, wgb_ref, wub_ref, wdb_ref):
    i = pl.program_id(0)

    @pl.when((i == 0) | (be_ref[i] != be_ref[jnp.maximum(i - 1, 0)]))
    def _():
        wgb_ref[...] = wg_ref[0].astype(MXU_DTYPE)
        wub_ref[...] = wu_ref[0].astype(MXU_DTYPE)
        wdb_ref[...] = wd_ref[0].astype(MXU_DTYPE)

    @pl.when(i < nu_ref[0])
    def _():
        live = lax.broadcasted_iota(jnp.int32, (ROW_BLOCK, 1), 0) < nv_ref[i]
        parts = [jnp.where(live, v, jnp.zeros_like(v)) for v in _unpack_rows(xs_ref[...])]
        dk = wgb_ref.shape[0] // len(parts)

        def proj(w_ref):
            acc = _dot(parts[0], w_ref[0:dk, :])
            for n in range(1, len(parts)):
                acc = acc + _dot(parts[n], w_ref[n * dk:(n + 1) * dk, :])
            return acc

        a = (_silu(proj(wgb_ref)) * proj(wub_ref)).astype(MXU_DTYPE)
        ys_ref[...] = _pack_rows(_dot(a, wdb_ref[...]))


def _experts(xs, block_e, block_valid, n_used, w_gate, w_up, w_down):
    n_rows, W = xs.shape
    D = w_gate.shape[1]
    n_blocks = n_rows // ROW_BLOCK
    blk = lambda i, be, nv, nu: (jnp.minimum(i, nu[0] - 1), 0)
    wsel = lambda i, be, nv, nu: (be[i], 0, 0)
    return pl.pallas_call(
        _expert_kernel,
        grid_spec=pltpu.PrefetchScalarGridSpec(
            num_scalar_prefetch=3,
            grid=(n_blocks,),
            in_specs=[
                pl.BlockSpec((ROW_BLOCK, W), blk),
                pl.BlockSpec((1, D, D_EXPERT), wsel),
                pl.BlockSpec((1, D, D_EXPERT), wsel),
                pl.BlockSpec((1, D_EXPERT, D), wsel),
            ],
            out_specs=pl.BlockSpec((ROW_BLOCK, W), blk),
            scratch_shapes=[pltpu.VMEM((D, D_EXPERT), MXU_DTYPE), pltpu.VMEM((D, D_EXPERT), MXU_DTYPE),
                            pltpu.VMEM((D_EXPERT, D), MXU_DTYPE)],
        ),
        out_shape=jax.ShapeDtypeStruct((n_rows, W), xs.dtype),
        compiler_params=_cparams(("arbitrary",)),
        name="experts",
    )(block_e, block_valid, n_used, xs, w_gate, w_up, w_down)


SC_CORES = 2
SC_SUBCORES = 16
SC_GATHER_ROWS = 64
COMBINE_CHUNKS = 8


def _sc_gather_rows(table, idx):
    n = idx.shape[0]
    w = table.shape[1]
    n_workers = SC_CORES * SC_SUBCORES
    per_worker = n // n_workers
    assert n % n_workers == 0 and per_worker % SC_GATHER_ROWS == 0
    mesh = plsc.VectorSubcoreMesh(core_axis_name="c", subcore_axis_name="s")

    @functools.partial(
        pl.kernel, mesh=mesh,
        out_type=jax.ShapeDtypeStruct((n, w), table.dtype),
        scratch_types=[
            pltpu.VMEM((2, SC_GATHER_ROWS), jnp.int32),
            pltpu.VMEM((2, SC_GATHER_ROWS, w), table.dtype),
            pltpu.SemaphoreType.DMA((2,)),
        ],
        name="sc_gather_rows",
    )
    def gather(table_hbm, idx_hbm, out_hbm, idx_v, rows_v, sem):
        wid = lax.axis_index("s") * SC_CORES + lax.axis_index("c")
        base = wid * per_worker
        n_steps = per_worker // SC_GATHER_ROWS

        def gather_copy(slot):
            return pltpu.make_async_copy(table_hbm.at[idx_v.at[slot]], rows_v.at[slot], sem.at[slot])

        def start(step, slot):
            pltpu.sync_copy(idx_hbm.at[pl.ds(base + step * SC_GATHER_ROWS, SC_GATHER_ROWS)], idx_v.at[slot])
            gather_copy(slot).start()

        start(0, 0)

        @pl.loop(0, n_steps, step=2)
        def _(g):
            for slot in range(2):
                step = g + slot

                @pl.when(step + 1 < n_steps)
                def _():
                    start(step + 1, 1 - slot)

                gather_copy(slot).wait()
                pltpu.sync_copy(rows_v.at[slot], out_hbm.at[pl.ds(base + step * SC_GATHER_ROWS, SC_GATHER_ROWS)])

    return gather(table, idx)


SC_SCATTER_ROWS = 64


def _sc_scatter_rows(rows, idx3, n_out):
    n_src, w = rows.shape
    n_chunks, n_dst, batch = idx3.shape
    n_workers = SC_CORES * SC_SUBCORES
    assert batch == SC_SCATTER_ROWS and n_chunks * batch == n_src and n_chunks % (2 * n_workers) == 0
    per_worker = n_chunks // n_workers
    mesh = plsc.VectorSubcoreMesh(core_axis_name="c", subcore_axis_name="s")

    @functools.partial(
        pl.kernel, mesh=mesh,
        out_type=jax.ShapeDtypeStruct((n_out, w), rows.dtype),
        scratch_types=[
            pltpu.VMEM((2, n_dst, batch), jnp.int32),
            pltpu.VMEM((2, batch, w), rows.dtype),
            pltpu.SemaphoreType.DMA((2,)),
            pltpu.SemaphoreType.DMA,
        ],
        name="sc_scatter_rows",
    )
    def scatter(rows_hbm, idx_hbm, out_hbm, idx_v, rows_v, load_sem, store_sem):
        wid = lax.axis_index("s") * SC_CORES + lax.axis_index("c")

        def load_copy(step, slot):
            c = wid * per_worker + step
            return pltpu.make_async_copy(rows_hbm.at[pl.ds(c * batch, batch)], rows_v.at[slot], load_sem.at[slot])

        def load(step, slot):
            pltpu.sync_copy(idx_hbm.at[wid * per_worker + step], idx_v.at[slot])
            load_copy(step, slot).start()

        def store_copy(slot, k):
            return pltpu.make_async_copy(rows_v.at[slot], out_hbm.at[idx_v.at[slot].at[k]], store_sem)

        load(0, 0)

        @pl.loop(0, per_worker, step=2)
        def _(g):
            for slot in range(2):
                step = g + slot

                @pl.when(step + 1 < per_worker)
                def _():
                    load(step + 1, 1 - slot)

                load_copy(step, slot).wait()
                for k in range(n_dst):
                    store_copy(slot, k).start()
                for k in range(n_dst):
                    store_copy(slot, k).wait()

    return scatter(rows, idx3)


def _combine2_kernel(wk_ref, x1_ref, g_ref_rows, wsg_ref, wsu_ref, wsd_ref, g_ref, b_ref, o_ref):
    x1 = x1_ref[...]
    xb = x1.astype(MXU_DTYPE)
    a = (_silu(_dot(xb, wsg_ref[...])) * _dot(xb, wsu_ref[...])).astype(MXU_DTYPE)
    shared = _dot(a, wsd_ref[...])
    wk = wk_ref[...].T
    groups = [wk[:, 0:1] * v for v in _unpack_rows_f32(g_ref_rows[0])]
    for k in range(1, TOP_K):
        groups = [g + wk[:, k:k + 1] * v for g, v in zip(groups, _unpack_rows_f32(g_ref_rows[k]))]
    routed = jnp.concatenate(groups, axis=1)
    o_ref[...] = _layer_norm(ALPHA * x1 + (routed + shared), g_ref[...], b_ref[...])


def _combine2_kernel_into(wk_ref, x1_ref, g_ref_rows, wsg_ref, wsu_ref, wsd_ref, g_ref, b_ref, prev_ref, o_ref):
    del prev_ref
    _combine2_kernel(wk_ref, x1_ref, g_ref_rows, wsg_ref, wsu_ref, wsd_ref, g_ref, b_ref, o_ref)


def _combine2(wk_t, x1, gathered, w_sg, w_su, w_sd, ln_g, ln_b, tc, chunk, prev):
    T, D = x1.shape
    _, t_chunk, W = gathered.shape
    base = chunk * (t_chunk // tc)
    row = lambda i: (base + i, 0)
    c2 = lambda i: (0, 0)
    in_specs = [
        pl.BlockSpec((TOP_K, tc), lambda i: (0, base + i)),
        pl.BlockSpec((tc, D), row),
        pl.BlockSpec((TOP_K, tc, W), lambda i: (0, i, 0)),
        pl.BlockSpec(w_sg.shape, c2),
        pl.BlockSpec(w_su.shape, c2),
        pl.BlockSpec(w_sd.shape, c2),
        pl.BlockSpec((1, D), c2),
        pl.BlockSpec((1, D), c2),
    ]
    args = [wk_t, x1, gathered, w_sg, w_su, w_sd, ln_g, ln_b]
    if prev is None:
        body, aliases = _combine2_kernel, {}
    else:
        body, aliases = _combine2_kernel_into, {len(args): 0}
        in_specs.append(pl.BlockSpec(memory_space=pl.ANY))
        args.append(prev)
    return pl.pallas_call(
        body,
        grid=(t_chunk // tc,),
        in_specs=in_specs,
        out_specs=pl.BlockSpec((tc, D), row),
        out_shape=jax.ShapeDtypeStruct((T, D), jnp.float32),
        input_output_aliases=aliases,
        compiler_params=_cparams(("arbitrary",)),
        name="combine",
    )(*args)


def _split_w_in(w_in):
    o_kv = Q_RANK
    o_ki = o_kv + KV_RANK
    o_iw = o_ki + IDX_DIM
    o_rest = o_iw + N_IDX_HEADS
    w_small = jnp.pad(w_in[:, o_ki:o_rest], ((0, 0), (0, LANES - IDX_DIM - N_IDX_HEADS)))
    return jnp.concatenate([w_in[:, :o_ki], w_in[:, o_rest:], w_small], axis=1).astype(MXU_DTYPE)


def _stages(x, mem, w_in, q_norm_g, kv_norm_g, w_uq, w_uk, w_uv, w_qidx, rel_bias, conv_w, w_mem_k, w_mem_v, w_out, ln1_g, ln1_b, w_router, router_bias, w_e_gate, w_e_up, w_e_down, w_s_gate, w_s_up, w_s_down, ln2_g, ln2_b):
    B, S, D = x.shape
    T = B * S
    bf = MXU_DTYPE
    assert w_in.shape[0] == DEPTH == 1, "single-layer stack"
    l = 0
    res = {}
    x2 = x.reshape(T, D)
    cq, ckv, ckvt, kidx, iwt, yb, yc = _proj(
        x2, mem, _split_w_in(w_in[l]), q_norm_g[l].reshape(1, -1), kv_norm_g[l].reshape(1, -1), conv_w[l],
        w_mem_k[l].astype(bf), w_mem_v[l].astype(bf), B, S, tm=min(1024, S))
    res.update(c_q=cq, c_kv=ckv, k_idx=kidx, y_b=yb, y_c=yc,
               idx_w=jnp.swapaxes(iwt, 1, 2) / (N_IDX_HEADS ** -0.5 * IDX_DIM ** -0.5))
    bias_t = _bias_tiles(rel_bias)
    ya = _dsa(cq, iwt, kidx, ckv, ckvt,
              w_qidx[l].reshape(Q_RANK, -1).astype(bf), w_uq[l].reshape(Q_RANK, -1).astype(bf),
              jnp.transpose(w_uk[l], (1, 0, 2)).astype(bf), jnp.transpose(w_uv[l], (1, 2, 0)).astype(bf),
              bias_t, B, S)
    res.update(y_a=ya)

    x1, x1p, sel_t, w_t, pos_t, cnt = _mix_router(
        x2, ya, yb, yc, w_out[l].astype(bf), ln1_g[l].reshape(1, -1), ln1_b[l].reshape(1, -1),
        w_router[l].T, router_bias[l].reshape(-1, 1), tm=min(1024, T))
    res.update(x1=x1)

    counts = cnt[:, 0].astype(jnp.int32)
    padded = (counts + ROW_BLOCK - 1) // ROW_BLOCK * ROW_BLOCK
    pad_end = jnp.cumsum(padded)
    pad_start = pad_end - padded
    n_blocks = -(-(T * TOP_K) // ROW_BLOCK) + N_EXPERTS
    n_rows = n_blocks * ROW_BLOCK
    block_start = jnp.arange(n_blocks, dtype=jnp.int32) * ROW_BLOCK
    block_e = jnp.minimum(jnp.sum((pad_end[None, :] <= block_start[:, None]).astype(jnp.int32), axis=1),
                          N_EXPERTS - 1)
    n_used = (pad_end[-1:] // ROW_BLOCK).astype(jnp.int32)

    dest_t, wk_t = _compact(sel_t, w_t, pos_t, pad_start.astype(jnp.float32).reshape(-1, 1), tm=min(8192, T))
    block_valid = jnp.clip((pad_start + counts)[block_e] - block_start, 0, ROW_BLOCK).astype(jnp.int32)
    bt = SC_SCATTER_ROWS
    idx3 = jnp.transpose(dest_t.reshape(TOP_K, T // bt, bt), (1, 0, 2))
    xs = _sc_scatter_rows(x1p, idx3, n_rows)
    ys = _experts(xs, block_e, block_valid, n_used, w_e_gate[l], w_e_up[l], w_e_down[l])
    n_chunks = COMBINE_CHUNKS if T % (COMBINE_CHUNKS * 512) == 0 else 1
    t_chunk = T // n_chunks
    out = None
    for c in range(n_chunks):
        idx_c = dest_t[:, c * t_chunk:(c + 1) * t_chunk].reshape(-1)
        gathered = _sc_gather_rows(ys, idx_c).reshape(TOP_K, t_chunk, -1)
        out = _combine2(wk_t, x1, gathered, w_s_gate[l].astype(bf), w_s_up[l].astype(bf), w_s_down[l].astype(bf),
                        ln2_g[l].reshape(1, -1), ln2_b[l].reshape(1, -1), tc=min(512, t_chunk), chunk=c, prev=out)
    res.update(out=out.reshape(B, S, D))
    return res


def kernel(x, mem, w_in, q_norm_g, kv_norm_g, w_uq, w_uk, w_uv, w_qidx, rel_bias, conv_w, w_mem_k, w_mem_v, w_out, ln1_g, ln1_b, w_router, router_bias, w_e_gate, w_e_up, w_e_down, w_s_gate, w_s_up, w_s_down, ln2_g, ln2_b):
    return _stages(x, mem, w_in, q_norm_g, kv_norm_g, w_uq, w_uk, w_uv, w_qidx, rel_bias, conv_w, w_mem_k, w_mem_v, w_out, ln1_g, ln1_b, w_router, router_bias, w_e_gate, w_e_up, w_e_down, w_s_gate, w_s_up, w_s_down, ln2_g, ln2_b)["out"]
```

```python
import functools
import math

import jax
import jax.numpy as jnp
from jax import lax
from jax.experimental import pallas as pl
from jax.experimental.pallas import tpu as pltpu
from jax.experimental.pallas import tpu_sc as plsc

N_HEADS_A = 8
HEAD_DIM = 64
Q_RANK = 256
KV_RANK = 128
N_IDX_HEADS = 8
IDX_DIM = 64
TOPK_MAX = 256
REL_BUCKETS = 32
REL_MAX_DIST = 128
CONV_CH = 256
CONV_WIDTH = 3
N_MEM_HEADS = 4
MIX_A = N_HEADS_A * HEAD_DIM
MIX_C = N_MEM_HEADS * HEAD_DIM
N_EXPERTS = 64
N_GROUPS = 8
GROUP_SIZE = N_EXPERTS // N_GROUPS
TOPK_GROUPS = 4
TOP_K = 8
D_EXPERT = 256
ROUTED_SCALE = 2.5
DEPTH = 1
ALPHA = (2.0 * DEPTH) ** 0.25
LN_EPS = 1e-5
RMS_EPS = 1e-6
LOG2_E = math.log2(math.e)

LANES = 128
SUBLANES = 8
QB = 128
F32_LOWEST = -3.4028234663852886e38
VMEM_LIMIT = 56 * 1024 * 1024
MXU_DTYPE = jnp.bfloat16
ROW_BLOCK = 1024

_NT = (((1,), (1,)), ((), ()))


def _dot(a, b):
    return jnp.dot(a, b, preferred_element_type=jnp.float32)


def _dot_nt(a, b):
    return lax.dot_general(a, b, _NT, preferred_element_type=jnp.float32)


def _cparams(sem):
    return pltpu.CompilerParams(dimension_semantics=sem, vmem_limit_bytes=VMEM_LIMIT)


def _bias_kernel(rb_ref, o_ref):
    s = lax.broadcasted_iota(jnp.int32, (QB, QB), 0)
    t = lax.broadcasted_iota(jnp.int32, (QB, QB), 1)
    max_exact = REL_BUCKETS // 2
    for tile in range(3):
        n = jnp.maximum(t - s + (2 - tile) * QB, 0)
        nf = jnp.maximum(n.astype(jnp.float32), 1.0)
        large = max_exact + (jnp.log(nf / max_exact) / math.log(REL_MAX_DIST / max_exact)
                             * (REL_BUCKETS - max_exact)).astype(jnp.int32)
        large = jnp.minimum(large, REL_BUCKETS - 1)
        bucket = jnp.where(n < max_exact, n, large)
        for h in range(N_HEADS_A):
            acc = jnp.zeros((QB, QB), jnp.float32)
            for b in range(REL_BUCKETS):
                acc = jnp.where(bucket == b, rb_ref[b, h], acc)
            o_ref[tile, h] = acc * LOG2_E


def _bias_tiles(rel_bias):
    return pl.pallas_call(
        _bias_kernel,
        in_specs=[pl.BlockSpec(memory_space=pltpu.SMEM)],
        out_specs=pl.BlockSpec(memory_space=pltpu.VMEM),
        out_shape=jax.ShapeDtypeStruct((3, N_HEADS_A, QB, QB), jnp.float32),
        name="bias_tiles",
    )(rel_bias)


def _proj_kernel(x_ref, mem_ref, wm_ref, qg_ref, kvg_ref, cw_ref, wmk_ref, wmv_ref,
                 cq_ref, ckv_ref, ckvt_ref, kidx_ref, iwt_ref, yb_ref, yc_ref,
                 carry_ref, mk_ref, mv_ref, *, tm):
    si = pl.program_id(1)

    @pl.when(si == 0)
    def _():
        carry_ref[...] = jnp.zeros_like(carry_ref)
        mb = mem_ref[0].astype(MXU_DTYPE)
        mk_ref[...] = _dot(mb, wmk_ref[...]).astype(MXU_DTYPE)
        mv_ref[...] = _dot(mb, wmv_ref[...]).astype(MXU_DTYPE)

    xb = x_ref[...].astype(MXU_DTYPE)
    p = _dot(xb, wm_ref[...])
    small = p[:, p.shape[1] - LANES:]

    o = 0
    cq = p[:, o:o + Q_RANK]; o += Q_RANK
    ckv = p[:, o:o + KV_RANK]; o += KV_RANK
    g_b = p[:, o:o + CONV_CH]; o += CONV_CH
    g_c = p[:, o:o + CONV_CH]; o += CONV_CH
    h_c = p[:, o:o + CONV_CH]; o += CONV_CH
    q_mem = p[:, o:o + MIX_C]

    cq = cq * lax.rsqrt(jnp.mean(cq * cq, axis=-1, keepdims=True) + RMS_EPS) * qg_ref[...]
    ckv = ckv * lax.rsqrt(jnp.mean(ckv * ckv, axis=-1, keepdims=True) + RMS_EPS) * kvg_ref[...]
    cq_ref[...] = cq.astype(MXU_DTYPE)
    ckv_b = ckv.astype(MXU_DTYPE)
    ckv_ref[...] = ckv_b
    ckvt_ref[0] = ckv.T.astype(MXU_DTYPE)

    kidx_ref[...] = small[:, :IDX_DIM].astype(MXU_DTYPE)
    small_t = small.T
    iwt_ref[0] = small_t[IDX_DIM:IDX_DIM + N_IDX_HEADS, :] * (N_IDX_HEADS ** -0.5 * IDX_DIM ** -0.5)

    u = g_c * h_c
    rows = lax.broadcasted_iota(jnp.int32, (tm, 1), 0)
    c6 = carry_ref[SUBLANES - 2:SUBLANES - 1, :]
    c7 = carry_ref[SUBLANES - 1:SUBLANES, :]
    u1 = jnp.where(rows == 0, c7, pltpu.roll(u, 1, 0))
    u2 = jnp.where(rows == 0, c6, jnp.where(rows == 1, c7, pltpu.roll(u, 2, 0)))
    y = cw_ref[0:1, :] * u2
    y = y + cw_ref[1:2, :] * u1
    y = y + cw_ref[2:3, :] * u
    yb_ref[...] = (g_b * y).astype(MXU_DTYPE)
    carry_ref[...] = u[tm - SUBLANES:, :]

    qm = q_mem.astype(MXU_DTYPE)
    outs = []
    for h in range(N_MEM_HEADS):
        sl = slice(h * HEAD_DIM, (h + 1) * HEAD_DIM)
        lg = _dot_nt(qm[:, sl], mk_ref[:, sl]) * (HEAD_DIM ** -0.5)
        lg = lg - jnp.max(lg, axis=-1, keepdims=True)
        e = jnp.exp(lg)
        pr = e / jnp.sum(e, axis=-1, keepdims=True)
        outs.append(_dot(pr.astype(MXU_DTYPE), mv_ref[:, sl]))
    yc_ref[...] = jnp.concatenate(outs, axis=-1).astype(MXU_DTYPE)


def _proj(x2, mem, w_main, q_g, kv_g, conv_w, w_mk, w_mv, B, S, tm):
    T, D = x2.shape
    n_mem = mem.shape[1]
    ns = S // tm
    row = lambda b, s: (b * ns + s, 0)
    const2 = lambda b, s: (0, 0)
    bf = MXU_DTYPE
    return pl.pallas_call(
        functools.partial(_proj_kernel, tm=tm),
        grid=(B, ns),
        in_specs=[
            pl.BlockSpec((tm, D), row),
            pl.BlockSpec((1, n_mem, D), lambda b, s: (b, 0, 0)),
            pl.BlockSpec(w_main.shape, const2),
            pl.BlockSpec(q_g.shape, const2),
            pl.BlockSpec(kv_g.shape, const2),
            pl.BlockSpec(conv_w.shape, const2),
            pl.BlockSpec(w_mk.shape, const2),
            pl.BlockSpec(w_mv.shape, const2),
        ],
        out_specs=[
            pl.BlockSpec((tm, Q_RANK), row),
            pl.BlockSpec((tm, KV_RANK), row),
            pl.BlockSpec((1, KV_RANK, tm), lambda b, s: (b, 0, s)),
            pl.BlockSpec((tm, IDX_DIM), row),
            pl.BlockSpec((1, N_IDX_HEADS, tm), lambda b, s: (b, 0, s)),
            pl.BlockSpec((tm, CONV_CH), row),
            pl.BlockSpec((tm, MIX_C), row),
        ],
        out_shape=[
            jax.ShapeDtypeStruct((T, Q_RANK), bf),
            jax.ShapeDtypeStruct((T, KV_RANK), bf),
            jax.ShapeDtypeStruct((B, KV_RANK, S), bf),
            jax.ShapeDtypeStruct((T, IDX_DIM), bf),
            jax.ShapeDtypeStruct((B, N_IDX_HEADS, S), jnp.float32),
            jax.ShapeDtypeStruct((T, CONV_CH), bf),
            jax.ShapeDtypeStruct((T, MIX_C), bf),
        ],
        scratch_shapes=[
            pltpu.VMEM((SUBLANES, CONV_CH), jnp.float32),
            pltpu.VMEM((n_mem, MIX_C), bf),
            pltpu.VMEM((n_mem, MIX_C), bf),
        ],
        compiler_params=_cparams(("arbitrary", "arbitrary")),
        name="proj",
    )(x2, mem, w_main, q_g, kv_g, conv_w, w_mk, w_mv)


def _key_to_f32(key):
    bits = jnp.where(key < 0, key ^ jnp.int32(0x7FFFFFFF), key)
    return pltpu.bitcast(bits, jnp.float32)


def _colsum8(v):
    return jnp.sum(v.reshape(QB // SUBLANES, SUBLANES, QB), axis=0)


def _colmax8(v):
    return jnp.max(v.reshape(QB // SUBLANES, SUBLANES, QB), axis=0)


UNROLL_WIDTHS = (8, 4, 2, 1)


def _dsa_kernel(cq_ref, iwt_ref, kidx_ref, ckv_ref, ckvt_ref, wqi_ref, wuq_ref, wuk_ref, wuvt_ref,
                bias_ref, o_ref, wfold_ref, qidx_ref, qlat_ref, score_ref, logit_ref, acc_ref,
                *, k_sel, idx_bits):
    i = pl.program_id(1)
    f32 = jnp.float32
    bf = MXU_DTYPE
    n_blocks = i + 1
    n_blocks = n_blocks + jnp.where((n_blocks % 4 == 3) & (n_blocks < pl.num_programs(1)), 1, 0)
    s_loc = lax.broadcasted_iota(jnp.int32, (QB, QB), 0)
    t_glob = i * QB + lax.broadcasted_iota(jnp.int32, (QB, QB), 1)

    def blk(jb):
        return pl.multiple_of(jb * QB, QB)

    def block_loop(fn, init):
        c, start = init, 0
        for width in UNROLL_WIDTHS:
            n = (n_blocks - start) // width
            c = lax.fori_loop(0, n, lambda it, c, w=width, s=start: fn(s + it * w, w, c), c)
            start = start + n * width
        return c

    @pl.when(i == 0)
    def _():
        for h in range(N_HEADS_A):
            wfold_ref[:, h * KV_RANK:(h + 1) * KV_RANK] = (
                _dot_nt(wuq_ref[:, h * HEAD_DIM:(h + 1) * HEAD_DIM], wuk_ref[h])
                * (HEAD_DIM ** -0.5 * LOG2_E)).astype(bf)

    cq = cq_ref[...]
    q_idx = _dot(cq, wqi_ref[...]).astype(bf)
    q_lat = _dot(cq, wfold_ref[...]).astype(bf)
    for h in range(N_HEADS_A):
        qidx_ref[h * QB:(h + 1) * QB, :] = q_idx[:, h * IDX_DIM:(h + 1) * IDX_DIM]
        qlat_ref[h * QB:(h + 1) * QB, :] = q_lat[:, h * KV_RANK:(h + 1) * KV_RANK]
    iw = iwt_ref[0]

    def score_body(jb0, nb, n_pos8):
        d_blk = _dot_nt(kidx_ref[pl.ds(blk(jb0), nb * QB), :], qidx_ref[...])
        for sb in range(nb):
            off = blk(jb0 + sb)
            d_all = d_blk[sb * QB:(sb + 1) * QB, :]
            acc = jnp.maximum(d_all[:, 0:QB], 0.0) * iw[0:1, :]
            for h in range(1, N_IDX_HEADS):
                acc = acc + jnp.maximum(d_all[:, h * QB:(h + 1) * QB], 0.0) * iw[h:h + 1, :]
            sc = jnp.where(s_loc + off <= t_glob, acc + 0.0, F32_LOWEST)
            score_ref[pl.ds(off, QB), :] = sc
            n_pos8 = n_pos8 + _colsum8(jnp.where(sc >= 0.0, 1.0, 0.0))
        return n_pos8

    n_pos8 = block_loop(score_body, jnp.zeros((SUBLANES, QB), f32))

    def count_where(pred):
        def body(jb0, nb, acc):
            for sb in range(nb):
                off = blk(jb0 + sb)
                acc = acc + _colsum8(jnp.where(pred(score_ref[pl.ds(off, QB), :], off), 1.0, 0.0))
            return acc
        acc = block_loop(body, jnp.zeros((SUBLANES, QB), f32))
        return jnp.sum(acc, axis=0, keepdims=True)

    kf = float(k_sel)

    def search():
        c0 = jnp.sum(n_pos8, axis=0, keepdims=True)
        cand0 = jnp.where(c0 >= kf, jnp.int32(0), jnp.int32(-2 ** 31))
        n_ge0 = jnp.where(c0 >= kf, c0, -1.0)

        def bit_body(it, carry):
            cand, n_ge = carry
            trial = cand + lax.shift_left(jnp.int32(1), 30 - it)
            tf = _key_to_f32(trial)
            cnt = count_where(lambda sc, off: sc >= tf)
            take = cnt >= kf
            return jnp.where(take, trial, cand), jnp.where(take, cnt, n_ge)

        cand, n_ge = lax.fori_loop(0, 31, bit_body, (cand0, n_ge0))
        thr = _key_to_f32(cand)
        keep_all_ties = jnp.full((1, QB), 2 ** idx_bits - 1, jnp.int32)

        def resolve_ties():
            n_gt = count_where(lambda sc, off: sc > thr)
            n_eq = count_where(lambda sc, off: sc == thr)
            need = kf - n_gt

            def tie_search():
                def tbody(it, xcut):
                    trial = xcut + lax.shift_left(jnp.int32(1), idx_bits - 1 - it)
                    cnt = count_where(lambda sc, off: (sc == thr) & (s_loc + off < trial))
                    return jnp.where(cnt < need, trial, xcut)
                return lax.fori_loop(0, idx_bits, tbody, jnp.zeros((1, QB), jnp.int32))

            return lax.cond(jnp.max(n_eq - need) > 0.0, tie_search, lambda: keep_all_ties)

        plain = jnp.max(jnp.abs(n_ge - kf)) == 0.0
        xcut = lax.cond(plain, lambda: keep_all_ties, resolve_ties)
        return thr, xcut, plain

    def no_search():
        return (jnp.full((1, QB), F32_LOWEST, f32), jnp.full((1, QB), 2 ** idx_bits - 1, jnp.int32),
                jnp.zeros((), jnp.bool_))

    thr, xcut, plain = lax.cond((i + 1) * QB > k_sel, search, no_search)

    def general_mask(off):
        sc = score_ref[pl.ds(off, QB), :]
        s_glob = s_loc + off
        keep = ((sc > thr) | ((sc == thr) & (s_glob <= xcut))) & (s_glob <= t_glob)
        return jnp.where(keep, 0.0, -jnp.inf)

    def plain_mask(off):
        return jnp.where(score_ref[pl.ds(off, QB), :] >= thr, 0.0, -jnp.inf)

    acc_ref[...] = jnp.zeros_like(acc_ref)

    def att_body(selection_mask, jb0, nb, carry):
        m, l8 = list(carry[0]), list(carry[1])
        rows = nb * QB
        lg_blk = _dot_nt(ckv_ref[pl.ds(blk(jb0), rows), :], qlat_ref[...])
        blk_max = [None] * N_HEADS_A
        for sb in range(nb):
            off = blk(jb0 + sb)
            msk = selection_mask(off)
            bsel = jnp.clip(jb0 + sb - i + 2, 0, 2)
            for h in range(N_HEADS_A):
                lgh = lg_blk[sb * QB:(sb + 1) * QB, h * QB:(h + 1) * QB] + bias_ref[bsel, h] + msk
                logit_ref[sb * QB:(sb + 1) * QB, h * QB:(h + 1) * QB] = lgh
                cm = _colmax8(lgh)
                blk_max[h] = cm if blk_max[h] is None else jnp.maximum(blk_max[h], cm)
        ps, scales = [], []
        for h in range(N_HEADS_A):
            m_new = jnp.maximum(m[h], jnp.max(blk_max[h], axis=0, keepdims=True))
            m_ref = jnp.where(m_new == -jnp.inf, 0.0, m_new)
            p = jnp.exp2(logit_ref[0:rows, h * QB:(h + 1) * QB] - m_ref)
            scale = jnp.exp2(m[h] - m_ref)
            l8[h] = l8[h] * scale + jnp.sum(p.reshape(rows // SUBLANES, SUBLANES, QB), axis=0)
            m[h] = m_new
            ps.append(p.astype(bf))
            scales.append(scale)
        pv = _dot(ckvt_ref[0, :, pl.ds(blk(jb0), rows)], jnp.concatenate(ps, axis=1))
        for h in range(N_HEADS_A):
            hs = slice(h * QB, (h + 1) * QB)
            acc_ref[:, hs] = acc_ref[:, hs] * scales[h] + pv[:, hs]
        return tuple(m), tuple(l8)

    carry0 = (tuple(jnp.full((1, QB), -jnp.inf, f32) for _ in range(N_HEADS_A)),
              tuple(jnp.zeros((SUBLANES, QB), f32) for _ in range(N_HEADS_A)))
    _, l8 = lax.cond(plain,
                     lambda: block_loop(functools.partial(att_body, plain_mask), carry0),
                     lambda: block_loop(functools.partial(att_body, general_mask), carry0))

    outs = []
    for h in range(N_HEADS_A):
        l_row = jnp.sum(l8[h], axis=0, keepdims=True)
        o_lat_t = (acc_ref[:, h * QB:(h + 1) * QB] / l_row).astype(bf)
        outs.append(_dot(wuvt_ref[h], o_lat_t))
    o_ref[...] = jnp.concatenate(outs, axis=0).T.astype(o_ref.dtype)


def _dsa(cq, iwt, kidx, ckv, ckvt, w_qidx, w_uq, w_uk_h, w_uvt_h, bias_tiles, B, S):
    T = cq.shape[0]
    assert S % QB == 0 and QB >= REL_MAX_DIST
    nq = S // QB
    k_sel = min(TOPK_MAX, S // 4)
    idx_bits = max(1, (S - 1).bit_length())
    c2 = lambda b, i: (0, 0)
    c3 = lambda b, i: (0, 0, 0)
    return pl.pallas_call(
        functools.partial(_dsa_kernel, k_sel=k_sel, idx_bits=idx_bits),
        grid=(B, nq),
        in_specs=[
            pl.BlockSpec((QB, Q_RANK), lambda b, i: (b * nq + i, 0)),
            pl.BlockSpec((1, N_IDX_HEADS, QB), lambda b, i: (b, 0, i)),
            pl.BlockSpec((S, IDX_DIM), lambda b, i: (b, 0)),
            pl.BlockSpec((S, KV_RANK), lambda b, i: (b, 0)),
            pl.BlockSpec((1, KV_RANK, S), lambda b, i: (b, 0, 0)),
            pl.BlockSpec(w_qidx.shape, c2),
            pl.BlockSpec(w_uq.shape, c2),
            pl.BlockSpec(w_uk_h.shape, c3),
            pl.BlockSpec(w_uvt_h.shape, c3),
            pl.BlockSpec(bias_tiles.shape, lambda b, i: (0, 0, 0, 0)),
        ],
        out_specs=pl.BlockSpec((QB, MIX_A), lambda b, i: (b * nq + i, 0)),
        out_shape=jax.ShapeDtypeStruct((T, MIX_A), MXU_DTYPE),
        scratch_shapes=[
            pltpu.VMEM((Q_RANK, N_HEADS_A * KV_RANK), MXU_DTYPE),
            pltpu.VMEM((N_IDX_HEADS * QB, IDX_DIM), MXU_DTYPE),
            pltpu.VMEM((N_HEADS_A * QB, KV_RANK), MXU_DTYPE),
            pltpu.VMEM((S, QB), jnp.float32),
            pltpu.VMEM((max(UNROLL_WIDTHS) * QB, N_HEADS_A * QB), jnp.float32),
            pltpu.VMEM((KV_RANK, N_HEADS_A * QB), jnp.float32),
        ],
        compiler_params=_cparams(("arbitrary", "arbitrary")),
        name="dsa",
    )(cq, iwt, kidx, ckv, ckvt, w_qidx, w_uq, w_uk_h, w_uvt_h, bias_tiles)


def _layer_norm(xf, g, b):
    mu = jnp.mean(xf, axis=-1, keepdims=True)
    xc = xf - mu
    var = jnp.mean(xc * xc, axis=-1, keepdims=True)
    return xc * lax.rsqrt(var + LN_EPS) * g + b


def _rank_rows(v, n):
    ri = lax.broadcasted_iota(jnp.int32, v.shape, 0)
    rank = jnp.zeros(v.shape, jnp.float32)
    for r2 in range(n):
        row = v[r2:r2 + 1, :]
        beats = (row > v) | ((row == v) & (ri > r2))
        rank = rank + jnp.where(beats, 1.0, 0.0)
    return rank


def _top_rows(v, k):
    n = v.shape[0]
    ri = lax.broadcasted_iota(jnp.int32, v.shape, 0)
    sel = jnp.zeros(v.shape, jnp.float32)
    for _ in range(k):
        m = jnp.max(v, axis=0, keepdims=True)
        first = jnp.min(jnp.where(v == m, ri, n), axis=0, keepdims=True)
        pick = ri == first
        sel = jnp.where(pick, 1.0, sel)
        v = jnp.where(pick, -jnp.inf, v)
    return sel > 0.5


def _pack_factor():
    return 4 // jnp.dtype(MXU_DTYPE).itemsize


def _pack_rows(x):
    if _pack_factor() == 1:
        return pltpu.bitcast(x, jnp.int32)
    half = x.shape[1] // 2
    b = pltpu.bitcast(x.astype(MXU_DTYPE).astype(jnp.float32), jnp.int32)
    return b[:, half:] | lax.shift_right_logical(b[:, :half], jnp.int32(16))


_HIGH_HALF = -(1 << 16)


def _unpack_rows_f32(p):
    if _pack_factor() == 1:
        return [pltpu.bitcast(p, jnp.float32)]
    lo = pltpu.bitcast(lax.shift_left(p, jnp.int32(16)), jnp.float32)
    hi = pltpu.bitcast(p & jnp.int32(_HIGH_HALF), jnp.float32)
    return [lo, hi]


def _unpack_rows(p):
    return [v.astype(MXU_DTYPE) for v in _unpack_rows_f32(p)]


def _mix_router_kernel(x_ref, ya_ref, yb_ref, yc_ref, wo_ref, g_ref, b_ref, wrt_ref, rb_ref, exp_ref,
                       x1_ref, x1p_ref, sel_ref, w_ref, pos_ref, cnt_ref, base_ref, *, tm):
    step = pl.program_id(0)
    f32 = jnp.float32

    @pl.when(step == 0)
    def _():
        base_ref[...] = jnp.zeros_like(base_ref)

    mix = _dot(ya_ref[...], wo_ref[0:MIX_A, :])
    mix = mix + _dot(yb_ref[...], wo_ref[MIX_A:MIX_A + CONV_CH, :])
    mix = mix + _dot(yc_ref[...], wo_ref[MIX_A + CONV_CH:, :])
    x1 = _layer_norm(ALPHA * x_ref[...] + mix, g_ref[...], b_ref[...])
    x1_ref[...] = x1
    x1p_ref[...] = _pack_rows(x1)

    lg = lax.dot_general(wrt_ref[...], x1, _NT, precision=lax.Precision.HIGHEST, preferred_element_type=f32)
    s = 1.0 / (1.0 + jnp.exp(-lg))
    sc = s + rb_ref[...]

    g3 = sc.reshape(N_GROUPS, GROUP_SIZE, tm)
    m1 = jnp.max(g3, axis=1, keepdims=True)
    is_m1 = g3 == m1
    n_m1 = jnp.sum(jnp.where(is_m1, 1.0, 0.0), axis=1, keepdims=True)
    m2 = jnp.max(jnp.where(is_m1, -jnp.inf, g3), axis=1, keepdims=True)
    gscore = (m1 + jnp.where(n_m1 > 1.0, m1, m2)).reshape(N_GROUPS, tm)
    gsel = jnp.where(_rank_rows(gscore, N_GROUPS) < float(TOPK_GROUPS), 1.0, 0.0)
    emask = _dot(exp_ref[...], gsel.astype(MXU_DTYPE)) > 0.5
    masked = jnp.where(emask, sc, -jnp.inf)
    sel = _top_rows(masked, TOP_K) & emask
    self_ = jnp.where(sel, 1.0, 0.0)
    top_s = jnp.where(sel, s, 0.0)
    w = top_s / jnp.sum(top_s, axis=0, keepdims=True) * ROUTED_SCALE

    t_r = lax.broadcasted_iota(jnp.int32, (tm, tm), 0)
    t_c = lax.broadcasted_iota(jnp.int32, (tm, tm), 1)
    upper = jnp.where(t_r < t_c, 1.0, 0.0).astype(MXU_DTYPE)
    pref = _dot(self_.astype(MXU_DTYPE), upper)
    base = base_ref[...]
    sel_ref[...] = self_
    w_ref[...] = w
    pos_ref[...] = base + pref
    base = base + jnp.sum(self_, axis=1, keepdims=True)
    base_ref[...] = base
    cnt_ref[...] = jnp.broadcast_to(base, cnt_ref.shape)


def _mix_router(x2, ya, yb, yc, w_out, ln_g, ln_b, w_router_t, router_bias, tm):
    T, D = x2.shape
    E = N_EXPERTS
    expand = (jnp.arange(E)[:, None] // GROUP_SIZE == jnp.arange(N_GROUPS)[None, :]).astype(MXU_DTYPE)
    row = lambda i: (i, 0)
    col = lambda i: (0, i)
    c2 = lambda i: (0, 0)
    f32 = jnp.float32
    return pl.pallas_call(
        functools.partial(_mix_router_kernel, tm=tm),
        grid=(T // tm,),
        in_specs=[
            pl.BlockSpec((tm, D), row),
            pl.BlockSpec((tm, MIX_A), row),
            pl.BlockSpec((tm, CONV_CH), row),
            pl.BlockSpec((tm, MIX_C), row),
            pl.BlockSpec(w_out.shape, c2),
            pl.BlockSpec((1, D), c2),
            pl.BlockSpec((1, D), c2),
            pl.BlockSpec((E, D), c2),
            pl.BlockSpec((E, 1), c2),
            pl.BlockSpec((E, N_GROUPS), c2),
        ],
        out_specs=[
            pl.BlockSpec((tm, D), row),
            pl.BlockSpec((tm, D // _pack_factor()), row),
            pl.BlockSpec((E, tm), col),
            pl.BlockSpec((E, tm), col),
            pl.BlockSpec((E, tm), col),
            pl.BlockSpec((E, LANES), c2),
        ],
        out_shape=[
            jax.ShapeDtypeStruct((T, D), f32),
            jax.ShapeDtypeStruct((T, D // _pack_factor()), jnp.int32),
            jax.ShapeDtypeStruct((E, T), f32),
            jax.ShapeDtypeStruct((E, T), f32),
            jax.ShapeDtypeStruct((E, T), f32),
            jax.ShapeDtypeStruct((E, LANES), f32),
        ],
        scratch_shapes=[pltpu.VMEM((E, 1), f32)],
        compiler_params=_cparams(("arbitrary",)),
        name="mix_router",
    )(x2, ya, yb, yc, w_out, ln_g, ln_b, w_router_t, router_bias, expand)


def _compact_kernel(sel_ref, w_ref, pos_ref, pstart_ref, low_ref, dest_ref, wk_ref):
    sel = sel_ref[...]
    on = sel > 0.5
    rank = _dot(low_ref[...], sel.astype(MXU_DTYPE))
    row = pstart_ref[...] + pos_ref[...]
    w = w_ref[...]
    dests, ws = [], []
    for k in range(TOP_K):
        m = on & (rank == float(k))
        dests.append(jnp.sum(jnp.where(m, row, 0.0), axis=0, keepdims=True))
        ws.append(jnp.sum(jnp.where(m, w, 0.0), axis=0, keepdims=True))
    dest_ref[...] = jnp.concatenate(dests, axis=0).astype(jnp.int32)
    wk_ref[...] = jnp.concatenate(ws, axis=0)


def _compact(sel_t, w_t, pos_t, pad_start, tm):
    E, T = sel_t.shape
    lower = (jnp.arange(E)[None, :] < jnp.arange(E)[:, None]).astype(MXU_DTYPE)
    col = lambda i: (0, i)
    c2 = lambda i: (0, 0)
    return pl.pallas_call(
        _compact_kernel,
        grid=(T // tm,),
        in_specs=[pl.BlockSpec((E, tm), col), pl.BlockSpec((E, tm), col), pl.BlockSpec((E, tm), col),
                  pl.BlockSpec((E, 1), c2), pl.BlockSpec((E, E), c2)],
        out_specs=[pl.BlockSpec((TOP_K, tm), col), pl.BlockSpec((TOP_K, tm), col)],
        out_shape=[jax.ShapeDtypeStruct((TOP_K, T), jnp.int32), jax.ShapeDtypeStruct((TOP_K, T), jnp.float32)],
        compiler_params=_cparams(("arbitrary",)),
        name="route_compact",
    )(sel_t, w_t, pos_t, pad_start, lower)


def _silu(g):
    return g / (1.0 + jnp.exp(-g))


def _expert_kernel(be_ref, nv_ref, nu_ref, xs_ref, wg_ref, wu_ref, wd_ref, ys_ref, wgb_ref, wub_ref, wdb_ref):
    i = pl.program_id(0)

    @pl.when((i == 0) | (be_ref[i] != be_ref[jnp.maximum(i - 1, 0)]))
    def _():
        wgb_ref[...] = wg_ref[0].astype(MXU_DTYPE)
        wub_ref[...] = wu_ref[0].astype(MXU_DTYPE)
        wdb_ref[...] = wd_ref[0].astype(MXU_DTYPE)

    @pl.when(i < nu_ref[0])
    def _():
        live = lax.broadcasted_iota(jnp.int32, (ROW_BLOCK, 1), 0) < nv_ref[i]
        parts = [jnp.where(live, v, jnp.zeros_like(v)) for v in _unpack_rows(xs_ref[...])]
        dk = wgb_ref.shape[0] // len(parts)

        def proj(w_ref):
            acc = _dot(parts[0], w_ref[0:dk, :])
            for n in range(1, len(parts)):
                acc = acc + _dot(parts[n], w_ref[n * dk:(n + 1) * dk, :])
            return acc

        a = (_silu(proj(wgb_ref)) * proj(wub_ref)).astype(MXU_DTYPE)
        ys_ref[...] = _pack_rows(_dot(a, wdb_ref[...]))


def _experts(xs, block_e, block_valid, n_used, w_gate, w_up, w_down):
    n_rows, W = xs.shape
    D = w_gate.shape[1]
    n_blocks = n_rows // ROW_BLOCK
    blk = lambda i, be, nv, nu: (jnp.minimum(i, nu[0] - 1), 0)
    wsel = lambda i, be, nv, nu: (be[i], 0, 0)
    return pl.pallas_call(
        _expert_kernel,
        grid_spec=pltpu.PrefetchScalarGridSpec(
            num_scalar_prefetch=3,
            grid=(n_blocks,),
            in_specs=[
                pl.BlockSpec((ROW_BLOCK, W), blk),
                pl.BlockSpec((1, D, D_EXPERT), wsel),
                pl.BlockSpec((1, D, D_EXPERT), wsel),
                pl.BlockSpec((1, D_EXPERT, D), wsel),
            ],
            out_specs=pl.BlockSpec((ROW_BLOCK, W), blk),
            scratch_shapes=[pltpu.VMEM((D, D_EXPERT), MXU_DTYPE), pltpu.VMEM((D, D_EXPERT), MXU_DTYPE),
                            pltpu.VMEM((D_EXPERT, D), MXU_DTYPE)],
        ),
        out_shape=jax.ShapeDtypeStruct((n_rows, W), xs.dtype),
        compiler_params=_cparams(("arbitrary",)),
        name="experts",
    )(block_e, block_valid, n_used, xs, w_gate, w_up, w_down)


SC_CORES = 2
SC_SUBCORES = 16
SC_GATHER_ROWS = 64
COMBINE_CHUNKS = 8


def _sc_gather_rows(table, idx):
    n = idx.shape[0]
    w = table.shape[1]
    n_workers = SC_CORES * SC_SUBCORES
    per_worker = n // n_workers
    assert n % n_workers == 0 and per_worker % SC_GATHER_ROWS == 0
    mesh = plsc.VectorSubcoreMesh(core_axis_name="c", subcore_axis_name="s")

    @functools.partial(
        pl.kernel, mesh=mesh,
        out_type=jax.ShapeDtypeStruct((n, w), table.dtype),
        scratch_types=[
            pltpu.VMEM((2, SC_GATHER_ROWS), jnp.int32),
            pltpu.VMEM((2, SC_GATHER_ROWS, w), table.dtype),
            pltpu.SemaphoreType.DMA((2,)),
        ],
        name="sc_gather_rows",
    )
    def gather(table_hbm, idx_hbm, out_hbm, idx_v, rows_v, sem):
        wid = lax.axis_index("s") * SC_CORES + lax.axis_index("c")
        base = wid * per_worker
        n_steps = per_worker // SC_GATHER_ROWS

        def gather_copy(slot):
            return pltpu.make_async_copy(table_hbm.at[idx_v.at[slot]], rows_v.at[slot], sem.at[slot])

        def start(step, slot):
            pltpu.sync_copy(idx_hbm.at[pl.ds(base + step * SC_GATHER_ROWS, SC_GATHER_ROWS)], idx_v.at[slot])
            gather_copy(slot).start()

        start(0, 0)

        @pl.loop(0, n_steps, step=2)
        def _(g):
            for slot in range(2):
                step = g + slot

                @pl.when(step + 1 < n_steps)
                def _():
                    start(step + 1, 1 - slot)

                gather_copy(slot).wait()
                pltpu.sync_copy(rows_v.at[slot], out_hbm.at[pl.ds(base + step * SC_GATHER_ROWS, SC_GATHER_ROWS)])

    return gather(table, idx)


SC_SCATTER_ROWS = 64


def _sc_scatter_rows(rows, idx3, n_out):
    n_src, w = rows.shape
    n_chunks, n_dst, batch = idx3.shape
    n_workers = SC_CORES * SC_SUBCORES
    assert batch == SC_SCATTER_ROWS and n_chunks * batch == n_src and n_chunks % (2 * n_workers) == 0
    per_worker = n_chunks // n_workers
    mesh = plsc.VectorSubcoreMesh(core_axis_name="c", subcore_axis_name="s")

    @functools.partial(
        pl.kernel, mesh=mesh,
        out_type=jax.ShapeDtypeStruct((n_out, w), rows.dtype),
        scratch_types=[
            pltpu.VMEM((2, n_dst, batch), jnp.int32),
            pltpu.VMEM((2, batch, w), rows.dtype),
            pltpu.SemaphoreType.DMA((2,)),
            pltpu.SemaphoreType.DMA,
        ],
        name="sc_scatter_rows",
    )
    def scatter(rows_hbm, idx_hbm, out_hbm, idx_v, rows_v, load_sem, store_sem):
        wid = lax.axis_index("s") * SC_CORES + lax.axis_index("c")

        def load_copy(step, slot):
            c = wid * per_worker + step
            return pltpu.make_async_copy(rows_hbm.at[pl.ds(c * batch, batch)], rows_v.at[slot], load_sem.at[slot])

        def load(step, slot):
            pltpu.sync_copy(idx_hbm.at[wid * per_worker + step], idx_v.at[slot])
            load_copy(step, slot).start()

        def store_copy(slot, k):
            return pltpu.make_async_copy(rows_v.at[slot], out_hbm.at[idx_v.at[slot].at[k]], store_sem)

        load(0, 0)

        @pl.loop(0, per_worker, step=2)
        def _(g):
            for slot in range(2):
                step = g + slot

                @pl.when(step + 1 < per_worker)
                def _():
                    load(step + 1, 1 - slot)

                load_copy(step, slot).wait()
                for k in range(n_dst):
                    store_copy(slot, k).start()
                for k in range(n_dst):
                    store_copy(slot, k).wait()

    return scatter(rows, idx3)


def _combine2_kernel(wk_ref, x1_ref, g_ref_rows, wsg_ref, wsu_ref, wsd_ref, g_ref, b_ref, o_ref):
    x1 = x1_ref[...]
    xb = x1.astype(MXU_DTYPE)
    a = (_silu(_dot(xb, wsg_ref[...])) * _dot(xb, wsu_ref[...])).astype(MXU_DTYPE)
    shared = _dot(a, wsd_ref[...])
    wk = wk_ref[...].T
    groups = [wk[:, 0:1] * v for v in _unpack_rows_f32(g_ref_rows[0])]
    for k in range(1, TOP_K):
        groups = [g + wk[:, k:k + 1] * v for g, v in zip(groups, _unpack_rows_f32(g_ref_rows[k]))]
    routed = jnp.concatenate(groups, axis=1)
    o_ref[...] = _layer_norm(ALPHA * x1 + (routed + shared), g_ref[...], b_ref[...])


def _combine2_kernel_into(wk_ref, x1_ref, g_ref_rows, wsg_ref, wsu_ref, wsd_ref, g_ref, b_ref, prev_ref, o_ref):
    del prev_ref
    _combine2_kernel(wk_ref, x1_ref, g_ref_rows, wsg_ref, wsu_ref, wsd_ref, g_ref, b_ref, o_ref)


def _combine2(wk_t, x1, gathered, w_sg, w_su, w_sd, ln_g, ln_b, tc, chunk, prev):
    T, D = x1.shape
    _, t_chunk, W = gathered.shape
    base = chunk * (t_chunk // tc)
    row = lambda i: (base + i, 0)
    c2 = lambda i: (0, 0)
    in_specs = [
        pl.BlockSpec((TOP_K, tc), lambda i: (0, base + i)),
        pl.BlockSpec((tc, D), row),
        pl.BlockSpec((TOP_K, tc, W), lambda i: (0, i, 0)),
        pl.BlockSpec(w_sg.shape, c2),
        pl.BlockSpec(w_su.shape, c2),
        pl.BlockSpec(w_sd.shape, c2),
        pl.BlockSpec((1, D), c2),
        pl.BlockSpec((1, D), c2),
    ]
    args = [wk_t, x1, gathered, w_sg, w_su, w_sd, ln_g, ln_b]
    if prev is None:
        body, aliases = _combine2_kernel, {}
    else:
        body, aliases = _combine2_kernel_into, {len(args): 0}
        in_specs.append(pl.BlockSpec(memory_space=pl.ANY))
        args.append(prev)
    return pl.pallas_call(
        body,
        grid=(t_chunk // tc,),
        in_specs=in_specs,
        out_specs=pl.BlockSpec((tc, D), row),
        out_shape=jax.ShapeDtypeStruct((T, D), jnp.float32),
        input_output_aliases=aliases,
        compiler_params=_cparams(("arbitrary",)),
        name="combine",
    )(*args)


def _split_w_in(w_in):
    o_kv = Q_RANK
    o_ki = o_kv + KV_RANK
    o_iw = o_ki + IDX_DIM
    o_rest = o_iw + N_IDX_HEADS
    w_small = jnp.pad(w_in[:, o_ki:o_rest], ((0, 0), (0, LANES - IDX_DIM - N_IDX_HEADS)))
    return jnp.concatenate([w_in[:, :o_ki], w_in[:, o_rest:], w_small], axis=1).astype(MXU_DTYPE)


def _stages(x, mem, w_in, q_norm_g, kv_norm_g, w_uq, w_uk, w_uv, w_qidx, rel_bias, conv_w, w_mem_k, w_mem_v, w_out, ln1_g, ln1_b, w_router, router_bias, w_e_gate, w_e_up, w_e_down, w_s_gate, w_s_up, w_s_down, ln2_g, ln2_b):
    B, S, D = x.shape
    T = B * S
    bf = MXU_DTYPE
    assert w_in.shape[0] == DEPTH == 1, "single-layer stack"
    l = 0
    res = {}
    x2 = x.reshape(T, D)
    cq, ckv, ckvt, kidx, iwt, yb, yc = _proj(
        x2, mem, _split_w_in(w_in[l]), q_norm_g[l].reshape(1, -1), kv_norm_g[l].reshape(1, -1), conv_w[l],
        w_mem_k[l].astype(bf), w_mem_v[l].astype(bf), B, S, tm=min(1024, S))
    res.update(c_q=cq, c_kv=ckv, k_idx=kidx, y_b=yb, y_c=yc,
               idx_w=jnp.swapaxes(iwt, 1, 2) / (N_IDX_HEADS ** -0.5 * IDX_DIM ** -0.5))
    bias_t = _bias_tiles(rel_bias)
    ya = _dsa(cq, iwt, kidx, ckv, ckvt,
              w_qidx[l].reshape(Q_RANK, -1).astype(bf), w_uq[l].reshape(Q_RANK, -1).astype(bf),
              jnp.transpose(w_uk[l], (1, 0, 2)).astype(bf), jnp.transpose(w_uv[l], (1, 2, 0)).astype(bf),
              bias_t, B, S)
    res.update(y_a=ya)

    x1, x1p, sel_t, w_t, pos_t, cnt = _mix_router(
        x2, ya, yb, yc, w_out[l].astype(bf), ln1_g[l].reshape(1, -1), ln1_b[l].reshape(1, -1),
        w_router[l].T, router_bias[l].reshape(-1, 1), tm=min(1024, T))
    res.update(x1=x1)

    counts = cnt[:, 0].astype(jnp.int32)
    padded = (counts + ROW_BLOCK - 1) // ROW_BLOCK * ROW_BLOCK
    pad_end = jnp.cumsum(padded)
    pad_start = pad_end - padded
    n_blocks = -(-(T * TOP_K) // ROW_BLOCK) + N_EXPERTS
    n_rows = n_blocks * ROW_BLOCK
    block_start = jnp.arange(n_blocks, dtype=jnp.int32) * ROW_BLOCK
    block_e = jnp.minimum(jnp.sum((pad_end[None, :] <= block_start[:, None]).astype(jnp.int32), axis=1),
                          N_EXPERTS - 1)
    n_used = (pad_end[-1:] // ROW_BLOCK).astype(jnp.int32)

    dest_t, wk_t = _compact(sel_t, w_t, pos_t, pad_start.astype(jnp.float32).reshape(-1, 1), tm=min(8192, T))
    is_e = block_e[:, None] == jnp.arange(N_EXPERTS, dtype=jnp.int32)[None, :]
    live_end = jnp.sum(jnp.where(is_e, (pad_start + counts)[None, :], 0), axis=1)
    block_valid = jnp.clip(live_end - block_start, 0, ROW_BLOCK).astype(jnp.int32)
    bt = SC_SCATTER_ROWS
    idx3 = jnp.transpose(dest_t.reshape(TOP_K, T // bt, bt), (1, 0, 2))
    xs = _sc_scatter_rows(x1p, idx3, n_rows)
    ys = _experts(xs, block_e, block_valid, n_used, w_e_gate[l], w_e_up[l], w_e_down[l])
    n_chunks = COMBINE_CHUNKS if T % (COMBINE_CHUNKS * 512) == 0 else 1
    t_chunk = T // n_chunks
    out = None
    for c in range(n_chunks):
        idx_c = dest_t[:, c * t_chunk:(c + 1) * t_chunk].reshape(-1)
        gathered = _sc_gather_rows(ys, idx_c).reshape(TOP_K, t_chunk, -1)
        out = _combine2(wk_t, x1, gathered, w_s_gate[l].astype(bf), w_s_up[l].astype(bf), w_s_down[l].astype(bf),
                        ln2_g[l].reshape(1, -1), ln2_b[l].reshape(1, -1), tc=min(512, t_chunk), chunk=c, prev=out)
    res.update(out=out.reshape(B, S, D))
    return res


def kernel(x, mem, w_in, q_norm_g, kv_norm_g, w_uq, w_uk, w_uv, w_qidx, rel_bias, conv_w, w_mem_k, w_mem_v, w_out, ln1_g, ln1_b, w_router, router_bias, w_e_gate, w_e_up, w_e_down, w_s_gate, w_s_up, w_s_down, ln2_g, ln2_b):
    return _stages(x, mem, w_in, q_norm_g, kv_norm_g, w_uq, w_uk, w_uv, w_qidx, rel_bias, conv_w, w_mem_k, w_mem_v, w_out, ln1_g, ln1_b, w_router, router_bias, w_e_gate, w_e_up, w_e_down, w_s_gate, w_s_up, w_s_down, ln2_g, ln2_b)["out"]
```

```python
import functools
import math

import jax
import jax.numpy as jnp
from jax import lax
from jax.experimental import pallas as pl
from jax.experimental.pallas import tpu as pltpu
from jax.experimental.pallas import tpu_sc as plsc

N_HEADS_A = 8
HEAD_DIM = 64
Q_RANK = 256
KV_RANK = 128
N_IDX_HEADS = 8
IDX_DIM = 64
TOPK_MAX = 256
REL_BUCKETS = 32
REL_MAX_DIST = 128
CONV_CH = 256
CONV_WIDTH = 3
N_MEM_HEADS = 4
MIX_A = N_HEADS_A * HEAD_DIM
MIX_C = N_MEM_HEADS * HEAD_DIM
N_EXPERTS = 64
N_GROUPS = 8
GROUP_SIZE = N_EXPERTS // N_GROUPS
TOPK_GROUPS = 4
TOP_K = 8
D_EXPERT = 256
ROUTED_SCALE = 2.5
DEPTH = 1
ALPHA = (2.0 * DEPTH) ** 0.25
LN_EPS = 1e-5
RMS_EPS = 1e-6
LOG2_E = math.log2(math.e)

LANES = 128
SUBLANES = 8
QB = 128
F32_LOWEST = -3.4028234663852886e38
VMEM_LIMIT = 56 * 1024 * 1024
MXU_DTYPE = jnp.bfloat16
ROW_BLOCK = 1024

_NT = (((1,), (1,)), ((), ()))


def _dot(a, b):
    return jnp.dot(a, b, preferred_element_type=jnp.float32)


def _dot_nt(a, b):
    return lax.dot_general(a, b, _NT, preferred_element_type=jnp.float32)


def _cparams(sem):
    return pltpu.CompilerParams(dimension_semantics=sem, vmem_limit_bytes=VMEM_LIMIT)


def _bias_kernel(rb_ref, o_ref):
    s = lax.broadcasted_iota(jnp.int32, (QB, QB), 0)
    t = lax.broadcasted_iota(jnp.int32, (QB, QB), 1)
    max_exact = REL_BUCKETS // 2
    for tile in range(3):
        n = jnp.maximum(t - s + (2 - tile) * QB, 0)
        nf = jnp.maximum(n.astype(jnp.float32), 1.0)
        large = max_exact + (jnp.log(nf / max_exact) / math.log(REL_MAX_DIST / max_exact)
                             * (REL_BUCKETS - max_exact)).astype(jnp.int32)
        large = jnp.minimum(large, REL_BUCKETS - 1)
        bucket = jnp.where(n < max_exact, n, large)
        for h in range(N_HEADS_A):
            acc = jnp.zeros((QB, QB), jnp.float32)
            for b in range(REL_BUCKETS):
                acc = jnp.where(bucket == b, rb_ref[b, h], acc)
            o_ref[tile, h] = acc * LOG2_E


def _bias_tiles(rel_bias):
    return pl.pallas_call(
        _bias_kernel,
        in_specs=[pl.BlockSpec(memory_space=pltpu.SMEM)],
        out_specs=pl.BlockSpec(memory_space=pltpu.VMEM),
        out_shape=jax.ShapeDtypeStruct((3, N_HEADS_A, QB, QB), jnp.float32),
        name="bias_tiles",
    )(rel_bias)


def _proj_kernel(x_ref, mem_ref, wm_ref, qg_ref, kvg_ref, cw_ref, wmk_ref, wmv_ref,
                 cq_ref, ckv_ref, ckvt_ref, kidx_ref, iwt_ref, yb_ref, yc_ref,
                 carry_ref, mk_ref, mv_ref, *, tm):
    si = pl.program_id(1)

    @pl.when(si == 0)
    def _():
        carry_ref[...] = jnp.zeros_like(carry_ref)
        mb = mem_ref[0].astype(MXU_DTYPE)
        mk_ref[...] = _dot(mb, wmk_ref[...]).astype(MXU_DTYPE)
        mv_ref[...] = _dot(mb, wmv_ref[...]).astype(MXU_DTYPE)

    xb = x_ref[...].astype(MXU_DTYPE)
    p = _dot(xb, wm_ref[...])
    small = p[:, p.shape[1] - LANES:]

    o = 0
    cq = p[:, o:o + Q_RANK]; o += Q_RANK
    ckv = p[:, o:o + KV_RANK]; o += KV_RANK
    g_b = p[:, o:o + CONV_CH]; o += CONV_CH
    g_c = p[:, o:o + CONV_CH]; o += CONV_CH
    h_c = p[:, o:o + CONV_CH]; o += CONV_CH
    q_mem = p[:, o:o + MIX_C]

    cq = cq * lax.rsqrt(jnp.mean(cq * cq, axis=-1, keepdims=True) + RMS_EPS) * qg_ref[...]
    ckv = ckv * lax.rsqrt(jnp.mean(ckv * ckv, axis=-1, keepdims=True) + RMS_EPS) * kvg_ref[...]
    cq_ref[...] = cq.astype(MXU_DTYPE)
    ckv_b = ckv.astype(MXU_DTYPE)
    ckv_ref[...] = ckv_b
    ckvt_ref[0] = ckv.T.astype(MXU_DTYPE)

    kidx_ref[...] = small[:, :IDX_DIM].astype(MXU_DTYPE)
    small_t = small.T
    iwt_ref[0] = small_t[IDX_DIM:IDX_DIM + N_IDX_HEADS, :] * (N_IDX_HEADS ** -0.5 * IDX_DIM ** -0.5)

    u = g_c * h_c
    rows = lax.broadcasted_iota(jnp.int32, (tm, 1), 0)
    c6 = carry_ref[SUBLANES - 2:SUBLANES - 1, :]
    c7 = carry_ref[SUBLANES - 1:SUBLANES, :]
    u1 = jnp.where(rows == 0, c7, pltpu.roll(u, 1, 0))
    u2 = jnp.where(rows == 0, c6, jnp.where(rows == 1, c7, pltpu.roll(u, 2, 0)))
    y = cw_ref[0:1, :] * u2
    y = y + cw_ref[1:2, :] * u1
    y = y + cw_ref[2:3, :] * u
    yb_ref[...] = (g_b * y).astype(MXU_DTYPE)
    carry_ref[...] = u[tm - SUBLANES:, :]

    qm = q_mem.astype(MXU_DTYPE)
    outs = []
    for h in range(N_MEM_HEADS):
        sl = slice(h * HEAD_DIM, (h + 1) * HEAD_DIM)
        lg = _dot_nt(qm[:, sl], mk_ref[:, sl]) * (HEAD_DIM ** -0.5)
        lg = lg - jnp.max(lg, axis=-1, keepdims=True)
        e = jnp.exp(lg)
        pr = e / jnp.sum(e, axis=-1, keepdims=True)
        outs.append(_dot(pr.astype(MXU_DTYPE), mv_ref[:, sl]))
    yc_ref[...] = jnp.concatenate(outs, axis=-1).astype(MXU_DTYPE)


def _proj(x2, mem, w_main, q_g, kv_g, conv_w, w_mk, w_mv, B, S, tm):
    T, D = x2.shape
    n_mem = mem.shape[1]
    ns = S // tm
    row = lambda b, s: (b * ns + s, 0)
    const2 = lambda b, s: (0, 0)
    bf = MXU_DTYPE
    return pl.pallas_call(
        functools.partial(_proj_kernel, tm=tm),
        grid=(B, ns),
        in_specs=[
            pl.BlockSpec((tm, D), row),
            pl.BlockSpec((1, n_mem, D), lambda b, s: (b, 0, 0)),
            pl.BlockSpec(w_main.shape, const2),
            pl.BlockSpec(q_g.shape, const2),
            pl.BlockSpec(kv_g.shape, const2),
            pl.BlockSpec(conv_w.shape, const2),
            pl.BlockSpec(w_mk.shape, const2),
            pl.BlockSpec(w_mv.shape, const2),
        ],
        out_specs=[
            pl.BlockSpec((tm, Q_RANK), row),
            pl.BlockSpec((tm, KV_RANK), row),
            pl.BlockSpec((1, KV_RANK, tm), lambda b, s: (b, 0, s)),
            pl.BlockSpec((tm, IDX_DIM), row),
            pl.BlockSpec((1, N_IDX_HEADS, tm), lambda b, s: (b, 0, s)),
            pl.BlockSpec((tm, CONV_CH), row),
            pl.BlockSpec((tm, MIX_C), row),
        ],
        out_shape=[
            jax.ShapeDtypeStruct((T, Q_RANK), bf),
            jax.ShapeDtypeStruct((T, KV_RANK), bf),
            jax.ShapeDtypeStruct((B, KV_RANK, S), bf),
            jax.ShapeDtypeStruct((T, IDX_DIM), bf),
            jax.ShapeDtypeStruct((B, N_IDX_HEADS, S), jnp.float32),
            jax.ShapeDtypeStruct((T, CONV_CH), bf),
            jax.ShapeDtypeStruct((T, MIX_C), bf),
        ],
        scratch_shapes=[
            pltpu.VMEM((SUBLANES, CONV_CH), jnp.float32),
            pltpu.VMEM((n_mem, MIX_C), bf),
            pltpu.VMEM((n_mem, MIX_C), bf),
        ],
        compiler_params=_cparams(("arbitrary", "arbitrary")),
        name="proj",
    )(x2, mem, w_main, q_g, kv_g, conv_w, w_mk, w_mv)


def _key_to_f32(key):
    bits = jnp.where(key < 0, key ^ jnp.int32(0x7FFFFFFF), key)
    return pltpu.bitcast(bits, jnp.float32)


def _colsum8(v):
    return jnp.sum(v.reshape(QB // SUBLANES, SUBLANES, QB), axis=0)


def _colmax8(v):
    return jnp.max(v.reshape(QB // SUBLANES, SUBLANES, QB), axis=0)


UNROLL_WIDTHS = (8, 4, 2, 1)


def _dsa_kernel(cq_ref, iwt_ref, kidx_ref, ckv_ref, ckvt_ref, wqi_ref, wuq_ref, wuk_ref, wuvt_ref,
                bias_ref, o_ref, wfold_ref, qidx_ref, qlat_ref, score_ref, logit_ref, acc_ref,
                *, k_sel, idx_bits):
    i = pl.program_id(1)
    f32 = jnp.float32
    bf = MXU_DTYPE
    n_blocks = i + 1
    n_blocks = n_blocks + jnp.where((n_blocks % 4 == 3) & (n_blocks < pl.num_programs(1)), 1, 0)
    s_loc = lax.broadcasted_iota(jnp.int32, (QB, QB), 0)
    t_glob = i * QB + lax.broadcasted_iota(jnp.int32, (QB, QB), 1)

    def blk(jb):
        return pl.multiple_of(jb * QB, QB)

    def block_loop(fn, init):
        c, start = init, 0
        for width in UNROLL_WIDTHS:
            n = (n_blocks - start) // width
            c = lax.fori_loop(0, n, lambda it, c, w=width, s=start: fn(s + it * w, w, c), c)
            start = start + n * width
        return c

    @pl.when(i == 0)
    def _():
        for h in range(N_HEADS_A):
            wfold_ref[:, h * KV_RANK:(h + 1) * KV_RANK] = (
                _dot_nt(wuq_ref[:, h * HEAD_DIM:(h + 1) * HEAD_DIM], wuk_ref[h])
                * (HEAD_DIM ** -0.5 * LOG2_E)).astype(bf)

    cq = cq_ref[...]
    q_idx = _dot(cq, wqi_ref[...]).astype(bf)
    q_lat = _dot(cq, wfold_ref[...]).astype(bf)
    for h in range(N_HEADS_A):
        qidx_ref[h * QB:(h + 1) * QB, :] = q_idx[:, h * IDX_DIM:(h + 1) * IDX_DIM]
        qlat_ref[h * QB:(h + 1) * QB, :] = q_lat[:, h * KV_RANK:(h + 1) * KV_RANK]
    iw = iwt_ref[0]

    def score_body(jb0, nb, n_pos8):
        d_blk = _dot_nt(kidx_ref[pl.ds(blk(jb0), nb * QB), :], qidx_ref[...])
        for sb in range(nb):
            off = blk(jb0 + sb)
            d_all = d_blk[sb * QB:(sb + 1) * QB, :]
            acc = jnp.maximum(d_all[:, 0:QB], 0.0) * iw[0:1, :]
            for h in range(1, N_IDX_HEADS):
                acc = acc + jnp.maximum(d_all[:, h * QB:(h + 1) * QB], 0.0) * iw[h:h + 1, :]
            sc = jnp.where(s_loc + off <= t_glob, acc + 0.0, F32_LOWEST)
            score_ref[pl.ds(off, QB), :] = sc
            n_pos8 = n_pos8 + _colsum8(jnp.where(sc >= 0.0, 1.0, 0.0))
        return n_pos8

    n_pos8 = block_loop(score_body, jnp.zeros((SUBLANES, QB), f32))

    def count_where(pred):
        def body(jb0, nb, acc):
            for sb in range(nb):
                off = blk(jb0 + sb)
                acc = acc + _colsum8(jnp.where(pred(score_ref[pl.ds(off, QB), :], off), 1.0, 0.0))
            return acc
        acc = block_loop(body, jnp.zeros((SUBLANES, QB), f32))
        return jnp.sum(acc, axis=0, keepdims=True)

    kf = float(k_sel)

    def search():
        c0 = jnp.sum(n_pos8, axis=0, keepdims=True)
        cand0 = jnp.where(c0 >= kf, jnp.int32(0), jnp.int32(-2 ** 31))
        n_ge0 = jnp.where(c0 >= kf, c0, -1.0)

        def bit_body(it, carry):
            cand, n_ge = carry
            trial = cand + lax.shift_left(jnp.int32(1), 30 - it)
            tf = _key_to_f32(trial)
            cnt = count_where(lambda sc, off: sc >= tf)
            take = cnt >= kf
            return jnp.where(take, trial, cand), jnp.where(take, cnt, n_ge)

        cand, n_ge = lax.fori_loop(0, 31, bit_body, (cand0, n_ge0))
        thr = _key_to_f32(cand)
        keep_all_ties = jnp.full((1, QB), 2 ** idx_bits - 1, jnp.int32)

        def resolve_ties():
            n_gt = count_where(lambda sc, off: sc > thr)
            n_eq = count_where(lambda sc, off: sc == thr)
            need = kf - n_gt

            def tie_search():
                def tbody(it, xcut):
                    trial = xcut + lax.shift_left(jnp.int32(1), idx_bits - 1 - it)
                    cnt = count_where(lambda sc, off: (sc == thr) & (s_loc + off < trial))
                    return jnp.where(cnt < need, trial, xcut)
                return lax.fori_loop(0, idx_bits, tbody, jnp.zeros((1, QB), jnp.int32))

            return lax.cond(jnp.max(n_eq - need) > 0.0, tie_search, lambda: keep_all_ties)

        plain = jnp.max(jnp.abs(n_ge - kf)) == 0.0
        xcut = lax.cond(plain, lambda: keep_all_ties, resolve_ties)
        return thr, xcut, plain

    def no_search():
        return (jnp.full((1, QB), F32_LOWEST, f32), jnp.full((1, QB), 2 ** idx_bits - 1, jnp.int32),
                jnp.zeros((), jnp.bool_))

    thr, xcut, plain = lax.cond((i + 1) * QB > k_sel, search, no_search)

    def general_mask(off):
        sc = score_ref[pl.ds(off, QB), :]
        s_glob = s_loc + off
        keep = ((sc > thr) | ((sc == thr) & (s_glob <= xcut))) & (s_glob <= t_glob)
        return jnp.where(keep, 0.0, -jnp.inf)

    def plain_mask(off):
        return jnp.where(score_ref[pl.ds(off, QB), :] >= thr, 0.0, -jnp.inf)

    acc_ref[...] = jnp.zeros_like(acc_ref)

    def att_body(selection_mask, jb0, nb, carry):
        rows = nb * QB
        all_far = jb0 + nb - 1 < i - 1
        return lax.cond(all_far,
                        lambda: att_step(selection_mask, True, jb0, nb, rows, carry),
                        lambda: att_step(selection_mask, False, jb0, nb, rows, carry))

    def att_step(selection_mask, far, jb0, nb, rows, carry):
        m, l8 = list(carry[0]), list(carry[1])
        lg_blk = _dot_nt(ckv_ref[pl.ds(blk(jb0), rows), :], qlat_ref[...])
        blk_max = [None] * N_HEADS_A
        for sb in range(nb):
            off = blk(jb0 + sb)
            msk = selection_mask(off)
            bsel = jnp.clip(jb0 + sb - i + 2, 0, 2)
            for h in range(N_HEADS_A):
                lgh = lg_blk[sb * QB:(sb + 1) * QB, h * QB:(h + 1) * QB] + msk
                if not far:
                    lgh = lgh + bias_ref[bsel, h]
                logit_ref[sb * QB:(sb + 1) * QB, h * QB:(h + 1) * QB] = lgh
                cm = _colmax8(lgh)
                blk_max[h] = cm if blk_max[h] is None else jnp.maximum(blk_max[h], cm)
        ps, scales = [], []
        for h in range(N_HEADS_A):
            far_bias = bias_ref[0, h, 0:1, :] if far else 0.0
            m_new = jnp.maximum(m[h], jnp.max(blk_max[h], axis=0, keepdims=True) + far_bias)
            m_ref = jnp.where(m_new == -jnp.inf, 0.0, m_new)
            p = jnp.exp2(logit_ref[0:rows, h * QB:(h + 1) * QB] - (m_ref - far_bias))
            scale = jnp.exp2(m[h] - m_ref)
            l8[h] = l8[h] * scale + jnp.sum(p.reshape(rows // SUBLANES, SUBLANES, QB), axis=0)
            m[h] = m_new
            ps.append(p.astype(bf))
            scales.append(scale)
        pv = _dot(ckvt_ref[0, :, pl.ds(blk(jb0), rows)], jnp.concatenate(ps, axis=1))
        for h in range(N_HEADS_A):
            hs = slice(h * QB, (h + 1) * QB)
            acc_ref[:, hs] = acc_ref[:, hs] * scales[h] + pv[:, hs]
        return tuple(m), tuple(l8)

    carry0 = (tuple(jnp.full((1, QB), -jnp.inf, f32) for _ in range(N_HEADS_A)),
              tuple(jnp.zeros((SUBLANES, QB), f32) for _ in range(N_HEADS_A)))
    _, l8 = lax.cond(plain,
                     lambda: block_loop(functools.partial(att_body, plain_mask), carry0),
                     lambda: block_loop(functools.partial(att_body, general_mask), carry0))

    outs = []
    for h in range(N_HEADS_A):
        l_row = jnp.sum(l8[h], axis=0, keepdims=True)
        o_lat_t = (acc_ref[:, h * QB:(h + 1) * QB] / l_row).astype(bf)
        outs.append(_dot(wuvt_ref[h], o_lat_t))
    o_ref[...] = jnp.concatenate(outs, axis=0).T.astype(o_ref.dtype)


def _dsa(cq, iwt, kidx, ckv, ckvt, w_qidx, w_uq, w_uk_h, w_uvt_h, bias_tiles, B, S):
    T = cq.shape[0]
    assert S % QB == 0 and QB >= REL_MAX_DIST
    nq = S // QB
    k_sel = min(TOPK_MAX, S // 4)
    idx_bits = max(1, (S - 1).bit_length())
    c2 = lambda b, i: (0, 0)
    c3 = lambda b, i: (0, 0, 0)
    return pl.pallas_call(
        functools.partial(_dsa_kernel, k_sel=k_sel, idx_bits=idx_bits),
        grid=(B, nq),
        in_specs=[
            pl.BlockSpec((QB, Q_RANK), lambda b, i: (b * nq + i, 0)),
            pl.BlockSpec((1, N_IDX_HEADS, QB), lambda b, i: (b, 0, i)),
            pl.BlockSpec((S, IDX_DIM), lambda b, i: (b, 0)),
            pl.BlockSpec((S, KV_RANK), lambda b, i: (b, 0)),
            pl.BlockSpec((1, KV_RANK, S), lambda b, i: (b, 0, 0)),
            pl.BlockSpec(w_qidx.shape, c2),
            pl.BlockSpec(w_uq.shape, c2),
            pl.BlockSpec(w_uk_h.shape, c3),
            pl.BlockSpec(w_uvt_h.shape, c3),
            pl.BlockSpec(bias_tiles.shape, lambda b, i: (0, 0, 0, 0)),
        ],
        out_specs=pl.BlockSpec((QB, MIX_A), lambda b, i: (b * nq + i, 0)),
        out_shape=jax.ShapeDtypeStruct((T, MIX_A), MXU_DTYPE),
        scratch_shapes=[
            pltpu.VMEM((Q_RANK, N_HEADS_A * KV_RANK), MXU_DTYPE),
            pltpu.VMEM((N_IDX_HEADS * QB, IDX_DIM), MXU_DTYPE),
            pltpu.VMEM((N_HEADS_A * QB, KV_RANK), MXU_DTYPE),
            pltpu.VMEM((S, QB), jnp.float32),
            pltpu.VMEM((max(UNROLL_WIDTHS) * QB, N_HEADS_A * QB), jnp.float32),
            pltpu.VMEM((KV_RANK, N_HEADS_A * QB), jnp.float32),
        ],
        compiler_params=_cparams(("arbitrary", "arbitrary")),
        name="dsa",
    )(cq, iwt, kidx, ckv, ckvt, w_qidx, w_uq, w_uk_h, w_uvt_h, bias_tiles)


def _layer_norm(xf, g, b):
    mu = jnp.mean(xf, axis=-1, keepdims=True)
    xc = xf - mu
    var = jnp.mean(xc * xc, axis=-1, keepdims=True)
    return xc * lax.rsqrt(var + LN_EPS) * g + b


def _rank_rows(v, n):
    ri = lax.broadcasted_iota(jnp.int32, v.shape, 0)
    rank = jnp.zeros(v.shape, jnp.float32)
    for r2 in range(n):
        row = v[r2:r2 + 1, :]
        beats = (row > v) | ((row == v) & (ri > r2))
        rank = rank + jnp.where(beats, 1.0, 0.0)
    return rank


def _top_rows(v, k):
    n = v.shape[0]
    ri = lax.broadcasted_iota(jnp.int32, v.shape, 0)
    sel = jnp.zeros(v.shape, jnp.float32)
    for _ in range(k):
        m = jnp.max(v, axis=0, keepdims=True)
        first = jnp.min(jnp.where(v == m, ri, n), axis=0, keepdims=True)
        pick = ri == first
        sel = jnp.where(pick, 1.0, sel)
        v = jnp.where(pick, -jnp.inf, v)
    return sel > 0.5


def _pack_factor():
    return 4 // jnp.dtype(MXU_DTYPE).itemsize


def _pack_rows(x):
    if _pack_factor() == 1:
        return pltpu.bitcast(x, jnp.int32)
    half = x.shape[1] // 2
    b = pltpu.bitcast(x.astype(MXU_DTYPE).astype(jnp.float32), jnp.int32)
    return b[:, half:] | lax.shift_right_logical(b[:, :half], jnp.int32(16))


_HIGH_HALF = -(1 << 16)


def _unpack_rows_f32(p):
    if _pack_factor() == 1:
        return [pltpu.bitcast(p, jnp.float32)]
    lo = pltpu.bitcast(lax.shift_left(p, jnp.int32(16)), jnp.float32)
    hi = pltpu.bitcast(p & jnp.int32(_HIGH_HALF), jnp.float32)
    return [lo, hi]


def _unpack_rows(p):
    return [v.astype(MXU_DTYPE) for v in _unpack_rows_f32(p)]


def _mix_router_kernel(x_ref, ya_ref, yb_ref, yc_ref, wo_ref, g_ref, b_ref, wrt_ref, rb_ref, exp_ref,
                       x1_ref, x1p_ref, sel_ref, w_ref, pos_ref, cnt_ref, base_ref, *, tm):
    step = pl.program_id(0)
    f32 = jnp.float32

    @pl.when(step == 0)
    def _():
        base_ref[...] = jnp.zeros_like(base_ref)

    mix = _dot(ya_ref[...], wo_ref[0:MIX_A, :])
    mix = mix + _dot(yb_ref[...], wo_ref[MIX_A:MIX_A + CONV_CH, :])
    mix = mix + _dot(yc_ref[...], wo_ref[MIX_A + CONV_CH:, :])
    x1 = _layer_norm(ALPHA * x_ref[...] + mix, g_ref[...], b_ref[...])
    x1_ref[...] = x1
    x1p_ref[...] = _pack_rows(x1)

    lg = lax.dot_general(wrt_ref[...], x1, _NT, precision=lax.Precision.HIGHEST, preferred_element_type=f32)
    s = 1.0 / (1.0 + jnp.exp(-lg))
    sc = s + rb_ref[...]

    g3 = sc.reshape(N_GROUPS, GROUP_SIZE, tm)
    m1 = jnp.max(g3, axis=1, keepdims=True)
    is_m1 = g3 == m1
    n_m1 = jnp.sum(jnp.where(is_m1, 1.0, 0.0), axis=1, keepdims=True)
    m2 = jnp.max(jnp.where(is_m1, -jnp.inf, g3), axis=1, keepdims=True)
    gscore = (m1 + jnp.where(n_m1 > 1.0, m1, m2)).reshape(N_GROUPS, tm)
    gsel = jnp.where(_rank_rows(gscore, N_GROUPS) < float(TOPK_GROUPS), 1.0, 0.0)
    emask = _dot(exp_ref[...], gsel.astype(MXU_DTYPE)) > 0.5
    masked = jnp.where(emask, sc, -jnp.inf)
    sel = _top_rows(masked, TOP_K) & emask
    self_ = jnp.where(sel, 1.0, 0.0)
    top_s = jnp.where(sel, s, 0.0)
    w = top_s / jnp.sum(top_s, axis=0, keepdims=True) * ROUTED_SCALE

    t_r = lax.broadcasted_iota(jnp.int32, (tm, tm), 0)
    t_c = lax.broadcasted_iota(jnp.int32, (tm, tm), 1)
    upper = jnp.where(t_r < t_c, 1.0, 0.0).astype(MXU_DTYPE)
    pref = _dot(self_.astype(MXU_DTYPE), upper)
    base = base_ref[...]
    sel_ref[...] = self_
    w_ref[...] = w
    pos_ref[...] = base + pref
    base = base + jnp.sum(self_, axis=1, keepdims=True)
    base_ref[...] = base
    cnt_ref[...] = jnp.broadcast_to(base, cnt_ref.shape)


def _mix_router(x2, ya, yb, yc, w_out, ln_g, ln_b, w_router_t, router_bias, tm):
    T, D = x2.shape
    E = N_EXPERTS
    expand = (jnp.arange(E)[:, None] // GROUP_SIZE == jnp.arange(N_GROUPS)[None, :]).astype(MXU_DTYPE)
    row = lambda i: (i, 0)
    col = lambda i: (0, i)
    c2 = lambda i: (0, 0)
    f32 = jnp.float32
    return pl.pallas_call(
        functools.partial(_mix_router_kernel, tm=tm),
        grid=(T // tm,),
        in_specs=[
            pl.BlockSpec((tm, D), row),
            pl.BlockSpec((tm, MIX_A), row),
            pl.BlockSpec((tm, CONV_CH), row),
            pl.BlockSpec((tm, MIX_C), row),
            pl.BlockSpec(w_out.shape, c2),
            pl.BlockSpec((1, D), c2),
            pl.BlockSpec((1, D), c2),
            pl.BlockSpec((E, D), c2),
            pl.BlockSpec((E, 1), c2),
            pl.BlockSpec((E, N_GROUPS), c2),
        ],
        out_specs=[
            pl.BlockSpec((tm, D), row),
            pl.BlockSpec((tm, D // _pack_factor()), row),
            pl.BlockSpec((E, tm), col),
            pl.BlockSpec((E, tm), col),
            pl.BlockSpec((E, tm), col),
            pl.BlockSpec((E, LANES), c2),
        ],
        out_shape=[
            jax.ShapeDtypeStruct((T, D), f32),
            jax.ShapeDtypeStruct((T, D // _pack_factor()), jnp.int32),
            jax.ShapeDtypeStruct((E, T), f32),
            jax.ShapeDtypeStruct((E, T), f32),
            jax.ShapeDtypeStruct((E, T), f32),
            jax.ShapeDtypeStruct((E, LANES), f32),
        ],
        scratch_shapes=[pltpu.VMEM((E, 1), f32)],
        compiler_params=_cparams(("arbitrary",)),
        name="mix_router",
    )(x2, ya, yb, yc, w_out, ln_g, ln_b, w_router_t, router_bias, expand)


def _compact_kernel(sel_ref, w_ref, pos_ref, pstart_ref, low_ref, dest_ref, wk_ref):
    sel = sel_ref[...]
    on = sel > 0.5
    rank = _dot(low_ref[...], sel.astype(MXU_DTYPE))
    row = pstart_ref[...] + pos_ref[...]
    w = w_ref[...]
    dests, ws = [], []
    for k in range(TOP_K):
        m = on & (rank == float(k))
        dests.append(jnp.sum(jnp.where(m, row, 0.0), axis=0, keepdims=True))
        ws.append(jnp.sum(jnp.where(m, w, 0.0), axis=0, keepdims=True))
    dest_ref[...] = jnp.concatenate(dests, axis=0).astype(jnp.int32)
    wk_ref[...] = jnp.concatenate(ws, axis=0)


def _compact(sel_t, w_t, pos_t, pad_start, tm):
    E, T = sel_t.shape
    lower = (jnp.arange(E)[None, :] < jnp.arange(E)[:, None]).astype(MXU_DTYPE)
    col = lambda i: (0, i)
    c2 = lambda i: (0, 0)
    return pl.pallas_call(
        _compact_kernel,
        grid=(T // tm,),
        in_specs=[pl.BlockSpec((E, tm), col), pl.BlockSpec((E, tm), col), pl.BlockSpec((E, tm), col),
                  pl.BlockSpec((E, 1), c2), pl.BlockSpec((E, E), c2)],
        out_specs=[pl.BlockSpec((TOP_K, tm), col), pl.BlockSpec((TOP_K, tm), col)],
        out_shape=[jax.ShapeDtypeStruct((TOP_K, T), jnp.int32), jax.ShapeDtypeStruct((TOP_K, T), jnp.float32)],
        compiler_params=_cparams(("arbitrary",)),
        name="route_compact",
    )(sel_t, w_t, pos_t, pad_start, lower)


def _silu(g):
    return g / (1.0 + jnp.exp(-g))


def _expert_kernel(be_ref, nv_ref, nu_ref, xs_ref, wg_ref, wu_ref, wd_ref, ys_ref, wgb_ref, wub_ref, wdb_ref):
    i = pl.program_id(0)

    @pl.when((i == 0) | (be_ref[i] != be_ref[jnp.maximum(i - 1, 0)]))
    def _():
        wgb_ref[...] = wg_ref[0].astype(MXU_DTYPE)
        wub_ref[...] = wu_ref[0].astype(MXU_DTYPE)
        wdb_ref[...] = wd_ref[0].astype(MXU_DTYPE)

    @pl.when(i < nu_ref[0])
    def _():
        live = lax.broadcasted_iota(jnp.int32, (ROW_BLOCK, 1), 0) < nv_ref[i]
        parts = [jnp.where(live, v, jnp.zeros_like(v)) for v in _unpack_rows(xs_ref[...])]
        dk = wgb_ref.shape[0] // len(parts)

        def proj(w_ref):
            acc = _dot(parts[0], w_ref[0:dk, :])
            for n in range(1, len(parts)):
                acc = acc + _dot(parts[n], w_ref[n * dk:(n + 1) * dk, :])
            return acc

        a = (_silu(proj(wgb_ref)) * proj(wub_ref)).astype(MXU_DTYPE)
        ys_ref[...] = _pack_rows(_dot(a, wdb_ref[...]))


def _experts(xs, block_e, block_valid, n_used, w_gate, w_up, w_down):
    n_rows, W = xs.shape
    D = w_gate.shape[1]
    n_blocks = n_rows // ROW_BLOCK
    blk = lambda i, be, nv, nu: (jnp.minimum(i, nu[0] - 1), 0)
    wsel = lambda i, be, nv, nu: (be[i], 0, 0)
    return pl.pallas_call(
        _expert_kernel,
        grid_spec=pltpu.PrefetchScalarGridSpec(
            num_scalar_prefetch=3,
            grid=(n_blocks,),
            in_specs=[
                pl.BlockSpec((ROW_BLOCK, W), blk),
                pl.BlockSpec((1, D, D_EXPERT), wsel),
                pl.BlockSpec((1, D, D_EXPERT), wsel),
                pl.BlockSpec((1, D_EXPERT, D), wsel),
            ],
            out_specs=pl.BlockSpec((ROW_BLOCK, W), blk),
            scratch_shapes=[pltpu.VMEM((D, D_EXPERT), MXU_DTYPE), pltpu.VMEM((D, D_EXPERT), MXU_DTYPE),
                            pltpu.VMEM((D_EXPERT, D), MXU_DTYPE)],
        ),
        out_shape=jax.ShapeDtypeStruct((n_rows, W), xs.dtype),
        compiler_params=_cparams(("arbitrary",)),
        name="experts",
    )(block_e, block_valid, n_used, xs, w_gate, w_up, w_down)


SC_CORES = 2
SC_SUBCORES = 16
SC_GATHER_ROWS = 64
COMBINE_CHUNKS = 8


def _sc_gather_rows(table, idx):
    n = idx.shape[0]
    w = table.shape[1]
    n_workers = SC_CORES * SC_SUBCORES
    per_worker = n // n_workers
    assert n % n_workers == 0 and per_worker % SC_GATHER_ROWS == 0
    mesh = plsc.VectorSubcoreMesh(core_axis_name="c", subcore_axis_name="s")

    @functools.partial(
        pl.kernel, mesh=mesh,
        out_type=jax.ShapeDtypeStruct((n, w), table.dtype),
        scratch_types=[
            pltpu.VMEM((2, SC_GATHER_ROWS), jnp.int32),
            pltpu.VMEM((2, SC_GATHER_ROWS, w), table.dtype),
            pltpu.SemaphoreType.DMA((2,)),
        ],
        name="sc_gather_rows",
    )
    def gather(table_hbm, idx_hbm, out_hbm, idx_v, rows_v, sem):
        wid = lax.axis_index("s") * SC_CORES + lax.axis_index("c")
        base = wid * per_worker
        n_steps = per_worker // SC_GATHER_ROWS

        def gather_copy(slot):
            return pltpu.make_async_copy(table_hbm.at[idx_v.at[slot]], rows_v.at[slot], sem.at[slot])

        def start(step, slot):
            pltpu.sync_copy(idx_hbm.at[pl.ds(base + step * SC_GATHER_ROWS, SC_GATHER_ROWS)], idx_v.at[slot])
            gather_copy(slot).start()

        start(0, 0)

        @pl.loop(0, n_steps, step=2)
        def _(g):
            for slot in range(2):
                step = g + slot

                @pl.when(step + 1 < n_steps)
                def _():
                    start(step + 1, 1 - slot)

                gather_copy(slot).wait()
                pltpu.sync_copy(rows_v.at[slot], out_hbm.at[pl.ds(base + step * SC_GATHER_ROWS, SC_GATHER_ROWS)])

    return gather(table, idx)


SC_SCATTER_ROWS = 64


def _sc_scatter_rows(rows, idx3, n_out):
    n_src, w = rows.shape
    n_chunks, n_dst, batch = idx3.shape
    n_workers = SC_CORES * SC_SUBCORES
    assert batch == SC_SCATTER_ROWS and n_chunks * batch == n_src and n_chunks % (2 * n_workers) == 0
    per_worker = n_chunks // n_workers
    mesh = plsc.VectorSubcoreMesh(core_axis_name="c", subcore_axis_name="s")

    @functools.partial(
        pl.kernel, mesh=mesh,
        out_type=jax.ShapeDtypeStruct((n_out, w), rows.dtype),
        scratch_types=[
            pltpu.VMEM((2, n_dst, batch), jnp.int32),
            pltpu.VMEM((2, batch, w), rows.dtype),
            pltpu.SemaphoreType.DMA((2,)),
            pltpu.SemaphoreType.DMA,
        ],
        name="sc_scatter_rows",
    )
    def scatter(rows_hbm, idx_hbm, out_hbm, idx_v, rows_v, load_sem, store_sem):
        wid = lax.axis_index("s") * SC_CORES + lax.axis_index("c")

        def load_copy(step, slot):
            c = wid * per_worker + step
            return pltpu.make_async_copy(rows_hbm.at[pl.ds(c * batch, batch)], rows_v.at[slot], load_sem.at[slot])

        def load(step, slot):
            pltpu.sync_copy(idx_hbm.at[wid * per_worker + step], idx_v.at[slot])
            load_copy(step, slot).start()

        def store_copy(slot, k):
            return pltpu.make_async_copy(rows_v.at[slot], out_hbm.at[idx_v.at[slot].at[k]], store_sem)

        load(0, 0)

        @pl.loop(0, per_worker, step=2)
        def _(g):
            for slot in range(2):
                step = g + slot

                @pl.when(step + 1 < per_worker)
                def _():
                    load(step + 1, 1 - slot)

                load_copy(step, slot).wait()
                for k in range(n_dst):
                    store_copy(slot, k).start()
                for k in range(n_dst):
                    store_copy(slot, k).wait()

    return scatter(rows, idx3)


def _combine2_kernel(wk_ref, x1_ref, g_ref_rows, wsg_ref, wsu_ref, wsd_ref, g_ref, b_ref, o_ref):
    x1 = x1_ref[...]
    xb = x1.astype(MXU_DTYPE)
    a = (_silu(_dot(xb, wsg_ref[...])) * _dot(xb, wsu_ref[...])).astype(MXU_DTYPE)
    shared = _dot(a, wsd_ref[...])
    wk = wk_ref[...].T
    groups = [wk[:, 0:1] * v for v in _unpack_rows_f32(g_ref_rows[0])]
    for k in range(1, TOP_K):
        groups = [g + wk[:, k:k + 1] * v for g, v in zip(groups, _unpack_rows_f32(g_ref_rows[k]))]
    routed = jnp.concatenate(groups, axis=1)
    o_ref[...] = _layer_norm(ALPHA * x1 + (routed + shared), g_ref[...], b_ref[...])


def _combine2_kernel_into(wk_ref, x1_ref, g_ref_rows, wsg_ref, wsu_ref, wsd_ref, g_ref, b_ref, prev_ref, o_ref):
    del prev_ref
    _combine2_kernel(wk_ref, x1_ref, g_ref_rows, wsg_ref, wsu_ref, wsd_ref, g_ref, b_ref, o_ref)


def _combine2(wk_t, x1, gathered, w_sg, w_su, w_sd, ln_g, ln_b, tc, chunk, prev):
    T, D = x1.shape
    _, t_chunk, W = gathered.shape
    base = chunk * (t_chunk // tc)
    row = lambda i: (base + i, 0)
    c2 = lambda i: (0, 0)
    in_specs = [
        pl.BlockSpec((TOP_K, tc), lambda i: (0, base + i)),
        pl.BlockSpec((tc, D), row),
        pl.BlockSpec((TOP_K, tc, W), lambda i: (0, i, 0)),
        pl.BlockSpec(w_sg.shape, c2),
        pl.BlockSpec(w_su.shape, c2),
        pl.BlockSpec(w_sd.shape, c2),
        pl.BlockSpec((1, D), c2),
        pl.BlockSpec((1, D), c2),
    ]
    args = [wk_t, x1, gathered, w_sg, w_su, w_sd, ln_g, ln_b]
    if prev is None:
        body, aliases = _combine2_kernel, {}
    else:
        body, aliases = _combine2_kernel_into, {len(args): 0}
        in_specs.append(pl.BlockSpec(memory_space=pl.ANY))
        args.append(prev)
    return pl.pallas_call(
        body,
        grid=(t_chunk // tc,),
        in_specs=in_specs,
        out_specs=pl.BlockSpec((tc, D), row),
        out_shape=jax.ShapeDtypeStruct((T, D), jnp.float32),
        input_output_aliases=aliases,
        compiler_params=_cparams(("arbitrary",)),
        name="combine",
    )(*args)


def _split_w_in(w_in):
    o_kv = Q_RANK
    o_ki = o_kv + KV_RANK
    o_iw = o_ki + IDX_DIM
    o_rest = o_iw + N_IDX_HEADS
    w_small = jnp.pad(w_in[:, o_ki:o_rest], ((0, 0), (0, LANES - IDX_DIM - N_IDX_HEADS)))
    return jnp.concatenate([w_in[:, :o_ki], w_in[:, o_rest:], w_small], axis=1).astype(MXU_DTYPE)


def _stages(x, mem, w_in, q_norm_g, kv_norm_g, w_uq, w_uk, w_uv, w_qidx, rel_bias, conv_w, w_mem_k, w_mem_v, w_out, ln1_g, ln1_b, w_router, router_bias, w_e_gate, w_e_up, w_e_down, w_s_gate, w_s_up, w_s_down, ln2_g, ln2_b):
    B, S, D = x.shape
    T = B * S
    bf = MXU_DTYPE
    assert w_in.shape[0] == DEPTH == 1, "single-layer stack"
    l = 0
    res = {}
    x2 = x.reshape(T, D)
    cq, ckv, ckvt, kidx, iwt, yb, yc = _proj(
        x2, mem, _split_w_in(w_in[l]), q_norm_g[l].reshape(1, -1), kv_norm_g[l].reshape(1, -1), conv_w[l],
        w_mem_k[l].astype(bf), w_mem_v[l].astype(bf), B, S, tm=min(1024, S))
    res.update(c_q=cq, c_kv=ckv, k_idx=kidx, y_b=yb, y_c=yc,
               idx_w=jnp.swapaxes(iwt, 1, 2) / (N_IDX_HEADS ** -0.5 * IDX_DIM ** -0.5))
    bias_t = _bias_tiles(rel_bias)
    ya = _dsa(cq, iwt, kidx, ckv, ckvt,
              w_qidx[l].reshape(Q_RANK, -1).astype(bf), w_uq[l].reshape(Q_RANK, -1).astype(bf),
              jnp.transpose(w_uk[l], (1, 0, 2)).astype(bf), jnp.transpose(w_uv[l], (1, 2, 0)).astype(bf),
              bias_t, B, S)
    res.update(y_a=ya)

    x1, x1p, sel_t, w_t, pos_t, cnt = _mix_router(
        x2, ya, yb, yc, w_out[l].astype(bf), ln1_g[l].reshape(1, -1), ln1_b[l].reshape(1, -1),
        w_router[l].T, router_bias[l].reshape(-1, 1), tm=min(1024, T))
    res.update(x1=x1)

    counts = cnt[:, 0].astype(jnp.int32)
    padded = (counts + ROW_BLOCK - 1) // ROW_BLOCK * ROW_BLOCK
    pad_end = jnp.cumsum(padded)
    pad_start = pad_end - padded
    n_blocks = -(-(T * TOP_K) // ROW_BLOCK) + N_EXPERTS
    n_rows = n_blocks * ROW_BLOCK
    block_start = jnp.arange(n_blocks, dtype=jnp.int32) * ROW_BLOCK
    block_e = jnp.minimum(jnp.sum((pad_end[None, :] <= block_start[:, None]).astype(jnp.int32), axis=1),
                          N_EXPERTS - 1)
    n_used = (pad_end[-1:] // ROW_BLOCK).astype(jnp.int32)

    dest_t, wk_t = _compact(sel_t, w_t, pos_t, pad_start.astype(jnp.float32).reshape(-1, 1), tm=min(8192, T))
    block_valid = jnp.clip((pad_start + counts)[block_e] - block_start, 0, ROW_BLOCK).astype(jnp.int32)
    bt = SC_SCATTER_ROWS
    idx3 = jnp.transpose(dest_t.reshape(TOP_K, T // bt, bt), (1, 0, 2))
    xs = _sc_scatter_rows(x1p, idx3, n_rows)
    ys = _experts(xs, block_e, block_valid, n_used, w_e_gate[l], w_e_up[l], w_e_down[l])
    n_chunks = COMBINE_CHUNKS if T % (COMBINE_CHUNKS * 512) == 0 else 1
    t_chunk = T // n_chunks
    out = None
    for c in range(n_chunks):
        idx_c = dest_t[:, c * t_chunk:(c + 1) * t_chunk].reshape(-1)
        gathered = _sc_gather_rows(ys, idx_c).reshape(TOP_K, t_chunk, -1)
        out = _combine2(wk_t, x1, gathered, w_s_gate[l].astype(bf), w_s_up[l].astype(bf), w_s_down[l].astype(bf),
                        ln2_g[l].reshape(1, -1), ln2_b[l].reshape(1, -1), tc=min(512, t_chunk), chunk=c, prev=out)
    res.update(out=out.reshape(B, S, D))
    return res


def kernel(x, mem, w_in, q_norm_g, kv_norm_g, w_uq, w_uk, w_uv, w_qidx, rel_bias, conv_w, w_mem_k, w_mem_v, w_out, ln1_g, ln1_b, w_router, router_bias, w_e_gate, w_e_up, w_e_down, w_s_gate, w_s_up, w_s_down, ln2_g, ln2_b):
    return _stages(x, mem, w_in, q_norm_g, kv_norm_g, w_uq, w_uk, w_uv, w_qidx, rel_bias, conv_w, w_mem_k, w_mem_v, w_out, ln1_g, ln1_b, w_router, router_bias, w_e_gate, w_e_up, w_e_down, w_s_gate, w_s_up, w_s_down, ln2_g, ln2_b)["out"]
```

```python
import functools
import math

import jax
import jax.numpy as jnp
from jax import lax
from jax.experimental import pallas as pl
from jax.experimental.pallas import tpu as pltpu
from jax.experimental.pallas import tpu_sc as plsc

N_HEADS_A = 8
HEAD_DIM = 64
Q_RANK = 256
KV_RANK = 128
N_IDX_HEADS = 8
IDX_DIM = 64
TOPK_MAX = 256
REL_BUCKETS = 32
REL_MAX_DIST = 128
CONV_CH = 256
CONV_WIDTH = 3
N_MEM_HEADS = 4
MIX_A = N_HEADS_A * HEAD_DIM
MIX_C = N_MEM_HEADS * HEAD_DIM
N_EXPERTS = 64
N_GROUPS = 8
GROUP_SIZE = N_EXPERTS // N_GROUPS
TOPK_GROUPS = 4
TOP_K = 8
D_EXPERT = 256
ROUTED_SCALE = 2.5
DEPTH = 1
ALPHA = (2.0 * DEPTH) ** 0.25
LN_EPS = 1e-5
RMS_EPS = 1e-6
LOG2_E = math.log2(math.e)

LANES = 128
SUBLANES = 8
QB = 128
F32_LOWEST = -3.4028234663852886e38
VMEM_LIMIT = 56 * 1024 * 1024
MXU_DTYPE = jnp.bfloat16
ROW_BLOCK = 1024

_NT = (((1,), (1,)), ((), ()))


def _dot(a, b):
    return jnp.dot(a, b, preferred_element_type=jnp.float32)


def _dot_nt(a, b):
    return lax.dot_general(a, b, _NT, preferred_element_type=jnp.float32)


def _cparams(sem):
    return pltpu.CompilerParams(dimension_semantics=sem, vmem_limit_bytes=VMEM_LIMIT)


def _bias_kernel(rb_ref, o_ref):
    s = lax.broadcasted_iota(jnp.int32, (QB, QB), 0)
    t = lax.broadcasted_iota(jnp.int32, (QB, QB), 1)
    max_exact = REL_BUCKETS // 2
    for tile in range(3):
        n = jnp.maximum(t - s + (2 - tile) * QB, 0)
        nf = jnp.maximum(n.astype(jnp.float32), 1.0)
        large = max_exact + (jnp.log(nf / max_exact) / math.log(REL_MAX_DIST / max_exact)
                             * (REL_BUCKETS - max_exact)).astype(jnp.int32)
        large = jnp.minimum(large, REL_BUCKETS - 1)
        bucket = jnp.where(n < max_exact, n, large)
        for h in range(N_HEADS_A):
            acc = jnp.zeros((QB, QB), jnp.float32)
            for b in range(REL_BUCKETS):
                acc = jnp.where(bucket == b, rb_ref[b, h], acc)
            o_ref[tile, h] = acc * LOG2_E


def _bias_tiles(rel_bias):
    return pl.pallas_call(
        _bias_kernel,
        in_specs=[pl.BlockSpec(memory_space=pltpu.SMEM)],
        out_specs=pl.BlockSpec(memory_space=pltpu.VMEM),
        out_shape=jax.ShapeDtypeStruct((3, N_HEADS_A, QB, QB), jnp.float32),
        name="bias_tiles",
    )(rel_bias)


def _proj_kernel(x_ref, mem_ref, wm_ref, qg_ref, kvg_ref, cw_ref, wmk_ref, wmv_ref,
                 cq_ref, ckv_ref, ckvt_ref, kidx_ref, iwt_ref, yb_ref, yc_ref,
                 carry_ref, mk_ref, mv_ref, *, tm):
    si = pl.program_id(1)

    @pl.when(si == 0)
    def _():
        carry_ref[...] = jnp.zeros_like(carry_ref)
        mb = mem_ref[0].astype(MXU_DTYPE)
        mk_ref[...] = _dot(mb, wmk_ref[...]).astype(MXU_DTYPE)
        mv_ref[...] = _dot(mb, wmv_ref[...]).astype(MXU_DTYPE)

    xb = x_ref[...].astype(MXU_DTYPE)
    p = _dot(xb, wm_ref[...])
    small = p[:, p.shape[1] - LANES:]

    o = 0
    cq = p[:, o:o + Q_RANK]; o += Q_RANK
    ckv = p[:, o:o + KV_RANK]; o += KV_RANK
    g_b = p[:, o:o + CONV_CH]; o += CONV_CH
    g_c = p[:, o:o + CONV_CH]; o += CONV_CH
    h_c = p[:, o:o + CONV_CH]; o += CONV_CH
    q_mem = p[:, o:o + MIX_C]

    cq = cq * lax.rsqrt(jnp.mean(cq * cq, axis=-1, keepdims=True) + RMS_EPS) * qg_ref[...]
    ckv = ckv * lax.rsqrt(jnp.mean(ckv * ckv, axis=-1, keepdims=True) + RMS_EPS) * kvg_ref[...]
    cq_ref[...] = cq.astype(MXU_DTYPE)
    ckv_b = ckv.astype(MXU_DTYPE)
    ckv_ref[...] = ckv_b
    ckvt_ref[0] = ckv.T.astype(MXU_DTYPE)

    kidx_ref[...] = small[:, :IDX_DIM].astype(MXU_DTYPE)
    small_t = small.T
    iwt_ref[0] = small_t[IDX_DIM:IDX_DIM + N_IDX_HEADS, :] * (N_IDX_HEADS ** -0.5 * IDX_DIM ** -0.5)

    u = g_c * h_c
    rows = lax.broadcasted_iota(jnp.int32, (tm, 1), 0)
    c6 = carry_ref[SUBLANES - 2:SUBLANES - 1, :]
    c7 = carry_ref[SUBLANES - 1:SUBLANES, :]
    u1 = jnp.where(rows == 0, c7, pltpu.roll(u, 1, 0))
    u2 = jnp.where(rows == 0, c6, jnp.where(rows == 1, c7, pltpu.roll(u, 2, 0)))
    y = cw_ref[0:1, :] * u2
    y = y + cw_ref[1:2, :] * u1
    y = y + cw_ref[2:3, :] * u
    yb_ref[...] = (g_b * y).astype(MXU_DTYPE)
    carry_ref[...] = u[tm - SUBLANES:, :]

    qm = q_mem.astype(MXU_DTYPE)
    outs = []
    for h in range(N_MEM_HEADS):
        sl = slice(h * HEAD_DIM, (h + 1) * HEAD_DIM)
        lg = _dot_nt(qm[:, sl], mk_ref[:, sl]) * (HEAD_DIM ** -0.5)
        lg = lg - jnp.max(lg, axis=-1, keepdims=True)
        e = jnp.exp(lg)
        pr = e / jnp.sum(e, axis=-1, keepdims=True)
        outs.append(_dot(pr.astype(MXU_DTYPE), mv_ref[:, sl]))
    yc_ref[...] = jnp.concatenate(outs, axis=-1).astype(MXU_DTYPE)


def _proj(x2, mem, w_main, q_g, kv_g, conv_w, w_mk, w_mv, B, S, tm):
    T, D = x2.shape
    n_mem = mem.shape[1]
    ns = S // tm
    row = lambda b, s: (b * ns + s, 0)
    const2 = lambda b, s: (0, 0)
    bf = MXU_DTYPE
    return pl.pallas_call(
        functools.partial(_proj_kernel, tm=tm),
        grid=(B, ns),
        in_specs=[
            pl.BlockSpec((tm, D), row),
            pl.BlockSpec((1, n_mem, D), lambda b, s: (b, 0, 0)),
            pl.BlockSpec(w_main.shape, const2),
            pl.BlockSpec(q_g.shape, const2),
            pl.BlockSpec(kv_g.shape, const2),
            pl.BlockSpec(conv_w.shape, const2),
            pl.BlockSpec(w_mk.shape, const2),
            pl.BlockSpec(w_mv.shape, const2),
        ],
        out_specs=[
            pl.BlockSpec((tm, Q_RANK), row),
            pl.BlockSpec((tm, KV_RANK), row),
            pl.BlockSpec((1, KV_RANK, tm), lambda b, s: (b, 0, s)),
            pl.BlockSpec((tm, IDX_DIM), row),
            pl.BlockSpec((1, N_IDX_HEADS, tm), lambda b, s: (b, 0, s)),
            pl.BlockSpec((tm, CONV_CH), row),
            pl.BlockSpec((tm, MIX_C), row),
        ],
        out_shape=[
            jax.ShapeDtypeStruct((T, Q_RANK), bf),
            jax.ShapeDtypeStruct((T, KV_RANK), bf),
            jax.ShapeDtypeStruct((B, KV_RANK, S), bf),
            jax.ShapeDtypeStruct((T, IDX_DIM), bf),
            jax.ShapeDtypeStruct((B, N_IDX_HEADS, S), jnp.float32),
            jax.ShapeDtypeStruct((T, CONV_CH), bf),
            jax.ShapeDtypeStruct((T, MIX_C), bf),
        ],
        scratch_shapes=[
            pltpu.VMEM((SUBLANES, CONV_CH), jnp.float32),
            pltpu.VMEM((n_mem, MIX_C), bf),
            pltpu.VMEM((n_mem, MIX_C), bf),
        ],
        compiler_params=_cparams(("arbitrary", "arbitrary")),
        name="proj",
    )(x2, mem, w_main, q_g, kv_g, conv_w, w_mk, w_mv)


def _key_to_f32(key):
    bits = jnp.where(key < 0, key ^ jnp.int32(0x7FFFFFFF), key)
    return pltpu.bitcast(bits, jnp.float32)


def _colsum8(v):
    return jnp.sum(v.reshape(QB // SUBLANES, SUBLANES, QB), axis=0)


def _colmax8(v):
    return jnp.max(v.reshape(QB // SUBLANES, SUBLANES, QB), axis=0)


UNROLL_WIDTHS = (8, 4, 2, 1)


def _dsa_kernel(cq_ref, iwt_ref, kidx_ref, ckv_ref, ckvt_ref, wqi_ref, wuq_ref, wuk_ref, wuvt_ref,
                bias_ref, o_ref, wfold_ref, qidx_ref, qlat_ref, score_ref, logit_ref, acc_ref,
                *, k_sel, idx_bits):
    i = pl.program_id(1)
    f32 = jnp.float32
    bf = MXU_DTYPE
    n_blocks = i + 1
    n_blocks = n_blocks + jnp.where((n_blocks % 4 == 3) & (n_blocks < pl.num_programs(1)), 1, 0)
    s_loc = lax.broadcasted_iota(jnp.int32, (QB, QB), 0)
    t_glob = i * QB + lax.broadcasted_iota(jnp.int32, (QB, QB), 1)

    def blk(jb):
        return pl.multiple_of(jb * QB, QB)

    def block_loop(fn, init):
        c, start = init, 0
        for width in UNROLL_WIDTHS:
            n = (n_blocks - start) // width
            c = lax.fori_loop(0, n, lambda it, c, w=width, s=start: fn(s + it * w, w, c), c)
            start = start + n * width
        return c

    @pl.when(i == 0)
    def _():
        for h in range(N_HEADS_A):
            wfold_ref[:, h * KV_RANK:(h + 1) * KV_RANK] = (
                _dot_nt(wuq_ref[:, h * HEAD_DIM:(h + 1) * HEAD_DIM], wuk_ref[h])
                * (HEAD_DIM ** -0.5 * LOG2_E)).astype(bf)

    cq = cq_ref[...]
    q_idx = _dot(cq, wqi_ref[...]).astype(bf)
    q_lat = _dot(cq, wfold_ref[...]).astype(bf)
    for h in range(N_HEADS_A):
        qidx_ref[h * QB:(h + 1) * QB, :] = q_idx[:, h * IDX_DIM:(h + 1) * IDX_DIM]
        qlat_ref[h * QB:(h + 1) * QB, :] = q_lat[:, h * KV_RANK:(h + 1) * KV_RANK]
    iw = iwt_ref[0]

    def score_body(jb0, nb, n_pos8):
        d_blk = _dot_nt(kidx_ref[pl.ds(blk(jb0), nb * QB), :], qidx_ref[...])
        for sb in range(nb):
            off = blk(jb0 + sb)
            d_all = d_blk[sb * QB:(sb + 1) * QB, :]
            acc = jnp.maximum(d_all[:, 0:QB], 0.0) * iw[0:1, :]
            for h in range(1, N_IDX_HEADS):
                acc = acc + jnp.maximum(d_all[:, h * QB:(h + 1) * QB], 0.0) * iw[h:h + 1, :]
            sc = jnp.where(s_loc + off <= t_glob, acc + 0.0, F32_LOWEST)
            score_ref[pl.ds(off, QB), :] = sc
            n_pos8 = n_pos8 + _colsum8(jnp.where(sc >= 0.0, 1.0, 0.0))
        return n_pos8

    n_pos8 = block_loop(score_body, jnp.zeros((SUBLANES, QB), f32))

    def count_where(pred):
        def body(jb0, nb, acc):
            for sb in range(nb):
                off = blk(jb0 + sb)
                acc = acc + _colsum8(jnp.where(pred(score_ref[pl.ds(off, QB), :], off), 1.0, 0.0))
            return acc
        acc = block_loop(body, jnp.zeros((SUBLANES, QB), f32))
        return jnp.sum(acc, axis=0, keepdims=True)

    kf = float(k_sel)

    def search():
        c0 = jnp.sum(n_pos8, axis=0, keepdims=True)
        cand0 = jnp.where(c0 >= kf, jnp.int32(0), jnp.int32(-2 ** 31))
        n_ge0 = jnp.where(c0 >= kf, c0, -1.0)

        def bit_body(it, carry):
            cand, n_ge = carry
            trial = cand + lax.shift_left(jnp.int32(1), 30 - it)
            tf = _key_to_f32(trial)
            cnt = count_where(lambda sc, off: sc >= tf)
            take = cnt >= kf
            return jnp.where(take, trial, cand), jnp.where(take, cnt, n_ge)

        cand, n_ge = lax.fori_loop(0, 31, bit_body, (cand0, n_ge0))
        thr = _key_to_f32(cand)
        keep_all_ties = jnp.full((1, QB), 2 ** idx_bits - 1, jnp.int32)

        def resolve_ties():
            n_gt = count_where(lambda sc, off: sc > thr)
            n_eq = count_where(lambda sc, off: sc == thr)
            need = kf - n_gt

            def tie_search():
                def tbody(it, xcut):
                    trial = xcut + lax.shift_left(jnp.int32(1), idx_bits - 1 - it)
                    cnt = count_where(lambda sc, off: (sc == thr) & (s_loc + off < trial))
                    return jnp.where(cnt < need, trial, xcut)
                return lax.fori_loop(0, idx_bits, tbody, jnp.zeros((1, QB), jnp.int32))

            return lax.cond(jnp.max(n_eq - need) > 0.0, tie_search, lambda: keep_all_ties)

        plain = jnp.max(jnp.abs(n_ge - kf)) == 0.0
        xcut = lax.cond(plain, lambda: keep_all_ties, resolve_ties)
        return thr, xcut, plain

    def no_search():
        return (jnp.full((1, QB), F32_LOWEST, f32), jnp.full((1, QB), 2 ** idx_bits - 1, jnp.int32),
                jnp.zeros((), jnp.bool_))

    thr, xcut, plain = lax.cond((i + 1) * QB > k_sel, search, no_search)

    def general_mask(off):
        sc = score_ref[pl.ds(off, QB), :]
        s_glob = s_loc + off
        keep = ((sc > thr) | ((sc == thr) & (s_glob <= xcut))) & (s_glob <= t_glob)
        return jnp.where(keep, 0.0, -jnp.inf)

    def plain_mask(off):
        return jnp.where(score_ref[pl.ds(off, QB), :] >= thr, 0.0, -jnp.inf)

    acc_ref[...] = jnp.zeros_like(acc_ref)

    def att_body(selection_mask, jb0, nb, carry):
        rows = nb * QB
        all_far = jb0 + nb - 1 < i - 1
        return lax.cond(all_far,
                        lambda: att_step(selection_mask, True, jb0, nb, rows, carry),
                        lambda: att_step(selection_mask, False, jb0, nb, rows, carry))

    def att_step(selection_mask, far, jb0, nb, rows, carry):
        m, l8 = list(carry[0]), list(carry[1])
        lg_blk = _dot_nt(ckv_ref[pl.ds(blk(jb0), rows), :], qlat_ref[...])
        blk_max = [None] * N_HEADS_A
        for sb in range(nb):
            off = blk(jb0 + sb)
            msk = selection_mask(off)
            bsel = jnp.clip(jb0 + sb - i + 2, 0, 2)
            for h in range(N_HEADS_A):
                lgh = lg_blk[sb * QB:(sb + 1) * QB, h * QB:(h + 1) * QB] + msk
                if not far:
                    lgh = lgh + bias_ref[bsel, h]
                logit_ref[sb * QB:(sb + 1) * QB, h * QB:(h + 1) * QB] = lgh
                cm = _colmax8(lgh)
                blk_max[h] = cm if blk_max[h] is None else jnp.maximum(blk_max[h], cm)
        ps, scales = [], []
        for h in range(N_HEADS_A):
            far_bias = bias_ref[0, h, 0:1, :] if far else 0.0
            m_new = jnp.maximum(m[h], jnp.max(blk_max[h], axis=0, keepdims=True) + far_bias)
            m_ref = jnp.where(m_new == -jnp.inf, 0.0, m_new)
            p = jnp.exp2(logit_ref[0:rows, h * QB:(h + 1) * QB] - (m_ref - far_bias))
            scale = jnp.exp2(m[h] - m_ref)
            l8[h] = l8[h] * scale + jnp.sum(p.reshape(rows // SUBLANES, SUBLANES, QB), axis=0)
            m[h] = m_new
            ps.append(p.astype(bf))
            scales.append(scale)
        pv = _dot(ckvt_ref[0, :, pl.ds(blk(jb0), rows)], jnp.concatenate(ps, axis=1))
        for h in range(N_HEADS_A):
            hs = slice(h * QB, (h + 1) * QB)
            acc_ref[:, hs] = acc_ref[:, hs] * scales[h] + pv[:, hs]
        return tuple(m), tuple(l8)

    carry0 = (tuple(jnp.full((1, QB), -jnp.inf, f32) for _ in range(N_HEADS_A)),
              tuple(jnp.zeros((SUBLANES, QB), f32) for _ in range(N_HEADS_A)))
    _, l8 = lax.cond(plain,
                     lambda: block_loop(functools.partial(att_body, plain_mask), carry0),
                     lambda: block_loop(functools.partial(att_body, general_mask), carry0))

    outs = []
    for h in range(N_HEADS_A):
        l_row = jnp.sum(l8[h], axis=0, keepdims=True)
        o_lat_t = (acc_ref[:, h * QB:(h + 1) * QB] / l_row).astype(bf)
        outs.append(_dot(wuvt_ref[h], o_lat_t))
    o_ref[...] = jnp.concatenate(outs, axis=0).T.astype(o_ref.dtype)


def _dsa(cq, iwt, kidx, ckv, ckvt, w_qidx, w_uq, w_uk_h, w_uvt_h, bias_tiles, B, S):
    T = cq.shape[0]
    assert S % QB == 0 and QB >= REL_MAX_DIST
    nq = S // QB
    k_sel = min(TOPK_MAX, S // 4)
    idx_bits = max(1, (S - 1).bit_length())
    c2 = lambda b, i: (0, 0)
    c3 = lambda b, i: (0, 0, 0)
    return pl.pallas_call(
        functools.partial(_dsa_kernel, k_sel=k_sel, idx_bits=idx_bits),
        grid=(B, nq),
        in_specs=[
            pl.BlockSpec((QB, Q_RANK), lambda b, i: (b * nq + i, 0)),
            pl.BlockSpec((1, N_IDX_HEADS, QB), lambda b, i: (b, 0, i)),
            pl.BlockSpec((S, IDX_DIM), lambda b, i: (b, 0)),
            pl.BlockSpec((S, KV_RANK), lambda b, i: (b, 0)),
            pl.BlockSpec((1, KV_RANK, S), lambda b, i: (b, 0, 0)),
            pl.BlockSpec(w_qidx.shape, c2),
            pl.BlockSpec(w_uq.shape, c2),
            pl.BlockSpec(w_uk_h.shape, c3),
            pl.BlockSpec(w_uvt_h.shape, c3),
            pl.BlockSpec(bias_tiles.shape, lambda b, i: (0, 0, 0, 0)),
        ],
        out_specs=pl.BlockSpec((QB, MIX_A), lambda b, i: (b * nq + i, 0)),
        out_shape=jax.ShapeDtypeStruct((T, MIX_A), MXU_DTYPE),
        scratch_shapes=[
            pltpu.VMEM((Q_RANK, N_HEADS_A * KV_RANK), MXU_DTYPE),
            pltpu.VMEM((N_IDX_HEADS * QB, IDX_DIM), MXU_DTYPE),
            pltpu.VMEM((N_HEADS_A * QB, KV_RANK), MXU_DTYPE),
            pltpu.VMEM((S, QB), jnp.float32),
            pltpu.VMEM((max(UNROLL_WIDTHS) * QB, N_HEADS_A * QB), jnp.float32),
            pltpu.VMEM((KV_RANK, N_HEADS_A * QB), jnp.float32),
        ],
        compiler_params=_cparams(("arbitrary", "arbitrary")),
        name="dsa",
    )(cq, iwt, kidx, ckv, ckvt, w_qidx, w_uq, w_uk_h, w_uvt_h, bias_tiles)


def _layer_norm(xf, g, b):
    mu = jnp.mean(xf, axis=-1, keepdims=True)
    xc = xf - mu
    var = jnp.mean(xc * xc, axis=-1, keepdims=True)
    return xc * lax.rsqrt(var + LN_EPS) * g + b


def _rank_rows(v, n):
    ri = lax.broadcasted_iota(jnp.int32, v.shape, 0)
    rank = jnp.zeros(v.shape, jnp.float32)
    for r2 in range(n):
        row = v[r2:r2 + 1, :]
        beats = (row > v) | ((row == v) & (ri > r2))
        rank = rank + jnp.where(beats, 1.0, 0.0)
    return rank


def _top_rows(v, k):
    n = v.shape[0]
    ri = lax.broadcasted_iota(jnp.int32, v.shape, 0)
    sel = jnp.zeros(v.shape, jnp.float32)
    for _ in range(k):
        m = jnp.max(v, axis=0, keepdims=True)
        first = jnp.min(jnp.where(v == m, ri, n), axis=0, keepdims=True)
        pick = ri == first
        sel = jnp.where(pick, 1.0, sel)
        v = jnp.where(pick, -jnp.inf, v)
    return sel > 0.5


def _pack_factor():
    return 4 // jnp.dtype(MXU_DTYPE).itemsize


def _pack_rows(x):
    if _pack_factor() == 1:
        return pltpu.bitcast(x, jnp.int32)
    half = x.shape[1] // 2
    b = pltpu.bitcast(x.astype(MXU_DTYPE).astype(jnp.float32), jnp.int32)
    return b[:, half:] | lax.shift_right_logical(b[:, :half], jnp.int32(16))


_HIGH_HALF = -(1 << 16)


def _unpack_rows_f32(p):
    if _pack_factor() == 1:
        return [pltpu.bitcast(p, jnp.float32)]
    lo = pltpu.bitcast(lax.shift_left(p, jnp.int32(16)), jnp.float32)
    hi = pltpu.bitcast(p & jnp.int32(_HIGH_HALF), jnp.float32)
    return [lo, hi]


def _unpack_rows(p):
    return [v.astype(MXU_DTYPE) for v in _unpack_rows_f32(p)]


def _mix_router_kernel(x_ref, ya_ref, yb_ref, yc_ref, wo_ref, g_ref, b_ref, wrt_ref, rb_ref, exp_ref,
                       x1_ref, x1p_ref, sel_ref, w_ref, pos_ref, cnt_ref, base_ref, upper_ref, *, tm):
    step = pl.program_id(0)
    f32 = jnp.float32

    @pl.when(step == 0)
    def _():
        base_ref[...] = jnp.zeros_like(base_ref)
        t_r = lax.broadcasted_iota(jnp.int32, (tm, tm), 0)
        t_c = lax.broadcasted_iota(jnp.int32, (tm, tm), 1)
        upper_ref[...] = jnp.where(t_r < t_c, 1.0, 0.0).astype(MXU_DTYPE)

    mix = _dot(ya_ref[...], wo_ref[0:MIX_A, :])
    mix = mix + _dot(yb_ref[...], wo_ref[MIX_A:MIX_A + CONV_CH, :])
    mix = mix + _dot(yc_ref[...], wo_ref[MIX_A + CONV_CH:, :])
    x1 = _layer_norm(ALPHA * x_ref[...] + mix, g_ref[...], b_ref[...])
    x1_ref[...] = x1
    x1p_ref[...] = _pack_rows(x1)

    lg = lax.dot_general(wrt_ref[...], x1, _NT, precision=lax.Precision.HIGHEST, preferred_element_type=f32)
    s = 1.0 / (1.0 + jnp.exp(-lg))
    sc = s + rb_ref[...]

    g3 = sc.reshape(N_GROUPS, GROUP_SIZE, tm)
    m1 = jnp.max(g3, axis=1, keepdims=True)
    is_m1 = g3 == m1
    n_m1 = jnp.sum(jnp.where(is_m1, 1.0, 0.0), axis=1, keepdims=True)
    m2 = jnp.max(jnp.where(is_m1, -jnp.inf, g3), axis=1, keepdims=True)
    gscore = (m1 + jnp.where(n_m1 > 1.0, m1, m2)).reshape(N_GROUPS, tm)
    gsel = jnp.where(_rank_rows(gscore, N_GROUPS) < float(TOPK_GROUPS), 1.0, 0.0)
    emask = _dot(exp_ref[...], gsel.astype(MXU_DTYPE)) > 0.5
    masked = jnp.where(emask, sc, -jnp.inf)
    sel = _top_rows(masked, TOP_K) & emask
    self_ = jnp.where(sel, 1.0, 0.0)
    top_s = jnp.where(sel, s, 0.0)
    w = top_s / jnp.sum(top_s, axis=0, keepdims=True) * ROUTED_SCALE

    pref = _dot(self_.astype(MXU_DTYPE), upper_ref[...])
    base = base_ref[...]
    sel_ref[...] = self_
    w_ref[...] = w
    pos_ref[...] = base + pref
    base = base + jnp.sum(self_, axis=1, keepdims=True)
    base_ref[...] = base
    cnt_ref[...] = jnp.broadcast_to(base, cnt_ref.shape)


def _mix_router(x2, ya, yb, yc, w_out, ln_g, ln_b, w_router_t, router_bias, tm):
    T, D = x2.shape
    E = N_EXPERTS
    expand = (jnp.arange(E)[:, None] // GROUP_SIZE == jnp.arange(N_GROUPS)[None, :]).astype(MXU_DTYPE)
    row = lambda i: (i, 0)
    col = lambda i: (0, i)
    c2 = lambda i: (0, 0)
    f32 = jnp.float32
    return pl.pallas_call(
        functools.partial(_mix_router_kernel, tm=tm),
        grid=(T // tm,),
        in_specs=[
            pl.BlockSpec((tm, D), row),
            pl.BlockSpec((tm, MIX_A), row),
            pl.BlockSpec((tm, CONV_CH), row),
            pl.BlockSpec((tm, MIX_C), row),
            pl.BlockSpec(w_out.shape, c2),
            pl.BlockSpec((1, D), c2),
            pl.BlockSpec((1, D), c2),
            pl.BlockSpec((E, D), c2),
            pl.BlockSpec((E, 1), c2),
            pl.BlockSpec((E, N_GROUPS), c2),
        ],
        out_specs=[
            pl.BlockSpec((tm, D), row),
            pl.BlockSpec((tm, D // _pack_factor()), row),
            pl.BlockSpec((E, tm), col),
            pl.BlockSpec((E, tm), col),
            pl.BlockSpec((E, tm), col),
            pl.BlockSpec((E, LANES), c2),
        ],
        out_shape=[
            jax.ShapeDtypeStruct((T, D), f32),
            jax.ShapeDtypeStruct((T, D // _pack_factor()), jnp.int32),
            jax.ShapeDtypeStruct((E, T), f32),
            jax.ShapeDtypeStruct((E, T), f32),
            jax.ShapeDtypeStruct((E, T), f32),
            jax.ShapeDtypeStruct((E, LANES), f32),
        ],
        scratch_shapes=[pltpu.VMEM((E, 1), f32), pltpu.VMEM((tm, tm), MXU_DTYPE)],
        compiler_params=_cparams(("arbitrary",)),
        name="mix_router",
    )(x2, ya, yb, yc, w_out, ln_g, ln_b, w_router_t, router_bias, expand)


def _compact_kernel(sel_ref, w_ref, pos_ref, pstart_ref, low_ref, dest_ref, wk_ref):
    sel = sel_ref[...]
    on = sel > 0.5
    rank = _dot(low_ref[...], sel.astype(MXU_DTYPE))
    row = pstart_ref[...] + pos_ref[...]
    w = w_ref[...]
    dests, ws = [], []
    for k in range(TOP_K):
        m = on & (rank == float(k))
        dests.append(jnp.sum(jnp.where(m, row, 0.0), axis=0, keepdims=True))
        ws.append(jnp.sum(jnp.where(m, w, 0.0), axis=0, keepdims=True))
    dest_ref[...] = jnp.concatenate(dests, axis=0).astype(jnp.int32)
    wk_ref[...] = jnp.concatenate(ws, axis=0)


def _compact(sel_t, w_t, pos_t, pad_start, tm):
    E, T = sel_t.shape
    lower = (jnp.arange(E)[None, :] < jnp.arange(E)[:, None]).astype(MXU_DTYPE)
    col = lambda i: (0, i)
    c2 = lambda i: (0, 0)
    return pl.pallas_call(
        _compact_kernel,
        grid=(T // tm,),
        in_specs=[pl.BlockSpec((E, tm), col), pl.BlockSpec((E, tm), col), pl.BlockSpec((E, tm), col),
                  pl.BlockSpec((E, 1), c2), pl.BlockSpec((E, E), c2)],
        out_specs=[pl.BlockSpec((TOP_K, tm), col), pl.BlockSpec((TOP_K, tm), col)],
        out_shape=[jax.ShapeDtypeStruct((TOP_K, T), jnp.int32), jax.ShapeDtypeStruct((TOP_K, T), jnp.float32)],
        compiler_params=_cparams(("arbitrary",)),
        name="route_compact",
    )(sel_t, w_t, pos_t, pad_start, lower)


def _silu(g):
    return g / (1.0 + jnp.exp(-g))


def _expert_kernel(be_ref, nv_ref, nu_ref, xs_ref, wg_ref, wu_ref, wd_ref, ys_ref, wgb_ref, wub_ref, wdb_ref):
    i = pl.program_id(0)

    @pl.when((i == 0) | (be_ref[i] != be_ref[jnp.maximum(i - 1, 0)]))
    def _():
        wgb_ref[...] = wg_ref[0].astype(MXU_DTYPE)
        wub_ref[...] = wu_ref[0].astype(MXU_DTYPE)
        wdb_ref[...] = wd_ref[0].astype(MXU_DTYPE)

    @pl.when(i < nu_ref[0])
    def _():
        live = lax.broadcasted_iota(jnp.int32, (ROW_BLOCK, 1), 0) < nv_ref[i]
        parts = [jnp.where(live, v, jnp.zeros_like(v)) for v in _unpack_rows(xs_ref[...])]
        dk = wgb_ref.shape[0] // len(parts)

        def proj(w_ref):
            acc = _dot(parts[0], w_ref[0:dk, :])
            for n in range(1, len(parts)):
                acc = acc + _dot(parts[n], w_ref[n * dk:(n + 1) * dk, :])
            return acc

        a = (_silu(proj(wgb_ref)) * proj(wub_ref)).astype(MXU_DTYPE)
        ys_ref[...] = _pack_rows(_dot(a, wdb_ref[...]))


def _experts(xs, block_e, block_valid, n_used, w_gate, w_up, w_down):
    n_rows, W = xs.shape
    D = w_gate.shape[1]
    n_blocks = n_rows // ROW_BLOCK
    blk = lambda i, be, nv, nu: (jnp.minimum(i, nu[0] - 1), 0)
    wsel = lambda i, be, nv, nu: (be[i], 0, 0)
    return pl.pallas_call(
        _expert_kernel,
        grid_spec=pltpu.PrefetchScalarGridSpec(
            num_scalar_prefetch=3,
            grid=(n_blocks,),
            in_specs=[
                pl.BlockSpec((ROW_BLOCK, W), blk),
                pl.BlockSpec((1, D, D_EXPERT), wsel),
                pl.BlockSpec((1, D, D_EXPERT), wsel),
                pl.BlockSpec((1, D_EXPERT, D), wsel),
            ],
            out_specs=pl.BlockSpec((ROW_BLOCK, W), blk),
            scratch_shapes=[pltpu.VMEM((D, D_EXPERT), MXU_DTYPE), pltpu.VMEM((D, D_EXPERT), MXU_DTYPE),
                            pltpu.VMEM((D_EXPERT, D), MXU_DTYPE)],
        ),
        out_shape=jax.ShapeDtypeStruct((n_rows, W), xs.dtype),
        compiler_params=_cparams(("arbitrary",)),
        name="experts",
    )(block_e, block_valid, n_used, xs, w_gate, w_up, w_down)


SC_CORES = 2
SC_SUBCORES = 16
SC_GATHER_ROWS = 64
COMBINE_CHUNKS = 8


def _sc_gather_rows(table, idx):
    n = idx.shape[0]
    w = table.shape[1]
    n_workers = SC_CORES * SC_SUBCORES
    per_worker = n // n_workers
    assert n % n_workers == 0 and per_worker % SC_GATHER_ROWS == 0
    mesh = plsc.VectorSubcoreMesh(core_axis_name="c", subcore_axis_name="s")

    @functools.partial(
        pl.kernel, mesh=mesh,
        out_type=jax.ShapeDtypeStruct((n, w), table.dtype),
        scratch_types=[
            pltpu.VMEM((2, SC_GATHER_ROWS), jnp.int32),
            pltpu.VMEM((2, SC_GATHER_ROWS, w), table.dtype),
            pltpu.SemaphoreType.DMA((2,)),
        ],
        name="sc_gather_rows",
    )
    def gather(table_hbm, idx_hbm, out_hbm, idx_v, rows_v, sem):
        wid = lax.axis_index("s") * SC_CORES + lax.axis_index("c")
        base = wid * per_worker
        n_steps = per_worker // SC_GATHER_ROWS

        def gather_copy(slot):
            return pltpu.make_async_copy(table_hbm.at[idx_v.at[slot]], rows_v.at[slot], sem.at[slot])

        def start(step, slot):
            pltpu.sync_copy(idx_hbm.at[pl.ds(base + step * SC_GATHER_ROWS, SC_GATHER_ROWS)], idx_v.at[slot])
            gather_copy(slot).start()

        start(0, 0)

        @pl.loop(0, n_steps, step=2)
        def _(g):
            for slot in range(2):
                step = g + slot

                @pl.when(step + 1 < n_steps)
                def _():
                    start(step + 1, 1 - slot)

                gather_copy(slot).wait()
                pltpu.sync_copy(rows_v.at[slot], out_hbm.at[pl.ds(base + step * SC_GATHER_ROWS, SC_GATHER_ROWS)])

    return gather(table, idx)


SC_SCATTER_ROWS = 64


def _sc_scatter_rows(rows, idx3, n_out):
    n_src, w = rows.shape
    n_chunks, n_dst, batch = idx3.shape
    n_workers = SC_CORES * SC_SUBCORES
    assert batch == SC_SCATTER_ROWS and n_chunks * batch == n_src and n_chunks % (2 * n_workers) == 0
    per_worker = n_chunks // n_workers
    mesh = plsc.VectorSubcoreMesh(core_axis_name="c", subcore_axis_name="s")

    @functools.partial(
        pl.kernel, mesh=mesh,
        out_type=jax.ShapeDtypeStruct((n_out, w), rows.dtype),
        scratch_types=[
            pltpu.VMEM((2, n_dst, batch), jnp.int32),
            pltpu.VMEM((2, batch, w), rows.dtype),
            pltpu.SemaphoreType.DMA((2,)),
            pltpu.SemaphoreType.DMA,
        ],
        name="sc_scatter_rows",
    )
    def scatter(rows_hbm, idx_hbm, out_hbm, idx_v, rows_v, load_sem, store_sem):
        wid = lax.axis_index("s") * SC_CORES + lax.axis_index("c")

        def load_copy(step, slot):
            c = wid * per_worker + step
            return pltpu.make_async_copy(rows_hbm.at[pl.ds(c * batch, batch)], rows_v.at[slot], load_sem.at[slot])

        def load(step, slot):
            pltpu.sync_copy(idx_hbm.at[wid * per_worker + step], idx_v.at[slot])
            load_copy(step, slot).start()

        def store_copy(slot, k):
            return pltpu.make_async_copy(rows_v.at[slot], out_hbm.at[idx_v.at[slot].at[k]], store_sem)

        load(0, 0)

        @pl.loop(0, per_worker, step=2)
        def _(g):
            for slot in range(2):
                step = g + slot

                @pl.when(step + 1 < per_worker)
                def _():
                    load(step + 1, 1 - slot)

                load_copy(step, slot).wait()
                for k in range(n_dst):
                    store_copy(slot, k).start()
                for k in range(n_dst):
                    store_copy(slot, k).wait()

    return scatter(rows, idx3)


def _combine2_kernel(wk_ref, x1_ref, g_ref_rows, wsg_ref, wsu_ref, wsd_ref, g_ref, b_ref, o_ref):
    x1 = x1_ref[...]
    xb = x1.astype(MXU_DTYPE)
    a = (_silu(_dot(xb, wsg_ref[...])) * _dot(xb, wsu_ref[...])).astype(MXU_DTYPE)
    shared = _dot(a, wsd_ref[...])
    wk = wk_ref[...].T
    groups = [wk[:, 0:1] * v for v in _unpack_rows_f32(g_ref_rows[0])]
    for k in range(1, TOP_K):
        groups = [g + wk[:, k:k + 1] * v for g, v in zip(groups, _unpack_rows_f32(g_ref_rows[k]))]
    routed = jnp.concatenate(groups, axis=1)
    o_ref[...] = _layer_norm(ALPHA * x1 + (routed + shared), g_ref[...], b_ref[...])


def _combine2_kernel_into(wk_ref, x1_ref, g_ref_rows, wsg_ref, wsu_ref, wsd_ref, g_ref, b_ref, prev_ref, o_ref):
    del prev_ref
    _combine2_kernel(wk_ref, x1_ref, g_ref_rows, wsg_ref, wsu_ref, wsd_ref, g_ref, b_ref, o_ref)


def _combine2(wk_t, x1, gathered, w_sg, w_su, w_sd, ln_g, ln_b, tc, chunk, prev):
    T, D = x1.shape
    _, t_chunk, W = gathered.shape
    base = chunk * (t_chunk // tc)
    row = lambda i: (base + i, 0)
    c2 = lambda i: (0, 0)
    in_specs = [
        pl.BlockSpec((TOP_K, tc), lambda i: (0, base + i)),
        pl.BlockSpec((tc, D), row),
        pl.BlockSpec((TOP_K, tc, W), lambda i: (0, i, 0)),
        pl.BlockSpec(w_sg.shape, c2),
        pl.BlockSpec(w_su.shape, c2),
        pl.BlockSpec(w_sd.shape, c2),
        pl.BlockSpec((1, D), c2),
        pl.BlockSpec((1, D), c2),
    ]
    args = [wk_t, x1, gathered, w_sg, w_su, w_sd, ln_g, ln_b]
    if prev is None:
        body, aliases = _combine2_kernel, {}
    else:
        body, aliases = _combine2_kernel_into, {len(args): 0}
        in_specs.append(pl.BlockSpec(memory_space=pl.ANY))
        args.append(prev)
    return pl.pallas_call(
        body,
        grid=(t_chunk // tc,),
        in_specs=in_specs,
        out_specs=pl.BlockSpec((tc, D), row),
        out_shape=jax.ShapeDtypeStruct((T, D), jnp.float32),
        input_output_aliases=aliases,
        compiler_params=_cparams(("arbitrary",)),
        name="combine",
    )(*args)


def _split_w_in(w_in):
    o_kv = Q_RANK
    o_ki = o_kv + KV_RANK
    o_iw = o_ki + IDX_DIM
    o_rest = o_iw + N_IDX_HEADS
    w_small = jnp.pad(w_in[:, o_ki:o_rest], ((0, 0), (0, LANES - IDX_DIM - N_IDX_HEADS)))
    return jnp.concatenate([w_in[:, :o_ki], w_in[:, o_rest:], w_small], axis=1).astype(MXU_DTYPE)


def _stages(x, mem, w_in, q_norm_g, kv_norm_g, w_uq, w_uk, w_uv, w_qidx, rel_bias, conv_w, w_mem_k, w_mem_v, w_out, ln1_g, ln1_b, w_router, router_bias, w_e_gate, w_e_up, w_e_down, w_s_gate, w_s_up, w_s_down, ln2_g, ln2_b):
    B, S, D = x.shape
    T = B * S
    bf = MXU_DTYPE
    assert w_in.shape[0] == DEPTH == 1, "single-layer stack"
    l = 0
    res = {}
    x2 = x.reshape(T, D)
    cq, ckv, ckvt, kidx, iwt, yb, yc = _proj(
        x2, mem, _split_w_in(w_in[l]), q_norm_g[l].reshape(1, -1), kv_norm_g[l].reshape(1, -1), conv_w[l],
        w_mem_k[l].astype(bf), w_mem_v[l].astype(bf), B, S, tm=min(1024, S))
    res.update(c_q=cq, c_kv=ckv, k_idx=kidx, y_b=yb, y_c=yc,
               idx_w=jnp.swapaxes(iwt, 1, 2) / (N_IDX_HEADS ** -0.5 * IDX_DIM ** -0.5))
    bias_t = _bias_tiles(rel_bias)
    ya = _dsa(cq, iwt, kidx, ckv, ckvt,
              w_qidx[l].reshape(Q_RANK, -1).astype(bf), w_uq[l].reshape(Q_RANK, -1).astype(bf),
              jnp.transpose(w_uk[l], (1, 0, 2)).astype(bf), jnp.transpose(w_uv[l], (1, 2, 0)).astype(bf),
              bias_t, B, S)
    res.update(y_a=ya)

    x1, x1p, sel_t, w_t, pos_t, cnt = _mix_router(
        x2, ya, yb, yc, w_out[l].astype(bf), ln1_g[l].reshape(1, -1), ln1_b[l].reshape(1, -1),
        w_router[l].T, router_bias[l].reshape(-1, 1), tm=min(1024, T))
    res.update(x1=x1)

    counts = cnt[:, 0].astype(jnp.int32)
    padded = (counts + ROW_BLOCK - 1) // ROW_BLOCK * ROW_BLOCK
    pad_end = jnp.cumsum(padded)
    pad_start = pad_end - padded
    n_blocks = -(-(T * TOP_K) // ROW_BLOCK) + N_EXPERTS
    n_rows = n_blocks * ROW_BLOCK
    block_start = jnp.arange(n_blocks, dtype=jnp.int32) * ROW_BLOCK
    block_e = jnp.minimum(jnp.sum((pad_end[None, :] <= block_start[:, None]).astype(jnp.int32), axis=1),
                          N_EXPERTS - 1)
    n_used = (pad_end[-1:] // ROW_BLOCK).astype(jnp.int32)

    dest_t, wk_t = _compact(sel_t, w_t, pos_t, pad_start.astype(jnp.float32).reshape(-1, 1), tm=min(8192, T))
    block_valid = jnp.clip((pad_start + counts)[block_e] - block_start, 0, ROW_BLOCK).astype(jnp.int32)
    bt = SC_SCATTER_ROWS
    idx3 = jnp.transpose(dest_t.reshape(TOP_K, T // bt, bt), (1, 0, 2))
    xs = _sc_scatter_rows(x1p, idx3, n_rows)
    ys = _experts(xs, block_e, block_valid, n_used, w_e_gate[l], w_e_up[l], w_e_down[l])
    n_chunks = COMBINE_CHUNKS if T % (COMBINE_CHUNKS * 512) == 0 else 1
    t_chunk = T // n_chunks
    out = None
    for c in range(n_chunks):
        idx_c = dest_t[:, c * t_chunk:(c + 1) * t_chunk].reshape(-1)
        gathered = _sc_gather_rows(ys, idx_c).reshape(TOP_K, t_chunk, -1)
        out = _combine2(wk_t, x1, gathered, w_s_gate[l].astype(bf), w_s_up[l].astype(bf), w_s_down[l].astype(bf),
                        ln2_g[l].reshape(1, -1), ln2_b[l].reshape(1, -1), tc=min(512, t_chunk), chunk=c, prev=out)
    res.update(out=out.reshape(B, S, D))
    return res


def kernel(x, mem, w_in, q_norm_g, kv_norm_g, w_uq, w_uk, w_uv, w_qidx, rel_bias, conv_w, w_mem_k, w_mem_v, w_out, ln1_g, ln1_b, w_router, router_bias, w_e_gate, w_e_up, w_e_down, w_s_gate, w_s_up, w_s_down, ln2_g, ln2_b):
    return _stages(x, mem, w_in, q_norm_g, kv_norm_g, w_uq, w_uk, w_uv, w_qidx, rel_bias, conv_w, w_mem_k, w_mem_v, w_out, ln1_g, ln1_b, w_router, router_bias, w_e_gate, w_e_up, w_e_down, w_s_gate, w_s_up, w_s_down, ln2_g, ln2_b)["out"]
```

```python
import functools
import math

import jax
import jax.numpy as jnp
from jax import lax
from jax.experimental import pallas as pl
from jax.experimental.pallas import tpu as pltpu
from jax.experimental.pallas import tpu_sc as plsc

N_HEADS_A = 8
HEAD_DIM = 64
Q_RANK = 256
KV_RANK = 128
N_IDX_HEADS = 8
IDX_DIM = 64
TOPK_MAX = 256
REL_BUCKETS = 32
REL_MAX_DIST = 128
CONV_CH = 256
CONV_WIDTH = 3
N_MEM_HEADS = 4
MIX_A = N_HEADS_A * HEAD_DIM
MIX_C = N_MEM_HEADS * HEAD_DIM
N_EXPERTS = 64
N_GROUPS = 8
GROUP_SIZE = N_EXPERTS // N_GROUPS
TOPK_GROUPS = 4
TOP_K = 8
D_EXPERT = 256
ROUTED_SCALE = 2.5
DEPTH = 1
ALPHA = (2.0 * DEPTH) ** 0.25
LN_EPS = 1e-5
RMS_EPS = 1e-6
LOG2_E = math.log2(math.e)

LANES = 128
SUBLANES = 8
QB = 128
F32_LOWEST = -3.4028234663852886e38
VMEM_LIMIT = 56 * 1024 * 1024
MXU_DTYPE = jnp.bfloat16
ROW_BLOCK = 1024

_NT = (((1,), (1,)), ((), ()))


def _dot(a, b):
    return jnp.dot(a, b, preferred_element_type=jnp.float32)


def _dot_nt(a, b):
    return lax.dot_general(a, b, _NT, preferred_element_type=jnp.float32)


def _cparams(sem):
    return pltpu.CompilerParams(dimension_semantics=sem, vmem_limit_bytes=VMEM_LIMIT)


def _bias_kernel(rb_ref, o_ref):
    s = lax.broadcasted_iota(jnp.int32, (QB, QB), 0)
    t = lax.broadcasted_iota(jnp.int32, (QB, QB), 1)
    max_exact = REL_BUCKETS // 2
    for tile in range(3):
        n = jnp.maximum(t - s + (2 - tile) * QB, 0)
        nf = jnp.maximum(n.astype(jnp.float32), 1.0)
        large = max_exact + (jnp.log(nf / max_exact) / math.log(REL_MAX_DIST / max_exact)
                             * (REL_BUCKETS - max_exact)).astype(jnp.int32)
        large = jnp.minimum(large, REL_BUCKETS - 1)
        bucket = jnp.where(n < max_exact, n, large)
        for h in range(N_HEADS_A):
            acc = jnp.zeros((QB, QB), jnp.float32)
            for b in range(REL_BUCKETS):
                acc = jnp.where(bucket == b, rb_ref[b, h], acc)
            o_ref[tile, h] = acc * LOG2_E


def _bias_tiles(rel_bias):
    return pl.pallas_call(
        _bias_kernel,
        in_specs=[pl.BlockSpec(memory_space=pltpu.SMEM)],
        out_specs=pl.BlockSpec(memory_space=pltpu.VMEM),
        out_shape=jax.ShapeDtypeStruct((3, N_HEADS_A, QB, QB), jnp.float32),
        name="bias_tiles",
    )(rel_bias)


def _proj_kernel(x_ref, mem_ref, wm_ref, qg_ref, kvg_ref, cw_ref, wmk_ref, wmv_ref,
                 cq_ref, ckv_ref, ckvt_ref, kidx_ref, iwt_ref, yb_ref, yc_ref,
                 carry_ref, mk_ref, mv_ref, *, tm):
    si = pl.program_id(1)

    @pl.when(si == 0)
    def _():
        carry_ref[...] = jnp.zeros_like(carry_ref)
        mb = mem_ref[0].astype(MXU_DTYPE)
        mk_ref[...] = _dot(mb, wmk_ref[...]).astype(MXU_DTYPE)
        mv_ref[...] = _dot(mb, wmv_ref[...]).astype(MXU_DTYPE)

    xb = x_ref[...].astype(MXU_DTYPE)
    p = _dot(xb, wm_ref[...])
    small = p[:, p.shape[1] - LANES:]

    o = 0
    cq = p[:, o:o + Q_RANK]; o += Q_RANK
    ckv = p[:, o:o + KV_RANK]; o += KV_RANK
    g_b = p[:, o:o + CONV_CH]; o += CONV_CH
    g_c = p[:, o:o + CONV_CH]; o += CONV_CH
    h_c = p[:, o:o + CONV_CH]; o += CONV_CH
    q_mem = p[:, o:o + MIX_C]

    cq = cq * lax.rsqrt(jnp.mean(cq * cq, axis=-1, keepdims=True) + RMS_EPS) * qg_ref[...]
    ckv = ckv * lax.rsqrt(jnp.mean(ckv * ckv, axis=-1, keepdims=True) + RMS_EPS) * kvg_ref[...]
    cq_ref[...] = cq.astype(MXU_DTYPE)
    ckv_b = ckv.astype(MXU_DTYPE)
    ckv_ref[...] = ckv_b
    ckvt_ref[0] = ckv.T.astype(MXU_DTYPE)

    kidx_ref[...] = small[:, :IDX_DIM].astype(MXU_DTYPE)
    small_t = small.T
    iwt_ref[0] = small_t[IDX_DIM:IDX_DIM + N_IDX_HEADS, :] * (N_IDX_HEADS ** -0.5 * IDX_DIM ** -0.5)

    u = g_c * h_c
    rows = lax.broadcasted_iota(jnp.int32, (tm, 1), 0)
    c6 = carry_ref[SUBLANES - 2:SUBLANES - 1, :]
    c7 = carry_ref[SUBLANES - 1:SUBLANES, :]
    u1 = jnp.where(rows == 0, c7, pltpu.roll(u, 1, 0))
    u2 = jnp.where(rows == 0, c6, jnp.where(rows == 1, c7, pltpu.roll(u, 2, 0)))
    y = cw_ref[0:1, :] * u2
    y = y + cw_ref[1:2, :] * u1
    y = y + cw_ref[2:3, :] * u
    yb_ref[...] = (g_b * y).astype(MXU_DTYPE)
    carry_ref[...] = u[tm - SUBLANES:, :]

    qm = q_mem.astype(MXU_DTYPE)
    outs = []
    for h in range(N_MEM_HEADS):
        sl = slice(h * HEAD_DIM, (h + 1) * HEAD_DIM)
        lg = _dot_nt(qm[:, sl], mk_ref[:, sl]) * (HEAD_DIM ** -0.5)
        lg = lg - jnp.max(lg, axis=-1, keepdims=True)
        e = jnp.exp(lg)
        pr = e / jnp.sum(e, axis=-1, keepdims=True)
        outs.append(_dot(pr.astype(MXU_DTYPE), mv_ref[:, sl]))
    yc_ref[...] = jnp.concatenate(outs, axis=-1).astype(MXU_DTYPE)


def _proj(x2, mem, w_main, q_g, kv_g, conv_w, w_mk, w_mv, B, S, tm):
    T, D = x2.shape
    n_mem = mem.shape[1]
    ns = S // tm
    row = lambda b, s: (b * ns + s, 0)
    const2 = lambda b, s: (0, 0)
    bf = MXU_DTYPE
    return pl.pallas_call(
        functools.partial(_proj_kernel, tm=tm),
        grid=(B, ns),
        in_specs=[
            pl.BlockSpec((tm, D), row),
            pl.BlockSpec((1, n_mem, D), lambda b, s: (b, 0, 0)),
            pl.BlockSpec(w_main.shape, const2),
            pl.BlockSpec(q_g.shape, const2),
            pl.BlockSpec(kv_g.shape, const2),
            pl.BlockSpec(conv_w.shape, const2),
            pl.BlockSpec(w_mk.shape, const2),
            pl.BlockSpec(w_mv.shape, const2),
        ],
        out_specs=[
            pl.BlockSpec((tm, Q_RANK), row),
            pl.BlockSpec((tm, KV_RANK), row),
            pl.BlockSpec((1, KV_RANK, tm), lambda b, s: (b, 0, s)),
            pl.BlockSpec((tm, IDX_DIM), row),
            pl.BlockSpec((1, N_IDX_HEADS, tm), lambda b, s: (b, 0, s)),
            pl.BlockSpec((tm, CONV_CH), row),
            pl.BlockSpec((tm, MIX_C), row),
        ],
        out_shape=[
            jax.ShapeDtypeStruct((T, Q_RANK), bf),
            jax.ShapeDtypeStruct((T, KV_RANK), bf),
            jax.ShapeDtypeStruct((B, KV_RANK, S), bf),
            jax.ShapeDtypeStruct((T, IDX_DIM), bf),
            jax.ShapeDtypeStruct((B, N_IDX_HEADS, S), jnp.float32),
            jax.ShapeDtypeStruct((T, CONV_CH), bf),
            jax.ShapeDtypeStruct((T, MIX_C), bf),
        ],
        scratch_shapes=[
            pltpu.VMEM((SUBLANES, CONV_CH), jnp.float32),
            pltpu.VMEM((n_mem, MIX_C), bf),
            pltpu.VMEM((n_mem, MIX_C), bf),
        ],
        compiler_params=_cparams(("arbitrary", "arbitrary")),
        name="proj",
    )(x2, mem, w_main, q_g, kv_g, conv_w, w_mk, w_mv)


def _key_to_f32(key):
    bits = jnp.where(key < 0, key ^ jnp.int32(0x7FFFFFFF), key)
    return pltpu.bitcast(bits, jnp.float32)


def _colsum8(v):
    return jnp.sum(v.reshape(QB // SUBLANES, SUBLANES, QB), axis=0)


def _colmax8(v):
    return jnp.max(v.reshape(QB // SUBLANES, SUBLANES, QB), axis=0)


UNROLL_WIDTHS = (8, 4, 2, 1)


def _dsa_kernel(cq_ref, iwt_ref, kidx_ref, ckv_ref, ckvt_ref, wqi_ref, wuq_ref, wuk_ref, wuvt_ref,
                bias_ref, o_ref, wfold_ref, qidx_ref, qlat_ref, score_ref, logit_ref, acc_ref,
                *, k_sel, idx_bits):
    i = pl.program_id(1)
    f32 = jnp.float32
    bf = MXU_DTYPE
    n_blocks = i + 1
    n_blocks = n_blocks + jnp.where((n_blocks % 4 == 3) & (n_blocks < pl.num_programs(1)), 1, 0)
    s_loc = lax.broadcasted_iota(jnp.int32, (QB, QB), 0)
    t_glob = i * QB + lax.broadcasted_iota(jnp.int32, (QB, QB), 1)

    def blk(jb):
        return pl.multiple_of(jb * QB, QB)

    def block_loop(fn, init):
        c, start = init, 0
        for width in UNROLL_WIDTHS:
            n = (n_blocks - start) // width
            c = lax.fori_loop(0, n, lambda it, c, w=width, s=start: fn(s + it * w, w, c), c)
            start = start + n * width
        return c

    @pl.when(i == 0)
    def _():
        for h in range(N_HEADS_A):
            wfold_ref[:, h * KV_RANK:(h + 1) * KV_RANK] = (
                _dot_nt(wuq_ref[:, h * HEAD_DIM:(h + 1) * HEAD_DIM], wuk_ref[h])
                * (HEAD_DIM ** -0.5 * LOG2_E)).astype(bf)

    cq = cq_ref[...]
    q_idx = _dot(cq, wqi_ref[...]).astype(bf)
    q_lat = _dot(cq, wfold_ref[...]).astype(bf)
    for h in range(N_HEADS_A):
        qidx_ref[h * QB:(h + 1) * QB, :] = q_idx[:, h * IDX_DIM:(h + 1) * IDX_DIM]
        qlat_ref[h * QB:(h + 1) * QB, :] = q_lat[:, h * KV_RANK:(h + 1) * KV_RANK]
    iw = iwt_ref[0]

    def score_body(jb0, nb, n_pos8):
        d_blk = _dot_nt(kidx_ref[pl.ds(blk(jb0), nb * QB), :], qidx_ref[...])
        for sb in range(nb):
            off = blk(jb0 + sb)
            d_all = d_blk[sb * QB:(sb + 1) * QB, :]
            acc = jnp.maximum(d_all[:, 0:QB], 0.0) * iw[0:1, :]
            for h in range(1, N_IDX_HEADS):
                acc = acc + jnp.maximum(d_all[:, h * QB:(h + 1) * QB], 0.0) * iw[h:h + 1, :]
            sc = jnp.where(s_loc + off <= t_glob, acc + 0.0, F32_LOWEST)
            score_ref[pl.ds(off, QB), :] = sc
            n_pos8 = n_pos8 + _colsum8(jnp.where(sc >= 0.0, 1.0, 0.0))
        return n_pos8

    n_pos8 = block_loop(score_body, jnp.zeros((SUBLANES, QB), f32))

    def count_where(pred):
        def body(jb0, nb, acc):
            for sb in range(nb):
                off = blk(jb0 + sb)
                acc = acc + _colsum8(jnp.where(pred(score_ref[pl.ds(off, QB), :], off), 1.0, 0.0))
            return acc
        acc = block_loop(body, jnp.zeros((SUBLANES, QB), f32))
        return jnp.sum(acc, axis=0, keepdims=True)

    kf = float(k_sel)

    def search():
        c0 = jnp.sum(n_pos8, axis=0, keepdims=True)
        cand0 = jnp.where(c0 >= kf, jnp.int32(0), jnp.int32(-2 ** 31))
        n_ge0 = jnp.where(c0 >= kf, c0, -1.0)

        def bit_body(it, carry):
            cand, n_ge = carry
            trial = cand + lax.shift_left(jnp.int32(1), 30 - it)
            tf = _key_to_f32(trial)
            cnt = count_where(lambda sc, off: sc >= tf)
            take = cnt >= kf
            return jnp.where(take, trial, cand), jnp.where(take, cnt, n_ge)

        cand, n_ge = lax.fori_loop(0, 31, bit_body, (cand0, n_ge0))
        thr = _key_to_f32(cand)
        keep_all_ties = jnp.full((1, QB), 2 ** idx_bits - 1, jnp.int32)

        def resolve_ties():
            n_gt = count_where(lambda sc, off: sc > thr)
            n_eq = count_where(lambda sc, off: sc == thr)
            need = kf - n_gt

            def tie_search():
                def tbody(it, xcut):
                    trial = xcut + lax.shift_left(jnp.int32(1), idx_bits - 1 - it)
                    cnt = count_where(lambda sc, off: (sc == thr) & (s_loc + off < trial))
                    return jnp.where(cnt < need, trial, xcut)
                return lax.fori_loop(0, idx_bits, tbody, jnp.zeros((1, QB), jnp.int32))

            return lax.cond(jnp.max(n_eq - need) > 0.0, tie_search, lambda: keep_all_ties)

        plain = jnp.max(jnp.abs(n_ge - kf)) == 0.0
        xcut = lax.cond(plain, lambda: keep_all_ties, resolve_ties)
        return thr, xcut, plain

    def no_search():
        return (jnp.full((1, QB), F32_LOWEST, f32), jnp.full((1, QB), 2 ** idx_bits - 1, jnp.int32),
                jnp.zeros((), jnp.bool_))

    thr, xcut, plain = lax.cond((i + 1) * QB > k_sel, search, no_search)

    def general_mask(off):
        sc = score_ref[pl.ds(off, QB), :]
        s_glob = s_loc + off
        keep = ((sc > thr) | ((sc == thr) & (s_glob <= xcut))) & (s_glob <= t_glob)
        return jnp.where(keep, 0.0, -jnp.inf)

    def plain_mask(off):
        return jnp.where(score_ref[pl.ds(off, QB), :] >= thr, 0.0, -jnp.inf)

    acc_ref[...] = jnp.zeros_like(acc_ref)

    def att_body(selection_mask, jb0, nb, carry):
        rows = nb * QB
        all_far = jb0 + nb - 1 < i - 1
        return lax.cond(all_far,
                        lambda: att_step(selection_mask, True, jb0, nb, rows, carry),
                        lambda: att_step(selection_mask, False, jb0, nb, rows, carry))

    def att_step(selection_mask, far, jb0, nb, rows, carry):
        m, l8 = list(carry[0]), list(carry[1])
        lg_blk = _dot_nt(ckv_ref[pl.ds(blk(jb0), rows), :], qlat_ref[...])
        blk_max = [None] * N_HEADS_A
        for sb in range(nb):
            off = blk(jb0 + sb)
            msk = selection_mask(off)
            bsel = jnp.clip(jb0 + sb - i + 2, 0, 2)
            for h in range(N_HEADS_A):
                lgh = lg_blk[sb * QB:(sb + 1) * QB, h * QB:(h + 1) * QB] + msk
                if not far:
                    lgh = lgh + bias_ref[bsel, h]
                logit_ref[sb * QB:(sb + 1) * QB, h * QB:(h + 1) * QB] = lgh
                cm = _colmax8(lgh)
                blk_max[h] = cm if blk_max[h] is None else jnp.maximum(blk_max[h], cm)
        ps, scales = [], []
        for h in range(N_HEADS_A):
            far_bias = bias_ref[0, h, 0:1, :] if far else 0.0
            m_new = jnp.maximum(m[h], jnp.max(blk_max[h], axis=0, keepdims=True) + far_bias)
            m_ref = jnp.where(m_new == -jnp.inf, 0.0, m_new)
            p = jnp.exp2(logit_ref[0:rows, h * QB:(h + 1) * QB] - (m_ref - far_bias))
            scale = jnp.exp2(m[h] - m_ref)
            l8[h] = l8[h] * scale + jnp.sum(p.reshape(rows // SUBLANES, SUBLANES, QB), axis=0)
            m[h] = m_new
            ps.append(p.astype(bf))
            scales.append(scale)
        pv = _dot(ckvt_ref[0, :, pl.ds(blk(jb0), rows)], jnp.concatenate(ps, axis=1))
        for h in range(N_HEADS_A):
            hs = slice(h * QB, (h + 1) * QB)
            acc_ref[:, hs] = acc_ref[:, hs] * scales[h] + pv[:, hs]
        return tuple(m), tuple(l8)

    carry0 = (tuple(jnp.full((1, QB), -jnp.inf, f32) for _ in range(N_HEADS_A)),
              tuple(jnp.zeros((SUBLANES, QB), f32) for _ in range(N_HEADS_A)))
    _, l8 = lax.cond(plain,
                     lambda: block_loop(functools.partial(att_body, plain_mask), carry0),
                     lambda: block_loop(functools.partial(att_body, general_mask), carry0))

    outs = []
    for h in range(N_HEADS_A):
        l_row = jnp.sum(l8[h], axis=0, keepdims=True)
        o_lat_t = (acc_ref[:, h * QB:(h + 1) * QB] / l_row).astype(bf)
        outs.append(_dot(wuvt_ref[h], o_lat_t))
    o_ref[...] = jnp.concatenate(outs, axis=0).T.astype(o_ref.dtype)


def _dsa(cq, iwt, kidx, ckv, ckvt, w_qidx, w_uq, w_uk_h, w_uvt_h, bias_tiles, B, S):
    T = cq.shape[0]
    assert S % QB == 0 and QB >= REL_MAX_DIST
    nq = S // QB
    k_sel = min(TOPK_MAX, S // 4)
    idx_bits = max(1, (S - 1).bit_length())
    c2 = lambda b, i: (0, 0)
    c3 = lambda b, i: (0, 0, 0)
    return pl.pallas_call(
        functools.partial(_dsa_kernel, k_sel=k_sel, idx_bits=idx_bits),
        grid=(B, nq),
        in_specs=[
            pl.BlockSpec((QB, Q_RANK), lambda b, i: (b * nq + i, 0)),
            pl.BlockSpec((1, N_IDX_HEADS, QB), lambda b, i: (b, 0, i)),
            pl.BlockSpec((S, IDX_DIM), lambda b, i: (b, 0)),
            pl.BlockSpec((S, KV_RANK), lambda b, i: (b, 0)),
            pl.BlockSpec((1, KV_RANK, S), lambda b, i: (b, 0, 0)),
            pl.BlockSpec(w_qidx.shape, c2),
            pl.BlockSpec(w_uq.shape, c2),
            pl.BlockSpec(w_uk_h.shape, c3),
            pl.BlockSpec(w_uvt_h.shape, c3),
            pl.BlockSpec(bias_tiles.shape, lambda b, i: (0, 0, 0, 0)),
        ],
        out_specs=pl.BlockSpec((QB, MIX_A), lambda b, i: (b * nq + i, 0)),
        out_shape=jax.ShapeDtypeStruct((T, MIX_A), MXU_DTYPE),
        scratch_shapes=[
            pltpu.VMEM((Q_RANK, N_HEADS_A * KV_RANK), MXU_DTYPE),
            pltpu.VMEM((N_IDX_HEADS * QB, IDX_DIM), MXU_DTYPE),
            pltpu.VMEM((N_HEADS_A * QB, KV_RANK), MXU_DTYPE),
            pltpu.VMEM((S, QB), jnp.float32),
            pltpu.VMEM((max(UNROLL_WIDTHS) * QB, N_HEADS_A * QB), jnp.float32),
            pltpu.VMEM((KV_RANK, N_HEADS_A * QB), jnp.float32),
        ],
        compiler_params=_cparams(("arbitrary", "arbitrary")),
        name="dsa",
    )(cq, iwt, kidx, ckv, ckvt, w_qidx, w_uq, w_uk_h, w_uvt_h, bias_tiles)


def _layer_norm(xf, g, b):
    mu = jnp.mean(xf, axis=-1, keepdims=True)
    xc = xf - mu
    var = jnp.mean(xc * xc, axis=-1, keepdims=True)
    return xc * lax.rsqrt(var + LN_EPS) * g + b


def _rank_rows(v, n):
    ri = lax.broadcasted_iota(jnp.int32, v.shape, 0)
    rank = jnp.zeros(v.shape, jnp.float32)
    for r2 in range(n):
        row = v[r2:r2 + 1, :]
        beats = (row > v) | ((row == v) & (ri > r2))
        rank = rank + jnp.where(beats, 1.0, 0.0)
    return rank


def _top_rows(v, k):
    n = v.shape[0]
    ri = lax.broadcasted_iota(jnp.int32, v.shape, 0)
    sel = jnp.zeros(v.shape, jnp.float32)
    for _ in range(k):
        m = jnp.max(v, axis=0, keepdims=True)
        first = jnp.min(jnp.where(v == m, ri, n), axis=0, keepdims=True)
        pick = ri == first
        sel = jnp.where(pick, 1.0, sel)
        v = jnp.where(pick, -jnp.inf, v)
    return sel > 0.5


def _pack_factor():
    return 4 // jnp.dtype(MXU_DTYPE).itemsize


def _pack_rows(x):
    if _pack_factor() == 1:
        return pltpu.bitcast(x, jnp.int32)
    half = x.shape[1] // 2
    b = pltpu.bitcast(x.astype(MXU_DTYPE).astype(jnp.float32), jnp.int32)
    return b[:, half:] | lax.shift_right_logical(b[:, :half], jnp.int32(16))


_HIGH_HALF = -(1 << 16)


def _unpack_rows_f32(p):
    if _pack_factor() == 1:
        return [pltpu.bitcast(p, jnp.float32)]
    lo = pltpu.bitcast(lax.shift_left(p, jnp.int32(16)), jnp.float32)
    hi = pltpu.bitcast(p & jnp.int32(_HIGH_HALF), jnp.float32)
    return [lo, hi]


def _unpack_rows(p):
    return [v.astype(MXU_DTYPE) for v in _unpack_rows_f32(p)]


def _mix_router_kernel(x_ref, ya_ref, yb_ref, yc_ref, wo_ref, g_ref, b_ref, wrt_ref, rb_ref, exp_ref,
                       x1_ref, x1p_ref, sel_ref, w_ref, pos_ref, cnt_ref, base_ref, *, tm):
    step = pl.program_id(0)
    f32 = jnp.float32

    @pl.when(step == 0)
    def _():
        base_ref[...] = jnp.zeros_like(base_ref)

    mix = _dot(ya_ref[...], wo_ref[0:MIX_A, :])
    mix = mix + _dot(yb_ref[...], wo_ref[MIX_A:MIX_A + CONV_CH, :])
    mix = mix + _dot(yc_ref[...], wo_ref[MIX_A + CONV_CH:, :])
    x1 = _layer_norm(ALPHA * x_ref[...] + mix, g_ref[...], b_ref[...])
    x1_ref[...] = x1
    x1p_ref[...] = _pack_rows(x1)

    lg = lax.dot_general(wrt_ref[...], x1, _NT, precision=lax.Precision.HIGHEST, preferred_element_type=f32)
    s = 1.0 / (1.0 + jnp.exp(-lg))
    sc = s + rb_ref[...]

    g3 = sc.reshape(N_GROUPS, GROUP_SIZE, tm)
    m1 = jnp.max(g3, axis=1, keepdims=True)
    is_m1 = g3 == m1
    n_m1 = jnp.sum(jnp.where(is_m1, 1.0, 0.0), axis=1, keepdims=True)
    m2 = jnp.max(jnp.where(is_m1, -jnp.inf, g3), axis=1, keepdims=True)
    gscore = (m1 + jnp.where(n_m1 > 1.0, m1, m2)).reshape(N_GROUPS, tm)
    gsel = jnp.where(_rank_rows(gscore, N_GROUPS) < float(TOPK_GROUPS), 1.0, 0.0)
    emask = _dot(exp_ref[...], gsel.astype(MXU_DTYPE)) > 0.5
    masked = jnp.where(emask, sc, -jnp.inf)
    sel = _top_rows(masked, TOP_K) & emask
    self_ = jnp.where(sel, 1.0, 0.0)
    top_s = jnp.where(sel, s, 0.0)
    w = top_s / jnp.sum(top_s, axis=0, keepdims=True) * ROUTED_SCALE

    t_r = lax.broadcasted_iota(jnp.int32, (tm, tm), 0)
    t_c = lax.broadcasted_iota(jnp.int32, (tm, tm), 1)
    upper = jnp.where(t_r < t_c, 1.0, 0.0).astype(MXU_DTYPE)
    pref = _dot(self_.astype(MXU_DTYPE), upper)
    base = base_ref[...]
    sel_ref[...] = self_
    w_ref[...] = w
    pos_ref[...] = base + pref
    base = base + jnp.sum(self_, axis=1, keepdims=True)
    base_ref[...] = base
    cnt_ref[...] = jnp.broadcast_to(base, cnt_ref.shape)


def _mix_router(x2, ya, yb, yc, w_out, ln_g, ln_b, w_router_t, router_bias, tm):
    T, D = x2.shape
    E = N_EXPERTS
    expand = (jnp.arange(E)[:, None] // GROUP_SIZE == jnp.arange(N_GROUPS)[None, :]).astype(MXU_DTYPE)
    row = lambda i: (i, 0)
    col = lambda i: (0, i)
    c2 = lambda i: (0, 0)
    f32 = jnp.float32
    return pl.pallas_call(
        functools.partial(_mix_router_kernel, tm=tm),
        grid=(T // tm,),
        in_specs=[
            pl.BlockSpec((tm, D), row),
            pl.BlockSpec((tm, MIX_A), row),
            pl.BlockSpec((tm, CONV_CH), row),
            pl.BlockSpec((tm, MIX_C), row),
            pl.BlockSpec(w_out.shape, c2),
            pl.BlockSpec((1, D), c2),
            pl.BlockSpec((1, D), c2),
            pl.BlockSpec((E, D), c2),
            pl.BlockSpec((E, 1), c2),
            pl.BlockSpec((E, N_GROUPS), c2),
        ],
        out_specs=[
            pl.BlockSpec((tm, D), row),
            pl.BlockSpec((tm, D // _pack_factor()), row),
            pl.BlockSpec((E, tm), col),
            pl.BlockSpec((E, tm), col),
            pl.BlockSpec((E, tm), col),
            pl.BlockSpec((E, LANES), c2),
        ],
        out_shape=[
            jax.ShapeDtypeStruct((T, D), f32),
            jax.ShapeDtypeStruct((T, D // _pack_factor()), jnp.int32),
            jax.ShapeDtypeStruct((E, T), f32),
            jax.ShapeDtypeStruct((E, T), f32),
            jax.ShapeDtypeStruct((E, T), f32),
            jax.ShapeDtypeStruct((E, LANES), f32),
        ],
        scratch_shapes=[pltpu.VMEM((E, 1), f32)],
        compiler_params=_cparams(("arbitrary",)),
        name="mix_router",
    )(x2, ya, yb, yc, w_out, ln_g, ln_b, w_router_t, router_bias, expand)


def _compact_kernel(sel_ref, w_ref, pos_ref, pstart_ref, low_ref, dest_ref, wk_ref):
    sel = sel_ref[...]
    on = sel > 0.5
    rank = _dot(low_ref[...], sel.astype(MXU_DTYPE))
    row = pstart_ref[...] + pos_ref[...]
    w = w_ref[...]
    dests, ws = [], []
    for k in range(TOP_K):
        m = on & (rank == float(k))
        dests.append(jnp.sum(jnp.where(m, row, 0.0), axis=0, keepdims=True))
        ws.append(jnp.sum(jnp.where(m, w, 0.0), axis=0, keepdims=True))
    dest_ref[...] = jnp.concatenate(dests, axis=0).astype(jnp.int32)
    wk_ref[...] = jnp.concatenate(ws, axis=0)


def _compact(sel_t, w_t, pos_t, pad_start, tm):
    E, T = sel_t.shape
    lower = (jnp.arange(E)[None, :] < jnp.arange(E)[:, None]).astype(MXU_DTYPE)
    col = lambda i: (0, i)
    c2 = lambda i: (0, 0)
    return pl.pallas_call(
        _compact_kernel,
        grid=(T // tm,),
        in_specs=[pl.BlockSpec((E, tm), col), pl.BlockSpec((E, tm), col), pl.BlockSpec((E, tm), col),
                  pl.BlockSpec((E, 1), c2), pl.BlockSpec((E, E), c2)],
        out_specs=[pl.BlockSpec((TOP_K, tm), col), pl.BlockSpec((TOP_K, tm), col)],
        out_shape=[jax.ShapeDtypeStruct((TOP_K, T), jnp.int32), jax.ShapeDtypeStruct((TOP_K, T), jnp.float32)],
        compiler_params=_cparams(("arbitrary",)),
        name="route_compact",
    )(sel_t, w_t, pos_t, pad_start, lower)


def _silu(g):
    return g / (1.0 + jnp.exp(-g))


def _expert_kernel(be_ref, nv_ref, nu_ref, xs_ref, wg_ref, wu_ref, wd_ref, ys_ref, wgb_ref, wub_ref, wdb_ref):
    i = pl.program_id(0)

    @pl.when((i == 0) | (be_ref[i] != be_ref[jnp.maximum(i - 1, 0)]))
    def _():
        wgb_ref[...] = wg_ref[0].astype(MXU_DTYPE)
        wub_ref[...] = wu_ref[0].astype(MXU_DTYPE)
        wdb_ref[...] = wd_ref[0].astype(MXU_DTYPE)

    @pl.when(i < nu_ref[0])
    def _():
        live = lax.broadcasted_iota(jnp.int32, (ROW_BLOCK, 1), 0) < nv_ref[i]
        parts = [jnp.where(live, v, jnp.zeros_like(v)) for v in _unpack_rows(xs_ref[...])]
        dk = wgb_ref.shape[0] // len(parts)

        def proj(w_ref):
            acc = _dot(parts[0], w_ref[0:dk, :])
            for n in range(1, len(parts)):
                acc = acc + _dot(parts[n], w_ref[n * dk:(n + 1) * dk, :])
            return acc

        a = (_silu(proj(wgb_ref)) * proj(wub_ref)).astype(MXU_DTYPE)
        ys_ref[...] = _pack_rows(_dot(a, wdb_ref[...]))


def _experts(xs, block_e, block_valid, n_used, w_gate, w_up, w_down):
    n_rows, W = xs.shape
    D = w_gate.shape[1]
    n_blocks = n_rows // ROW_BLOCK
    blk = lambda i, be, nv, nu: (jnp.minimum(i, nu[0] - 1), 0)
    wsel = lambda i, be, nv, nu: (be[i], 0, 0)
    return pl.pallas_call(
        _expert_kernel,
        grid_spec=pltpu.PrefetchScalarGridSpec(
            num_scalar_prefetch=3,
            grid=(n_blocks,),
            in_specs=[
                pl.BlockSpec((ROW_BLOCK, W), blk),
                pl.BlockSpec((1, D, D_EXPERT), wsel),
                pl.BlockSpec((1, D, D_EXPERT), wsel),
                pl.BlockSpec((1, D_EXPERT, D), wsel),
            ],
            out_specs=pl.BlockSpec((ROW_BLOCK, W), blk),
            scratch_shapes=[pltpu.VMEM((D, D_EXPERT), MXU_DTYPE), pltpu.VMEM((D, D_EXPERT), MXU_DTYPE),
                            pltpu.VMEM((D_EXPERT, D), MXU_DTYPE)],
        ),
        out_shape=jax.ShapeDtypeStruct((n_rows, W), xs.dtype),
        compiler_params=_cparams(("arbitrary",)),
        name="experts",
    )(block_e, block_valid, n_used, xs, w_gate, w_up, w_down)


SC_CORES = 2
SC_SUBCORES = 16
SC_GATHER_ROWS = 64
COMBINE_CHUNKS = 16


def _sc_gather_rows(table, idx):
    n = idx.shape[0]
    w = table.shape[1]
    n_workers = SC_CORES * SC_SUBCORES
    per_worker = n // n_workers
    assert n % n_workers == 0 and per_worker % SC_GATHER_ROWS == 0
    mesh = plsc.VectorSubcoreMesh(core_axis_name="c", subcore_axis_name="s")

    @functools.partial(
        pl.kernel, mesh=mesh,
        out_type=jax.ShapeDtypeStruct((n, w), table.dtype),
        scratch_types=[
            pltpu.VMEM((2, SC_GATHER_ROWS), jnp.int32),
            pltpu.VMEM((2, SC_GATHER_ROWS, w), table.dtype),
            pltpu.SemaphoreType.DMA((2,)),
        ],
        name="sc_gather_rows",
    )
    def gather(table_hbm, idx_hbm, out_hbm, idx_v, rows_v, sem):
        wid = lax.axis_index("s") * SC_CORES + lax.axis_index("c")
        base = wid * per_worker
        n_steps = per_worker // SC_GATHER_ROWS

        def gather_copy(slot):
            return pltpu.make_async_copy(table_hbm.at[idx_v.at[slot]], rows_v.at[slot], sem.at[slot])

        def start(step, slot):
            pltpu.sync_copy(idx_hbm.at[pl.ds(base + step * SC_GATHER_ROWS, SC_GATHER_ROWS)], idx_v.at[slot])
            gather_copy(slot).start()

        start(0, 0)

        @pl.loop(0, n_steps, step=2)
        def _(g):
            for slot in range(2):
                step = g + slot

                @pl.when(step + 1 < n_steps)
                def _():
                    start(step + 1, 1 - slot)

                gather_copy(slot).wait()
                pltpu.sync_copy(rows_v.at[slot], out_hbm.at[pl.ds(base + step * SC_GATHER_ROWS, SC_GATHER_ROWS)])

    return gather(table, idx)


SC_SCATTER_ROWS = 64


def _sc_scatter_rows(rows, idx3, n_out):
    n_src, w = rows.shape
    n_chunks, n_dst, batch = idx3.shape
    n_workers = SC_CORES * SC_SUBCORES
    assert batch == SC_SCATTER_ROWS and n_chunks * batch == n_src and n_chunks % (2 * n_workers) == 0
    per_worker = n_chunks // n_workers
    mesh = plsc.VectorSubcoreMesh(core_axis_name="c", subcore_axis_name="s")

    @functools.partial(
        pl.kernel, mesh=mesh,
        out_type=jax.ShapeDtypeStruct((n_out, w), rows.dtype),
        scratch_types=[
            pltpu.VMEM((2, n_dst, batch), jnp.int32),
            pltpu.VMEM((2, batch, w), rows.dtype),
            pltpu.SemaphoreType.DMA((2,)),
            pltpu.SemaphoreType.DMA,
        ],
        name="sc_scatter_rows",
    )
    def scatter(rows_hbm, idx_hbm, out_hbm, idx_v, rows_v, load_sem, store_sem):
        wid = lax.axis_index("s") * SC_CORES + lax.axis_index("c")

        def load_copy(step, slot):
            c = wid * per_worker + step
            return pltpu.make_async_copy(rows_hbm.at[pl.ds(c * batch, batch)], rows_v.at[slot], load_sem.at[slot])

        def load(step, slot):
            pltpu.sync_copy(idx_hbm.at[wid * per_worker + step], idx_v.at[slot])
            load_copy(step, slot).start()

        def store_copy(slot, k):
            return pltpu.make_async_copy(rows_v.at[slot], out_hbm.at[idx_v.at[slot].at[k]], store_sem)

        load(0, 0)

        @pl.loop(0, per_worker, step=2)
        def _(g):
            for slot in range(2):
                step = g + slot

                @pl.when(step + 1 < per_worker)
                def _():
                    load(step + 1, 1 - slot)

                load_copy(step, slot).wait()
                for k in range(n_dst):
                    store_copy(slot, k).start()
                for k in range(n_dst):
                    store_copy(slot, k).wait()

    return scatter(rows, idx3)


def _combine2_kernel(wk_ref, x1_ref, g_ref_rows, wsg_ref, wsu_ref, wsd_ref, g_ref, b_ref, o_ref):
    x1 = x1_ref[...]
    xb = x1.astype(MXU_DTYPE)
    a = (_silu(_dot(xb, wsg_ref[...])) * _dot(xb, wsu_ref[...])).astype(MXU_DTYPE)
    shared = _dot(a, wsd_ref[...])
    wk = wk_ref[...].T
    groups = [wk[:, 0:1] * v for v in _unpack_rows_f32(g_ref_rows[0])]
    for k in range(1, TOP_K):
        groups = [g + wk[:, k:k + 1] * v for g, v in zip(groups, _unpack_rows_f32(g_ref_rows[k]))]
    routed = jnp.concatenate(groups, axis=1)
    o_ref[...] = _layer_norm(ALPHA * x1 + (routed + shared), g_ref[...], b_ref[...])


def _combine2_kernel_into(wk_ref, x1_ref, g_ref_rows, wsg_ref, wsu_ref, wsd_ref, g_ref, b_ref, prev_ref, o_ref):
    del prev_ref
    _combine2_kernel(wk_ref, x1_ref, g_ref_rows, wsg_ref, wsu_ref, wsd_ref, g_ref, b_ref, o_ref)


def _combine2(wk_t, x1, gathered, w_sg, w_su, w_sd, ln_g, ln_b, tc, chunk, prev):
    T, D = x1.shape
    _, t_chunk, W = gathered.shape
    base = chunk * (t_chunk // tc)
    row = lambda i: (base + i, 0)
    c2 = lambda i: (0, 0)
    in_specs = [
        pl.BlockSpec((TOP_K, tc), lambda i: (0, base + i)),
        pl.BlockSpec((tc, D), row),
        pl.BlockSpec((TOP_K, tc, W), lambda i: (0, i, 0)),
        pl.BlockSpec(w_sg.shape, c2),
        pl.BlockSpec(w_su.shape, c2),
        pl.BlockSpec(w_sd.shape, c2),
        pl.BlockSpec((1, D), c2),
        pl.BlockSpec((1, D), c2),
    ]
    args = [wk_t, x1, gathered, w_sg, w_su, w_sd, ln_g, ln_b]
    if prev is None:
        body, aliases = _combine2_kernel, {}
    else:
        body, aliases = _combine2_kernel_into, {len(args): 0}
        in_specs.append(pl.BlockSpec(memory_space=pl.ANY))
        args.append(prev)
    return pl.pallas_call(
        body,
        grid=(t_chunk // tc,),
        in_specs=in_specs,
        out_specs=pl.BlockSpec((tc, D), row),
        out_shape=jax.ShapeDtypeStruct((T, D), jnp.float32),
        input_output_aliases=aliases,
        compiler_params=_cparams(("arbitrary",)),
        name="combine",
    )(*args)


def _split_w_in(w_in):
    o_kv = Q_RANK
    o_ki = o_kv + KV_RANK
    o_iw = o_ki + IDX_DIM
    o_rest = o_iw + N_IDX_HEADS
    w_small = jnp.pad(w_in[:, o_ki:o_rest], ((0, 0), (0, LANES - IDX_DIM - N_IDX_HEADS)))
    return jnp.concatenate([w_in[:, :o_ki], w_in[:, o_rest:], w_small], axis=1).astype(MXU_DTYPE)


def _stages(x, mem, w_in, q_norm_g, kv_norm_g, w_uq, w_uk, w_uv, w_qidx, rel_bias, conv_w, w_mem_k, w_mem_v, w_out, ln1_g, ln1_b, w_router, router_bias, w_e_gate, w_e_up, w_e_down, w_s_gate, w_s_up, w_s_down, ln2_g, ln2_b):
    B, S, D = x.shape
    T = B * S
    bf = MXU_DTYPE
    assert w_in.shape[0] == DEPTH == 1, "single-layer stack"
    l = 0
    res = {}
    x2 = x.reshape(T, D)
    cq, ckv, ckvt, kidx, iwt, yb, yc = _proj(
        x2, mem, _split_w_in(w_in[l]), q_norm_g[l].reshape(1, -1), kv_norm_g[l].reshape(1, -1), conv_w[l],
        w_mem_k[l].astype(bf), w_mem_v[l].astype(bf), B, S, tm=min(1024, S))
    res.update(c_q=cq, c_kv=ckv, k_idx=kidx, y_b=yb, y_c=yc,
               idx_w=jnp.swapaxes(iwt, 1, 2) / (N_IDX_HEADS ** -0.5 * IDX_DIM ** -0.5))
    bias_t = _bias_tiles(rel_bias)
    ya = _dsa(cq, iwt, kidx, ckv, ckvt,
              w_qidx[l].reshape(Q_RANK, -1).astype(bf), w_uq[l].reshape(Q_RANK, -1).astype(bf),
              jnp.transpose(w_uk[l], (1, 0, 2)).astype(bf), jnp.transpose(w_uv[l], (1, 2, 0)).astype(bf),
              bias_t, B, S)
    res.update(y_a=ya)

    x1, x1p, sel_t, w_t, pos_t, cnt = _mix_router(
        x2, ya, yb, yc, w_out[l].astype(bf), ln1_g[l].reshape(1, -1), ln1_b[l].reshape(1, -1),
        w_router[l].T, router_bias[l].reshape(-1, 1), tm=min(1024, T))
    res.update(x1=x1)

    counts = cnt[:, 0].astype(jnp.int32)
    padded = (counts + ROW_BLOCK - 1) // ROW_BLOCK * ROW_BLOCK
    pad_end = jnp.cumsum(padded)
    pad_start = pad_end - padded
    n_blocks = -(-(T * TOP_K) // ROW_BLOCK) + N_EXPERTS
    n_rows = n_blocks * ROW_BLOCK
    block_start = jnp.arange(n_blocks, dtype=jnp.int32) * ROW_BLOCK
    block_e = jnp.minimum(jnp.sum((pad_end[None, :] <= block_start[:, None]).astype(jnp.int32), axis=1),
                          N_EXPERTS - 1)
    n_used = (pad_end[-1:] // ROW_BLOCK).astype(jnp.int32)

    dest_t, wk_t = _compact(sel_t, w_t, pos_t, pad_start.astype(jnp.float32).reshape(-1, 1), tm=min(8192, T))
    block_valid = jnp.clip((pad_start + counts)[block_e] - block_start, 0, ROW_BLOCK).astype(jnp.int32)
    bt = SC_SCATTER_ROWS
    idx3 = jnp.transpose(dest_t.reshape(TOP_K, T // bt, bt), (1, 0, 2))
    xs = _sc_scatter_rows(x1p, idx3, n_rows)
    ys = _experts(xs, block_e, block_valid, n_used, w_e_gate[l], w_e_up[l], w_e_down[l])
    n_chunks = COMBINE_CHUNKS if T % (COMBINE_CHUNKS * 512) == 0 else 1
    t_chunk = T // n_chunks
    out = None
    for c in range(n_chunks):
        idx_c = dest_t[:, c * t_chunk:(c + 1) * t_chunk].reshape(-1)
        gathered = _sc_gather_rows(ys, idx_c).reshape(TOP_K, t_chunk, -1)
        out = _combine2(wk_t, x1, gathered, w_s_gate[l].astype(bf), w_s_up[l].astype(bf), w_s_down[l].astype(bf),
                        ln2_g[l].reshape(1, -1), ln2_b[l].reshape(1, -1), tc=min(512, t_chunk), chunk=c, prev=out)
    res.update(out=out.reshape(B, S, D))
    return res


def kernel(x, mem, w_in, q_norm_g, kv_norm_g, w_uq, w_uk, w_uv, w_qidx, rel_bias, conv_w, w_mem_k, w_mem_v, w_out, ln1_g, ln1_b, w_router, router_bias, w_e_gate, w_e_up, w_e_down, w_s_gate, w_s_up, w_s_down, ln2_g, ln2_b):
    return _stages(x, mem, w_in, q_norm_g, kv_norm_g, w_uq, w_uk, w_uv, w_qidx, rel_bias, conv_w, w_mem_k, w_mem_v, w_out, ln1_g, ln1_b, w_router, router_bias, w_e_gate, w_e_up, w_e_down, w_s_gate, w_s_up, w_s_down, ln2_g, ln2_b)["out"]
```
